```python
import math
import jax, jax.numpy as jnp
from jax import lax
import numpy as np

D_MODEL = 1024
BATCH = 16
SEQ = 4096
DEPTH = 1

D_MIX = D_MODEL
D_CONV = D_MIX // 2
D_SSM = D_MIX - D_CONV
CONV_HEADS = 8
CONV_HEAD_DIM = D_CONV // CONV_HEADS
CONV_WIDTH = 31
SSM_GROUP = 16
SSM_GROUPS = D_SSM // SSM_GROUP
SSM_STATE = 64
D_FF = 128 * ((8 * D_MODEL // 3 + 127) // 128)
D_IN = 2 * D_CONV + D_SSM
FFN_RES = 0.5
EPS = 1e-6

kernel_name = "macaron_conv_s5_hybrid_layer"


def rmsnorm(x, g):
    xf = x.astype(jnp.float32)
    xf = xf * lax.rsqrt(jnp.mean(xf * xf, axis=-1, keepdims=True) + EPS)
    return (xf * g.astype(jnp.float32)).astype(x.dtype)


def layernorm(x, g, b):
    xf = x.astype(jnp.float32)
    mu = jnp.mean(xf, axis=-1, keepdims=True)
    xc = xf - mu
    var = jnp.mean(xc * xc, axis=-1, keepdims=True)
    y = xc * lax.rsqrt(var + EPS) * g.astype(jnp.float32) + b.astype(jnp.float32)
    return y.astype(x.dtype)


def swiglu(h, w1, w3, w2):
    return (jax.nn.silu(h @ w1) * (h @ w3)) @ w2


def conv_module(a_val, a_gate, conv_w, conv_b, ln_g, ln_b):
    a = a_val * jax.nn.sigmoid(a_gate)
    a = lax.conv_general_dilated(
        a, conv_w[:, None, :].astype(a.dtype),
        window_strides=(1,), padding=[(CONV_WIDTH - 1, 0)],
        dimension_numbers=("NWC", "WIO", "NWC"),
        feature_group_count=D_CONV) + conv_b
    a = layernorm(a, ln_g, ln_b)
    return jax.nn.silu(a)


def _complex_affine_combine(e1, e2):
    a1r, a1i, b1r, b1i = e1
    a2r, a2i, b2r, b2i = e2
    ar = a2r * a1r - a2i * a1i
    ai = a2r * a1i + a2i * a1r
    br = a2r * b1r - a2i * b1i + b2r
    bi = a2r * b1i + a2i * b1r + b2i
    return (ar, ai, br, bi)


def s5_layer(u, A_re, A_im, log_dt, B_re, B_im, C_re, C_im, D_skip, glu_w, glu_b):
    bsz, seq = u.shape[0], u.shape[1]
    ug = u.astype(jnp.float32).reshape(bsz, seq, SSM_GROUPS, SSM_GROUP)
    dt = jnp.exp(log_dt.astype(jnp.float32))[:, None]
    lr = A_re.astype(jnp.float32)
    li = A_im.astype(jnp.float32)
    zr, zi = lr * dt, li * dt
    mag = jnp.exp(zr)
    abar_r, abar_i = mag * jnp.cos(zi), mag * jnp.sin(zi)
    den = lr * lr + li * li
    nr = abar_r - 1.0
    coef_r = (nr * lr + abar_i * li) / den
    coef_i = (abar_i * lr - nr * li) / den
    br_, bi_ = B_re.astype(jnp.float32), B_im.astype(jnp.float32)
    bb_r = coef_r[..., None] * br_ - coef_i[..., None] * bi_
    bb_i = coef_r[..., None] * bi_ + coef_i[..., None] * br_
    bu_r = jnp.einsum("bsgh,gph->bsgp", ug, bb_r)
    bu_i = jnp.einsum("bsgh,gph->bsgp", ug, bb_i)
    ar_all = jnp.broadcast_to(abar_r, bu_r.shape)
    ai_all = jnp.broadcast_to(abar_i, bu_r.shape)
    _, _, xr, xi = lax.associative_scan(
        _complex_affine_combine, (ar_all, ai_all, bu_r, bu_i), axis=1)
    y = (jnp.einsum("bsgp,ghp->bsgh", xr, C_re.astype(jnp.float32))
         - jnp.einsum("bsgp,ghp->bsgh", xi, C_im.astype(jnp.float32)))
    y = y + D_skip.astype(jnp.float32).reshape(SSM_GROUPS, SSM_GROUP) * ug
    y = y.reshape(bsz, seq, D_SSM).astype(u.dtype)
    y = jax.nn.gelu(y)
    return y * jax.nn.sigmoid(y @ glu_w + glu_b)


def _fwd_setup_inputs(seed: int = 0) -> dict:
    key = jax.random.key(seed)
    ks = iter(jax.random.split(key, 48))
    L = DEPTH
    f32 = jnp.float32

    def nrm(shape, scale):
        return scale * jax.random.normal(next(ks), shape, f32)

    def gain(shape):
        return 1.0 + nrm(shape, 0.02)

    n_idx = jnp.arange(SSM_STATE, dtype=f32)
    A_re = -0.5 + nrm((L, SSM_GROUPS, SSM_STATE), 0.01)
    A_im = jnp.pi * n_idx[None, None, :] + nrm((L, SSM_GROUPS, SSM_STATE), 0.01)
    log_dt = jax.random.uniform(next(ks), (L, SSM_GROUPS), f32,
                                minval=math.log(1e-3), maxval=math.log(1e-1))
    b_scale = (SSM_GROUP ** -0.5) / math.sqrt(2.0)
    c_scale = (SSM_STATE ** -0.5) / math.sqrt(2.0)
    return {
        "x": nrm((BATCH, SEQ, D_MODEL), 1.0),
        "norm_ffn1": gain((L, D_MODEL)),
        "ffn1_w1": nrm((L, D_MODEL, D_FF), D_MODEL ** -0.5),
        "ffn1_w3": nrm((L, D_MODEL, D_FF), D_MODEL ** -0.5),
        "ffn1_w2": nrm((L, D_FF, D_MODEL), D_FF ** -0.5),
        "norm_mix": gain((L, D_MODEL)),
        "w_in": nrm((L, D_MODEL, D_IN), D_MODEL ** -0.5),
        "conv_w": nrm((L, CONV_WIDTH, D_CONV), CONV_WIDTH ** -0.5),
        "conv_b": nrm((L, D_CONV), 0.02),
        "conv_ln_g": gain((L, D_CONV)),
        "conv_ln_b": nrm((L, D_CONV), 0.02),
        "conv_out_g": gain((L, D_CONV)),
        "ssm_A_re": A_re,
        "ssm_A_im": A_im,
        "ssm_log_dt": log_dt,
        "ssm_B_re": nrm((L, SSM_GROUPS, SSM_STATE, SSM_GROUP), b_scale),
        "ssm_B_im": nrm((L, SSM_GROUPS, SSM_STATE, SSM_GROUP), b_scale),
        "ssm_C_re": nrm((L, SSM_GROUPS, SSM_GROUP, SSM_STATE), c_scale),
        "ssm_C_im": nrm((L, SSM_GROUPS, SSM_GROUP, SSM_STATE), c_scale),
        "ssm_D": 1.0 + nrm((L, D_SSM), 0.1),
        "ssm_glu_w": nrm((L, D_SSM, D_SSM), D_SSM ** -0.5),
        "ssm_glu_b": nrm((L, D_SSM), 0.02),
        "ssm_out_g": gain((L, D_SSM)),
        "w_out": nrm((L, D_MIX, D_MODEL), D_MIX ** -0.5),
        "norm_ffn2": gain((L, D_MODEL)),
        "ffn2_w1": nrm((L, D_MODEL, D_FF), D_MODEL ** -0.5),
        "ffn2_w3": nrm((L, D_MODEL, D_FF), D_MODEL ** -0.5),
        "ffn2_w2": nrm((L, D_FF, D_MODEL), D_FF ** -0.5),
        "norm_final": gain((D_MODEL,)),
    }


def _fwd_reference(x, norm_ffn1, ffn1_w1, ffn1_w3, ffn1_w2, norm_mix, w_in,
              conv_w, conv_b, conv_ln_g, conv_ln_b, conv_out_g,
              ssm_A_re, ssm_A_im, ssm_log_dt, ssm_B_re, ssm_B_im, ssm_C_re, ssm_C_im,
              ssm_D, ssm_glu_w, ssm_glu_b, ssm_out_g, w_out,
              norm_ffn2, ffn2_w1, ffn2_w3, ffn2_w2, norm_final):
    for l in range(DEPTH):
        x = x + FFN_RES * swiglu(rmsnorm(x, norm_ffn1[l]), ffn1_w1[l], ffn1_w3[l], ffn1_w2[l])

        h = rmsnorm(x, norm_mix[l])
        proj = h @ w_in[l]
        a_val = proj[..., :D_CONV]
        a_gate = proj[..., D_CONV:2 * D_CONV]
        u = proj[..., 2 * D_CONV:]

        a = conv_module(a_val, a_gate, conv_w[l], conv_b[l], conv_ln_g[l], conv_ln_b[l])
        a = rmsnorm(a, conv_out_g[l])

        s = s5_layer(u, ssm_A_re[l], ssm_A_im[l], ssm_log_dt[l], ssm_B_re[l], ssm_B_im[l],
                     ssm_C_re[l], ssm_C_im[l], ssm_D[l], ssm_glu_w[l], ssm_glu_b[l])
        s = rmsnorm(s, ssm_out_g[l])

        mixed = jnp.concatenate([a, s], axis=-1)
        x = x + mixed @ w_out[l]

        x = x + FFN_RES * swiglu(rmsnorm(x, norm_ffn2[l]), ffn2_w1[l], ffn2_w3[l], ffn2_w2[l])
    return rmsnorm(x, norm_final)


import jax as _jax
import jax.numpy as _jnp

TWIN_FORMAT = 'train_step'
FWD_PARAMS = ['x', 'norm_ffn1', 'ffn1_w1', 'ffn1_w3', 'ffn1_w2', 'norm_mix', 'w_in', 'conv_w', 'conv_b', 'conv_ln_g', 'conv_ln_b', 'conv_out_g', 'ssm_A_re', 'ssm_A_im', 'ssm_log_dt', 'ssm_B_re', 'ssm_B_im', 'ssm_C_re', 'ssm_C_im', 'ssm_D', 'ssm_glu_w', 'ssm_glu_b', 'ssm_out_g', 'w_out', 'norm_ffn2', 'ffn2_w1', 'ffn2_w3', 'ffn2_w2', 'norm_final']
TWIN_WEIGHTS = ['norm_ffn1', 'ffn1_w1', 'ffn1_w3', 'ffn1_w2', 'norm_mix', 'w_in', 'conv_w', 'conv_b', 'conv_ln_g', 'conv_ln_b', 'conv_out_g', 'ssm_A_re', 'ssm_A_im', 'ssm_log_dt', 'ssm_B_re', 'ssm_B_im', 'ssm_C_re', 'ssm_C_im', 'ssm_D', 'ssm_glu_w', 'ssm_glu_b', 'ssm_out_g', 'w_out', 'norm_ffn2', 'ffn2_w1', 'ffn2_w3', 'ffn2_w2', 'norm_final']
TWIN_DIFF_INPUT = 'x'
TWIN_INPUTS = ['x', 'norm_ffn1', 'ffn1_w1', 'ffn1_w3', 'ffn1_w2', 'norm_mix', 'w_in', 'conv_w', 'conv_b', 'conv_ln_g', 'conv_ln_b', 'conv_out_g', 'ssm_A_re', 'ssm_A_im', 'ssm_log_dt', 'ssm_B_re', 'ssm_B_im', 'ssm_C_re', 'ssm_C_im', 'ssm_D', 'ssm_glu_w', 'ssm_glu_b', 'ssm_out_g', 'w_out', 'norm_ffn2', 'ffn2_w1', 'ffn2_w3', 'ffn2_w2', 'norm_final', 'loss_target', 'm_norm_ffn1', 'm_ffn1_w1', 'm_ffn1_w3', 'm_ffn1_w2', 'm_norm_mix', 'm_w_in', 'm_conv_w', 'm_conv_b', 'm_conv_ln_g', 'm_conv_ln_b', 'm_conv_out_g', 'm_ssm_A_re', 'm_ssm_A_im', 'm_ssm_log_dt', 'm_ssm_B_re', 'm_ssm_B_im', 'm_ssm_C_re', 'm_ssm_C_im', 'm_ssm_D', 'm_ssm_glu_w', 'm_ssm_glu_b', 'm_ssm_out_g', 'm_w_out', 'm_norm_ffn2', 'm_ffn2_w1', 'm_ffn2_w3', 'm_ffn2_w2', 'm_norm_final', 'v_norm_ffn1', 'v_ffn1_w1', 'v_ffn1_w3', 'v_ffn1_w2', 'v_norm_mix', 'v_w_in', 'v_conv_w', 'v_conv_b', 'v_conv_ln_g', 'v_conv_ln_b', 'v_conv_out_g', 'v_ssm_A_re', 'v_ssm_A_im', 'v_ssm_log_dt', 'v_ssm_B_re', 'v_ssm_B_im', 'v_ssm_C_re', 'v_ssm_C_im', 'v_ssm_D', 'v_ssm_glu_w', 'v_ssm_glu_b', 'v_ssm_out_g', 'v_w_out', 'v_norm_ffn2', 'v_ffn2_w1', 'v_ffn2_w3', 'v_ffn2_w2', 'v_norm_final']
TWIN_OUTPUTS = ['loss', 'grad_x', 'grad_norm_ffn1', 'grad_ffn1_w1', 'grad_ffn1_w3', 'grad_ffn1_w2', 'grad_norm_mix', 'grad_w_in', 'grad_conv_w', 'grad_conv_b', 'grad_conv_ln_g', 'grad_conv_ln_b', 'grad_conv_out_g', 'grad_ssm_A_re', 'grad_ssm_A_im', 'grad_ssm_log_dt', 'grad_ssm_B_re', 'grad_ssm_B_im', 'grad_ssm_C_re', 'grad_ssm_C_im', 'grad_ssm_D', 'grad_ssm_glu_w', 'grad_ssm_glu_b', 'grad_ssm_out_g', 'grad_w_out', 'grad_norm_ffn2', 'grad_ffn2_w1', 'grad_ffn2_w3', 'grad_ffn2_w2', 'grad_norm_final', 'delta_norm_ffn1', 'delta_ffn1_w1', 'delta_ffn1_w3', 'delta_ffn1_w2', 'delta_norm_mix', 'delta_w_in', 'delta_conv_w', 'delta_conv_b', 'delta_conv_ln_g', 'delta_conv_ln_b', 'delta_conv_out_g', 'delta_ssm_A_re', 'delta_ssm_A_im', 'delta_ssm_log_dt', 'delta_ssm_B_re', 'delta_ssm_B_im', 'delta_ssm_C_re', 'delta_ssm_C_im', 'delta_ssm_D', 'delta_ssm_glu_w', 'delta_ssm_glu_b', 'delta_ssm_out_g', 'delta_w_out', 'delta_norm_ffn2', 'delta_ffn2_w1', 'delta_ffn2_w3', 'delta_ffn2_w2', 'delta_norm_final', 'new_m_norm_ffn1', 'new_m_ffn1_w1', 'new_m_ffn1_w3', 'new_m_ffn1_w2', 'new_m_norm_mix', 'new_m_w_in', 'new_m_conv_w', 'new_m_conv_b', 'new_m_conv_ln_g', 'new_m_conv_ln_b', 'new_m_conv_out_g', 'new_m_ssm_A_re', 'new_m_ssm_A_im', 'new_m_ssm_log_dt', 'new_m_ssm_B_re', 'new_m_ssm_B_im', 'new_m_ssm_C_re', 'new_m_ssm_C_im', 'new_m_ssm_D', 'new_m_ssm_glu_w', 'new_m_ssm_glu_b', 'new_m_ssm_out_g', 'new_m_w_out', 'new_m_norm_ffn2', 'new_m_ffn2_w1', 'new_m_ffn2_w3', 'new_m_ffn2_w2', 'new_m_norm_final', 'new_v_norm_ffn1', 'new_v_ffn1_w1', 'new_v_ffn1_w3', 'new_v_ffn1_w2', 'new_v_norm_mix', 'new_v_w_in', 'new_v_conv_w', 'new_v_conv_b', 'new_v_conv_ln_g', 'new_v_conv_ln_b', 'new_v_conv_out_g', 'new_v_ssm_A_re', 'new_v_ssm_A_im', 'new_v_ssm_log_dt', 'new_v_ssm_B_re', 'new_v_ssm_B_im', 'new_v_ssm_C_re', 'new_v_ssm_C_im', 'new_v_ssm_D', 'new_v_ssm_glu_w', 'new_v_ssm_glu_b', 'new_v_ssm_out_g', 'new_v_w_out', 'new_v_norm_ffn2', 'new_v_ffn2_w1', 'new_v_ffn2_w3', 'new_v_ffn2_w2', 'new_v_norm_final']
TWIN_LEAF_KINDS = {'loss': 'loss', 'grad_x': 'grad_x', 'grad_norm_ffn1': 'grad_w', 'grad_ffn1_w1': 'grad_w', 'grad_ffn1_w3': 'grad_w', 'grad_ffn1_w2': 'grad_w', 'grad_norm_mix': 'grad_w', 'grad_w_in': 'grad_w', 'grad_conv_w': 'grad_w', 'grad_conv_b': 'grad_w', 'grad_conv_ln_g': 'grad_w', 'grad_conv_ln_b': 'grad_w', 'grad_conv_out_g': 'grad_w', 'grad_ssm_A_re': 'grad_w', 'grad_ssm_A_im': 'grad_w', 'grad_ssm_log_dt': 'grad_w', 'grad_ssm_B_re': 'grad_w', 'grad_ssm_B_im': 'grad_w', 'grad_ssm_C_re': 'grad_w', 'grad_ssm_C_im': 'grad_w', 'grad_ssm_D': 'grad_w', 'grad_ssm_glu_w': 'grad_w', 'grad_ssm_glu_b': 'grad_w', 'grad_ssm_out_g': 'grad_w', 'grad_w_out': 'grad_w', 'grad_norm_ffn2': 'grad_w', 'grad_ffn2_w1': 'grad_w', 'grad_ffn2_w3': 'grad_w', 'grad_ffn2_w2': 'grad_w', 'grad_norm_final': 'grad_w', 'delta_norm_ffn1': 'delta_w', 'delta_ffn1_w1': 'delta_w', 'delta_ffn1_w3': 'delta_w', 'delta_ffn1_w2': 'delta_w', 'delta_norm_mix': 'delta_w', 'delta_w_in': 'delta_w', 'delta_conv_w': 'delta_w', 'delta_conv_b': 'delta_w', 'delta_conv_ln_g': 'delta_w', 'delta_conv_ln_b': 'delta_w', 'delta_conv_out_g': 'delta_w', 'delta_ssm_A_re': 'delta_w', 'delta_ssm_A_im': 'delta_w', 'delta_ssm_log_dt': 'delta_w', 'delta_ssm_B_re': 'delta_w', 'delta_ssm_B_im': 'delta_w', 'delta_ssm_C_re': 'delta_w', 'delta_ssm_C_im': 'delta_w', 'delta_ssm_D': 'delta_w', 'delta_ssm_glu_w': 'delta_w', 'delta_ssm_glu_b': 'delta_w', 'delta_ssm_out_g': 'delta_w', 'delta_w_out': 'delta_w', 'delta_norm_ffn2': 'delta_w', 'delta_ffn2_w1': 'delta_w', 'delta_ffn2_w3': 'delta_w', 'delta_ffn2_w2': 'delta_w', 'delta_norm_final': 'delta_w', 'new_m_norm_ffn1': 'new_m', 'new_m_ffn1_w1': 'new_m', 'new_m_ffn1_w3': 'new_m', 'new_m_ffn1_w2': 'new_m', 'new_m_norm_mix': 'new_m', 'new_m_w_in': 'new_m', 'new_m_conv_w': 'new_m', 'new_m_conv_b': 'new_m', 'new_m_conv_ln_g': 'new_m', 'new_m_conv_ln_b': 'new_m', 'new_m_conv_out_g': 'new_m', 'new_m_ssm_A_re': 'new_m', 'new_m_ssm_A_im': 'new_m', 'new_m_ssm_log_dt': 'new_m', 'new_m_ssm_B_re': 'new_m', 'new_m_ssm_B_im': 'new_m', 'new_m_ssm_C_re': 'new_m', 'new_m_ssm_C_im': 'new_m', 'new_m_ssm_D': 'new_m', 'new_m_ssm_glu_w': 'new_m', 'new_m_ssm_glu_b': 'new_m', 'new_m_ssm_out_g': 'new_m', 'new_m_w_out': 'new_m', 'new_m_norm_ffn2': 'new_m', 'new_m_ffn2_w1': 'new_m', 'new_m_ffn2_w3': 'new_m', 'new_m_ffn2_w2': 'new_m', 'new_m_norm_final': 'new_m', 'new_v_norm_ffn1': 'new_v', 'new_v_ffn1_w1': 'new_v', 'new_v_ffn1_w3': 'new_v', 'new_v_ffn1_w2': 'new_v', 'new_v_norm_mix': 'new_v', 'new_v_w_in': 'new_v', 'new_v_conv_w': 'new_v', 'new_v_conv_b': 'new_v', 'new_v_conv_ln_g': 'new_v', 'new_v_conv_ln_b': 'new_v', 'new_v_conv_out_g': 'new_v', 'new_v_ssm_A_re': 'new_v', 'new_v_ssm_A_im': 'new_v', 'new_v_ssm_log_dt': 'new_v', 'new_v_ssm_B_re': 'new_v', 'new_v_ssm_B_im': 'new_v', 'new_v_ssm_C_re': 'new_v', 'new_v_ssm_C_im': 'new_v', 'new_v_ssm_D': 'new_v', 'new_v_ssm_glu_w': 'new_v', 'new_v_ssm_glu_b': 'new_v', 'new_v_ssm_out_g': 'new_v', 'new_v_w_out': 'new_v', 'new_v_norm_ffn2': 'new_v', 'new_v_ffn2_w1': 'new_v', 'new_v_ffn2_w3': 'new_v', 'new_v_ffn2_w2': 'new_v', 'new_v_norm_final': 'new_v'}


def _forward(args):
    return _fwd_reference(*[args[k] for k in FWD_PARAMS])


def _output_shape():
    out = _jax.eval_shape(lambda: _forward(_fwd_setup_inputs(0)))
    return out.shape, out.dtype

N_MICROBATCH = 1
ADAM_LR = 0.001
ADAM_B1 = 0.9
ADAM_B2 = 0.999
ADAM_EPS = 1e-08
ADAM_WD = 0.01
ADAM_STEP = 10
PER_EXAMPLE_BATCH_AXIS = {'x': 0, 'loss_target': 0}
SHARED_INPUTS = []
_WEIGHT_DTYPES = {'norm_ffn1': _jnp.float32, 'ffn1_w1': _jnp.float32, 'ffn1_w3': _jnp.float32, 'ffn1_w2': _jnp.float32, 'norm_mix': _jnp.float32, 'w_in': _jnp.float32, 'conv_w': _jnp.float32, 'conv_b': _jnp.float32, 'conv_ln_g': _jnp.float32, 'conv_ln_b': _jnp.float32, 'conv_out_g': _jnp.float32, 'ssm_A_re': _jnp.float32, 'ssm_A_im': _jnp.float32, 'ssm_log_dt': _jnp.float32, 'ssm_B_re': _jnp.float32, 'ssm_B_im': _jnp.float32, 'ssm_C_re': _jnp.float32, 'ssm_C_im': _jnp.float32, 'ssm_D': _jnp.float32, 'ssm_glu_w': _jnp.float32, 'ssm_glu_b': _jnp.float32, 'ssm_out_g': _jnp.float32, 'w_out': _jnp.float32, 'norm_ffn2': _jnp.float32, 'ffn2_w1': _jnp.float32, 'ffn2_w3': _jnp.float32, 'ffn2_w2': _jnp.float32, 'norm_final': _jnp.float32}
MOMENT_SCALE = {'norm_ffn1': 1.081477e-01, 'ffn1_w1': 4.757948e-02, 'ffn1_w3': 4.620844e-02, 'ffn1_w2': 7.635220e-02, 'norm_mix': 2.061853e-01, 'w_in': 1.581852e-01, 'conv_w': 1.887481e-01, 'conv_b': 4.297885e-01, 'conv_ln_g': 2.694869e-01, 'conv_ln_b': 2.790459e-01, 'conv_out_g': 2.142947e-01, 'ssm_A_re': 1.137551e-02, 'ssm_A_im': 1.204671e-02, 'ssm_log_dt': 1.160448e+01, 'ssm_B_re': 6.849674e-03, 'ssm_B_im': 7.104989e-03, 'ssm_C_re': 1.357394e-02, 'ssm_C_im': 1.392035e-02, 'ssm_D': 2.252102e-01, 'ssm_glu_w': 5.053383e-02, 'ssm_glu_b': 7.726442e-02, 'ssm_out_g': 1.793964e-01, 'w_out': 1.922881e-01, 'norm_ffn2': 7.867474e-02, 'ffn2_w1': 3.168756e-02, 'ffn2_w3': 3.072497e-02, 'ffn2_w2': 5.124326e-02, 'norm_final': 6.409702e+01}


def _to_microbatches(a, axis):
    t = _jnp.moveaxis(a, axis, 0)
    t = t.reshape((N_MICROBATCH, t.shape[0] // N_MICROBATCH) + t.shape[1:])
    return _jnp.moveaxis(t, 1, axis + 1)


def setup_inputs(seed: int = 0) -> dict:
    inp = _fwd_setup_inputs(seed)
    key = _jax.random.fold_in(_jax.random.key(seed), 7919)
    shape, _ = _output_shape()
    out = dict(inp)
    out["loss_target"] = _jax.random.normal(_jax.random.fold_in(key, 0), shape, _jnp.float32)
    for i, name in enumerate(TWIN_WEIGHTS):
        w = inp[name].astype(_jnp.float32)
        if MOMENT_SCALE is None:
            s = _jnp.sqrt(_jnp.mean(_jnp.square(w)) + 1e-30)
        else:
            s = MOMENT_SCALE[name]
        km, kv = _jax.random.split(_jax.random.fold_in(key, i + 1))
        out[name] = w
        out["m_" + name] = s * _jax.random.normal(km, w.shape, _jnp.float32)
        out["v_" + name] = (s * s) * _jax.random.uniform(kv, w.shape, _jnp.float32, 0.5, 1.5)
    if N_MICROBATCH > 1:
        for name, axis in PER_EXAMPLE_BATCH_AXIS.items():
            out[name] = _to_microbatches(out[name], axis)
    return {'x': out['x'], 'norm_ffn1': out['norm_ffn1'], 'ffn1_w1': out['ffn1_w1'], 'ffn1_w3': out['ffn1_w3'], 'ffn1_w2': out['ffn1_w2'], 'norm_mix': out['norm_mix'], 'w_in': out['w_in'], 'conv_w': out['conv_w'], 'conv_b': out['conv_b'], 'conv_ln_g': out['conv_ln_g'], 'conv_ln_b': out['conv_ln_b'], 'conv_out_g': out['conv_out_g'], 'ssm_A_re': out['ssm_A_re'], 'ssm_A_im': out['ssm_A_im'], 'ssm_log_dt': out['ssm_log_dt'], 'ssm_B_re': out['ssm_B_re'], 'ssm_B_im': out['ssm_B_im'], 'ssm_C_re': out['ssm_C_re'], 'ssm_C_im': out['ssm_C_im'], 'ssm_D': out['ssm_D'], 'ssm_glu_w': out['ssm_glu_w'], 'ssm_glu_b': out['ssm_glu_b'], 'ssm_out_g': out['ssm_out_g'], 'w_out': out['w_out'], 'norm_ffn2': out['norm_ffn2'], 'ffn2_w1': out['ffn2_w1'], 'ffn2_w3': out['ffn2_w3'], 'ffn2_w2': out['ffn2_w2'], 'norm_final': out['norm_final'], 'loss_target': out['loss_target'], 'm_norm_ffn1': out['m_norm_ffn1'], 'm_ffn1_w1': out['m_ffn1_w1'], 'm_ffn1_w3': out['m_ffn1_w3'], 'm_ffn1_w2': out['m_ffn1_w2'], 'm_norm_mix': out['m_norm_mix'], 'm_w_in': out['m_w_in'], 'm_conv_w': out['m_conv_w'], 'm_conv_b': out['m_conv_b'], 'm_conv_ln_g': out['m_conv_ln_g'], 'm_conv_ln_b': out['m_conv_ln_b'], 'm_conv_out_g': out['m_conv_out_g'], 'm_ssm_A_re': out['m_ssm_A_re'], 'm_ssm_A_im': out['m_ssm_A_im'], 'm_ssm_log_dt': out['m_ssm_log_dt'], 'm_ssm_B_re': out['m_ssm_B_re'], 'm_ssm_B_im': out['m_ssm_B_im'], 'm_ssm_C_re': out['m_ssm_C_re'], 'm_ssm_C_im': out['m_ssm_C_im'], 'm_ssm_D': out['m_ssm_D'], 'm_ssm_glu_w': out['m_ssm_glu_w'], 'm_ssm_glu_b': out['m_ssm_glu_b'], 'm_ssm_out_g': out['m_ssm_out_g'], 'm_w_out': out['m_w_out'], 'm_norm_ffn2': out['m_norm_ffn2'], 'm_ffn2_w1': out['m_ffn2_w1'], 'm_ffn2_w3': out['m_ffn2_w3'], 'm_ffn2_w2': out['m_ffn2_w2'], 'm_norm_final': out['m_norm_final'], 'v_norm_ffn1': out['v_norm_ffn1'], 'v_ffn1_w1': out['v_ffn1_w1'], 'v_ffn1_w3': out['v_ffn1_w3'], 'v_ffn1_w2': out['v_ffn1_w2'], 'v_norm_mix': out['v_norm_mix'], 'v_w_in': out['v_w_in'], 'v_conv_w': out['v_conv_w'], 'v_conv_b': out['v_conv_b'], 'v_conv_ln_g': out['v_conv_ln_g'], 'v_conv_ln_b': out['v_conv_ln_b'], 'v_conv_out_g': out['v_conv_out_g'], 'v_ssm_A_re': out['v_ssm_A_re'], 'v_ssm_A_im': out['v_ssm_A_im'], 'v_ssm_log_dt': out['v_ssm_log_dt'], 'v_ssm_B_re': out['v_ssm_B_re'], 'v_ssm_B_im': out['v_ssm_B_im'], 'v_ssm_C_re': out['v_ssm_C_re'], 'v_ssm_C_im': out['v_ssm_C_im'], 'v_ssm_D': out['v_ssm_D'], 'v_ssm_glu_w': out['v_ssm_glu_w'], 'v_ssm_glu_b': out['v_ssm_glu_b'], 'v_ssm_out_g': out['v_ssm_out_g'], 'v_w_out': out['v_w_out'], 'v_norm_ffn2': out['v_norm_ffn2'], 'v_ffn2_w1': out['v_ffn2_w1'], 'v_ffn2_w3': out['v_ffn2_w3'], 'v_ffn2_w2': out['v_ffn2_w2'], 'v_norm_final': out['v_norm_final']}


def _loss(weights, diff, rest, loss_target):
    with _jax.named_scope("forward"):
        args = {**rest, TWIN_DIFF_INPUT: diff, **{k: w.astype(_WEIGHT_DTYPES[k]) for k, w in weights.items()}}
        y = _forward(args)
    with _jax.named_scope("loss_head"):
        err = _jnp.square(y.astype(_jnp.float32) - loss_target)
        return 0.5 * _jnp.sum(_jnp.mean(err, axis=-1)) if err.ndim else 0.5 * err


def _adamw(w, g, m, v):
    m = ADAM_B1 * m + (1.0 - ADAM_B1) * g
    v = ADAM_B2 * v + (1.0 - ADAM_B2) * _jnp.square(g)
    m_hat = m / (1.0 - ADAM_B1 ** ADAM_STEP)
    v_hat = v / (1.0 - ADAM_B2 ** ADAM_STEP)
    delta = -ADAM_LR * (m_hat / (_jnp.sqrt(v_hat) + ADAM_EPS) + ADAM_WD * w)
    return delta, m, v


def reference(x, norm_ffn1, ffn1_w1, ffn1_w3, ffn1_w2, norm_mix, w_in, conv_w, conv_b, conv_ln_g, conv_ln_b, conv_out_g, ssm_A_re, ssm_A_im, ssm_log_dt, ssm_B_re, ssm_B_im, ssm_C_re, ssm_C_im, ssm_D, ssm_glu_w, ssm_glu_b, ssm_out_g, w_out, norm_ffn2, ffn2_w1, ffn2_w3, ffn2_w2, norm_final, loss_target, m_norm_ffn1, m_ffn1_w1, m_ffn1_w3, m_ffn1_w2, m_norm_mix, m_w_in, m_conv_w, m_conv_b, m_conv_ln_g, m_conv_ln_b, m_conv_out_g, m_ssm_A_re, m_ssm_A_im, m_ssm_log_dt, m_ssm_B_re, m_ssm_B_im, m_ssm_C_re, m_ssm_C_im, m_ssm_D, m_ssm_glu_w, m_ssm_glu_b, m_ssm_out_g, m_w_out, m_norm_ffn2, m_ffn2_w1, m_ffn2_w3, m_ffn2_w2, m_norm_final, v_norm_ffn1, v_ffn1_w1, v_ffn1_w3, v_ffn1_w2, v_norm_mix, v_w_in, v_conv_w, v_conv_b, v_conv_ln_g, v_conv_ln_b, v_conv_out_g, v_ssm_A_re, v_ssm_A_im, v_ssm_log_dt, v_ssm_B_re, v_ssm_B_im, v_ssm_C_re, v_ssm_C_im, v_ssm_D, v_ssm_glu_w, v_ssm_glu_b, v_ssm_out_g, v_w_out, v_norm_ffn2, v_ffn2_w1, v_ffn2_w3, v_ffn2_w2, v_norm_final):
    given = dict(x=x, norm_ffn1=norm_ffn1, ffn1_w1=ffn1_w1, ffn1_w3=ffn1_w3, ffn1_w2=ffn1_w2, norm_mix=norm_mix, w_in=w_in, conv_w=conv_w, conv_b=conv_b, conv_ln_g=conv_ln_g, conv_ln_b=conv_ln_b, conv_out_g=conv_out_g, ssm_A_re=ssm_A_re, ssm_A_im=ssm_A_im, ssm_log_dt=ssm_log_dt, ssm_B_re=ssm_B_re, ssm_B_im=ssm_B_im, ssm_C_re=ssm_C_re, ssm_C_im=ssm_C_im, ssm_D=ssm_D, ssm_glu_w=ssm_glu_w, ssm_glu_b=ssm_glu_b, ssm_out_g=ssm_out_g, w_out=w_out, norm_ffn2=norm_ffn2, ffn2_w1=ffn2_w1, ffn2_w3=ffn2_w3, ffn2_w2=ffn2_w2, norm_final=norm_final, loss_target=loss_target, m_norm_ffn1=m_norm_ffn1, m_ffn1_w1=m_ffn1_w1, m_ffn1_w3=m_ffn1_w3, m_ffn1_w2=m_ffn1_w2, m_norm_mix=m_norm_mix, m_w_in=m_w_in, m_conv_w=m_conv_w, m_conv_b=m_conv_b, m_conv_ln_g=m_conv_ln_g, m_conv_ln_b=m_conv_ln_b, m_conv_out_g=m_conv_out_g, m_ssm_A_re=m_ssm_A_re, m_ssm_A_im=m_ssm_A_im, m_ssm_log_dt=m_ssm_log_dt, m_ssm_B_re=m_ssm_B_re, m_ssm_B_im=m_ssm_B_im, m_ssm_C_re=m_ssm_C_re, m_ssm_C_im=m_ssm_C_im, m_ssm_D=m_ssm_D, m_ssm_glu_w=m_ssm_glu_w, m_ssm_glu_b=m_ssm_glu_b, m_ssm_out_g=m_ssm_out_g, m_w_out=m_w_out, m_norm_ffn2=m_norm_ffn2, m_ffn2_w1=m_ffn2_w1, m_ffn2_w3=m_ffn2_w3, m_ffn2_w2=m_ffn2_w2, m_norm_final=m_norm_final, v_norm_ffn1=v_norm_ffn1, v_ffn1_w1=v_ffn1_w1, v_ffn1_w3=v_ffn1_w3, v_ffn1_w2=v_ffn1_w2, v_norm_mix=v_norm_mix, v_w_in=v_w_in, v_conv_w=v_conv_w, v_conv_b=v_conv_b, v_conv_ln_g=v_conv_ln_g, v_conv_ln_b=v_conv_ln_b, v_conv_out_g=v_conv_out_g, v_ssm_A_re=v_ssm_A_re, v_ssm_A_im=v_ssm_A_im, v_ssm_log_dt=v_ssm_log_dt, v_ssm_B_re=v_ssm_B_re, v_ssm_B_im=v_ssm_B_im, v_ssm_C_re=v_ssm_C_re, v_ssm_C_im=v_ssm_C_im, v_ssm_D=v_ssm_D, v_ssm_glu_w=v_ssm_glu_w, v_ssm_glu_b=v_ssm_glu_b, v_ssm_out_g=v_ssm_out_g, v_w_out=v_w_out, v_norm_ffn2=v_norm_ffn2, v_ffn2_w1=v_ffn2_w1, v_ffn2_w3=v_ffn2_w3, v_ffn2_w2=v_ffn2_w2, v_norm_final=v_norm_final)
    weights = {n: given[n] for n in TWIN_WEIGHTS}
    shared = {n: given[n] for n in SHARED_INPUTS}
    per_example = {n: given[n] for n in ['x']}
    grad_fn = _jax.value_and_grad(_loss, argnums=(0, 1))

    def one_microbatch(ex, loss_target):
        ex = dict(ex)
        diff = ex.pop(TWIN_DIFF_INPUT)
        return grad_fn(weights, diff, {**shared, **ex}, loss_target)

    if N_MICROBATCH == 1:
        loss, (grad_w, grad_x) = one_microbatch(per_example, given["loss_target"])
    else:
        def body(carry, xs):
            loss_sum, grad_sum = carry
            l_k, (gw_k, gx_k) = one_microbatch(xs[0], xs[1])
            with _jax.named_scope("update"):
                return (loss_sum + l_k, _jax.tree.map(_jnp.add, grad_sum, gw_k)), gx_k

        init = (_jnp.zeros((), _jnp.float32), _jax.tree.map(_jnp.zeros_like, weights))
        (loss, grad_w), grad_x = _jax.lax.scan(body, init, (per_example, given["loss_target"]))
    with _jax.named_scope("update"):
        delta_w, new_m, new_v = {}, {}, {}
        for n in TWIN_WEIGHTS:
            delta_w[n], new_m[n], new_v[n] = _adamw(weights[n], grad_w[n], given["m_" + n], given["v_" + n])
    return (loss, grad_x, *[grad_w[n] for n in TWIN_WEIGHTS], *[delta_w[n] for n in TWIN_WEIGHTS],
            *[new_m[n] for n in TWIN_WEIGHTS], *[new_v[n] for n in TWIN_WEIGHTS])
```

```python
import functools
import math

import jax
import jax.numpy as jnp
from jax import lax
from jax.experimental import pallas as pl
from jax.experimental.pallas import tpu as pltpu

F32 = jnp.float32
BF16 = jnp.bfloat16
EPS = 1e-6
ADAM_LR, ADAM_B1, ADAM_B2, ADAM_EPS, ADAM_WD, ADAM_STEP = 0.001, 0.9, 0.999, 1e-08, 0.01, 10
MESH = pl.DeviceIdType.MESH
ANY = pl.BlockSpec(memory_space=pl.ANY)
LANE = 128
SUBLANE = 8
VMEM_LIMIT_BYTES = 56 << 20
ROW_TILE = 256
ROW_TILE_ELEMS = 256 * 1024
WHOLE_ELEMS = 512 * 1024
CONV_TILE = 128
CONV_SUB = 32
HALO = 32
SCAN_TILE = 128
SCAN_COLS = 512
N_CHIPS = 4
CHIP_RELS = ((1, 0), (0, 1), (1, 1))
GELU_K = math.sqrt(2.0 / math.pi)
GELU_C = 0.044715

WEIGHTS = ['norm_ffn1', 'ffn1_w1', 'ffn1_w3', 'ffn1_w2', 'norm_mix', 'w_in', 'conv_w', 'conv_b', 'conv_ln_g', 'conv_ln_b',
           'conv_out_g', 'ssm_A_re', 'ssm_A_im', 'ssm_log_dt', 'ssm_B_re', 'ssm_B_im', 'ssm_C_re', 'ssm_C_im', 'ssm_D',
           'ssm_glu_w', 'ssm_glu_b', 'ssm_out_g', 'w_out', 'norm_ffn2', 'ffn2_w1', 'ffn2_w3', 'ffn2_w2', 'norm_final']
BIG = ['ffn1_w1', 'ffn1_w3', 'ffn1_w2', 'w_in', 'ssm_glu_w', 'w_out', 'ffn2_w1', 'ffn2_w3', 'ffn2_w2']
BIG_AXIS = {'ffn1_w1': 1, 'ffn1_w3': 1, 'ffn1_w2': 0, 'w_in': 1, 'ssm_glu_w': 0, 'w_out': 0, 'ffn2_w1': 1, 'ffn2_w3': 1,
            'ffn2_w2': 0}
SMALL = [n for n in WEIGHTS if n not in BIG]


def _round_up(n, m):
    return -(-n // m) * m


def _pick(n, cands):
    for c in cands:
        if c <= n and n % c == 0:
            return c
    return n


def _params(*sem):
    return pltpu.CompilerParams(dimension_semantics=sem, vmem_limit_bytes=VMEM_LIMIT_BYTES)


def _rms_r(x):
    return lax.rsqrt(jnp.mean(x * x, axis=-1, keepdims=True) + EPS)


def _rms_bwd(x, r, g, dy):
    dyg = dy * g
    return r * dyg - x * (r * r * r) * jnp.mean(x * dyg, axis=-1, keepdims=True)


def _sigmoid(x):
    return jax.nn.sigmoid(x)


def _dsilu(a, s):
    return s * (1.0 + a * (1.0 - s))


def _gelu(x):
    return 0.5 * x * (1.0 + jnp.tanh(GELU_K * (x + GELU_C * x * x * x)))


def _dgelu(x):
    t = jnp.tanh(GELU_K * (x + GELU_C * x * x * x))
    return 0.5 * (1.0 + t) + 0.5 * x * (1.0 - t * t) * GELU_K * (1.0 + 3.0 * GELU_C * x * x)


def _colsum(v):
    return jnp.sum(v, axis=0, keepdims=True)


def _rowwise(name, body, n_rows, row_ins, par_ins, row_outs, acc_outs):
    widest = max([w for (_, w, _) in row_ins] + [w for (w, _) in row_outs])
    tt = _pick(n_rows, [t for t in (256, 128, 64, 32, 16, 8) if t * widest <= ROW_TILE_ELEMS])
    in_specs = [pl.BlockSpec((tt, w), lambda i, cb=cb: (i, cb)) for (_, w, cb) in row_ins]
    in_specs += [pl.BlockSpec(p.shape, lambda i: (0, 0)) for p in par_ins]
    out_specs = [pl.BlockSpec((tt, w), lambda i: (i, 0)) for (w, _) in row_outs]
    out_specs += [pl.BlockSpec((r, w), lambda i: (0, 0)) for (r, w) in acc_outs]
    out_shape = [jax.ShapeDtypeStruct((n_rows, w), dt) for (w, dt) in row_outs]
    out_shape += [jax.ShapeDtypeStruct((r, w), F32) for (r, w) in acc_outs]
    n_in, n_ro = len(row_ins) + len(par_ins), len(row_outs)

    def kern(*refs):
        accs = refs[n_in + n_ro:]
        if accs:
            @pl.when(pl.program_id(0) == 0)
            def _():
                for a in accs:
                    a[...] = jnp.zeros_like(a)
        body(refs[:n_in], refs[n_in:n_in + n_ro], accs)

    return pl.pallas_call(kern, name=name, grid=(n_rows // tt,), in_specs=in_specs, out_specs=out_specs,
                          out_shape=out_shape, compiler_params=_params("arbitrary"))(*[a for a, _, _ in row_ins], *par_ins)


def _mm(name, a, b, ca, cb, out_dtype=F32, addend=None, alpha=1.0, a_cols=None):
    a_start, a_width = a_cols if a_cols else (0, a.shape[1])
    m, k = (a.shape[0], a_width) if ca == 1 else (a_width, a.shape[0])
    n = b.shape[1 - cb]
    assert b.shape[cb] == k, (name, a.shape, b.shape)
    tm = _pick(m, (512, 256, 128))
    tn = _pick(n, (1024, 768, 512, 384, 256, 128))
    tk = _pick(k, (1024, 768, 512, 256, 128))
    nk = k // tk
    if ca == 1:
        assert a_start % tk == 0
        a_spec = pl.BlockSpec((tm, tk), lambda i, j, kk: (i, kk + a_start // tk))
    else:
        assert a_start % tm == 0
        a_spec = pl.BlockSpec((tk, tm), lambda i, j, kk: (kk, i + a_start // tm))
    b_spec = pl.BlockSpec((tk, tn), lambda i, j, kk: (kk, j)) if cb == 0 else pl.BlockSpec((tn, tk), lambda i, j, kk: (j, kk))
    o_spec = pl.BlockSpec((tm, tn), lambda i, j, kk: (i, j))
    ins, in_specs = [a, b], [a_spec, b_spec]
    if addend is not None:
        ins.append(addend)
        in_specs.append(o_spec)
    dims = (((ca,), (cb,)), ((), ()))

    def kern(*refs):
        a_ref, b_ref = refs[0], refs[1]
        o_ref, acc_ref = refs[-2], refs[-1]
        kk = pl.program_id(2)

        @pl.when(kk == 0)
        def _():
            acc_ref[...] = jnp.zeros_like(acc_ref)

        acc_ref[...] += lax.dot_general(a_ref[...].astype(BF16), b_ref[...].astype(BF16), dims,
                                        preferred_element_type=F32)

        @pl.when(kk == nk - 1)
        def _():
            r = acc_ref[...]
            if alpha != 1.0:
                r = r * alpha
            if addend is not None:
                r = r + refs[2][...].astype(F32)
            o_ref[...] = r.astype(out_dtype)

    return pl.pallas_call(kern, name=name, grid=(m // tm, n // tn, nk), in_specs=in_specs, out_specs=o_spec,
                          out_shape=jax.ShapeDtypeStruct((m, n), out_dtype), scratch_shapes=[pltpu.VMEM((tm, tn), F32)],
                          compiler_params=_params("arbitrary", "arbitrary", "arbitrary"))(*ins)


def _rms_fwd(name, x, g):
    def body(ins, outs, accs):
        xv = ins[0][...]
        outs[0][...] = (xv * _rms_r(xv) * ins[1][...]).astype(BF16)

    return _rowwise(name, body, x.shape[0], [(x, x.shape[1], 0)], [g], [(x.shape[1], BF16)], [])[0]


def _rms_bwd_res(name, x, g, dh, dres):
    d = x.shape[1]

    def body(ins, outs, accs):
        xv, dhv, gv = ins[0][...], ins[1][...], ins[3][...]
        r = _rms_r(xv)
        outs[0][...] = ins[2][...] + _rms_bwd(xv, r, gv, dhv)
        accs[0][...] += _colsum(dhv * xv * r)

    return _rowwise(name, body, x.shape[0], [(x, d, 0), (dh, d, 0), (dres, d, 0)], [g], [(d, F32)], [(1, d)])


def _swiglu(name, a, b):
    w = a.shape[1]

    def body(ins, outs, accs):
        av, bv = ins[0][...].astype(F32), ins[1][...].astype(F32)
        outs[0][...] = (av * _sigmoid(av) * bv).astype(BF16)

    return _rowwise(name, body, a.shape[0], [(a, w, 0), (b, w, 0)], [], [(w, BF16)], [])[0]


def _swiglu_bwd(name, dz, a, b):
    w = a.shape[1]

    def body(ins, outs, accs):
        dzv, av, bv = ins[0][...].astype(F32), ins[1][...].astype(F32), ins[2][...].astype(F32)
        s = _sigmoid(av)
        outs[0][...] = (dzv * bv * _dsilu(av, s)).astype(BF16)
        outs[1][...] = (dzv * av * s).astype(BF16)

    return _rowwise(name, body, a.shape[0], [(dz, w, 0), (a, w, 0), (b, w, 0)], [], [(w, BF16), (w, BF16)], [])


def _loss_head(x3, gf, tgt):
    d = x3.shape[1]

    def body(ins, outs, accs):
        xv, tv, gv = ins[0][...], ins[1][...], ins[2][...]
        r = _rms_r(xv)
        e = xv * r * gv - tv
        sq = jnp.sum(jnp.sum(e * e, axis=-1, keepdims=True), axis=0, keepdims=True)
        accs[0][...] += jnp.broadcast_to(sq * (0.5 / d), (1, LANE))
        dy = e * (1.0 / d)
        outs[0][...] = _rms_bwd(xv, r, gv, dy)
        accs[1][...] += _colsum(dy * xv * r)

    return _rowwise("loss_head", body, x3.shape[0], [(x3, d, 0), (tgt, d, 0)], [gf], [(d, F32)], [(1, LANE), (1, d)])


def _ffn_fwd(tag, x, g, w1, w3, w2):
    h = _rms_fwd(tag + "_rms", x, g)
    a = _mm(tag + "_up1", h, w1, 1, 0, BF16)
    b = _mm(tag + "_up3", h, w3, 1, 0, BF16)
    z = _swiglu(tag + "_glu", a, b)
    return _mm(tag + "_down", z, w2, 1, 0, F32, addend=x, alpha=0.5), (h, a, b, z)


def _ffn_bwd(tag, x, g, w1, w3, w2, saved, dxo):
    h, a, b, z = saved
    dz = _mm(tag + "_dz", dxo, w2, 1, 1, BF16, alpha=0.5)
    dw2 = _mm(tag + "_dw2", z, dxo, 0, 0, F32, alpha=0.5)
    da, db = _swiglu_bwd(tag + "_dglu", dz, a, b)
    dw1 = _mm(tag + "_dw1", h, da, 0, 0)
    dw3 = _mm(tag + "_dw3", h, db, 0, 0)
    dh = _mm(tag + "_dh1", da, w1, 1, 1)
    dh = _mm(tag + "_dh3", db, w3, 1, 1, addend=dh)
    dx, dg = _rms_bwd_res(tag + "_drms", x, g, dh, dxo)
    return dx, dg, dw1, dw3, dw2


def _conv_fwd(proj, cw, cb, lng, lnb, og, seq):
    n_rows, c = proj.shape[0], cb.shape[1]
    kw = HALO - 1
    tt = _pick(seq, (CONV_TILE,))
    hb = tt // HALO

    def kern(v_ref, g_ref, vp_ref, gp_ref, w_ref, cb_ref, lg_ref, lb_ref, og_ref, c_ref, an_ref, ext_ref):
        first = (pl.program_id(0) * tt) % seq == 0
        ext_ref[pl.ds(HALO, tt), :] = v_ref[...] * _sigmoid(g_ref[...])
        ext_ref[pl.ds(0, HALO), :] = vp_ref[...] * _sigmoid(gp_ref[...]) * jnp.where(first, 0.0, 1.0)
        for r0 in range(0, tt, CONV_SUB):
            rows = min(CONV_SUB, tt - r0)
            acc = jnp.zeros((rows, c), F32)
            for k in range(kw):
                acc = acc + w_ref[pl.ds(k, 1), :] * ext_ref[pl.ds(r0 + HALO - (kw - 1) + k, rows), :]
            c_ref[pl.ds(r0, rows), :] = acc + cb_ref[...]
        cv = c_ref[...]
        mu = jnp.mean(cv, axis=-1, keepdims=True)
        xc = cv - mu
        rstd = lax.rsqrt(jnp.mean(xc * xc, axis=-1, keepdims=True) + EPS)
        lv = xc * rstd * lg_ref[...] + lb_ref[...]
        sl = lv * _sigmoid(lv)
        an_ref[...] = (sl * _rms_r(sl) * og_ref[...]).astype(BF16)

    cur = lambda cbk: pl.BlockSpec((tt, c), lambda i: (i, cbk))
    prev = lambda cbk: pl.BlockSpec((HALO, c), lambda i: (jnp.maximum(i * hb - 1, 0), cbk))
    par = lambda p: pl.BlockSpec(p.shape, lambda i: (0, 0))
    return pl.pallas_call(
        kern, name="conv_fwd", grid=(n_rows // tt,),
        in_specs=[cur(0), cur(1), prev(0), prev(1), par(cw), par(cb), par(lng), par(lnb), par(og)],
        out_specs=[pl.BlockSpec((tt, c), lambda i: (i, 0))] * 2,
        out_shape=[jax.ShapeDtypeStruct((n_rows, c), F32), jax.ShapeDtypeStruct((n_rows, c), BF16)],
        scratch_shapes=[pltpu.VMEM((tt + HALO, c), F32)], compiler_params=_params("arbitrary"),
    )(proj, proj, proj, proj, cw, cb, lng, lnb, og)


def _conv_bwd_rows(dmixed, cpre, lng, lnb, og):
    c = cpre.shape[1]

    def body(ins, outs, accs):
        dan, cv, lg, lb, ogv = ins[0][...], ins[1][...], ins[2][...], ins[3][...], ins[4][...]
        mu = jnp.mean(cv, axis=-1, keepdims=True)
        xc = cv - mu
        rstd = lax.rsqrt(jnp.mean(xc * xc, axis=-1, keepdims=True) + EPS)
        xh = xc * rstd
        lv = xh * lg + lb
        s = _sigmoid(lv)
        sl = lv * s
        r2 = _rms_r(sl)
        accs[0][...] += _colsum(dan * sl * r2)
        dl = _rms_bwd(sl, r2, ogv, dan) * _dsilu(lv, s)
        accs[1][...] += _colsum(dl * xh)
        accs[2][...] += _colsum(dl)
        dxh = dl * lg
        dc = rstd * (dxh - jnp.mean(dxh, axis=-1, keepdims=True) - xh * jnp.mean(dxh * xh, axis=-1, keepdims=True))
        outs[0][...] = dc
        accs[3][...] += _colsum(dc)

    return _rowwise("conv_bwd_rows", body, cpre.shape[0], [(dmixed, c, 0), (cpre, c, 0)], [lng, lnb, og], [(c, F32)],
                    [(1, c)] * 4)


def _conv_bwd_taps(proj, dc, cw, seq):
    n_rows, c = dc.shape
    kw = HALO - 1
    tt = _pick(seq, (CONV_TILE,))
    hb = tt // HALO
    last_blk = n_rows // HALO - 1

    def kern(v_ref, g_ref, vp_ref, gp_ref, dc_ref, dn_ref, w_ref, dv_ref, dg_ref, dw_ref, exta_ref, extd_ref):
        i = pl.program_id(0)
        first = (i * tt) % seq == 0
        last = ((i + 1) * tt) % seq == 0

        @pl.when(i == 0)
        def _():
            dw_ref[...] = jnp.zeros_like(dw_ref)

        sg = _sigmoid(g_ref[...])
        exta_ref[pl.ds(HALO, tt), :] = v_ref[...] * sg
        exta_ref[pl.ds(0, HALO), :] = vp_ref[...] * _sigmoid(gp_ref[...]) * jnp.where(first, 0.0, 1.0)
        dcv = dc_ref[...]
        extd_ref[pl.ds(0, tt), :] = dcv
        extd_ref[pl.ds(tt, HALO), :] = dn_ref[...] * jnp.where(last, 0.0, 1.0)
        for k in range(kw):
            dw_ref[pl.ds(k, 1), :] += _colsum(exta_ref[pl.ds(HALO - (kw - 1) + k, tt), :] * dcv)
        for r0 in range(0, tt, CONV_SUB):
            rows = min(CONV_SUB, tt - r0)
            acc = jnp.zeros((rows, c), F32)
            for k in range(kw):
                acc = acc + w_ref[pl.ds(k, 1), :] * extd_ref[pl.ds(r0 + (kw - 1) - k, rows), :]
            dv_ref[pl.ds(r0, rows), :] = acc
        da = dv_ref[...]
        dv_ref[...] = da * sg
        dg_ref[...] = da * v_ref[...] * sg * (1.0 - sg)

    cur = lambda cbk: pl.BlockSpec((tt, c), lambda i: (i, cbk))
    prev = lambda cbk: pl.BlockSpec((HALO, c), lambda i: (jnp.maximum(i * hb - 1, 0), cbk))
    nxt = pl.BlockSpec((HALO, c), lambda i: (jnp.minimum((i + 1) * hb, last_blk), 0))
    return pl.pallas_call(
        kern, name="conv_bwd_taps", grid=(n_rows // tt,),
        in_specs=[cur(0), cur(1), prev(0), prev(1), cur(0), nxt, pl.BlockSpec(cw.shape, lambda i: (0, 0))],
        out_specs=[cur(0), cur(0), pl.BlockSpec((HALO, c), lambda i: (0, 0))],
        out_shape=[jax.ShapeDtypeStruct((n_rows, c), F32), jax.ShapeDtypeStruct((n_rows, c), F32),
                   jax.ShapeDtypeStruct((HALO, c), F32)],
        scratch_shapes=[pltpu.VMEM((tt + HALO, c), F32), pltpu.VMEM((tt + HALO, c), F32)],
        compiler_params=_params("arbitrary"),
    )(proj, proj, proj, proj, dc, dc, cw)


def _s5_params_fwd(lr, li, ldt, btr, bti):
    ns = lr.shape[1]

    def kern(lr_ref, li_ref, ldt_ref, btr_ref, bti_ref, ar_ref, ai_ref, bbr_ref, bbi_ref, pw_ref):
        lrv, liv = lr_ref[...], li_ref[...]
        dt = jnp.exp(ldt_ref[...])
        zr, zi = lrv * dt, liv * dt
        mag = jnp.exp(zr)
        ar, ai = mag * jnp.cos(zi), mag * jnp.sin(zi)
        den = lrv * lrv + liv * liv
        nr = ar - 1.0
        cr = (nr * lrv + ai * liv) / den
        ci = (ai * lrv - nr * liv) / den
        ar_ref[...] = ar
        ai_ref[...] = ai
        bbr_ref[...] = cr * btr_ref[...] - ci * bti_ref[...]
        bbi_ref[...] = cr * bti_ref[...] + ci * btr_ref[...]
        pr, pi = ar, ai
        for e in range(SUBLANE):
            pw_ref[pl.ds(e, 1), pl.ds(0, ns)] = pr
            pw_ref[pl.ds(e, 1), pl.ds(ns, ns)] = pi
            pr, pi = pr * ar - pi * ai, pr * ai + pi * ar

    h = btr.shape[0]
    shapes = [jax.ShapeDtypeStruct((1, ns), F32)] * 2 + [jax.ShapeDtypeStruct((h, ns), F32)] * 2
    shapes += [jax.ShapeDtypeStruct((SUBLANE, 2 * ns), F32)]
    return pl.pallas_call(kern, name="s5_params_fwd", out_shape=shapes)(lr, li, ldt, btr, bti)


def _s5_params_bwd(lr, li, ldt, btr, bti, dar, dai, dbbr, dbbi):
    def kern(lr_ref, li_ref, ldt_ref, btr_ref, bti_ref, dar_ref, dai_ref, dbr_ref, dbi_ref,
             dlr_ref, dli_ref, dldt_ref, dbtr_ref, dbti_ref):
        lrv, liv = lr_ref[...], li_ref[...]
        dt = jnp.exp(ldt_ref[...])
        zr, zi = lrv * dt, liv * dt
        mag = jnp.exp(zr)
        ar, ai = mag * jnp.cos(zi), mag * jnp.sin(zi)
        den = lrv * lrv + liv * liv
        nr = ar - 1.0
        cr = (nr * lrv + ai * liv) / den
        ci = (ai * lrv - nr * liv) / den
        dbr, dbi, br, bi = dbr_ref[...], dbi_ref[...], btr_ref[...], bti_ref[...]
        dbtr_ref[...] = cr * dbr + ci * dbi
        dbti_ref[...] = cr * dbi - ci * dbr
        dcr = _colsum(br * dbr + bi * dbi)
        dci = _colsum(br * dbi - bi * dbr)
        ir, ii = lrv / den, -liv / den
        dnr = ir * dcr + ii * dci
        dni = ir * dci - ii * dcr
        wr, wi = cr * ir - ci * ii, cr * ii + ci * ir
        dl1r = -(wr * dcr + wi * dci)
        dl1i = -(wr * dci - wi * dcr)
        dtr, dti = dar_ref[...] + dnr, dai_ref[...] + dni
        dzr = ar * dtr + ai * dti
        dzi = ar * dti - ai * dtr
        dlr_ref[...] = dl1r + dt * dzr
        dli_ref[...] = dl1i + dt * dzi
        dldt_ref[...] = (dzr * lrv + dzi * liv) * dt

    ns, h = lr.shape[1], btr.shape[0]
    shapes = [jax.ShapeDtypeStruct((1, ns), F32)] * 3 + [jax.ShapeDtypeStruct((h, ns), F32)] * 2
    return pl.pallas_call(kern, name="s5_params_bwd", out_shape=shapes)(lr, li, ldt, btr, bti, dar, dai, dbbr, dbbi)


def _scan(name, src, xs, tabs, seq, reverse):
    n_rows, w = src.shape
    ns = w // 2
    tt = _pick(seq, (SCAN_TILE,))
    nt, ng = n_rows // tt, tt // SUBLANE
    cw = _pick(ns, (SCAN_COLS,))
    with_x = xs is not None
    carry_row = 0 if reverse else SUBLANE - 1

    def kern(*refs):
        if with_x:
            s_ref, x_ref, l1, l2, l4, pw, o_ref, da_ref, car_ref, acc_ref = refs
        else:
            s_ref, l1, l2, l4, pw, o_ref, car_ref = refs
        i = pl.program_id(0)
        ti = (nt - 1 - i) if reverse else i
        restart = (((ti + 1) * tt) % seq == 0) if reverse else ((ti * tt) % seq == 0)

        @pl.when(restart)
        def _():
            car_ref[...] = jnp.zeros_like(car_ref)

        if with_x:
            @pl.when(i == 0)
            def _():
                acc_ref[...] = jnp.zeros_like(acc_ref)

        row = lax.broadcasted_iota(jnp.int32, (SUBLANE, cw), 0)

        def group(gi, carry):
            g = (ng - 1 - gi) if reverse else gi
            rows = pl.ds(pl.multiple_of(g * SUBLANE, SUBLANE), SUBLANE)
            for c0 in range(0, ns, cw):
                cr, ci = pl.ds(c0, cw), pl.ds(ns + c0, cw)
                xr, xi = s_ref[rows, cr], s_ref[rows, ci]
                for s, lt in ((1, l1), (2, l2), (4, l4)):
                    sh = (SUBLANE - s) if reverse else s
                    sr, si = pltpu.roll(xr, sh, 0), pltpu.roll(xi, sh, 0)
                    ar, ai = lt[:, cr], lt[:, ci]
                    xr, xi = xr + ar * sr - ai * si, xi + ar * si + ai * sr
                kr, ki = car_ref[pl.ds(carry_row, 1), cr], car_ref[pl.ds(carry_row, 1), ci]
                pr, pi = pw[:, cr], pw[:, ci]
                xr, xi = xr + pr * kr - pi * ki, xi + pr * ki + pi * kr
                o_ref[rows, cr] = xr
                o_ref[rows, ci] = xi
                car_ref[:, cr] = xr
                car_ref[:, ci] = xi
                if with_x:
                    nr = jnp.where(row == SUBLANE - 1, kr, pltpu.roll(xr, SUBLANE - 1, 0))
                    ni = jnp.where(row == SUBLANE - 1, ki, pltpu.roll(xi, SUBLANE - 1, 0))
                    pxr, pxi = x_ref[rows, cr], x_ref[rows, ci]
                    acc_ref[:, cr] += nr * pxr + ni * pxi
                    acc_ref[:, ci] += ni * pxr - nr * pxi
            return carry

        lax.fori_loop(0, ng, group, 0)

        if with_x:
            @pl.when(i == nt - 1)
            def _():
                da_ref[...] = _colsum(acc_ref[...])

    tile = pl.BlockSpec((tt, w), (lambda i: (nt - 1 - i, 0)) if reverse else (lambda i: (i, 0)))
    tab = pl.BlockSpec((SUBLANE, w), lambda i: (0, 0))
    ins = [src] + ([xs] if with_x else []) + list(tabs)
    in_specs = [tile] * (2 if with_x else 1) + [tab] * 4
    out_specs, out_shape = [tile], [jax.ShapeDtypeStruct((n_rows, w), F32)]
    scratch = [pltpu.VMEM((SUBLANE, w), F32)]
    if with_x:
        out_specs.append(pl.BlockSpec((1, w), lambda i: (0, 0)))
        out_shape.append(jax.ShapeDtypeStruct((1, w), F32))
        scratch.append(pltpu.VMEM((SUBLANE, w), F32))
    return pl.pallas_call(kern, name=name, grid=(nt,), in_specs=in_specs, out_specs=out_specs, out_shape=out_shape,
                          scratch_shapes=scratch, compiler_params=_params("arbitrary"))(*ins)


def _s5_post1(y0, proj, dskip):
    c = y0.shape[1]

    def body(ins, outs, accs):
        ypre = ins[0][...] + ins[2][...] * ins[1][...]
        outs[0][...] = ypre
        outs[1][...] = _gelu(ypre).astype(BF16)

    return _rowwise("s5_post1", body, y0.shape[0], [(y0, c, 0), (proj, c, 2)], [dskip], [(c, F32), (c, BF16)], [])


def _s5_post2(yg, q0, bg, og):
    c = yg.shape[1]

    def body(ins, outs, accs):
        ygv = ins[0][...].astype(F32)
        sg = ygv * _sigmoid(ins[1][...] + ins[2][...])
        outs[0][...] = (sg * _rms_r(sg) * ins[3][...]).astype(BF16)

    return _rowwise("s5_post2", body, yg.shape[0], [(yg, c, 0), (q0, c, 0)], [bg, og], [(c, BF16)], [])[0]


def _s5_post2_bwd(dmixed, yg, q0, bg, og):
    c = yg.shape[1]

    def body(ins, outs, accs):
        dsn, ygv = ins[0][...], ins[1][...].astype(F32)
        s = _sigmoid(ins[2][...] + ins[3][...])
        sg = ygv * s
        r = _rms_r(sg)
        accs[0][...] += _colsum(dsn * sg * r)
        dsg = _rms_bwd(sg, r, ins[4][...], dsn)
        dq = dsg * ygv * s * (1.0 - s)
        outs[0][...] = dq.astype(BF16)
        outs[1][...] = dsg * s
        accs[1][...] += _colsum(dq)

    return _rowwise("s5_post2_bwd", body, yg.shape[0], [(dmixed, c, 1), (yg, c, 0), (q0, c, 0)], [bg, og],
                    [(c, BF16), (c, F32)], [(1, c)] * 2)


def _s5_post1_bwd(dyg1, dyg2, ypre, proj, dskip):
    c = ypre.shape[1]

    def body(ins, outs, accs):
        dyp = (ins[0][...] + ins[1][...]) * _dgelu(ins[2][...])
        outs[0][...] = dyp.astype(BF16)
        outs[1][...] = dyp * ins[4][...]
        accs[0][...] += _colsum(dyp * ins[3][...])

    return _rowwise("s5_post1_bwd", body, ypre.shape[0], [(dyg1, c, 0), (dyg2, c, 0), (ypre, c, 0), (proj, c, 2)], [dskip],
                    [(c, BF16), (c, F32)], [(1, c)])


def _place():
    return lax.axis_index("x"), lax.axis_index("y"), lax.axis_index("c")


def _window(ref, axis, q, rows, cols):
    if axis == 0:
        return ref.at[pl.ds(pl.multiple_of(q * rows, SUBLANE), rows), :]
    return ref.at[:, pl.ds(pl.multiple_of(q * cols, LANE), cols)]


def _all_gather_chips(shards, axes):
    n = len(shards)

    def kern(*refs):
        ins, outs = refs[:n], refs[n:2 * n]
        send_sems, recv_sems, local_sems = refs[2 * n:]
        x, y, c = _place()
        local, remote = [], []
        for a in range(n):
            rows, cols = shards[a].shape
            mine = pltpu.make_async_copy(ins[a], _window(outs[a], axes[a], 2 * x + y, rows, cols), local_sems.at[a])
            mine.start()
            local.append(mine)
            for j, (fx, fy) in enumerate(CHIP_RELS):
                px, py = (1 - x) if fx else x, (1 - y) if fy else y
                cp = pltpu.make_async_remote_copy(
                    src_ref=ins[a], dst_ref=_window(outs[a], axes[a], 2 * x + y, rows, cols),
                    send_sem=send_sems.at[3 * a + j], recv_sem=recv_sems.at[3 * a + j],
                    device_id=(px, py, c), device_id_type=MESH)
                cp.start()
                remote.append(pltpu.make_async_remote_copy(
                    src_ref=ins[a], dst_ref=_window(outs[a], axes[a], 2 * px + py, rows, cols),
                    send_sem=send_sems.at[3 * a + j], recv_sem=recv_sems.at[3 * a + j],
                    device_id=(px, py, c), device_id_type=MESH))
        for cp in remote:
            cp.wait()
        for cp in local:
            cp.wait()

    out_shape = [jax.ShapeDtypeStruct((N_CHIPS * s.shape[0], s.shape[1]) if ax == 0 else (s.shape[0], N_CHIPS * s.shape[1]),
                                      s.dtype) for s, ax in zip(shards, axes)]
    return pl.pallas_call(
        kern, name="gather_weights", in_specs=[ANY] * n, out_specs=[ANY] * n, out_shape=out_shape,
        scratch_shapes=[pltpu.SemaphoreType.DMA((3 * n,)), pltpu.SemaphoreType.DMA((3 * n,)), pltpu.SemaphoreType.DMA((n,))],
    )(*shards)


def _exchange_grad_shards(grads, axes):
    n = len(grads)
    shard = [(g.shape[0] // N_CHIPS, g.shape[1]) if ax == 0 else (g.shape[0], g.shape[1] // N_CHIPS)
             for g, ax in zip(grads, axes)]

    def kern(*refs):
        ins, outs = refs[:n], refs[n:2 * n]
        send_sems, recv_sems, local_sems = refs[2 * n:]
        x, y, c = _place()
        copies = []
        for a in range(n):
            rows, cols = shard[a]
            mine = pltpu.make_async_copy(_window(ins[a], axes[a], 2 * x + y, rows, cols), outs[a].at[3], local_sems.at[a])
            mine.start()
            copies.append(mine)
            for j, (fx, fy) in enumerate(CHIP_RELS):
                px, py = (1 - x) if fx else x, (1 - y) if fy else y
                cp = pltpu.make_async_remote_copy(
                    src_ref=_window(ins[a], axes[a], 2 * px + py, rows, cols), dst_ref=outs[a].at[j],
                    send_sem=send_sems.at[3 * a + j], recv_sem=recv_sems.at[3 * a + j],
                    device_id=(px, py, c), device_id_type=MESH)
                cp.start()
                copies.append(cp)
        for cp in copies:
            cp.wait()

    out_shape = [jax.ShapeDtypeStruct((N_CHIPS,) + s, F32) for s in shard]
    return pl.pallas_call(
        kern, name="exchange_grad_shards", in_specs=[ANY] * n, out_specs=[ANY] * n, out_shape=out_shape,
        scratch_shapes=[pltpu.SemaphoreType.DMA((3 * n,)), pltpu.SemaphoreType.DMA((3 * n,)), pltpu.SemaphoreType.DMA((n,))],
    )(*grads)


def _swap_with_sibling(arrs):
    n = len(arrs)

    def kern(*refs):
        ins, outs = refs[:n], refs[n:2 * n]
        send_sems, recv_sems = refs[2 * n:]
        x, y, c = _place()
        copies = [pltpu.make_async_remote_copy(src_ref=ins[a], dst_ref=outs[a], send_sem=send_sems.at[a],
                                               recv_sem=recv_sems.at[a], device_id=(x, y, 1 - c), device_id_type=MESH)
                  for a in range(n)]
        for cp in copies:
            cp.start()
        for cp in copies:
            cp.wait()

    return pl.pallas_call(
        kern, name="swap_with_sibling", in_specs=[ANY] * n, out_specs=[ANY] * n,
        out_shape=[jax.ShapeDtypeStruct(a.shape, a.dtype) for a in arrs],
        scratch_shapes=[pltpu.SemaphoreType.DMA((n,)), pltpu.SemaphoreType.DMA((n,))],
    )(*arrs)


def _all_reduce_small(buf):
    rels = [(fx, fy, fc) for fx in (0, 1) for fy in (0, 1) for fc in (0, 1)][1:]
    n_dev = len(rels) + 1

    def kern(b_ref, o_ref, recv_ref, send_sems, recv_sems):
        x, y, c = _place()
        me = 4 * x + 2 * y + c
        copies = []
        for k, (fx, fy, fc) in enumerate(rels):
            peer = ((1 - x) if fx else x, (1 - y) if fy else y, (1 - c) if fc else c)
            cp = pltpu.make_async_remote_copy(src_ref=b_ref, dst_ref=recv_ref.at[me], send_sem=send_sems.at[k],
                                              recv_sem=recv_sems.at[k], device_id=peer, device_id_type=MESH)
            cp.start()
            copies.append((cp, peer))
        recv_ref[me] = b_ref[...]
        for k, (cp, (px, py, pc)) in enumerate(copies):
            cp.wait_send()
            pltpu.make_async_remote_copy(src_ref=b_ref, dst_ref=recv_ref.at[4 * px + 2 * py + pc], send_sem=send_sems.at[k],
                                         recv_sem=recv_sems.at[k], device_id=(px, py, pc), device_id_type=MESH).wait_recv()
        acc = recv_ref[0]
        for d in range(1, n_dev):
            acc = acc + recv_ref[d]
        o_ref[...] = acc

    vm = pl.BlockSpec(memory_space=pltpu.VMEM)
    return pl.pallas_call(
        kern, name="all_reduce_small", in_specs=[vm], out_specs=vm, out_shape=jax.ShapeDtypeStruct(buf.shape, F32),
        scratch_shapes=[pltpu.VMEM((n_dev,) + buf.shape, F32), pltpu.SemaphoreType.DMA((n_dev - 1,)),
                        pltpu.SemaphoreType.DMA((n_dev - 1,))],
        compiler_params=pltpu.CompilerParams(vmem_limit_bytes=VMEM_LIMIT_BYTES),
    )(buf)


def _sum_slots(name, parts):
    _, rows, cols = parts.shape
    tr = _pick(rows, (ROW_TILE, 128, 64, 32))

    def kern(p_ref, o_ref):
        o_ref[...] = ((p_ref[3] + p_ref[0]) + p_ref[1]) + p_ref[2]

    return pl.pallas_call(kern, name=name, grid=(rows // tr,),
                          in_specs=[pl.BlockSpec((N_CHIPS, tr, cols), lambda i: (0, i, 0))],
                          out_specs=pl.BlockSpec((tr, cols), lambda i: (i, 0)),
                          out_shape=jax.ShapeDtypeStruct((rows, cols), F32), compiler_params=_params("arbitrary"))(parts)


def _adamw_math(g, w, m, v):
    m2 = ADAM_B1 * m + (1.0 - ADAM_B1) * g
    v2 = ADAM_B2 * v + (1.0 - ADAM_B2) * (g * g)
    m_hat = m2 / (1.0 - ADAM_B1 ** ADAM_STEP)
    v_hat = v2 / (1.0 - ADAM_B2 ** ADAM_STEP)
    return -ADAM_LR * (m_hat / (jnp.sqrt(v_hat) + ADAM_EPS) + ADAM_WD * w), m2, v2


def _adamw(name, parts, w, m, v):
    rows, cols = w.shape
    tr = rows if rows * cols <= WHOLE_ELEMS else _pick(rows, (ROW_TILE, 352, 128, 64, 32, 8))
    n = len(parts)

    def kern(*refs):
        g = refs[0][:, pl.ds(0, cols)]
        for p in refs[1:n]:
            g = g + p[:, pl.ds(0, cols)]
        d, m2, v2 = _adamw_math(g, refs[n][...], refs[n + 1][...], refs[n + 2][...])
        refs[n + 3][...] = g
        refs[n + 4][...] = d
        refs[n + 5][...] = m2
        refs[n + 6][...] = v2

    spec = pl.BlockSpec((tr, cols), lambda i: (i, 0))
    return pl.pallas_call(kern, name=name, grid=(rows // tr,),
                          in_specs=[pl.BlockSpec((tr, p.shape[1]), lambda i: (i, 0)) for p in parts] + [spec] * 3,
                          out_specs=[spec] * 4, out_shape=[jax.ShapeDtypeStruct((rows, cols), F32)] * 4,
                          compiler_params=_params("arbitrary"))(*parts, w, m, v)


def _pack(arrs):
    parts, rows = [], []
    for a in arrs:
        r = _round_up(-(-a.size // LANE), SUBLANE)
        parts.append(jnp.pad(a.reshape(-1).astype(F32), (0, r * LANE - a.size)).reshape(r, LANE))
        rows.append(r)
    return jnp.concatenate(parts, axis=0), rows


def _unpack(buf, rows, shapes):
    out, r0 = [], 0
    for r, s in zip(rows, shapes):
        size = math.prod(s)
        out.append(buf[r0:r0 + r].reshape(-1)[:size].reshape(s))
        r0 += r
    return out


def kernel(x, norm_ffn1, ffn1_w1, ffn1_w3, ffn1_w2, norm_mix, w_in, conv_w, conv_b, conv_ln_g, conv_ln_b, conv_out_g, ssm_A_re, ssm_A_im, ssm_log_dt, ssm_B_re, ssm_B_im, ssm_C_re, ssm_C_im, ssm_D, ssm_glu_w, ssm_glu_b, ssm_out_g, w_out, norm_ffn2, ffn2_w1, ffn2_w3, ffn2_w2, norm_final, loss_target, m_norm_ffn1, m_ffn1_w1, m_ffn1_w3, m_ffn1_w2, m_norm_mix, m_w_in, m_conv_w, m_conv_b, m_conv_ln_g, m_conv_ln_b, m_conv_out_g, m_ssm_A_re, m_ssm_A_im, m_ssm_log_dt, m_ssm_B_re, m_ssm_B_im, m_ssm_C_re, m_ssm_C_im, m_ssm_D, m_ssm_glu_w, m_ssm_glu_b, m_ssm_out_g, m_w_out, m_norm_ffn2, m_ffn2_w1, m_ffn2_w3, m_ffn2_w2, m_norm_final, v_norm_ffn1, v_ffn1_w1, v_ffn1_w3, v_ffn1_w2, v_norm_mix, v_w_in, v_conv_w, v_conv_b, v_conv_ln_g, v_conv_ln_b, v_conv_out_g, v_ssm_A_re, v_ssm_A_im, v_ssm_log_dt, v_ssm_B_re, v_ssm_B_im, v_ssm_C_re, v_ssm_C_im, v_ssm_D, v_ssm_glu_w, v_ssm_glu_b, v_ssm_out_g, v_w_out, v_norm_ffn2, v_ffn2_w1, v_ffn2_w3, v_ffn2_w2, v_norm_final):
    given = dict(locals())
    wts = {n: given[n] for n in WEIGHTS}
    n_seq, seq, d = x.shape
    n_rows = n_seq * seq
    xf = x.reshape(n_rows, d)
    tgt = loss_target.reshape(n_rows, d)
    row = lambda a: a.reshape(1, -1)

    f = ffn1_w1.shape[-1]
    fp = _round_up(f, LANE)
    shards = []
    for n in BIG:
        s = wts[n][0].astype(BF16)
        if n.endswith('_w1') or n.endswith('_w3'):
            s = jnp.pad(s, ((0, 0), (0, fp - f)))
        elif n.endswith('_w2'):
            s = jnp.pad(s, ((0, fp - f), (0, 0)))
        shards.append(s)
    n_taps, c_shard = conv_w.shape[1], conv_w.shape[2]
    shards.append(jnp.pad(conv_w[0], ((0, HALO - n_taps), (0, 0))))
    gathered = _all_gather_chips(shards, [BIG_AXIS[n] for n in BIG] + [1])
    full = dict(zip(BIG, gathered))
    cw = gathered[-1]

    _, n_grp, n_state = ssm_A_re.shape
    grp = ssm_B_re.shape[-1]
    ns = n_grp * n_state
    c_ssm = n_grp * grp
    lr, li = ssm_A_re.reshape(1, ns), ssm_A_im.reshape(1, ns)
    ldt = jnp.repeat(ssm_log_dt.reshape(n_grp), n_state).reshape(1, ns)
    btr = ssm_B_re[0].transpose(2, 0, 1).reshape(grp, ns)
    bti = ssm_B_im[0].transpose(2, 0, 1).reshape(grp, ns)
    ctr = ssm_C_re[0].transpose(1, 0, 2).reshape(grp, ns)
    cti = ssm_C_im[0].transpose(1, 0, 2).reshape(grp, ns)
    _, _, bbr, bbi, pw = _s5_params_fwd(lr, li, ldt, btr, bti)
    diag = (jnp.arange(c_ssm)[:, None] // grp) == (jnp.arange(ns)[None, :] // n_state)
    spread = lambda t: jnp.where(diag, jnp.tile(t, (n_grp, 1)), 0.0)
    gather_diag = lambda t: (t * diag).reshape(n_grp, grp, ns).sum(0)
    bd = jnp.concatenate([spread(bbr), spread(bbi)], axis=1).astype(BF16)
    cd = jnp.concatenate([spread(ctr).T, -spread(cti).T], axis=0).astype(BF16)
    rowi = jnp.arange(SUBLANE)[:, None]
    pwc = jnp.concatenate([pw[:, :ns], -pw[:, ns:]], axis=1)
    tabs_f = [jnp.where(rowi >= s, pw[s - 1][None, :], 0.0) for s in (1, 2, 4)] + [pw]
    tabs_b = [jnp.where(rowi <= SUBLANE - 1 - s, pwc[s - 1][None, :], 0.0) for s in (1, 2, 4)] + [pwc[::-1]]

    x1, saved1 = _ffn_fwd("ffn1", xf, norm_ffn1, full['ffn1_w1'], full['ffn1_w3'], full['ffn1_w2'])
    h2 = _rms_fwd("mix_rms", x1, norm_mix)
    proj = _mm("mix_in", h2, full['w_in'], 1, 0, F32)
    c_conv = conv_b.shape[1]
    assert c_conv == c_ssm and proj.shape[1] == 3 * c_conv
    cpre, an = _conv_fwd(proj, cw, conv_b, conv_ln_g, conv_ln_b, conv_out_g, seq)
    bu = _mm("s5_bu", proj, bd, 1, 0, F32, a_cols=(2 * c_conv, c_ssm))
    xs = _scan("s5_scan", bu, None, tabs_f, seq, False)[0]
    y0 = _mm("s5_cx", xs, cd, 1, 0, F32)
    ypre, yg = _s5_post1(y0, proj, ssm_D)
    q0 = _mm("s5_gate", yg, full['ssm_glu_w'], 1, 0, F32)
    sn = _s5_post2(yg, q0, ssm_glu_b, ssm_out_g)
    wo = full['w_out']
    x2 = _mm("mix_out_a", an, wo[:c_conv], 1, 0, F32, addend=x1)
    x2 = _mm("mix_out_s", sn, wo[c_conv:], 1, 0, F32, addend=x2)
    x3, saved2 = _ffn_fwd("ffn2", x2, norm_ffn2, full['ffn2_w1'], full['ffn2_w3'], full['ffn2_w2'])
    dx3, loss_row, d_norm_final = _loss_head(x3, row(norm_final), tgt)

    g = {}
    dx2, g['norm_ffn2'], dw1, dw3, dw2 = _ffn_bwd("ffn2", x2, norm_ffn2, full['ffn2_w1'], full['ffn2_w3'],
                                                   full['ffn2_w2'], saved2, dx3)
    big_g = {'ffn2_w1': dw1, 'ffn2_w3': dw3, 'ffn2_w2': dw2}
    dmixed = _mm("mix_dmixed", dx2, wo, 1, 1, F32)
    big_g['w_out'] = jnp.concatenate([_mm("mix_dwo_a", an, dx2, 0, 0), _mm("mix_dwo_s", sn, dx2, 0, 0)], axis=0)
    dq, dyg1, g['ssm_out_g'], g['ssm_glu_b'] = _s5_post2_bwd(dmixed, yg, q0, ssm_glu_b, ssm_out_g)
    dyg2 = _mm("s5_dgate", dq, full['ssm_glu_w'], 1, 1, F32)
    big_g['ssm_glu_w'] = _mm("s5_dwg", yg, dq, 0, 0)
    dypre, du_skip, g['ssm_D'] = _s5_post1_bwd(dyg1, dyg2, ypre, proj, ssm_D)
    gx = _mm("s5_dx", dypre, cd, 1, 1, F32)
    dcd = _mm("s5_dc", xs, dypre, 0, 0)
    lam, dabar = _scan("s5_scan_bwd", gx, xs, tabs_b, seq, True)
    du = _mm("s5_du", lam, bd, 1, 1, F32, addend=du_skip)
    dbd = _mm("s5_db", proj, lam, 0, 0, a_cols=(2 * c_conv, c_ssm))
    dlr, dli, dldt, dbtr, dbti = _s5_params_bwd(lr, li, ldt, btr, bti, dabar[:, :ns], dabar[:, ns:],
                                                gather_diag(dbd[:, :ns]), gather_diag(dbd[:, ns:]))
    g['ssm_A_re'], g['ssm_A_im'] = dlr, dli
    g['ssm_log_dt'] = dldt.reshape(n_grp, n_state).sum(axis=1)
    g['ssm_B_re'] = dbtr.reshape(grp, n_grp, n_state).transpose(1, 2, 0)
    g['ssm_B_im'] = dbti.reshape(grp, n_grp, n_state).transpose(1, 2, 0)
    g['ssm_C_re'] = gather_diag(dcd[:ns].T).reshape(grp, n_grp, n_state).transpose(1, 0, 2)
    g['ssm_C_im'] = -gather_diag(dcd[ns:].T).reshape(grp, n_grp, n_state).transpose(1, 0, 2)
    dc, g['conv_out_g'], g['conv_ln_g'], g['conv_ln_b'], g['conv_b'] = _conv_bwd_rows(dmixed, cpre, conv_ln_g, conv_ln_b,
                                                                                    conv_out_g)
    dval, dgate, dcw = _conv_bwd_taps(proj, dc, cw, seq)
    dproj = jnp.concatenate([dval, dgate, du], axis=1)
    big_g['w_in'] = _mm("mix_dwin", h2, dproj, 0, 0)
    dh2 = _mm("mix_dh", dproj, full['w_in'], 1, 1, F32)
    dx1, g['norm_mix'] = _rms_bwd_res("mix_drms", x1, norm_mix, dh2, dx2)
    dx0, g['norm_ffn1'], dw1, dw3, dw2 = _ffn_bwd("ffn1", xf, norm_ffn1, full['ffn1_w1'], full['ffn1_w3'],
                                                   full['ffn1_w2'], saved1, dx1)
    big_g.update({'ffn1_w1': dw1, 'ffn1_w3': dw3, 'ffn1_w2': dw2})
    g['norm_final'] = d_norm_final
    g['conv_w'] = dcw[:n_taps]

    small_shapes = [(n_taps, c_conv) if n == 'conv_w' else wts[n].shape for n in SMALL]
    buf, buf_rows = _pack([g[n] for n in SMALL] + [loss_row])
    total = _unpack(_all_reduce_small(buf), buf_rows, small_shapes + [(1, LANE)])
    loss = total[-1][0, 0]
    grads = dict(zip(SMALL, total[:-1]))
    chip = 2 * lax.axis_index("x") + lax.axis_index("y")
    grads['conv_w'] = lax.dynamic_slice_in_dim(grads['conv_w'], chip * c_shard, c_shard, axis=1)[None]
    g_buf, rows_s = _pack([grads[n] for n in SMALL])
    packed = [_pack([given[p + n] for n in SMALL])[0] for p in ('', 'm_', 'v_')]
    _, d_buf, m_buf, v_buf = _adamw("adamw_small", [g_buf], *packed)
    shapes_s = [wts[n].shape for n in SMALL]
    deltas = dict(zip(SMALL, _unpack(d_buf, rows_s, shapes_s)))
    new_m = dict(zip(SMALL, _unpack(m_buf, rows_s, shapes_s)))
    new_v = dict(zip(SMALL, _unpack(v_buf, rows_s, shapes_s)))

    slots = _exchange_grad_shards([big_g[n] for n in BIG], [BIG_AXIS[n] for n in BIG])
    sums = [_sum_slots("sum_" + n, s) for n, s in zip(BIG, slots)]
    theirs = _swap_with_sibling(sums)
    for n, mine, other in zip(BIG, sums, theirs):
        grads[n], deltas[n], new_m[n], new_v[n] = (
            o[None] for o in _adamw("adamw_" + n, [mine, other], given[n][0], given['m_' + n][0], given['v_' + n][0]))

    return (loss, dx0.reshape(x.shape), *[grads[n] for n in WEIGHTS], *[deltas[n] for n in WEIGHTS],
            *[new_m[n] for n in WEIGHTS], *[new_v[n] for n in WEIGHTS])
```

```python
import functools
import math

import jax
import jax.numpy as jnp
from jax import lax
from jax.experimental import pallas as pl
from jax.experimental.pallas import tpu as pltpu

F32 = jnp.float32
BF16 = jnp.bfloat16
EPS = 1e-6
ADAM_LR, ADAM_B1, ADAM_B2, ADAM_EPS, ADAM_WD, ADAM_STEP = 0.001, 0.9, 0.999, 1e-08, 0.01, 10
MESH = pl.DeviceIdType.MESH
ANY = pl.BlockSpec(memory_space=pl.ANY)
LANE = 128
SUBLANE = 8
VMEM_LIMIT_BYTES = 56 << 20
ROW_TILE = 256
ROW_TILE_ELEMS = 256 * 1024
WHOLE_ELEMS = 512 * 1024
CONV_TILE = 128
CONV_SUB = 32
HALO = 32
SCAN_TILE = 128
SCAN_COLS = 512
N_CHIPS = 4
CHIP_RELS = ((1, 0), (0, 1), (1, 1))
NT = (((1,), (1,)), ((), ()))
GELU_K = math.sqrt(2.0 / math.pi)
GELU_C = 0.044715

WEIGHTS = ['norm_ffn1', 'ffn1_w1', 'ffn1_w3', 'ffn1_w2', 'norm_mix', 'w_in', 'conv_w', 'conv_b', 'conv_ln_g', 'conv_ln_b',
           'conv_out_g', 'ssm_A_re', 'ssm_A_im', 'ssm_log_dt', 'ssm_B_re', 'ssm_B_im', 'ssm_C_re', 'ssm_C_im', 'ssm_D',
           'ssm_glu_w', 'ssm_glu_b', 'ssm_out_g', 'w_out', 'norm_ffn2', 'ffn2_w1', 'ffn2_w3', 'ffn2_w2', 'norm_final']
BIG = ['ffn1_w1', 'ffn1_w3', 'ffn1_w2', 'w_in', 'ssm_glu_w', 'w_out', 'ffn2_w1', 'ffn2_w3', 'ffn2_w2']
BIG_AXIS = {'ffn1_w1': 1, 'ffn1_w3': 1, 'ffn1_w2': 0, 'w_in': 1, 'ssm_glu_w': 0, 'w_out': 0, 'ffn2_w1': 1, 'ffn2_w3': 1,
            'ffn2_w2': 0}
SMALL = [n for n in WEIGHTS if n not in BIG]


def _round_up(n, m):
    return -(-n // m) * m


def _pick(n, cands):
    for c in cands:
        if c <= n and n % c == 0:
            return c
    return n


def _params(*sem):
    return pltpu.CompilerParams(dimension_semantics=sem, vmem_limit_bytes=VMEM_LIMIT_BYTES)


def _rms_r(x):
    return lax.rsqrt(jnp.mean(x * x, axis=-1, keepdims=True) + EPS)


def _rms_bwd(x, r, g, dy):
    dyg = dy * g
    return r * dyg - x * (r * r * r) * jnp.mean(x * dyg, axis=-1, keepdims=True)


def _sigmoid(x):
    return jax.nn.sigmoid(x)


def _dsilu(a, s):
    return s * (1.0 + a * (1.0 - s))


def _gelu(x):
    return 0.5 * x * (1.0 + jnp.tanh(GELU_K * (x + GELU_C * x * x * x)))


def _dgelu(x):
    t = jnp.tanh(GELU_K * (x + GELU_C * x * x * x))
    return 0.5 * (1.0 + t) + 0.5 * x * (1.0 - t * t) * GELU_K * (1.0 + 3.0 * GELU_C * x * x)


def _colsum(v):
    return jnp.sum(v, axis=0, keepdims=True)


def _rowwise(name, body, n_rows, row_ins, par_ins, row_outs, acc_outs):
    widest = max([w for (_, w, _) in row_ins] + [w for (w, _) in row_outs])
    tt = _pick(n_rows, [t for t in (256, 128, 64, 32, 16, 8) if t * widest <= ROW_TILE_ELEMS])
    in_specs = [pl.BlockSpec((tt, w), lambda i, cb=cb: (i, cb)) for (_, w, cb) in row_ins]
    in_specs += [pl.BlockSpec(p.shape, lambda i: (0, 0)) for p in par_ins]
    out_specs = [pl.BlockSpec((tt, w), lambda i: (i, 0)) for (w, _) in row_outs]
    out_specs += [pl.BlockSpec((r, w), lambda i: (0, 0)) for (r, w) in acc_outs]
    out_shape = [jax.ShapeDtypeStruct((n_rows, w), dt) for (w, dt) in row_outs]
    out_shape += [jax.ShapeDtypeStruct((r, w), F32) for (r, w) in acc_outs]
    n_in, n_ro = len(row_ins) + len(par_ins), len(row_outs)

    def kern(*refs):
        accs = refs[n_in + n_ro:]
        if accs:
            @pl.when(pl.program_id(0) == 0)
            def _():
                for a in accs:
                    a[...] = jnp.zeros_like(a)
        body(refs[:n_in], refs[n_in:n_in + n_ro], accs)

    return pl.pallas_call(kern, name=name, grid=(n_rows // tt,), in_specs=in_specs, out_specs=out_specs,
                          out_shape=out_shape, compiler_params=_params("arbitrary"))(*[a for a, _, _ in row_ins], *par_ins)


def _mm(name, a, b, ca, cb, out_dtype=F32, addend=None, alpha=1.0, a_cols=None):
    a_start, a_width = a_cols if a_cols else (0, a.shape[1])
    m, k = (a.shape[0], a_width) if ca == 1 else (a_width, a.shape[0])
    n = b.shape[1 - cb]
    assert b.shape[cb] == k, (name, a.shape, b.shape)
    tm = _pick(m, (1024, 512, 256, 128))
    tn = _pick(n, (1024, 768, 512, 384, 256, 128))
    tk = _pick(k, (1024, 768, 512, 256, 128))
    nk = k // tk
    if ca == 1:
        assert a_start % tk == 0
        a_spec = pl.BlockSpec((tm, tk), lambda i, j, kk: (i, kk + a_start // tk))
    else:
        assert a_start % tm == 0
        a_spec = pl.BlockSpec((tk, tm), lambda i, j, kk: (kk, i + a_start // tm))
    b_spec = pl.BlockSpec((tk, tn), lambda i, j, kk: (kk, j)) if cb == 0 else pl.BlockSpec((tn, tk), lambda i, j, kk: (j, kk))
    o_spec = pl.BlockSpec((tm, tn), lambda i, j, kk: (i, j))
    ins, in_specs = [a, b], [a_spec, b_spec]
    if addend is not None:
        ins.append(addend)
        in_specs.append(o_spec)
    dims = (((ca,), (cb,)), ((), ()))

    def finish(refs, r):
        if alpha != 1.0:
            r = r * alpha
        if addend is not None:
            r = r + refs[2][...].astype(F32)
        return r.astype(out_dtype)

    def kern_one(*refs):
        refs[-1][...] = finish(refs, lax.dot_general(refs[0][...].astype(BF16), refs[1][...].astype(BF16), dims,
                                                     preferred_element_type=F32))

    def kern_acc(*refs):
        o_ref, acc_ref = refs[-2], refs[-1]
        kk = pl.program_id(2)

        @pl.when(kk == 0)
        def _():
            acc_ref[...] = jnp.zeros_like(acc_ref)

        acc_ref[...] += lax.dot_general(refs[0][...].astype(BF16), refs[1][...].astype(BF16), dims,
                                        preferred_element_type=F32)

        @pl.when(kk == nk - 1)
        def _():
            o_ref[...] = finish(refs, acc_ref[...])

    return pl.pallas_call(kern_one if nk == 1 else kern_acc, name=name, grid=(m // tm, n // tn, nk), in_specs=in_specs,
                          out_specs=o_spec, out_shape=jax.ShapeDtypeStruct((m, n), out_dtype),
                          scratch_shapes=[] if nk == 1 else [pltpu.VMEM((tm, tn), F32)],
                          compiler_params=_params("arbitrary", "arbitrary", "arbitrary"))(*ins)


def _bmm_rows(name, a, a_blk, a_off, w, trans_w, out_blk, addend=None):
    t, nb = a.shape[0], w.shape[0]
    tm = _pick(t, (1024, 512, 256, 128))
    dims = NT if trans_w else (((1,), (0,)), ((), ()))

    def kern(*refs):
        r = lax.dot_general(refs[0][...].astype(BF16), refs[1][...], dims, preferred_element_type=F32)
        if addend is not None:
            r = r + refs[2][...]
        refs[-1][...] = r

    o_spec = pl.BlockSpec((tm, out_blk), lambda i, j: (i, j))
    in_specs = [pl.BlockSpec((tm, a_blk), lambda i, j: (i, a_off + j)),
                pl.BlockSpec((None,) + w.shape[1:], lambda i, j: (j, 0, 0))]
    ins = [a, w]
    if addend is not None:
        in_specs.append(o_spec)
        ins.append(addend)
    return pl.pallas_call(kern, name=name, grid=(t // tm, nb), in_specs=in_specs, out_specs=o_spec,
                          out_shape=jax.ShapeDtypeStruct((t, nb * out_blk), F32),
                          compiler_params=_params("arbitrary", "arbitrary"))(*ins)


def _bmm_wgrad(name, a, a_blk, a_off, b, b_blk):
    t, nb = a.shape[0], b.shape[1] // b_blk
    tk = _pick(t, (1024, 512, 256, 128))
    nk = t // tk

    def kern(a_ref, b_ref, o_ref, acc_ref):
        kk = pl.program_id(1)
        part = lax.dot_general(a_ref[...].astype(BF16), b_ref[...].astype(BF16), (((0,), (0,)), ((), ())),
                               preferred_element_type=F32)

        @pl.when(kk == 0)
        def _():
            acc_ref[...] = part

        @pl.when(kk > 0)
        def _():
            acc_ref[...] += part

        @pl.when(kk == nk - 1)
        def _():
            o_ref[...] = acc_ref[...]

    return pl.pallas_call(kern, name=name, grid=(nb, nk),
                          in_specs=[pl.BlockSpec((tk, a_blk), lambda j, kk: (kk, a_off + j)),
                                    pl.BlockSpec((tk, b_blk), lambda j, kk: (kk, j))],
                          out_specs=pl.BlockSpec((None, a_blk, b_blk), lambda j, kk: (j, 0, 0)),
                          out_shape=jax.ShapeDtypeStruct((nb, a_blk, b_blk), F32),
                          scratch_shapes=[pltpu.VMEM((a_blk, b_blk), F32)],
                          compiler_params=_params("arbitrary", "arbitrary"))(a, b)


def _rms_fwd(name, x, g):
    def body(ins, outs, accs):
        xv = ins[0][...]
        outs[0][...] = (xv * _rms_r(xv) * ins[1][...]).astype(BF16)

    return _rowwise(name, body, x.shape[0], [(x, x.shape[1], 0)], [g], [(x.shape[1], BF16)], [])[0]


def _rms_bwd_res(name, x, g, dh, dres):
    d = x.shape[1]

    def body(ins, outs, accs):
        xv, dhv, gv = ins[0][...], ins[1][...], ins[3][...]
        r = _rms_r(xv)
        outs[0][...] = ins[2][...] + _rms_bwd(xv, r, gv, dhv)
        accs[0][...] += _colsum(dhv * xv * r)

    return _rowwise(name, body, x.shape[0], [(x, d, 0), (dh, d, 0), (dres, d, 0)], [g], [(d, F32)], [(1, d)])


def _ffn_up(name, h, w1, w3):
    t, d = h.shape
    ff = w1.shape[1]
    tm, tn = _pick(t, (1024, 512, 256, 128)), _pick(ff, (1024, 768, 512, 256, 128))

    def kern(h_ref, w1_ref, w3_ref, a_ref, b_ref, z_ref):
        hv = h_ref[...]
        a = jnp.dot(hv, w1_ref[...], preferred_element_type=F32)
        b = jnp.dot(hv, w3_ref[...], preferred_element_type=F32)
        a_ref[...] = a.astype(BF16)
        b_ref[...] = b.astype(BF16)
        z_ref[...] = (a * _sigmoid(a) * b).astype(BF16)

    w_spec = pl.BlockSpec((d, tn), lambda i, j: (0, j))
    o_spec = pl.BlockSpec((tm, tn), lambda i, j: (i, j))
    return pl.pallas_call(kern, name=name, grid=(t // tm, ff // tn),
                          in_specs=[pl.BlockSpec((tm, d), lambda i, j: (i, 0)), w_spec, w_spec], out_specs=[o_spec] * 3,
                          out_shape=[jax.ShapeDtypeStruct((t, ff), BF16)] * 3,
                          compiler_params=_params("arbitrary", "arbitrary"))(h, w1, w3)


def _ffn_dglu(name, dxo, w2, a, b):
    t, d = dxo.shape
    ff = w2.shape[0]
    tm, tn = _pick(t, (1024, 512, 256, 128)), _pick(ff, (1024, 768, 512, 256, 128))

    def kern(dx_ref, w2_ref, a_ref, b_ref, da_ref, db_ref):
        dz = lax.dot_general(dx_ref[...].astype(BF16), w2_ref[...], NT, preferred_element_type=F32) * 0.5
        av, bv = a_ref[...].astype(F32), b_ref[...].astype(F32)
        s = _sigmoid(av)
        da_ref[...] = (dz * bv * _dsilu(av, s)).astype(BF16)
        db_ref[...] = (dz * av * s).astype(BF16)

    o_spec = pl.BlockSpec((tm, tn), lambda i, j: (i, j))
    return pl.pallas_call(kern, name=name, grid=(t // tm, ff // tn),
                          in_specs=[pl.BlockSpec((tm, d), lambda i, j: (i, 0)), pl.BlockSpec((tn, d), lambda i, j: (j, 0)),
                                    o_spec, o_spec],
                          out_specs=[o_spec] * 2, out_shape=[jax.ShapeDtypeStruct((t, ff), BF16)] * 2,
                          compiler_params=_params("arbitrary", "arbitrary"))(dxo, w2, a, b)


def _ffn_dh(name, da, db, w1, w3, x, g, dres):
    t, d = x.shape
    ff = da.shape[1]
    tm, tk = _pick(t, (512, 256, 128)), _pick(ff, (1024, 768, 512, 256, 128))
    nk = ff // tk

    def kern(da_ref, db_ref, w1_ref, w3_ref, x_ref, g_ref, dres_ref, dx_ref, dg_ref, acc_ref):
        i, kk = pl.program_id(0), pl.program_id(1)

        @pl.when(jnp.logical_and(i == 0, kk == 0))
        def _():
            dg_ref[...] = jnp.zeros_like(dg_ref)

        part = (lax.dot_general(da_ref[...], w1_ref[...], NT, preferred_element_type=F32)
                + lax.dot_general(db_ref[...], w3_ref[...], NT, preferred_element_type=F32))

        @pl.when(kk == 0)
        def _():
            acc_ref[...] = part

        @pl.when(kk > 0)
        def _():
            acc_ref[...] += part

        @pl.when(kk == nk - 1)
        def _():
            dh, xv = acc_ref[...], x_ref[...]
            r = _rms_r(xv)
            dx_ref[...] = dres_ref[...] + _rms_bwd(xv, r, g_ref[...], dh)
            dg_ref[...] += _colsum(dh * xv * r)

    act = pl.BlockSpec((tm, tk), lambda i, kk: (i, kk))
    wgt = pl.BlockSpec((d, tk), lambda i, kk: (0, kk))
    rows = pl.BlockSpec((tm, d), lambda i, kk: (i, 0))
    gain = pl.BlockSpec((1, d), lambda i, kk: (0, 0))
    return pl.pallas_call(kern, name=name, grid=(t // tm, nk), in_specs=[act, act, wgt, wgt, rows, gain, rows],
                          out_specs=[rows, gain],
                          out_shape=[jax.ShapeDtypeStruct((t, d), F32), jax.ShapeDtypeStruct((1, d), F32)],
                          scratch_shapes=[pltpu.VMEM((tm, d), F32)],
                          compiler_params=_params("arbitrary", "arbitrary"))(da, db, w1, w3, x, g, dres)


def _loss_head(x3, gf, tgt):
    d = x3.shape[1]

    def body(ins, outs, accs):
        xv, tv, gv = ins[0][...], ins[1][...], ins[2][...]
        r = _rms_r(xv)
        e = xv * r * gv - tv
        sq = jnp.sum(jnp.sum(e * e, axis=-1, keepdims=True), axis=0, keepdims=True)
        accs[0][...] += jnp.broadcast_to(sq * (0.5 / d), (1, LANE))
        dy = e * (1.0 / d)
        outs[0][...] = _rms_bwd(xv, r, gv, dy)
        accs[1][...] += _colsum(dy * xv * r)

    return _rowwise("loss_head", body, x3.shape[0], [(x3, d, 0), (tgt, d, 0)], [gf], [(d, F32)], [(1, LANE), (1, d)])


def _ffn_fwd(tag, x, g, w1, w3, w2):
    h = _rms_fwd(tag + "_rms", x, g)
    a, b, z = _ffn_up(tag + "_up", h, w1, w3)
    return _mm(tag + "_down", z, w2, 1, 0, F32, addend=x, alpha=0.5), (h, a, b, z)


def _ffn_bwd(tag, x, g, w1, w3, w2, saved, dxo):
    h, a, b, z = saved
    dw2 = _mm(tag + "_dw2", z, dxo, 0, 0, BF16, alpha=0.5)
    da, db = _ffn_dglu(tag + "_dglu", dxo, w2, a, b)
    dw1 = _mm(tag + "_dw1", h, da, 0, 0, BF16)
    dw3 = _mm(tag + "_dw3", h, db, 0, 0, BF16)
    dx, dg = _ffn_dh(tag + "_dh", da, db, w1, w3, x, g, dxo)
    return dx, dg, dw1, dw3, dw2


def _conv_fwd(proj, cw, cb, lng, lnb, og, seq):
    n_rows, c = proj.shape[0], cb.shape[1]
    kw = HALO - 1
    tt = _pick(seq, (CONV_TILE,))
    hb = tt // HALO

    def kern(v_ref, g_ref, vp_ref, gp_ref, w_ref, cb_ref, lg_ref, lb_ref, og_ref, c_ref, an_ref, ext_ref):
        first = (pl.program_id(0) * tt) % seq == 0
        ext_ref[pl.ds(HALO, tt), :] = v_ref[...] * _sigmoid(g_ref[...])
        ext_ref[pl.ds(0, HALO), :] = vp_ref[...] * _sigmoid(gp_ref[...]) * jnp.where(first, 0.0, 1.0)
        for r0 in range(0, tt, CONV_SUB):
            rows = min(CONV_SUB, tt - r0)
            acc = jnp.zeros((rows, c), F32)
            for k in range(kw):
                acc = acc + w_ref[pl.ds(k, 1), :] * ext_ref[pl.ds(r0 + HALO - (kw - 1) + k, rows), :]
            c_ref[pl.ds(r0, rows), :] = acc + cb_ref[...]
        cv = c_ref[...]
        mu = jnp.mean(cv, axis=-1, keepdims=True)
        xc = cv - mu
        rstd = lax.rsqrt(jnp.mean(xc * xc, axis=-1, keepdims=True) + EPS)
        lv = xc * rstd * lg_ref[...] + lb_ref[...]
        sl = lv * _sigmoid(lv)
        an_ref[...] = (sl * _rms_r(sl) * og_ref[...]).astype(BF16)

    cur = lambda cbk: pl.BlockSpec((tt, c), lambda i: (i, cbk))
    prev = lambda cbk: pl.BlockSpec((HALO, c), lambda i: (jnp.maximum(i * hb - 1, 0), cbk))
    par = lambda p: pl.BlockSpec(p.shape, lambda i: (0, 0))
    return pl.pallas_call(
        kern, name="conv_fwd", grid=(n_rows // tt,),
        in_specs=[cur(0), cur(1), prev(0), prev(1), par(cw), par(cb), par(lng), par(lnb), par(og)],
        out_specs=[pl.BlockSpec((tt, c), lambda i: (i, 0))] * 2,
        out_shape=[jax.ShapeDtypeStruct((n_rows, c), F32), jax.ShapeDtypeStruct((n_rows, c), BF16)],
        scratch_shapes=[pltpu.VMEM((tt + HALO, c), F32)], compiler_params=_params("arbitrary"),
    )(proj, proj, proj, proj, cw, cb, lng, lnb, og)


def _conv_bwd_rows(dmixed, cpre, lng, lnb, og):
    c = cpre.shape[1]

    def body(ins, outs, accs):
        dan, cv, lg, lb, ogv = ins[0][...], ins[1][...], ins[2][...], ins[3][...], ins[4][...]
        mu = jnp.mean(cv, axis=-1, keepdims=True)
        xc = cv - mu
        rstd = lax.rsqrt(jnp.mean(xc * xc, axis=-1, keepdims=True) + EPS)
        xh = xc * rstd
        lv = xh * lg + lb
        s = _sigmoid(lv)
        sl = lv * s
        r2 = _rms_r(sl)
        accs[0][...] += _colsum(dan * sl * r2)
        dl = _rms_bwd(sl, r2, ogv, dan) * _dsilu(lv, s)
        accs[1][...] += _colsum(dl * xh)
        accs[2][...] += _colsum(dl)
        dxh = dl * lg
        dc = rstd * (dxh - jnp.mean(dxh, axis=-1, keepdims=True) - xh * jnp.mean(dxh * xh, axis=-1, keepdims=True))
        outs[0][...] = dc
        accs[3][...] += _colsum(dc)

    return _rowwise("conv_bwd_rows", body, cpre.shape[0], [(dmixed, c, 0), (cpre, c, 0)], [lng, lnb, og], [(c, F32)],
                    [(1, c)] * 4)


def _conv_bwd_taps(proj, dc, cw, seq):
    n_rows, c = dc.shape
    kw = HALO - 1
    tt = _pick(seq, (CONV_TILE,))
    hb = tt // HALO
    last_blk = n_rows // HALO - 1

    def kern(v_ref, g_ref, vp_ref, gp_ref, dc_ref, dn_ref, w_ref, dv_ref, dg_ref, dw_ref, exta_ref, extd_ref):
        i = pl.program_id(0)
        first = (i * tt) % seq == 0
        last = ((i + 1) * tt) % seq == 0

        @pl.when(i == 0)
        def _():
            dw_ref[...] = jnp.zeros_like(dw_ref)

        sg = _sigmoid(g_ref[...])
        exta_ref[pl.ds(HALO, tt), :] = v_ref[...] * sg
        exta_ref[pl.ds(0, HALO), :] = vp_ref[...] * _sigmoid(gp_ref[...]) * jnp.where(first, 0.0, 1.0)
        dcv = dc_ref[...]
        extd_ref[pl.ds(0, tt), :] = dcv
        extd_ref[pl.ds(tt, HALO), :] = dn_ref[...] * jnp.where(last, 0.0, 1.0)
        for k in range(kw):
            dw_ref[pl.ds(k, 1), :] += _colsum(exta_ref[pl.ds(HALO - (kw - 1) + k, tt), :] * dcv)
        for r0 in range(0, tt, CONV_SUB):
            rows = min(CONV_SUB, tt - r0)
            acc = jnp.zeros((rows, c), F32)
            for k in range(kw):
                acc = acc + w_ref[pl.ds(k, 1), :] * extd_ref[pl.ds(r0 + (kw - 1) - k, rows), :]
            dv_ref[pl.ds(r0, rows), :] = acc
        da = dv_ref[...]
        dv_ref[...] = da * sg
        dg_ref[...] = da * v_ref[...] * sg * (1.0 - sg)

    cur = lambda cbk: pl.BlockSpec((tt, c), lambda i: (i, cbk))
    prev = lambda cbk: pl.BlockSpec((HALO, c), lambda i: (jnp.maximum(i * hb - 1, 0), cbk))
    nxt = pl.BlockSpec((HALO, c), lambda i: (jnp.minimum((i + 1) * hb, last_blk), 0))
    return pl.pallas_call(
        kern, name="conv_bwd_taps", grid=(n_rows // tt,),
        in_specs=[cur(0), cur(1), prev(0), prev(1), cur(0), nxt, pl.BlockSpec(cw.shape, lambda i: (0, 0))],
        out_specs=[cur(0), cur(0), pl.BlockSpec((HALO, c), lambda i: (0, 0))],
        out_shape=[jax.ShapeDtypeStruct((n_rows, c), F32), jax.ShapeDtypeStruct((n_rows, c), F32),
                   jax.ShapeDtypeStruct((HALO, c), F32)],
        scratch_shapes=[pltpu.VMEM((tt + HALO, c), F32), pltpu.VMEM((tt + HALO, c), F32)],
        compiler_params=_params("arbitrary"),
    )(proj, proj, proj, proj, dc, dc, cw)


def _s5_params_fwd(lr, li, ldt, btr, bti):
    ns = lr.shape[1]

    def kern(lr_ref, li_ref, ldt_ref, btr_ref, bti_ref, ar_ref, ai_ref, bbr_ref, bbi_ref, pw_ref):
        lrv, liv = lr_ref[...], li_ref[...]
        dt = jnp.exp(ldt_ref[...])
        zr, zi = lrv * dt, liv * dt
        mag = jnp.exp(zr)
        ar, ai = mag * jnp.cos(zi), mag * jnp.sin(zi)
        den = lrv * lrv + liv * liv
        nr = ar - 1.0
        cr = (nr * lrv + ai * liv) / den
        ci = (ai * lrv - nr * liv) / den
        ar_ref[...] = ar
        ai_ref[...] = ai
        bbr_ref[...] = cr * btr_ref[...] - ci * bti_ref[...]
        bbi_ref[...] = cr * bti_ref[...] + ci * btr_ref[...]
        pr, pi = ar, ai
        for e in range(SUBLANE):
            pw_ref[pl.ds(e, 1), pl.ds(0, ns)] = pr
            pw_ref[pl.ds(e, 1), pl.ds(ns, ns)] = pi
            pr, pi = pr * ar - pi * ai, pr * ai + pi * ar

    h = btr.shape[0]
    shapes = [jax.ShapeDtypeStruct((1, ns), F32)] * 2 + [jax.ShapeDtypeStruct((h, ns), F32)] * 2
    shapes += [jax.ShapeDtypeStruct((SUBLANE, 2 * ns), F32)]
    return pl.pallas_call(kern, name="s5_params_fwd", out_shape=shapes)(lr, li, ldt, btr, bti)


def _s5_params_bwd(lr, li, ldt, btr, bti, dar, dai, dbbr, dbbi):
    def kern(lr_ref, li_ref, ldt_ref, btr_ref, bti_ref, dar_ref, dai_ref, dbr_ref, dbi_ref,
             dlr_ref, dli_ref, dldt_ref, dbtr_ref, dbti_ref):
        lrv, liv = lr_ref[...], li_ref[...]
        dt = jnp.exp(ldt_ref[...])
        zr, zi = lrv * dt, liv * dt
        mag = jnp.exp(zr)
        ar, ai = mag * jnp.cos(zi), mag * jnp.sin(zi)
        den = lrv * lrv + liv * liv
        nr = ar - 1.0
        cr = (nr * lrv + ai * liv) / den
        ci = (ai * lrv - nr * liv) / den
        dbr, dbi, br, bi = dbr_ref[...], dbi_ref[...], btr_ref[...], bti_ref[...]
        dbtr_ref[...] = cr * dbr + ci * dbi
        dbti_ref[...] = cr * dbi - ci * dbr
        dcr = _colsum(br * dbr + bi * dbi)
        dci = _colsum(br * dbi - bi * dbr)
        ir, ii = lrv / den, -liv / den
        dnr = ir * dcr + ii * dci
        dni = ir * dci - ii * dcr
        wr, wi = cr * ir - ci * ii, cr * ii + ci * ir
        dl1r = -(wr * dcr + wi * dci)
        dl1i = -(wr * dci - wi * dcr)
        dtr, dti = dar_ref[...] + dnr, dai_ref[...] + dni
        dzr = ar * dtr + ai * dti
        dzi = ar * dti - ai * dtr
        dlr_ref[...] = dl1r + dt * dzr
        dli_ref[...] = dl1i + dt * dzi
        dldt_ref[...] = (dzr * lrv + dzi * liv) * dt

    ns, h = lr.shape[1], btr.shape[0]
    shapes = [jax.ShapeDtypeStruct((1, ns), F32)] * 3 + [jax.ShapeDtypeStruct((h, ns), F32)] * 2
    return pl.pallas_call(kern, name="s5_params_bwd", out_shape=shapes)(lr, li, ldt, btr, bti, dar, dai, dbbr, dbbi)


def _scan(name, src, xs, tabs, seq, sb, reverse):
    n_rows, w = src.shape
    tt = _pick(seq, (SCAN_TILE,))
    nt, ng = n_rows // tt, tt // SUBLANE
    cw = _pick(sb, (SCAN_COLS,))
    with_x = xs is not None
    carry_row = 0 if reverse else SUBLANE - 1

    def kern(*refs):
        if with_x:
            s_ref, x_ref, l1, l2, l4, pw, o_ref, da_ref, car_ref, acc_ref = refs
        else:
            s_ref, l1, l2, l4, pw, o_ref, car_ref = refs
        i = pl.program_id(0)
        ti = (nt - 1 - i) if reverse else i
        restart = (((ti + 1) * tt) % seq == 0) if reverse else ((ti * tt) % seq == 0)

        @pl.when(restart)
        def _():
            car_ref[...] = jnp.zeros_like(car_ref)

        if with_x:
            @pl.when(i == 0)
            def _():
                acc_ref[...] = jnp.zeros_like(acc_ref)

        row = lax.broadcasted_iota(jnp.int32, (SUBLANE, cw), 0)

        def group(gi, carry):
            g = (ng - 1 - gi) if reverse else gi
            rows = pl.ds(pl.multiple_of(g * SUBLANE, SUBLANE), SUBLANE)
            for c0 in [b0 + o for b0 in range(0, w, 2 * sb) for o in range(0, sb, cw)]:
                cr, ci = pl.ds(c0, cw), pl.ds(c0 + sb, cw)
                xr, xi = s_ref[rows, cr], s_ref[rows, ci]
                for s, lt in ((1, l1), (2, l2), (4, l4)):
                    sh = (SUBLANE - s) if reverse else s
                    sr, si = pltpu.roll(xr, sh, 0), pltpu.roll(xi, sh, 0)
                    ar, ai = lt[:, cr], lt[:, ci]
                    xr, xi = xr + ar * sr - ai * si, xi + ar * si + ai * sr
                kr, ki = car_ref[pl.ds(carry_row, 1), cr], car_ref[pl.ds(carry_row, 1), ci]
                pr, pi = pw[:, cr], pw[:, ci]
                xr, xi = xr + pr * kr - pi * ki, xi + pr * ki + pi * kr
                o_ref[rows, cr] = xr
                o_ref[rows, ci] = xi
                car_ref[:, cr] = xr
                car_ref[:, ci] = xi
                if with_x:
                    nr = jnp.where(row == SUBLANE - 1, kr, pltpu.roll(xr, SUBLANE - 1, 0))
                    ni = jnp.where(row == SUBLANE - 1, ki, pltpu.roll(xi, SUBLANE - 1, 0))
                    pxr, pxi = x_ref[rows, cr], x_ref[rows, ci]
                    acc_ref[:, cr] += nr * pxr + ni * pxi
                    acc_ref[:, ci] += ni * pxr - nr * pxi
            return carry

        lax.fori_loop(0, ng, group, 0)

        if with_x:
            @pl.when(i == nt - 1)
            def _():
                da_ref[...] = _colsum(acc_ref[...])

    tile = pl.BlockSpec((tt, w), (lambda i: (nt - 1 - i, 0)) if reverse else (lambda i: (i, 0)))
    tab = pl.BlockSpec((SUBLANE, w), lambda i: (0, 0))
    ins = [src] + ([xs] if with_x else []) + list(tabs)
    in_specs = [tile] * (2 if with_x else 1) + [tab] * 4
    out_specs, out_shape = [tile], [jax.ShapeDtypeStruct((n_rows, w), F32)]
    scratch = [pltpu.VMEM((SUBLANE, w), F32)]
    if with_x:
        out_specs.append(pl.BlockSpec((1, w), lambda i: (0, 0)))
        out_shape.append(jax.ShapeDtypeStruct((1, w), F32))
        scratch.append(pltpu.VMEM((SUBLANE, w), F32))
    return pl.pallas_call(kern, name=name, grid=(nt,), in_specs=in_specs, out_specs=out_specs, out_shape=out_shape,
                          scratch_shapes=scratch, compiler_params=_params("arbitrary"))(*ins)


def _s5_post1(y0, proj, dskip):
    c = y0.shape[1]

    def body(ins, outs, accs):
        ypre = ins[0][...] + ins[2][...] * ins[1][...]
        outs[0][...] = ypre
        outs[1][...] = _gelu(ypre).astype(BF16)

    return _rowwise("s5_post1", body, y0.shape[0], [(y0, c, 0), (proj, c, 2)], [dskip], [(c, F32), (c, BF16)], [])


def _s5_post2(yg, q0, bg, og):
    c = yg.shape[1]

    def body(ins, outs, accs):
        ygv = ins[0][...].astype(F32)
        sg = ygv * _sigmoid(ins[1][...] + ins[2][...])
        outs[0][...] = (sg * _rms_r(sg) * ins[3][...]).astype(BF16)

    return _rowwise("s5_post2", body, yg.shape[0], [(yg, c, 0), (q0, c, 0)], [bg, og], [(c, BF16)], [])[0]


def _s5_post2_bwd(dmixed, yg, q0, bg, og):
    c = yg.shape[1]

    def body(ins, outs, accs):
        dsn, ygv = ins[0][...], ins[1][...].astype(F32)
        s = _sigmoid(ins[2][...] + ins[3][...])
        sg = ygv * s
        r = _rms_r(sg)
        accs[0][...] += _colsum(dsn * sg * r)
        dsg = _rms_bwd(sg, r, ins[4][...], dsn)
        dq = dsg * ygv * s * (1.0 - s)
        outs[0][...] = dq.astype(BF16)
        outs[1][...] = dsg * s
        accs[1][...] += _colsum(dq)

    return _rowwise("s5_post2_bwd", body, yg.shape[0], [(dmixed, c, 1), (yg, c, 0), (q0, c, 0)], [bg, og],
                    [(c, BF16), (c, F32)], [(1, c)] * 2)


def _s5_post1_bwd(dyg1, dyg2, ypre, proj, dskip):
    c = ypre.shape[1]

    def body(ins, outs, accs):
        dyp = (ins[0][...] + ins[1][...]) * _dgelu(ins[2][...])
        outs[0][...] = dyp.astype(BF16)
        outs[1][...] = dyp * ins[4][...]
        accs[0][...] += _colsum(dyp * ins[3][...])

    return _rowwise("s5_post1_bwd", body, ypre.shape[0], [(dyg1, c, 0), (dyg2, c, 0), (ypre, c, 0), (proj, c, 2)], [dskip],
                    [(c, BF16), (c, F32)], [(1, c)])


def _place():
    return lax.axis_index("x"), lax.axis_index("y"), lax.axis_index("c")


def _window(ref, axis, q, rows, cols):
    if axis == 0:
        return ref.at[pl.ds(pl.multiple_of(q * rows, SUBLANE), rows), :]
    return ref.at[:, pl.ds(pl.multiple_of(q * cols, LANE), cols)]


def _all_gather_chips(shards, axes):
    n = len(shards)

    def kern(*refs):
        ins, outs = refs[:n], refs[n:2 * n]
        send_sems, recv_sems, local_sems = refs[2 * n:]
        x, y, c = _place()
        local, remote = [], []
        for a in range(n):
            rows, cols = shards[a].shape
            mine = pltpu.make_async_copy(ins[a], _window(outs[a], axes[a], 2 * x + y, rows, cols), local_sems.at[a])
            mine.start()
            local.append(mine)
            for j, (fx, fy) in enumerate(CHIP_RELS):
                px, py = (1 - x) if fx else x, (1 - y) if fy else y
                cp = pltpu.make_async_remote_copy(
                    src_ref=ins[a], dst_ref=_window(outs[a], axes[a], 2 * x + y, rows, cols),
                    send_sem=send_sems.at[3 * a + j], recv_sem=recv_sems.at[3 * a + j],
                    device_id=(px, py, c), device_id_type=MESH)
                cp.start()
                remote.append(pltpu.make_async_remote_copy(
                    src_ref=ins[a], dst_ref=_window(outs[a], axes[a], 2 * px + py, rows, cols),
                    send_sem=send_sems.at[3 * a + j], recv_sem=recv_sems.at[3 * a + j],
                    device_id=(px, py, c), device_id_type=MESH))
        for cp in remote:
            cp.wait()
        for cp in local:
            cp.wait()

    out_shape = [jax.ShapeDtypeStruct((N_CHIPS * s.shape[0], s.shape[1]) if ax == 0 else (s.shape[0], N_CHIPS * s.shape[1]),
                                      s.dtype) for s, ax in zip(shards, axes)]
    return pl.pallas_call(
        kern, name="gather_weights", in_specs=[ANY] * n, out_specs=[ANY] * n, out_shape=out_shape,
        scratch_shapes=[pltpu.SemaphoreType.DMA((3 * n,)), pltpu.SemaphoreType.DMA((3 * n,)), pltpu.SemaphoreType.DMA((n,))],
    )(*shards)


def _exchange_grad_shards(grads, axes):
    n = len(grads)
    shard = [(g.shape[0] // N_CHIPS, g.shape[1]) if ax == 0 else (g.shape[0], g.shape[1] // N_CHIPS)
             for g, ax in zip(grads, axes)]

    def kern(*refs):
        ins, outs = refs[:n], refs[n:2 * n]
        send_sems, recv_sems, local_sems = refs[2 * n:]
        x, y, c = _place()
        copies = []
        for a in range(n):
            rows, cols = shard[a]
            mine = pltpu.make_async_copy(_window(ins[a], axes[a], 2 * x + y, rows, cols), outs[a].at[3], local_sems.at[a])
            mine.start()
            copies.append(mine)
            for j, (fx, fy) in enumerate(CHIP_RELS):
                px, py = (1 - x) if fx else x, (1 - y) if fy else y
                cp = pltpu.make_async_remote_copy(
                    src_ref=_window(ins[a], axes[a], 2 * px + py, rows, cols), dst_ref=outs[a].at[j],
                    send_sem=send_sems.at[3 * a + j], recv_sem=recv_sems.at[3 * a + j],
                    device_id=(px, py, c), device_id_type=MESH)
                cp.start()
                copies.append(cp)
        for cp in copies:
            cp.wait()

    out_shape = [jax.ShapeDtypeStruct((N_CHIPS,) + s, g.dtype) for s, g in zip(shard, grads)]
    return pl.pallas_call(
        kern, name="exchange_grad_shards", in_specs=[ANY] * n, out_specs=[ANY] * n, out_shape=out_shape,
        scratch_shapes=[pltpu.SemaphoreType.DMA((3 * n,)), pltpu.SemaphoreType.DMA((3 * n,)), pltpu.SemaphoreType.DMA((n,))],
    )(*grads)


def _swap_with_sibling(arrs):
    n = len(arrs)

    def kern(*refs):
        ins, outs = refs[:n], refs[n:2 * n]
        send_sems, recv_sems = refs[2 * n:]
        x, y, c = _place()
        copies = [pltpu.make_async_remote_copy(src_ref=ins[a], dst_ref=outs[a], send_sem=send_sems.at[a],
                                               recv_sem=recv_sems.at[a], device_id=(x, y, 1 - c), device_id_type=MESH)
                  for a in range(n)]
        for cp in copies:
            cp.start()
        for cp in copies:
            cp.wait()

    return pl.pallas_call(
        kern, name="swap_with_sibling", in_specs=[ANY] * n, out_specs=[ANY] * n,
        out_shape=[jax.ShapeDtypeStruct(a.shape, a.dtype) for a in arrs],
        scratch_shapes=[pltpu.SemaphoreType.DMA((n,)), pltpu.SemaphoreType.DMA((n,))],
    )(*arrs)


def _all_reduce_small(buf):
    rels = [(fx, fy, fc) for fx in (0, 1) for fy in (0, 1) for fc in (0, 1)][1:]
    n_dev = len(rels) + 1

    def kern(b_ref, o_ref, recv_ref, send_sems, recv_sems):
        x, y, c = _place()
        me = 4 * x + 2 * y + c
        copies = []
        for k, (fx, fy, fc) in enumerate(rels):
            peer = ((1 - x) if fx else x, (1 - y) if fy else y, (1 - c) if fc else c)
            cp = pltpu.make_async_remote_copy(src_ref=b_ref, dst_ref=recv_ref.at[me], send_sem=send_sems.at[k],
                                              recv_sem=recv_sems.at[k], device_id=peer, device_id_type=MESH)
            cp.start()
            copies.append((cp, peer))
        recv_ref[me] = b_ref[...]
        for k, (cp, (px, py, pc)) in enumerate(copies):
            cp.wait_send()
            pltpu.make_async_remote_copy(src_ref=b_ref, dst_ref=recv_ref.at[4 * px + 2 * py + pc], send_sem=send_sems.at[k],
                                         recv_sem=recv_sems.at[k], device_id=(px, py, pc), device_id_type=MESH).wait_recv()
        acc = recv_ref[0]
        for d in range(1, n_dev):
            acc = acc + recv_ref[d]
        o_ref[...] = acc

    vm = pl.BlockSpec(memory_space=pltpu.VMEM)
    return pl.pallas_call(
        kern, name="all_reduce_small", in_specs=[vm], out_specs=vm, out_shape=jax.ShapeDtypeStruct(buf.shape, F32),
        scratch_shapes=[pltpu.VMEM((n_dev,) + buf.shape, F32), pltpu.SemaphoreType.DMA((n_dev - 1,)),
                        pltpu.SemaphoreType.DMA((n_dev - 1,))],
        compiler_params=pltpu.CompilerParams(vmem_limit_bytes=VMEM_LIMIT_BYTES),
    )(buf)


def _sum_slots(name, parts):
    _, rows, cols = parts.shape
    tr = _pick(rows, (ROW_TILE, 128, 64, 32))

    def kern(p_ref, o_ref):
        o_ref[...] = ((p_ref[3].astype(F32) + p_ref[0].astype(F32)) + p_ref[1].astype(F32)) + p_ref[2].astype(F32)

    return pl.pallas_call(kern, name=name, grid=(rows // tr,),
                          in_specs=[pl.BlockSpec((N_CHIPS, tr, cols), lambda i: (0, i, 0))],
                          out_specs=pl.BlockSpec((tr, cols), lambda i: (i, 0)),
                          out_shape=jax.ShapeDtypeStruct((rows, cols), F32), compiler_params=_params("arbitrary"))(parts)


def _adamw_math(g, w, m, v):
    m2 = ADAM_B1 * m + (1.0 - ADAM_B1) * g
    v2 = ADAM_B2 * v + (1.0 - ADAM_B2) * (g * g)
    m_hat = m2 / (1.0 - ADAM_B1 ** ADAM_STEP)
    v_hat = v2 / (1.0 - ADAM_B2 ** ADAM_STEP)
    return -ADAM_LR * (m_hat / (jnp.sqrt(v_hat) + ADAM_EPS) + ADAM_WD * w), m2, v2


def _adamw(name, parts, w, m, v):
    rows, cols = w.shape
    tr = rows if rows * cols <= WHOLE_ELEMS else _pick(rows, (ROW_TILE, 352, 128, 64, 32, 8))
    n = len(parts)

    def kern(*refs):
        g = refs[0][:, pl.ds(0, cols)]
        for p in refs[1:n]:
            g = g + p[:, pl.ds(0, cols)]
        d, m2, v2 = _adamw_math(g, refs[n][...], refs[n + 1][...], refs[n + 2][...])
        refs[n + 3][...] = g
        refs[n + 4][...] = d
        refs[n + 5][...] = m2
        refs[n + 6][...] = v2

    spec = pl.BlockSpec((tr, cols), lambda i: (i, 0))
    return pl.pallas_call(kern, name=name, grid=(rows // tr,),
                          in_specs=[pl.BlockSpec((tr, p.shape[1]), lambda i: (i, 0)) for p in parts] + [spec] * 3,
                          out_specs=[spec] * 4, out_shape=[jax.ShapeDtypeStruct((rows, cols), F32)] * 4,
                          compiler_params=_params("arbitrary"))(*parts, w, m, v)


def _pack(arrs):
    parts, rows = [], []
    for a in arrs:
        r = _round_up(-(-a.size // LANE), SUBLANE)
        parts.append(jnp.pad(a.reshape(-1).astype(F32), (0, r * LANE - a.size)).reshape(r, LANE))
        rows.append(r)
    return jnp.concatenate(parts, axis=0), rows


def _unpack(buf, rows, shapes):
    out, r0 = [], 0
    for r, s in zip(rows, shapes):
        size = math.prod(s)
        out.append(buf[r0:r0 + r].reshape(-1)[:size].reshape(s))
        r0 += r
    return out


def kernel(x, norm_ffn1, ffn1_w1, ffn1_w3, ffn1_w2, norm_mix, w_in, conv_w, conv_b, conv_ln_g, conv_ln_b, conv_out_g, ssm_A_re, ssm_A_im, ssm_log_dt, ssm_B_re, ssm_B_im, ssm_C_re, ssm_C_im, ssm_D, ssm_glu_w, ssm_glu_b, ssm_out_g, w_out, norm_ffn2, ffn2_w1, ffn2_w3, ffn2_w2, norm_final, loss_target, m_norm_ffn1, m_ffn1_w1, m_ffn1_w3, m_ffn1_w2, m_norm_mix, m_w_in, m_conv_w, m_conv_b, m_conv_ln_g, m_conv_ln_b, m_conv_out_g, m_ssm_A_re, m_ssm_A_im, m_ssm_log_dt, m_ssm_B_re, m_ssm_B_im, m_ssm_C_re, m_ssm_C_im, m_ssm_D, m_ssm_glu_w, m_ssm_glu_b, m_ssm_out_g, m_w_out, m_norm_ffn2, m_ffn2_w1, m_ffn2_w3, m_ffn2_w2, m_norm_final, v_norm_ffn1, v_ffn1_w1, v_ffn1_w3, v_ffn1_w2, v_norm_mix, v_w_in, v_conv_w, v_conv_b, v_conv_ln_g, v_conv_ln_b, v_conv_out_g, v_ssm_A_re, v_ssm_A_im, v_ssm_log_dt, v_ssm_B_re, v_ssm_B_im, v_ssm_C_re, v_ssm_C_im, v_ssm_D, v_ssm_glu_w, v_ssm_glu_b, v_ssm_out_g, v_w_out, v_norm_ffn2, v_ffn2_w1, v_ffn2_w3, v_ffn2_w2, v_norm_final):
    given = dict(locals())
    wts = {n: given[n] for n in WEIGHTS}
    n_seq, seq, d = x.shape
    n_rows = n_seq * seq
    xf = x.reshape(n_rows, d)
    tgt = loss_target.reshape(n_rows, d)
    row = lambda a: a.reshape(1, -1)

    f = ffn1_w1.shape[-1]
    fp = _round_up(f, LANE)
    shards = []
    for n in BIG:
        s = wts[n][0].astype(BF16)
        if n.endswith('_w1') or n.endswith('_w3'):
            s = jnp.pad(s, ((0, 0), (0, fp - f)))
        elif n.endswith('_w2'):
            s = jnp.pad(s, ((0, fp - f), (0, 0)))
        shards.append(s)
    n_taps, c_shard = conv_w.shape[1], conv_w.shape[2]
    shards.append(jnp.pad(conv_w[0], ((0, HALO - n_taps), (0, 0))))
    gathered = _all_gather_chips(shards, [BIG_AXIS[n] for n in BIG] + [1])
    full = dict(zip(BIG, gathered))
    cw = gathered[-1]

    _, n_grp, n_state = ssm_A_re.shape
    grp = ssm_B_re.shape[-1]
    ns = n_grp * n_state
    c_ssm = n_grp * grp
    lr, li = ssm_A_re.reshape(1, ns), ssm_A_im.reshape(1, ns)
    ldt = jnp.repeat(ssm_log_dt.reshape(n_grp), n_state).reshape(1, ns)
    btr = ssm_B_re[0].transpose(2, 0, 1).reshape(grp, ns)
    bti = ssm_B_im[0].transpose(2, 0, 1).reshape(grp, ns)
    ctr = ssm_C_re[0].transpose(1, 0, 2).reshape(grp, ns)
    cti = ssm_C_im[0].transpose(1, 0, 2).reshape(grp, ns)
    _, _, bbr, bbi, pw = _s5_params_fwd(lr, li, ldt, btr, bti)
    nb = c_ssm // LANE
    sb, gpb = ns // nb, n_grp // nb
    diag = (jnp.arange(LANE)[:, None] // grp) == (jnp.arange(sb)[None, :] // n_state)

    def spread(t):
        return jnp.where(diag, jnp.tile(t.reshape(grp, nb, sb).transpose(1, 0, 2), (1, gpb, 1)), 0.0)

    def gather_diag(t):
        return (t * diag).reshape(nb, gpb, grp, sb).sum(1).transpose(1, 0, 2).reshape(grp, ns)

    def interleave(re, im):
        return jnp.stack([re.reshape(-1, nb, sb), im.reshape(-1, nb, sb)], axis=2).reshape(-1, 2 * ns)

    bdc = jnp.concatenate([spread(bbr), spread(bbi)], axis=2).astype(BF16)
    cdc = jnp.concatenate([spread(ctr).transpose(0, 2, 1), -spread(cti).transpose(0, 2, 1)], axis=1).astype(BF16)
    rowi = jnp.arange(SUBLANE)[:, None]
    pwf, pwc = interleave(pw[:, :ns], pw[:, ns:]), interleave(pw[:, :ns], -pw[:, ns:])
    tabs_f = [jnp.where(rowi >= s, pwf[s - 1][None, :], 0.0) for s in (1, 2, 4)] + [pwf]
    tabs_b = [jnp.where(rowi <= SUBLANE - 1 - s, pwc[s - 1][None, :], 0.0) for s in (1, 2, 4)] + [pwc[::-1]]
    c_conv = conv_b.shape[1]
    u_blk = 2 * c_conv // LANE

    x1, saved1 = _ffn_fwd("ffn1", xf, norm_ffn1, full['ffn1_w1'], full['ffn1_w3'], full['ffn1_w2'])
    h2 = _rms_fwd("mix_rms", x1, norm_mix)
    proj = _mm("mix_in", h2, full['w_in'], 1, 0, F32)
    assert c_conv == c_ssm and proj.shape[1] == 3 * c_conv
    cpre, an = _conv_fwd(proj, cw, conv_b, conv_ln_g, conv_ln_b, conv_out_g, seq)
    bu = _bmm_rows("s5_bu", proj, LANE, u_blk, bdc, False, 2 * sb)
    xs = _scan("s5_scan", bu, None, tabs_f, seq, sb, False)[0]
    y0 = _bmm_rows("s5_cx", xs, 2 * sb, 0, cdc, False, LANE)
    ypre, yg = _s5_post1(y0, proj, ssm_D)
    q0 = _mm("s5_gate", yg, full['ssm_glu_w'], 1, 0, F32)
    sn = _s5_post2(yg, q0, ssm_glu_b, ssm_out_g)
    wo = full['w_out']
    x2 = _mm("mix_out_a", an, wo[:c_conv], 1, 0, F32, addend=x1)
    x2 = _mm("mix_out_s", sn, wo[c_conv:], 1, 0, F32, addend=x2)
    x3, saved2 = _ffn_fwd("ffn2", x2, norm_ffn2, full['ffn2_w1'], full['ffn2_w3'], full['ffn2_w2'])
    dx3, loss_row, d_norm_final = _loss_head(x3, row(norm_final), tgt)

    g = {}
    dx2, g['norm_ffn2'], dw1, dw3, dw2 = _ffn_bwd("ffn2", x2, norm_ffn2, full['ffn2_w1'], full['ffn2_w3'],
                                                   full['ffn2_w2'], saved2, dx3)
    big_g = {'ffn2_w1': dw1, 'ffn2_w3': dw3, 'ffn2_w2': dw2}
    dmixed = _mm("mix_dmixed", dx2, wo, 1, 1, F32)
    big_g['w_out'] = jnp.concatenate([_mm("mix_dwo_a", an, dx2, 0, 0, BF16), _mm("mix_dwo_s", sn, dx2, 0, 0, BF16)], axis=0)
    dq, dyg1, g['ssm_out_g'], g['ssm_glu_b'] = _s5_post2_bwd(dmixed, yg, q0, ssm_glu_b, ssm_out_g)
    dyg2 = _mm("s5_dgate", dq, full['ssm_glu_w'], 1, 1, F32)
    big_g['ssm_glu_w'] = _mm("s5_dwg", yg, dq, 0, 0, BF16)
    dypre, du_skip, g['ssm_D'] = _s5_post1_bwd(dyg1, dyg2, ypre, proj, ssm_D)
    gx = _bmm_rows("s5_dx", dypre, LANE, 0, cdc, True, 2 * sb)
    dcdc = _bmm_wgrad("s5_dc", xs, 2 * sb, 0, dypre, LANE)
    lam, dabar = _scan("s5_scan_bwd", gx, xs, tabs_b, seq, sb, True)
    du = _bmm_rows("s5_du", lam, 2 * sb, 0, bdc, True, LANE, addend=du_skip)
    dbdc = _bmm_wgrad("s5_db", proj, LANE, u_blk, lam, 2 * sb)
    dabar = dabar.reshape(nb, 2, sb)
    dlr, dli, dldt, dbtr, dbti = _s5_params_bwd(lr, li, ldt, btr, bti, dabar[:, 0].reshape(1, ns), dabar[:, 1].reshape(1, ns),
                                                gather_diag(dbdc[:, :, :sb]), gather_diag(dbdc[:, :, sb:]))
    g['ssm_A_re'], g['ssm_A_im'] = dlr, dli
    g['ssm_log_dt'] = dldt.reshape(n_grp, n_state).sum(axis=1)
    g['ssm_B_re'] = dbtr.reshape(grp, n_grp, n_state).transpose(1, 2, 0)
    g['ssm_B_im'] = dbti.reshape(grp, n_grp, n_state).transpose(1, 2, 0)
    g['ssm_C_re'] = gather_diag(dcdc[:, :sb].transpose(0, 2, 1)).reshape(grp, n_grp, n_state).transpose(1, 0, 2)
    g['ssm_C_im'] = -gather_diag(dcdc[:, sb:].transpose(0, 2, 1)).reshape(grp, n_grp, n_state).transpose(1, 0, 2)
    dc, g['conv_out_g'], g['conv_ln_g'], g['conv_ln_b'], g['conv_b'] = _conv_bwd_rows(dmixed, cpre, conv_ln_g, conv_ln_b,
                                                                                    conv_out_g)
    dval, dgate, dcw = _conv_bwd_taps(proj, dc, cw, seq)
    dproj = jnp.concatenate([dval, dgate, du], axis=1)
    big_g['w_in'] = _mm("mix_dwin", h2, dproj, 0, 0, BF16)
    dh2 = _mm("mix_dh", dproj, full['w_in'], 1, 1, F32)
    dx1, g['norm_mix'] = _rms_bwd_res("mix_drms", x1, norm_mix, dh2, dx2)
    dx0, g['norm_ffn1'], dw1, dw3, dw2 = _ffn_bwd("ffn1", xf, norm_ffn1, full['ffn1_w1'], full['ffn1_w3'],
                                                   full['ffn1_w2'], saved1, dx1)
    big_g.update({'ffn1_w1': dw1, 'ffn1_w3': dw3, 'ffn1_w2': dw2})
    g['norm_final'] = d_norm_final
    g['conv_w'] = dcw[:n_taps]

    small_shapes = [(n_taps, c_conv) if n == 'conv_w' else wts[n].shape for n in SMALL]
    buf, buf_rows = _pack([g[n] for n in SMALL] + [loss_row])
    total = _unpack(_all_reduce_small(buf), buf_rows, small_shapes + [(1, LANE)])
    loss = total[-1][0, 0]
    grads = dict(zip(SMALL, total[:-1]))
    chip = 2 * lax.axis_index("x") + lax.axis_index("y")
    grads['conv_w'] = lax.dynamic_slice_in_dim(grads['conv_w'], chip * c_shard, c_shard, axis=1)[None]
    g_buf, rows_s = _pack([grads[n] for n in SMALL])
    packed = [_pack([given[p + n] for n in SMALL])[0] for p in ('', 'm_', 'v_')]
    _, d_buf, m_buf, v_buf = _adamw("adamw_small", [g_buf], *packed)
    shapes_s = [wts[n].shape for n in SMALL]
    deltas = dict(zip(SMALL, _unpack(d_buf, rows_s, shapes_s)))
    new_m = dict(zip(SMALL, _unpack(m_buf, rows_s, shapes_s)))
    new_v = dict(zip(SMALL, _unpack(v_buf, rows_s, shapes_s)))

    slots = _exchange_grad_shards([big_g[n] for n in BIG], [BIG_AXIS[n] for n in BIG])
    sums = [_sum_slots("sum_" + n, s) for n, s in zip(BIG, slots)]
    theirs = _swap_with_sibling(sums)
    for n, mine, other in zip(BIG, sums, theirs):
        grads[n], deltas[n], new_m[n], new_v[n] = (
            o[None] for o in _adamw("adamw_" + n, [mine, other], given[n][0], given['m_' + n][0], given['v_' + n][0]))

    return (loss, dx0.reshape(x.shape), *[grads[n] for n in WEIGHTS], *[deltas[n] for n in WEIGHTS],
            *[new_m[n] for n in WEIGHTS], *[new_v[n] for n in WEIGHTS])
```

```python
import functools
import math

import jax
import jax.numpy as jnp
from jax import lax
from jax.experimental import pallas as pl
from jax.experimental.pallas import tpu as pltpu

F32 = jnp.float32
BF16 = jnp.bfloat16
EPS = 1e-6
ADAM_LR, ADAM_B1, ADAM_B2, ADAM_EPS, ADAM_WD, ADAM_STEP = 0.001, 0.9, 0.999, 1e-08, 0.01, 10
MESH = pl.DeviceIdType.MESH
ANY = pl.BlockSpec(memory_space=pl.ANY)
LANE = 128
SUBLANE = 8
VMEM_LIMIT_BYTES = 56 << 20
ROW_TILE = 256
ROW_TILE_ELEMS = 256 * 1024
WHOLE_ELEMS = 512 * 1024
CONV_TILE = 128
CONV_SUB = 32
HALO = 32
SCAN_TILE = 128
SCAN_COLS = 512
N_CHIPS = 4
CHIP_RELS = ((1, 0), (0, 1), (1, 1))
NT = (((1,), (1,)), ((), ()))
GELU_K = math.sqrt(2.0 / math.pi)
GELU_C = 0.044715

WEIGHTS = ['norm_ffn1', 'ffn1_w1', 'ffn1_w3', 'ffn1_w2', 'norm_mix', 'w_in', 'conv_w', 'conv_b', 'conv_ln_g', 'conv_ln_b',
           'conv_out_g', 'ssm_A_re', 'ssm_A_im', 'ssm_log_dt', 'ssm_B_re', 'ssm_B_im', 'ssm_C_re', 'ssm_C_im', 'ssm_D',
           'ssm_glu_w', 'ssm_glu_b', 'ssm_out_g', 'w_out', 'norm_ffn2', 'ffn2_w1', 'ffn2_w3', 'ffn2_w2', 'norm_final']
BIG = ['ffn1_w1', 'ffn1_w3', 'ffn1_w2', 'w_in', 'ssm_glu_w', 'w_out', 'ffn2_w1', 'ffn2_w3', 'ffn2_w2']
BIG_AXIS = {'ffn1_w1': 1, 'ffn1_w3': 1, 'ffn1_w2': 0, 'w_in': 1, 'ssm_glu_w': 0, 'w_out': 0, 'ffn2_w1': 1, 'ffn2_w3': 1,
            'ffn2_w2': 0}
SMALL = [n for n in WEIGHTS if n not in BIG]


def _round_up(n, m):
    return -(-n // m) * m


def _pick(n, cands):
    for c in cands:
        if c <= n and n % c == 0:
            return c
    return n


def _params(*sem):
    return pltpu.CompilerParams(dimension_semantics=sem, vmem_limit_bytes=VMEM_LIMIT_BYTES)


def _rms_r(x):
    return lax.rsqrt(jnp.mean(x * x, axis=-1, keepdims=True) + EPS)


def _rms_bwd(x, r, g, dy):
    dyg = dy * g
    return r * dyg - x * (r * r * r) * jnp.mean(x * dyg, axis=-1, keepdims=True)


def _sigmoid(x):
    return jax.nn.sigmoid(x)


def _dsilu(a, s):
    return s * (1.0 + a * (1.0 - s))


def _gelu(x):
    return 0.5 * x * (1.0 + jnp.tanh(GELU_K * (x + GELU_C * x * x * x)))


def _dgelu(x):
    t = jnp.tanh(GELU_K * (x + GELU_C * x * x * x))
    return 0.5 * (1.0 + t) + 0.5 * x * (1.0 - t * t) * GELU_K * (1.0 + 3.0 * GELU_C * x * x)


def _colsum(v):
    return jnp.sum(v, axis=0, keepdims=True)


def _rowwise(name, body, n_rows, row_ins, par_ins, row_outs, acc_outs, after=()):
    widest = max([w for (_, w, _) in row_ins] + [w for (w, _) in row_outs])
    tt = _pick(n_rows, [t for t in (256, 128, 64, 32, 16, 8) if t * widest <= ROW_TILE_ELEMS])
    in_specs = [pl.BlockSpec((tt, w), lambda i, cb=cb: (i, cb)) for (_, w, cb) in row_ins]
    in_specs += [pl.BlockSpec(p.shape, lambda i: (0, 0)) for p in par_ins] + [ANY] * len(after)
    out_specs = [pl.BlockSpec((tt, w), lambda i: (i, 0)) for (w, _) in row_outs]
    out_specs += [pl.BlockSpec((r, w), lambda i: (0, 0)) for (r, w) in acc_outs]
    out_shape = [jax.ShapeDtypeStruct((n_rows, w), dt) for (w, dt) in row_outs]
    out_shape += [jax.ShapeDtypeStruct((r, w), F32) for (r, w) in acc_outs]
    n_in, n_ro = len(row_ins) + len(par_ins), len(row_outs)
    o0 = n_in + len(after)

    def kern(*refs):
        accs = refs[o0 + n_ro:]
        if accs:
            @pl.when(pl.program_id(0) == 0)
            def _():
                for a in accs:
                    a[...] = jnp.zeros_like(a)
        body(refs[:n_in], refs[o0:o0 + n_ro], accs)

    return pl.pallas_call(kern, name=name, grid=(n_rows // tt,), in_specs=in_specs, out_specs=out_specs, out_shape=out_shape,
                          compiler_params=_params("arbitrary"))(*[a for a, _, _ in row_ins], *par_ins, *after)


def _mm(name, a, b, ca, cb, out_dtype=F32, addend=None, alpha=1.0, a_cols=None, after=()):
    a_start, a_width = a_cols if a_cols else (0, a.shape[1])
    m, k = (a.shape[0], a_width) if ca == 1 else (a_width, a.shape[0])
    n = b.shape[1 - cb]
    assert b.shape[cb] == k, (name, a.shape, b.shape)
    tm = _pick(m, (1024, 512, 256, 128))
    tn = _pick(n, (1024, 768, 512, 384, 256, 128))
    tk = _pick(k, (1024, 768, 512, 256, 128))
    nk = k // tk
    if ca == 1:
        assert a_start % tk == 0
        a_spec = pl.BlockSpec((tm, tk), lambda i, j, kk: (i, kk + a_start // tk))
    else:
        assert a_start % tm == 0
        a_spec = pl.BlockSpec((tk, tm), lambda i, j, kk: (kk, i + a_start // tm))
    b_spec = pl.BlockSpec((tk, tn), lambda i, j, kk: (kk, j)) if cb == 0 else pl.BlockSpec((tn, tk), lambda i, j, kk: (j, kk))
    o_spec = pl.BlockSpec((tm, tn), lambda i, j, kk: (i, j))
    ins, in_specs = [a, b], [a_spec, b_spec]
    if addend is not None:
        ins.append(addend)
        in_specs.append(o_spec)
    ins += list(after)
    in_specs += [ANY] * len(after)
    dims = (((ca,), (cb,)), ((), ()))

    def finish(refs, r):
        if alpha != 1.0:
            r = r * alpha
        if addend is not None:
            r = r + refs[2][...].astype(F32)
        return r.astype(out_dtype)

    def kern_one(*refs):
        refs[-1][...] = finish(refs, lax.dot_general(refs[0][...].astype(BF16), refs[1][...].astype(BF16), dims,
                                                     preferred_element_type=F32))

    def kern_acc(*refs):
        o_ref, acc_ref = refs[-2], refs[-1]
        kk = pl.program_id(2)

        @pl.when(kk == 0)
        def _():
            acc_ref[...] = jnp.zeros_like(acc_ref)

        acc_ref[...] += lax.dot_general(refs[0][...].astype(BF16), refs[1][...].astype(BF16), dims,
                                        preferred_element_type=F32)

        @pl.when(kk == nk - 1)
        def _():
            o_ref[...] = finish(refs, acc_ref[...])

    return pl.pallas_call(kern_one if nk == 1 else kern_acc, name=name, grid=(m // tm, n // tn, nk), in_specs=in_specs,
                          out_specs=o_spec, out_shape=jax.ShapeDtypeStruct((m, n), out_dtype),
                          scratch_shapes=[] if nk == 1 else [pltpu.VMEM((tm, tn), F32)],
                          compiler_params=_params("arbitrary", "arbitrary", "arbitrary"))(*ins)


def _bmm_rows(name, a, a_blk, a_off, w, trans_w, out_blk, addend=None):
    t, nb = a.shape[0], w.shape[0]
    tm = _pick(t, (1024, 512, 256, 128))
    dims = NT if trans_w else (((1,), (0,)), ((), ()))

    def kern(*refs):
        r = lax.dot_general(refs[0][...].astype(BF16), refs[1][...], dims, preferred_element_type=F32)
        if addend is not None:
            r = r + refs[2][...]
        refs[-1][...] = r

    o_spec = pl.BlockSpec((tm, out_blk), lambda i, j: (i, j))
    in_specs = [pl.BlockSpec((tm, a_blk), lambda i, j: (i, a_off + j)),
                pl.BlockSpec((None,) + w.shape[1:], lambda i, j: (j, 0, 0))]
    ins = [a, w]
    if addend is not None:
        in_specs.append(o_spec)
        ins.append(addend)
    return pl.pallas_call(kern, name=name, grid=(t // tm, nb), in_specs=in_specs, out_specs=o_spec,
                          out_shape=jax.ShapeDtypeStruct((t, nb * out_blk), F32),
                          compiler_params=_params("arbitrary", "arbitrary"))(*ins)


def _bmm_wgrad(name, a, a_blk, a_off, b, b_blk):
    t, nb = a.shape[0], b.shape[1] // b_blk
    tk = _pick(t, (1024, 512, 256, 128))
    nk = t // tk

    def kern(a_ref, b_ref, o_ref, acc_ref):
        kk = pl.program_id(1)
        part = lax.dot_general(a_ref[...].astype(BF16), b_ref[...].astype(BF16), (((0,), (0,)), ((), ())),
                               preferred_element_type=F32)

        @pl.when(kk == 0)
        def _():
            acc_ref[...] = part

        @pl.when(kk > 0)
        def _():
            acc_ref[...] += part

        @pl.when(kk == nk - 1)
        def _():
            o_ref[...] = acc_ref[...]

    return pl.pallas_call(kern, name=name, grid=(nb, nk),
                          in_specs=[pl.BlockSpec((tk, a_blk), lambda j, kk: (kk, a_off + j)),
                                    pl.BlockSpec((tk, b_blk), lambda j, kk: (kk, j))],
                          out_specs=pl.BlockSpec((None, a_blk, b_blk), lambda j, kk: (j, 0, 0)),
                          out_shape=jax.ShapeDtypeStruct((nb, a_blk, b_blk), F32),
                          scratch_shapes=[pltpu.VMEM((a_blk, b_blk), F32)],
                          compiler_params=_params("arbitrary", "arbitrary"))(a, b)


def _rms_fwd(name, x, g):
    def body(ins, outs, accs):
        xv = ins[0][...]
        outs[0][...] = (xv * _rms_r(xv) * ins[1][...]).astype(BF16)

    return _rowwise(name, body, x.shape[0], [(x, x.shape[1], 0)], [g], [(x.shape[1], BF16)], [])[0]


def _rms_bwd_res(name, x, g, dh, dres):
    d = x.shape[1]

    def body(ins, outs, accs):
        xv, dhv, gv = ins[0][...], ins[1][...], ins[3][...]
        r = _rms_r(xv)
        outs[0][...] = ins[2][...] + _rms_bwd(xv, r, gv, dhv)
        accs[0][...] += _colsum(dhv * xv * r)

    return _rowwise(name, body, x.shape[0], [(x, d, 0), (dh, d, 0), (dres, d, 0)], [g], [(d, F32)], [(1, d)])


def _ffn_up(name, h, w1, w3):
    t, d = h.shape
    ff = w1.shape[1]
    tm, tn = _pick(t, (1024, 512, 256, 128)), _pick(ff, (1024, 768, 512, 256, 128))

    def kern(h_ref, w1_ref, w3_ref, a_ref, b_ref, z_ref):
        hv = h_ref[...]
        a = jnp.dot(hv, w1_ref[...], preferred_element_type=F32)
        b = jnp.dot(hv, w3_ref[...], preferred_element_type=F32)
        a_ref[...] = a.astype(BF16)
        b_ref[...] = b.astype(BF16)
        z_ref[...] = (a * _sigmoid(a) * b).astype(BF16)

    w_spec = pl.BlockSpec((d, tn), lambda i, j: (0, j))
    o_spec = pl.BlockSpec((tm, tn), lambda i, j: (i, j))
    return pl.pallas_call(kern, name=name, grid=(t // tm, ff // tn),
                          in_specs=[pl.BlockSpec((tm, d), lambda i, j: (i, 0)), w_spec, w_spec], out_specs=[o_spec] * 3,
                          out_shape=[jax.ShapeDtypeStruct((t, ff), BF16)] * 3,
                          compiler_params=_params("arbitrary", "arbitrary"))(h, w1, w3)


def _ffn_dglu(name, dxo, w2, a, b, after=()):
    t, d = dxo.shape
    ff = w2.shape[0]
    tm, tn = _pick(t, (1024, 512, 256, 128)), _pick(ff, (1024, 768, 512, 256, 128))

    def kern(dx_ref, w2_ref, a_ref, b_ref, *rest):
        da_ref, db_ref = rest[-2:]
        dz = lax.dot_general(dx_ref[...].astype(BF16), w2_ref[...], NT, preferred_element_type=F32) * 0.5
        av, bv = a_ref[...].astype(F32), b_ref[...].astype(F32)
        s = _sigmoid(av)
        da_ref[...] = (dz * bv * _dsilu(av, s)).astype(BF16)
        db_ref[...] = (dz * av * s).astype(BF16)

    o_spec = pl.BlockSpec((tm, tn), lambda i, j: (i, j))
    return pl.pallas_call(kern, name=name, grid=(t // tm, ff // tn),
                          in_specs=[pl.BlockSpec((tm, d), lambda i, j: (i, 0)), pl.BlockSpec((tn, d), lambda i, j: (j, 0)),
                                    o_spec, o_spec] + [ANY] * len(after),
                          out_specs=[o_spec] * 2, out_shape=[jax.ShapeDtypeStruct((t, ff), BF16)] * 2,
                          compiler_params=_params("arbitrary", "arbitrary"))(dxo, w2, a, b, *after)


def _ffn_dh(name, da, db, w1, w3, x, g, dres, after=()):
    t, d = x.shape
    ff = da.shape[1]
    tm, tk = _pick(t, (512, 256, 128)), _pick(ff, (1024, 768, 512, 256, 128))
    nk = ff // tk

    def kern(da_ref, db_ref, w1_ref, w3_ref, x_ref, g_ref, dres_ref, *rest):
        dx_ref, dg_ref, acc_ref = rest[-3:]
        i, kk = pl.program_id(0), pl.program_id(1)

        @pl.when(jnp.logical_and(i == 0, kk == 0))
        def _():
            dg_ref[...] = jnp.zeros_like(dg_ref)

        part = (lax.dot_general(da_ref[...], w1_ref[...], NT, preferred_element_type=F32)
                + lax.dot_general(db_ref[...], w3_ref[...], NT, preferred_element_type=F32))

        @pl.when(kk == 0)
        def _():
            acc_ref[...] = part

        @pl.when(kk > 0)
        def _():
            acc_ref[...] += part

        @pl.when(kk == nk - 1)
        def _():
            dh, xv = acc_ref[...], x_ref[...]
            r = _rms_r(xv)
            dx_ref[...] = dres_ref[...] + _rms_bwd(xv, r, g_ref[...], dh)
            dg_ref[...] += _colsum(dh * xv * r)

    act = pl.BlockSpec((tm, tk), lambda i, kk: (i, kk))
    wgt = pl.BlockSpec((d, tk), lambda i, kk: (0, kk))
    rows = pl.BlockSpec((tm, d), lambda i, kk: (i, 0))
    gain = pl.BlockSpec((1, d), lambda i, kk: (0, 0))
    return pl.pallas_call(kern, name=name, grid=(t // tm, nk),
                          in_specs=[act, act, wgt, wgt, rows, gain, rows] + [ANY] * len(after), out_specs=[rows, gain],
                          out_shape=[jax.ShapeDtypeStruct((t, d), F32), jax.ShapeDtypeStruct((1, d), F32)],
                          scratch_shapes=[pltpu.VMEM((tm, d), F32)],
                          compiler_params=_params("arbitrary", "arbitrary"))(da, db, w1, w3, x, g, dres, *after)


def _loss_head(x3, gf, tgt):
    d = x3.shape[1]

    def body(ins, outs, accs):
        xv, tv, gv = ins[0][...], ins[1][...], ins[2][...]
        r = _rms_r(xv)
        e = xv * r * gv - tv
        sq = jnp.sum(jnp.sum(e * e, axis=-1, keepdims=True), axis=0, keepdims=True)
        accs[0][...] += jnp.broadcast_to(sq * (0.5 / d), (1, LANE))
        dy = e * (1.0 / d)
        outs[0][...] = _rms_bwd(xv, r, gv, dy)
        accs[1][...] += _colsum(dy * xv * r)

    return _rowwise("loss_head", body, x3.shape[0], [(x3, d, 0), (tgt, d, 0)], [gf], [(d, F32)], [(1, LANE), (1, d)])


def _ffn_fwd(tag, x, g, w1, w3, w2):
    h = _rms_fwd(tag + "_rms", x, g)
    a, b, z = _ffn_up(tag + "_up", h, w1, w3)
    return _mm(tag + "_down", z, w2, 1, 0, F32, addend=x, alpha=0.5), (h, a, b, z)


def _ffn_bwd(tag, x, g, w1, w3, w2, saved, dxo):
    h, a, b, z = saved
    dw2 = _mm(tag + "_dw2", z, dxo, 0, 0, BF16, alpha=0.5)
    s2 = _exchange_start(tag + "_w2_send", False, [dw2], [0])
    da, db = _ffn_dglu(tag + "_dglu", dxo, w2, a, b, after=[s2[3]])
    dw1 = _mm(tag + "_dw1", h, da, 0, 0, BF16)
    s1 = _exchange_start(tag + "_w1_send", False, [dw1], [1])
    dw3 = _mm(tag + "_dw3", h, db, 0, 0, BF16, after=[s1[3]])
    s3 = _exchange_start(tag + "_w3_send", False, [dw3], [1])
    dx, dg = _ffn_dh(tag + "_dh", da, db, w1, w3, x, g, dxo, after=[s3[3]])
    return dx, dg, {tag + "_w1": (s1, [1]), tag + "_w3": (s3, [1]), tag + "_w2": (s2, [0])}


def _conv_fwd(proj, cw, cb, lng, lnb, og, seq):
    n_rows, c = proj.shape[0], cb.shape[1]
    kw = HALO - 1
    tt = _pick(seq, (CONV_TILE,))
    hb = tt // HALO

    def kern(v_ref, g_ref, vp_ref, gp_ref, w_ref, cb_ref, lg_ref, lb_ref, og_ref, c_ref, an_ref, ext_ref):
        first = (pl.program_id(0) * tt) % seq == 0
        ext_ref[pl.ds(HALO, tt), :] = v_ref[...] * _sigmoid(g_ref[...])
        ext_ref[pl.ds(0, HALO), :] = vp_ref[...] * _sigmoid(gp_ref[...]) * jnp.where(first, 0.0, 1.0)
        for r0 in range(0, tt, CONV_SUB):
            rows = min(CONV_SUB, tt - r0)
            acc = jnp.zeros((rows, c), F32)
            for k in range(kw):
                acc = acc + w_ref[pl.ds(k, 1), :] * ext_ref[pl.ds(r0 + HALO - (kw - 1) + k, rows), :]
            c_ref[pl.ds(r0, rows), :] = acc + cb_ref[...]
        cv = c_ref[...]
        mu = jnp.mean(cv, axis=-1, keepdims=True)
        xc = cv - mu
        rstd = lax.rsqrt(jnp.mean(xc * xc, axis=-1, keepdims=True) + EPS)
        lv = xc * rstd * lg_ref[...] + lb_ref[...]
        sl = lv * _sigmoid(lv)
        an_ref[...] = (sl * _rms_r(sl) * og_ref[...]).astype(BF16)

    cur = lambda cbk: pl.BlockSpec((tt, c), lambda i: (i, cbk))
    prev = lambda cbk: pl.BlockSpec((HALO, c), lambda i: (jnp.maximum(i * hb - 1, 0), cbk))
    par = lambda p: pl.BlockSpec(p.shape, lambda i: (0, 0))
    return pl.pallas_call(
        kern, name="conv_fwd", grid=(n_rows // tt,),
        in_specs=[cur(0), cur(1), prev(0), prev(1), par(cw), par(cb), par(lng), par(lnb), par(og)],
        out_specs=[pl.BlockSpec((tt, c), lambda i: (i, 0))] * 2,
        out_shape=[jax.ShapeDtypeStruct((n_rows, c), F32), jax.ShapeDtypeStruct((n_rows, c), BF16)],
        scratch_shapes=[pltpu.VMEM((tt + HALO, c), F32)], compiler_params=_params("arbitrary"),
    )(proj, proj, proj, proj, cw, cb, lng, lnb, og)


def _conv_bwd_rows(dmixed, cpre, lng, lnb, og):
    c = cpre.shape[1]

    def body(ins, outs, accs):
        dan, cv, lg, lb, ogv = ins[0][...], ins[1][...], ins[2][...], ins[3][...], ins[4][...]
        mu = jnp.mean(cv, axis=-1, keepdims=True)
        xc = cv - mu
        rstd = lax.rsqrt(jnp.mean(xc * xc, axis=-1, keepdims=True) + EPS)
        xh = xc * rstd
        lv = xh * lg + lb
        s = _sigmoid(lv)
        sl = lv * s
        r2 = _rms_r(sl)
        accs[0][...] += _colsum(dan * sl * r2)
        dl = _rms_bwd(sl, r2, ogv, dan) * _dsilu(lv, s)
        accs[1][...] += _colsum(dl * xh)
        accs[2][...] += _colsum(dl)
        dxh = dl * lg
        dc = rstd * (dxh - jnp.mean(dxh, axis=-1, keepdims=True) - xh * jnp.mean(dxh * xh, axis=-1, keepdims=True))
        outs[0][...] = dc
        accs[3][...] += _colsum(dc)

    return _rowwise("conv_bwd_rows", body, cpre.shape[0], [(dmixed, c, 0), (cpre, c, 0)], [lng, lnb, og], [(c, F32)],
                    [(1, c)] * 4)


def _conv_bwd_taps(proj, dc, cw, seq):
    n_rows, c = dc.shape
    kw = HALO - 1
    tt = _pick(seq, (CONV_TILE,))
    hb = tt // HALO
    last_blk = n_rows // HALO - 1

    def kern(v_ref, g_ref, vp_ref, gp_ref, dc_ref, dn_ref, w_ref, dv_ref, dg_ref, dw_ref, exta_ref, extd_ref):
        i = pl.program_id(0)
        first = (i * tt) % seq == 0
        last = ((i + 1) * tt) % seq == 0

        @pl.when(i == 0)
        def _():
            dw_ref[...] = jnp.zeros_like(dw_ref)

        sg = _sigmoid(g_ref[...])
        exta_ref[pl.ds(HALO, tt), :] = v_ref[...] * sg
        exta_ref[pl.ds(0, HALO), :] = vp_ref[...] * _sigmoid(gp_ref[...]) * jnp.where(first, 0.0, 1.0)
        dcv = dc_ref[...]
        extd_ref[pl.ds(0, tt), :] = dcv
        extd_ref[pl.ds(tt, HALO), :] = dn_ref[...] * jnp.where(last, 0.0, 1.0)
        for k in range(kw):
            dw_ref[pl.ds(k, 1), :] += _colsum(exta_ref[pl.ds(HALO - (kw - 1) + k, tt), :] * dcv)
        for r0 in range(0, tt, CONV_SUB):
            rows = min(CONV_SUB, tt - r0)
            acc = jnp.zeros((rows, c), F32)
            for k in range(kw):
                acc = acc + w_ref[pl.ds(k, 1), :] * extd_ref[pl.ds(r0 + (kw - 1) - k, rows), :]
            dv_ref[pl.ds(r0, rows), :] = acc
        da = dv_ref[...]
        dv_ref[...] = da * sg
        dg_ref[...] = da * v_ref[...] * sg * (1.0 - sg)

    cur = lambda cbk: pl.BlockSpec((tt, c), lambda i: (i, cbk))
    prev = lambda cbk: pl.BlockSpec((HALO, c), lambda i: (jnp.maximum(i * hb - 1, 0), cbk))
    nxt = pl.BlockSpec((HALO, c), lambda i: (jnp.minimum((i + 1) * hb, last_blk), 0))
    return pl.pallas_call(
        kern, name="conv_bwd_taps", grid=(n_rows // tt,),
        in_specs=[cur(0), cur(1), prev(0), prev(1), cur(0), nxt, pl.BlockSpec(cw.shape, lambda i: (0, 0))],
        out_specs=[cur(0), cur(0), pl.BlockSpec((HALO, c), lambda i: (0, 0))],
        out_shape=[jax.ShapeDtypeStruct((n_rows, c), F32), jax.ShapeDtypeStruct((n_rows, c), F32),
                   jax.ShapeDtypeStruct((HALO, c), F32)],
        scratch_shapes=[pltpu.VMEM((tt + HALO, c), F32), pltpu.VMEM((tt + HALO, c), F32)],
        compiler_params=_params("arbitrary"),
    )(proj, proj, proj, proj, dc, dc, cw)


def _s5_params_fwd(lr, li, ldt, btr, bti):
    ns = lr.shape[1]

    def kern(lr_ref, li_ref, ldt_ref, btr_ref, bti_ref, ar_ref, ai_ref, bbr_ref, bbi_ref, pw_ref):
        lrv, liv = lr_ref[...], li_ref[...]
        dt = jnp.exp(ldt_ref[...])
        zr, zi = lrv * dt, liv * dt
        mag = jnp.exp(zr)
        ar, ai = mag * jnp.cos(zi), mag * jnp.sin(zi)
        den = lrv * lrv + liv * liv
        nr = ar - 1.0
        cr = (nr * lrv + ai * liv) / den
        ci = (ai * lrv - nr * liv) / den
        ar_ref[...] = ar
        ai_ref[...] = ai
        bbr_ref[...] = cr * btr_ref[...] - ci * bti_ref[...]
        bbi_ref[...] = cr * bti_ref[...] + ci * btr_ref[...]
        pr, pi = ar, ai
        for e in range(SUBLANE):
            pw_ref[pl.ds(e, 1), pl.ds(0, ns)] = pr
            pw_ref[pl.ds(e, 1), pl.ds(ns, ns)] = pi
            pr, pi = pr * ar - pi * ai, pr * ai + pi * ar

    h = btr.shape[0]
    shapes = [jax.ShapeDtypeStruct((1, ns), F32)] * 2 + [jax.ShapeDtypeStruct((h, ns), F32)] * 2
    shapes += [jax.ShapeDtypeStruct((SUBLANE, 2 * ns), F32)]
    return pl.pallas_call(kern, name="s5_params_fwd", out_shape=shapes)(lr, li, ldt, btr, bti)


def _s5_params_bwd(lr, li, ldt, btr, bti, dar, dai, dbbr, dbbi):
    def kern(lr_ref, li_ref, ldt_ref, btr_ref, bti_ref, dar_ref, dai_ref, dbr_ref, dbi_ref,
             dlr_ref, dli_ref, dldt_ref, dbtr_ref, dbti_ref):
        lrv, liv = lr_ref[...], li_ref[...]
        dt = jnp.exp(ldt_ref[...])
        zr, zi = lrv * dt, liv * dt
        mag = jnp.exp(zr)
        ar, ai = mag * jnp.cos(zi), mag * jnp.sin(zi)
        den = lrv * lrv + liv * liv
        nr = ar - 1.0
        cr = (nr * lrv + ai * liv) / den
        ci = (ai * lrv - nr * liv) / den
        dbr, dbi, br, bi = dbr_ref[...], dbi_ref[...], btr_ref[...], bti_ref[...]
        dbtr_ref[...] = cr * dbr + ci * dbi
        dbti_ref[...] = cr * dbi - ci * dbr
        dcr = _colsum(br * dbr + bi * dbi)
        dci = _colsum(br * dbi - bi * dbr)
        ir, ii = lrv / den, -liv / den
        dnr = ir * dcr + ii * dci
        dni = ir * dci - ii * dcr
        wr, wi = cr * ir - ci * ii, cr * ii + ci * ir
        dl1r = -(wr * dcr + wi * dci)
        dl1i = -(wr * dci - wi * dcr)
        dtr, dti = dar_ref[...] + dnr, dai_ref[...] + dni
        dzr = ar * dtr + ai * dti
        dzi = ar * dti - ai * dtr
        dlr_ref[...] = dl1r + dt * dzr
        dli_ref[...] = dl1i + dt * dzi
        dldt_ref[...] = (dzr * lrv + dzi * liv) * dt

    ns, h = lr.shape[1], btr.shape[0]
    shapes = [jax.ShapeDtypeStruct((1, ns), F32)] * 3 + [jax.ShapeDtypeStruct((h, ns), F32)] * 2
    return pl.pallas_call(kern, name="s5_params_bwd", out_shape=shapes)(lr, li, ldt, btr, bti, dar, dai, dbbr, dbbi)


def _scan(name, src, xs, tabs, seq, sb, reverse):
    n_rows, w = src.shape
    tt = _pick(seq, (SCAN_TILE,))
    nt, ng = n_rows // tt, tt // SUBLANE
    cw = _pick(sb, (SCAN_COLS,))
    with_x = xs is not None
    carry_row = 0 if reverse else SUBLANE - 1

    def kern(*refs):
        if with_x:
            s_ref, x_ref, l1, l2, l4, pw, o_ref, da_ref, car_ref, acc_ref = refs
        else:
            s_ref, l1, l2, l4, pw, o_ref, car_ref = refs
        i = pl.program_id(0)
        ti = (nt - 1 - i) if reverse else i
        restart = (((ti + 1) * tt) % seq == 0) if reverse else ((ti * tt) % seq == 0)

        @pl.when(restart)
        def _():
            car_ref[...] = jnp.zeros_like(car_ref)

        if with_x:
            @pl.when(i == 0)
            def _():
                acc_ref[...] = jnp.zeros_like(acc_ref)

        row = lax.broadcasted_iota(jnp.int32, (SUBLANE, cw), 0)

        def group(gi, carry):
            g = (ng - 1 - gi) if reverse else gi
            rows = pl.ds(pl.multiple_of(g * SUBLANE, SUBLANE), SUBLANE)
            for c0 in [b0 + o for b0 in range(0, w, 2 * sb) for o in range(0, sb, cw)]:
                cr, ci = pl.ds(c0, cw), pl.ds(c0 + sb, cw)
                xr, xi = s_ref[rows, cr], s_ref[rows, ci]
                for s, lt in ((1, l1), (2, l2), (4, l4)):
                    sh = (SUBLANE - s) if reverse else s
                    sr, si = pltpu.roll(xr, sh, 0), pltpu.roll(xi, sh, 0)
                    ar, ai = lt[:, cr], lt[:, ci]
                    xr, xi = xr + ar * sr - ai * si, xi + ar * si + ai * sr
                kr, ki = car_ref[pl.ds(carry_row, 1), cr], car_ref[pl.ds(carry_row, 1), ci]
                pr, pi = pw[:, cr], pw[:, ci]
                xr, xi = xr + pr * kr - pi * ki, xi + pr * ki + pi * kr
                o_ref[rows, cr] = xr
                o_ref[rows, ci] = xi
                car_ref[:, cr] = xr
                car_ref[:, ci] = xi
                if with_x:
                    nr = jnp.where(row == SUBLANE - 1, kr, pltpu.roll(xr, SUBLANE - 1, 0))
                    ni = jnp.where(row == SUBLANE - 1, ki, pltpu.roll(xi, SUBLANE - 1, 0))
                    pxr, pxi = x_ref[rows, cr], x_ref[rows, ci]
                    acc_ref[:, cr] += nr * pxr + ni * pxi
                    acc_ref[:, ci] += ni * pxr - nr * pxi
            return carry

        lax.fori_loop(0, ng, group, 0)

        if with_x:
            @pl.when(i == nt - 1)
            def _():
                da_ref[...] = _colsum(acc_ref[...])

    tile = pl.BlockSpec((tt, w), (lambda i: (nt - 1 - i, 0)) if reverse else (lambda i: (i, 0)))
    tab = pl.BlockSpec((SUBLANE, w), lambda i: (0, 0))
    ins = [src] + ([xs] if with_x else []) + list(tabs)
    in_specs = [tile] * (2 if with_x else 1) + [tab] * 4
    out_specs, out_shape = [tile], [jax.ShapeDtypeStruct((n_rows, w), F32)]
    scratch = [pltpu.VMEM((SUBLANE, w), F32)]
    if with_x:
        out_specs.append(pl.BlockSpec((1, w), lambda i: (0, 0)))
        out_shape.append(jax.ShapeDtypeStruct((1, w), F32))
        scratch.append(pltpu.VMEM((SUBLANE, w), F32))
    return pl.pallas_call(kern, name=name, grid=(nt,), in_specs=in_specs, out_specs=out_specs, out_shape=out_shape,
                          scratch_shapes=scratch, compiler_params=_params("arbitrary"))(*ins)


def _s5_post1(y0, proj, dskip):
    c = y0.shape[1]

    def body(ins, outs, accs):
        ypre = ins[0][...] + ins[2][...] * ins[1][...]
        outs[0][...] = ypre
        outs[1][...] = _gelu(ypre).astype(BF16)

    return _rowwise("s5_post1", body, y0.shape[0], [(y0, c, 0), (proj, c, 2)], [dskip], [(c, F32), (c, BF16)], [])


def _s5_post2(yg, q0, bg, og):
    c = yg.shape[1]

    def body(ins, outs, accs):
        ygv = ins[0][...].astype(F32)
        sg = ygv * _sigmoid(ins[1][...] + ins[2][...])
        outs[0][...] = (sg * _rms_r(sg) * ins[3][...]).astype(BF16)

    return _rowwise("s5_post2", body, yg.shape[0], [(yg, c, 0), (q0, c, 0)], [bg, og], [(c, BF16)], [])[0]


def _s5_post2_bwd(dmixed, yg, q0, bg, og):
    c = yg.shape[1]

    def body(ins, outs, accs):
        dsn, ygv = ins[0][...], ins[1][...].astype(F32)
        s = _sigmoid(ins[2][...] + ins[3][...])
        sg = ygv * s
        r = _rms_r(sg)
        accs[0][...] += _colsum(dsn * sg * r)
        dsg = _rms_bwd(sg, r, ins[4][...], dsn)
        dq = dsg * ygv * s * (1.0 - s)
        outs[0][...] = dq.astype(BF16)
        outs[1][...] = dsg * s
        accs[1][...] += _colsum(dq)

    return _rowwise("s5_post2_bwd", body, yg.shape[0], [(dmixed, c, 1), (yg, c, 0), (q0, c, 0)], [bg, og],
                    [(c, BF16), (c, F32)], [(1, c)] * 2)


def _s5_post1_bwd(dyg1, dyg2, ypre, proj, dskip, after=()):
    c = ypre.shape[1]

    def body(ins, outs, accs):
        dyp = (ins[0][...] + ins[1][...]) * _dgelu(ins[2][...])
        outs[0][...] = dyp.astype(BF16)
        outs[1][...] = dyp * ins[4][...]
        accs[0][...] += _colsum(dyp * ins[3][...])

    return _rowwise("s5_post1_bwd", body, ypre.shape[0], [(dyg1, c, 0), (dyg2, c, 0), (ypre, c, 0), (proj, c, 2)], [dskip],
                    [(c, BF16), (c, F32)], [(1, c)], after=after)


def _place():
    return lax.axis_index("x"), lax.axis_index("y"), lax.axis_index("c")


def _window(ref, axis, q, rows, cols):
    if axis == 0:
        return ref.at[pl.ds(pl.multiple_of(q * rows, SUBLANE), rows), :]
    return ref.at[:, pl.ds(pl.multiple_of(q * cols, LANE), cols)]


def _chip_copies(gather, srcs, lands, shards, axes, send_sems, recv_sems, local_sems):
    x, y, c = _place()
    me = 2 * x + y
    starts, waits = [], []
    for a, (src, land) in enumerate(zip(srcs, lands)):
        rows, cols = shards[a]
        if gather:
            own = pltpu.make_async_copy(src, _window(land, axes[a], me, rows, cols), local_sems.at[a])
        else:
            own = pltpu.make_async_copy(_window(src, axes[a], me, rows, cols), land.at[3], local_sems.at[a])
        starts.append(own)
        waits.append(own)
        for j, (fx, fy) in enumerate(CHIP_RELS):
            px, py = (1 - x) if fx else x, (1 - y) if fy else y
            peer = 2 * px + py
            on = dict(send_sem=send_sems.at[3 * a + j], recv_sem=recv_sems.at[3 * a + j], device_id=(px, py, c),
                      device_id_type=MESH)
            if gather:
                starts.append(pltpu.make_async_remote_copy(src_ref=src, dst_ref=_window(land, axes[a], me, rows, cols), **on))
                waits.append(pltpu.make_async_remote_copy(src_ref=src, dst_ref=_window(land, axes[a], peer, rows, cols), **on))
            else:
                cp = pltpu.make_async_remote_copy(src_ref=_window(src, axes[a], peer, rows, cols), dst_ref=land.at[j], **on)
                starts.append(cp)
                waits.append(cp)
    return starts, waits


HBM = pl.BlockSpec(memory_space=pltpu.HBM)
SEM = pl.BlockSpec(memory_space=pltpu.SEMAPHORE)


def _shard_shapes(gather, arrs, axes):
    if gather:
        return [a.shape for a in arrs]
    return [(a.shape[0] // N_CHIPS, a.shape[1]) if ax == 0 else (a.shape[0], a.shape[1] // N_CHIPS) for a, ax in zip(arrs, axes)]


def _exchange_start(name, gather, arrs, axes, after=()):
    n, n_after = len(arrs), len(after)
    shards = _shard_shapes(gather, arrs, axes)
    if gather:
        land_shapes = [(N_CHIPS * r, c) if ax == 0 else (r, N_CHIPS * c) for (r, c), ax in zip(shards, axes)]
    else:
        land_shapes = [(N_CHIPS,) + s for s in shards]
    lands = [lax.empty(s, a.dtype) for s, a in zip(land_shapes, arrs)]

    def kern(*refs):
        outs = refs[2 * n + n_after:]
        starts, _ = _chip_copies(gather, refs[:n], refs[n:2 * n], shards, axes, outs[0], outs[1], outs[2])
        for cp in starts:
            cp.start()
        outs[-1][...] = jnp.zeros_like(outs[-1])

    kept = [pltpu.HBM(a.shape, a.dtype) for a in arrs] + [pltpu.HBM(s, a.dtype) for s, a in zip(land_shapes, arrs)]
    res = pl.pallas_call(
        kern, name=name, in_specs=[HBM] * (2 * n) + [ANY] * n_after,
        out_specs=[SEM] * 3 + [HBM] * (2 * n) + [pl.BlockSpec(memory_space=pltpu.VMEM)],
        out_shape=[pltpu.SemaphoreType.DMA((3 * n,)), pltpu.SemaphoreType.DMA((3 * n,)), pltpu.SemaphoreType.DMA((n,))]
        + kept + [jax.ShapeDtypeStruct((SUBLANE, LANE), F32)],
        input_output_aliases={i: 3 + i for i in range(2 * n)},
        compiler_params=pltpu.CompilerParams(has_side_effects=pltpu.SideEffectType.DATAFLOW_SIDE_EFFECTING),
    )(*[pltpu.with_memory_space_constraint(a, pltpu.HBM) for a in list(arrs) + lands], *after)
    return res[:3], res[3:3 + n], res[3 + n:3 + 2 * n], res[-1]


def _exchange_wait(name, gather, started, axes, after):
    sems, srcs, lands, _ = started
    n, n_after = len(srcs), len(after)
    shards = _shard_shapes(gather, srcs, axes)

    def kern(*refs):
        sem_refs = refs[2 * n:2 * n + 3]
        _, waits = _chip_copies(gather, refs[:n], refs[n:2 * n], shards, axes, *sem_refs)
        for cp in waits:
            cp.wait()

    res = pl.pallas_call(
        kern, name=name, in_specs=[HBM] * (2 * n) + [SEM] * 3 + [ANY] * n_after, out_specs=[HBM] * (2 * n),
        out_shape=[pltpu.HBM(a.shape, a.dtype) for a in list(srcs) + list(lands)],
        input_output_aliases={i: i for i in range(2 * n)},
        compiler_params=pltpu.CompilerParams(has_side_effects=pltpu.SideEffectType.DATAFLOW_SIDE_EFFECTING),
    )(*srcs, *lands, *sems, *after)
    return res[n:]


def _swap_with_sibling(arrs):
    n = len(arrs)

    def kern(*refs):
        ins, outs = refs[:n], refs[n:2 * n]
        send_sems, recv_sems = refs[2 * n:]
        x, y, c = _place()
        copies = [pltpu.make_async_remote_copy(src_ref=ins[a], dst_ref=outs[a], send_sem=send_sems.at[a],
                                               recv_sem=recv_sems.at[a], device_id=(x, y, 1 - c), device_id_type=MESH)
                  for a in range(n)]
        for cp in copies:
            cp.start()
        for cp in copies:
            cp.wait()

    return pl.pallas_call(
        kern, name="swap_with_sibling", in_specs=[ANY] * n, out_specs=[ANY] * n,
        out_shape=[jax.ShapeDtypeStruct(a.shape, a.dtype) for a in arrs],
        scratch_shapes=[pltpu.SemaphoreType.DMA((n,)), pltpu.SemaphoreType.DMA((n,))],
    )(*arrs)


def _all_reduce_small(buf):
    rels = [(fx, fy, fc) for fx in (0, 1) for fy in (0, 1) for fc in (0, 1)][1:]
    n_dev = len(rels) + 1

    def kern(b_ref, o_ref, recv_ref, send_sems, recv_sems):
        x, y, c = _place()
        me = 4 * x + 2 * y + c
        copies = []
        for k, (fx, fy, fc) in enumerate(rels):
            peer = ((1 - x) if fx else x, (1 - y) if fy else y, (1 - c) if fc else c)
            cp = pltpu.make_async_remote_copy(src_ref=b_ref, dst_ref=recv_ref.at[me], send_sem=send_sems.at[k],
                                              recv_sem=recv_sems.at[k], device_id=peer, device_id_type=MESH)
            cp.start()
            copies.append((cp, peer))
        recv_ref[me] = b_ref[...]
        for k, (cp, (px, py, pc)) in enumerate(copies):
            cp.wait_send()
            pltpu.make_async_remote_copy(src_ref=b_ref, dst_ref=recv_ref.at[4 * px + 2 * py + pc], send_sem=send_sems.at[k],
                                         recv_sem=recv_sems.at[k], device_id=(px, py, pc), device_id_type=MESH).wait_recv()
        acc = recv_ref[0]
        for d in range(1, n_dev):
            acc = acc + recv_ref[d]
        o_ref[...] = acc

    vm = pl.BlockSpec(memory_space=pltpu.VMEM)
    return pl.pallas_call(
        kern, name="all_reduce_small", in_specs=[vm], out_specs=vm, out_shape=jax.ShapeDtypeStruct(buf.shape, F32),
        scratch_shapes=[pltpu.VMEM((n_dev,) + buf.shape, F32), pltpu.SemaphoreType.DMA((n_dev - 1,)),
                        pltpu.SemaphoreType.DMA((n_dev - 1,))],
        compiler_params=pltpu.CompilerParams(vmem_limit_bytes=VMEM_LIMIT_BYTES),
    )(buf)


def _sum_slots(name, parts):
    _, rows, cols = parts.shape
    tr = _pick(rows, (ROW_TILE, 128, 64, 32))

    def kern(p_ref, o_ref):
        o_ref[...] = ((p_ref[3].astype(F32) + p_ref[0].astype(F32)) + p_ref[1].astype(F32)) + p_ref[2].astype(F32)

    return pl.pallas_call(kern, name=name, grid=(rows // tr,),
                          in_specs=[pl.BlockSpec((N_CHIPS, tr, cols), lambda i: (0, i, 0))],
                          out_specs=pl.BlockSpec((tr, cols), lambda i: (i, 0)),
                          out_shape=jax.ShapeDtypeStruct((rows, cols), F32), compiler_params=_params("arbitrary"))(parts)


def _adamw_math(g, w, m, v):
    m2 = ADAM_B1 * m + (1.0 - ADAM_B1) * g
    v2 = ADAM_B2 * v + (1.0 - ADAM_B2) * (g * g)
    m_hat = m2 / (1.0 - ADAM_B1 ** ADAM_STEP)
    v_hat = v2 / (1.0 - ADAM_B2 ** ADAM_STEP)
    return -ADAM_LR * (m_hat / (jnp.sqrt(v_hat) + ADAM_EPS) + ADAM_WD * w), m2, v2


def _adamw(name, parts, w, m, v):
    rows, cols = w.shape
    tr = rows if rows * cols <= WHOLE_ELEMS else _pick(rows, (ROW_TILE, 352, 128, 64, 32, 8))
    n = len(parts)

    def kern(*refs):
        g = refs[0][:, pl.ds(0, cols)]
        for p in refs[1:n]:
            g = g + p[:, pl.ds(0, cols)]
        d, m2, v2 = _adamw_math(g, refs[n][...], refs[n + 1][...], refs[n + 2][...])
        refs[n + 3][...] = g
        refs[n + 4][...] = d
        refs[n + 5][...] = m2
        refs[n + 6][...] = v2

    spec = pl.BlockSpec((tr, cols), lambda i: (i, 0))
    return pl.pallas_call(kern, name=name, grid=(rows // tr,),
                          in_specs=[pl.BlockSpec((tr, p.shape[1]), lambda i: (i, 0)) for p in parts] + [spec] * 3,
                          out_specs=[spec] * 4, out_shape=[jax.ShapeDtypeStruct((rows, cols), F32)] * 4,
                          compiler_params=_params("arbitrary"))(*parts, w, m, v)


def _pack(arrs):
    parts, rows = [], []
    for a in arrs:
        r = _round_up(-(-a.size // LANE), SUBLANE)
        parts.append(jnp.pad(a.reshape(-1).astype(F32), (0, r * LANE - a.size)).reshape(r, LANE))
        rows.append(r)
    return jnp.concatenate(parts, axis=0), rows


def _unpack(buf, rows, shapes):
    out, r0 = [], 0
    for r, s in zip(rows, shapes):
        size = math.prod(s)
        out.append(buf[r0:r0 + r].reshape(-1)[:size].reshape(s))
        r0 += r
    return out


def kernel(x, norm_ffn1, ffn1_w1, ffn1_w3, ffn1_w2, norm_mix, w_in, conv_w, conv_b, conv_ln_g, conv_ln_b, conv_out_g, ssm_A_re, ssm_A_im, ssm_log_dt, ssm_B_re, ssm_B_im, ssm_C_re, ssm_C_im, ssm_D, ssm_glu_w, ssm_glu_b, ssm_out_g, w_out, norm_ffn2, ffn2_w1, ffn2_w3, ffn2_w2, norm_final, loss_target, m_norm_ffn1, m_ffn1_w1, m_ffn1_w3, m_ffn1_w2, m_norm_mix, m_w_in, m_conv_w, m_conv_b, m_conv_ln_g, m_conv_ln_b, m_conv_out_g, m_ssm_A_re, m_ssm_A_im, m_ssm_log_dt, m_ssm_B_re, m_ssm_B_im, m_ssm_C_re, m_ssm_C_im, m_ssm_D, m_ssm_glu_w, m_ssm_glu_b, m_ssm_out_g, m_w_out, m_norm_ffn2, m_ffn2_w1, m_ffn2_w3, m_ffn2_w2, m_norm_final, v_norm_ffn1, v_ffn1_w1, v_ffn1_w3, v_ffn1_w2, v_norm_mix, v_w_in, v_conv_w, v_conv_b, v_conv_ln_g, v_conv_ln_b, v_conv_out_g, v_ssm_A_re, v_ssm_A_im, v_ssm_log_dt, v_ssm_B_re, v_ssm_B_im, v_ssm_C_re, v_ssm_C_im, v_ssm_D, v_ssm_glu_w, v_ssm_glu_b, v_ssm_out_g, v_w_out, v_norm_ffn2, v_ffn2_w1, v_ffn2_w3, v_ffn2_w2, v_norm_final):
    given = dict(locals())
    wts = {n: given[n] for n in WEIGHTS}
    n_seq, seq, d = x.shape
    n_rows = n_seq * seq
    xf = x.reshape(n_rows, d)
    tgt = loss_target.reshape(n_rows, d)
    row = lambda a: a.reshape(1, -1)

    f = ffn1_w1.shape[-1]
    fp = _round_up(f, LANE)
    shards = []
    for n in BIG:
        s = wts[n][0].astype(BF16)
        if n.endswith('_w1') or n.endswith('_w3'):
            s = jnp.pad(s, ((0, 0), (0, fp - f)))
        elif n.endswith('_w2'):
            s = jnp.pad(s, ((0, fp - f), (0, 0)))
        shards.append(s)
    n_taps, c_shard = conv_w.shape[1], conv_w.shape[2]
    shards.append(jnp.pad(conv_w[0], ((0, HALO - n_taps), (0, 0))))
    shard_of = dict(zip(BIG + ['conv_w'], shards))
    axis_of = dict(BIG_AXIS, conv_w=1)
    groups = [['ffn1_w1', 'ffn1_w3'], ['ffn1_w2', 'w_in', 'conv_w', 'ssm_glu_w', 'w_out'], ['ffn2_w1', 'ffn2_w3', 'ffn2_w2']]
    fetch, tok = [], []
    for k, names in enumerate(groups):
        fetch.append(_exchange_start("gather%d_send" % k, True, [shard_of[n] for n in names], [axis_of[n] for n in names], tok))
        tok = [fetch[-1][3]]
    full = {}

    def arrive(k, after):
        lands = _exchange_wait("gather%d_recv" % k, True, fetch[k], [axis_of[n] for n in groups[k]], after)
        full.update(zip(groups[k], lands))

    arrive(0, tok)

    _, n_grp, n_state = ssm_A_re.shape
    grp = ssm_B_re.shape[-1]
    ns = n_grp * n_state
    c_ssm = n_grp * grp
    lr, li = ssm_A_re.reshape(1, ns), ssm_A_im.reshape(1, ns)
    ldt = jnp.repeat(ssm_log_dt.reshape(n_grp), n_state).reshape(1, ns)
    btr = ssm_B_re[0].transpose(2, 0, 1).reshape(grp, ns)
    bti = ssm_B_im[0].transpose(2, 0, 1).reshape(grp, ns)
    ctr = ssm_C_re[0].transpose(1, 0, 2).reshape(grp, ns)
    cti = ssm_C_im[0].transpose(1, 0, 2).reshape(grp, ns)
    _, _, bbr, bbi, pw = _s5_params_fwd(lr, li, ldt, btr, bti)
    nb = c_ssm // LANE
    sb, gpb = ns // nb, n_grp // nb
    diag = (jnp.arange(LANE)[:, None] // grp) == (jnp.arange(sb)[None, :] // n_state)

    def spread(t):
        return jnp.where(diag, jnp.tile(t.reshape(grp, nb, sb).transpose(1, 0, 2), (1, gpb, 1)), 0.0)

    def gather_diag(t):
        return (t * diag).reshape(nb, gpb, grp, sb).sum(1).transpose(1, 0, 2).reshape(grp, ns)

    def interleave(re, im):
        return jnp.stack([re.reshape(-1, nb, sb), im.reshape(-1, nb, sb)], axis=2).reshape(-1, 2 * ns)

    bdc = jnp.concatenate([spread(bbr), spread(bbi)], axis=2).astype(BF16)
    cdc = jnp.concatenate([spread(ctr).transpose(0, 2, 1), -spread(cti).transpose(0, 2, 1)], axis=1).astype(BF16)
    rowi = jnp.arange(SUBLANE)[:, None]
    pwf, pwc = interleave(pw[:, :ns], pw[:, ns:]), interleave(pw[:, :ns], -pw[:, ns:])
    tabs_f = [jnp.where(rowi >= s, pwf[s - 1][None, :], 0.0) for s in (1, 2, 4)] + [pwf]
    tabs_b = [jnp.where(rowi <= SUBLANE - 1 - s, pwc[s - 1][None, :], 0.0) for s in (1, 2, 4)] + [pwc[::-1]]
    c_conv = conv_b.shape[1]
    u_blk = 2 * c_conv // LANE

    h1 = _rms_fwd("ffn1_rms", xf, norm_ffn1)
    a1, b1, z1 = _ffn_up("ffn1_up", h1, full['ffn1_w1'], full['ffn1_w3'])
    arrive(1, [z1])
    x1 = _mm("ffn1_down", z1, full['ffn1_w2'], 1, 0, F32, addend=xf, alpha=0.5)
    saved1 = (h1, a1, b1, z1)
    cw = full['conv_w']
    h2 = _rms_fwd("mix_rms", x1, norm_mix)
    proj = _mm("mix_in", h2, full['w_in'], 1, 0, F32)
    assert c_conv == c_ssm and proj.shape[1] == 3 * c_conv
    cpre, an = _conv_fwd(proj, cw, conv_b, conv_ln_g, conv_ln_b, conv_out_g, seq)
    bu = _bmm_rows("s5_bu", proj, LANE, u_blk, bdc, False, 2 * sb)
    xs = _scan("s5_scan", bu, None, tabs_f, seq, sb, False)[0]
    y0 = _bmm_rows("s5_cx", xs, 2 * sb, 0, cdc, False, LANE)
    ypre, yg = _s5_post1(y0, proj, ssm_D)
    q0 = _mm("s5_gate", yg, full['ssm_glu_w'], 1, 0, F32)
    sn = _s5_post2(yg, q0, ssm_glu_b, ssm_out_g)
    wo = full['w_out']
    x2 = _mm("mix_out_a", an, wo[:c_conv], 1, 0, F32, addend=x1)
    x2 = _mm("mix_out_s", sn, wo[c_conv:], 1, 0, F32, addend=x2)
    arrive(2, [x2])
    x3, saved2 = _ffn_fwd("ffn2", x2, norm_ffn2, full['ffn2_w1'], full['ffn2_w3'], full['ffn2_w2'])
    dx3, loss_row, d_norm_final = _loss_head(x3, row(norm_final), tgt)

    g = {}
    dx2, g['norm_ffn2'], sent = _ffn_bwd("ffn2", x2, norm_ffn2, full['ffn2_w1'], full['ffn2_w3'], full['ffn2_w2'], saved2, dx3)
    dmixed = _mm("mix_dmixed", dx2, wo, 1, 1, F32)
    dwo = jnp.concatenate([_mm("mix_dwo_a", an, dx2, 0, 0, BF16), _mm("mix_dwo_s", sn, dx2, 0, 0, BF16)], axis=0)
    dq, dyg1, g['ssm_out_g'], g['ssm_glu_b'] = _s5_post2_bwd(dmixed, yg, q0, ssm_glu_b, ssm_out_g)
    dyg2 = _mm("s5_dgate", dq, full['ssm_glu_w'], 1, 1, F32)
    dwg = _mm("s5_dwg", yg, dq, 0, 0, BF16)
    sent['w_out ssm_glu_w'] = (_exchange_start("mix_wo_wg_send", False, [dwo, dwg], [0, 0]), [0, 0])
    dypre, du_skip, g['ssm_D'] = _s5_post1_bwd(dyg1, dyg2, ypre, proj, ssm_D, after=[sent['w_out ssm_glu_w'][0][3]])
    gx = _bmm_rows("s5_dx", dypre, LANE, 0, cdc, True, 2 * sb)
    dcdc = _bmm_wgrad("s5_dc", xs, 2 * sb, 0, dypre, LANE)
    lam, dabar = _scan("s5_scan_bwd", gx, xs, tabs_b, seq, sb, True)
    du = _bmm_rows("s5_du", lam, 2 * sb, 0, bdc, True, LANE, addend=du_skip)
    dbdc = _bmm_wgrad("s5_db", proj, LANE, u_blk, lam, 2 * sb)
    dabar = dabar.reshape(nb, 2, sb)
    dlr, dli, dldt, dbtr, dbti = _s5_params_bwd(lr, li, ldt, btr, bti, dabar[:, 0].reshape(1, ns), dabar[:, 1].reshape(1, ns),
                                                gather_diag(dbdc[:, :, :sb]), gather_diag(dbdc[:, :, sb:]))
    g['ssm_A_re'], g['ssm_A_im'] = dlr, dli
    g['ssm_log_dt'] = dldt.reshape(n_grp, n_state).sum(axis=1)
    g['ssm_B_re'] = dbtr.reshape(grp, n_grp, n_state).transpose(1, 2, 0)
    g['ssm_B_im'] = dbti.reshape(grp, n_grp, n_state).transpose(1, 2, 0)
    g['ssm_C_re'] = gather_diag(dcdc[:, :sb].transpose(0, 2, 1)).reshape(grp, n_grp, n_state).transpose(1, 0, 2)
    g['ssm_C_im'] = -gather_diag(dcdc[:, sb:].transpose(0, 2, 1)).reshape(grp, n_grp, n_state).transpose(1, 0, 2)
    dc, g['conv_out_g'], g['conv_ln_g'], g['conv_ln_b'], g['conv_b'] = _conv_bwd_rows(dmixed, cpre, conv_ln_g, conv_ln_b,
                                                                                    conv_out_g)
    dval, dgate, dcw = _conv_bwd_taps(proj, dc, cw, seq)
    dproj = jnp.concatenate([dval, dgate, du], axis=1)
    sent['w_in'] = (_exchange_start("mix_win_send", False, [_mm("mix_dwin", h2, dproj, 0, 0, BF16)], [1]), [1])
    dh2 = _mm("mix_dh", dproj, full['w_in'], 1, 1, F32, after=[sent['w_in'][0][3]])
    dx1, g['norm_mix'] = _rms_bwd_res("mix_drms", x1, norm_mix, dh2, dx2)
    dx0, g['norm_ffn1'], sent1 = _ffn_bwd("ffn1", xf, norm_ffn1, full['ffn1_w1'], full['ffn1_w3'], full['ffn1_w2'], saved1, dx1)
    sent.update(sent1)
    g['norm_final'] = d_norm_final
    g['conv_w'] = dcw[:n_taps]

    small_shapes = [(n_taps, c_conv) if n == 'conv_w' else wts[n].shape for n in SMALL]
    buf, buf_rows = _pack([g[n] for n in SMALL] + [loss_row])
    total = _unpack(_all_reduce_small(buf), buf_rows, small_shapes + [(1, LANE)])
    loss = total[-1][0, 0]
    grads = dict(zip(SMALL, total[:-1]))
    chip = 2 * lax.axis_index("x") + lax.axis_index("y")
    grads['conv_w'] = lax.dynamic_slice_in_dim(grads['conv_w'], chip * c_shard, c_shard, axis=1)[None]
    g_buf, rows_s = _pack([grads[n] for n in SMALL])
    packed = [_pack([given[p + n] for n in SMALL])[0] for p in ('', 'm_', 'v_')]
    _, d_buf, m_buf, v_buf = _adamw("adamw_small", [g_buf], *packed)
    shapes_s = [wts[n].shape for n in SMALL]
    deltas = dict(zip(SMALL, _unpack(d_buf, rows_s, shapes_s)))
    new_m = dict(zip(SMALL, _unpack(m_buf, rows_s, shapes_s)))
    new_v = dict(zip(SMALL, _unpack(v_buf, rows_s, shapes_s)))

    slots = {}
    for names, (started, axes) in sent.items():
        lands = _exchange_wait(names.replace(' ', '_') + "_recv", False, started, axes, after=[dx0])
        slots.update(zip(names.split(), lands))
    sums = [_sum_slots("sum_" + n, slots[n]) for n in BIG]
    theirs = _swap_with_sibling(sums)
    for n, mine, other in zip(BIG, sums, theirs):
        grads[n], deltas[n], new_m[n], new_v[n] = (
            o[None] for o in _adamw("adamw_" + n, [mine, other], given[n][0], given['m_' + n][0], given['v_' + n][0]))

    return (loss, dx0.reshape(x.shape), *[grads[n] for n in WEIGHTS], *[deltas[n] for n in WEIGHTS],
            *[new_m[n] for n in WEIGHTS], *[new_v[n] for n in WEIGHTS])
```

```python
import math

import jax
import jax.numpy as jnp
from jax import lax
from jax.experimental import pallas as pl
from jax.experimental.pallas import tpu as pltpu

F32 = jnp.float32
BF16 = jnp.bfloat16
EPS = 1e-6
ADAM_LR, ADAM_B1, ADAM_B2, ADAM_EPS, ADAM_WD, ADAM_STEP = 0.001, 0.9, 0.999, 1e-08, 0.01, 10
MESH = pl.DeviceIdType.MESH
ANY = pl.BlockSpec(memory_space=pl.ANY)
LANE = 128
SUBLANE = 8
VMEM_LIMIT_BYTES = 56 << 20
ROW_TILE = 256
ROW_TILE_ELEMS = 256 * 1024
WHOLE_ELEMS = 512 * 1024
FFN_ROWS = 256
CONV_TILE = 128
CONV_SUB = 32
HALO = 32
SCAN_TILE = 128
SCAN_COLS = 512
N_CHIPS = 4
CHIP_RELS = ((1, 0), (0, 1), (1, 1))
NT = (((1,), (1,)), ((), ()))
GELU_K = math.sqrt(2.0 / math.pi)
GELU_C = 0.044715

WEIGHTS = ['norm_ffn1', 'ffn1_w1', 'ffn1_w3', 'ffn1_w2', 'norm_mix', 'w_in', 'conv_w', 'conv_b', 'conv_ln_g', 'conv_ln_b',
           'conv_out_g', 'ssm_A_re', 'ssm_A_im', 'ssm_log_dt', 'ssm_B_re', 'ssm_B_im', 'ssm_C_re', 'ssm_C_im', 'ssm_D',
           'ssm_glu_w', 'ssm_glu_b', 'ssm_out_g', 'w_out', 'norm_ffn2', 'ffn2_w1', 'ffn2_w3', 'ffn2_w2', 'norm_final']
BIG = ['ffn1_w1', 'ffn1_w3', 'ffn1_w2', 'w_in', 'ssm_glu_w', 'w_out', 'ffn2_w1', 'ffn2_w3', 'ffn2_w2']
BIG_AXIS = {'ffn1_w1': 1, 'ffn1_w3': 1, 'ffn1_w2': 0, 'w_in': 1, 'ssm_glu_w': 0, 'w_out': 0, 'ffn2_w1': 1, 'ffn2_w3': 1,
            'ffn2_w2': 0}
SMALL = [n for n in WEIGHTS if n not in BIG]


def _round_up(n, m):
    return -(-n // m) * m


def _pick(n, cands):
    for c in cands:
        if c <= n and n % c == 0:
            return c
    return n


def _params(*sem):
    return pltpu.CompilerParams(dimension_semantics=sem, vmem_limit_bytes=VMEM_LIMIT_BYTES)


def _rms_r(x):
    return lax.rsqrt(jnp.mean(x * x, axis=-1, keepdims=True) + EPS)


def _rms_bwd(x, r, g, dy):
    dyg = dy * g
    return r * dyg - x * (r * r * r) * jnp.mean(x * dyg, axis=-1, keepdims=True)


def _sigmoid(x):
    return jax.nn.sigmoid(x)


def _dsilu(a, s):
    return s * (1.0 + a * (1.0 - s))


def _gelu(x):
    return 0.5 * x * (1.0 + jnp.tanh(GELU_K * (x + GELU_C * x * x * x)))


def _dgelu(x):
    t = jnp.tanh(GELU_K * (x + GELU_C * x * x * x))
    return 0.5 * (1.0 + t) + 0.5 * x * (1.0 - t * t) * GELU_K * (1.0 + 3.0 * GELU_C * x * x)


def _colsum(v):
    return jnp.sum(v, axis=0, keepdims=True)


def _rowwise(name, body, n_rows, row_ins, par_ins, row_outs, acc_outs, after=()):
    widest = max([w for (_, w, _) in row_ins] + [w for (w, _) in row_outs])
    tt = _pick(n_rows, [t for t in (256, 128, 64, 32, 16, 8) if t * widest <= ROW_TILE_ELEMS])
    in_specs = [pl.BlockSpec((tt, w), lambda i, cb=cb: (i, cb)) for (_, w, cb) in row_ins]
    in_specs += [pl.BlockSpec(p.shape, lambda i: (0, 0)) for p in par_ins] + [ANY] * len(after)
    out_specs = [pl.BlockSpec((tt, w), lambda i: (i, 0)) for (w, _) in row_outs]
    out_specs += [pl.BlockSpec((r, w), lambda i: (0, 0)) for (r, w) in acc_outs]
    out_shape = [jax.ShapeDtypeStruct((n_rows, w), dt) for (w, dt) in row_outs]
    out_shape += [jax.ShapeDtypeStruct((r, w), F32) for (r, w) in acc_outs]
    n_in, n_ro = len(row_ins) + len(par_ins), len(row_outs)
    o0 = n_in + len(after)

    def kern(*refs):
        accs = refs[o0 + n_ro:]
        if accs:
            @pl.when(pl.program_id(0) == 0)
            def _():
                for a in accs:
                    a[...] = jnp.zeros_like(a)
        body(refs[:n_in], refs[o0:o0 + n_ro], accs)

    return pl.pallas_call(kern, name=name, grid=(n_rows // tt,), in_specs=in_specs, out_specs=out_specs, out_shape=out_shape,
                          compiler_params=_params("arbitrary"))(*[a for a, _, _ in row_ins], *par_ins, *after)


def _mm(name, a, b, ca, cb, out_dtype=F32, addend=None, alpha=1.0, a_cols=None, after=()):
    a_start, a_width = a_cols if a_cols else (0, a.shape[1])
    m, k = (a.shape[0], a_width) if ca == 1 else (a_width, a.shape[0])
    n = b.shape[1 - cb]
    assert b.shape[cb] == k, (name, a.shape, b.shape)
    tm = _pick(m, (1024, 512, 256, 128))
    tn = _pick(n, (1024, 768, 512, 384, 256, 128))
    tk = _pick(k, (1024, 768, 512, 256, 128))
    nk = k // tk
    if ca == 1:
        assert a_start % tk == 0
        a_spec = pl.BlockSpec((tm, tk), lambda i, j, kk: (i, kk + a_start // tk))
    else:
        assert a_start % tm == 0
        a_spec = pl.BlockSpec((tk, tm), lambda i, j, kk: (kk, i + a_start // tm))
    b_spec = pl.BlockSpec((tk, tn), lambda i, j, kk: (kk, j)) if cb == 0 else pl.BlockSpec((tn, tk), lambda i, j, kk: (j, kk))
    o_spec = pl.BlockSpec((tm, tn), lambda i, j, kk: (i, j))
    ins, in_specs = [a, b], [a_spec, b_spec]
    if addend is not None:
        ins.append(addend)
        in_specs.append(o_spec)
    ins += list(after)
    in_specs += [ANY] * len(after)
    dims = (((ca,), (cb,)), ((), ()))

    def finish(refs, r):
        if alpha != 1.0:
            r = r * alpha
        if addend is not None:
            r = r + refs[2][...].astype(F32)
        return r.astype(out_dtype)

    def kern_one(*refs):
        refs[-1][...] = finish(refs, lax.dot_general(refs[0][...].astype(BF16), refs[1][...].astype(BF16), dims,
                                                     preferred_element_type=F32))

    def kern_acc(*refs):
        o_ref, acc_ref = refs[-2], refs[-1]
        kk = pl.program_id(2)

        @pl.when(kk == 0)
        def _():
            acc_ref[...] = jnp.zeros_like(acc_ref)

        acc_ref[...] += lax.dot_general(refs[0][...].astype(BF16), refs[1][...].astype(BF16), dims,
                                        preferred_element_type=F32)

        @pl.when(kk == nk - 1)
        def _():
            o_ref[...] = finish(refs, acc_ref[...])

    return pl.pallas_call(kern_one if nk == 1 else kern_acc, name=name, grid=(m // tm, n // tn, nk), in_specs=in_specs,
                          out_specs=o_spec, out_shape=jax.ShapeDtypeStruct((m, n), out_dtype),
                          scratch_shapes=[] if nk == 1 else [pltpu.VMEM((tm, tn), F32)],
                          compiler_params=_params("arbitrary", "arbitrary", "arbitrary"))(*ins)


def _rms_fwd(name, x, g):
    def body(ins, outs, accs):
        xv = ins[0][...]
        outs[0][...] = (xv * _rms_r(xv) * ins[1][...]).astype(BF16)

    return _rowwise(name, body, x.shape[0], [(x, x.shape[1], 0)], [g], [(x.shape[1], BF16)], [])[0]


def _rms_bwd_res(name, x, g, dh, dres):
    d = x.shape[1]

    def body(ins, outs, accs):
        xv, dhv, gv = ins[0][...], ins[1][...], ins[3][...]
        r = _rms_r(xv)
        outs[0][...] = ins[2][...] + _rms_bwd(xv, r, gv, dhv)
        accs[0][...] += _colsum(dhv * xv * r)

    return _rowwise(name, body, x.shape[0], [(x, d, 0), (dh, d, 0), (dres, d, 0)], [g], [(d, F32)], [(1, d)])


def _ffn_up(name, h, w1, w3):
    t, d = h.shape
    ff = w1.shape[1]
    tm, tn = _pick(t, (1024, 512, 256, 128)), _pick(ff, (1024, 768, 512, 256, 128))

    def kern(h_ref, w1_ref, w3_ref, a_ref, b_ref, z_ref):
        hv = h_ref[...]
        a = jnp.dot(hv, w1_ref[...], preferred_element_type=F32)
        b = jnp.dot(hv, w3_ref[...], preferred_element_type=F32)
        a_ref[...] = a.astype(BF16)
        b_ref[...] = b.astype(BF16)
        z_ref[...] = (a * _sigmoid(a) * b).astype(BF16)

    w_spec = pl.BlockSpec((d, tn), lambda i, j: (0, j))
    o_spec = pl.BlockSpec((tm, tn), lambda i, j: (i, j))
    return pl.pallas_call(kern, name=name, grid=(t // tm, ff // tn),
                          in_specs=[pl.BlockSpec((tm, d), lambda i, j: (i, 0)), w_spec, w_spec], out_specs=[o_spec] * 3,
                          out_shape=[jax.ShapeDtypeStruct((t, ff), BF16)] * 3,
                          compiler_params=_params("arbitrary", "arbitrary"))(h, w1, w3)


def _ffn_dglu(name, dxo, w2, a, b, after=()):
    t, d = dxo.shape
    ff = w2.shape[0]
    tm = _pick(t, (FFN_ROWS, 128))

    def kern(dx_ref, w2_ref, a_ref, b_ref, *rest):
        da_ref, db_ref = rest[-2:]
        dz = lax.dot_general(dx_ref[...].astype(BF16), w2_ref[...], NT, preferred_element_type=F32) * 0.5
        av, bv = a_ref[...].astype(F32), b_ref[...].astype(F32)
        s = _sigmoid(av)
        da_ref[...] = (dz * bv * _dsilu(av, s)).astype(BF16)
        db_ref[...] = (dz * av * s).astype(BF16)

    o_spec = pl.BlockSpec((tm, ff), lambda i: (i, 0))
    return pl.pallas_call(kern, name=name, grid=(t // tm,),
                          in_specs=[pl.BlockSpec((tm, d), lambda i: (i, 0)), pl.BlockSpec((ff, d), lambda i: (0, 0)),
                                    o_spec, o_spec] + [ANY] * len(after),
                          out_specs=[o_spec] * 2, out_shape=[jax.ShapeDtypeStruct((t, ff), BF16)] * 2,
                          compiler_params=_params("arbitrary"))(dxo, w2, a, b, *after)


def _ffn_dh(name, da, db, w1, w3, x, g, dres, after=()):
    t, d = x.shape
    ff = da.shape[1]
    tm = _pick(t, (FFN_ROWS, 128))

    def kern(da_ref, db_ref, w1_ref, w3_ref, x_ref, g_ref, dres_ref, *rest):
        dx_ref, dg_ref = rest[-2:]

        @pl.when(pl.program_id(0) == 0)
        def _():
            dg_ref[...] = jnp.zeros_like(dg_ref)

        dh = (lax.dot_general(da_ref[...], w1_ref[...], NT, preferred_element_type=F32)
              + lax.dot_general(db_ref[...], w3_ref[...], NT, preferred_element_type=F32))
        xv = x_ref[...]
        r = _rms_r(xv)
        dx_ref[...] = dres_ref[...] + _rms_bwd(xv, r, g_ref[...], dh)
        dg_ref[...] += _colsum(dh * xv * r)

    act = pl.BlockSpec((tm, ff), lambda i: (i, 0))
    wgt = pl.BlockSpec((d, ff), lambda i: (0, 0))
    rows = pl.BlockSpec((tm, d), lambda i: (i, 0))
    gain = pl.BlockSpec((1, d), lambda i: (0, 0))
    return pl.pallas_call(kern, name=name, grid=(t // tm,),
                          in_specs=[act, act, wgt, wgt, rows, gain, rows] + [ANY] * len(after), out_specs=[rows, gain],
                          out_shape=[jax.ShapeDtypeStruct((t, d), F32), jax.ShapeDtypeStruct((1, d), F32)],
                          compiler_params=_params("arbitrary"))(da, db, w1, w3, x, g, dres, *after)


def _loss_head(x3, gf, tgt):
    d = x3.shape[1]

    def body(ins, outs, accs):
        xv, tv, gv = ins[0][...], ins[1][...], ins[2][...]
        r = _rms_r(xv)
        e = xv * r * gv - tv
        sq = jnp.sum(jnp.sum(e * e, axis=-1, keepdims=True), axis=0, keepdims=True)
        accs[0][...] += jnp.broadcast_to(sq * (0.5 / d), (1, LANE))
        dy = e * (1.0 / d)
        outs[0][...] = _rms_bwd(xv, r, gv, dy)
        accs[1][...] += _colsum(dy * xv * r)

    return _rowwise("loss_head", body, x3.shape[0], [(x3, d, 0), (tgt, d, 0)], [gf], [(d, F32)], [(1, LANE), (1, d)])


def _ffn_fwd(tag, x, g, w1, w3, w2):
    h = _rms_fwd(tag + "_rms", x, g)
    a, b, z = _ffn_up(tag + "_up", h, w1, w3)
    return _mm(tag + "_down", z, w2, 1, 0, F32, addend=x, alpha=0.5), (h, a, b, z)


def _ffn_bwd(tag, x, g, w1, w3, w2, saved, dxo):
    h, a, b, z = saved
    dw2 = _mm(tag + "_dw2", z, dxo, 0, 0, BF16, alpha=0.5)
    s2 = _exchange_start(tag + "_w2_send", False, [dw2], [0])
    da, db = _ffn_dglu(tag + "_dglu", dxo, w2, a, b, after=[s2[3]])
    dw1 = _mm(tag + "_dw1", h, da, 0, 0, BF16)
    s1 = _exchange_start(tag + "_w1_send", False, [dw1], [1])
    dw3 = _mm(tag + "_dw3", h, db, 0, 0, BF16, after=[s1[3]])
    s3 = _exchange_start(tag + "_w3_send", False, [dw3], [1])
    dx, dg = _ffn_dh(tag + "_dh", da, db, w1, w3, x, g, dxo, after=[s3[3]])
    return dx, dg, {tag + "_w1": (s1, [1]), tag + "_w3": (s3, [1]), tag + "_w2": (s2, [0])}


def _conv_fwd(proj, cw, cb, lng, lnb, og, seq):
    n_rows, c = proj.shape[0], cb.shape[1]
    kw = HALO - 1
    tt = _pick(seq, (CONV_TILE,))
    hb = tt // HALO

    def kern(v_ref, g_ref, vp_ref, gp_ref, w_ref, cb_ref, lg_ref, lb_ref, og_ref, c_ref, an_ref, ext_ref):
        first = (pl.program_id(0) * tt) % seq == 0
        ext_ref[pl.ds(HALO, tt), :] = v_ref[...] * _sigmoid(g_ref[...])
        ext_ref[pl.ds(0, HALO), :] = vp_ref[...] * _sigmoid(gp_ref[...]) * jnp.where(first, 0.0, 1.0)
        for r0 in range(0, tt, CONV_SUB):
            rows = min(CONV_SUB, tt - r0)
            acc = jnp.zeros((rows, c), F32)
            for k in range(kw):
                acc = acc + w_ref[pl.ds(k, 1), :] * ext_ref[pl.ds(r0 + HALO - (kw - 1) + k, rows), :]
            c_ref[pl.ds(r0, rows), :] = acc + cb_ref[...]
        cv = c_ref[...]
        mu = jnp.mean(cv, axis=-1, keepdims=True)
        xc = cv - mu
        rstd = lax.rsqrt(jnp.mean(xc * xc, axis=-1, keepdims=True) + EPS)
        lv = xc * rstd * lg_ref[...] + lb_ref[...]
        sl = lv * _sigmoid(lv)
        an_ref[...] = (sl * _rms_r(sl) * og_ref[...]).astype(BF16)

    cur = lambda cbk: pl.BlockSpec((tt, c), lambda i: (i, cbk))
    prev = lambda cbk: pl.BlockSpec((HALO, c), lambda i: (jnp.maximum(i * hb - 1, 0), cbk))
    par = lambda p: pl.BlockSpec(p.shape, lambda i: (0, 0))
    return pl.pallas_call(
        kern, name="conv_fwd", grid=(n_rows // tt,),
        in_specs=[cur(0), cur(1), prev(0), prev(1), par(cw), par(cb), par(lng), par(lnb), par(og)],
        out_specs=[pl.BlockSpec((tt, c), lambda i: (i, 0))] * 2,
        out_shape=[jax.ShapeDtypeStruct((n_rows, c), F32), jax.ShapeDtypeStruct((n_rows, c), BF16)],
        scratch_shapes=[pltpu.VMEM((tt + HALO, c), F32)], compiler_params=_params("arbitrary"),
    )(proj, proj, proj, proj, cw, cb, lng, lnb, og)


def _conv_bwd_rows(dmixed, cpre, lng, lnb, og):
    c = cpre.shape[1]

    def body(ins, outs, accs):
        dan, cv, lg, lb, ogv = ins[0][...], ins[1][...], ins[2][...], ins[3][...], ins[4][...]
        mu = jnp.mean(cv, axis=-1, keepdims=True)
        xc = cv - mu
        rstd = lax.rsqrt(jnp.mean(xc * xc, axis=-1, keepdims=True) + EPS)
        xh = xc * rstd
        lv = xh * lg + lb
        s = _sigmoid(lv)
        sl = lv * s
        r2 = _rms_r(sl)
        accs[0][...] += _colsum(dan * sl * r2)
        dl = _rms_bwd(sl, r2, ogv, dan) * _dsilu(lv, s)
        accs[1][...] += _colsum(dl * xh)
        accs[2][...] += _colsum(dl)
        dxh = dl * lg
        dc = rstd * (dxh - jnp.mean(dxh, axis=-1, keepdims=True) - xh * jnp.mean(dxh * xh, axis=-1, keepdims=True))
        outs[0][...] = dc
        accs[3][...] += _colsum(dc)

    return _rowwise("conv_bwd_rows", body, cpre.shape[0], [(dmixed, c, 0), (cpre, c, 0)], [lng, lnb, og], [(c, F32)],
                    [(1, c)] * 4)


def _conv_bwd_taps(proj, dc, cw, seq):
    n_rows, c = dc.shape
    kw = HALO - 1
    tt = _pick(seq, (CONV_TILE,))
    hb = tt // HALO
    last_blk = n_rows // HALO - 1

    def kern(v_ref, g_ref, vp_ref, gp_ref, dc_ref, dn_ref, w_ref, dv_ref, dg_ref, dw_ref, exta_ref, extd_ref):
        i = pl.program_id(0)
        first = (i * tt) % seq == 0
        last = ((i + 1) * tt) % seq == 0

        @pl.when(i == 0)
        def _():
            dw_ref[...] = jnp.zeros_like(dw_ref)

        sg = _sigmoid(g_ref[...])
        exta_ref[pl.ds(HALO, tt), :] = v_ref[...] * sg
        exta_ref[pl.ds(0, HALO), :] = vp_ref[...] * _sigmoid(gp_ref[...]) * jnp.where(first, 0.0, 1.0)
        dcv = dc_ref[...]
        extd_ref[pl.ds(0, tt), :] = dcv
        extd_ref[pl.ds(tt, HALO), :] = dn_ref[...] * jnp.where(last, 0.0, 1.0)
        for k in range(kw):
            dw_ref[pl.ds(k, 1), :] += _colsum(exta_ref[pl.ds(HALO - (kw - 1) + k, tt), :] * dcv)
        for r0 in range(0, tt, CONV_SUB):
            rows = min(CONV_SUB, tt - r0)
            acc = jnp.zeros((rows, c), F32)
            for k in range(kw):
                acc = acc + w_ref[pl.ds(k, 1), :] * extd_ref[pl.ds(r0 + (kw - 1) - k, rows), :]
            dv_ref[pl.ds(r0, rows), :] = acc
        da = dv_ref[...]
        dv_ref[...] = da * sg
        dg_ref[...] = da * v_ref[...] * sg * (1.0 - sg)

    cur = lambda cbk: pl.BlockSpec((tt, c), lambda i: (i, cbk))
    prev = lambda cbk: pl.BlockSpec((HALO, c), lambda i: (jnp.maximum(i * hb - 1, 0), cbk))
    nxt = pl.BlockSpec((HALO, c), lambda i: (jnp.minimum((i + 1) * hb, last_blk), 0))
    return pl.pallas_call(
        kern, name="conv_bwd_taps", grid=(n_rows // tt,),
        in_specs=[cur(0), cur(1), prev(0), prev(1), cur(0), nxt, pl.BlockSpec(cw.shape, lambda i: (0, 0))],
        out_specs=[cur(0), cur(0), pl.BlockSpec((HALO, c), lambda i: (0, 0))],
        out_shape=[jax.ShapeDtypeStruct((n_rows, c), F32), jax.ShapeDtypeStruct((n_rows, c), F32),
                   jax.ShapeDtypeStruct((HALO, c), F32)],
        scratch_shapes=[pltpu.VMEM((tt + HALO, c), F32), pltpu.VMEM((tt + HALO, c), F32)],
        compiler_params=_params("arbitrary"),
    )(proj, proj, proj, proj, dc, dc, cw)


def _s5_params_fwd(lr, li, ldt, btr, bti):
    ns = lr.shape[1]

    def kern(lr_ref, li_ref, ldt_ref, btr_ref, bti_ref, ar_ref, ai_ref, bbr_ref, bbi_ref, pw_ref):
        lrv, liv = lr_ref[...], li_ref[...]
        dt = jnp.exp(ldt_ref[...])
        zr, zi = lrv * dt, liv * dt
        mag = jnp.exp(zr)
        ar, ai = mag * jnp.cos(zi), mag * jnp.sin(zi)
        den = lrv * lrv + liv * liv
        nr = ar - 1.0
        cr = (nr * lrv + ai * liv) / den
        ci = (ai * lrv - nr * liv) / den
        ar_ref[...] = ar
        ai_ref[...] = ai
        bbr_ref[...] = cr * btr_ref[...] - ci * bti_ref[...]
        bbi_ref[...] = cr * bti_ref[...] + ci * btr_ref[...]
        pr, pi = ar, ai
        for e in range(SUBLANE):
            pw_ref[pl.ds(e, 1), pl.ds(0, ns)] = pr
            pw_ref[pl.ds(e, 1), pl.ds(ns, ns)] = pi
            pr, pi = pr * ar - pi * ai, pr * ai + pi * ar

    h = btr.shape[0]
    shapes = [jax.ShapeDtypeStruct((1, ns), F32)] * 2 + [jax.ShapeDtypeStruct((h, ns), F32)] * 2
    shapes += [jax.ShapeDtypeStruct((SUBLANE, 2 * ns), F32)]
    return pl.pallas_call(kern, name="s5_params_fwd", out_shape=shapes)(lr, li, ldt, btr, bti)


def _s5_params_bwd(lr, li, ldt, btr, bti, dar, dai, dbbr, dbbi):
    def kern(lr_ref, li_ref, ldt_ref, btr_ref, bti_ref, dar_ref, dai_ref, dbr_ref, dbi_ref,
             dlr_ref, dli_ref, dldt_ref, dbtr_ref, dbti_ref):
        lrv, liv = lr_ref[...], li_ref[...]
        dt = jnp.exp(ldt_ref[...])
        zr, zi = lrv * dt, liv * dt
        mag = jnp.exp(zr)
        ar, ai = mag * jnp.cos(zi), mag * jnp.sin(zi)
        den = lrv * lrv + liv * liv
        nr = ar - 1.0
        cr = (nr * lrv + ai * liv) / den
        ci = (ai * lrv - nr * liv) / den
        dbr, dbi, br, bi = dbr_ref[...], dbi_ref[...], btr_ref[...], bti_ref[...]
        dbtr_ref[...] = cr * dbr + ci * dbi
        dbti_ref[...] = cr * dbi - ci * dbr
        dcr = _colsum(br * dbr + bi * dbi)
        dci = _colsum(br * dbi - bi * dbr)
        ir, ii = lrv / den, -liv / den
        dnr = ir * dcr + ii * dci
        dni = ir * dci - ii * dcr
        wr, wi = cr * ir - ci * ii, cr * ii + ci * ir
        dl1r = -(wr * dcr + wi * dci)
        dl1i = -(wr * dci - wi * dcr)
        dtr, dti = dar_ref[...] + dnr, dai_ref[...] + dni
        dzr = ar * dtr + ai * dti
        dzi = ar * dti - ai * dtr
        dlr_ref[...] = dl1r + dt * dzr
        dli_ref[...] = dl1i + dt * dzi
        dldt_ref[...] = (dzr * lrv + dzi * liv) * dt

    ns, h = lr.shape[1], btr.shape[0]
    shapes = [jax.ShapeDtypeStruct((1, ns), F32)] * 3 + [jax.ShapeDtypeStruct((h, ns), F32)] * 2
    return pl.pallas_call(kern, name="s5_params_bwd", out_shape=shapes)(lr, li, ldt, btr, bti, dar, dai, dbbr, dbbi)


def _scan_tile(s_ref, o_ref, tabs, car_ref, sb, reverse, x_ref=None, acc_ref=None):
    l1, l2, l4, pw = tabs
    rows_t, w = s_ref.shape
    ng = rows_t // SUBLANE
    cw = _pick(sb, (SCAN_COLS,))
    carry_row = 0 if reverse else SUBLANE - 1
    row = lax.broadcasted_iota(jnp.int32, (SUBLANE, cw), 0)

    def group(gi, carry):
        g = (ng - 1 - gi) if reverse else gi
        rows = pl.ds(pl.multiple_of(g * SUBLANE, SUBLANE), SUBLANE)
        for c0 in [b0 + o for b0 in range(0, w, 2 * sb) for o in range(0, sb, cw)]:
            cr, ci = pl.ds(c0, cw), pl.ds(c0 + sb, cw)
            xr, xi = s_ref[rows, cr], s_ref[rows, ci]
            for s, lt in ((1, l1), (2, l2), (4, l4)):
                sh = (SUBLANE - s) if reverse else s
                sr, si = pltpu.roll(xr, sh, 0), pltpu.roll(xi, sh, 0)
                ar, ai = lt[:, cr], lt[:, ci]
                xr, xi = xr + ar * sr - ai * si, xi + ar * si + ai * sr
            kr, ki = car_ref[pl.ds(carry_row, 1), cr], car_ref[pl.ds(carry_row, 1), ci]
            pr, pi = pw[:, cr], pw[:, ci]
            xr, xi = xr + pr * kr - pi * ki, xi + pr * ki + pi * kr
            o_ref[rows, cr] = xr
            o_ref[rows, ci] = xi
            car_ref[:, cr] = xr
            car_ref[:, ci] = xi
            if acc_ref is not None:
                nr = jnp.where(row == SUBLANE - 1, kr, pltpu.roll(xr, SUBLANE - 1, 0))
                ni = jnp.where(row == SUBLANE - 1, ki, pltpu.roll(xi, SUBLANE - 1, 0))
                pxr, pxi = x_ref[rows, cr], x_ref[rows, ci]
                acc_ref[:, cr] += nr * pxr + ni * pxi
                acc_ref[:, ci] += ni * pxr - nr * pxi
        return carry

    lax.fori_loop(0, ng, group, 0)


def _s5_fwd(proj, u_blk, bdc, cdc, tabs, dskip, seq, sb):
    n_rows = proj.shape[0]
    nb, blk, w_blk = bdc.shape
    c, w = nb * blk, nb * w_blk
    tt = _pick(seq, (SCAN_TILE,))

    def kern(u_ref, bd_ref, cd_ref, l1, l2, l4, pw, d_ref, xs_ref, yp_ref, yg_ref, bu_ref, car_ref):
        @pl.when((pl.program_id(0) * tt) % seq == 0)
        def _():
            car_ref[...] = jnp.zeros_like(car_ref)

        for j in range(nb):
            bu_ref[:, pl.ds(j * w_blk, w_blk)] = jnp.dot(u_ref[:, pl.ds(j * blk, blk)].astype(BF16), bd_ref[j],
                                                         preferred_element_type=F32)
        _scan_tile(bu_ref, xs_ref, (l1, l2, l4, pw), car_ref, sb, False)
        for j in range(nb):
            cols = pl.ds(j * blk, blk)
            y0 = jnp.dot(xs_ref[:, pl.ds(j * w_blk, w_blk)].astype(BF16), cd_ref[j], preferred_element_type=F32)
            ypre = y0 + d_ref[:, cols] * u_ref[:, cols]
            yp_ref[:, cols] = ypre
            yg_ref[:, cols] = _gelu(ypre).astype(BF16)

    tab = pl.BlockSpec((SUBLANE, w), lambda i: (0, 0))
    rows = pl.BlockSpec((tt, c), lambda i: (i, 0))
    return pl.pallas_call(
        kern, name="s5_fwd", grid=(n_rows // tt,),
        in_specs=[pl.BlockSpec((tt, c), lambda i: (i, u_blk * blk // c)), pl.BlockSpec(bdc.shape, lambda i: (0, 0, 0)),
                  pl.BlockSpec(cdc.shape, lambda i: (0, 0, 0)), tab, tab, tab, tab, pl.BlockSpec((1, c), lambda i: (0, 0))],
        out_specs=[pl.BlockSpec((tt, w), lambda i: (i, 0)), rows, rows],
        out_shape=[jax.ShapeDtypeStruct((n_rows, w), F32), jax.ShapeDtypeStruct((n_rows, c), F32),
                   jax.ShapeDtypeStruct((n_rows, c), BF16)],
        scratch_shapes=[pltpu.VMEM((tt, w), F32), pltpu.VMEM((SUBLANE, w), F32)],
        compiler_params=_params("arbitrary"))(proj, bdc, cdc, *tabs, dskip)


def _s5_bwd(dypre, du_skip, xs, proj, u_blk, bdc, cdc, tabs, seq, sb):
    n_rows = proj.shape[0]
    nb, blk, w_blk = bdc.shape
    c, w = nb * blk, nb * w_blk
    tt = _pick(seq, (SCAN_TILE,))
    nt = n_rows // tt
    tn = (((0,), (0,)), ((), ()))

    def kern(dy_ref, ds_ref, x_ref, u_ref, bd_ref, cd_ref, l1, l2, l4, pw, du_ref, da_ref, db_ref, dc_ref,
             gx_ref, lam_ref, car_ref, acc_ref):
        i = pl.program_id(0)

        @pl.when(((nt - i) * tt) % seq == 0)
        def _():
            car_ref[...] = jnp.zeros_like(car_ref)

        @pl.when(i == 0)
        def _():
            acc_ref[...] = jnp.zeros_like(acc_ref)
            db_ref[...] = jnp.zeros_like(db_ref)
            dc_ref[...] = jnp.zeros_like(dc_ref)

        for j in range(nb):
            gx_ref[:, pl.ds(j * w_blk, w_blk)] = lax.dot_general(dy_ref[:, pl.ds(j * blk, blk)], cd_ref[j], NT,
                                                                 preferred_element_type=F32)
        _scan_tile(gx_ref, lam_ref, (l1, l2, l4, pw), car_ref, sb, True, x_ref, acc_ref)
        for j in range(nb):
            cols, wide = pl.ds(j * blk, blk), pl.ds(j * w_blk, w_blk)
            lam = lam_ref[:, wide].astype(BF16)
            du_ref[:, cols] = ds_ref[:, cols] + lax.dot_general(lam, bd_ref[j], NT, preferred_element_type=F32)
            db_ref[j] += lax.dot_general(u_ref[:, cols].astype(BF16), lam, tn, preferred_element_type=F32)
            dc_ref[j] += lax.dot_general(x_ref[:, wide].astype(BF16), dy_ref[:, cols], tn, preferred_element_type=F32)

        @pl.when(i == nt - 1)
        def _():
            da_ref[...] = _colsum(acc_ref[...])

    back = lambda i: (nt - 1 - i, 0)
    tab = pl.BlockSpec((SUBLANE, w), lambda i: (0, 0))
    rows = pl.BlockSpec((tt, c), back)
    whole = lambda a: pl.BlockSpec(a.shape, lambda i: (0, 0, 0))
    return pl.pallas_call(
        kern, name="s5_bwd", grid=(nt,),
        in_specs=[rows, rows, pl.BlockSpec((tt, w), back), pl.BlockSpec((tt, c), lambda i: (nt - 1 - i, u_blk * blk // c)),
                  whole(bdc), whole(cdc), tab, tab, tab, tab],
        out_specs=[rows, pl.BlockSpec((1, w), lambda i: (0, 0)), whole(bdc), whole(cdc)],
        out_shape=[jax.ShapeDtypeStruct((n_rows, c), F32), jax.ShapeDtypeStruct((1, w), F32),
                   jax.ShapeDtypeStruct(bdc.shape, F32), jax.ShapeDtypeStruct(cdc.shape, F32)],
        scratch_shapes=[pltpu.VMEM((tt, w), F32), pltpu.VMEM((tt, w), F32), pltpu.VMEM((SUBLANE, w), F32),
                        pltpu.VMEM((SUBLANE, w), F32)],
        compiler_params=_params("arbitrary"))(dypre, du_skip, xs, proj, bdc, cdc, *tabs)


def _s5_post2(yg, q0, bg, og):
    c = yg.shape[1]

    def body(ins, outs, accs):
        ygv = ins[0][...].astype(F32)
        sg = ygv * _sigmoid(ins[1][...] + ins[2][...])
        outs[0][...] = (sg * _rms_r(sg) * ins[3][...]).astype(BF16)

    return _rowwise("s5_post2", body, yg.shape[0], [(yg, c, 0), (q0, c, 0)], [bg, og], [(c, BF16)], [])[0]


def _s5_post2_bwd(dmixed, yg, q0, bg, og):
    c = yg.shape[1]

    def body(ins, outs, accs):
        dsn, ygv = ins[0][...], ins[1][...].astype(F32)
        s = _sigmoid(ins[2][...] + ins[3][...])
        sg = ygv * s
        r = _rms_r(sg)
        accs[0][...] += _colsum(dsn * sg * r)
        dsg = _rms_bwd(sg, r, ins[4][...], dsn)
        dq = dsg * ygv * s * (1.0 - s)
        outs[0][...] = dq.astype(BF16)
        outs[1][...] = dsg * s
        accs[1][...] += _colsum(dq)

    return _rowwise("s5_post2_bwd", body, yg.shape[0], [(dmixed, c, 1), (yg, c, 0), (q0, c, 0)], [bg, og],
                    [(c, BF16), (c, F32)], [(1, c)] * 2)


def _s5_post1_bwd(dyg1, dyg2, ypre, proj, dskip, after=()):
    c = ypre.shape[1]

    def body(ins, outs, accs):
        dyp = (ins[0][...] + ins[1][...]) * _dgelu(ins[2][...])
        outs[0][...] = dyp.astype(BF16)
        outs[1][...] = dyp * ins[4][...]
        accs[0][...] += _colsum(dyp * ins[3][...])

    return _rowwise("s5_post1_bwd", body, ypre.shape[0], [(dyg1, c, 0), (dyg2, c, 0), (ypre, c, 0), (proj, c, 2)], [dskip],
                    [(c, BF16), (c, F32)], [(1, c)], after=after)


def _place():
    return lax.axis_index("x"), lax.axis_index("y"), lax.axis_index("c")


def _window(ref, axis, q, rows, cols):
    if axis == 0:
        return ref.at[pl.ds(pl.multiple_of(q * rows, SUBLANE), rows), :]
    return ref.at[:, pl.ds(pl.multiple_of(q * cols, LANE), cols)]


def _chip_copies(gather, srcs, lands, shards, axes, send_sems, recv_sems, local_sems):
    x, y, c = _place()
    me = 2 * x + y
    starts, waits = [], []
    for a, (src, land) in enumerate(zip(srcs, lands)):
        rows, cols = shards[a]
        if gather:
            own = pltpu.make_async_copy(src, _window(land, axes[a], me, rows, cols), local_sems.at[a])
        else:
            own = pltpu.make_async_copy(_window(src, axes[a], me, rows, cols), land.at[3], local_sems.at[a])
        starts.append(own)
        waits.append(own)
        for j, (fx, fy) in enumerate(CHIP_RELS):
            px, py = (1 - x) if fx else x, (1 - y) if fy else y
            peer = 2 * px + py
            on = dict(send_sem=send_sems.at[3 * a + j], recv_sem=recv_sems.at[3 * a + j], device_id=(px, py, c),
                      device_id_type=MESH)
            if gather:
                starts.append(pltpu.make_async_remote_copy(src_ref=src, dst_ref=_window(land, axes[a], me, rows, cols), **on))
                waits.append(pltpu.make_async_remote_copy(src_ref=src, dst_ref=_window(land, axes[a], peer, rows, cols), **on))
            else:
                cp = pltpu.make_async_remote_copy(src_ref=_window(src, axes[a], peer, rows, cols), dst_ref=land.at[j], **on)
                starts.append(cp)
                waits.append(cp)
    return starts, waits


HBM = pl.BlockSpec(memory_space=pltpu.HBM)
SEM = pl.BlockSpec(memory_space=pltpu.SEMAPHORE)


def _shard_shapes(gather, arrs, axes):
    if gather:
        return [a.shape for a in arrs]
    return [(a.shape[0] // N_CHIPS, a.shape[1]) if ax == 0 else (a.shape[0], a.shape[1] // N_CHIPS) for a, ax in zip(arrs, axes)]


def _exchange_start(name, gather, arrs, axes, after=()):
    n, n_after = len(arrs), len(after)
    shards = _shard_shapes(gather, arrs, axes)
    if gather:
        land_shapes = [(N_CHIPS * r, c) if ax == 0 else (r, N_CHIPS * c) for (r, c), ax in zip(shards, axes)]
    else:
        land_shapes = [(N_CHIPS,) + s for s in shards]
    lands = [lax.empty(s, a.dtype) for s, a in zip(land_shapes, arrs)]

    def kern(*refs):
        outs = refs[2 * n + n_after:]
        starts, _ = _chip_copies(gather, refs[:n], refs[n:2 * n], shards, axes, outs[0], outs[1], outs[2])
        for cp in starts:
            cp.start()
        outs[-1][...] = jnp.zeros_like(outs[-1])

    kept = [pltpu.HBM(a.shape, a.dtype) for a in arrs] + [pltpu.HBM(s, a.dtype) for s, a in zip(land_shapes, arrs)]
    res = pl.pallas_call(
        kern, name=name, in_specs=[HBM] * (2 * n) + [ANY] * n_after,
        out_specs=[SEM] * 3 + [HBM] * (2 * n) + [pl.BlockSpec(memory_space=pltpu.VMEM)],
        out_shape=[pltpu.SemaphoreType.DMA((3 * n,)), pltpu.SemaphoreType.DMA((3 * n,)), pltpu.SemaphoreType.DMA((n,))]
        + kept + [jax.ShapeDtypeStruct((SUBLANE, LANE), F32)],
        input_output_aliases={i: 3 + i for i in range(2 * n)},
        compiler_params=pltpu.CompilerParams(has_side_effects=pltpu.SideEffectType.DATAFLOW_SIDE_EFFECTING),
    )(*[pltpu.with_memory_space_constraint(a, pltpu.HBM) for a in list(arrs) + lands], *after)
    return res[:3], res[3:3 + n], res[3 + n:3 + 2 * n], res[-1]


def _exchange_wait(name, gather, started, axes, after):
    sems, srcs, lands, _ = started
    n, n_after = len(srcs), len(after)
    shards = _shard_shapes(gather, srcs, axes)

    def kern(*refs):
        sem_refs = refs[2 * n:2 * n + 3]
        _, waits = _chip_copies(gather, refs[:n], refs[n:2 * n], shards, axes, *sem_refs)
        for cp in waits:
            cp.wait()

    res = pl.pallas_call(
        kern, name=name, in_specs=[HBM] * (2 * n) + [SEM] * 3 + [ANY] * n_after, out_specs=[HBM] * (2 * n),
        out_shape=[pltpu.HBM(a.shape, a.dtype) for a in list(srcs) + list(lands)],
        input_output_aliases={i: i for i in range(2 * n)},
        compiler_params=pltpu.CompilerParams(has_side_effects=pltpu.SideEffectType.DATAFLOW_SIDE_EFFECTING),
    )(*srcs, *lands, *sems, *after)
    return res[n:]


def _swap_with_sibling(arrs):
    n = len(arrs)

    def kern(*refs):
        ins, outs = refs[:n], refs[n:2 * n]
        send_sems, recv_sems = refs[2 * n:]
        x, y, c = _place()
        copies = [pltpu.make_async_remote_copy(src_ref=ins[a], dst_ref=outs[a], send_sem=send_sems.at[a],
                                               recv_sem=recv_sems.at[a], device_id=(x, y, 1 - c), device_id_type=MESH)
                  for a in range(n)]
        for cp in copies:
            cp.start()
        for cp in copies:
            cp.wait()

    return pl.pallas_call(
        kern, name="swap_with_sibling", in_specs=[ANY] * n, out_specs=[ANY] * n,
        out_shape=[jax.ShapeDtypeStruct(a.shape, a.dtype) for a in arrs],
        scratch_shapes=[pltpu.SemaphoreType.DMA((n,)), pltpu.SemaphoreType.DMA((n,))],
    )(*arrs)


def _all_reduce_small(buf):
    rels = [(fx, fy, fc) for fx in (0, 1) for fy in (0, 1) for fc in (0, 1)][1:]
    n_dev = len(rels) + 1

    def kern(b_ref, o_ref, recv_ref, send_sems, recv_sems):
        x, y, c = _place()
        me = 4 * x + 2 * y + c
        copies = []
        for k, (fx, fy, fc) in enumerate(rels):
            peer = ((1 - x) if fx else x, (1 - y) if fy else y, (1 - c) if fc else c)
            cp = pltpu.make_async_remote_copy(src_ref=b_ref, dst_ref=recv_ref.at[me], send_sem=send_sems.at[k],
                                              recv_sem=recv_sems.at[k], device_id=peer, device_id_type=MESH)
            cp.start()
            copies.append((cp, peer))
        recv_ref[me] = b_ref[...]
        for k, (cp, (px, py, pc)) in enumerate(copies):
            cp.wait_send()
            pltpu.make_async_remote_copy(src_ref=b_ref, dst_ref=recv_ref.at[4 * px + 2 * py + pc], send_sem=send_sems.at[k],
                                         recv_sem=recv_sems.at[k], device_id=(px, py, pc), device_id_type=MESH).wait_recv()
        acc = recv_ref[0]
        for d in range(1, n_dev):
            acc = acc + recv_ref[d]
        o_ref[...] = acc

    vm = pl.BlockSpec(memory_space=pltpu.VMEM)
    return pl.pallas_call(
        kern, name="all_reduce_small", in_specs=[vm], out_specs=vm, out_shape=jax.ShapeDtypeStruct(buf.shape, F32),
        scratch_shapes=[pltpu.VMEM((n_dev,) + buf.shape, F32), pltpu.SemaphoreType.DMA((n_dev - 1,)),
                        pltpu.SemaphoreType.DMA((n_dev - 1,))],
        compiler_params=pltpu.CompilerParams(vmem_limit_bytes=VMEM_LIMIT_BYTES),
    )(buf)


def _sum_slots(name, parts):
    _, rows, cols = parts.shape
    tr = _pick(rows, (ROW_TILE, 128, 64, 32))

    def kern(p_ref, o_ref):
        o_ref[...] = ((p_ref[3].astype(F32) + p_ref[0].astype(F32)) + p_ref[1].astype(F32)) + p_ref[2].astype(F32)

    return pl.pallas_call(kern, name=name, grid=(rows // tr,),
                          in_specs=[pl.BlockSpec((N_CHIPS, tr, cols), lambda i: (0, i, 0))],
                          out_specs=pl.BlockSpec((tr, cols), lambda i: (i, 0)),
                          out_shape=jax.ShapeDtypeStruct((rows, cols), F32), compiler_params=_params("arbitrary"))(parts)


def _adamw_math(g, w, m, v):
    m2 = ADAM_B1 * m + (1.0 - ADAM_B1) * g
    v2 = ADAM_B2 * v + (1.0 - ADAM_B2) * (g * g)
    m_hat = m2 / (1.0 - ADAM_B1 ** ADAM_STEP)
    v_hat = v2 / (1.0 - ADAM_B2 ** ADAM_STEP)
    return -ADAM_LR * (m_hat / (jnp.sqrt(v_hat) + ADAM_EPS) + ADAM_WD * w), m2, v2


def _adamw(name, parts, w, m, v):
    rows, cols = w.shape
    tr = rows if rows * cols <= WHOLE_ELEMS else _pick(rows, (ROW_TILE, 352, 128, 64, 32, 8))
    n = len(parts)

    def kern(*refs):
        g = refs[0][:, pl.ds(0, cols)]
        for p in refs[1:n]:
            g = g + p[:, pl.ds(0, cols)]
        d, m2, v2 = _adamw_math(g, refs[n][...], refs[n + 1][...], refs[n + 2][...])
        refs[n + 3][...] = g
        refs[n + 4][...] = d
        refs[n + 5][...] = m2
        refs[n + 6][...] = v2

    spec = pl.BlockSpec((tr, cols), lambda i: (i, 0))
    return pl.pallas_call(kern, name=name, grid=(rows // tr,),
                          in_specs=[pl.BlockSpec((tr, p.shape[1]), lambda i: (i, 0)) for p in parts] + [spec] * 3,
                          out_specs=[spec] * 4, out_shape=[jax.ShapeDtypeStruct((rows, cols), F32)] * 4,
                          compiler_params=_params("arbitrary"))(*parts, w, m, v)


def _pack(arrs):
    parts, rows = [], []
    for a in arrs:
        r = _round_up(-(-a.size // LANE), SUBLANE)
        parts.append(jnp.pad(a.reshape(-1).astype(F32), (0, r * LANE - a.size)).reshape(r, LANE))
        rows.append(r)
    return jnp.concatenate(parts, axis=0), rows


def _unpack(buf, rows, shapes):
    out, r0 = [], 0
    for r, s in zip(rows, shapes):
        size = math.prod(s)
        out.append(buf[r0:r0 + r].reshape(-1)[:size].reshape(s))
        r0 += r
    return out


def kernel(x, norm_ffn1, ffn1_w1, ffn1_w3, ffn1_w2, norm_mix, w_in, conv_w, conv_b, conv_ln_g, conv_ln_b, conv_out_g, ssm_A_re, ssm_A_im, ssm_log_dt, ssm_B_re, ssm_B_im, ssm_C_re, ssm_C_im, ssm_D, ssm_glu_w, ssm_glu_b, ssm_out_g, w_out, norm_ffn2, ffn2_w1, ffn2_w3, ffn2_w2, norm_final, loss_target, m_norm_ffn1, m_ffn1_w1, m_ffn1_w3, m_ffn1_w2, m_norm_mix, m_w_in, m_conv_w, m_conv_b, m_conv_ln_g, m_conv_ln_b, m_conv_out_g, m_ssm_A_re, m_ssm_A_im, m_ssm_log_dt, m_ssm_B_re, m_ssm_B_im, m_ssm_C_re, m_ssm_C_im, m_ssm_D, m_ssm_glu_w, m_ssm_glu_b, m_ssm_out_g, m_w_out, m_norm_ffn2, m_ffn2_w1, m_ffn2_w3, m_ffn2_w2, m_norm_final, v_norm_ffn1, v_ffn1_w1, v_ffn1_w3, v_ffn1_w2, v_norm_mix, v_w_in, v_conv_w, v_conv_b, v_conv_ln_g, v_conv_ln_b, v_conv_out_g, v_ssm_A_re, v_ssm_A_im, v_ssm_log_dt, v_ssm_B_re, v_ssm_B_im, v_ssm_C_re, v_ssm_C_im, v_ssm_D, v_ssm_glu_w, v_ssm_glu_b, v_ssm_out_g, v_w_out, v_norm_ffn2, v_ffn2_w1, v_ffn2_w3, v_ffn2_w2, v_norm_final):
    given = dict(locals())
    wts = {n: given[n] for n in WEIGHTS}
    n_seq, seq, d = x.shape
    n_rows = n_seq * seq
    xf = x.reshape(n_rows, d)
    tgt = loss_target.reshape(n_rows, d)
    row = lambda a: a.reshape(1, -1)

    f = ffn1_w1.shape[-1]
    fp = _round_up(f, LANE)
    shards = []
    for n in BIG:
        s = wts[n][0].astype(BF16)
        if n.endswith('_w1') or n.endswith('_w3'):
            s = jnp.pad(s, ((0, 0), (0, fp - f)))
        elif n.endswith('_w2'):
            s = jnp.pad(s, ((0, fp - f), (0, 0)))
        shards.append(s)
    n_taps, c_shard = conv_w.shape[1], conv_w.shape[2]
    shards.append(jnp.pad(conv_w[0], ((0, HALO - n_taps), (0, 0))))
    shard_of = dict(zip(BIG + ['conv_w'], shards))
    axis_of = dict(BIG_AXIS, conv_w=1)
    groups = [['ffn1_w1', 'ffn1_w3'], ['ffn1_w2', 'w_in', 'conv_w', 'ssm_glu_w', 'w_out'], ['ffn2_w1', 'ffn2_w3', 'ffn2_w2']]
    fetch, tok = [], []
    for k, names in enumerate(groups):
        fetch.append(_exchange_start("gather%d_send" % k, True, [shard_of[n] for n in names], [axis_of[n] for n in names], tok))
        tok = [fetch[-1][3]]
    full = {}

    def arrive(k, after):
        lands = _exchange_wait("gather%d_recv" % k, True, fetch[k], [axis_of[n] for n in groups[k]], after)
        full.update(zip(groups[k], lands))

    arrive(0, tok)

    _, n_grp, n_state = ssm_A_re.shape
    grp = ssm_B_re.shape[-1]
    ns = n_grp * n_state
    c_ssm = n_grp * grp
    lr, li = ssm_A_re.reshape(1, ns), ssm_A_im.reshape(1, ns)
    ldt = jnp.repeat(ssm_log_dt.reshape(n_grp), n_state).reshape(1, ns)
    btr = ssm_B_re[0].transpose(2, 0, 1).reshape(grp, ns)
    bti = ssm_B_im[0].transpose(2, 0, 1).reshape(grp, ns)
    ctr = ssm_C_re[0].transpose(1, 0, 2).reshape(grp, ns)
    cti = ssm_C_im[0].transpose(1, 0, 2).reshape(grp, ns)
    _, _, bbr, bbi, pw = _s5_params_fwd(lr, li, ldt, btr, bti)
    nb = c_ssm // LANE
    sb, gpb = ns // nb, n_grp // nb
    diag = (jnp.arange(LANE)[:, None] // grp) == (jnp.arange(sb)[None, :] // n_state)

    def spread(t):
        return jnp.where(diag, jnp.tile(t.reshape(grp, nb, sb).transpose(1, 0, 2), (1, gpb, 1)), 0.0)

    def gather_diag(t):
        return (t * diag).reshape(nb, gpb, grp, sb).sum(1).transpose(1, 0, 2).reshape(grp, ns)

    def interleave(re, im):
        return jnp.stack([re.reshape(-1, nb, sb), im.reshape(-1, nb, sb)], axis=2).reshape(-1, 2 * ns)

    bdc = jnp.concatenate([spread(bbr), spread(bbi)], axis=2).astype(BF16)
    cdc = jnp.concatenate([spread(ctr).transpose(0, 2, 1), -spread(cti).transpose(0, 2, 1)], axis=1).astype(BF16)
    rowi = jnp.arange(SUBLANE)[:, None]
    pwf, pwc = interleave(pw[:, :ns], pw[:, ns:]), interleave(pw[:, :ns], -pw[:, ns:])
    tabs_f = [jnp.where(rowi >= s, pwf[s - 1][None, :], 0.0) for s in (1, 2, 4)] + [pwf]
    tabs_b = [jnp.where(rowi <= SUBLANE - 1 - s, pwc[s - 1][None, :], 0.0) for s in (1, 2, 4)] + [pwc[::-1]]
    c_conv = conv_b.shape[1]
    u_blk = 2 * c_conv // LANE

    h1 = _rms_fwd("ffn1_rms", xf, norm_ffn1)
    a1, b1, z1 = _ffn_up("ffn1_up", h1, full['ffn1_w1'], full['ffn1_w3'])
    arrive(1, [z1])
    x1 = _mm("ffn1_down", z1, full['ffn1_w2'], 1, 0, F32, addend=xf, alpha=0.5)
    saved1 = (h1, a1, b1, z1)
    cw = full['conv_w']
    h2 = _rms_fwd("mix_rms", x1, norm_mix)
    proj = _mm("mix_in", h2, full['w_in'], 1, 0, F32)
    assert c_conv == c_ssm and proj.shape[1] == 3 * c_conv
    cpre, an = _conv_fwd(proj, cw, conv_b, conv_ln_g, conv_ln_b, conv_out_g, seq)
    xs, ypre, yg = _s5_fwd(proj, u_blk, bdc, cdc, tabs_f, ssm_D, seq, sb)
    q0 = _mm("s5_gate", yg, full['ssm_glu_w'], 1, 0, F32)
    sn = _s5_post2(yg, q0, ssm_glu_b, ssm_out_g)
    wo = full['w_out']
    x2 = _mm("mix_out_a", an, wo[:c_conv], 1, 0, F32, addend=x1)
    x2 = _mm("mix_out_s", sn, wo[c_conv:], 1, 0, F32, addend=x2)
    arrive(2, [x2])
    x3, saved2 = _ffn_fwd("ffn2", x2, norm_ffn2, full['ffn2_w1'], full['ffn2_w3'], full['ffn2_w2'])
    dx3, loss_row, d_norm_final = _loss_head(x3, row(norm_final), tgt)

    g = {}
    dx2, g['norm_ffn2'], sent = _ffn_bwd("ffn2", x2, norm_ffn2, full['ffn2_w1'], full['ffn2_w3'], full['ffn2_w2'], saved2, dx3)
    dmixed = _mm("mix_dmixed", dx2, wo, 1, 1, F32)
    dwo = jnp.concatenate([_mm("mix_dwo_a", an, dx2, 0, 0, BF16), _mm("mix_dwo_s", sn, dx2, 0, 0, BF16)], axis=0)
    dq, dyg1, g['ssm_out_g'], g['ssm_glu_b'] = _s5_post2_bwd(dmixed, yg, q0, ssm_glu_b, ssm_out_g)
    dyg2 = _mm("s5_dgate", dq, full['ssm_glu_w'], 1, 1, F32)
    dwg = _mm("s5_dwg", yg, dq, 0, 0, BF16)
    sent['w_out ssm_glu_w'] = (_exchange_start("mix_wo_wg_send", False, [dwo, dwg], [0, 0]), [0, 0])
    dypre, du_skip, g['ssm_D'] = _s5_post1_bwd(dyg1, dyg2, ypre, proj, ssm_D, after=[sent['w_out ssm_glu_w'][0][3]])
    du, dabar, dbdc, dcdc = _s5_bwd(dypre, du_skip, xs, proj, u_blk, bdc, cdc, tabs_b, seq, sb)
    dabar = dabar.reshape(nb, 2, sb)
    dlr, dli, dldt, dbtr, dbti = _s5_params_bwd(lr, li, ldt, btr, bti, dabar[:, 0].reshape(1, ns), dabar[:, 1].reshape(1, ns),
                                                gather_diag(dbdc[:, :, :sb]), gather_diag(dbdc[:, :, sb:]))
    g['ssm_A_re'], g['ssm_A_im'] = dlr, dli
    g['ssm_log_dt'] = dldt.reshape(n_grp, n_state).sum(axis=1)
    g['ssm_B_re'] = dbtr.reshape(grp, n_grp, n_state).transpose(1, 2, 0)
    g['ssm_B_im'] = dbti.reshape(grp, n_grp, n_state).transpose(1, 2, 0)
    g['ssm_C_re'] = gather_diag(dcdc[:, :sb].transpose(0, 2, 1)).reshape(grp, n_grp, n_state).transpose(1, 0, 2)
    g['ssm_C_im'] = -gather_diag(dcdc[:, sb:].transpose(0, 2, 1)).reshape(grp, n_grp, n_state).transpose(1, 0, 2)
    dc, g['conv_out_g'], g['conv_ln_g'], g['conv_ln_b'], g['conv_b'] = _conv_bwd_rows(dmixed, cpre, conv_ln_g, conv_ln_b,
                                                                                    conv_out_g)
    dval, dgate, dcw = _conv_bwd_taps(proj, dc, cw, seq)
    dproj = jnp.concatenate([dval, dgate, du], axis=1)
    sent['w_in'] = (_exchange_start("mix_win_send", False, [_mm("mix_dwin", h2, dproj, 0, 0, BF16)], [1]), [1])
    dh2 = _mm("mix_dh", dproj, full['w_in'], 1, 1, F32, after=[sent['w_in'][0][3]])
    dx1, g['norm_mix'] = _rms_bwd_res("mix_drms", x1, norm_mix, dh2, dx2)
    dx0, g['norm_ffn1'], sent1 = _ffn_bwd("ffn1", xf, norm_ffn1, full['ffn1_w1'], full['ffn1_w3'], full['ffn1_w2'], saved1, dx1)
    sent.update(sent1)
    g['norm_final'] = d_norm_final
    g['conv_w'] = dcw[:n_taps]

    small_shapes = [(n_taps, c_conv) if n == 'conv_w' else wts[n].shape for n in SMALL]
    buf, buf_rows = _pack([g[n] for n in SMALL] + [loss_row])
    total = _unpack(_all_reduce_small(buf), buf_rows, small_shapes + [(1, LANE)])
    loss = total[-1][0, 0]
    grads = dict(zip(SMALL, total[:-1]))
    chip = 2 * lax.axis_index("x") + lax.axis_index("y")
    grads['conv_w'] = lax.dynamic_slice_in_dim(grads['conv_w'], chip * c_shard, c_shard, axis=1)[None]
    g_buf, rows_s = _pack([grads[n] for n in SMALL])
    packed = [_pack([given[p + n] for n in SMALL])[0] for p in ('', 'm_', 'v_')]
    _, d_buf, m_buf, v_buf = _adamw("adamw_small", [g_buf], *packed)
    shapes_s = [wts[n].shape for n in SMALL]
    deltas = dict(zip(SMALL, _unpack(d_buf, rows_s, shapes_s)))
    new_m = dict(zip(SMALL, _unpack(m_buf, rows_s, shapes_s)))
    new_v = dict(zip(SMALL, _unpack(v_buf, rows_s, shapes_s)))

    slots = {}
    for names, (started, axes) in sent.items():
        lands = _exchange_wait(names.replace(' ', '_') + "_recv", False, started, axes, after=[dx0])
        slots.update(zip(names.split(), lands))
    sums = [_sum_slots("sum_" + n, slots[n]) for n in BIG]
    theirs = _swap_with_sibling(sums)
    for n, mine, other in zip(BIG, sums, theirs):
        grads[n], deltas[n], new_m[n], new_v[n] = (
            o[None] for o in _adamw("adamw_" + n, [mine, other], given[n][0], given['m_' + n][0], given['v_' + n][0]))

    return (loss, dx0.reshape(x.shape), *[grads[n] for n in WEIGHTS], *[deltas[n] for n in WEIGHTS],
            *[new_m[n] for n in WEIGHTS], *[new_v[n] for n in WEIGHTS])
```

```python
import math

import jax
import jax.numpy as jnp
from jax import lax
from jax.experimental import pallas as pl
from jax.experimental.pallas import tpu as pltpu

F32 = jnp.float32
BF16 = jnp.bfloat16
EPS = 1e-6
ADAM_LR, ADAM_B1, ADAM_B2, ADAM_EPS, ADAM_WD, ADAM_STEP = 0.001, 0.9, 0.999, 1e-08, 0.01, 10
MESH = pl.DeviceIdType.MESH
ANY = pl.BlockSpec(memory_space=pl.ANY)
LANE = 128
SUBLANE = 8
VMEM_LIMIT_BYTES = 56 << 20
ROW_TILE = 256
ROW_TILE_ELEMS = 256 * 1024
WHOLE_ELEMS = 512 * 1024
FFN_ROWS = 256
CONV_TILE = 128
CONV_SUB = 32
HALO = 32
SCAN_TILE = 128
SCAN_COLS = 512
N_CHIPS = 4
CHIP_RELS = ((1, 0), (0, 1), (1, 1))
NT = (((1,), (1,)), ((), ()))
GELU_K = math.sqrt(2.0 / math.pi)
GELU_C = 0.044715

WEIGHTS = ['norm_ffn1', 'ffn1_w1', 'ffn1_w3', 'ffn1_w2', 'norm_mix', 'w_in', 'conv_w', 'conv_b', 'conv_ln_g', 'conv_ln_b',
           'conv_out_g', 'ssm_A_re', 'ssm_A_im', 'ssm_log_dt', 'ssm_B_re', 'ssm_B_im', 'ssm_C_re', 'ssm_C_im', 'ssm_D',
           'ssm_glu_w', 'ssm_glu_b', 'ssm_out_g', 'w_out', 'norm_ffn2', 'ffn2_w1', 'ffn2_w3', 'ffn2_w2', 'norm_final']
BIG = ['ffn1_w1', 'ffn1_w3', 'ffn1_w2', 'w_in', 'ssm_glu_w', 'w_out', 'ffn2_w1', 'ffn2_w3', 'ffn2_w2']
BIG_AXIS = {'ffn1_w1': 1, 'ffn1_w3': 1, 'ffn1_w2': 0, 'w_in': 1, 'ssm_glu_w': 0, 'w_out': 0, 'ffn2_w1': 1, 'ffn2_w3': 1,
            'ffn2_w2': 0}
SMALL = [n for n in WEIGHTS if n not in BIG]


def _round_up(n, m):
    return -(-n // m) * m


def _pick(n, cands):
    for c in cands:
        if c <= n and n % c == 0:
            return c
    return n


def _params(*sem):
    return pltpu.CompilerParams(dimension_semantics=sem, vmem_limit_bytes=VMEM_LIMIT_BYTES)


def _rms_r(x):
    return lax.rsqrt(jnp.mean(x * x, axis=-1, keepdims=True) + EPS)


def _rms_bwd(x, r, g, dy):
    dyg = dy * g
    return r * dyg - x * (r * r * r) * jnp.mean(x * dyg, axis=-1, keepdims=True)


def _sigmoid(x):
    return jax.nn.sigmoid(x)


def _dsilu(a, s):
    return s * (1.0 + a * (1.0 - s))


def _gelu(x):
    return 0.5 * x * (1.0 + jnp.tanh(GELU_K * (x + GELU_C * x * x * x)))


def _dgelu(x):
    t = jnp.tanh(GELU_K * (x + GELU_C * x * x * x))
    return 0.5 * (1.0 + t) + 0.5 * x * (1.0 - t * t) * GELU_K * (1.0 + 3.0 * GELU_C * x * x)


def _colsum(v):
    return jnp.sum(v, axis=0, keepdims=True)


def _rowwise(name, body, n_rows, row_ins, par_ins, row_outs, acc_outs, after=()):
    widest = max([w for (_, w, _) in row_ins] + [w for (w, _) in row_outs])
    tt = _pick(n_rows, [t for t in (256, 128, 64, 32, 16, 8) if t * widest <= ROW_TILE_ELEMS])
    in_specs = [pl.BlockSpec((tt, w), lambda i, cb=cb: (i, cb)) for (_, w, cb) in row_ins]
    in_specs += [pl.BlockSpec(p.shape, lambda i: (0, 0)) for p in par_ins] + [ANY] * len(after)
    out_specs = [pl.BlockSpec((tt, w), lambda i: (i, 0)) for (w, _) in row_outs]
    out_specs += [pl.BlockSpec((r, w), lambda i: (0, 0)) for (r, w) in acc_outs]
    out_shape = [jax.ShapeDtypeStruct((n_rows, w), dt) for (w, dt) in row_outs]
    out_shape += [jax.ShapeDtypeStruct((r, w), F32) for (r, w) in acc_outs]
    n_in, n_ro = len(row_ins) + len(par_ins), len(row_outs)
    o0 = n_in + len(after)

    def kern(*refs):
        accs = refs[o0 + n_ro:]
        if accs:
            @pl.when(pl.program_id(0) == 0)
            def _():
                for a in accs:
                    a[...] = jnp.zeros_like(a)
        body(refs[:n_in], refs[o0:o0 + n_ro], accs)

    return pl.pallas_call(kern, name=name, grid=(n_rows // tt,), in_specs=in_specs, out_specs=out_specs, out_shape=out_shape,
                          compiler_params=_params("arbitrary"))(*[a for a, _, _ in row_ins], *par_ins, *after)


def _mm(name, a, b, ca, cb, out_dtype=F32, addend=None, alpha=1.0, a_cols=None, after=()):
    a_start, a_width = a_cols if a_cols else (0, a.shape[1])
    m, k = (a.shape[0], a_width) if ca == 1 else (a_width, a.shape[0])
    n = b.shape[1 - cb]
    assert b.shape[cb] == k, (name, a.shape, b.shape)
    tm = _pick(m, (1024, 512, 256, 128))
    tn = _pick(n, (1024, 768, 512, 384, 256, 128))
    tk = _pick(k, (1024, 768, 512, 256, 128))
    nk = k // tk
    if ca == 1:
        assert a_start % tk == 0
        a_spec = pl.BlockSpec((tm, tk), lambda i, j, kk: (i, kk + a_start // tk))
    else:
        assert a_start % tm == 0
        a_spec = pl.BlockSpec((tk, tm), lambda i, j, kk: (kk, i + a_start // tm))
    b_spec = pl.BlockSpec((tk, tn), lambda i, j, kk: (kk, j)) if cb == 0 else pl.BlockSpec((tn, tk), lambda i, j, kk: (j, kk))
    o_spec = pl.BlockSpec((tm, tn), lambda i, j, kk: (i, j))
    ins, in_specs = [a, b], [a_spec, b_spec]
    if addend is not None:
        ins.append(addend)
        in_specs.append(o_spec)
    ins += list(after)
    in_specs += [ANY] * len(after)
    dims = (((ca,), (cb,)), ((), ()))

    def finish(refs, r):
        if alpha != 1.0:
            r = r * alpha
        if addend is not None:
            r = r + refs[2][...].astype(F32)
        return r.astype(out_dtype)

    def kern_one(*refs):
        refs[-1][...] = finish(refs, lax.dot_general(refs[0][...].astype(BF16), refs[1][...].astype(BF16), dims,
                                                     preferred_element_type=F32))

    def kern_acc(*refs):
        o_ref, acc_ref = refs[-2], refs[-1]
        kk = pl.program_id(2)

        @pl.when(kk == 0)
        def _():
            acc_ref[...] = jnp.zeros_like(acc_ref)

        acc_ref[...] += lax.dot_general(refs[0][...].astype(BF16), refs[1][...].astype(BF16), dims,
                                        preferred_element_type=F32)

        @pl.when(kk == nk - 1)
        def _():
            o_ref[...] = finish(refs, acc_ref[...])

    return pl.pallas_call(kern_one if nk == 1 else kern_acc, name=name, grid=(m // tm, n // tn, nk), in_specs=in_specs,
                          out_specs=o_spec, out_shape=jax.ShapeDtypeStruct((m, n), out_dtype),
                          scratch_shapes=[] if nk == 1 else [pltpu.VMEM((tm, tn), F32)],
                          compiler_params=_params("arbitrary", "arbitrary", "arbitrary"))(*ins)


def _rms_fwd(name, x, g):
    def body(ins, outs, accs):
        xv = ins[0][...]
        outs[0][...] = (xv * _rms_r(xv) * ins[1][...]).astype(BF16)

    return _rowwise(name, body, x.shape[0], [(x, x.shape[1], 0)], [g], [(x.shape[1], BF16)], [])[0]


def _rms_bwd_res(name, x, g, dh, dres):
    d = x.shape[1]

    def body(ins, outs, accs):
        xv, dhv, gv = ins[0][...], ins[1][...], ins[3][...]
        r = _rms_r(xv)
        outs[0][...] = ins[2][...] + _rms_bwd(xv, r, gv, dhv)
        accs[0][...] += _colsum(dhv * xv * r)

    return _rowwise(name, body, x.shape[0], [(x, d, 0), (dh, d, 0), (dres, d, 0)], [g], [(d, F32)], [(1, d)])


def _ffn_up(name, h, w1, w3):
    t, d = h.shape
    ff = w1.shape[1]
    tm, tn = _pick(t, (1024, 512, 256, 128)), _pick(ff, (1024, 768, 512, 256, 128))

    def kern(h_ref, w1_ref, w3_ref, a_ref, b_ref, z_ref):
        hv = h_ref[...]
        a = jnp.dot(hv, w1_ref[...], preferred_element_type=F32)
        b = jnp.dot(hv, w3_ref[...], preferred_element_type=F32)
        a_ref[...] = a.astype(BF16)
        b_ref[...] = b.astype(BF16)
        z_ref[...] = (a * _sigmoid(a) * b).astype(BF16)

    w_spec = pl.BlockSpec((d, tn), lambda i, j: (0, j))
    o_spec = pl.BlockSpec((tm, tn), lambda i, j: (i, j))
    return pl.pallas_call(kern, name=name, grid=(t // tm, ff // tn),
                          in_specs=[pl.BlockSpec((tm, d), lambda i, j: (i, 0)), w_spec, w_spec], out_specs=[o_spec] * 3,
                          out_shape=[jax.ShapeDtypeStruct((t, ff), BF16)] * 3,
                          compiler_params=_params("arbitrary", "arbitrary"))(h, w1, w3)


def _ffn_dglu(name, dxo, w2, a, b, after=()):
    t, d = dxo.shape
    ff = w2.shape[0]
    tm = _pick(t, (FFN_ROWS, 128))

    def kern(dx_ref, w2_ref, a_ref, b_ref, *rest):
        da_ref, db_ref = rest[-2:]
        dz = lax.dot_general(dx_ref[...].astype(BF16), w2_ref[...], NT, preferred_element_type=F32) * 0.5
        av, bv = a_ref[...].astype(F32), b_ref[...].astype(F32)
        s = _sigmoid(av)
        da_ref[...] = (dz * bv * _dsilu(av, s)).astype(BF16)
        db_ref[...] = (dz * av * s).astype(BF16)

    o_spec = pl.BlockSpec((tm, ff), lambda i: (i, 0))
    return pl.pallas_call(kern, name=name, grid=(t // tm,),
                          in_specs=[pl.BlockSpec((tm, d), lambda i: (i, 0)), pl.BlockSpec((ff, d), lambda i: (0, 0)),
                                    o_spec, o_spec] + [ANY] * len(after),
                          out_specs=[o_spec] * 2, out_shape=[jax.ShapeDtypeStruct((t, ff), BF16)] * 2,
                          compiler_params=_params("arbitrary"))(dxo, w2, a, b, *after)


def _ffn_dh(name, da, db, w1, w3, x, g, dres, after=()):
    t, d = x.shape
    ff = da.shape[1]
    tm = _pick(t, (FFN_ROWS, 128))

    def kern(da_ref, db_ref, w1_ref, w3_ref, x_ref, g_ref, dres_ref, *rest):
        dx_ref, dg_ref = rest[-2:]

        @pl.when(pl.program_id(0) == 0)
        def _():
            dg_ref[...] = jnp.zeros_like(dg_ref)

        dh = (lax.dot_general(da_ref[...], w1_ref[...], NT, preferred_element_type=F32)
              + lax.dot_general(db_ref[...], w3_ref[...], NT, preferred_element_type=F32))
        xv = x_ref[...]
        r = _rms_r(xv)
        dx_ref[...] = dres_ref[...] + _rms_bwd(xv, r, g_ref[...], dh)
        dg_ref[...] += _colsum(dh * xv * r)

    act = pl.BlockSpec((tm, ff), lambda i: (i, 0))
    wgt = pl.BlockSpec((d, ff), lambda i: (0, 0))
    rows = pl.BlockSpec((tm, d), lambda i: (i, 0))
    gain = pl.BlockSpec((1, d), lambda i: (0, 0))
    return pl.pallas_call(kern, name=name, grid=(t // tm,),
                          in_specs=[act, act, wgt, wgt, rows, gain, rows] + [ANY] * len(after), out_specs=[rows, gain],
                          out_shape=[jax.ShapeDtypeStruct((t, d), F32), jax.ShapeDtypeStruct((1, d), F32)],
                          compiler_params=_params("arbitrary"))(da, db, w1, w3, x, g, dres, *after)


def _loss_head(x3, gf, tgt):
    d = x3.shape[1]

    def body(ins, outs, accs):
        xv, tv, gv = ins[0][...], ins[1][...], ins[2][...]
        r = _rms_r(xv)
        e = xv * r * gv - tv
        sq = jnp.sum(jnp.sum(e * e, axis=-1, keepdims=True), axis=0, keepdims=True)
        accs[0][...] += jnp.broadcast_to(sq * (0.5 / d), (1, LANE))
        dy = e * (1.0 / d)
        outs[0][...] = _rms_bwd(xv, r, gv, dy)
        accs[1][...] += _colsum(dy * xv * r)

    return _rowwise("loss_head", body, x3.shape[0], [(x3, d, 0), (tgt, d, 0)], [gf], [(d, F32)], [(1, LANE), (1, d)])


def _ffn_fwd(tag, x, g, w1, w3, w2):
    h = _rms_fwd(tag + "_rms", x, g)
    a, b, z = _ffn_up(tag + "_up", h, w1, w3)
    return _mm(tag + "_down", z, w2, 1, 0, F32, addend=x, alpha=0.5), (h, a, b, z)


def _ffn_bwd(tag, x, g, w1, w3, w2, saved, dxo):
    h, a, b, z = saved
    dw2 = _mm(tag + "_dw2", z, dxo, 0, 0, BF16, alpha=0.5)
    s2 = _exchange_start(tag + "_w2_send", False, [dw2], [0])
    da, db = _ffn_dglu(tag + "_dglu", dxo, w2, a, b, after=[s2[3]])
    dw1 = _mm(tag + "_dw1", h, da, 0, 0, BF16)
    s1 = _exchange_start(tag + "_w1_send", False, [dw1], [1])
    dw3 = _mm(tag + "_dw3", h, db, 0, 0, BF16, after=[s1[3]])
    s3 = _exchange_start(tag + "_w3_send", False, [dw3], [1])
    dx, dg = _ffn_dh(tag + "_dh", da, db, w1, w3, x, g, dxo, after=[s3[3]])
    return dx, dg, {tag + "_w1": (s1, [1]), tag + "_w3": (s3, [1]), tag + "_w2": (s2, [0])}


def _shift_copies(ext_ref, sh_ref):
    n = ext_ref.shape[0] - SUBLANE
    for r in range(1, SUBLANE):
        sh_ref[r, pl.ds(0, n), :] = ext_ref[pl.ds(r, n), :]


def _rows_at(ext_ref, sh_ref, off, rows):
    r = off % SUBLANE
    return ext_ref[pl.ds(off, rows), :] if r == 0 else sh_ref[r, pl.ds(off - r, rows), :]


def _conv_fwd(proj, cw, cb, lng, lnb, og, seq):
    n_rows, c = proj.shape[0], cb.shape[1]
    kw = HALO - 1
    tt = _pick(seq, (CONV_TILE,))
    hb = tt // HALO

    def kern(v_ref, g_ref, vp_ref, gp_ref, w_ref, cb_ref, lg_ref, lb_ref, og_ref, c_ref, an_ref, ext_ref, sh_ref):
        first = (pl.program_id(0) * tt) % seq == 0
        ext_ref[pl.ds(HALO, tt), :] = v_ref[...] * _sigmoid(g_ref[...])
        ext_ref[pl.ds(0, HALO), :] = vp_ref[...] * _sigmoid(gp_ref[...]) * jnp.where(first, 0.0, 1.0)
        _shift_copies(ext_ref, sh_ref)
        for r0 in range(0, tt, CONV_SUB):
            rows = min(CONV_SUB, tt - r0)
            acc = jnp.zeros((rows, c), F32)
            for k in range(kw):
                acc = acc + w_ref[pl.ds(k, 1), :] * _rows_at(ext_ref, sh_ref, r0 + HALO - (kw - 1) + k, rows)
            c_ref[pl.ds(r0, rows), :] = acc + cb_ref[...]
        cv = c_ref[...]
        mu = jnp.mean(cv, axis=-1, keepdims=True)
        xc = cv - mu
        rstd = lax.rsqrt(jnp.mean(xc * xc, axis=-1, keepdims=True) + EPS)
        lv = xc * rstd * lg_ref[...] + lb_ref[...]
        sl = lv * _sigmoid(lv)
        an_ref[...] = (sl * _rms_r(sl) * og_ref[...]).astype(BF16)

    cur = lambda cbk: pl.BlockSpec((tt, c), lambda i: (i, cbk))
    prev = lambda cbk: pl.BlockSpec((HALO, c), lambda i: (jnp.maximum(i * hb - 1, 0), cbk))
    par = lambda p: pl.BlockSpec(p.shape, lambda i: (0, 0))
    return pl.pallas_call(
        kern, name="conv_fwd", grid=(n_rows // tt,),
        in_specs=[cur(0), cur(1), prev(0), prev(1), par(cw), par(cb), par(lng), par(lnb), par(og)],
        out_specs=[pl.BlockSpec((tt, c), lambda i: (i, 0))] * 2,
        out_shape=[jax.ShapeDtypeStruct((n_rows, c), F32), jax.ShapeDtypeStruct((n_rows, c), BF16)],
        scratch_shapes=[pltpu.VMEM((tt + HALO, c), F32), pltpu.VMEM((SUBLANE, tt + HALO, c), F32)],
        compiler_params=_params("arbitrary"),
    )(proj, proj, proj, proj, cw, cb, lng, lnb, og)


def _conv_bwd_rows(dmixed, cpre, lng, lnb, og):
    c = cpre.shape[1]

    def body(ins, outs, accs):
        dan, cv, lg, lb, ogv = ins[0][...], ins[1][...], ins[2][...], ins[3][...], ins[4][...]
        mu = jnp.mean(cv, axis=-1, keepdims=True)
        xc = cv - mu
        rstd = lax.rsqrt(jnp.mean(xc * xc, axis=-1, keepdims=True) + EPS)
        xh = xc * rstd
        lv = xh * lg + lb
        s = _sigmoid(lv)
        sl = lv * s
        r2 = _rms_r(sl)
        accs[0][...] += _colsum(dan * sl * r2)
        dl = _rms_bwd(sl, r2, ogv, dan) * _dsilu(lv, s)
        accs[1][...] += _colsum(dl * xh)
        accs[2][...] += _colsum(dl)
        dxh = dl * lg
        dc = rstd * (dxh - jnp.mean(dxh, axis=-1, keepdims=True) - xh * jnp.mean(dxh * xh, axis=-1, keepdims=True))
        outs[0][...] = dc
        accs[3][...] += _colsum(dc)

    return _rowwise("conv_bwd_rows", body, cpre.shape[0], [(dmixed, c, 0), (cpre, c, 0)], [lng, lnb, og], [(c, F32)],
                    [(1, c)] * 4)


def _conv_bwd_taps(proj, dc, cw, seq):
    n_rows, c = dc.shape
    kw = HALO - 1
    tt = _pick(seq, (CONV_TILE,))
    hb = tt // HALO
    last_blk = n_rows // HALO - 1

    def kern(v_ref, g_ref, vp_ref, gp_ref, dc_ref, dn_ref, w_ref, dv_ref, dg_ref, dw_ref, exta_ref, extd_ref, sha_ref, shd_ref):
        i = pl.program_id(0)
        first = (i * tt) % seq == 0
        last = ((i + 1) * tt) % seq == 0

        @pl.when(i == 0)
        def _():
            dw_ref[...] = jnp.zeros_like(dw_ref)

        sg = _sigmoid(g_ref[...])
        exta_ref[pl.ds(HALO, tt), :] = v_ref[...] * sg
        exta_ref[pl.ds(0, HALO), :] = vp_ref[...] * _sigmoid(gp_ref[...]) * jnp.where(first, 0.0, 1.0)
        dcv = dc_ref[...]
        extd_ref[pl.ds(0, tt), :] = dcv
        extd_ref[pl.ds(tt, HALO), :] = dn_ref[...] * jnp.where(last, 0.0, 1.0)
        _shift_copies(exta_ref, sha_ref)
        _shift_copies(extd_ref, shd_ref)
        for k in range(kw):
            dw_ref[pl.ds(k, 1), :] += _colsum(_rows_at(exta_ref, sha_ref, HALO - (kw - 1) + k, tt) * dcv)
        for r0 in range(0, tt, CONV_SUB):
            rows = min(CONV_SUB, tt - r0)
            acc = jnp.zeros((rows, c), F32)
            for k in range(kw):
                acc = acc + w_ref[pl.ds(k, 1), :] * _rows_at(extd_ref, shd_ref, r0 + (kw - 1) - k, rows)
            dv_ref[pl.ds(r0, rows), :] = acc
        da = dv_ref[...]
        dv_ref[...] = da * sg
        dg_ref[...] = da * v_ref[...] * sg * (1.0 - sg)

    cur = lambda cbk: pl.BlockSpec((tt, c), lambda i: (i, cbk))
    prev = lambda cbk: pl.BlockSpec((HALO, c), lambda i: (jnp.maximum(i * hb - 1, 0), cbk))
    nxt = pl.BlockSpec((HALO, c), lambda i: (jnp.minimum((i + 1) * hb, last_blk), 0))
    return pl.pallas_call(
        kern, name="conv_bwd_taps", grid=(n_rows // tt,),
        in_specs=[cur(0), cur(1), prev(0), prev(1), cur(0), nxt, pl.BlockSpec(cw.shape, lambda i: (0, 0))],
        out_specs=[cur(0), cur(0), pl.BlockSpec((HALO, c), lambda i: (0, 0))],
        out_shape=[jax.ShapeDtypeStruct((n_rows, c), F32), jax.ShapeDtypeStruct((n_rows, c), F32),
                   jax.ShapeDtypeStruct((HALO, c), F32)],
        scratch_shapes=[pltpu.VMEM((tt + HALO, c), F32)] * 2 + [pltpu.VMEM((SUBLANE, tt + HALO, c), F32)] * 2,
        compiler_params=_params("arbitrary"),
    )(proj, proj, proj, proj, dc, dc, cw)


def _s5_params_fwd(lr, li, ldt, btr, bti):
    ns = lr.shape[1]

    def kern(lr_ref, li_ref, ldt_ref, btr_ref, bti_ref, ar_ref, ai_ref, bbr_ref, bbi_ref, pw_ref):
        lrv, liv = lr_ref[...], li_ref[...]
        dt = jnp.exp(ldt_ref[...])
        zr, zi = lrv * dt, liv * dt
        mag = jnp.exp(zr)
        ar, ai = mag * jnp.cos(zi), mag * jnp.sin(zi)
        den = lrv * lrv + liv * liv
        nr = ar - 1.0
        cr = (nr * lrv + ai * liv) / den
        ci = (ai * lrv - nr * liv) / den
        ar_ref[...] = ar
        ai_ref[...] = ai
        bbr_ref[...] = cr * btr_ref[...] - ci * bti_ref[...]
        bbi_ref[...] = cr * bti_ref[...] + ci * btr_ref[...]
        pr, pi = ar, ai
        for e in range(SUBLANE):
            pw_ref[pl.ds(e, 1), pl.ds(0, ns)] = pr
            pw_ref[pl.ds(e, 1), pl.ds(ns, ns)] = pi
            pr, pi = pr * ar - pi * ai, pr * ai + pi * ar

    h = btr.shape[0]
    shapes = [jax.ShapeDtypeStruct((1, ns), F32)] * 2 + [jax.ShapeDtypeStruct((h, ns), F32)] * 2
    shapes += [jax.ShapeDtypeStruct((SUBLANE, 2 * ns), F32)]
    return pl.pallas_call(kern, name="s5_params_fwd", out_shape=shapes)(lr, li, ldt, btr, bti)


def _s5_params_bwd(lr, li, ldt, btr, bti, dar, dai, dbbr, dbbi):
    def kern(lr_ref, li_ref, ldt_ref, btr_ref, bti_ref, dar_ref, dai_ref, dbr_ref, dbi_ref,
             dlr_ref, dli_ref, dldt_ref, dbtr_ref, dbti_ref):
        lrv, liv = lr_ref[...], li_ref[...]
        dt = jnp.exp(ldt_ref[...])
        zr, zi = lrv * dt, liv * dt
        mag = jnp.exp(zr)
        ar, ai = mag * jnp.cos(zi), mag * jnp.sin(zi)
        den = lrv * lrv + liv * liv
        nr = ar - 1.0
        cr = (nr * lrv + ai * liv) / den
        ci = (ai * lrv - nr * liv) / den
        dbr, dbi, br, bi = dbr_ref[...], dbi_ref[...], btr_ref[...], bti_ref[...]
        dbtr_ref[...] = cr * dbr + ci * dbi
        dbti_ref[...] = cr * dbi - ci * dbr
        dcr = _colsum(br * dbr + bi * dbi)
        dci = _colsum(br * dbi - bi * dbr)
        ir, ii = lrv / den, -liv / den
        dnr = ir * dcr + ii * dci
        dni = ir * dci - ii * dcr
        wr, wi = cr * ir - ci * ii, cr * ii + ci * ir
        dl1r = -(wr * dcr + wi * dci)
        dl1i = -(wr * dci - wi * dcr)
        dtr, dti = dar_ref[...] + dnr, dai_ref[...] + dni
        dzr = ar * dtr + ai * dti
        dzi = ar * dti - ai * dtr
        dlr_ref[...] = dl1r + dt * dzr
        dli_ref[...] = dl1i + dt * dzi
        dldt_ref[...] = (dzr * lrv + dzi * liv) * dt

    ns, h = lr.shape[1], btr.shape[0]
    shapes = [jax.ShapeDtypeStruct((1, ns), F32)] * 3 + [jax.ShapeDtypeStruct((h, ns), F32)] * 2
    return pl.pallas_call(kern, name="s5_params_bwd", out_shape=shapes)(lr, li, ldt, btr, bti, dar, dai, dbbr, dbbi)


def _scan_tile(s_ref, o_ref, tabs, car_ref, sb, reverse, x_ref=None, acc_ref=None):
    l1, l2, l4, pw = tabs
    rows_t, w = s_ref.shape
    ng = rows_t // SUBLANE
    cw = _pick(sb, (SCAN_COLS,))
    carry_row = 0 if reverse else SUBLANE - 1
    row = lax.broadcasted_iota(jnp.int32, (SUBLANE, cw), 0)

    def group(gi, carry):
        g = (ng - 1 - gi) if reverse else gi
        rows = pl.ds(pl.multiple_of(g * SUBLANE, SUBLANE), SUBLANE)
        for c0 in [b0 + o for b0 in range(0, w, 2 * sb) for o in range(0, sb, cw)]:
            cr, ci = pl.ds(c0, cw), pl.ds(c0 + sb, cw)
            xr, xi = s_ref[rows, cr], s_ref[rows, ci]
            for s, lt in ((1, l1), (2, l2), (4, l4)):
                sh = (SUBLANE - s) if reverse else s
                sr, si = pltpu.roll(xr, sh, 0), pltpu.roll(xi, sh, 0)
                ar, ai = lt[:, cr], lt[:, ci]
                xr, xi = xr + ar * sr - ai * si, xi + ar * si + ai * sr
            kr, ki = car_ref[pl.ds(carry_row, 1), cr], car_ref[pl.ds(carry_row, 1), ci]
            pr, pi = pw[:, cr], pw[:, ci]
            xr, xi = xr + pr * kr - pi * ki, xi + pr * ki + pi * kr
            o_ref[rows, cr] = xr
            o_ref[rows, ci] = xi
            car_ref[:, cr] = xr
            car_ref[:, ci] = xi
            if acc_ref is not None:
                nr = jnp.where(row == SUBLANE - 1, kr, pltpu.roll(xr, SUBLANE - 1, 0))
                ni = jnp.where(row == SUBLANE - 1, ki, pltpu.roll(xi, SUBLANE - 1, 0))
                pxr, pxi = x_ref[rows, cr], x_ref[rows, ci]
                acc_ref[:, cr] += nr * pxr + ni * pxi
                acc_ref[:, ci] += ni * pxr - nr * pxi
        return carry

    lax.fori_loop(0, ng, group, 0)


def _s5_fwd(proj, u_blk, bdc, cdc, tabs, dskip, seq, sb):
    n_rows = proj.shape[0]
    nb, blk, w_blk = bdc.shape
    c, w = nb * blk, nb * w_blk
    tt = _pick(seq, (SCAN_TILE,))

    def kern(u_ref, bd_ref, cd_ref, l1, l2, l4, pw, d_ref, xs_ref, yp_ref, yg_ref, bu_ref, car_ref):
        @pl.when((pl.program_id(0) * tt) % seq == 0)
        def _():
            car_ref[...] = jnp.zeros_like(car_ref)

        for j in range(nb):
            bu_ref[:, pl.ds(j * w_blk, w_blk)] = jnp.dot(u_ref[:, pl.ds(j * blk, blk)].astype(BF16), bd_ref[j],
                                                         preferred_element_type=F32)
        _scan_tile(bu_ref, xs_ref, (l1, l2, l4, pw), car_ref, sb, False)
        for j in range(nb):
            cols = pl.ds(j * blk, blk)
            y0 = jnp.dot(xs_ref[:, pl.ds(j * w_blk, w_blk)].astype(BF16), cd_ref[j], preferred_element_type=F32)
            ypre = y0 + d_ref[:, cols] * u_ref[:, cols]
            yp_ref[:, cols] = ypre
            yg_ref[:, cols] = _gelu(ypre).astype(BF16)

    tab = pl.BlockSpec((SUBLANE, w), lambda i: (0, 0))
    rows = pl.BlockSpec((tt, c), lambda i: (i, 0))
    return pl.pallas_call(
        kern, name="s5_fwd", grid=(n_rows // tt,),
        in_specs=[pl.BlockSpec((tt, c), lambda i: (i, u_blk * blk // c)), pl.BlockSpec(bdc.shape, lambda i: (0, 0, 0)),
                  pl.BlockSpec(cdc.shape, lambda i: (0, 0, 0)), tab, tab, tab, tab, pl.BlockSpec((1, c), lambda i: (0, 0))],
        out_specs=[pl.BlockSpec((tt, w), lambda i: (i, 0)), rows, rows],
        out_shape=[jax.ShapeDtypeStruct((n_rows, w), F32), jax.ShapeDtypeStruct((n_rows, c), F32),
                   jax.ShapeDtypeStruct((n_rows, c), BF16)],
        scratch_shapes=[pltpu.VMEM((tt, w), F32), pltpu.VMEM((SUBLANE, w), F32)],
        compiler_params=_params("arbitrary"))(proj, bdc, cdc, *tabs, dskip)


def _s5_bwd(dypre, du_skip, xs, proj, u_blk, bdc, cdc, tabs, seq, sb):
    n_rows = proj.shape[0]
    nb, blk, w_blk = bdc.shape
    c, w = nb * blk, nb * w_blk
    tt = _pick(seq, (SCAN_TILE,))
    nt = n_rows // tt
    tn = (((0,), (0,)), ((), ()))

    def kern(dy_ref, ds_ref, x_ref, u_ref, bd_ref, cd_ref, l1, l2, l4, pw, du_ref, da_ref, db_ref, dc_ref,
             gx_ref, lam_ref, car_ref, acc_ref):
        i = pl.program_id(0)

        @pl.when(((nt - i) * tt) % seq == 0)
        def _():
            car_ref[...] = jnp.zeros_like(car_ref)

        @pl.when(i == 0)
        def _():
            acc_ref[...] = jnp.zeros_like(acc_ref)
            db_ref[...] = jnp.zeros_like(db_ref)
            dc_ref[...] = jnp.zeros_like(dc_ref)

        for j in range(nb):
            gx_ref[:, pl.ds(j * w_blk, w_blk)] = lax.dot_general(dy_ref[:, pl.ds(j * blk, blk)], cd_ref[j], NT,
                                                                 preferred_element_type=F32)
        _scan_tile(gx_ref, lam_ref, (l1, l2, l4, pw), car_ref, sb, True, x_ref, acc_ref)
        for j in range(nb):
            cols, wide = pl.ds(j * blk, blk), pl.ds(j * w_blk, w_blk)
            lam = lam_ref[:, wide].astype(BF16)
            du_ref[:, cols] = ds_ref[:, cols] + lax.dot_general(lam, bd_ref[j], NT, preferred_element_type=F32)
            db_ref[j] += lax.dot_general(u_ref[:, cols].astype(BF16), lam, tn, preferred_element_type=F32)
            dc_ref[j] += lax.dot_general(x_ref[:, wide].astype(BF16), dy_ref[:, cols], tn, preferred_element_type=F32)

        @pl.when(i == nt - 1)
        def _():
            da_ref[...] = _colsum(acc_ref[...])

    back = lambda i: (nt - 1 - i, 0)
    tab = pl.BlockSpec((SUBLANE, w), lambda i: (0, 0))
    rows = pl.BlockSpec((tt, c), back)
    whole = lambda a: pl.BlockSpec(a.shape, lambda i: (0, 0, 0))
    return pl.pallas_call(
        kern, name="s5_bwd", grid=(nt,),
        in_specs=[rows, rows, pl.BlockSpec((tt, w), back), pl.BlockSpec((tt, c), lambda i: (nt - 1 - i, u_blk * blk // c)),
                  whole(bdc), whole(cdc), tab, tab, tab, tab],
        out_specs=[rows, pl.BlockSpec((1, w), lambda i: (0, 0)), whole(bdc), whole(cdc)],
        out_shape=[jax.ShapeDtypeStruct((n_rows, c), F32), jax.ShapeDtypeStruct((1, w), F32),
                   jax.ShapeDtypeStruct(bdc.shape, F32), jax.ShapeDtypeStruct(cdc.shape, F32)],
        scratch_shapes=[pltpu.VMEM((tt, w), F32), pltpu.VMEM((tt, w), F32), pltpu.VMEM((SUBLANE, w), F32),
                        pltpu.VMEM((SUBLANE, w), F32)],
        compiler_params=_params("arbitrary"))(dypre, du_skip, xs, proj, bdc, cdc, *tabs)


def _s5_post2(yg, q0, bg, og):
    c = yg.shape[1]

    def body(ins, outs, accs):
        ygv = ins[0][...].astype(F32)
        sg = ygv * _sigmoid(ins[1][...] + ins[2][...])
        outs[0][...] = (sg * _rms_r(sg) * ins[3][...]).astype(BF16)

    return _rowwise("s5_post2", body, yg.shape[0], [(yg, c, 0), (q0, c, 0)], [bg, og], [(c, BF16)], [])[0]


def _s5_post2_bwd(dmixed, yg, q0, bg, og):
    c = yg.shape[1]

    def body(ins, outs, accs):
        dsn, ygv = ins[0][...], ins[1][...].astype(F32)
        s = _sigmoid(ins[2][...] + ins[3][...])
        sg = ygv * s
        r = _rms_r(sg)
        accs[0][...] += _colsum(dsn * sg * r)
        dsg = _rms_bwd(sg, r, ins[4][...], dsn)
        dq = dsg * ygv * s * (1.0 - s)
        outs[0][...] = dq.astype(BF16)
        outs[1][...] = dsg * s
        accs[1][...] += _colsum(dq)

    return _rowwise("s5_post2_bwd", body, yg.shape[0], [(dmixed, c, 1), (yg, c, 0), (q0, c, 0)], [bg, og],
                    [(c, BF16), (c, F32)], [(1, c)] * 2)


def _s5_post1_bwd(dyg1, dyg2, ypre, proj, dskip, after=()):
    c = ypre.shape[1]

    def body(ins, outs, accs):
        dyp = (ins[0][...] + ins[1][...]) * _dgelu(ins[2][...])
        outs[0][...] = dyp.astype(BF16)
        outs[1][...] = dyp * ins[4][...]
        accs[0][...] += _colsum(dyp * ins[3][...])

    return _rowwise("s5_post1_bwd", body, ypre.shape[0], [(dyg1, c, 0), (dyg2, c, 0), (ypre, c, 0), (proj, c, 2)], [dskip],
                    [(c, BF16), (c, F32)], [(1, c)], after=after)


def _place():
    return lax.axis_index("x"), lax.axis_index("y"), lax.axis_index("c")


def _window(ref, axis, q, rows, cols):
    if axis == 0:
        return ref.at[pl.ds(pl.multiple_of(q * rows, SUBLANE), rows), :]
    return ref.at[:, pl.ds(pl.multiple_of(q * cols, LANE), cols)]


def _chip_copies(gather, srcs, lands, shards, axes, send_sems, recv_sems, local_sems):
    x, y, c = _place()
    me = 2 * x + y
    starts, waits = [], []
    for a, (src, land) in enumerate(zip(srcs, lands)):
        rows, cols = shards[a]
        if gather:
            own = pltpu.make_async_copy(src, _window(land, axes[a], me, rows, cols), local_sems.at[a])
        else:
            own = pltpu.make_async_copy(_window(src, axes[a], me, rows, cols), land.at[3], local_sems.at[a])
        starts.append(own)
        waits.append(own)
        for j, (fx, fy) in enumerate(CHIP_RELS):
            px, py = (1 - x) if fx else x, (1 - y) if fy else y
            peer = 2 * px + py
            on = dict(send_sem=send_sems.at[3 * a + j], recv_sem=recv_sems.at[3 * a + j], device_id=(px, py, c),
                      device_id_type=MESH)
            if gather:
                starts.append(pltpu.make_async_remote_copy(src_ref=src, dst_ref=_window(land, axes[a], me, rows, cols), **on))
                waits.append(pltpu.make_async_remote_copy(src_ref=src, dst_ref=_window(land, axes[a], peer, rows, cols), **on))
            else:
                cp = pltpu.make_async_remote_copy(src_ref=_window(src, axes[a], peer, rows, cols), dst_ref=land.at[j], **on)
                starts.append(cp)
                waits.append(cp)
    return starts, waits


HBM = pl.BlockSpec(memory_space=pltpu.HBM)
SEM = pl.BlockSpec(memory_space=pltpu.SEMAPHORE)


def _shard_shapes(gather, arrs, axes):
    if gather:
        return [a.shape for a in arrs]
    return [(a.shape[0] // N_CHIPS, a.shape[1]) if ax == 0 else (a.shape[0], a.shape[1] // N_CHIPS) for a, ax in zip(arrs, axes)]


def _exchange_start(name, gather, arrs, axes, after=()):
    n, n_after = len(arrs), len(after)
    shards = _shard_shapes(gather, arrs, axes)
    if gather:
        land_shapes = [(N_CHIPS * r, c) if ax == 0 else (r, N_CHIPS * c) for (r, c), ax in zip(shards, axes)]
    else:
        land_shapes = [(N_CHIPS,) + s for s in shards]
    lands = [lax.empty(s, a.dtype) for s, a in zip(land_shapes, arrs)]

    def kern(*refs):
        outs = refs[2 * n + n_after:]
        starts, _ = _chip_copies(gather, refs[:n], refs[n:2 * n], shards, axes, outs[0], outs[1], outs[2])
        for cp in starts:
            cp.start()
        outs[-1][...] = jnp.zeros_like(outs[-1])

    kept = [pltpu.HBM(a.shape, a.dtype) for a in arrs] + [pltpu.HBM(s, a.dtype) for s, a in zip(land_shapes, arrs)]
    res = pl.pallas_call(
        kern, name=name, in_specs=[HBM] * (2 * n) + [ANY] * n_after,
        out_specs=[SEM] * 3 + [HBM] * (2 * n) + [pl.BlockSpec(memory_space=pltpu.VMEM)],
        out_shape=[pltpu.SemaphoreType.DMA((3 * n,)), pltpu.SemaphoreType.DMA((3 * n,)), pltpu.SemaphoreType.DMA((n,))]
        + kept + [jax.ShapeDtypeStruct((SUBLANE, LANE), F32)],
        input_output_aliases={i: 3 + i for i in range(2 * n)},
        compiler_params=pltpu.CompilerParams(has_side_effects=pltpu.SideEffectType.DATAFLOW_SIDE_EFFECTING),
    )(*[pltpu.with_memory_space_constraint(a, pltpu.HBM) for a in list(arrs) + lands], *after)
    return res[:3], res[3:3 + n], res[3 + n:3 + 2 * n], res[-1]


def _exchange_wait(name, gather, started, axes, after):
    sems, srcs, lands, _ = started
    n, n_after = len(srcs), len(after)
    shards = _shard_shapes(gather, srcs, axes)

    def kern(*refs):
        sem_refs = refs[2 * n:2 * n + 3]
        _, waits = _chip_copies(gather, refs[:n], refs[n:2 * n], shards, axes, *sem_refs)
        for cp in waits:
            cp.wait()

    res = pl.pallas_call(
        kern, name=name, in_specs=[HBM] * (2 * n) + [SEM] * 3 + [ANY] * n_after, out_specs=[HBM] * (2 * n),
        out_shape=[pltpu.HBM(a.shape, a.dtype) for a in list(srcs) + list(lands)],
        input_output_aliases={i: i for i in range(2 * n)},
        compiler_params=pltpu.CompilerParams(has_side_effects=pltpu.SideEffectType.DATAFLOW_SIDE_EFFECTING),
    )(*srcs, *lands, *sems, *after)
    return res[n:]


def _swap_with_sibling(arrs):
    n = len(arrs)

    def kern(*refs):
        ins, outs = refs[:n], refs[n:2 * n]
        send_sems, recv_sems = refs[2 * n:]
        x, y, c = _place()
        copies = [pltpu.make_async_remote_copy(src_ref=ins[a], dst_ref=outs[a], send_sem=send_sems.at[a],
                                               recv_sem=recv_sems.at[a], device_id=(x, y, 1 - c), device_id_type=MESH)
                  for a in range(n)]
        for cp in copies:
            cp.start()
        for cp in copies:
            cp.wait()

    return pl.pallas_call(
        kern, name="swap_with_sibling", in_specs=[ANY] * n, out_specs=[ANY] * n,
        out_shape=[jax.ShapeDtypeStruct(a.shape, a.dtype) for a in arrs],
        scratch_shapes=[pltpu.SemaphoreType.DMA((n,)), pltpu.SemaphoreType.DMA((n,))],
    )(*arrs)


def _all_reduce_small(buf):
    rels = [(fx, fy, fc) for fx in (0, 1) for fy in (0, 1) for fc in (0, 1)][1:]
    n_dev = len(rels) + 1

    def kern(b_ref, o_ref, recv_ref, send_sems, recv_sems):
        x, y, c = _place()
        me = 4 * x + 2 * y + c
        copies = []
        for k, (fx, fy, fc) in enumerate(rels):
            peer = ((1 - x) if fx else x, (1 - y) if fy else y, (1 - c) if fc else c)
            cp = pltpu.make_async_remote_copy(src_ref=b_ref, dst_ref=recv_ref.at[me], send_sem=send_sems.at[k],
                                              recv_sem=recv_sems.at[k], device_id=peer, device_id_type=MESH)
            cp.start()
            copies.append((cp, peer))
        recv_ref[me] = b_ref[...]
        for k, (cp, (px, py, pc)) in enumerate(copies):
            cp.wait_send()
            pltpu.make_async_remote_copy(src_ref=b_ref, dst_ref=recv_ref.at[4 * px + 2 * py + pc], send_sem=send_sems.at[k],
                                         recv_sem=recv_sems.at[k], device_id=(px, py, pc), device_id_type=MESH).wait_recv()
        acc = recv_ref[0]
        for d in range(1, n_dev):
            acc = acc + recv_ref[d]
        o_ref[...] = acc

    vm = pl.BlockSpec(memory_space=pltpu.VMEM)
    return pl.pallas_call(
        kern, name="all_reduce_small", in_specs=[vm], out_specs=vm, out_shape=jax.ShapeDtypeStruct(buf.shape, F32),
        scratch_shapes=[pltpu.VMEM((n_dev,) + buf.shape, F32), pltpu.SemaphoreType.DMA((n_dev - 1,)),
                        pltpu.SemaphoreType.DMA((n_dev - 1,))],
        compiler_params=pltpu.CompilerParams(vmem_limit_bytes=VMEM_LIMIT_BYTES),
    )(buf)


def _sum_slots(name, parts):
    _, rows, cols = parts.shape
    tr = _pick(rows, (ROW_TILE, 128, 64, 32))

    def kern(p_ref, o_ref):
        o_ref[...] = ((p_ref[3].astype(F32) + p_ref[0].astype(F32)) + p_ref[1].astype(F32)) + p_ref[2].astype(F32)

    return pl.pallas_call(kern, name=name, grid=(rows // tr,),
                          in_specs=[pl.BlockSpec((N_CHIPS, tr, cols), lambda i: (0, i, 0))],
                          out_specs=pl.BlockSpec((tr, cols), lambda i: (i, 0)),
                          out_shape=jax.ShapeDtypeStruct((rows, cols), F32), compiler_params=_params("arbitrary"))(parts)


def _adamw_math(g, w, m, v):
    m2 = ADAM_B1 * m + (1.0 - ADAM_B1) * g
    v2 = ADAM_B2 * v + (1.0 - ADAM_B2) * (g * g)
    m_hat = m2 / (1.0 - ADAM_B1 ** ADAM_STEP)
    v_hat = v2 / (1.0 - ADAM_B2 ** ADAM_STEP)
    return -ADAM_LR * (m_hat / (jnp.sqrt(v_hat) + ADAM_EPS) + ADAM_WD * w), m2, v2


def _adamw(name, parts, w, m, v):
    rows, cols = w.shape
    tr = rows if rows * cols <= WHOLE_ELEMS else _pick(rows, (ROW_TILE, 352, 128, 64, 32, 8))
    n = len(parts)

    def kern(*refs):
        g = refs[0][:, pl.ds(0, cols)]
        for p in refs[1:n]:
            g = g + p[:, pl.ds(0, cols)]
        d, m2, v2 = _adamw_math(g, refs[n][...], refs[n + 1][...], refs[n + 2][...])
        refs[n + 3][...] = g
        refs[n + 4][...] = d
        refs[n + 5][...] = m2
        refs[n + 6][...] = v2

    spec = pl.BlockSpec((tr, cols), lambda i: (i, 0))
    return pl.pallas_call(kern, name=name, grid=(rows // tr,),
                          in_specs=[pl.BlockSpec((tr, p.shape[1]), lambda i: (i, 0)) for p in parts] + [spec] * 3,
                          out_specs=[spec] * 4, out_shape=[jax.ShapeDtypeStruct((rows, cols), F32)] * 4,
                          compiler_params=_params("arbitrary"))(*parts, w, m, v)


def _adamw_many(name, gs, ws, ms, vs):
    n = len(gs)

    def kern(*refs):
        for p in range(n):
            d, m2, v2 = _adamw_math(refs[p][...], refs[n + p][...], refs[2 * n + p][...], refs[3 * n + p][...])
            refs[4 * n + p][...] = d
            refs[5 * n + p][...] = m2
            refs[6 * n + p][...] = v2

    res = pl.pallas_call(kern, name=name, out_shape=[jax.ShapeDtypeStruct(w.shape, F32) for w in ws] * 3,
                         compiler_params=pltpu.CompilerParams(vmem_limit_bytes=VMEM_LIMIT_BYTES))(*gs, *ws, *ms, *vs)
    return res[:n], res[n:2 * n], res[2 * n:]


def _pack(arrs):
    parts, rows = [], []
    for a in arrs:
        r = _round_up(-(-a.size // LANE), SUBLANE)
        parts.append(jnp.pad(a.reshape(-1).astype(F32), (0, r * LANE - a.size)).reshape(r, LANE))
        rows.append(r)
    return jnp.concatenate(parts, axis=0), rows


def _unpack(buf, rows, shapes):
    out, r0 = [], 0
    for r, s in zip(rows, shapes):
        size = math.prod(s)
        out.append(buf[r0:r0 + r].reshape(-1)[:size].reshape(s))
        r0 += r
    return out


def kernel(x, norm_ffn1, ffn1_w1, ffn1_w3, ffn1_w2, norm_mix, w_in, conv_w, conv_b, conv_ln_g, conv_ln_b, conv_out_g, ssm_A_re, ssm_A_im, ssm_log_dt, ssm_B_re, ssm_B_im, ssm_C_re, ssm_C_im, ssm_D, ssm_glu_w, ssm_glu_b, ssm_out_g, w_out, norm_ffn2, ffn2_w1, ffn2_w3, ffn2_w2, norm_final, loss_target, m_norm_ffn1, m_ffn1_w1, m_ffn1_w3, m_ffn1_w2, m_norm_mix, m_w_in, m_conv_w, m_conv_b, m_conv_ln_g, m_conv_ln_b, m_conv_out_g, m_ssm_A_re, m_ssm_A_im, m_ssm_log_dt, m_ssm_B_re, m_ssm_B_im, m_ssm_C_re, m_ssm_C_im, m_ssm_D, m_ssm_glu_w, m_ssm_glu_b, m_ssm_out_g, m_w_out, m_norm_ffn2, m_ffn2_w1, m_ffn2_w3, m_ffn2_w2, m_norm_final, v_norm_ffn1, v_ffn1_w1, v_ffn1_w3, v_ffn1_w2, v_norm_mix, v_w_in, v_conv_w, v_conv_b, v_conv_ln_g, v_conv_ln_b, v_conv_out_g, v_ssm_A_re, v_ssm_A_im, v_ssm_log_dt, v_ssm_B_re, v_ssm_B_im, v_ssm_C_re, v_ssm_C_im, v_ssm_D, v_ssm_glu_w, v_ssm_glu_b, v_ssm_out_g, v_w_out, v_norm_ffn2, v_ffn2_w1, v_ffn2_w3, v_ffn2_w2, v_norm_final):
    given = dict(locals())
    wts = {n: given[n] for n in WEIGHTS}
    n_seq, seq, d = x.shape
    n_rows = n_seq * seq
    xf = x.reshape(n_rows, d)
    tgt = loss_target.reshape(n_rows, d)
    row = lambda a: a.reshape(1, -1)

    f = ffn1_w1.shape[-1]
    fp = _round_up(f, LANE)
    shards = []
    for n in BIG:
        s = wts[n][0].astype(BF16)
        if n.endswith('_w1') or n.endswith('_w3'):
            s = jnp.pad(s, ((0, 0), (0, fp - f)))
        elif n.endswith('_w2'):
            s = jnp.pad(s, ((0, fp - f), (0, 0)))
        shards.append(s)
    n_taps, c_shard = conv_w.shape[1], conv_w.shape[2]
    shards.append(jnp.pad(conv_w[0], ((0, HALO - n_taps), (0, 0))))
    shard_of = dict(zip(BIG + ['conv_w'], shards))
    axis_of = dict(BIG_AXIS, conv_w=1)
    groups = [['ffn1_w1', 'ffn1_w3'], ['ffn1_w2', 'w_in', 'conv_w', 'ssm_glu_w', 'w_out'], ['ffn2_w1', 'ffn2_w3', 'ffn2_w2']]
    fetch, tok = [], []
    for k, names in enumerate(groups):
        fetch.append(_exchange_start("gather%d_send" % k, True, [shard_of[n] for n in names], [axis_of[n] for n in names], tok))
        tok = [fetch[-1][3]]
    full = {}

    def arrive(k, after):
        lands = _exchange_wait("gather%d_recv" % k, True, fetch[k], [axis_of[n] for n in groups[k]], after)
        full.update(zip(groups[k], lands))

    arrive(0, tok)

    _, n_grp, n_state = ssm_A_re.shape
    grp = ssm_B_re.shape[-1]
    ns = n_grp * n_state
    c_ssm = n_grp * grp
    lr, li = ssm_A_re.reshape(1, ns), ssm_A_im.reshape(1, ns)
    ldt = jnp.repeat(ssm_log_dt.reshape(n_grp), n_state).reshape(1, ns)
    btr = ssm_B_re[0].transpose(2, 0, 1).reshape(grp, ns)
    bti = ssm_B_im[0].transpose(2, 0, 1).reshape(grp, ns)
    ctr = ssm_C_re[0].transpose(1, 0, 2).reshape(grp, ns)
    cti = ssm_C_im[0].transpose(1, 0, 2).reshape(grp, ns)
    _, _, bbr, bbi, pw = _s5_params_fwd(lr, li, ldt, btr, bti)
    nb = c_ssm // LANE
    sb, gpb = ns // nb, n_grp // nb
    diag = (jnp.arange(LANE)[:, None] // grp) == (jnp.arange(sb)[None, :] // n_state)

    def spread(t):
        return jnp.where(diag, jnp.tile(t.reshape(grp, nb, sb).transpose(1, 0, 2), (1, gpb, 1)), 0.0)

    def gather_diag(t):
        return (t * diag).reshape(nb, gpb, grp, sb).sum(1).transpose(1, 0, 2).reshape(grp, ns)

    def interleave(re, im):
        return jnp.stack([re.reshape(-1, nb, sb), im.reshape(-1, nb, sb)], axis=2).reshape(-1, 2 * ns)

    bdc = jnp.concatenate([spread(bbr), spread(bbi)], axis=2).astype(BF16)
    cdc = jnp.concatenate([spread(ctr).transpose(0, 2, 1), -spread(cti).transpose(0, 2, 1)], axis=1).astype(BF16)
    rowi = jnp.arange(SUBLANE)[:, None]
    pwf, pwc = interleave(pw[:, :ns], pw[:, ns:]), interleave(pw[:, :ns], -pw[:, ns:])
    tabs_f = [jnp.where(rowi >= s, pwf[s - 1][None, :], 0.0) for s in (1, 2, 4)] + [pwf]
    tabs_b = [jnp.where(rowi <= SUBLANE - 1 - s, pwc[s - 1][None, :], 0.0) for s in (1, 2, 4)] + [pwc[::-1]]
    c_conv = conv_b.shape[1]
    u_blk = 2 * c_conv // LANE

    h1 = _rms_fwd("ffn1_rms", xf, norm_ffn1)
    a1, b1, z1 = _ffn_up("ffn1_up", h1, full['ffn1_w1'], full['ffn1_w3'])
    arrive(1, [z1])
    x1 = _mm("ffn1_down", z1, full['ffn1_w2'], 1, 0, F32, addend=xf, alpha=0.5)
    saved1 = (h1, a1, b1, z1)
    cw = full['conv_w']
    h2 = _rms_fwd("mix_rms", x1, norm_mix)
    proj = _mm("mix_in", h2, full['w_in'], 1, 0, F32)
    assert c_conv == c_ssm and proj.shape[1] == 3 * c_conv
    cpre, an = _conv_fwd(proj, cw, conv_b, conv_ln_g, conv_ln_b, conv_out_g, seq)
    xs, ypre, yg = _s5_fwd(proj, u_blk, bdc, cdc, tabs_f, ssm_D, seq, sb)
    q0 = _mm("s5_gate", yg, full['ssm_glu_w'], 1, 0, F32)
    sn = _s5_post2(yg, q0, ssm_glu_b, ssm_out_g)
    wo = full['w_out']
    x2 = _mm("mix_out_a", an, wo[:c_conv], 1, 0, F32, addend=x1)
    x2 = _mm("mix_out_s", sn, wo[c_conv:], 1, 0, F32, addend=x2)
    arrive(2, [x2])
    x3, saved2 = _ffn_fwd("ffn2", x2, norm_ffn2, full['ffn2_w1'], full['ffn2_w3'], full['ffn2_w2'])
    dx3, loss_row, d_norm_final = _loss_head(x3, row(norm_final), tgt)

    g = {}
    dx2, g['norm_ffn2'], sent = _ffn_bwd("ffn2", x2, norm_ffn2, full['ffn2_w1'], full['ffn2_w3'], full['ffn2_w2'], saved2, dx3)
    dmixed = _mm("mix_dmixed", dx2, wo, 1, 1, F32)
    dwo = jnp.concatenate([_mm("mix_dwo_a", an, dx2, 0, 0, BF16), _mm("mix_dwo_s", sn, dx2, 0, 0, BF16)], axis=0)
    dq, dyg1, g['ssm_out_g'], g['ssm_glu_b'] = _s5_post2_bwd(dmixed, yg, q0, ssm_glu_b, ssm_out_g)
    dyg2 = _mm("s5_dgate", dq, full['ssm_glu_w'], 1, 1, F32)
    dwg = _mm("s5_dwg", yg, dq, 0, 0, BF16)
    sent['w_out ssm_glu_w'] = (_exchange_start("mix_wo_wg_send", False, [dwo, dwg], [0, 0]), [0, 0])
    dypre, du_skip, g['ssm_D'] = _s5_post1_bwd(dyg1, dyg2, ypre, proj, ssm_D, after=[sent['w_out ssm_glu_w'][0][3]])
    du, dabar, dbdc, dcdc = _s5_bwd(dypre, du_skip, xs, proj, u_blk, bdc, cdc, tabs_b, seq, sb)
    dabar = dabar.reshape(nb, 2, sb)
    dlr, dli, dldt, dbtr, dbti = _s5_params_bwd(lr, li, ldt, btr, bti, dabar[:, 0].reshape(1, ns), dabar[:, 1].reshape(1, ns),
                                                gather_diag(dbdc[:, :, :sb]), gather_diag(dbdc[:, :, sb:]))
    g['ssm_A_re'], g['ssm_A_im'] = dlr, dli
    g['ssm_log_dt'] = dldt.reshape(n_grp, n_state).sum(axis=1)
    g['ssm_B_re'] = dbtr.reshape(grp, n_grp, n_state).transpose(1, 2, 0)
    g['ssm_B_im'] = dbti.reshape(grp, n_grp, n_state).transpose(1, 2, 0)
    g['ssm_C_re'] = gather_diag(dcdc[:, :sb].transpose(0, 2, 1)).reshape(grp, n_grp, n_state).transpose(1, 0, 2)
    g['ssm_C_im'] = -gather_diag(dcdc[:, sb:].transpose(0, 2, 1)).reshape(grp, n_grp, n_state).transpose(1, 0, 2)
    dc, g['conv_out_g'], g['conv_ln_g'], g['conv_ln_b'], g['conv_b'] = _conv_bwd_rows(dmixed, cpre, conv_ln_g, conv_ln_b,
                                                                                    conv_out_g)
    dval, dgate, dcw = _conv_bwd_taps(proj, dc, cw, seq)
    dproj = jnp.concatenate([dval, dgate, du], axis=1)
    sent['w_in'] = (_exchange_start("mix_win_send", False, [_mm("mix_dwin", h2, dproj, 0, 0, BF16)], [1]), [1])
    dh2 = _mm("mix_dh", dproj, full['w_in'], 1, 1, F32, after=[sent['w_in'][0][3]])
    dx1, g['norm_mix'] = _rms_bwd_res("mix_drms", x1, norm_mix, dh2, dx2)
    dx0, g['norm_ffn1'], sent1 = _ffn_bwd("ffn1", xf, norm_ffn1, full['ffn1_w1'], full['ffn1_w3'], full['ffn1_w2'], saved1, dx1)
    sent.update(sent1)
    g['norm_final'] = d_norm_final
    g['conv_w'] = dcw[:n_taps]

    small_shapes = [(n_taps, c_conv) if n == 'conv_w' else wts[n].shape for n in SMALL]
    buf, buf_rows = _pack([g[n] for n in SMALL] + [loss_row])
    total = _unpack(_all_reduce_small(buf), buf_rows, small_shapes + [(1, LANE)])
    loss = total[-1][0, 0]
    grads = dict(zip(SMALL, total[:-1]))
    chip = 2 * lax.axis_index("x") + lax.axis_index("y")
    grads['conv_w'] = lax.dynamic_slice_in_dim(grads['conv_w'], chip * c_shard, c_shard, axis=1)[None]
    flat = lambda a: a.reshape(-1, a.shape[-1])
    small = _adamw_many("adamw_small", *[[flat(src[p + n]) for n in SMALL]
                                         for src, p in ((grads, ''), (given, ''), (given, 'm_'), (given, 'v_'))])
    deltas, new_m, new_v = ({n: o.reshape(wts[n].shape) for n, o in zip(SMALL, outs)} for outs in small)

    slots = {}
    for names, (started, axes) in sent.items():
        lands = _exchange_wait(names.replace(' ', '_') + "_recv", False, started, axes, after=[dx0])
        slots.update(zip(names.split(), lands))
    sums = [_sum_slots("sum_" + n, slots[n]) for n in BIG]
    theirs = _swap_with_sibling(sums)
    for n, mine, other in zip(BIG, sums, theirs):
        grads[n], deltas[n], new_m[n], new_v[n] = (
            o[None] for o in _adamw("adamw_" + n, [mine, other], given[n][0], given['m_' + n][0], given['v_' + n][0]))

    return (loss, dx0.reshape(x.shape), *[grads[n] for n in WEIGHTS], *[deltas[n] for n in WEIGHTS],
            *[new_m[n] for n in WEIGHTS], *[new_v[n] for n in WEIGHTS])
```

```python
import math
from typing import Callable, NamedTuple

import jax
import jax.numpy as jnp
from jax import lax
from jax.experimental import pallas as pl
from jax.experimental.pallas import tpu as pltpu

F32 = jnp.float32
BF16 = jnp.bfloat16
EPS = 1e-6
ADAM_LR, ADAM_B1, ADAM_B2, ADAM_EPS, ADAM_WD, ADAM_STEP = 0.001, 0.9, 0.999, 1e-08, 0.01, 10
MESH = pl.DeviceIdType.MESH
ANY = pl.BlockSpec(memory_space=pl.ANY)
LANE = 128
SUBLANE = 8
VMEM_LIMIT_BYTES = 56 << 20
ROW_TILE = 256
ROW_TILE_ELEMS = 256 * 1024
WHOLE_ELEMS = 512 * 1024
FFN_ROWS = 256
CONV_TILE = 128
CONV_SUB = 32
HALO = 32
SCAN_TILE = 128
SCAN_COLS = 512
N_CHIPS = 4
CHIP_RELS = ((1, 0), (0, 1), (1, 1))
NT = (((1,), (1,)), ((), ()))
GELU_K = math.sqrt(2.0 / math.pi)
GELU_C = 0.044715

WEIGHTS = ['norm_ffn1', 'ffn1_w1', 'ffn1_w3', 'ffn1_w2', 'norm_mix', 'w_in', 'conv_w', 'conv_b', 'conv_ln_g', 'conv_ln_b',
           'conv_out_g', 'ssm_A_re', 'ssm_A_im', 'ssm_log_dt', 'ssm_B_re', 'ssm_B_im', 'ssm_C_re', 'ssm_C_im', 'ssm_D',
           'ssm_glu_w', 'ssm_glu_b', 'ssm_out_g', 'w_out', 'norm_ffn2', 'ffn2_w1', 'ffn2_w3', 'ffn2_w2', 'norm_final']
BIG = ['ffn1_w1', 'ffn1_w3', 'ffn1_w2', 'w_in', 'ssm_glu_w', 'w_out', 'ffn2_w1', 'ffn2_w3', 'ffn2_w2']
BIG_AXIS = {'ffn1_w1': 0, 'ffn1_w3': 0, 'ffn1_w2': 0, 'w_in': 1, 'ssm_glu_w': 0, 'w_out': 0, 'ffn2_w1': 0, 'ffn2_w3': 0,
            'ffn2_w2': 0}
TRANSPOSED = ('ffn1_w1', 'ffn1_w3', 'ffn2_w1', 'ffn2_w3')
SMALL = [n for n in WEIGHTS if n not in BIG]


def _round_up(n, m):
    return -(-n // m) * m


def _pick(n, cands):
    for c in cands:
        if c <= n and n % c == 0:
            return c
    return n


def _params(*sem):
    return pltpu.CompilerParams(dimension_semantics=sem, vmem_limit_bytes=VMEM_LIMIT_BYTES)


def _rms_r(x):
    return lax.rsqrt(jnp.mean(x * x, axis=-1, keepdims=True) + EPS)


def _rms_bwd(x, r, g, dy):
    dyg = dy * g
    return r * dyg - x * (r * r * r) * jnp.mean(x * dyg, axis=-1, keepdims=True)


def _sigmoid(x):
    return jax.nn.sigmoid(x)


def _dsilu(a, s):
    return s * (1.0 + a * (1.0 - s))


def _gelu(x):
    return 0.5 * x * (1.0 + jnp.tanh(GELU_K * (x + GELU_C * x * x * x)))


def _dgelu(x):
    t = jnp.tanh(GELU_K * (x + GELU_C * x * x * x))
    return 0.5 * (1.0 + t) + 0.5 * x * (1.0 - t * t) * GELU_K * (1.0 + 3.0 * GELU_C * x * x)


def _colsum(v):
    return jnp.sum(v, axis=0, keepdims=True)


def _rowwise(name, body, n_rows, row_ins, par_ins, row_outs, acc_outs, after=()):
    widest = max([w for (_, w, _) in row_ins] + [w for (w, _) in row_outs])
    tt = _pick(n_rows, [t for t in (256, 128, 64, 32, 16, 8) if t * widest <= ROW_TILE_ELEMS])
    in_specs = [pl.BlockSpec((tt, w), lambda i, cb=cb: (i, cb)) for (_, w, cb) in row_ins]
    in_specs += [pl.BlockSpec(p.shape, lambda i: (0, 0)) for p in par_ins] + [ANY] * len(after)
    out_specs = [pl.BlockSpec((tt, w), lambda i: (i, 0)) for (w, _) in row_outs]
    out_specs += [pl.BlockSpec((r, w), lambda i: (0, 0)) for (r, w) in acc_outs]
    out_shape = [jax.ShapeDtypeStruct((n_rows, w), dt) for (w, dt) in row_outs]
    out_shape += [jax.ShapeDtypeStruct((r, w), F32) for (r, w) in acc_outs]
    n_in, n_ro = len(row_ins) + len(par_ins), len(row_outs)
    o0 = n_in + len(after)

    def kern(*refs):
        accs = refs[o0 + n_ro:]
        if accs:
            @pl.when(pl.program_id(0) == 0)
            def _():
                for a in accs:
                    a[...] = jnp.zeros_like(a)
        body(refs[:n_in], refs[o0:o0 + n_ro], accs)

    return pl.pallas_call(kern, name=name, grid=(n_rows // tt,), in_specs=in_specs, out_specs=out_specs, out_shape=out_shape,
                          compiler_params=_params("arbitrary"))(*[a for a, _, _ in row_ins], *par_ins, *after)


class Post(NamedTuple):
    rows: list
    gains: list
    outs: list
    sums: list
    fn: Callable


def _post_rms(gain):
    return Post([], [gain], [F32, BF16], [], lambda r, rows, gains: ([r, r * _rms_r(r) * gains[0]], []))


def _post_rms_bwd(x, gain, dres):
    def fn(dh, rows, gains):
        r = _rms_r(rows[0])
        return [rows[1] + _rms_bwd(rows[0], r, gains[0], dh)], [_colsum(dh * rows[0] * r)]

    return Post([x, dres], [gain], [F32], [x.shape[1]], fn)


def _post_loss(gain, tgt):
    d = tgt.shape[1]

    def fn(xv, rows, gains):
        r = _rms_r(xv)
        e = xv * r * gains[0] - rows[0]
        sq = jnp.sum(jnp.sum(e * e, axis=-1, keepdims=True), axis=0, keepdims=True)
        dy = e * (1.0 / d)
        return [_rms_bwd(xv, r, gains[0], dy)], [jnp.broadcast_to(sq * (0.5 / d), (1, LANE)), _colsum(dy * xv * r)]

    return Post([tgt], [gain], [F32], [LANE, d], fn)


def _mm(name, a, b, ca, cb, out_dtype=F32, addend=None, alpha=1.0, a_cols=None, after=(), post=None):
    a_start, a_width = a_cols if a_cols else (0, a.shape[1])
    m, k = (a.shape[0], a_width) if ca == 1 else (a_width, a.shape[0])
    n = b.shape[1 - cb]
    assert b.shape[cb] == k, (name, a.shape, b.shape)
    tm = _pick(m, (512, 256, 128) if post else (1024, 512, 256, 128))
    tn = _pick(n, (1024, 768, 512, 384, 256, 128))
    tk = _pick(k, (1024, 768, 512, 256, 128))
    nk = k // tk
    if ca == 1:
        assert a_start % tk == 0
        a_spec = pl.BlockSpec((tm, tk), lambda i, j, kk: (i, kk + a_start // tk))
    else:
        assert a_start % tm == 0
        a_spec = pl.BlockSpec((tk, tm), lambda i, j, kk: (kk, i + a_start // tm))
    b_spec = pl.BlockSpec((tk, tn), lambda i, j, kk: (kk, j)) if cb == 0 else pl.BlockSpec((tn, tk), lambda i, j, kk: (j, kk))
    o_spec = pl.BlockSpec((tm, tn), lambda i, j, kk: (i, j))
    fixed = lambda w: pl.BlockSpec((1, w), lambda i, j, kk: (0, 0))
    ins, in_specs = [a, b], [a_spec, b_spec]
    if addend is not None:
        ins.append(addend)
        in_specs.append(o_spec)
    n_plain = len(ins)
    n_rows, n_gains = (len(post.rows), len(post.gains)) if post else (0, 0)
    if post:
        assert tn == n, name
        ins += post.rows + post.gains
        in_specs += [o_spec] * n_rows + [fixed(n)] * n_gains
    ins += list(after)
    in_specs += [ANY] * len(after)
    n_in = len(ins)
    if post:
        out_specs = [o_spec] * len(post.outs) + [fixed(w) for w in post.sums]
        out_shape = [jax.ShapeDtypeStruct((m, n), dt) for dt in post.outs] + [jax.ShapeDtypeStruct((1, w), F32) for w in post.sums]
    else:
        out_specs, out_shape = [o_spec], [jax.ShapeDtypeStruct((m, n), out_dtype)]
    n_out = len(out_specs)
    dims = (((ca,), (cb,)), ((), ()))

    def emit(refs, r):
        if alpha != 1.0:
            r = r * alpha
        if addend is not None:
            r = r + refs[2][...].astype(F32)
        outs = refs[n_in:n_in + n_out]
        if post is None:
            outs[0][...] = r.astype(out_dtype)
            return
        vals, incs = post.fn(r, [q[...] for q in refs[n_plain:n_plain + n_rows]],
                             [q[...] for q in refs[n_plain + n_rows:n_plain + n_rows + n_gains]])
        for o_ref, val in zip(outs, vals):
            o_ref[...] = val.astype(o_ref.dtype)
        for s_ref, inc in zip(outs[len(vals):], incs):
            s_ref[...] += inc

    def kern(*refs):
        kk = pl.program_id(2)
        if post and post.sums:
            @pl.when(jnp.logical_and(jnp.logical_and(pl.program_id(0) == 0, pl.program_id(1) == 0), kk == 0))
            def _():
                for s_ref in refs[n_in + len(post.outs):n_in + n_out]:
                    s_ref[...] = jnp.zeros_like(s_ref)

        part = lax.dot_general(refs[0][...].astype(BF16), refs[1][...].astype(BF16), dims, preferred_element_type=F32)
        if nk == 1:
            emit(refs, part)
            return
        acc_ref = refs[-1]

        @pl.when(kk == 0)
        def _():
            acc_ref[...] = part

        @pl.when(kk > 0)
        def _():
            acc_ref[...] += part

        @pl.when(kk == nk - 1)
        def _():
            emit(refs, acc_ref[...])

    res = pl.pallas_call(kern, name=name, grid=(m // tm, n // tn, nk), in_specs=in_specs, out_specs=out_specs,
                         out_shape=out_shape, scratch_shapes=[] if nk == 1 else [pltpu.VMEM((tm, tn), F32)],
                         compiler_params=_params("arbitrary", "arbitrary", "arbitrary"))(*ins)
    return res if post else res[0]


def _rms_fwd(name, x, g):
    def body(ins, outs, accs):
        xv = ins[0][...]
        outs[0][...] = (xv * _rms_r(xv) * ins[1][...]).astype(BF16)

    return _rowwise(name, body, x.shape[0], [(x, x.shape[1], 0)], [g], [(x.shape[1], BF16)], [])[0]


def _ffn_up(name, h, w1, w3):
    t, d = h.shape
    ff = w1.shape[0]
    tm, tn = _pick(t, (1024, 512, 256, 128)), _pick(ff, (1024, 768, 512, 256, 128))

    def kern(h_ref, w1_ref, w3_ref, a_ref, b_ref, z_ref):
        hv = h_ref[...]
        a = lax.dot_general(hv, w1_ref[...], NT, preferred_element_type=F32)
        b = lax.dot_general(hv, w3_ref[...], NT, preferred_element_type=F32)
        a_ref[...] = a.astype(BF16)
        b_ref[...] = b.astype(BF16)
        z_ref[...] = (a * _sigmoid(a) * b).astype(BF16)

    w_spec = pl.BlockSpec((tn, d), lambda i, j: (j, 0))
    o_spec = pl.BlockSpec((tm, tn), lambda i, j: (i, j))
    return pl.pallas_call(kern, name=name, grid=(t // tm, ff // tn),
                          in_specs=[pl.BlockSpec((tm, d), lambda i, j: (i, 0)), w_spec, w_spec], out_specs=[o_spec] * 3,
                          out_shape=[jax.ShapeDtypeStruct((t, ff), BF16)] * 3,
                          compiler_params=_params("arbitrary", "arbitrary"))(h, w1, w3)


def _ffn_dglu(name, dxo, w2, a, b, after=()):
    t, d = dxo.shape
    ff = w2.shape[0]
    tm = _pick(t, (FFN_ROWS, 128))

    def kern(dx_ref, w2_ref, a_ref, b_ref, *rest):
        da_ref, db_ref = rest[-2:]
        dz = lax.dot_general(dx_ref[...].astype(BF16), w2_ref[...], NT, preferred_element_type=F32) * 0.5
        av, bv = a_ref[...].astype(F32), b_ref[...].astype(F32)
        s = _sigmoid(av)
        da_ref[...] = (dz * bv * _dsilu(av, s)).astype(BF16)
        db_ref[...] = (dz * av * s).astype(BF16)

    o_spec = pl.BlockSpec((tm, ff), lambda i: (i, 0))
    return pl.pallas_call(kern, name=name, grid=(t // tm,),
                          in_specs=[pl.BlockSpec((tm, d), lambda i: (i, 0)), pl.BlockSpec((ff, d), lambda i: (0, 0)),
                                    o_spec, o_spec] + [ANY] * len(after),
                          out_specs=[o_spec] * 2, out_shape=[jax.ShapeDtypeStruct((t, ff), BF16)] * 2,
                          compiler_params=_params("arbitrary"))(dxo, w2, a, b, *after)


def _ffn_dh(name, da, db, w1, w3, x, g, dres, after=()):
    t, d = x.shape
    ff = da.shape[1]
    tm = _pick(t, (FFN_ROWS, 128))

    def kern(da_ref, db_ref, w1_ref, w3_ref, x_ref, g_ref, dres_ref, *rest):
        dx_ref, dg_ref = rest[-2:]

        @pl.when(pl.program_id(0) == 0)
        def _():
            dg_ref[...] = jnp.zeros_like(dg_ref)

        dh = (jnp.dot(da_ref[...], w1_ref[...], preferred_element_type=F32)
              + jnp.dot(db_ref[...], w3_ref[...], preferred_element_type=F32))
        xv = x_ref[...]
        r = _rms_r(xv)
        dx_ref[...] = dres_ref[...] + _rms_bwd(xv, r, g_ref[...], dh)
        dg_ref[...] += _colsum(dh * xv * r)

    act = pl.BlockSpec((tm, ff), lambda i: (i, 0))
    wgt = pl.BlockSpec((ff, d), lambda i: (0, 0))
    rows = pl.BlockSpec((tm, d), lambda i: (i, 0))
    gain = pl.BlockSpec((1, d), lambda i: (0, 0))
    return pl.pallas_call(kern, name=name, grid=(t // tm,),
                          in_specs=[act, act, wgt, wgt, rows, gain, rows] + [ANY] * len(after), out_specs=[rows, gain],
                          out_shape=[jax.ShapeDtypeStruct((t, d), F32), jax.ShapeDtypeStruct((1, d), F32)],
                          compiler_params=_params("arbitrary"))(da, db, w1, w3, x, g, dres, *after)


def _ffn_bwd(tag, x, g, w1, w3, w2, saved, dxo):
    h, a, b, z = saved
    dw2 = _mm(tag + "_dw2", z, dxo, 0, 0, BF16, alpha=0.5)
    s2 = _exchange_start(tag + "_w2_send", False, [dw2], [0])
    da, db = _ffn_dglu(tag + "_dglu", dxo, w2, a, b, after=[s2[3]])
    dw1 = _mm(tag + "_dw1", da, h, 0, 0, BF16)
    s1 = _exchange_start(tag + "_w1_send", False, [dw1], [0])
    dw3 = _mm(tag + "_dw3", db, h, 0, 0, BF16, after=[s1[3]])
    s3 = _exchange_start(tag + "_w3_send", False, [dw3], [0])
    dx, dg = _ffn_dh(tag + "_dh", da, db, w1, w3, x, g, dxo, after=[s3[3]])
    return dx, dg, {tag + "_w1": (s1, [0]), tag + "_w3": (s3, [0]), tag + "_w2": (s2, [0])}


def _shift_copies(ext_ref, sh_ref):
    n = ext_ref.shape[0] - SUBLANE
    for r in range(1, SUBLANE):
        sh_ref[r, pl.ds(0, n), :] = ext_ref[pl.ds(r, n), :]


def _rows_at(ext_ref, sh_ref, off, rows):
    r = off % SUBLANE
    return ext_ref[pl.ds(off, rows), :] if r == 0 else sh_ref[r, pl.ds(off - r, rows), :]


def _conv_fwd(proj, cw, cb, lng, lnb, og, seq):
    n_rows, c = proj.shape[0], cb.shape[1]
    kw = HALO - 1
    tt = _pick(seq, (CONV_TILE,))
    hb = tt // HALO

    def kern(v_ref, g_ref, vp_ref, gp_ref, w_ref, cb_ref, lg_ref, lb_ref, og_ref, c_ref, an_ref, ext_ref, sh_ref):
        first = (pl.program_id(0) * tt) % seq == 0
        ext_ref[pl.ds(HALO, tt), :] = v_ref[...] * _sigmoid(g_ref[...])
        ext_ref[pl.ds(0, HALO), :] = vp_ref[...] * _sigmoid(gp_ref[...]) * jnp.where(first, 0.0, 1.0)
        _shift_copies(ext_ref, sh_ref)
        for r0 in range(0, tt, CONV_SUB):
            rows = min(CONV_SUB, tt - r0)
            acc = jnp.zeros((rows, c), F32)
            for k in range(kw):
                acc = acc + w_ref[pl.ds(k, 1), :] * _rows_at(ext_ref, sh_ref, r0 + HALO - (kw - 1) + k, rows)
            c_ref[pl.ds(r0, rows), :] = acc + cb_ref[...]
        cv = c_ref[...]
        mu = jnp.mean(cv, axis=-1, keepdims=True)
        xc = cv - mu
        rstd = lax.rsqrt(jnp.mean(xc * xc, axis=-1, keepdims=True) + EPS)
        lv = xc * rstd * lg_ref[...] + lb_ref[...]
        sl = lv * _sigmoid(lv)
        an_ref[...] = (sl * _rms_r(sl) * og_ref[...]).astype(BF16)

    cur = lambda cbk: pl.BlockSpec((tt, c), lambda i: (i, cbk))
    prev = lambda cbk: pl.BlockSpec((HALO, c), lambda i: (jnp.maximum(i * hb - 1, 0), cbk))
    par = lambda p: pl.BlockSpec(p.shape, lambda i: (0, 0))
    return pl.pallas_call(
        kern, name="conv_fwd", grid=(n_rows // tt,),
        in_specs=[cur(0), cur(1), prev(0), prev(1), par(cw), par(cb), par(lng), par(lnb), par(og)],
        out_specs=[pl.BlockSpec((tt, c), lambda i: (i, 0))] * 2,
        out_shape=[jax.ShapeDtypeStruct((n_rows, c), F32), jax.ShapeDtypeStruct((n_rows, c), BF16)],
        scratch_shapes=[pltpu.VMEM((tt + HALO, c), F32), pltpu.VMEM((SUBLANE, tt + HALO, c), F32)],
        compiler_params=_params("arbitrary"),
    )(proj, proj, proj, proj, cw, cb, lng, lnb, og)


def _conv_bwd_rows(dmixed, cpre, lng, lnb, og):
    c = cpre.shape[1]

    def body(ins, outs, accs):
        dan, cv, lg, lb, ogv = ins[0][...], ins[1][...], ins[2][...], ins[3][...], ins[4][...]
        mu = jnp.mean(cv, axis=-1, keepdims=True)
        xc = cv - mu
        rstd = lax.rsqrt(jnp.mean(xc * xc, axis=-1, keepdims=True) + EPS)
        xh = xc * rstd
        lv = xh * lg + lb
        s = _sigmoid(lv)
        sl = lv * s
        r2 = _rms_r(sl)
        accs[0][...] += _colsum(dan * sl * r2)
        dl = _rms_bwd(sl, r2, ogv, dan) * _dsilu(lv, s)
        accs[1][...] += _colsum(dl * xh)
        accs[2][...] += _colsum(dl)
        dxh = dl * lg
        dc = rstd * (dxh - jnp.mean(dxh, axis=-1, keepdims=True) - xh * jnp.mean(dxh * xh, axis=-1, keepdims=True))
        outs[0][...] = dc
        accs[3][...] += _colsum(dc)

    return _rowwise("conv_bwd_rows", body, cpre.shape[0], [(dmixed, c, 0), (cpre, c, 0)], [lng, lnb, og], [(c, F32)],
                    [(1, c)] * 4)


def _conv_bwd_taps(proj, dc, cw, seq):
    n_rows, c = dc.shape
    kw = HALO - 1
    tt = _pick(seq, (CONV_TILE,))
    hb = tt // HALO
    last_blk = n_rows // HALO - 1

    def kern(v_ref, g_ref, vp_ref, gp_ref, dc_ref, dn_ref, w_ref, dv_ref, dg_ref, dw_ref, exta_ref, extd_ref, sha_ref, shd_ref):
        i = pl.program_id(0)
        first = (i * tt) % seq == 0
        last = ((i + 1) * tt) % seq == 0

        @pl.when(i == 0)
        def _():
            dw_ref[...] = jnp.zeros_like(dw_ref)

        sg = _sigmoid(g_ref[...])
        exta_ref[pl.ds(HALO, tt), :] = v_ref[...] * sg
        exta_ref[pl.ds(0, HALO), :] = vp_ref[...] * _sigmoid(gp_ref[...]) * jnp.where(first, 0.0, 1.0)
        dcv = dc_ref[...]
        extd_ref[pl.ds(0, tt), :] = dcv
        extd_ref[pl.ds(tt, HALO), :] = dn_ref[...] * jnp.where(last, 0.0, 1.0)
        _shift_copies(exta_ref, sha_ref)
        _shift_copies(extd_ref, shd_ref)
        for k in range(kw):
            dw_ref[pl.ds(k, 1), :] += _colsum(_rows_at(exta_ref, sha_ref, HALO - (kw - 1) + k, tt) * dcv)
        for r0 in range(0, tt, CONV_SUB):
            rows = min(CONV_SUB, tt - r0)
            acc = jnp.zeros((rows, c), F32)
            for k in range(kw):
                acc = acc + w_ref[pl.ds(k, 1), :] * _rows_at(extd_ref, shd_ref, r0 + (kw - 1) - k, rows)
            dv_ref[pl.ds(r0, rows), :] = acc
        da = dv_ref[...]
        dv_ref[...] = da * sg
        dg_ref[...] = da * v_ref[...] * sg * (1.0 - sg)

    cur = lambda cbk: pl.BlockSpec((tt, c), lambda i: (i, cbk))
    prev = lambda cbk: pl.BlockSpec((HALO, c), lambda i: (jnp.maximum(i * hb - 1, 0), cbk))
    nxt = pl.BlockSpec((HALO, c), lambda i: (jnp.minimum((i + 1) * hb, last_blk), 0))
    return pl.pallas_call(
        kern, name="conv_bwd_taps", grid=(n_rows // tt,),
        in_specs=[cur(0), cur(1), prev(0), prev(1), cur(0), nxt, pl.BlockSpec(cw.shape, lambda i: (0, 0))],
        out_specs=[cur(0), cur(0), pl.BlockSpec((HALO, c), lambda i: (0, 0))],
        out_shape=[jax.ShapeDtypeStruct((n_rows, c), F32), jax.ShapeDtypeStruct((n_rows, c), F32),
                   jax.ShapeDtypeStruct((HALO, c), F32)],
        scratch_shapes=[pltpu.VMEM((tt + HALO, c), F32)] * 2 + [pltpu.VMEM((SUBLANE, tt + HALO, c), F32)] * 2,
        compiler_params=_params("arbitrary"),
    )(proj, proj, proj, proj, dc, dc, cw)


def _s5_params_fwd(lr, li, ldt, btr, bti):
    ns = lr.shape[1]

    def kern(lr_ref, li_ref, ldt_ref, btr_ref, bti_ref, ar_ref, ai_ref, bbr_ref, bbi_ref, pw_ref, pwr_ref):
        lrv, liv = lr_ref[...], li_ref[...]
        dt = jnp.exp(ldt_ref[...])
        zr, zi = lrv * dt, liv * dt
        mag = jnp.exp(zr)
        ar, ai = mag * jnp.cos(zi), mag * jnp.sin(zi)
        den = lrv * lrv + liv * liv
        nr = ar - 1.0
        cr = (nr * lrv + ai * liv) / den
        ci = (ai * lrv - nr * liv) / den
        ar_ref[...] = ar
        ai_ref[...] = ai
        bbr_ref[...] = cr * btr_ref[...] - ci * bti_ref[...]
        bbi_ref[...] = cr * bti_ref[...] + ci * btr_ref[...]
        pr, pi = ar, ai
        for e in range(SUBLANE):
            for ref, at in ((pw_ref, e), (pwr_ref, SUBLANE - 1 - e)):
                ref[pl.ds(at, 1), pl.ds(0, ns)] = pr
                ref[pl.ds(at, 1), pl.ds(ns, ns)] = pi
            pr, pi = pr * ar - pi * ai, pr * ai + pi * ar

    h = btr.shape[0]
    shapes = [jax.ShapeDtypeStruct((1, ns), F32)] * 2 + [jax.ShapeDtypeStruct((h, ns), F32)] * 2
    shapes += [jax.ShapeDtypeStruct((SUBLANE, 2 * ns), F32)] * 2
    return pl.pallas_call(kern, name="s5_params_fwd", out_shape=shapes)(lr, li, ldt, btr, bti)


def _s5_params_bwd(lr, li, ldt, btr, bti, dar, dai, dbbr, dbbi):
    def kern(lr_ref, li_ref, ldt_ref, btr_ref, bti_ref, dar_ref, dai_ref, dbr_ref, dbi_ref,
             dlr_ref, dli_ref, dldt_ref, dbtr_ref, dbti_ref):
        lrv, liv = lr_ref[...], li_ref[...]
        dt = jnp.exp(ldt_ref[...])
        zr, zi = lrv * dt, liv * dt
        mag = jnp.exp(zr)
        ar, ai = mag * jnp.cos(zi), mag * jnp.sin(zi)
        den = lrv * lrv + liv * liv
        nr = ar - 1.0
        cr = (nr * lrv + ai * liv) / den
        ci = (ai * lrv - nr * liv) / den
        dbr, dbi, br, bi = dbr_ref[...], dbi_ref[...], btr_ref[...], bti_ref[...]
        dbtr_ref[...] = cr * dbr + ci * dbi
        dbti_ref[...] = cr * dbi - ci * dbr
        dcr = _colsum(br * dbr + bi * dbi)
        dci = _colsum(br * dbi - bi * dbr)
        ir, ii = lrv / den, -liv / den
        dnr = ir * dcr + ii * dci
        dni = ir * dci - ii * dcr
        wr, wi = cr * ir - ci * ii, cr * ii + ci * ir
        dl1r = -(wr * dcr + wi * dci)
        dl1i = -(wr * dci - wi * dcr)
        dtr, dti = dar_ref[...] + dnr, dai_ref[...] + dni
        dzr = ar * dtr + ai * dti
        dzi = ar * dti - ai * dtr
        dlr_ref[...] = dl1r + dt * dzr
        dli_ref[...] = dl1i + dt * dzi
        dldt_ref[...] = (dzr * lrv + dzi * liv) * dt

    ns, h = lr.shape[1], btr.shape[0]
    shapes = [jax.ShapeDtypeStruct((1, ns), F32)] * 3 + [jax.ShapeDtypeStruct((h, ns), F32)] * 2
    return pl.pallas_call(kern, name="s5_params_bwd", out_shape=shapes)(lr, li, ldt, btr, bti, dar, dai, dbbr, dbbi)


def _scan_tile(s_ref, o_ref, tabs, car_ref, sb, reverse, x_ref=None, acc_ref=None):
    l1, l2, l4, pw = tabs
    rows_t, w = s_ref.shape
    ng = rows_t // SUBLANE
    cw = _pick(sb, (SCAN_COLS,))
    carry_row = 0 if reverse else SUBLANE - 1
    row = lax.broadcasted_iota(jnp.int32, (SUBLANE, cw), 0)

    def group(gi, carry):
        g = (ng - 1 - gi) if reverse else gi
        rows = pl.ds(pl.multiple_of(g * SUBLANE, SUBLANE), SUBLANE)
        for c0 in [b0 + o for b0 in range(0, w, 2 * sb) for o in range(0, sb, cw)]:
            cr, ci = pl.ds(c0, cw), pl.ds(c0 + sb, cw)
            xr, xi = s_ref[rows, cr], s_ref[rows, ci]
            for s, lt in ((1, l1), (2, l2), (4, l4)):
                sh = (SUBLANE - s) if reverse else s
                sr, si = pltpu.roll(xr, sh, 0), pltpu.roll(xi, sh, 0)
                ar, ai = lt[:, cr], lt[:, ci]
                xr, xi = xr + ar * sr - ai * si, xi + ar * si + ai * sr
            kr, ki = car_ref[pl.ds(carry_row, 1), cr], car_ref[pl.ds(carry_row, 1), ci]
            pr, pi = pw[:, cr], pw[:, ci]
            xr, xi = xr + pr * kr - pi * ki, xi + pr * ki + pi * kr
            o_ref[rows, cr] = xr
            o_ref[rows, ci] = xi
            car_ref[:, cr] = xr
            car_ref[:, ci] = xi
            if acc_ref is not None:
                nr = jnp.where(row == SUBLANE - 1, kr, pltpu.roll(xr, SUBLANE - 1, 0))
                ni = jnp.where(row == SUBLANE - 1, ki, pltpu.roll(xi, SUBLANE - 1, 0))
                pxr, pxi = x_ref[rows, cr], x_ref[rows, ci]
                acc_ref[:, cr] += nr * pxr + ni * pxi
                acc_ref[:, ci] += ni * pxr - nr * pxi
        return carry

    lax.fori_loop(0, ng, group, 0)


def _s5_fwd(proj, u_blk, bdc, cdc, tabs, dskip, seq, sb):
    n_rows = proj.shape[0]
    nb, blk, w_blk = bdc.shape
    c, w = nb * blk, nb * w_blk
    tt = _pick(seq, (SCAN_TILE,))

    def kern(u_ref, bd_ref, cd_ref, l1, l2, l4, pw, d_ref, xs_ref, yp_ref, yg_ref, bu_ref, car_ref):
        @pl.when((pl.program_id(0) * tt) % seq == 0)
        def _():
            car_ref[...] = jnp.zeros_like(car_ref)

        for j in range(nb):
            bu_ref[:, pl.ds(j * w_blk, w_blk)] = jnp.dot(u_ref[:, pl.ds(j * blk, blk)].astype(BF16), bd_ref[j],
                                                         preferred_element_type=F32)
        _scan_tile(bu_ref, xs_ref, (l1, l2, l4, pw), car_ref, sb, False)
        for j in range(nb):
            cols = pl.ds(j * blk, blk)
            y0 = jnp.dot(xs_ref[:, pl.ds(j * w_blk, w_blk)].astype(BF16), cd_ref[j], preferred_element_type=F32)
            ypre = y0 + d_ref[:, cols] * u_ref[:, cols]
            yp_ref[:, cols] = ypre
            yg_ref[:, cols] = _gelu(ypre).astype(BF16)

    tab = pl.BlockSpec((SUBLANE, w), lambda i: (0, 0))
    rows = pl.BlockSpec((tt, c), lambda i: (i, 0))
    return pl.pallas_call(
        kern, name="s5_fwd", grid=(n_rows // tt,),
        in_specs=[pl.BlockSpec((tt, c), lambda i: (i, u_blk * blk // c)), pl.BlockSpec(bdc.shape, lambda i: (0, 0, 0)),
                  pl.BlockSpec(cdc.shape, lambda i: (0, 0, 0)), tab, tab, tab, tab, pl.BlockSpec((1, c), lambda i: (0, 0))],
        out_specs=[pl.BlockSpec((tt, w), lambda i: (i, 0)), rows, rows],
        out_shape=[jax.ShapeDtypeStruct((n_rows, w), F32), jax.ShapeDtypeStruct((n_rows, c), F32),
                   jax.ShapeDtypeStruct((n_rows, c), BF16)],
        scratch_shapes=[pltpu.VMEM((tt, w), F32), pltpu.VMEM((SUBLANE, w), F32)],
        compiler_params=_params("arbitrary"))(proj, bdc, cdc, *tabs, dskip)


def _s5_bwd(dypre, du_skip, xs, proj, u_blk, bdc, cdc, tabs, seq, sb):
    n_rows = proj.shape[0]
    nb, blk, w_blk = bdc.shape
    c, w = nb * blk, nb * w_blk
    tt = _pick(seq, (SCAN_TILE,))
    nt = n_rows // tt
    tn = (((0,), (0,)), ((), ()))

    def kern(dy_ref, ds_ref, x_ref, u_ref, bd_ref, cd_ref, l1, l2, l4, pw, du_ref, da_ref, db_ref, dc_ref,
             gx_ref, lam_ref, car_ref, acc_ref):
        i = pl.program_id(0)

        @pl.when(((nt - i) * tt) % seq == 0)
        def _():
            car_ref[...] = jnp.zeros_like(car_ref)

        @pl.when(i == 0)
        def _():
            acc_ref[...] = jnp.zeros_like(acc_ref)
            db_ref[...] = jnp.zeros_like(db_ref)
            dc_ref[...] = jnp.zeros_like(dc_ref)

        for j in range(nb):
            gx_ref[:, pl.ds(j * w_blk, w_blk)] = lax.dot_general(dy_ref[:, pl.ds(j * blk, blk)], cd_ref[j], NT,
                                                                 preferred_element_type=F32)
        _scan_tile(gx_ref, lam_ref, (l1, l2, l4, pw), car_ref, sb, True, x_ref, acc_ref)
        for j in range(nb):
            cols, wide = pl.ds(j * blk, blk), pl.ds(j * w_blk, w_blk)
            lam = lam_ref[:, wide].astype(BF16)
            du_ref[:, cols] = ds_ref[:, cols] + lax.dot_general(lam, bd_ref[j], NT, preferred_element_type=F32)
            db_ref[j] += lax.dot_general(u_ref[:, cols].astype(BF16), lam, tn, preferred_element_type=F32)
            dc_ref[j] += lax.dot_general(x_ref[:, wide].astype(BF16), dy_ref[:, cols], tn, preferred_element_type=F32)

        @pl.when(i == nt - 1)
        def _():
            da_ref[...] = _colsum(acc_ref[...])

    back = lambda i: (nt - 1 - i, 0)
    tab = pl.BlockSpec((SUBLANE, w), lambda i: (0, 0))
    rows = pl.BlockSpec((tt, c), back)
    whole = lambda a: pl.BlockSpec(a.shape, lambda i: (0, 0, 0))
    return pl.pallas_call(
        kern, name="s5_bwd", grid=(nt,),
        in_specs=[rows, rows, pl.BlockSpec((tt, w), back), pl.BlockSpec((tt, c), lambda i: (nt - 1 - i, u_blk * blk // c)),
                  whole(bdc), whole(cdc), tab, tab, tab, tab],
        out_specs=[rows, pl.BlockSpec((1, w), lambda i: (0, 0)), whole(bdc), whole(cdc)],
        out_shape=[jax.ShapeDtypeStruct((n_rows, c), F32), jax.ShapeDtypeStruct((1, w), F32),
                   jax.ShapeDtypeStruct(bdc.shape, F32), jax.ShapeDtypeStruct(cdc.shape, F32)],
        scratch_shapes=[pltpu.VMEM((tt, w), F32), pltpu.VMEM((tt, w), F32), pltpu.VMEM((SUBLANE, w), F32),
                        pltpu.VMEM((SUBLANE, w), F32)],
        compiler_params=_params("arbitrary"))(dypre, du_skip, xs, proj, bdc, cdc, *tabs)


def _s5_post2(yg, q0, bg, og):
    c = yg.shape[1]

    def body(ins, outs, accs):
        ygv = ins[0][...].astype(F32)
        sg = ygv * _sigmoid(ins[1][...] + ins[2][...])
        outs[0][...] = (sg * _rms_r(sg) * ins[3][...]).astype(BF16)

    return _rowwise("s5_post2", body, yg.shape[0], [(yg, c, 0), (q0, c, 0)], [bg, og], [(c, BF16)], [])[0]


def _s5_post2_bwd(dmixed, yg, q0, bg, og):
    c = yg.shape[1]

    def body(ins, outs, accs):
        dsn, ygv = ins[0][...], ins[1][...].astype(F32)
        s = _sigmoid(ins[2][...] + ins[3][...])
        sg = ygv * s
        r = _rms_r(sg)
        accs[0][...] += _colsum(dsn * sg * r)
        dsg = _rms_bwd(sg, r, ins[4][...], dsn)
        dq = dsg * ygv * s * (1.0 - s)
        outs[0][...] = dq.astype(BF16)
        outs[1][...] = dsg * s
        accs[1][...] += _colsum(dq)

    return _rowwise("s5_post2_bwd", body, yg.shape[0], [(dmixed, c, 1), (yg, c, 0), (q0, c, 0)], [bg, og],
                    [(c, BF16), (c, F32)], [(1, c)] * 2)


def _s5_post1_bwd(dyg1, dyg2, ypre, proj, dskip, after=()):
    c = ypre.shape[1]

    def body(ins, outs, accs):
        dyp = (ins[0][...] + ins[1][...]) * _dgelu(ins[2][...])
        outs[0][...] = dyp.astype(BF16)
        outs[1][...] = dyp * ins[4][...]
        accs[0][...] += _colsum(dyp * ins[3][...])

    return _rowwise("s5_post1_bwd", body, ypre.shape[0], [(dyg1, c, 0), (dyg2, c, 0), (ypre, c, 0), (proj, c, 2)], [dskip],
                    [(c, BF16), (c, F32)], [(1, c)], after=after)


def _place():
    return lax.axis_index("x"), lax.axis_index("y"), lax.axis_index("c")


def _window(ref, axis, q, rows, cols):
    if axis == 0:
        return ref.at[pl.ds(pl.multiple_of(q * rows, SUBLANE), rows), :]
    return ref.at[:, pl.ds(pl.multiple_of(q * cols, LANE), cols)]


def _chip_copies(gather, srcs, lands, shards, axes, send_sems, recv_sems, local_sems):
    x, y, c = _place()
    me = 2 * x + y
    starts, waits = [], []
    for a, (src, land) in enumerate(zip(srcs, lands)):
        rows, cols = shards[a]
        if gather:
            own = pltpu.make_async_copy(src, _window(land, axes[a], me, rows, cols), local_sems.at[a])
        else:
            own = pltpu.make_async_copy(_window(src, axes[a], me, rows, cols), land.at[3], local_sems.at[a])
        starts.append(own)
        waits.append(own)
        for j, (fx, fy) in enumerate(CHIP_RELS):
            px, py = (1 - x) if fx else x, (1 - y) if fy else y
            peer = 2 * px + py
            on = dict(send_sem=send_sems.at[3 * a + j], recv_sem=recv_sems.at[3 * a + j], device_id=(px, py, c),
                      device_id_type=MESH)
            if gather:
                starts.append(pltpu.make_async_remote_copy(src_ref=src, dst_ref=_window(land, axes[a], me, rows, cols), **on))
                waits.append(pltpu.make_async_remote_copy(src_ref=src, dst_ref=_window(land, axes[a], peer, rows, cols), **on))
            else:
                cp = pltpu.make_async_remote_copy(src_ref=_window(src, axes[a], peer, rows, cols), dst_ref=land.at[j], **on)
                starts.append(cp)
                waits.append(cp)
    return starts, waits


HBM = pl.BlockSpec(memory_space=pltpu.HBM)
SEM = pl.BlockSpec(memory_space=pltpu.SEMAPHORE)


def _shard_shapes(gather, arrs, axes):
    if gather:
        return [a.shape for a in arrs]
    return [(a.shape[0] // N_CHIPS, a.shape[1]) if ax == 0 else (a.shape[0], a.shape[1] // N_CHIPS) for a, ax in zip(arrs, axes)]


def _exchange_start(name, gather, arrs, axes, after=()):
    n, n_after = len(arrs), len(after)
    shards = _shard_shapes(gather, arrs, axes)
    if gather:
        land_shapes = [(N_CHIPS * r, c) if ax == 0 else (r, N_CHIPS * c) for (r, c), ax in zip(shards, axes)]
    else:
        land_shapes = [(N_CHIPS,) + s for s in shards]
    lands = [lax.empty(s, a.dtype) for s, a in zip(land_shapes, arrs)]

    def kern(*refs):
        outs = refs[2 * n + n_after:]
        starts, _ = _chip_copies(gather, refs[:n], refs[n:2 * n], shards, axes, outs[0], outs[1], outs[2])
        for cp in starts:
            cp.start()
        outs[-1][...] = jnp.zeros_like(outs[-1])

    kept = [pltpu.HBM(a.shape, a.dtype) for a in arrs] + [pltpu.HBM(s, a.dtype) for s, a in zip(land_shapes, arrs)]
    res = pl.pallas_call(
        kern, name=name, in_specs=[HBM] * (2 * n) + [ANY] * n_after,
        out_specs=[SEM] * 3 + [HBM] * (2 * n) + [pl.BlockSpec(memory_space=pltpu.VMEM)],
        out_shape=[pltpu.SemaphoreType.DMA((3 * n,)), pltpu.SemaphoreType.DMA((3 * n,)), pltpu.SemaphoreType.DMA((n,))]
        + kept + [jax.ShapeDtypeStruct((SUBLANE, LANE), F32)],
        input_output_aliases={i: 3 + i for i in range(2 * n)},
        compiler_params=pltpu.CompilerParams(has_side_effects=pltpu.SideEffectType.DATAFLOW_SIDE_EFFECTING),
    )(*[pltpu.with_memory_space_constraint(a, pltpu.HBM) for a in list(arrs) + lands], *after)
    return res[:3], res[3:3 + n], res[3 + n:3 + 2 * n], res[-1]


def _exchange_wait(name, gather, started, axes, after):
    sems, srcs, lands, _ = started
    n, n_after = len(srcs), len(after)
    shards = _shard_shapes(gather, srcs, axes)

    def kern(*refs):
        sem_refs = refs[2 * n:2 * n + 3]
        _, waits = _chip_copies(gather, refs[:n], refs[n:2 * n], shards, axes, *sem_refs)
        for cp in waits:
            cp.wait()

    res = pl.pallas_call(
        kern, name=name, in_specs=[HBM] * (2 * n) + [SEM] * 3 + [ANY] * n_after, out_specs=[HBM] * (2 * n),
        out_shape=[pltpu.HBM(a.shape, a.dtype) for a in list(srcs) + list(lands)],
        input_output_aliases={i: i for i in range(2 * n)},
        compiler_params=pltpu.CompilerParams(has_side_effects=pltpu.SideEffectType.DATAFLOW_SIDE_EFFECTING),
    )(*srcs, *lands, *sems, *after)
    return res[n:]


def _swap_with_sibling(arrs):
    n = len(arrs)

    def kern(*refs):
        ins, outs = refs[:n], refs[n:2 * n]
        send_sems, recv_sems = refs[2 * n:]
        x, y, c = _place()
        copies = [pltpu.make_async_remote_copy(src_ref=ins[a], dst_ref=outs[a], send_sem=send_sems.at[a],
                                               recv_sem=recv_sems.at[a], device_id=(x, y, 1 - c), device_id_type=MESH)
                  for a in range(n)]
        for cp in copies:
            cp.start()
        for cp in copies:
            cp.wait()

    return pl.pallas_call(
        kern, name="swap_with_sibling", in_specs=[ANY] * n, out_specs=[ANY] * n,
        out_shape=[jax.ShapeDtypeStruct(a.shape, a.dtype) for a in arrs],
        scratch_shapes=[pltpu.SemaphoreType.DMA((n,)), pltpu.SemaphoreType.DMA((n,))],
    )(*arrs)


def _all_reduce_small(buf):
    rels = [(fx, fy, fc) for fx in (0, 1) for fy in (0, 1) for fc in (0, 1)][1:]
    n_dev = len(rels) + 1

    def kern(b_ref, o_ref, recv_ref, send_sems, recv_sems):
        x, y, c = _place()
        me = 4 * x + 2 * y + c
        copies = []
        for k, (fx, fy, fc) in enumerate(rels):
            peer = ((1 - x) if fx else x, (1 - y) if fy else y, (1 - c) if fc else c)
            cp = pltpu.make_async_remote_copy(src_ref=b_ref, dst_ref=recv_ref.at[me], send_sem=send_sems.at[k],
                                              recv_sem=recv_sems.at[k], device_id=peer, device_id_type=MESH)
            cp.start()
            copies.append((cp, peer))
        recv_ref[me] = b_ref[...]
        for k, (cp, (px, py, pc)) in enumerate(copies):
            cp.wait_send()
            pltpu.make_async_remote_copy(src_ref=b_ref, dst_ref=recv_ref.at[4 * px + 2 * py + pc], send_sem=send_sems.at[k],
                                         recv_sem=recv_sems.at[k], device_id=(px, py, pc), device_id_type=MESH).wait_recv()
        acc = recv_ref[0]
        for d in range(1, n_dev):
            acc = acc + recv_ref[d]
        o_ref[...] = acc

    vm = pl.BlockSpec(memory_space=pltpu.VMEM)
    return pl.pallas_call(
        kern, name="all_reduce_small", in_specs=[vm], out_specs=vm, out_shape=jax.ShapeDtypeStruct(buf.shape, F32),
        scratch_shapes=[pltpu.VMEM((n_dev,) + buf.shape, F32), pltpu.SemaphoreType.DMA((n_dev - 1,)),
                        pltpu.SemaphoreType.DMA((n_dev - 1,))],
        compiler_params=pltpu.CompilerParams(vmem_limit_bytes=VMEM_LIMIT_BYTES),
    )(buf)


def _sum_slots(name, parts):
    _, rows, cols = parts.shape
    tr = _pick(rows, (ROW_TILE, 128, 64, 32))

    def kern(p_ref, o_ref):
        o_ref[...] = ((p_ref[3].astype(F32) + p_ref[0].astype(F32)) + p_ref[1].astype(F32)) + p_ref[2].astype(F32)

    return pl.pallas_call(kern, name=name, grid=(rows // tr,),
                          in_specs=[pl.BlockSpec((N_CHIPS, tr, cols), lambda i: (0, i, 0))],
                          out_specs=pl.BlockSpec((tr, cols), lambda i: (i, 0)),
                          out_shape=jax.ShapeDtypeStruct((rows, cols), F32), compiler_params=_params("arbitrary"))(parts)


def _adamw_math(g, w, m, v):
    m2 = ADAM_B1 * m + (1.0 - ADAM_B1) * g
    v2 = ADAM_B2 * v + (1.0 - ADAM_B2) * (g * g)
    m_hat = m2 / (1.0 - ADAM_B1 ** ADAM_STEP)
    v_hat = v2 / (1.0 - ADAM_B2 ** ADAM_STEP)
    return -ADAM_LR * (m_hat / (jnp.sqrt(v_hat) + ADAM_EPS) + ADAM_WD * w), m2, v2


def _adamw(name, parts, w, m, v):
    rows, cols = w.shape
    tr = rows if rows * cols <= WHOLE_ELEMS else _pick(rows, (ROW_TILE, 352, 128, 64, 32, 8))
    n = len(parts)

    def kern(*refs):
        g = refs[0][:, pl.ds(0, cols)]
        for p in refs[1:n]:
            g = g + p[:, pl.ds(0, cols)]
        d, m2, v2 = _adamw_math(g, refs[n][...], refs[n + 1][...], refs[n + 2][...])
        refs[n + 3][...] = g
        refs[n + 4][...] = d
        refs[n + 5][...] = m2
        refs[n + 6][...] = v2

    spec = pl.BlockSpec((tr, cols), lambda i: (i, 0))
    return pl.pallas_call(kern, name=name, grid=(rows // tr,),
                          in_specs=[pl.BlockSpec((tr, p.shape[1]), lambda i: (i, 0)) for p in parts] + [spec] * 3,
                          out_specs=[spec] * 4, out_shape=[jax.ShapeDtypeStruct((rows, cols), F32)] * 4,
                          compiler_params=_params("arbitrary"))(*parts, w, m, v)


def _adamw_many(name, gs, ws, ms, vs):
    n = len(gs)

    def kern(*refs):
        for p in range(n):
            d, m2, v2 = _adamw_math(refs[p][...], refs[n + p][...], refs[2 * n + p][...], refs[3 * n + p][...])
            refs[4 * n + p][...] = d
            refs[5 * n + p][...] = m2
            refs[6 * n + p][...] = v2

    res = pl.pallas_call(kern, name=name, out_shape=[jax.ShapeDtypeStruct(w.shape, F32) for w in ws] * 3,
                         compiler_params=pltpu.CompilerParams(vmem_limit_bytes=VMEM_LIMIT_BYTES))(*gs, *ws, *ms, *vs)
    return res[:n], res[n:2 * n], res[2 * n:]


def _pack(arrs):
    parts, rows = [], []
    for a in arrs:
        r = _round_up(-(-a.size // LANE), SUBLANE)
        parts.append(jnp.pad(a.reshape(-1).astype(F32), (0, r * LANE - a.size)).reshape(r, LANE))
        rows.append(r)
    return jnp.concatenate(parts, axis=0), rows


def _unpack(buf, rows, shapes):
    out, r0 = [], 0
    for r, s in zip(rows, shapes):
        size = math.prod(s)
        out.append(buf[r0:r0 + r].reshape(-1)[:size].reshape(s))
        r0 += r
    return out


def kernel(x, norm_ffn1, ffn1_w1, ffn1_w3, ffn1_w2, norm_mix, w_in, conv_w, conv_b, conv_ln_g, conv_ln_b, conv_out_g, ssm_A_re, ssm_A_im, ssm_log_dt, ssm_B_re, ssm_B_im, ssm_C_re, ssm_C_im, ssm_D, ssm_glu_w, ssm_glu_b, ssm_out_g, w_out, norm_ffn2, ffn2_w1, ffn2_w3, ffn2_w2, norm_final, loss_target, m_norm_ffn1, m_ffn1_w1, m_ffn1_w3, m_ffn1_w2, m_norm_mix, m_w_in, m_conv_w, m_conv_b, m_conv_ln_g, m_conv_ln_b, m_conv_out_g, m_ssm_A_re, m_ssm_A_im, m_ssm_log_dt, m_ssm_B_re, m_ssm_B_im, m_ssm_C_re, m_ssm_C_im, m_ssm_D, m_ssm_glu_w, m_ssm_glu_b, m_ssm_out_g, m_w_out, m_norm_ffn2, m_ffn2_w1, m_ffn2_w3, m_ffn2_w2, m_norm_final, v_norm_ffn1, v_ffn1_w1, v_ffn1_w3, v_ffn1_w2, v_norm_mix, v_w_in, v_conv_w, v_conv_b, v_conv_ln_g, v_conv_ln_b, v_conv_out_g, v_ssm_A_re, v_ssm_A_im, v_ssm_log_dt, v_ssm_B_re, v_ssm_B_im, v_ssm_C_re, v_ssm_C_im, v_ssm_D, v_ssm_glu_w, v_ssm_glu_b, v_ssm_out_g, v_w_out, v_norm_ffn2, v_ffn2_w1, v_ffn2_w3, v_ffn2_w2, v_norm_final):
    given = dict(locals())
    wts = {n: given[n] for n in WEIGHTS}
    n_seq, seq, d = x.shape
    n_rows = n_seq * seq
    xf = x.reshape(n_rows, d)
    tgt = loss_target.reshape(n_rows, d)
    row = lambda a: a.reshape(1, -1)

    f = ffn1_w1.shape[-1]
    fp = _round_up(f, LANE)
    held = lambda n, a: a[0].T if n in TRANSPOSED else a[0]
    shards = []
    for n in BIG:
        s = held(n, wts[n]).astype(BF16)
        if n.startswith('ffn'):
            s = jnp.pad(s, ((0, fp - f), (0, 0)))
        shards.append(s)
    n_taps, c_shard = conv_w.shape[1], conv_w.shape[2]
    shards.append(jnp.pad(conv_w[0], ((0, HALO - n_taps), (0, 0))))
    shard_of = dict(zip(BIG + ['conv_w'], shards))
    axis_of = dict(BIG_AXIS, conv_w=1)
    groups = [['ffn1_w1', 'ffn1_w3'], ['ffn1_w2', 'w_in', 'conv_w', 'ssm_glu_w', 'w_out'], ['ffn2_w1', 'ffn2_w3', 'ffn2_w2']]
    fetch, tok = [], []
    for k, names in enumerate(groups):
        fetch.append(_exchange_start("gather%d_send" % k, True, [shard_of[n] for n in names], [axis_of[n] for n in names], tok))
        tok = [fetch[-1][3]]
    full = {}

    def arrive(k, after):
        lands = _exchange_wait("gather%d_recv" % k, True, fetch[k], [axis_of[n] for n in groups[k]], after)
        full.update(zip(groups[k], lands))

    arrive(0, tok)

    _, n_grp, n_state = ssm_A_re.shape
    grp = ssm_B_re.shape[-1]
    ns = n_grp * n_state
    c_ssm = n_grp * grp
    lr, li = ssm_A_re.reshape(1, ns), ssm_A_im.reshape(1, ns)
    ldt = jnp.repeat(ssm_log_dt.reshape(n_grp), n_state).reshape(1, ns)
    btr = ssm_B_re[0].transpose(2, 0, 1).reshape(grp, ns)
    bti = ssm_B_im[0].transpose(2, 0, 1).reshape(grp, ns)
    ctr = ssm_C_re[0].transpose(1, 0, 2).reshape(grp, ns)
    cti = ssm_C_im[0].transpose(1, 0, 2).reshape(grp, ns)
    _, _, bbr, bbi, pw, pw_falling = _s5_params_fwd(lr, li, ldt, btr, bti)
    nb = c_ssm // LANE
    sb, gpb = ns // nb, n_grp // nb
    diag = (jnp.arange(LANE)[:, None] // grp) == (jnp.arange(sb)[None, :] // n_state)

    def spread(t):
        return jnp.where(diag, jnp.tile(t.reshape(grp, nb, sb).transpose(1, 0, 2), (1, gpb, 1)), 0.0)

    def gather_diag(t):
        return (t * diag).reshape(nb, gpb, grp, sb).sum(1).transpose(1, 0, 2).reshape(grp, ns)

    def interleave(re, im):
        return jnp.stack([re.reshape(-1, nb, sb), im.reshape(-1, nb, sb)], axis=2).reshape(-1, 2 * ns)

    bdc = jnp.concatenate([spread(bbr), spread(bbi)], axis=2).astype(BF16)
    cdc = jnp.concatenate([spread(ctr).transpose(0, 2, 1), -spread(cti).transpose(0, 2, 1)], axis=1).astype(BF16)
    rowi = jnp.arange(SUBLANE)[:, None]
    pwf, pwc = interleave(pw[:, :ns], pw[:, ns:]), interleave(pw[:, :ns], -pw[:, ns:])
    tabs_f = [jnp.where(rowi >= s, pwf[s - 1][None, :], 0.0) for s in (1, 2, 4)] + [pwf]
    tabs_b = [jnp.where(rowi <= SUBLANE - 1 - s, pwc[s - 1][None, :], 0.0) for s in (1, 2, 4)]
    tabs_b.append(interleave(pw_falling[:, :ns], -pw_falling[:, ns:]))
    c_conv = conv_b.shape[1]
    u_blk = 2 * c_conv // LANE

    h1 = _rms_fwd("ffn1_rms", xf, norm_ffn1)
    a1, b1, z1 = _ffn_up("ffn1_up", h1, full['ffn1_w1'], full['ffn1_w3'])
    arrive(1, [z1])
    x1, h2 = _mm("ffn1_down", z1, full['ffn1_w2'], 1, 0, addend=xf, alpha=0.5, post=_post_rms(norm_mix))
    saved1 = (h1, a1, b1, z1)
    cw = full['conv_w']
    proj = _mm("mix_in", h2, full['w_in'], 1, 0, F32)
    assert c_conv == c_ssm and proj.shape[1] == 3 * c_conv
    cpre, an = _conv_fwd(proj, cw, conv_b, conv_ln_g, conv_ln_b, conv_out_g, seq)
    xs, ypre, yg = _s5_fwd(proj, u_blk, bdc, cdc, tabs_f, ssm_D, seq, sb)
    q0 = _mm("s5_gate", yg, full['ssm_glu_w'], 1, 0, F32)
    sn = _s5_post2(yg, q0, ssm_glu_b, ssm_out_g)
    wo = full['w_out']
    x2 = _mm("mix_out_a", an, wo[:c_conv], 1, 0, F32, addend=x1)
    x2, h3 = _mm("mix_out_s", sn, wo[c_conv:], 1, 0, addend=x2, post=_post_rms(norm_ffn2))
    arrive(2, [x2])
    a3, b3, z3 = _ffn_up("ffn2_up", h3, full['ffn2_w1'], full['ffn2_w3'])
    saved2 = (h3, a3, b3, z3)
    dx3, loss_row, d_norm_final = _mm("ffn2_down", z3, full['ffn2_w2'], 1, 0, addend=x2, alpha=0.5,
                                      post=_post_loss(row(norm_final), tgt))

    g = {}
    dx2, g['norm_ffn2'], sent = _ffn_bwd("ffn2", x2, norm_ffn2, full['ffn2_w1'], full['ffn2_w3'], full['ffn2_w2'], saved2, dx3)
    dmixed = _mm("mix_dmixed", dx2, wo, 1, 1, F32)
    dwo = jnp.concatenate([_mm("mix_dwo_a", an, dx2, 0, 0, BF16), _mm("mix_dwo_s", sn, dx2, 0, 0, BF16)], axis=0)
    dq, dyg1, g['ssm_out_g'], g['ssm_glu_b'] = _s5_post2_bwd(dmixed, yg, q0, ssm_glu_b, ssm_out_g)
    dyg2 = _mm("s5_dgate", dq, full['ssm_glu_w'], 1, 1, F32)
    dwg = _mm("s5_dwg", yg, dq, 0, 0, BF16)
    sent['w_out ssm_glu_w'] = (_exchange_start("mix_wo_wg_send", False, [dwo, dwg], [0, 0]), [0, 0])
    dypre, du_skip, g['ssm_D'] = _s5_post1_bwd(dyg1, dyg2, ypre, proj, ssm_D, after=[sent['w_out ssm_glu_w'][0][3]])
    du, dabar, dbdc, dcdc = _s5_bwd(dypre, du_skip, xs, proj, u_blk, bdc, cdc, tabs_b, seq, sb)
    dabar = dabar.reshape(nb, 2, sb)
    dlr, dli, dldt, dbtr, dbti = _s5_params_bwd(lr, li, ldt, btr, bti, dabar[:, 0].reshape(1, ns), dabar[:, 1].reshape(1, ns),
                                                gather_diag(dbdc[:, :, :sb]), gather_diag(dbdc[:, :, sb:]))
    g['ssm_A_re'], g['ssm_A_im'] = dlr, dli
    g['ssm_log_dt'] = dldt.reshape(n_grp, n_state).sum(axis=1)
    g['ssm_B_re'] = dbtr.reshape(grp, n_grp, n_state).transpose(1, 2, 0)
    g['ssm_B_im'] = dbti.reshape(grp, n_grp, n_state).transpose(1, 2, 0)
    g['ssm_C_re'] = gather_diag(dcdc[:, :sb].transpose(0, 2, 1)).reshape(grp, n_grp, n_state).transpose(1, 0, 2)
    g['ssm_C_im'] = -gather_diag(dcdc[:, sb:].transpose(0, 2, 1)).reshape(grp, n_grp, n_state).transpose(1, 0, 2)
    dc, g['conv_out_g'], g['conv_ln_g'], g['conv_ln_b'], g['conv_b'] = _conv_bwd_rows(dmixed, cpre, conv_ln_g, conv_ln_b,
                                                                                    conv_out_g)
    dval, dgate, dcw = _conv_bwd_taps(proj, dc, cw, seq)
    dproj = jnp.concatenate([dval, dgate, du], axis=1)
    sent['w_in'] = (_exchange_start("mix_win_send", False, [_mm("mix_dwin", h2, dproj, 0, 0, BF16)], [1]), [1])
    dx1, g['norm_mix'] = _mm("mix_dh", dproj, full['w_in'], 1, 1, after=[sent['w_in'][0][3]],
                             post=_post_rms_bwd(x1, norm_mix, dx2))
    dx0, g['norm_ffn1'], sent1 = _ffn_bwd("ffn1", xf, norm_ffn1, full['ffn1_w1'], full['ffn1_w3'], full['ffn1_w2'], saved1, dx1)
    sent.update(sent1)
    g['norm_final'] = d_norm_final
    g['conv_w'] = dcw[:n_taps]

    small_shapes = [(n_taps, c_conv) if n == 'conv_w' else wts[n].shape for n in SMALL]
    buf, buf_rows = _pack([g[n] for n in SMALL] + [loss_row])
    total = _unpack(_all_reduce_small(buf), buf_rows, small_shapes + [(1, LANE)])
    loss = total[-1][0, 0]
    grads = dict(zip(SMALL, total[:-1]))
    chip = 2 * lax.axis_index("x") + lax.axis_index("y")
    grads['conv_w'] = lax.dynamic_slice_in_dim(grads['conv_w'], chip * c_shard, c_shard, axis=1)[None]
    flat = lambda a: a.reshape(-1, a.shape[-1])
    small = _adamw_many("adamw_small", *[[flat(src[p + n]) for n in SMALL]
                                         for src, p in ((grads, ''), (given, ''), (given, 'm_'), (given, 'v_'))])
    deltas, new_m, new_v = ({n: o.reshape(wts[n].shape) for n, o in zip(SMALL, outs)} for outs in small)

    slots = {}
    for names, (started, axes) in sent.items():
        lands = _exchange_wait(names.replace(' ', '_') + "_recv", False, started, axes, after=[dx0])
        slots.update(zip(names.split(), lands))
    sums = [_sum_slots("sum_" + n, slots[n]) for n in BIG]
    theirs = _swap_with_sibling(sums)
    for n, mine, other in zip(BIG, sums, theirs):
        grads[n], deltas[n], new_m[n], new_v[n] = (
            (o.T if n in TRANSPOSED else o)[None]
            for o in _adamw("adamw_" + n, [mine, other], held(n, given[n]), held(n, given['m_' + n]), held(n, given['v_' + n])))

    return (loss, dx0.reshape(x.shape), *[grads[n] for n in WEIGHTS], *[deltas[n] for n in WEIGHTS],
            *[new_m[n] for n in WEIGHTS], *[new_v[n] for n in WEIGHTS])
```

```python
import math
from typing import Callable, NamedTuple

import jax
import jax.numpy as jnp
from jax import lax
from jax.experimental import pallas as pl
from jax.experimental.pallas import tpu as pltpu

F32 = jnp.float32
BF16 = jnp.bfloat16
EPS = 1e-6
ADAM_LR, ADAM_B1, ADAM_B2, ADAM_EPS, ADAM_WD, ADAM_STEP = 0.001, 0.9, 0.999, 1e-08, 0.01, 10
MESH = pl.DeviceIdType.MESH
ANY = pl.BlockSpec(memory_space=pl.ANY)
LANE = 128
SUBLANE = 8
VMEM_LIMIT_BYTES = 56 << 20
ROW_TILE = 256
ROW_TILE_ELEMS = 256 * 1024
WHOLE_ELEMS = 512 * 1024
FFN_ROWS = 256
CONV_TILE = 128
CONV_SUB = 32
HALO = 32
SCAN_TILE = 512
SCAN_COLS = 512
N_CHIPS = 4
CHIP_RELS = ((1, 0), (0, 1), (1, 1))
NT = (((1,), (1,)), ((), ()))
GELU_K = math.sqrt(2.0 / math.pi)
GELU_C = 0.044715

WEIGHTS = ['norm_ffn1', 'ffn1_w1', 'ffn1_w3', 'ffn1_w2', 'norm_mix', 'w_in', 'conv_w', 'conv_b', 'conv_ln_g', 'conv_ln_b',
           'conv_out_g', 'ssm_A_re', 'ssm_A_im', 'ssm_log_dt', 'ssm_B_re', 'ssm_B_im', 'ssm_C_re', 'ssm_C_im', 'ssm_D',
           'ssm_glu_w', 'ssm_glu_b', 'ssm_out_g', 'w_out', 'norm_ffn2', 'ffn2_w1', 'ffn2_w3', 'ffn2_w2', 'norm_final']
BIG = ['ffn1_w1', 'ffn1_w3', 'ffn1_w2', 'w_in', 'ssm_glu_w', 'w_out', 'ffn2_w1', 'ffn2_w3', 'ffn2_w2']
BIG_AXIS = {'ffn1_w1': 0, 'ffn1_w3': 0, 'ffn1_w2': 0, 'w_in': 1, 'ssm_glu_w': 0, 'w_out': 0, 'ffn2_w1': 0, 'ffn2_w3': 0,
            'ffn2_w2': 0}
TRANSPOSED = ('ffn1_w1', 'ffn1_w3', 'ffn2_w1', 'ffn2_w3')
SMALL = [n for n in WEIGHTS if n not in BIG]


def _round_up(n, m):
    return -(-n // m) * m


def _pick(n, cands):
    for c in cands:
        if c <= n and n % c == 0:
            return c
    return n


def _params(*sem):
    return pltpu.CompilerParams(dimension_semantics=sem, vmem_limit_bytes=VMEM_LIMIT_BYTES)


def _rms_r(x):
    return lax.rsqrt(jnp.mean(x * x, axis=-1, keepdims=True) + EPS)


def _rms_bwd(x, r, g, dy):
    dyg = dy * g
    return r * dyg - x * (r * r * r) * jnp.mean(x * dyg, axis=-1, keepdims=True)


def _sigmoid(x):
    return jax.nn.sigmoid(x)


def _dsilu(a, s):
    return s * (1.0 + a * (1.0 - s))


def _gelu(x):
    return 0.5 * x * (1.0 + jnp.tanh(GELU_K * (x + GELU_C * x * x * x)))


def _dgelu(x):
    t = jnp.tanh(GELU_K * (x + GELU_C * x * x * x))
    return 0.5 * (1.0 + t) + 0.5 * x * (1.0 - t * t) * GELU_K * (1.0 + 3.0 * GELU_C * x * x)


def _colsum(v):
    return jnp.sum(v, axis=0, keepdims=True)


def _rowwise(name, body, n_rows, row_ins, par_ins, row_outs, acc_outs, after=()):
    widest = max([w for (_, w, _) in row_ins] + [w for (w, _) in row_outs])
    tt = _pick(n_rows, [t for t in (256, 128, 64, 32, 16, 8) if t * widest <= ROW_TILE_ELEMS])
    in_specs = [pl.BlockSpec((tt, w), lambda i, cb=cb: (i, cb)) for (_, w, cb) in row_ins]
    in_specs += [pl.BlockSpec(p.shape, lambda i: (0, 0)) for p in par_ins] + [ANY] * len(after)
    out_specs = [pl.BlockSpec((tt, w), lambda i: (i, 0)) for (w, _) in row_outs]
    out_specs += [pl.BlockSpec((r, w), lambda i: (0, 0)) for (r, w) in acc_outs]
    out_shape = [jax.ShapeDtypeStruct((n_rows, w), dt) for (w, dt) in row_outs]
    out_shape += [jax.ShapeDtypeStruct((r, w), F32) for (r, w) in acc_outs]
    n_in, n_ro = len(row_ins) + len(par_ins), len(row_outs)
    o0 = n_in + len(after)

    def kern(*refs):
        accs = refs[o0 + n_ro:]
        if accs:
            @pl.when(pl.program_id(0) == 0)
            def _():
                for a in accs:
                    a[...] = jnp.zeros_like(a)
        body(refs[:n_in], refs[o0:o0 + n_ro], accs)

    return pl.pallas_call(kern, name=name, grid=(n_rows // tt,), in_specs=in_specs, out_specs=out_specs, out_shape=out_shape,
                          compiler_params=_params("arbitrary"))(*[a for a, _, _ in row_ins], *par_ins, *after)


class Post(NamedTuple):
    rows: list
    gains: list
    outs: list
    sums: list
    fn: Callable


def _post_rms(gain):
    return Post([], [gain], [F32, BF16], [], lambda r, rows, gains: ([r, r * _rms_r(r) * gains[0]], []))


def _post_rms_bwd(x, gain, dres):
    def fn(dh, rows, gains):
        r = _rms_r(rows[0])
        return [rows[1] + _rms_bwd(rows[0], r, gains[0], dh)], [_colsum(dh * rows[0] * r)]

    return Post([x, dres], [gain], [F32], [x.shape[1]], fn)


def _post_loss(gain, tgt):
    d = tgt.shape[1]

    def fn(xv, rows, gains):
        r = _rms_r(xv)
        e = xv * r * gains[0] - rows[0]
        sq = jnp.sum(jnp.sum(e * e, axis=-1, keepdims=True), axis=0, keepdims=True)
        dy = e * (1.0 / d)
        return [_rms_bwd(xv, r, gains[0], dy)], [jnp.broadcast_to(sq * (0.5 / d), (1, LANE)), _colsum(dy * xv * r)]

    return Post([tgt], [gain], [F32], [LANE, d], fn)


def _mm(name, a, b, ca, cb, out_dtype=F32, addend=None, alpha=1.0, a_cols=None, after=(), post=None):
    a_start, a_width = a_cols if a_cols else (0, a.shape[1])
    m, k = (a.shape[0], a_width) if ca == 1 else (a_width, a.shape[0])
    n = b.shape[1 - cb]
    assert b.shape[cb] == k, (name, a.shape, b.shape)
    tm = _pick(m, (512, 256, 128) if post else (1024, 512, 256, 128))
    tn = _pick(n, (1024, 768, 512, 384, 256, 128))
    tk = _pick(k, (1024, 768, 512, 256, 128))
    nk = k // tk
    if ca == 1:
        assert a_start % tk == 0
        a_spec = pl.BlockSpec((tm, tk), lambda i, j, kk: (i, kk + a_start // tk))
    else:
        assert a_start % tm == 0
        a_spec = pl.BlockSpec((tk, tm), lambda i, j, kk: (kk, i + a_start // tm))
    b_spec = pl.BlockSpec((tk, tn), lambda i, j, kk: (kk, j)) if cb == 0 else pl.BlockSpec((tn, tk), lambda i, j, kk: (j, kk))
    o_spec = pl.BlockSpec((tm, tn), lambda i, j, kk: (i, j))
    fixed = lambda w: pl.BlockSpec((1, w), lambda i, j, kk: (0, 0))
    ins, in_specs = [a, b], [a_spec, b_spec]
    if addend is not None:
        ins.append(addend)
        in_specs.append(o_spec)
    n_plain = len(ins)
    n_rows, n_gains = (len(post.rows), len(post.gains)) if post else (0, 0)
    if post:
        assert tn == n, name
        ins += post.rows + post.gains
        in_specs += [o_spec] * n_rows + [fixed(n)] * n_gains
    ins += list(after)
    in_specs += [ANY] * len(after)
    n_in = len(ins)
    if post:
        out_specs = [o_spec] * len(post.outs) + [fixed(w) for w in post.sums]
        out_shape = [jax.ShapeDtypeStruct((m, n), dt) for dt in post.outs] + [jax.ShapeDtypeStruct((1, w), F32) for w in post.sums]
    else:
        out_specs, out_shape = [o_spec], [jax.ShapeDtypeStruct((m, n), out_dtype)]
    n_out = len(out_specs)
    dims = (((ca,), (cb,)), ((), ()))

    def emit(refs, r):
        if alpha != 1.0:
            r = r * alpha
        if addend is not None:
            r = r + refs[2][...].astype(F32)
        outs = refs[n_in:n_in + n_out]
        if post is None:
            outs[0][...] = r.astype(out_dtype)
            return
        vals, incs = post.fn(r, [q[...] for q in refs[n_plain:n_plain + n_rows]],
                             [q[...] for q in refs[n_plain + n_rows:n_plain + n_rows + n_gains]])
        for o_ref, val in zip(outs, vals):
            o_ref[...] = val.astype(o_ref.dtype)
        for s_ref, inc in zip(outs[len(vals):], incs):
            s_ref[...] += inc

    def kern(*refs):
        kk = pl.program_id(2)
        if post and post.sums:
            @pl.when(jnp.logical_and(jnp.logical_and(pl.program_id(0) == 0, pl.program_id(1) == 0), kk == 0))
            def _():
                for s_ref in refs[n_in + len(post.outs):n_in + n_out]:
                    s_ref[...] = jnp.zeros_like(s_ref)

        part = lax.dot_general(refs[0][...].astype(BF16), refs[1][...].astype(BF16), dims, preferred_element_type=F32)
        if nk == 1:
            emit(refs, part)
            return
        acc_ref = refs[-1]

        @pl.when(kk == 0)
        def _():
            acc_ref[...] = part

        @pl.when(kk > 0)
        def _():
            acc_ref[...] += part

        @pl.when(kk == nk - 1)
        def _():
            emit(refs, acc_ref[...])

    res = pl.pallas_call(kern, name=name, grid=(m // tm, n // tn, nk), in_specs=in_specs, out_specs=out_specs,
                         out_shape=out_shape, scratch_shapes=[] if nk == 1 else [pltpu.VMEM((tm, tn), F32)],
                         compiler_params=_params("arbitrary", "arbitrary", "arbitrary"))(*ins)
    return res if post else res[0]


def _rms_fwd(name, x, g):
    def body(ins, outs, accs):
        xv = ins[0][...]
        outs[0][...] = (xv * _rms_r(xv) * ins[1][...]).astype(BF16)

    return _rowwise(name, body, x.shape[0], [(x, x.shape[1], 0)], [g], [(x.shape[1], BF16)], [])[0]


def _ffn_up(name, h, w1, w3):
    t, d = h.shape
    ff = w1.shape[0]
    tm, tn = _pick(t, (1024, 512, 256, 128)), _pick(ff, (1024, 768, 512, 256, 128))

    def kern(h_ref, w1_ref, w3_ref, a_ref, b_ref, z_ref):
        hv = h_ref[...]
        a = lax.dot_general(hv, w1_ref[...], NT, preferred_element_type=F32)
        b = lax.dot_general(hv, w3_ref[...], NT, preferred_element_type=F32)
        a_ref[...] = a.astype(BF16)
        b_ref[...] = b.astype(BF16)
        z_ref[...] = (a * _sigmoid(a) * b).astype(BF16)

    w_spec = pl.BlockSpec((tn, d), lambda i, j: (j, 0))
    o_spec = pl.BlockSpec((tm, tn), lambda i, j: (i, j))
    return pl.pallas_call(kern, name=name, grid=(t // tm, ff // tn),
                          in_specs=[pl.BlockSpec((tm, d), lambda i, j: (i, 0)), w_spec, w_spec], out_specs=[o_spec] * 3,
                          out_shape=[jax.ShapeDtypeStruct((t, ff), BF16)] * 3,
                          compiler_params=_params("arbitrary", "arbitrary"))(h, w1, w3)


def _ffn_dglu(name, dxo, w2, a, b, after=()):
    t, d = dxo.shape
    ff = w2.shape[0]
    tm = _pick(t, (FFN_ROWS, 128))

    def kern(dx_ref, w2_ref, a_ref, b_ref, *rest):
        da_ref, db_ref = rest[-2:]
        dz = lax.dot_general(dx_ref[...].astype(BF16), w2_ref[...], NT, preferred_element_type=F32) * 0.5
        av, bv = a_ref[...].astype(F32), b_ref[...].astype(F32)
        s = _sigmoid(av)
        da_ref[...] = (dz * bv * _dsilu(av, s)).astype(BF16)
        db_ref[...] = (dz * av * s).astype(BF16)

    o_spec = pl.BlockSpec((tm, ff), lambda i: (i, 0))
    return pl.pallas_call(kern, name=name, grid=(t // tm,),
                          in_specs=[pl.BlockSpec((tm, d), lambda i: (i, 0)), pl.BlockSpec((ff, d), lambda i: (0, 0)),
                                    o_spec, o_spec] + [ANY] * len(after),
                          out_specs=[o_spec] * 2, out_shape=[jax.ShapeDtypeStruct((t, ff), BF16)] * 2,
                          compiler_params=_params("arbitrary"))(dxo, w2, a, b, *after)


def _ffn_dh(name, da, db, w1, w3, x, g, dres, after=()):
    t, d = x.shape
    ff = da.shape[1]
    tm = _pick(t, (FFN_ROWS, 128))

    def kern(da_ref, db_ref, w1_ref, w3_ref, x_ref, g_ref, dres_ref, *rest):
        dx_ref, dg_ref = rest[-2:]

        @pl.when(pl.program_id(0) == 0)
        def _():
            dg_ref[...] = jnp.zeros_like(dg_ref)

        dh = (jnp.dot(da_ref[...], w1_ref[...], preferred_element_type=F32)
              + jnp.dot(db_ref[...], w3_ref[...], preferred_element_type=F32))
        xv = x_ref[...]
        r = _rms_r(xv)
        dx_ref[...] = dres_ref[...] + _rms_bwd(xv, r, g_ref[...], dh)
        dg_ref[...] += _colsum(dh * xv * r)

    act = pl.BlockSpec((tm, ff), lambda i: (i, 0))
    wgt = pl.BlockSpec((ff, d), lambda i: (0, 0))
    rows = pl.BlockSpec((tm, d), lambda i: (i, 0))
    gain = pl.BlockSpec((1, d), lambda i: (0, 0))
    return pl.pallas_call(kern, name=name, grid=(t // tm,),
                          in_specs=[act, act, wgt, wgt, rows, gain, rows] + [ANY] * len(after), out_specs=[rows, gain],
                          out_shape=[jax.ShapeDtypeStruct((t, d), F32), jax.ShapeDtypeStruct((1, d), F32)],
                          compiler_params=_params("arbitrary"))(da, db, w1, w3, x, g, dres, *after)


def _ffn_bwd(tag, x, g, w1, w3, w2, saved, dxo):
    h, a, b, z = saved
    dw2 = _mm(tag + "_dw2", z, dxo, 0, 0, BF16, alpha=0.5)
    s2 = _exchange_start(tag + "_w2_send", "scatter", [dw2], [0])
    da, db = _ffn_dglu(tag + "_dglu", dxo, w2, a, b, after=[s2[3]])
    dw1 = _mm(tag + "_dw1", da, h, 0, 0, BF16)
    s1 = _exchange_start(tag + "_w1_send", "scatter", [dw1], [0])
    dw3 = _mm(tag + "_dw3", db, h, 0, 0, BF16, after=[s1[3]])
    s3 = _exchange_start(tag + "_w3_send", "scatter", [dw3], [0])
    dx, dg = _ffn_dh(tag + "_dh", da, db, w1, w3, x, g, dxo, after=[s3[3]])
    return dx, dg, {tag + "_w1": (s1, [0]), tag + "_w3": (s3, [0]), tag + "_w2": (s2, [0])}


def _shift_copies(ext_ref, sh_ref):
    n = ext_ref.shape[0] - SUBLANE
    for r in range(1, SUBLANE):
        sh_ref[r, pl.ds(0, n), :] = ext_ref[pl.ds(r, n), :]


def _rows_at(ext_ref, sh_ref, off, rows):
    r = off % SUBLANE
    return ext_ref[pl.ds(off, rows), :] if r == 0 else sh_ref[r, pl.ds(off - r, rows), :]


def _conv_fwd(proj, cw, cb, lng, lnb, og, seq):
    n_rows, c = proj.shape[0], cb.shape[1]
    kw = HALO - 1
    tt = _pick(seq, (CONV_TILE,))
    hb = tt // HALO

    def kern(v_ref, g_ref, vp_ref, gp_ref, w_ref, cb_ref, lg_ref, lb_ref, og_ref, c_ref, an_ref, ext_ref, sh_ref):
        first = (pl.program_id(0) * tt) % seq == 0
        ext_ref[pl.ds(HALO, tt), :] = v_ref[...] * _sigmoid(g_ref[...])
        ext_ref[pl.ds(0, HALO), :] = vp_ref[...] * _sigmoid(gp_ref[...]) * jnp.where(first, 0.0, 1.0)
        _shift_copies(ext_ref, sh_ref)
        for r0 in range(0, tt, CONV_SUB):
            rows = min(CONV_SUB, tt - r0)
            acc = jnp.zeros((rows, c), F32)
            for k in range(kw):
                acc = acc + w_ref[pl.ds(k, 1), :] * _rows_at(ext_ref, sh_ref, r0 + HALO - (kw - 1) + k, rows)
            c_ref[pl.ds(r0, rows), :] = acc + cb_ref[...]
        cv = c_ref[...]
        mu = jnp.mean(cv, axis=-1, keepdims=True)
        xc = cv - mu
        rstd = lax.rsqrt(jnp.mean(xc * xc, axis=-1, keepdims=True) + EPS)
        lv = xc * rstd * lg_ref[...] + lb_ref[...]
        sl = lv * _sigmoid(lv)
        an_ref[...] = (sl * _rms_r(sl) * og_ref[...]).astype(BF16)

    cur = lambda cbk: pl.BlockSpec((tt, c), lambda i: (i, cbk))
    prev = lambda cbk: pl.BlockSpec((HALO, c), lambda i: (jnp.maximum(i * hb - 1, 0), cbk))
    par = lambda p: pl.BlockSpec(p.shape, lambda i: (0, 0))
    return pl.pallas_call(
        kern, name="conv_fwd", grid=(n_rows // tt,),
        in_specs=[cur(0), cur(1), prev(0), prev(1), par(cw), par(cb), par(lng), par(lnb), par(og)],
        out_specs=[pl.BlockSpec((tt, c), lambda i: (i, 0))] * 2,
        out_shape=[jax.ShapeDtypeStruct((n_rows, c), F32), jax.ShapeDtypeStruct((n_rows, c), BF16)],
        scratch_shapes=[pltpu.VMEM((tt + HALO, c), F32), pltpu.VMEM((SUBLANE, tt + HALO, c), F32)],
        compiler_params=_params("arbitrary"),
    )(proj, proj, proj, proj, cw, cb, lng, lnb, og)


def _conv_bwd_rows(dmixed, cpre, lng, lnb, og):
    c = cpre.shape[1]

    def body(ins, outs, accs):
        dan, cv, lg, lb, ogv = ins[0][...], ins[1][...], ins[2][...], ins[3][...], ins[4][...]
        mu = jnp.mean(cv, axis=-1, keepdims=True)
        xc = cv - mu
        rstd = lax.rsqrt(jnp.mean(xc * xc, axis=-1, keepdims=True) + EPS)
        xh = xc * rstd
        lv = xh * lg + lb
        s = _sigmoid(lv)
        sl = lv * s
        r2 = _rms_r(sl)
        accs[0][...] += _colsum(dan * sl * r2)
        dl = _rms_bwd(sl, r2, ogv, dan) * _dsilu(lv, s)
        accs[1][...] += _colsum(dl * xh)
        accs[2][...] += _colsum(dl)
        dxh = dl * lg
        dc = rstd * (dxh - jnp.mean(dxh, axis=-1, keepdims=True) - xh * jnp.mean(dxh * xh, axis=-1, keepdims=True))
        outs[0][...] = dc
        accs[3][...] += _colsum(dc)

    return _rowwise("conv_bwd_rows", body, cpre.shape[0], [(dmixed, c, 0), (cpre, c, 0)], [lng, lnb, og], [(c, F32)],
                    [(1, c)] * 4)


def _conv_bwd_taps(proj, dc, cw, seq):
    n_rows, c = dc.shape
    kw = HALO - 1
    tt = _pick(seq, (CONV_TILE,))
    hb = tt // HALO
    last_blk = n_rows // HALO - 1

    def kern(v_ref, g_ref, vp_ref, gp_ref, dc_ref, dn_ref, w_ref, dv_ref, dg_ref, dw_ref, exta_ref, extd_ref, sha_ref, shd_ref):
        i = pl.program_id(0)
        first = (i * tt) % seq == 0
        last = ((i + 1) * tt) % seq == 0

        @pl.when(i == 0)
        def _():
            dw_ref[...] = jnp.zeros_like(dw_ref)

        sg = _sigmoid(g_ref[...])
        exta_ref[pl.ds(HALO, tt), :] = v_ref[...] * sg
        exta_ref[pl.ds(0, HALO), :] = vp_ref[...] * _sigmoid(gp_ref[...]) * jnp.where(first, 0.0, 1.0)
        dcv = dc_ref[...]
        extd_ref[pl.ds(0, tt), :] = dcv
        extd_ref[pl.ds(tt, HALO), :] = dn_ref[...] * jnp.where(last, 0.0, 1.0)
        _shift_copies(exta_ref, sha_ref)
        _shift_copies(extd_ref, shd_ref)
        for k in range(kw):
            dw_ref[pl.ds(k, 1), :] += _colsum(_rows_at(exta_ref, sha_ref, HALO - (kw - 1) + k, tt) * dcv)
        for r0 in range(0, tt, CONV_SUB):
            rows = min(CONV_SUB, tt - r0)
            acc = jnp.zeros((rows, c), F32)
            for k in range(kw):
                acc = acc + w_ref[pl.ds(k, 1), :] * _rows_at(extd_ref, shd_ref, r0 + (kw - 1) - k, rows)
            dv_ref[pl.ds(r0, rows), :] = acc
        da = dv_ref[...]
        dv_ref[...] = da * sg
        dg_ref[...] = da * v_ref[...] * sg * (1.0 - sg)

    cur = lambda cbk: pl.BlockSpec((tt, c), lambda i: (i, cbk))
    prev = lambda cbk: pl.BlockSpec((HALO, c), lambda i: (jnp.maximum(i * hb - 1, 0), cbk))
    nxt = pl.BlockSpec((HALO, c), lambda i: (jnp.minimum((i + 1) * hb, last_blk), 0))
    return pl.pallas_call(
        kern, name="conv_bwd_taps", grid=(n_rows // tt,),
        in_specs=[cur(0), cur(1), prev(0), prev(1), cur(0), nxt, pl.BlockSpec(cw.shape, lambda i: (0, 0))],
        out_specs=[cur(0), cur(0), pl.BlockSpec((HALO, c), lambda i: (0, 0))],
        out_shape=[jax.ShapeDtypeStruct((n_rows, c), F32), jax.ShapeDtypeStruct((n_rows, c), F32),
                   jax.ShapeDtypeStruct((HALO, c), F32)],
        scratch_shapes=[pltpu.VMEM((tt + HALO, c), F32)] * 2 + [pltpu.VMEM((SUBLANE, tt + HALO, c), F32)] * 2,
        compiler_params=_params("arbitrary"),
    )(proj, proj, proj, proj, dc, dc, cw)


def _s5_params_fwd(lr, li, ldt, btr, bti):
    ns = lr.shape[1]

    def kern(lr_ref, li_ref, ldt_ref, btr_ref, bti_ref, ar_ref, ai_ref, bbr_ref, bbi_ref, pw_ref, pwr_ref):
        lrv, liv = lr_ref[...], li_ref[...]
        dt = jnp.exp(ldt_ref[...])
        zr, zi = lrv * dt, liv * dt
        mag = jnp.exp(zr)
        ar, ai = mag * jnp.cos(zi), mag * jnp.sin(zi)
        den = lrv * lrv + liv * liv
        nr = ar - 1.0
        cr = (nr * lrv + ai * liv) / den
        ci = (ai * lrv - nr * liv) / den
        ar_ref[...] = ar
        ai_ref[...] = ai
        bbr_ref[...] = cr * btr_ref[...] - ci * bti_ref[...]
        bbi_ref[...] = cr * bti_ref[...] + ci * btr_ref[...]
        pr, pi = ar, ai
        for e in range(SUBLANE):
            for ref, at in ((pw_ref, e), (pwr_ref, SUBLANE - 1 - e)):
                ref[pl.ds(at, 1), pl.ds(0, ns)] = pr
                ref[pl.ds(at, 1), pl.ds(ns, ns)] = pi
            pr, pi = pr * ar - pi * ai, pr * ai + pi * ar

    h = btr.shape[0]
    shapes = [jax.ShapeDtypeStruct((1, ns), F32)] * 2 + [jax.ShapeDtypeStruct((h, ns), F32)] * 2
    shapes += [jax.ShapeDtypeStruct((SUBLANE, 2 * ns), F32)] * 2
    return pl.pallas_call(kern, name="s5_params_fwd", out_shape=shapes)(lr, li, ldt, btr, bti)


def _s5_params_bwd(lr, li, ldt, btr, bti, dar, dai, dbbr, dbbi):
    def kern(lr_ref, li_ref, ldt_ref, btr_ref, bti_ref, dar_ref, dai_ref, dbr_ref, dbi_ref,
             dlr_ref, dli_ref, dldt_ref, dbtr_ref, dbti_ref):
        lrv, liv = lr_ref[...], li_ref[...]
        dt = jnp.exp(ldt_ref[...])
        zr, zi = lrv * dt, liv * dt
        mag = jnp.exp(zr)
        ar, ai = mag * jnp.cos(zi), mag * jnp.sin(zi)
        den = lrv * lrv + liv * liv
        nr = ar - 1.0
        cr = (nr * lrv + ai * liv) / den
        ci = (ai * lrv - nr * liv) / den
        dbr, dbi, br, bi = dbr_ref[...], dbi_ref[...], btr_ref[...], bti_ref[...]
        dbtr_ref[...] = cr * dbr + ci * dbi
        dbti_ref[...] = cr * dbi - ci * dbr
        dcr = _colsum(br * dbr + bi * dbi)
        dci = _colsum(br * dbi - bi * dbr)
        ir, ii = lrv / den, -liv / den
        dnr = ir * dcr + ii * dci
        dni = ir * dci - ii * dcr
        wr, wi = cr * ir - ci * ii, cr * ii + ci * ir
        dl1r = -(wr * dcr + wi * dci)
        dl1i = -(wr * dci - wi * dcr)
        dtr, dti = dar_ref[...] + dnr, dai_ref[...] + dni
        dzr = ar * dtr + ai * dti
        dzi = ar * dti - ai * dtr
        dlr_ref[...] = dl1r + dt * dzr
        dli_ref[...] = dl1i + dt * dzi
        dldt_ref[...] = (dzr * lrv + dzi * liv) * dt

    ns, h = lr.shape[1], btr.shape[0]
    shapes = [jax.ShapeDtypeStruct((1, ns), F32)] * 3 + [jax.ShapeDtypeStruct((h, ns), F32)] * 2
    return pl.pallas_call(kern, name="s5_params_bwd", out_shape=shapes)(lr, li, ldt, btr, bti, dar, dai, dbbr, dbbi)


def _scan_tile(s_ref, o_ref, tabs, car_ref, sb, reverse, x_ref=None, acc_ref=None):
    l1, l2, l4, pw = tabs
    rows_t, w = s_ref.shape
    ng = rows_t // SUBLANE
    cw = _pick(sb, (SCAN_COLS,))
    carry_row = 0 if reverse else SUBLANE - 1
    row = lax.broadcasted_iota(jnp.int32, (SUBLANE, cw), 0)

    def group(gi, carry):
        g = (ng - 1 - gi) if reverse else gi
        rows = pl.ds(pl.multiple_of(g * SUBLANE, SUBLANE), SUBLANE)
        for c0 in [b0 + o for b0 in range(0, w, 2 * sb) for o in range(0, sb, cw)]:
            cr, ci = pl.ds(c0, cw), pl.ds(c0 + sb, cw)
            xr, xi = s_ref[rows, cr], s_ref[rows, ci]
            for s, lt in ((1, l1), (2, l2), (4, l4)):
                sh = (SUBLANE - s) if reverse else s
                sr, si = pltpu.roll(xr, sh, 0), pltpu.roll(xi, sh, 0)
                ar, ai = lt[:, cr], lt[:, ci]
                xr, xi = xr + ar * sr - ai * si, xi + ar * si + ai * sr
            kr, ki = car_ref[pl.ds(carry_row, 1), cr], car_ref[pl.ds(carry_row, 1), ci]
            pr, pi = pw[:, cr], pw[:, ci]
            xr, xi = xr + pr * kr - pi * ki, xi + pr * ki + pi * kr
            o_ref[rows, cr] = xr
            o_ref[rows, ci] = xi
            car_ref[:, cr] = xr
            car_ref[:, ci] = xi
            if acc_ref is not None:
                nr = jnp.where(row == SUBLANE - 1, kr, pltpu.roll(xr, SUBLANE - 1, 0))
                ni = jnp.where(row == SUBLANE - 1, ki, pltpu.roll(xi, SUBLANE - 1, 0))
                pxr, pxi = x_ref[rows, cr], x_ref[rows, ci]
                acc_ref[:, cr] += nr * pxr + ni * pxi
                acc_ref[:, ci] += ni * pxr - nr * pxi
        return carry

    lax.fori_loop(0, ng, group, 0)


def _s5_fwd(proj, u_blk, bdc, cdc, tabs, dskip, seq, sb):
    n_rows = proj.shape[0]
    nb, blk, w_blk = bdc.shape
    c, w = nb * blk, nb * w_blk
    tt = _pick(seq, (SCAN_TILE,))

    def kern(u_ref, bd_ref, cd_ref, l1, l2, l4, pw, d_ref, xs_ref, yp_ref, yg_ref, bu_ref, car_ref):
        @pl.when((pl.program_id(0) * tt) % seq == 0)
        def _():
            car_ref[...] = jnp.zeros_like(car_ref)

        for j in range(nb):
            bu_ref[:, pl.ds(j * w_blk, w_blk)] = jnp.dot(u_ref[:, pl.ds(j * blk, blk)].astype(BF16), bd_ref[j],
                                                         preferred_element_type=F32)
        _scan_tile(bu_ref, xs_ref, (l1, l2, l4, pw), car_ref, sb, False)
        for j in range(nb):
            cols = pl.ds(j * blk, blk)
            y0 = jnp.dot(xs_ref[:, pl.ds(j * w_blk, w_blk)].astype(BF16), cd_ref[j], preferred_element_type=F32)
            ypre = y0 + d_ref[:, cols] * u_ref[:, cols]
            yp_ref[:, cols] = ypre
            yg_ref[:, cols] = _gelu(ypre).astype(BF16)

    tab = pl.BlockSpec((SUBLANE, w), lambda i: (0, 0))
    rows = pl.BlockSpec((tt, c), lambda i: (i, 0))
    return pl.pallas_call(
        kern, name="s5_fwd", grid=(n_rows // tt,),
        in_specs=[pl.BlockSpec((tt, c), lambda i: (i, u_blk * blk // c)), pl.BlockSpec(bdc.shape, lambda i: (0, 0, 0)),
                  pl.BlockSpec(cdc.shape, lambda i: (0, 0, 0)), tab, tab, tab, tab, pl.BlockSpec((1, c), lambda i: (0, 0))],
        out_specs=[pl.BlockSpec((tt, w), lambda i: (i, 0)), rows, rows],
        out_shape=[jax.ShapeDtypeStruct((n_rows, w), F32), jax.ShapeDtypeStruct((n_rows, c), F32),
                   jax.ShapeDtypeStruct((n_rows, c), BF16)],
        scratch_shapes=[pltpu.VMEM((tt, w), F32), pltpu.VMEM((SUBLANE, w), F32)],
        compiler_params=_params("arbitrary"))(proj, bdc, cdc, *tabs, dskip)


def _s5_bwd(dypre, du_skip, xs, proj, u_blk, bdc, cdc, tabs, seq, sb):
    n_rows = proj.shape[0]
    nb, blk, w_blk = bdc.shape
    c, w = nb * blk, nb * w_blk
    tt = _pick(seq, (SCAN_TILE,))
    nt = n_rows // tt
    tn = (((0,), (0,)), ((), ()))

    def kern(dy_ref, ds_ref, x_ref, u_ref, bd_ref, cd_ref, l1, l2, l4, pw, du_ref, da_ref, db_ref, dc_ref,
             gx_ref, lam_ref, car_ref, acc_ref):
        i = pl.program_id(0)

        @pl.when(((nt - i) * tt) % seq == 0)
        def _():
            car_ref[...] = jnp.zeros_like(car_ref)

        @pl.when(i == 0)
        def _():
            acc_ref[...] = jnp.zeros_like(acc_ref)
            db_ref[...] = jnp.zeros_like(db_ref)
            dc_ref[...] = jnp.zeros_like(dc_ref)

        for j in range(nb):
            gx_ref[:, pl.ds(j * w_blk, w_blk)] = lax.dot_general(dy_ref[:, pl.ds(j * blk, blk)], cd_ref[j], NT,
                                                                 preferred_element_type=F32)
        _scan_tile(gx_ref, lam_ref, (l1, l2, l4, pw), car_ref, sb, True, x_ref, acc_ref)
        for j in range(nb):
            cols, wide = pl.ds(j * blk, blk), pl.ds(j * w_blk, w_blk)
            lam = lam_ref[:, wide].astype(BF16)
            du_ref[:, cols] = ds_ref[:, cols] + lax.dot_general(lam, bd_ref[j], NT, preferred_element_type=F32)
            db_ref[j] += lax.dot_general(u_ref[:, cols].astype(BF16), lam, tn, preferred_element_type=F32)
            dc_ref[j] += lax.dot_general(x_ref[:, wide].astype(BF16), dy_ref[:, cols], tn, preferred_element_type=F32)

        @pl.when(i == nt - 1)
        def _():
            da_ref[...] = _colsum(acc_ref[...])

    back = lambda i: (nt - 1 - i, 0)
    tab = pl.BlockSpec((SUBLANE, w), lambda i: (0, 0))
    rows = pl.BlockSpec((tt, c), back)
    whole = lambda a: pl.BlockSpec(a.shape, lambda i: (0, 0, 0))
    return pl.pallas_call(
        kern, name="s5_bwd", grid=(nt,),
        in_specs=[rows, rows, pl.BlockSpec((tt, w), back), pl.BlockSpec((tt, c), lambda i: (nt - 1 - i, u_blk * blk // c)),
                  whole(bdc), whole(cdc), tab, tab, tab, tab],
        out_specs=[rows, pl.BlockSpec((1, w), lambda i: (0, 0)), whole(bdc), whole(cdc)],
        out_shape=[jax.ShapeDtypeStruct((n_rows, c), F32), jax.ShapeDtypeStruct((1, w), F32),
                   jax.ShapeDtypeStruct(bdc.shape, F32), jax.ShapeDtypeStruct(cdc.shape, F32)],
        scratch_shapes=[pltpu.VMEM((tt, w), F32), pltpu.VMEM((tt, w), F32), pltpu.VMEM((SUBLANE, w), F32),
                        pltpu.VMEM((SUBLANE, w), F32)],
        compiler_params=_params("arbitrary"))(dypre, du_skip, xs, proj, bdc, cdc, *tabs)


def _s5_post2(yg, q0, bg, og):
    c = yg.shape[1]

    def body(ins, outs, accs):
        ygv = ins[0][...].astype(F32)
        sg = ygv * _sigmoid(ins[1][...] + ins[2][...])
        outs[0][...] = (sg * _rms_r(sg) * ins[3][...]).astype(BF16)

    return _rowwise("s5_post2", body, yg.shape[0], [(yg, c, 0), (q0, c, 0)], [bg, og], [(c, BF16)], [])[0]


def _s5_post2_bwd(dmixed, yg, q0, bg, og):
    c = yg.shape[1]

    def body(ins, outs, accs):
        dsn, ygv = ins[0][...], ins[1][...].astype(F32)
        s = _sigmoid(ins[2][...] + ins[3][...])
        sg = ygv * s
        r = _rms_r(sg)
        accs[0][...] += _colsum(dsn * sg * r)
        dsg = _rms_bwd(sg, r, ins[4][...], dsn)
        dq = dsg * ygv * s * (1.0 - s)
        outs[0][...] = dq.astype(BF16)
        outs[1][...] = dsg * s
        accs[1][...] += _colsum(dq)

    return _rowwise("s5_post2_bwd", body, yg.shape[0], [(dmixed, c, 1), (yg, c, 0), (q0, c, 0)], [bg, og],
                    [(c, BF16), (c, F32)], [(1, c)] * 2)


def _s5_post1_bwd(dyg1, dyg2, ypre, proj, dskip, after=()):
    c = ypre.shape[1]

    def body(ins, outs, accs):
        dyp = (ins[0][...] + ins[1][...]) * _dgelu(ins[2][...])
        outs[0][...] = dyp.astype(BF16)
        outs[1][...] = dyp * ins[4][...]
        accs[0][...] += _colsum(dyp * ins[3][...])

    return _rowwise("s5_post1_bwd", body, ypre.shape[0], [(dyg1, c, 0), (dyg2, c, 0), (ypre, c, 0), (proj, c, 2)], [dskip],
                    [(c, BF16), (c, F32)], [(1, c)], after=after)


def _place():
    return lax.axis_index("x"), lax.axis_index("y"), lax.axis_index("c")


def _window(ref, axis, q, rows, cols):
    if axis == 0:
        return ref.at[pl.ds(pl.multiple_of(q * rows, SUBLANE), rows), :]
    return ref.at[:, pl.ds(pl.multiple_of(q * cols, LANE), cols)]


ALL_RELS = [(fx, fy, fc) for fx in (0, 1) for fy in (0, 1) for fc in (0, 1)][1:]
N_PEERS = {"gather": 3, "scatter": 3, "sibling": 1, "all": len(ALL_RELS)}


def _copies(kind, srcs, lands, shards, axes, send_sems, recv_sems, local_sems):
    x, y, c = _place()
    me, dev = 2 * x + y, 4 * x + 2 * y + c
    n_peers = N_PEERS[kind]
    starts, waits = [], []
    for a, (src, land) in enumerate(zip(srcs, lands)):
        on = lambda k, peer: dict(send_sem=send_sems.at[n_peers * a + k], recv_sem=recv_sems.at[n_peers * a + k],
                                  device_id=peer, device_id_type=MESH)
        if kind == "sibling":
            cp = pltpu.make_async_remote_copy(src_ref=src, dst_ref=land, **on(0, (x, y, 1 - c)))
            starts.append(cp)
            waits.append(cp)
            continue
        if kind == "all":
            own = pltpu.make_async_copy(src, land.at[dev], local_sems.at[a])
            starts.append(own)
            waits.append(own)
            for k, (fx, fy, fc) in enumerate(ALL_RELS):
                px, py, pc = (1 - x) if fx else x, (1 - y) if fy else y, (1 - c) if fc else c
                starts.append(pltpu.make_async_remote_copy(src_ref=src, dst_ref=land.at[dev], **on(k, (px, py, pc))))
                waits.append(pltpu.make_async_remote_copy(src_ref=src, dst_ref=land.at[4 * px + 2 * py + pc],
                                                          **on(k, (px, py, pc))))
            continue
        rows, cols = shards[a]
        if kind == "gather":
            own = pltpu.make_async_copy(src, _window(land, axes[a], me, rows, cols), local_sems.at[a])
        else:
            own = pltpu.make_async_copy(_window(src, axes[a], me, rows, cols), land.at[3], local_sems.at[a])
        starts.append(own)
        waits.append(own)
        for j, (fx, fy) in enumerate(CHIP_RELS):
            px, py = (1 - x) if fx else x, (1 - y) if fy else y
            peer = 2 * px + py
            if kind == "gather":
                starts.append(pltpu.make_async_remote_copy(src_ref=src, dst_ref=_window(land, axes[a], me, rows, cols),
                                                           **on(j, (px, py, c))))
                waits.append(pltpu.make_async_remote_copy(src_ref=src, dst_ref=_window(land, axes[a], peer, rows, cols),
                                                          **on(j, (px, py, c))))
            else:
                cp = pltpu.make_async_remote_copy(src_ref=_window(src, axes[a], peer, rows, cols), dst_ref=land.at[j],
                                                  **on(j, (px, py, c)))
                starts.append(cp)
                waits.append(cp)
    return starts, waits


HBM = pl.BlockSpec(memory_space=pltpu.HBM)
SEM = pl.BlockSpec(memory_space=pltpu.SEMAPHORE)


def _shard_shapes(kind, arrs, axes):
    if kind != "scatter":
        return [a.shape for a in arrs]
    return [(a.shape[0] // N_CHIPS, a.shape[1]) if ax == 0 else (a.shape[0], a.shape[1] // N_CHIPS) for a, ax in zip(arrs, axes)]


def _land_shapes(kind, arrs, axes):
    if kind == "gather":
        return [(N_CHIPS * a.shape[0], a.shape[1]) if ax == 0 else (a.shape[0], N_CHIPS * a.shape[1]) for a, ax in zip(arrs, axes)]
    if kind == "scatter":
        return [(N_CHIPS,) + s for s in _shard_shapes(kind, arrs, axes)]
    return [a.shape if kind == "sibling" else (len(ALL_RELS) + 1,) + a.shape for a in arrs]


def _exchange_start(name, kind, arrs, axes, after=()):
    n, n_after = len(arrs), len(after)
    shards = _shard_shapes(kind, arrs, axes)
    land_shapes = _land_shapes(kind, arrs, axes)
    lands = [lax.empty(s, a.dtype) for s, a in zip(land_shapes, arrs)]

    def kern(*refs):
        outs = refs[2 * n + n_after:]
        starts, _ = _copies(kind, refs[:n], refs[n:2 * n], shards, axes, outs[0], outs[1], outs[2])
        for cp in starts:
            cp.start()
        outs[-1][...] = jnp.zeros_like(outs[-1])

    kept = [pltpu.HBM(a.shape, a.dtype) for a in arrs] + [pltpu.HBM(s, a.dtype) for s, a in zip(land_shapes, arrs)]
    n_sems = N_PEERS[kind] * n
    res = pl.pallas_call(
        kern, name=name, in_specs=[HBM] * (2 * n) + [ANY] * n_after,
        out_specs=[SEM] * 3 + [HBM] * (2 * n) + [pl.BlockSpec(memory_space=pltpu.VMEM)],
        out_shape=[pltpu.SemaphoreType.DMA((n_sems,)), pltpu.SemaphoreType.DMA((n_sems,)), pltpu.SemaphoreType.DMA((n,))]
        + kept + [jax.ShapeDtypeStruct((SUBLANE, LANE), F32)],
        input_output_aliases={i: 3 + i for i in range(2 * n)},
        compiler_params=pltpu.CompilerParams(has_side_effects=pltpu.SideEffectType.DATAFLOW_SIDE_EFFECTING),
    )(*[pltpu.with_memory_space_constraint(a, pltpu.HBM) for a in list(arrs) + lands], *after)
    return res[:3], res[3:3 + n], res[3 + n:3 + 2 * n], res[-1]


def _exchange_wait(name, kind, started, axes, after):
    sems, srcs, lands, _ = started
    n, n_after = len(srcs), len(after)
    shards = _shard_shapes(kind, srcs, axes)

    def kern(*refs):
        sem_refs = refs[2 * n:2 * n + 3]
        _, waits = _copies(kind, refs[:n], refs[n:2 * n], shards, axes, *sem_refs)
        for cp in waits:
            cp.wait()

    res = pl.pallas_call(
        kern, name=name, in_specs=[HBM] * (2 * n) + [SEM] * 3 + [ANY] * n_after, out_specs=[HBM] * (2 * n),
        out_shape=[pltpu.HBM(a.shape, a.dtype) for a in list(srcs) + list(lands)],
        input_output_aliases={i: i for i in range(2 * n)},
        compiler_params=pltpu.CompilerParams(has_side_effects=pltpu.SideEffectType.DATAFLOW_SIDE_EFFECTING),
    )(*srcs, *lands, *sems, *after)
    return res[n:]


def _sum_devices(parts):
    def kern(p_ref, o_ref):
        acc = p_ref[0]
        for d in range(1, parts.shape[0]):
            acc = acc + p_ref[d]
        o_ref[...] = acc

    return pl.pallas_call(kern, name="sum_devices", out_shape=jax.ShapeDtypeStruct(parts.shape[1:], F32),
                          compiler_params=pltpu.CompilerParams(vmem_limit_bytes=VMEM_LIMIT_BYTES))(parts)


def _sum_slots(name, parts):
    _, rows, cols = parts.shape
    tr = _pick(rows, (ROW_TILE, 128, 64, 32))

    def kern(p_ref, o_ref):
        o_ref[...] = ((p_ref[3].astype(F32) + p_ref[0].astype(F32)) + p_ref[1].astype(F32)) + p_ref[2].astype(F32)

    return pl.pallas_call(kern, name=name, grid=(rows // tr,),
                          in_specs=[pl.BlockSpec((N_CHIPS, tr, cols), lambda i: (0, i, 0))],
                          out_specs=pl.BlockSpec((tr, cols), lambda i: (i, 0)),
                          out_shape=jax.ShapeDtypeStruct((rows, cols), F32), compiler_params=_params("arbitrary"))(parts)


def _adamw_math(g, w, m, v):
    m2 = ADAM_B1 * m + (1.0 - ADAM_B1) * g
    v2 = ADAM_B2 * v + (1.0 - ADAM_B2) * (g * g)
    m_hat = m2 / (1.0 - ADAM_B1 ** ADAM_STEP)
    v_hat = v2 / (1.0 - ADAM_B2 ** ADAM_STEP)
    return -ADAM_LR * (m_hat / (jnp.sqrt(v_hat) + ADAM_EPS) + ADAM_WD * w), m2, v2


def _adamw(name, parts, w, m, v):
    rows, cols = w.shape
    tr = rows if rows * cols <= WHOLE_ELEMS else _pick(rows, (ROW_TILE, 352, 128, 64, 32, 8))
    n = len(parts)

    def kern(*refs):
        g = refs[0][:, pl.ds(0, cols)]
        for p in refs[1:n]:
            g = g + p[:, pl.ds(0, cols)]
        d, m2, v2 = _adamw_math(g, refs[n][...], refs[n + 1][...], refs[n + 2][...])
        refs[n + 3][...] = g
        refs[n + 4][...] = d
        refs[n + 5][...] = m2
        refs[n + 6][...] = v2

    spec = pl.BlockSpec((tr, cols), lambda i: (i, 0))
    return pl.pallas_call(kern, name=name, grid=(rows // tr,),
                          in_specs=[pl.BlockSpec((tr, p.shape[1]), lambda i: (i, 0)) for p in parts] + [spec] * 3,
                          out_specs=[spec] * 4, out_shape=[jax.ShapeDtypeStruct((rows, cols), F32)] * 4,
                          compiler_params=_params("arbitrary"))(*parts, w, m, v)


def _adamw_many(name, gs, ws, ms, vs):
    n = len(gs)

    def kern(*refs):
        for p in range(n):
            d, m2, v2 = _adamw_math(refs[p][...], refs[n + p][...], refs[2 * n + p][...], refs[3 * n + p][...])
            refs[4 * n + p][...] = d
            refs[5 * n + p][...] = m2
            refs[6 * n + p][...] = v2

    res = pl.pallas_call(kern, name=name, out_shape=[jax.ShapeDtypeStruct(w.shape, F32) for w in ws] * 3,
                         compiler_params=pltpu.CompilerParams(vmem_limit_bytes=VMEM_LIMIT_BYTES))(*gs, *ws, *ms, *vs)
    return res[:n], res[n:2 * n], res[2 * n:]


def _pack(arrs):
    parts, rows = [], []
    for a in arrs:
        r = _round_up(-(-a.size // LANE), SUBLANE)
        parts.append(jnp.pad(a.reshape(-1).astype(F32), (0, r * LANE - a.size)).reshape(r, LANE))
        rows.append(r)
    return jnp.concatenate(parts, axis=0), rows


def _unpack(buf, rows, shapes):
    out, r0 = [], 0
    for r, s in zip(rows, shapes):
        size = math.prod(s)
        out.append(buf[r0:r0 + r].reshape(-1)[:size].reshape(s))
        r0 += r
    return out


def kernel(x, norm_ffn1, ffn1_w1, ffn1_w3, ffn1_w2, norm_mix, w_in, conv_w, conv_b, conv_ln_g, conv_ln_b, conv_out_g, ssm_A_re, ssm_A_im, ssm_log_dt, ssm_B_re, ssm_B_im, ssm_C_re, ssm_C_im, ssm_D, ssm_glu_w, ssm_glu_b, ssm_out_g, w_out, norm_ffn2, ffn2_w1, ffn2_w3, ffn2_w2, norm_final, loss_target, m_norm_ffn1, m_ffn1_w1, m_ffn1_w3, m_ffn1_w2, m_norm_mix, m_w_in, m_conv_w, m_conv_b, m_conv_ln_g, m_conv_ln_b, m_conv_out_g, m_ssm_A_re, m_ssm_A_im, m_ssm_log_dt, m_ssm_B_re, m_ssm_B_im, m_ssm_C_re, m_ssm_C_im, m_ssm_D, m_ssm_glu_w, m_ssm_glu_b, m_ssm_out_g, m_w_out, m_norm_ffn2, m_ffn2_w1, m_ffn2_w3, m_ffn2_w2, m_norm_final, v_norm_ffn1, v_ffn1_w1, v_ffn1_w3, v_ffn1_w2, v_norm_mix, v_w_in, v_conv_w, v_conv_b, v_conv_ln_g, v_conv_ln_b, v_conv_out_g, v_ssm_A_re, v_ssm_A_im, v_ssm_log_dt, v_ssm_B_re, v_ssm_B_im, v_ssm_C_re, v_ssm_C_im, v_ssm_D, v_ssm_glu_w, v_ssm_glu_b, v_ssm_out_g, v_w_out, v_norm_ffn2, v_ffn2_w1, v_ffn2_w3, v_ffn2_w2, v_norm_final):
    given = dict(locals())
    wts = {n: given[n] for n in WEIGHTS}
    n_seq, seq, d = x.shape
    n_rows = n_seq * seq
    xf = x.reshape(n_rows, d)
    tgt = loss_target.reshape(n_rows, d)
    row = lambda a: a.reshape(1, -1)

    f = ffn1_w1.shape[-1]
    fp = _round_up(f, LANE)
    held = lambda n, a: a[0].T if n in TRANSPOSED else a[0]
    shards = []
    for n in BIG:
        s = held(n, wts[n]).astype(BF16)
        if n.startswith('ffn'):
            s = jnp.pad(s, ((0, fp - f), (0, 0)))
        shards.append(s)
    n_taps, c_shard = conv_w.shape[1], conv_w.shape[2]
    shards.append(jnp.pad(conv_w[0], ((0, HALO - n_taps), (0, 0))))
    shard_of = dict(zip(BIG + ['conv_w'], shards))
    axis_of = dict(BIG_AXIS, conv_w=1)
    groups = [['ffn1_w1', 'ffn1_w3'], ['ffn1_w2', 'w_in', 'conv_w', 'ssm_glu_w', 'w_out'], ['ffn2_w1', 'ffn2_w3', 'ffn2_w2']]
    fetch, tok = [], []
    for k, names in enumerate(groups):
        fetch.append(_exchange_start("gather%d_send" % k, "gather", [shard_of[n] for n in names],
                                     [axis_of[n] for n in names], tok))
        tok = [fetch[-1][3]]
    full = {}

    def arrive(k, after):
        lands = _exchange_wait("gather%d_recv" % k, "gather", fetch[k], [axis_of[n] for n in groups[k]], after)
        full.update(zip(groups[k], lands))

    h1 = _rms_fwd("ffn1_rms", xf, norm_ffn1)
    arrive(0, tok + [h1])

    _, n_grp, n_state = ssm_A_re.shape
    grp = ssm_B_re.shape[-1]
    ns = n_grp * n_state
    c_ssm = n_grp * grp
    lr, li = ssm_A_re.reshape(1, ns), ssm_A_im.reshape(1, ns)
    ldt = jnp.repeat(ssm_log_dt.reshape(n_grp), n_state).reshape(1, ns)
    btr = ssm_B_re[0].transpose(2, 0, 1).reshape(grp, ns)
    bti = ssm_B_im[0].transpose(2, 0, 1).reshape(grp, ns)
    ctr = ssm_C_re[0].transpose(1, 0, 2).reshape(grp, ns)
    cti = ssm_C_im[0].transpose(1, 0, 2).reshape(grp, ns)
    _, _, bbr, bbi, pw, pw_falling = _s5_params_fwd(lr, li, ldt, btr, bti)
    nb = c_ssm // LANE
    sb, gpb = ns // nb, n_grp // nb
    diag = (jnp.arange(LANE)[:, None] // grp) == (jnp.arange(sb)[None, :] // n_state)

    def spread(t):
        return jnp.where(diag, jnp.tile(t.reshape(grp, nb, sb).transpose(1, 0, 2), (1, gpb, 1)), 0.0)

    def gather_diag(t):
        return (t * diag).reshape(nb, gpb, grp, sb).sum(1).transpose(1, 0, 2).reshape(grp, ns)

    def interleave(re, im):
        return jnp.stack([re.reshape(-1, nb, sb), im.reshape(-1, nb, sb)], axis=2).reshape(-1, 2 * ns)

    bdc = jnp.concatenate([spread(bbr), spread(bbi)], axis=2).astype(BF16)
    cdc = jnp.concatenate([spread(ctr).transpose(0, 2, 1), -spread(cti).transpose(0, 2, 1)], axis=1).astype(BF16)
    rowi = jnp.arange(SUBLANE)[:, None]
    pwf, pwc = interleave(pw[:, :ns], pw[:, ns:]), interleave(pw[:, :ns], -pw[:, ns:])
    tabs_f = [jnp.where(rowi >= s, pwf[s - 1][None, :], 0.0) for s in (1, 2, 4)] + [pwf]
    tabs_b = [jnp.where(rowi <= SUBLANE - 1 - s, pwc[s - 1][None, :], 0.0) for s in (1, 2, 4)]
    tabs_b.append(interleave(pw_falling[:, :ns], -pw_falling[:, ns:]))
    c_conv = conv_b.shape[1]
    u_blk = 2 * c_conv // LANE

    a1, b1, z1 = _ffn_up("ffn1_up", h1, full['ffn1_w1'], full['ffn1_w3'])
    arrive(1, [z1])
    x1, h2 = _mm("ffn1_down", z1, full['ffn1_w2'], 1, 0, addend=xf, alpha=0.5, post=_post_rms(norm_mix))
    saved1 = (h1, a1, b1, z1)
    cw = full['conv_w']
    proj = _mm("mix_in", h2, full['w_in'], 1, 0, F32)
    assert c_conv == c_ssm and proj.shape[1] == 3 * c_conv
    cpre, an = _conv_fwd(proj, cw, conv_b, conv_ln_g, conv_ln_b, conv_out_g, seq)
    xs, ypre, yg = _s5_fwd(proj, u_blk, bdc, cdc, tabs_f, ssm_D, seq, sb)
    q0 = _mm("s5_gate", yg, full['ssm_glu_w'], 1, 0, F32)
    sn = _s5_post2(yg, q0, ssm_glu_b, ssm_out_g)
    wo = full['w_out']
    x2 = _mm("mix_out_a", an, wo[:c_conv], 1, 0, F32, addend=x1)
    x2, h3 = _mm("mix_out_s", sn, wo[c_conv:], 1, 0, addend=x2, post=_post_rms(norm_ffn2))
    arrive(2, [x2])
    a3, b3, z3 = _ffn_up("ffn2_up", h3, full['ffn2_w1'], full['ffn2_w3'])
    saved2 = (h3, a3, b3, z3)
    dx3, loss_row, d_norm_final = _mm("ffn2_down", z3, full['ffn2_w2'], 1, 0, addend=x2, alpha=0.5,
                                      post=_post_loss(row(norm_final), tgt))

    g = {}
    dx2, g['norm_ffn2'], sent = _ffn_bwd("ffn2", x2, norm_ffn2, full['ffn2_w1'], full['ffn2_w3'], full['ffn2_w2'], saved2, dx3)
    dmixed = _mm("mix_dmixed", dx2, wo, 1, 1, F32)
    dwo = jnp.concatenate([_mm("mix_dwo_a", an, dx2, 0, 0, BF16), _mm("mix_dwo_s", sn, dx2, 0, 0, BF16)], axis=0)
    dq, dyg1, g['ssm_out_g'], g['ssm_glu_b'] = _s5_post2_bwd(dmixed, yg, q0, ssm_glu_b, ssm_out_g)
    dyg2 = _mm("s5_dgate", dq, full['ssm_glu_w'], 1, 1, F32)
    dwg = _mm("s5_dwg", yg, dq, 0, 0, BF16)
    sent['w_out ssm_glu_w'] = (_exchange_start("mix_wo_wg_send", "scatter", [dwo, dwg], [0, 0]), [0, 0])
    dypre, du_skip, g['ssm_D'] = _s5_post1_bwd(dyg1, dyg2, ypre, proj, ssm_D, after=[sent['w_out ssm_glu_w'][0][3]])
    du, dabar, dbdc, dcdc = _s5_bwd(dypre, du_skip, xs, proj, u_blk, bdc, cdc, tabs_b, seq, sb)
    dabar = dabar.reshape(nb, 2, sb)
    dlr, dli, dldt, dbtr, dbti = _s5_params_bwd(lr, li, ldt, btr, bti, dabar[:, 0].reshape(1, ns), dabar[:, 1].reshape(1, ns),
                                                gather_diag(dbdc[:, :, :sb]), gather_diag(dbdc[:, :, sb:]))
    g['ssm_A_re'], g['ssm_A_im'] = dlr, dli
    g['ssm_log_dt'] = dldt.reshape(n_grp, n_state).sum(axis=1)
    g['ssm_B_re'] = dbtr.reshape(grp, n_grp, n_state).transpose(1, 2, 0)
    g['ssm_B_im'] = dbti.reshape(grp, n_grp, n_state).transpose(1, 2, 0)
    g['ssm_C_re'] = gather_diag(dcdc[:, :sb].transpose(0, 2, 1)).reshape(grp, n_grp, n_state).transpose(1, 0, 2)
    g['ssm_C_im'] = -gather_diag(dcdc[:, sb:].transpose(0, 2, 1)).reshape(grp, n_grp, n_state).transpose(1, 0, 2)
    dc, g['conv_out_g'], g['conv_ln_g'], g['conv_ln_b'], g['conv_b'] = _conv_bwd_rows(dmixed, cpre, conv_ln_g, conv_ln_b,
                                                                                    conv_out_g)
    dval, dgate, dcw = _conv_bwd_taps(proj, dc, cw, seq)
    dproj = jnp.concatenate([dval, dgate, du], axis=1)
    sent['w_in'] = (_exchange_start("mix_win_send", "scatter", [_mm("mix_dwin", h2, dproj, 0, 0, BF16)], [1]), [1])
    dx1, g['norm_mix'] = _mm("mix_dh", dproj, full['w_in'], 1, 1, after=[sent['w_in'][0][3]],
                             post=_post_rms_bwd(x1, norm_mix, dx2))
    dx0, g['norm_ffn1'], sent1 = _ffn_bwd("ffn1", xf, norm_ffn1, full['ffn1_w1'], full['ffn1_w3'], full['ffn1_w2'], saved1, dx1)
    sent.update(sent1)
    g['norm_final'] = d_norm_final
    g['conv_w'] = dcw[:n_taps]

    small_shapes = [(n_taps, c_conv) if n == 'conv_w' else wts[n].shape for n in SMALL]
    buf, buf_rows = _pack([g[n] for n in SMALL] + [loss_row])
    to_all = _exchange_start("small_send", "all", [buf], [0])
    slots = {}
    for names, (started, axes) in sent.items():
        lands = _exchange_wait(names.replace(' ', '_') + "_recv", "scatter", started, axes, after=[dx0, to_all[3]])
        slots.update(zip(names.split(), lands))
    sums = [_sum_slots("sum_" + n, slots[n]) for n in BIG]
    to_sibling = _exchange_start("sums_send", "sibling", sums, [0] * len(sums))
    from_all = _exchange_wait("small_recv", "all", to_all, [0], after=[to_sibling[3]])[0]
    total = _unpack(_sum_devices(from_all), buf_rows, small_shapes + [(1, LANE)])
    loss = total[-1][0, 0]
    grads = dict(zip(SMALL, total[:-1]))
    chip = 2 * lax.axis_index("x") + lax.axis_index("y")
    grads['conv_w'] = lax.dynamic_slice_in_dim(grads['conv_w'], chip * c_shard, c_shard, axis=1)[None]
    flat = lambda a: a.reshape(-1, a.shape[-1])
    small = _adamw_many("adamw_small", *[[flat(src[p + n]) for n in SMALL]
                                         for src, p in ((grads, ''), (given, ''), (given, 'm_'), (given, 'v_'))])
    deltas, new_m, new_v = ({n: o.reshape(wts[n].shape) for n, o in zip(SMALL, outs)} for outs in small)

    theirs = _exchange_wait("sums_recv", "sibling", to_sibling, [0] * len(sums), after=[new_v[SMALL[-1]]])
    for n, mine, other in zip(BIG, sums, theirs):
        grads[n], deltas[n], new_m[n], new_v[n] = (
            (o.T if n in TRANSPOSED else o)[None]
            for o in _adamw("adamw_" + n, [mine, other], held(n, given[n]), held(n, given['m_' + n]), held(n, given['v_' + n])))

    return (loss, dx0.reshape(x.shape), *[grads[n] for n in WEIGHTS], *[deltas[n] for n in WEIGHTS],
            *[new_m[n] for n in WEIGHTS], *[new_v[n] for n in WEIGHTS])
```

```python
import math
from typing import Callable, NamedTuple

import jax
import jax.numpy as jnp
from jax import lax
from jax.experimental import pallas as pl
from jax.experimental.pallas import tpu as pltpu

F32 = jnp.float32
BF16 = jnp.bfloat16
EPS = 1e-6
ADAM_LR, ADAM_B1, ADAM_B2, ADAM_EPS, ADAM_WD, ADAM_STEP = 0.001, 0.9, 0.999, 1e-08, 0.01, 10
MESH = pl.DeviceIdType.MESH
ANY = pl.BlockSpec(memory_space=pl.ANY)
LANE = 128
SUBLANE = 8
VMEM_LIMIT_BYTES = 56 << 20
ROW_TILE = 256
ROW_TILE_ELEMS = 256 * 1024
WHOLE_ELEMS = 512 * 1024
WHOLE_WEIGHT_BYTES = 8 << 20
FFN_ROWS = 256
CONV_TILE = 128
CONV_SUB = 32
HALO = 32
SCAN_TILE = 512
SCAN_COLS = 512
N_CHIPS = 4
CHIP_RELS = ((1, 0), (0, 1), (1, 1))
NT = (((1,), (1,)), ((), ()))
GELU_K = math.sqrt(2.0 / math.pi)
GELU_C = 0.044715

WEIGHTS = ['norm_ffn1', 'ffn1_w1', 'ffn1_w3', 'ffn1_w2', 'norm_mix', 'w_in', 'conv_w', 'conv_b', 'conv_ln_g', 'conv_ln_b',
           'conv_out_g', 'ssm_A_re', 'ssm_A_im', 'ssm_log_dt', 'ssm_B_re', 'ssm_B_im', 'ssm_C_re', 'ssm_C_im', 'ssm_D',
           'ssm_glu_w', 'ssm_glu_b', 'ssm_out_g', 'w_out', 'norm_ffn2', 'ffn2_w1', 'ffn2_w3', 'ffn2_w2', 'norm_final']
BIG = ['ffn1_w1', 'ffn1_w3', 'ffn1_w2', 'w_in', 'ssm_glu_w', 'w_out', 'ffn2_w1', 'ffn2_w3', 'ffn2_w2']
BIG_AXIS = {'ffn1_w1': 0, 'ffn1_w3': 0, 'ffn1_w2': 0, 'w_in': 1, 'ssm_glu_w': 0, 'w_out': 0, 'ffn2_w1': 0, 'ffn2_w3': 0,
            'ffn2_w2': 0}
TRANSPOSED = ('ffn1_w1', 'ffn1_w3', 'ffn2_w1', 'ffn2_w3')
SMALL = [n for n in WEIGHTS if n not in BIG]


def _round_up(n, m):
    return -(-n // m) * m


def _pick(n, cands):
    for c in cands:
        if c <= n and n % c == 0:
            return c
    return n


def _params(*sem):
    return pltpu.CompilerParams(dimension_semantics=sem, vmem_limit_bytes=VMEM_LIMIT_BYTES)


def _rms_r(x):
    return lax.rsqrt(jnp.mean(x * x, axis=-1, keepdims=True) + EPS)


def _rms_bwd(x, r, g, dy):
    dyg = dy * g
    return r * dyg - x * (r * r * r) * jnp.mean(x * dyg, axis=-1, keepdims=True)


def _sigmoid(x):
    return jax.nn.sigmoid(x)


def _dsilu(a, s):
    return s * (1.0 + a * (1.0 - s))


def _gelu(x):
    return 0.5 * x * (1.0 + jnp.tanh(GELU_K * (x + GELU_C * x * x * x)))


def _dgelu(x):
    t = jnp.tanh(GELU_K * (x + GELU_C * x * x * x))
    return 0.5 * (1.0 + t) + 0.5 * x * (1.0 - t * t) * GELU_K * (1.0 + 3.0 * GELU_C * x * x)


def _colsum(v):
    return jnp.sum(v, axis=0, keepdims=True)


def _rowwise(name, body, n_rows, row_ins, par_ins, row_outs, acc_outs, after=()):
    widest = max([w for (_, w, _) in row_ins] + [w for (w, _) in row_outs])
    tt = _pick(n_rows, [t for t in (256, 128, 64, 32, 16, 8) if t * widest <= ROW_TILE_ELEMS])
    in_specs = [pl.BlockSpec((tt, w), lambda i, cb=cb: (i, cb)) for (_, w, cb) in row_ins]
    in_specs += [pl.BlockSpec(p.shape, lambda i: (0, 0)) for p in par_ins] + [ANY] * len(after)
    out_specs = [pl.BlockSpec((tt, w), lambda i: (i, 0)) for (w, _) in row_outs]
    out_specs += [pl.BlockSpec((r, w), lambda i: (0, 0)) for (r, w) in acc_outs]
    out_shape = [jax.ShapeDtypeStruct((n_rows, w), dt) for (w, dt) in row_outs]
    out_shape += [jax.ShapeDtypeStruct((r, w), F32) for (r, w) in acc_outs]
    n_in, n_ro = len(row_ins) + len(par_ins), len(row_outs)
    o0 = n_in + len(after)

    def kern(*refs):
        accs = refs[o0 + n_ro:]
        if accs:
            @pl.when(pl.program_id(0) == 0)
            def _():
                for a in accs:
                    a[...] = jnp.zeros_like(a)
        body(refs[:n_in], refs[o0:o0 + n_ro], accs)

    return pl.pallas_call(kern, name=name, grid=(n_rows // tt,), in_specs=in_specs, out_specs=out_specs, out_shape=out_shape,
                          compiler_params=_params("arbitrary"))(*[a for a, _, _ in row_ins], *par_ins, *after)


class Post(NamedTuple):
    rows: list
    gains: list
    outs: list
    t_outs: list
    sums: list
    fn: Callable


def _post_rms(gain):
    def fn(r, rows, gains):
        h = r * _rms_r(r) * gains[0]
        return [r, h, h], []

    return Post([], [gain], [F32, BF16], [BF16], [], fn)


def _post_rms_bwd(x, gain, dres):
    def fn(dh, rows, gains):
        r = _rms_r(rows[0])
        dx = rows[1] + _rms_bwd(rows[0], r, gains[0], dh)
        return [dx, dx], [_colsum(dh * rows[0] * r)]

    return Post([x, dres], [gain], [F32], [BF16], [x.shape[1]], fn)


def _post_loss(gain, tgt):
    d = tgt.shape[1]

    def fn(xv, rows, gains):
        r = _rms_r(xv)
        e = xv * r * gains[0] - rows[0]
        sq = jnp.sum(jnp.sum(e * e, axis=-1, keepdims=True), axis=0, keepdims=True)
        dy = e * (1.0 / d)
        dx = _rms_bwd(xv, r, gains[0], dy)
        return [dx, dx], [jnp.broadcast_to(sq * (0.5 / d), (1, LANE)), _colsum(dy * xv * r)]

    return Post([tgt], [gain], [F32], [BF16], [LANE, d], fn)


def _mm(name, a, b, ca, cb, out_dtype=F32, addend=None, alpha=1.0, a_cols=None, after=(), post=None, transposed=False):
    a_start, a_width = a_cols if a_cols else (0, a.shape[1])
    m, k = (a.shape[0], a_width) if ca == 1 else (a_width, a.shape[0])
    n = b.shape[1 - cb]
    assert b.shape[cb] == k, (name, a.shape, b.shape)
    tn = _pick(n, (1024, 768, 512, 384, 256, 128))
    if post and k * tn * b.dtype.itemsize <= WHOLE_WEIGHT_BYTES:
        tk = k
        tm = _pick(m, (256, 128) if k > 1024 else (512, 256, 128))
    else:
        tm = _pick(m, (512, 256, 128) if post else (1024, 512, 256, 128))
        tk = _pick(k, (1024, 768, 512, 256, 128))
    nk = k // tk
    if ca == 1:
        assert a_start % tk == 0
        a_spec = pl.BlockSpec((tm, tk), lambda i, j, kk: (i, kk + a_start // tk))
    else:
        assert a_start % tm == 0
        a_spec = pl.BlockSpec((tk, tm), lambda i, j, kk: (kk, i + a_start // tm))
    b_spec = pl.BlockSpec((tk, tn), lambda i, j, kk: (kk, j)) if cb == 0 else pl.BlockSpec((tn, tk), lambda i, j, kk: (j, kk))
    o_spec = pl.BlockSpec((tm, tn), lambda i, j, kk: (i, j))
    t_spec = pl.BlockSpec((tn, tm), lambda i, j, kk: (j, i))
    fixed = lambda w: pl.BlockSpec((1, w), lambda i, j, kk: (0, 0))
    ins, in_specs = [a, b], [a_spec, b_spec]
    if addend is not None:
        ins.append(addend)
        in_specs.append(o_spec)
    n_plain = len(ins)
    n_rows, n_gains = (len(post.rows), len(post.gains)) if post else (0, 0)
    if post:
        assert tn == n, name
        ins += post.rows + post.gains
        in_specs += [o_spec] * n_rows + [fixed(n)] * n_gains
    ins += list(after)
    in_specs += [ANY] * len(after)
    n_in = len(ins)
    if post:
        n_straight, n_vals = len(post.outs), len(post.outs) + len(post.t_outs)
        out_specs = [o_spec] * n_straight + [t_spec] * len(post.t_outs) + [fixed(w) for w in post.sums]
        out_shape = [jax.ShapeDtypeStruct((m, n), dt) for dt in post.outs] + [jax.ShapeDtypeStruct((n, m), dt) for dt in post.t_outs]
        out_shape += [jax.ShapeDtypeStruct((1, w), F32) for w in post.sums]
    elif transposed:
        out_specs, out_shape = [t_spec], [jax.ShapeDtypeStruct((n, m), out_dtype)]
    else:
        out_specs, out_shape = [o_spec], [jax.ShapeDtypeStruct((m, n), out_dtype)]
    n_out = len(out_specs)
    dims = (((ca,), (cb,)), ((), ()))

    def emit(refs, r):
        if alpha != 1.0:
            r = r * alpha
        if addend is not None:
            r = r + refs[2][...].astype(F32)
        outs = refs[n_in:n_in + n_out]
        if post is None:
            outs[0][...] = (r.T if transposed else r).astype(out_dtype)
            return
        vals, incs = post.fn(r, [q[...] for q in refs[n_plain:n_plain + n_rows]],
                             [q[...] for q in refs[n_plain + n_rows:n_plain + n_rows + n_gains]])
        for at, (o_ref, val) in enumerate(zip(outs, vals)):
            o_ref[...] = (val if at < n_straight else val.T).astype(o_ref.dtype)
        for s_ref, inc in zip(outs[n_vals:], incs):
            s_ref[...] += inc

    def kern(*refs):
        kk = pl.program_id(2)
        if post and post.sums:
            @pl.when(jnp.logical_and(jnp.logical_and(pl.program_id(0) == 0, pl.program_id(1) == 0), kk == 0))
            def _():
                for s_ref in refs[n_in + n_vals:n_in + n_out]:
                    s_ref[...] = jnp.zeros_like(s_ref)

        part = lax.dot_general(refs[0][...].astype(BF16), refs[1][...].astype(BF16), dims, preferred_element_type=F32)
        if nk == 1:
            emit(refs, part)
            return
        acc_ref = refs[-1]

        @pl.when(kk == 0)
        def _():
            acc_ref[...] = part

        @pl.when(kk > 0)
        def _():
            acc_ref[...] += part

        @pl.when(kk == nk - 1)
        def _():
            emit(refs, acc_ref[...])

    res = pl.pallas_call(kern, name=name, grid=(m // tm, n // tn, nk), in_specs=in_specs, out_specs=out_specs,
                         out_shape=out_shape, scratch_shapes=[] if nk == 1 else [pltpu.VMEM((tm, tn), F32)],
                         compiler_params=_params("arbitrary", "arbitrary", "arbitrary"))(*ins)
    return res if post else res[0]


def _rms_fwd(name, x, g):
    t, d = x.shape
    tt = _pick(t, (ROW_TILE, LANE))

    def kern(x_ref, g_ref, h_ref, ht_ref):
        xv = x_ref[...]
        h = xv * _rms_r(xv) * g_ref[...]
        h_ref[...] = h.astype(BF16)
        ht_ref[...] = h.T.astype(BF16)

    return pl.pallas_call(kern, name=name, grid=(t // tt,),
                          in_specs=[pl.BlockSpec((tt, d), lambda i: (i, 0)), pl.BlockSpec((1, d), lambda i: (0, 0))],
                          out_specs=[pl.BlockSpec((tt, d), lambda i: (i, 0)), pl.BlockSpec((d, tt), lambda i: (0, i))],
                          out_shape=[jax.ShapeDtypeStruct((t, d), BF16), jax.ShapeDtypeStruct((d, t), BF16)],
                          compiler_params=_params("arbitrary"))(x, g)


def _ffn_up(name, h, w1, w3):
    t, d = h.shape
    ff = w1.shape[0]
    tm, tn = _pick(t, (1024, 512, 256, 128)), _pick(ff, (1024, 768, 512, 256, 128))

    def kern(h_ref, w1_ref, w3_ref, a_ref, b_ref, z_ref):
        hv = h_ref[...]
        a = lax.dot_general(hv, w1_ref[...], NT, preferred_element_type=F32)
        b = lax.dot_general(hv, w3_ref[...], NT, preferred_element_type=F32)
        a_ref[...] = a.astype(BF16)
        b_ref[...] = b.astype(BF16)
        z_ref[...] = (a * _sigmoid(a) * b).astype(BF16)

    w_spec = pl.BlockSpec((tn, d), lambda i, j: (j, 0))
    o_spec = pl.BlockSpec((tm, tn), lambda i, j: (i, j))
    return pl.pallas_call(kern, name=name, grid=(t // tm, ff // tn),
                          in_specs=[pl.BlockSpec((tm, d), lambda i, j: (i, 0)), w_spec, w_spec], out_specs=[o_spec] * 3,
                          out_shape=[jax.ShapeDtypeStruct((t, ff), BF16)] * 3,
                          compiler_params=_params("arbitrary", "arbitrary"))(h, w1, w3)


def _ffn_dglu(name, dxo, w2, a, b, after=()):
    t, d = dxo.shape
    ff = w2.shape[0]
    tm = _pick(t, (FFN_ROWS, 128))

    def kern(dx_ref, w2_ref, a_ref, b_ref, *rest):
        da_ref, db_ref = rest[-2:]
        dz = lax.dot_general(dx_ref[...].astype(BF16), w2_ref[...], NT, preferred_element_type=F32) * 0.5
        av, bv = a_ref[...].astype(F32), b_ref[...].astype(F32)
        s = _sigmoid(av)
        da_ref[...] = (dz * bv * _dsilu(av, s)).astype(BF16)
        db_ref[...] = (dz * av * s).astype(BF16)

    o_spec = pl.BlockSpec((tm, ff), lambda i: (i, 0))
    return pl.pallas_call(kern, name=name, grid=(t // tm,),
                          in_specs=[pl.BlockSpec((tm, d), lambda i: (i, 0)), pl.BlockSpec((ff, d), lambda i: (0, 0)),
                                    o_spec, o_spec] + [ANY] * len(after),
                          out_specs=[o_spec] * 2, out_shape=[jax.ShapeDtypeStruct((t, ff), BF16)] * 2,
                          compiler_params=_params("arbitrary"))(dxo, w2, a, b, *after)


def _ffn_dh(name, da, db, w1, w3, x, g, dres, after=()):
    t, d = x.shape
    ff = da.shape[1]
    tm = _pick(t, (FFN_ROWS, 128))

    def kern(da_ref, db_ref, w1_ref, w3_ref, x_ref, g_ref, dres_ref, *rest):
        dx_ref, dg_ref = rest[-2:]

        @pl.when(pl.program_id(0) == 0)
        def _():
            dg_ref[...] = jnp.zeros_like(dg_ref)

        dh = (jnp.dot(da_ref[...], w1_ref[...], preferred_element_type=F32)
              + jnp.dot(db_ref[...], w3_ref[...], preferred_element_type=F32))
        xv = x_ref[...]
        r = _rms_r(xv)
        dx_ref[...] = dres_ref[...] + _rms_bwd(xv, r, g_ref[...], dh)
        dg_ref[...] += _colsum(dh * xv * r)

    act = pl.BlockSpec((tm, ff), lambda i: (i, 0))
    wgt = pl.BlockSpec((ff, d), lambda i: (0, 0))
    rows = pl.BlockSpec((tm, d), lambda i: (i, 0))
    gain = pl.BlockSpec((1, d), lambda i: (0, 0))
    return pl.pallas_call(kern, name=name, grid=(t // tm,),
                          in_specs=[act, act, wgt, wgt, rows, gain, rows] + [ANY] * len(after), out_specs=[rows, gain],
                          out_shape=[jax.ShapeDtypeStruct((t, d), F32), jax.ShapeDtypeStruct((1, d), F32)],
                          compiler_params=_params("arbitrary"))(da, db, w1, w3, x, g, dres, *after)


def _ffn_bwd(tag, x, g, w1, w3, w2, saved, dxo, dxo_t):
    ht, a, b, z = saved
    dw2 = _mm(tag + "_dw2", dxo_t, z, 1, 0, BF16, alpha=0.5, transposed=True)
    s2 = _exchange_start(tag + "_w2_send", "scatter", [dw2], [0])
    da, db = _ffn_dglu(tag + "_dglu", dxo, w2, a, b, after=[s2[3]])
    dw1 = _mm(tag + "_dw1", ht, da, 1, 0, BF16, transposed=True)
    s1 = _exchange_start(tag + "_w1_send", "scatter", [dw1], [0])
    dw3 = _mm(tag + "_dw3", ht, db, 1, 0, BF16, after=[s1[3]], transposed=True)
    s3 = _exchange_start(tag + "_w3_send", "scatter", [dw3], [0])
    dx, dg = _ffn_dh(tag + "_dh", da, db, w1, w3, x, g, dxo, after=[s3[3]])
    return dx, dg, {tag + "_w1": (s1, [0]), tag + "_w3": (s3, [0]), tag + "_w2": (s2, [0])}


def _shift_copies(ext_ref, sh_ref):
    n = ext_ref.shape[0] - SUBLANE
    for r in range(1, SUBLANE):
        sh_ref[r, pl.ds(0, n), :] = ext_ref[pl.ds(r, n), :]


def _rows_at(ext_ref, sh_ref, off, rows):
    r = off % SUBLANE
    return ext_ref[pl.ds(off, rows), :] if r == 0 else sh_ref[r, pl.ds(off - r, rows), :]


def _conv_fwd(proj, cw, cb, lng, lnb, og, seq):
    n_rows, c = proj.shape[0], cb.shape[1]
    kw = HALO - 1
    tt = _pick(seq, (CONV_TILE,))
    hb = tt // HALO

    def kern(v_ref, g_ref, vp_ref, gp_ref, w_ref, cb_ref, lg_ref, lb_ref, og_ref, c_ref, an_ref, ext_ref, sh_ref):
        first = (pl.program_id(0) * tt) % seq == 0
        ext_ref[pl.ds(HALO, tt), :] = v_ref[...] * _sigmoid(g_ref[...])
        ext_ref[pl.ds(0, HALO), :] = vp_ref[...] * _sigmoid(gp_ref[...]) * jnp.where(first, 0.0, 1.0)
        _shift_copies(ext_ref, sh_ref)
        for r0 in range(0, tt, CONV_SUB):
            rows = min(CONV_SUB, tt - r0)
            acc = jnp.zeros((rows, c), F32)
            for k in range(kw):
                acc = acc + w_ref[pl.ds(k, 1), :] * _rows_at(ext_ref, sh_ref, r0 + HALO - (kw - 1) + k, rows)
            c_ref[pl.ds(r0, rows), :] = acc + cb_ref[...]
        cv = c_ref[...]
        mu = jnp.mean(cv, axis=-1, keepdims=True)
        xc = cv - mu
        rstd = lax.rsqrt(jnp.mean(xc * xc, axis=-1, keepdims=True) + EPS)
        lv = xc * rstd * lg_ref[...] + lb_ref[...]
        sl = lv * _sigmoid(lv)
        an_ref[...] = (sl * _rms_r(sl) * og_ref[...]).astype(BF16)

    cur = lambda cbk: pl.BlockSpec((tt, c), lambda i: (i, cbk))
    prev = lambda cbk: pl.BlockSpec((HALO, c), lambda i: (jnp.maximum(i * hb - 1, 0), cbk))
    par = lambda p: pl.BlockSpec(p.shape, lambda i: (0, 0))
    return pl.pallas_call(
        kern, name="conv_fwd", grid=(n_rows // tt,),
        in_specs=[cur(0), cur(1), prev(0), prev(1), par(cw), par(cb), par(lng), par(lnb), par(og)],
        out_specs=[pl.BlockSpec((tt, c), lambda i: (i, 0))] * 2,
        out_shape=[jax.ShapeDtypeStruct((n_rows, c), F32), jax.ShapeDtypeStruct((n_rows, c), BF16)],
        scratch_shapes=[pltpu.VMEM((tt + HALO, c), F32), pltpu.VMEM((SUBLANE, tt + HALO, c), F32)],
        compiler_params=_params("arbitrary"),
    )(proj, proj, proj, proj, cw, cb, lng, lnb, og)


def _conv_bwd_rows(dmixed, cpre, lng, lnb, og):
    c = cpre.shape[1]

    def body(ins, outs, accs):
        dan, cv, lg, lb, ogv = ins[0][...], ins[1][...], ins[2][...], ins[3][...], ins[4][...]
        mu = jnp.mean(cv, axis=-1, keepdims=True)
        xc = cv - mu
        rstd = lax.rsqrt(jnp.mean(xc * xc, axis=-1, keepdims=True) + EPS)
        xh = xc * rstd
        lv = xh * lg + lb
        s = _sigmoid(lv)
        sl = lv * s
        r2 = _rms_r(sl)
        accs[0][...] += _colsum(dan * sl * r2)
        dl = _rms_bwd(sl, r2, ogv, dan) * _dsilu(lv, s)
        accs[1][...] += _colsum(dl * xh)
        accs[2][...] += _colsum(dl)
        dxh = dl * lg
        dc = rstd * (dxh - jnp.mean(dxh, axis=-1, keepdims=True) - xh * jnp.mean(dxh * xh, axis=-1, keepdims=True))
        outs[0][...] = dc
        accs[3][...] += _colsum(dc)

    return _rowwise("conv_bwd_rows", body, cpre.shape[0], [(dmixed, c, 0), (cpre, c, 0)], [lng, lnb, og], [(c, F32)],
                    [(1, c)] * 4)


def _conv_bwd_taps(proj, dc, cw, seq):
    n_rows, c = dc.shape
    kw = HALO - 1
    tt = _pick(seq, (CONV_TILE,))
    hb = tt // HALO
    last_blk = n_rows // HALO - 1

    def kern(v_ref, g_ref, vp_ref, gp_ref, dc_ref, dn_ref, w_ref, dv_ref, dg_ref, dw_ref, exta_ref, extd_ref, sha_ref, shd_ref):
        i = pl.program_id(0)
        first = (i * tt) % seq == 0
        last = ((i + 1) * tt) % seq == 0

        @pl.when(i == 0)
        def _():
            dw_ref[...] = jnp.zeros_like(dw_ref)

        sg = _sigmoid(g_ref[...])
        exta_ref[pl.ds(HALO, tt), :] = v_ref[...] * sg
        exta_ref[pl.ds(0, HALO), :] = vp_ref[...] * _sigmoid(gp_ref[...]) * jnp.where(first, 0.0, 1.0)
        dcv = dc_ref[...]
        extd_ref[pl.ds(0, tt), :] = dcv
        extd_ref[pl.ds(tt, HALO), :] = dn_ref[...] * jnp.where(last, 0.0, 1.0)
        _shift_copies(exta_ref, sha_ref)
        _shift_copies(extd_ref, shd_ref)
        for k in range(kw):
            dw_ref[pl.ds(k, 1), :] += _colsum(_rows_at(exta_ref, sha_ref, HALO - (kw - 1) + k, tt) * dcv)
        for r0 in range(0, tt, CONV_SUB):
            rows = min(CONV_SUB, tt - r0)
            acc = jnp.zeros((rows, c), F32)
            for k in range(kw):
                acc = acc + w_ref[pl.ds(k, 1), :] * _rows_at(extd_ref, shd_ref, r0 + (kw - 1) - k, rows)
            dv_ref[pl.ds(r0, rows), :] = acc
        da = dv_ref[...]
        dv_ref[...] = da * sg
        dg_ref[...] = da * v_ref[...] * sg * (1.0 - sg)

    cur = lambda cbk: pl.BlockSpec((tt, c), lambda i: (i, cbk))
    prev = lambda cbk: pl.BlockSpec((HALO, c), lambda i: (jnp.maximum(i * hb - 1, 0), cbk))
    nxt = pl.BlockSpec((HALO, c), lambda i: (jnp.minimum((i + 1) * hb, last_blk), 0))
    return pl.pallas_call(
        kern, name="conv_bwd_taps", grid=(n_rows // tt,),
        in_specs=[cur(0), cur(1), prev(0), prev(1), cur(0), nxt, pl.BlockSpec(cw.shape, lambda i: (0, 0))],
        out_specs=[cur(0), cur(0), pl.BlockSpec((HALO, c), lambda i: (0, 0))],
        out_shape=[jax.ShapeDtypeStruct((n_rows, c), F32), jax.ShapeDtypeStruct((n_rows, c), F32),
                   jax.ShapeDtypeStruct((HALO, c), F32)],
        scratch_shapes=[pltpu.VMEM((tt + HALO, c), F32)] * 2 + [pltpu.VMEM((SUBLANE, tt + HALO, c), F32)] * 2,
        compiler_params=_params("arbitrary"),
    )(proj, proj, proj, proj, dc, dc, cw)


def _s5_params_fwd(lr, li, ldt, btr, bti):
    ns = lr.shape[1]

    def kern(lr_ref, li_ref, ldt_ref, btr_ref, bti_ref, ar_ref, ai_ref, bbr_ref, bbi_ref, pw_ref, pwr_ref):
        lrv, liv = lr_ref[...], li_ref[...]
        dt = jnp.exp(ldt_ref[...])
        zr, zi = lrv * dt, liv * dt
        mag = jnp.exp(zr)
        ar, ai = mag * jnp.cos(zi), mag * jnp.sin(zi)
        den = lrv * lrv + liv * liv
        nr = ar - 1.0
        cr = (nr * lrv + ai * liv) / den
        ci = (ai * lrv - nr * liv) / den
        ar_ref[...] = ar
        ai_ref[...] = ai
        bbr_ref[...] = cr * btr_ref[...] - ci * bti_ref[...]
        bbi_ref[...] = cr * bti_ref[...] + ci * btr_ref[...]
        pr, pi = ar, ai
        for e in range(SUBLANE):
            for ref, at in ((pw_ref, e), (pwr_ref, SUBLANE - 1 - e)):
                ref[pl.ds(at, 1), pl.ds(0, ns)] = pr
                ref[pl.ds(at, 1), pl.ds(ns, ns)] = pi
            pr, pi = pr * ar - pi * ai, pr * ai + pi * ar

    h = btr.shape[0]
    shapes = [jax.ShapeDtypeStruct((1, ns), F32)] * 2 + [jax.ShapeDtypeStruct((h, ns), F32)] * 2
    shapes += [jax.ShapeDtypeStruct((SUBLANE, 2 * ns), F32)] * 2
    return pl.pallas_call(kern, name="s5_params_fwd", out_shape=shapes)(lr, li, ldt, btr, bti)


def _s5_params_bwd(lr, li, ldt, btr, bti, dar, dai, dbbr, dbbi):
    def kern(lr_ref, li_ref, ldt_ref, btr_ref, bti_ref, dar_ref, dai_ref, dbr_ref, dbi_ref,
             dlr_ref, dli_ref, dldt_ref, dbtr_ref, dbti_ref):
        lrv, liv = lr_ref[...], li_ref[...]
        dt = jnp.exp(ldt_ref[...])
        zr, zi = lrv * dt, liv * dt
        mag = jnp.exp(zr)
        ar, ai = mag * jnp.cos(zi), mag * jnp.sin(zi)
        den = lrv * lrv + liv * liv
        nr = ar - 1.0
        cr = (nr * lrv + ai * liv) / den
        ci = (ai * lrv - nr * liv) / den
        dbr, dbi, br, bi = dbr_ref[...], dbi_ref[...], btr_ref[...], bti_ref[...]
        dbtr_ref[...] = cr * dbr + ci * dbi
        dbti_ref[...] = cr * dbi - ci * dbr
        dcr = _colsum(br * dbr + bi * dbi)
        dci = _colsum(br * dbi - bi * dbr)
        ir, ii = lrv / den, -liv / den
        dnr = ir * dcr + ii * dci
        dni = ir * dci - ii * dcr
        wr, wi = cr * ir - ci * ii, cr * ii + ci * ir
        dl1r = -(wr * dcr + wi * dci)
        dl1i = -(wr * dci - wi * dcr)
        dtr, dti = dar_ref[...] + dnr, dai_ref[...] + dni
        dzr = ar * dtr + ai * dti
        dzi = ar * dti - ai * dtr
        dlr_ref[...] = dl1r + dt * dzr
        dli_ref[...] = dl1i + dt * dzi
        dldt_ref[...] = (dzr * lrv + dzi * liv) * dt

    ns, h = lr.shape[1], btr.shape[0]
    shapes = [jax.ShapeDtypeStruct((1, ns), F32)] * 3 + [jax.ShapeDtypeStruct((h, ns), F32)] * 2
    return pl.pallas_call(kern, name="s5_params_bwd", out_shape=shapes)(lr, li, ldt, btr, bti, dar, dai, dbbr, dbbi)


def _scan_tile(s_ref, o_ref, tabs, car_ref, sb, reverse, x_ref=None, acc_ref=None):
    l1, l2, l4, pw = tabs
    rows_t, w = s_ref.shape
    ng = rows_t // SUBLANE
    cw = _pick(sb, (SCAN_COLS,))
    carry_row = 0 if reverse else SUBLANE - 1
    row = lax.broadcasted_iota(jnp.int32, (SUBLANE, cw), 0)

    def group(gi, carry):
        g = (ng - 1 - gi) if reverse else gi
        rows = pl.ds(pl.multiple_of(g * SUBLANE, SUBLANE), SUBLANE)
        for c0 in [b0 + o for b0 in range(0, w, 2 * sb) for o in range(0, sb, cw)]:
            cr, ci = pl.ds(c0, cw), pl.ds(c0 + sb, cw)
            xr, xi = s_ref[rows, cr], s_ref[rows, ci]
            for s, lt in ((1, l1), (2, l2), (4, l4)):
                sh = (SUBLANE - s) if reverse else s
                sr, si = pltpu.roll(xr, sh, 0), pltpu.roll(xi, sh, 0)
                ar, ai = lt[:, cr], lt[:, ci]
                xr, xi = xr + ar * sr - ai * si, xi + ar * si + ai * sr
            kr, ki = car_ref[pl.ds(carry_row, 1), cr], car_ref[pl.ds(carry_row, 1), ci]
            pr, pi = pw[:, cr], pw[:, ci]
            xr, xi = xr + pr * kr - pi * ki, xi + pr * ki + pi * kr
            o_ref[rows, cr] = xr
            o_ref[rows, ci] = xi
            car_ref[:, cr] = xr
            car_ref[:, ci] = xi
            if acc_ref is not None:
                nr = jnp.where(row == SUBLANE - 1, kr, pltpu.roll(xr, SUBLANE - 1, 0))
                ni = jnp.where(row == SUBLANE - 1, ki, pltpu.roll(xi, SUBLANE - 1, 0))
                pxr, pxi = x_ref[rows, cr], x_ref[rows, ci]
                acc_ref[:, cr] += nr * pxr + ni * pxi
                acc_ref[:, ci] += ni * pxr - nr * pxi
        return carry

    lax.fori_loop(0, ng, group, 0)


def _s5_fwd(proj, u_blk, bdc, cdc, tabs, dskip, seq, sb):
    n_rows = proj.shape[0]
    nb, blk, w_blk = bdc.shape
    c, w = nb * blk, nb * w_blk
    tt = _pick(seq, (SCAN_TILE,))

    def kern(u_ref, bd_ref, cd_ref, l1, l2, l4, pw, d_ref, xs_ref, yp_ref, yg_ref, bu_ref, car_ref):
        @pl.when((pl.program_id(0) * tt) % seq == 0)
        def _():
            car_ref[...] = jnp.zeros_like(car_ref)

        for j in range(nb):
            bu_ref[:, pl.ds(j * w_blk, w_blk)] = jnp.dot(u_ref[:, pl.ds(j * blk, blk)].astype(BF16), bd_ref[j],
                                                         preferred_element_type=F32)
        _scan_tile(bu_ref, xs_ref, (l1, l2, l4, pw), car_ref, sb, False)
        for j in range(nb):
            cols = pl.ds(j * blk, blk)
            y0 = jnp.dot(xs_ref[:, pl.ds(j * w_blk, w_blk)].astype(BF16), cd_ref[j], preferred_element_type=F32)
            ypre = y0 + d_ref[:, cols] * u_ref[:, cols]
            yp_ref[:, cols] = ypre
            yg_ref[:, cols] = _gelu(ypre).astype(BF16)

    tab = pl.BlockSpec((SUBLANE, w), lambda i: (0, 0))
    rows = pl.BlockSpec((tt, c), lambda i: (i, 0))
    return pl.pallas_call(
        kern, name="s5_fwd", grid=(n_rows // tt,),
        in_specs=[pl.BlockSpec((tt, c), lambda i: (i, u_blk * blk // c)), pl.BlockSpec(bdc.shape, lambda i: (0, 0, 0)),
                  pl.BlockSpec(cdc.shape, lambda i: (0, 0, 0)), tab, tab, tab, tab, pl.BlockSpec((1, c), lambda i: (0, 0))],
        out_specs=[pl.BlockSpec((tt, w), lambda i: (i, 0)), rows, rows],
        out_shape=[jax.ShapeDtypeStruct((n_rows, w), F32), jax.ShapeDtypeStruct((n_rows, c), F32),
                   jax.ShapeDtypeStruct((n_rows, c), BF16)],
        scratch_shapes=[pltpu.VMEM((tt, w), F32), pltpu.VMEM((SUBLANE, w), F32)],
        compiler_params=_params("arbitrary"))(proj, bdc, cdc, *tabs, dskip)


def _s5_bwd(dypre, du_skip, xs, proj, u_blk, bdc, cdc, tabs, seq, sb):
    n_rows = proj.shape[0]
    nb, blk, w_blk = bdc.shape
    c, w = nb * blk, nb * w_blk
    tt = _pick(seq, (SCAN_TILE,))
    nt = n_rows // tt
    tn = (((0,), (0,)), ((), ()))

    def kern(dy_ref, ds_ref, x_ref, u_ref, bd_ref, cd_ref, l1, l2, l4, pw, du_ref, da_ref, db_ref, dc_ref,
             gx_ref, lam_ref, car_ref, acc_ref):
        i = pl.program_id(0)

        @pl.when(((nt - i) * tt) % seq == 0)
        def _():
            car_ref[...] = jnp.zeros_like(car_ref)

        @pl.when(i == 0)
        def _():
            acc_ref[...] = jnp.zeros_like(acc_ref)
            db_ref[...] = jnp.zeros_like(db_ref)
            dc_ref[...] = jnp.zeros_like(dc_ref)

        for j in range(nb):
            gx_ref[:, pl.ds(j * w_blk, w_blk)] = lax.dot_general(dy_ref[:, pl.ds(j * blk, blk)], cd_ref[j], NT,
                                                                 preferred_element_type=F32)
        _scan_tile(gx_ref, lam_ref, (l1, l2, l4, pw), car_ref, sb, True, x_ref, acc_ref)
        for j in range(nb):
            cols, wide = pl.ds(j * blk, blk), pl.ds(j * w_blk, w_blk)
            lam = lam_ref[:, wide].astype(BF16)
            du_ref[:, cols] = ds_ref[:, cols] + lax.dot_general(lam, bd_ref[j], NT, preferred_element_type=F32)
            db_ref[j] += lax.dot_general(u_ref[:, cols].astype(BF16), lam, tn, preferred_element_type=F32)
            dc_ref[j] += lax.dot_general(x_ref[:, wide].astype(BF16), dy_ref[:, cols], tn, preferred_element_type=F32)

        @pl.when(i == nt - 1)
        def _():
            da_ref[...] = _colsum(acc_ref[...])

    back = lambda i: (nt - 1 - i, 0)
    tab = pl.BlockSpec((SUBLANE, w), lambda i: (0, 0))
    rows = pl.BlockSpec((tt, c), back)
    whole = lambda a: pl.BlockSpec(a.shape, lambda i: (0, 0, 0))
    return pl.pallas_call(
        kern, name="s5_bwd", grid=(nt,),
        in_specs=[rows, rows, pl.BlockSpec((tt, w), back), pl.BlockSpec((tt, c), lambda i: (nt - 1 - i, u_blk * blk // c)),
                  whole(bdc), whole(cdc), tab, tab, tab, tab],
        out_specs=[rows, pl.BlockSpec((1, w), lambda i: (0, 0)), whole(bdc), whole(cdc)],
        out_shape=[jax.ShapeDtypeStruct((n_rows, c), F32), jax.ShapeDtypeStruct((1, w), F32),
                   jax.ShapeDtypeStruct(bdc.shape, F32), jax.ShapeDtypeStruct(cdc.shape, F32)],
        scratch_shapes=[pltpu.VMEM((tt, w), F32), pltpu.VMEM((tt, w), F32), pltpu.VMEM((SUBLANE, w), F32),
                        pltpu.VMEM((SUBLANE, w), F32)],
        compiler_params=_params("arbitrary"))(dypre, du_skip, xs, proj, bdc, cdc, *tabs)


def _s5_post2(yg, q0, bg, og):
    c = yg.shape[1]

    def body(ins, outs, accs):
        ygv = ins[0][...].astype(F32)
        sg = ygv * _sigmoid(ins[1][...] + ins[2][...])
        outs[0][...] = (sg * _rms_r(sg) * ins[3][...]).astype(BF16)

    return _rowwise("s5_post2", body, yg.shape[0], [(yg, c, 0), (q0, c, 0)], [bg, og], [(c, BF16)], [])[0]


def _s5_post2_bwd(dmixed, yg, q0, bg, og):
    c = yg.shape[1]

    def body(ins, outs, accs):
        dsn, ygv = ins[0][...], ins[1][...].astype(F32)
        s = _sigmoid(ins[2][...] + ins[3][...])
        sg = ygv * s
        r = _rms_r(sg)
        accs[0][...] += _colsum(dsn * sg * r)
        dsg = _rms_bwd(sg, r, ins[4][...], dsn)
        dq = dsg * ygv * s * (1.0 - s)
        outs[0][...] = dq.astype(BF16)
        outs[1][...] = dsg * s
        accs[1][...] += _colsum(dq)

    return _rowwise("s5_post2_bwd", body, yg.shape[0], [(dmixed, c, 1), (yg, c, 0), (q0, c, 0)], [bg, og],
                    [(c, BF16), (c, F32)], [(1, c)] * 2)


def _s5_post1_bwd(dyg1, dyg2, ypre, proj, dskip, after=()):
    c = ypre.shape[1]

    def body(ins, outs, accs):
        dyp = (ins[0][...] + ins[1][...]) * _dgelu(ins[2][...])
        outs[0][...] = dyp.astype(BF16)
        outs[1][...] = dyp * ins[4][...]
        accs[0][...] += _colsum(dyp * ins[3][...])

    return _rowwise("s5_post1_bwd", body, ypre.shape[0], [(dyg1, c, 0), (dyg2, c, 0), (ypre, c, 0), (proj, c, 2)], [dskip],
                    [(c, BF16), (c, F32)], [(1, c)], after=after)


def _place():
    return lax.axis_index("x"), lax.axis_index("y"), lax.axis_index("c")


def _window(ref, axis, q, rows, cols):
    if axis == 0:
        return ref.at[pl.ds(pl.multiple_of(q * rows, SUBLANE), rows), :]
    return ref.at[:, pl.ds(pl.multiple_of(q * cols, LANE), cols)]


ALL_RELS = [(fx, fy, fc) for fx in (0, 1) for fy in (0, 1) for fc in (0, 1)][1:]
N_PEERS = {"gather": 3, "scatter": 3, "sibling": 1, "all": len(ALL_RELS)}


def _copies(kind, srcs, lands, shards, axes, send_sems, recv_sems, local_sems):
    x, y, c = _place()
    me, dev = 2 * x + y, 4 * x + 2 * y + c
    n_peers = N_PEERS[kind]
    starts, waits = [], []
    for a, (src, land) in enumerate(zip(srcs, lands)):
        on = lambda k, peer: dict(send_sem=send_sems.at[n_peers * a + k], recv_sem=recv_sems.at[n_peers * a + k],
                                  device_id=peer, device_id_type=MESH)
        if kind == "sibling":
            cp = pltpu.make_async_remote_copy(src_ref=src, dst_ref=land, **on(0, (x, y, 1 - c)))
            starts.append(cp)
            waits.append(cp)
            continue
        if kind == "all":
            own = pltpu.make_async_copy(src, land.at[dev], local_sems.at[a])
            starts.append(own)
            waits.append(own)
            for k, (fx, fy, fc) in enumerate(ALL_RELS):
                px, py, pc = (1 - x) if fx else x, (1 - y) if fy else y, (1 - c) if fc else c
                starts.append(pltpu.make_async_remote_copy(src_ref=src, dst_ref=land.at[dev], **on(k, (px, py, pc))))
                waits.append(pltpu.make_async_remote_copy(src_ref=src, dst_ref=land.at[4 * px + 2 * py + pc],
                                                          **on(k, (px, py, pc))))
            continue
        rows, cols = shards[a]
        if kind == "gather":
            own = pltpu.make_async_copy(src, _window(land, axes[a], me, rows, cols), local_sems.at[a])
        else:
            own = pltpu.make_async_copy(_window(src, axes[a], me, rows, cols), land.at[3], local_sems.at[a])
        starts.append(own)
        waits.append(own)
        for j, (fx, fy) in enumerate(CHIP_RELS):
            px, py = (1 - x) if fx else x, (1 - y) if fy else y
            peer = 2 * px + py
            if kind == "gather":
                starts.append(pltpu.make_async_remote_copy(src_ref=src, dst_ref=_window(land, axes[a], me, rows, cols),
                                                           **on(j, (px, py, c))))
                waits.append(pltpu.make_async_remote_copy(src_ref=src, dst_ref=_window(land, axes[a], peer, rows, cols),
                                                          **on(j, (px, py, c))))
            else:
                cp = pltpu.make_async_remote_copy(src_ref=_window(src, axes[a], peer, rows, cols), dst_ref=land.at[j],
                                                  **on(j, (px, py, c)))
                starts.append(cp)
                waits.append(cp)
    return starts, waits


HBM = pl.BlockSpec(memory_space=pltpu.HBM)
SEM = pl.BlockSpec(memory_space=pltpu.SEMAPHORE)


def _shard_shapes(kind, arrs, axes):
    if kind != "scatter":
        return [a.shape for a in arrs]
    return [(a.shape[0] // N_CHIPS, a.shape[1]) if ax == 0 else (a.shape[0], a.shape[1] // N_CHIPS) for a, ax in zip(arrs, axes)]


def _land_shapes(kind, arrs, axes):
    if kind == "gather":
        return [(N_CHIPS * a.shape[0], a.shape[1]) if ax == 0 else (a.shape[0], N_CHIPS * a.shape[1]) for a, ax in zip(arrs, axes)]
    if kind == "scatter":
        return [(N_CHIPS,) + s for s in _shard_shapes(kind, arrs, axes)]
    return [a.shape if kind == "sibling" else (len(ALL_RELS) + 1,) + a.shape for a in arrs]


def _exchange_start(name, kind, arrs, axes, after=()):
    n, n_after = len(arrs), len(after)
    shards = _shard_shapes(kind, arrs, axes)
    land_shapes = _land_shapes(kind, arrs, axes)
    lands = [lax.empty(s, a.dtype) for s, a in zip(land_shapes, arrs)]

    def kern(*refs):
        outs = refs[2 * n + n_after:]
        starts, _ = _copies(kind, refs[:n], refs[n:2 * n], shards, axes, outs[0], outs[1], outs[2])
        for cp in starts:
            cp.start()
        outs[-1][...] = jnp.zeros_like(outs[-1])

    kept = [pltpu.HBM(a.shape, a.dtype) for a in arrs] + [pltpu.HBM(s, a.dtype) for s, a in zip(land_shapes, arrs)]
    n_sems = N_PEERS[kind] * n
    res = pl.pallas_call(
        kern, name=name, in_specs=[HBM] * (2 * n) + [ANY] * n_after,
        out_specs=[SEM] * 3 + [HBM] * (2 * n) + [pl.BlockSpec(memory_space=pltpu.VMEM)],
        out_shape=[pltpu.SemaphoreType.DMA((n_sems,)), pltpu.SemaphoreType.DMA((n_sems,)), pltpu.SemaphoreType.DMA((n,))]
        + kept + [jax.ShapeDtypeStruct((SUBLANE, LANE), F32)],
        input_output_aliases={i: 3 + i for i in range(2 * n)},
        compiler_params=pltpu.CompilerParams(has_side_effects=pltpu.SideEffectType.DATAFLOW_SIDE_EFFECTING),
    )(*[pltpu.with_memory_space_constraint(a, pltpu.HBM) for a in list(arrs) + lands], *after)
    return res[:3], res[3:3 + n], res[3 + n:3 + 2 * n], res[-1]


def _exchange_wait(name, kind, started, axes, after, sources_too=False):
    sems, srcs, lands, _ = started
    n, n_after = len(srcs), len(after)
    shards = _shard_shapes(kind, srcs, axes)

    def kern(*refs):
        sem_refs = refs[2 * n:2 * n + 3]
        _, waits = _copies(kind, refs[:n], refs[n:2 * n], shards, axes, *sem_refs)
        for cp in waits:
            cp.wait()

    res = pl.pallas_call(
        kern, name=name, in_specs=[HBM] * (2 * n) + [SEM] * 3 + [ANY] * n_after, out_specs=[HBM] * (2 * n),
        out_shape=[pltpu.HBM(a.shape, a.dtype) for a in list(srcs) + list(lands)],
        input_output_aliases={i: i for i in range(2 * n)},
        compiler_params=pltpu.CompilerParams(has_side_effects=pltpu.SideEffectType.DATAFLOW_SIDE_EFFECTING),
    )(*srcs, *lands, *sems, *after)
    return (res[:n], res[n:]) if sources_too else res[n:]


def _sum_devices(parts):
    def kern(p_ref, o_ref):
        acc = p_ref[0]
        for d in range(1, parts.shape[0]):
            acc = acc + p_ref[d]
        o_ref[...] = acc

    return pl.pallas_call(kern, name="sum_devices", out_shape=jax.ShapeDtypeStruct(parts.shape[1:], F32),
                          compiler_params=pltpu.CompilerParams(vmem_limit_bytes=VMEM_LIMIT_BYTES))(parts)


def _sum_slots(name, parts):
    _, rows, cols = parts.shape
    tr = _pick(rows, (ROW_TILE, 128, 64, 32))

    def kern(p_ref, o_ref):
        o_ref[...] = ((p_ref[3].astype(F32) + p_ref[0].astype(F32)) + p_ref[1].astype(F32)) + p_ref[2].astype(F32)

    return pl.pallas_call(kern, name=name, grid=(rows // tr,),
                          in_specs=[pl.BlockSpec((N_CHIPS, tr, cols), lambda i: (0, i, 0))],
                          out_specs=pl.BlockSpec((tr, cols), lambda i: (i, 0)),
                          out_shape=jax.ShapeDtypeStruct((rows, cols), F32), compiler_params=_params("arbitrary"))(parts)


def _adamw_math(g, w, m, v):
    m2 = ADAM_B1 * m + (1.0 - ADAM_B1) * g
    v2 = ADAM_B2 * v + (1.0 - ADAM_B2) * (g * g)
    m_hat = m2 / (1.0 - ADAM_B1 ** ADAM_STEP)
    v_hat = v2 / (1.0 - ADAM_B2 ** ADAM_STEP)
    return -ADAM_LR * (m_hat / (jnp.sqrt(v_hat) + ADAM_EPS) + ADAM_WD * w), m2, v2


def _adamw(name, parts, w, m, v):
    rows, cols = w.shape
    tr = rows if rows * cols <= WHOLE_ELEMS else _pick(rows, (ROW_TILE, 352, 128, 64, 32, 8))
    n = len(parts)

    def kern(*refs):
        g = refs[0][:, pl.ds(0, cols)]
        for p in refs[1:n]:
            g = g + p[:, pl.ds(0, cols)]
        d, m2, v2 = _adamw_math(g, refs[n][...], refs[n + 1][...], refs[n + 2][...])
        refs[n + 3][...] = g
        refs[n + 4][...] = d
        refs[n + 5][...] = m2
        refs[n + 6][...] = v2

    spec = pl.BlockSpec((tr, cols), lambda i: (i, 0))
    return pl.pallas_call(kern, name=name, grid=(rows // tr,),
                          in_specs=[pl.BlockSpec((tr, p.shape[1]), lambda i: (i, 0)) for p in parts] + [spec] * 3,
                          out_specs=[spec] * 4, out_shape=[jax.ShapeDtypeStruct((rows, cols), F32)] * 4,
                          compiler_params=_params("arbitrary"))(*parts, w, m, v)


def _adamw_many(name, gs, ws, ms, vs):
    n = len(gs)

    def kern(*refs):
        for p in range(n):
            d, m2, v2 = _adamw_math(refs[p][...], refs[n + p][...], refs[2 * n + p][...], refs[3 * n + p][...])
            refs[4 * n + p][...] = d
            refs[5 * n + p][...] = m2
            refs[6 * n + p][...] = v2

    res = pl.pallas_call(kern, name=name, out_shape=[jax.ShapeDtypeStruct(w.shape, F32) for w in ws] * 3,
                         compiler_params=pltpu.CompilerParams(vmem_limit_bytes=VMEM_LIMIT_BYTES))(*gs, *ws, *ms, *vs)
    return res[:n], res[n:2 * n], res[2 * n:]


def _pack(arrs):
    parts, rows = [], []
    for a in arrs:
        r = _round_up(-(-a.size // LANE), SUBLANE)
        parts.append(jnp.pad(a.reshape(-1).astype(F32), (0, r * LANE - a.size)).reshape(r, LANE))
        rows.append(r)
    return jnp.concatenate(parts, axis=0), rows


def _unpack(buf, rows, shapes):
    out, r0 = [], 0
    for r, s in zip(rows, shapes):
        size = math.prod(s)
        out.append(buf[r0:r0 + r].reshape(-1)[:size].reshape(s))
        r0 += r
    return out


def kernel(x, norm_ffn1, ffn1_w1, ffn1_w3, ffn1_w2, norm_mix, w_in, conv_w, conv_b, conv_ln_g, conv_ln_b, conv_out_g, ssm_A_re, ssm_A_im, ssm_log_dt, ssm_B_re, ssm_B_im, ssm_C_re, ssm_C_im, ssm_D, ssm_glu_w, ssm_glu_b, ssm_out_g, w_out, norm_ffn2, ffn2_w1, ffn2_w3, ffn2_w2, norm_final, loss_target, m_norm_ffn1, m_ffn1_w1, m_ffn1_w3, m_ffn1_w2, m_norm_mix, m_w_in, m_conv_w, m_conv_b, m_conv_ln_g, m_conv_ln_b, m_conv_out_g, m_ssm_A_re, m_ssm_A_im, m_ssm_log_dt, m_ssm_B_re, m_ssm_B_im, m_ssm_C_re, m_ssm_C_im, m_ssm_D, m_ssm_glu_w, m_ssm_glu_b, m_ssm_out_g, m_w_out, m_norm_ffn2, m_ffn2_w1, m_ffn2_w3, m_ffn2_w2, m_norm_final, v_norm_ffn1, v_ffn1_w1, v_ffn1_w3, v_ffn1_w2, v_norm_mix, v_w_in, v_conv_w, v_conv_b, v_conv_ln_g, v_conv_ln_b, v_conv_out_g, v_ssm_A_re, v_ssm_A_im, v_ssm_log_dt, v_ssm_B_re, v_ssm_B_im, v_ssm_C_re, v_ssm_C_im, v_ssm_D, v_ssm_glu_w, v_ssm_glu_b, v_ssm_out_g, v_w_out, v_norm_ffn2, v_ffn2_w1, v_ffn2_w3, v_ffn2_w2, v_norm_final):
    given = dict(locals())
    wts = {n: given[n] for n in WEIGHTS}
    n_seq, seq, d = x.shape
    n_rows = n_seq * seq
    xf = x.reshape(n_rows, d)
    tgt = loss_target.reshape(n_rows, d)
    row = lambda a: a.reshape(1, -1)

    f = ffn1_w1.shape[-1]
    fp = _round_up(f, LANE)
    held = lambda n, a: a[0].T if n in TRANSPOSED else a[0]
    shards = []
    for n in BIG:
        s = held(n, wts[n]).astype(BF16)
        if n.startswith('ffn'):
            s = jnp.pad(s, ((0, fp - f), (0, 0)))
        shards.append(s)
    n_taps, c_shard = conv_w.shape[1], conv_w.shape[2]
    shards.append(jnp.pad(conv_w[0], ((0, HALO - n_taps), (0, 0))))
    shard_of = dict(zip(BIG + ['conv_w'], shards))
    axis_of = dict(BIG_AXIS, conv_w=1)
    groups = [['ffn1_w1', 'ffn1_w3'], ['ffn1_w2', 'w_in', 'conv_w', 'ssm_glu_w', 'w_out'], ['ffn2_w1', 'ffn2_w3', 'ffn2_w2']]
    fetch, tok = [], []
    for k, names in enumerate(groups):
        fetch.append(_exchange_start("gather%d_send" % k, "gather", [shard_of[n] for n in names],
                                     [axis_of[n] for n in names], tok))
        tok = [fetch[-1][3]]
    full = {}

    def arrive(k, after):
        lands = _exchange_wait("gather%d_recv" % k, "gather", fetch[k], [axis_of[n] for n in groups[k]], after)
        full.update(zip(groups[k], lands))

    h1, h1_t = _rms_fwd("ffn1_rms", xf, norm_ffn1)
    arrive(0, tok + [h1])

    _, n_grp, n_state = ssm_A_re.shape
    grp = ssm_B_re.shape[-1]
    ns = n_grp * n_state
    c_ssm = n_grp * grp
    lr, li = ssm_A_re.reshape(1, ns), ssm_A_im.reshape(1, ns)
    ldt = jnp.repeat(ssm_log_dt.reshape(n_grp), n_state).reshape(1, ns)
    btr = ssm_B_re[0].transpose(2, 0, 1).reshape(grp, ns)
    bti = ssm_B_im[0].transpose(2, 0, 1).reshape(grp, ns)
    ctr = ssm_C_re[0].transpose(1, 0, 2).reshape(grp, ns)
    cti = ssm_C_im[0].transpose(1, 0, 2).reshape(grp, ns)
    _, _, bbr, bbi, pw, pw_falling = _s5_params_fwd(lr, li, ldt, btr, bti)
    nb = c_ssm // LANE
    sb, gpb = ns // nb, n_grp // nb
    diag = (jnp.arange(LANE)[:, None] // grp) == (jnp.arange(sb)[None, :] // n_state)

    def spread(t):
        return jnp.where(diag, jnp.tile(t.reshape(grp, nb, sb).transpose(1, 0, 2), (1, gpb, 1)), 0.0)

    def gather_diag(t):
        return (t * diag).reshape(nb, gpb, grp, sb).sum(1).transpose(1, 0, 2).reshape(grp, ns)

    def interleave(re, im):
        return jnp.stack([re.reshape(-1, nb, sb), im.reshape(-1, nb, sb)], axis=2).reshape(-1, 2 * ns)

    bdc = jnp.concatenate([spread(bbr), spread(bbi)], axis=2).astype(BF16)
    cdc = jnp.concatenate([spread(ctr).transpose(0, 2, 1), -spread(cti).transpose(0, 2, 1)], axis=1).astype(BF16)
    rowi = jnp.arange(SUBLANE)[:, None]
    pwf, pwc = interleave(pw[:, :ns], pw[:, ns:]), interleave(pw[:, :ns], -pw[:, ns:])
    tabs_f = [jnp.where(rowi >= s, pwf[s - 1][None, :], 0.0) for s in (1, 2, 4)] + [pwf]
    tabs_b = [jnp.where(rowi <= SUBLANE - 1 - s, pwc[s - 1][None, :], 0.0) for s in (1, 2, 4)]
    tabs_b.append(interleave(pw_falling[:, :ns], -pw_falling[:, ns:]))
    c_conv = conv_b.shape[1]
    u_blk = 2 * c_conv // LANE

    a1, b1, z1 = _ffn_up("ffn1_up", h1, full['ffn1_w1'], full['ffn1_w3'])
    arrive(1, [z1])
    x1, h2, h2_t = _mm("ffn1_down", z1, full['ffn1_w2'], 1, 0, addend=xf, alpha=0.5, post=_post_rms(norm_mix))
    saved1 = (h1_t, a1, b1, z1)
    cw = full['conv_w']
    proj = _mm("mix_in", h2, full['w_in'], 1, 0, F32)
    assert c_conv == c_ssm and proj.shape[1] == 3 * c_conv
    cpre, an = _conv_fwd(proj, cw, conv_b, conv_ln_g, conv_ln_b, conv_out_g, seq)
    xs, ypre, yg = _s5_fwd(proj, u_blk, bdc, cdc, tabs_f, ssm_D, seq, sb)
    q0 = _mm("s5_gate", yg, full['ssm_glu_w'], 1, 0, F32)
    sn = _s5_post2(yg, q0, ssm_glu_b, ssm_out_g)
    wo = full['w_out']
    mixed = jnp.concatenate([an, sn], axis=1)
    x2, h3, h3_t = _mm("mix_out", mixed, wo, 1, 0, addend=x1, post=_post_rms(norm_ffn2))
    arrive(2, [x2])
    a3, b3, z3 = _ffn_up("ffn2_up", h3, full['ffn2_w1'], full['ffn2_w3'])
    saved2 = (h3_t, a3, b3, z3)
    dx3, dx3_t, loss_row, d_norm_final = _mm("ffn2_down", z3, full['ffn2_w2'], 1, 0, addend=x2, alpha=0.5,
                                             post=_post_loss(row(norm_final), tgt))

    g = {}
    dx2, g['norm_ffn2'], sent = _ffn_bwd("ffn2", x2, norm_ffn2, full['ffn2_w1'], full['ffn2_w3'], full['ffn2_w2'], saved2,
                                         dx3, dx3_t)
    dmixed = _mm("mix_dmixed", dx2, wo, 1, 1, F32)
    dwo = _mm("mix_dwo", mixed, dx2, 0, 0, BF16)
    dq, dyg1, g['ssm_out_g'], g['ssm_glu_b'] = _s5_post2_bwd(dmixed, yg, q0, ssm_glu_b, ssm_out_g)
    dyg2 = _mm("s5_dgate", dq, full['ssm_glu_w'], 1, 1, F32)
    dwg = _mm("s5_dwg", yg, dq, 0, 0, BF16)
    sent['w_out ssm_glu_w'] = (_exchange_start("mix_wo_wg_send", "scatter", [dwo, dwg], [0, 0]), [0, 0])
    dypre, du_skip, g['ssm_D'] = _s5_post1_bwd(dyg1, dyg2, ypre, proj, ssm_D, after=[sent['w_out ssm_glu_w'][0][3]])
    du, dabar, dbdc, dcdc = _s5_bwd(dypre, du_skip, xs, proj, u_blk, bdc, cdc, tabs_b, seq, sb)
    dabar = dabar.reshape(nb, 2, sb)
    dlr, dli, dldt, dbtr, dbti = _s5_params_bwd(lr, li, ldt, btr, bti, dabar[:, 0].reshape(1, ns), dabar[:, 1].reshape(1, ns),
                                                gather_diag(dbdc[:, :, :sb]), gather_diag(dbdc[:, :, sb:]))
    g['ssm_A_re'], g['ssm_A_im'] = dlr, dli
    g['ssm_log_dt'] = dldt.reshape(n_grp, n_state).sum(axis=1)
    g['ssm_B_re'] = dbtr.reshape(grp, n_grp, n_state).transpose(1, 2, 0)
    g['ssm_B_im'] = dbti.reshape(grp, n_grp, n_state).transpose(1, 2, 0)
    g['ssm_C_re'] = gather_diag(dcdc[:, :sb].transpose(0, 2, 1)).reshape(grp, n_grp, n_state).transpose(1, 0, 2)
    g['ssm_C_im'] = -gather_diag(dcdc[:, sb:].transpose(0, 2, 1)).reshape(grp, n_grp, n_state).transpose(1, 0, 2)
    dc, g['conv_out_g'], g['conv_ln_g'], g['conv_ln_b'], g['conv_b'] = _conv_bwd_rows(dmixed, cpre, conv_ln_g, conv_ln_b,
                                                                                    conv_out_g)
    dval, dgate, dcw = _conv_bwd_taps(proj, dc, cw, seq)
    dproj = jnp.concatenate([dval, dgate, du], axis=1)
    sent['w_in'] = (_exchange_start("mix_win_send", "scatter", [_mm("mix_dwin", h2_t, dproj, 1, 0, BF16)], [1]), [1])
    dx1, dx1_t, g['norm_mix'] = _mm("mix_dh", dproj, full['w_in'], 1, 1, after=[sent['w_in'][0][3]],
                                    post=_post_rms_bwd(x1, norm_mix, dx2))
    dx0, g['norm_ffn1'], sent1 = _ffn_bwd("ffn1", xf, norm_ffn1, full['ffn1_w1'], full['ffn1_w3'], full['ffn1_w2'], saved1,
                                          dx1, dx1_t)
    sent.update(sent1)
    g['norm_final'] = d_norm_final
    g['conv_w'] = dcw[:n_taps]

    small_shapes = [(n_taps, c_conv) if n == 'conv_w' else wts[n].shape for n in SMALL]
    buf, buf_rows = _pack([g[n] for n in SMALL] + [loss_row])
    to_all = _exchange_start("small_send", "all", [buf], [0])
    slots = {}
    for names, (started, axes) in sent.items():
        lands = _exchange_wait(names.replace(' ', '_') + "_recv", "scatter", started, axes, after=[dx0, to_all[3]])
        slots.update(zip(names.split(), lands))
    sums = [_sum_slots("sum_" + n, slots[n]) for n in BIG]
    to_sibling = _exchange_start("sums_send", "sibling", sums, [0] * len(sums))
    from_all = _exchange_wait("small_recv", "all", to_all, [0], after=[to_sibling[3]])[0]
    total = _unpack(_sum_devices(from_all), buf_rows, small_shapes + [(1, LANE)])
    loss = total[-1][0, 0]
    grads = dict(zip(SMALL, total[:-1]))
    chip = 2 * lax.axis_index("x") + lax.axis_index("y")
    grads['conv_w'] = lax.dynamic_slice_in_dim(grads['conv_w'], chip * c_shard, c_shard, axis=1)[None]
    flat = lambda a: a.reshape(-1, a.shape[-1])
    small = _adamw_many("adamw_small", *[[flat(src[p + n]) for n in SMALL]
                                         for src, p in ((grads, ''), (given, ''), (given, 'm_'), (given, 'v_'))])
    deltas, new_m, new_v = ({n: o.reshape(wts[n].shape) for n, o in zip(SMALL, outs)} for outs in small)

    sums, theirs = _exchange_wait("sums_recv", "sibling", to_sibling, [0] * len(sums), after=[new_v[SMALL[-1]]],
                                  sources_too=True)
    for n, mine, other in zip(BIG, sums, theirs):
        grads[n], deltas[n], new_m[n], new_v[n] = (
            (o.T if n in TRANSPOSED else o)[None]
            for o in _adamw("adamw_" + n, [mine, other], held(n, given[n]), held(n, given['m_' + n]), held(n, given['v_' + n])))

    return (loss, dx0.reshape(x.shape), *[grads[n] for n in WEIGHTS], *[deltas[n] for n in WEIGHTS],
            *[new_m[n] for n in WEIGHTS], *[new_v[n] for n in WEIGHTS])
```

```python
import math
from typing import Callable, NamedTuple

import jax
import jax.numpy as jnp
from jax import lax
from jax.experimental import pallas as pl
from jax.experimental.pallas import tpu as pltpu

F32 = jnp.float32
BF16 = jnp.bfloat16
EPS = 1e-6
ADAM_LR, ADAM_B1, ADAM_B2, ADAM_EPS, ADAM_WD, ADAM_STEP = 0.001, 0.9, 0.999, 1e-08, 0.01, 10
MESH = pl.DeviceIdType.MESH
ANY = pl.BlockSpec(memory_space=pl.ANY)
LANE = 128
SUBLANE = 8
VMEM_LIMIT_BYTES = 56 << 20
ROW_TILE = 256
ROW_TILE_ELEMS = 256 * 1024
WHOLE_ELEMS = 512 * 1024
WHOLE_WEIGHT_BYTES = 8 << 20
FFN_ROWS = 256
CONV_TILE = 128
CONV_SUB = 32
HALO = 32
SCAN_TILE = 512
SCAN_COLS = 512
N_CHIPS = 4
CHIP_RELS = ((1, 0), (0, 1), (1, 1))
NT = (((1,), (1,)), ((), ()))
GELU_K = math.sqrt(2.0 / math.pi)
GELU_C = 0.044715

WEIGHTS = ['norm_ffn1', 'ffn1_w1', 'ffn1_w3', 'ffn1_w2', 'norm_mix', 'w_in', 'conv_w', 'conv_b', 'conv_ln_g', 'conv_ln_b',
           'conv_out_g', 'ssm_A_re', 'ssm_A_im', 'ssm_log_dt', 'ssm_B_re', 'ssm_B_im', 'ssm_C_re', 'ssm_C_im', 'ssm_D',
           'ssm_glu_w', 'ssm_glu_b', 'ssm_out_g', 'w_out', 'norm_ffn2', 'ffn2_w1', 'ffn2_w3', 'ffn2_w2', 'norm_final']
BIG = ['ffn1_w1', 'ffn1_w3', 'ffn1_w2', 'w_in', 'ssm_glu_w', 'w_out', 'ffn2_w1', 'ffn2_w3', 'ffn2_w2']
BIG_AXIS = {'ffn1_w1': 0, 'ffn1_w3': 0, 'ffn1_w2': 0, 'w_in': 1, 'ssm_glu_w': 0, 'w_out': 0, 'ffn2_w1': 0, 'ffn2_w3': 0,
            'ffn2_w2': 0}
TRANSPOSED = ('ffn1_w1', 'ffn1_w3', 'ffn2_w1', 'ffn2_w3')
SMALL = [n for n in WEIGHTS if n not in BIG]


def _round_up(n, m):
    return -(-n // m) * m


def _pick(n, cands):
    for c in cands:
        if c <= n and n % c == 0:
            return c
    return n


def _params(*sem):
    return pltpu.CompilerParams(dimension_semantics=sem, vmem_limit_bytes=VMEM_LIMIT_BYTES)


def _rms_r(x):
    return lax.rsqrt(jnp.mean(x * x, axis=-1, keepdims=True) + EPS)


def _rms_bwd(x, r, g, dy):
    dyg = dy * g
    return r * dyg - x * (r * r * r) * jnp.mean(x * dyg, axis=-1, keepdims=True)


def _sigmoid(x):
    return jax.nn.sigmoid(x)


def _dsilu(a, s):
    return s * (1.0 + a * (1.0 - s))


def _gelu(x):
    return 0.5 * x * (1.0 + jnp.tanh(GELU_K * (x + GELU_C * x * x * x)))


def _dgelu(x):
    t = jnp.tanh(GELU_K * (x + GELU_C * x * x * x))
    return 0.5 * (1.0 + t) + 0.5 * x * (1.0 - t * t) * GELU_K * (1.0 + 3.0 * GELU_C * x * x)


def _colsum(v):
    return jnp.sum(v, axis=0, keepdims=True)


def _rowwise(name, body, n_rows, row_ins, par_ins, row_outs, acc_outs, after=()):
    widest = max([w for (_, w, _) in row_ins] + [w for (w, _) in row_outs])
    tt = _pick(n_rows, [t for t in (256, 128, 64, 32, 16, 8) if t * widest <= ROW_TILE_ELEMS])
    in_specs = [pl.BlockSpec((tt, w), lambda i, cb=cb: (i, cb)) for (_, w, cb) in row_ins]
    in_specs += [pl.BlockSpec(p.shape, lambda i: (0, 0)) for p in par_ins] + [ANY] * len(after)
    out_specs = [pl.BlockSpec((tt, w), lambda i: (i, 0)) for (w, _) in row_outs]
    out_specs += [pl.BlockSpec((r, w), lambda i: (0, 0)) for (r, w) in acc_outs]
    out_shape = [jax.ShapeDtypeStruct((n_rows, w), dt) for (w, dt) in row_outs]
    out_shape += [jax.ShapeDtypeStruct((r, w), F32) for (r, w) in acc_outs]
    n_in, n_ro = len(row_ins) + len(par_ins), len(row_outs)
    o0 = n_in + len(after)

    def kern(*refs):
        accs = refs[o0 + n_ro:]
        if accs:
            @pl.when(pl.program_id(0) == 0)
            def _():
                for a in accs:
                    a[...] = jnp.zeros_like(a)
        body(refs[:n_in], refs[o0:o0 + n_ro], accs)

    return pl.pallas_call(kern, name=name, grid=(n_rows // tt,), in_specs=in_specs, out_specs=out_specs, out_shape=out_shape,
                          compiler_params=_params("arbitrary"))(*[a for a, _, _ in row_ins], *par_ins, *after)


class Post(NamedTuple):
    rows: list
    gains: list
    outs: list
    t_outs: list
    sums: list
    fn: Callable


def _post_rms(gain):
    def fn(r, rows, gains):
        h = r * _rms_r(r) * gains[0]
        return [r, h, h], []

    return Post([], [gain], [F32, BF16], [BF16], [], fn)


def _post_rms_bwd(x, gain, dres):
    def fn(dh, rows, gains):
        r = _rms_r(rows[0])
        dx = rows[1] + _rms_bwd(rows[0], r, gains[0], dh)
        return [dx, dx], [_colsum(dh * rows[0] * r)]

    return Post([x, dres], [gain], [F32], [BF16], [x.shape[1]], fn)


def _post_loss(gain, tgt):
    d = tgt.shape[1]

    def fn(xv, rows, gains):
        r = _rms_r(xv)
        e = xv * r * gains[0] - rows[0]
        sq = jnp.sum(jnp.sum(e * e, axis=-1, keepdims=True), axis=0, keepdims=True)
        dy = e * (1.0 / d)
        dx = _rms_bwd(xv, r, gains[0], dy)
        return [dx, dx], [jnp.broadcast_to(sq * (0.5 / d), (1, LANE)), _colsum(dy * xv * r)]

    return Post([tgt], [gain], [F32], [BF16], [LANE, d], fn)


def _mm(name, a, b, ca, cb, out_dtype=F32, addend=None, alpha=1.0, a_cols=None, after=(), post=None, transposed=False):
    a_start, a_width = a_cols if a_cols else (0, a.shape[1])
    m, k = (a.shape[0], a_width) if ca == 1 else (a_width, a.shape[0])
    n = b.shape[1 - cb]
    assert b.shape[cb] == k, (name, a.shape, b.shape)
    tn = _pick(n, (1024, 768, 512, 384, 256, 128))
    if post and k * tn * b.dtype.itemsize <= WHOLE_WEIGHT_BYTES:
        tk = k
        tm = _pick(m, (256, 128) if k > 1024 else (512, 256, 128))
    else:
        tm = _pick(m, (512, 256, 128) if post else (1024, 512, 256, 128))
        tk = _pick(k, (2048, 1024, 768, 512, 256, 128) if k >= 4096 and not post else (1024, 768, 512, 256, 128))
    nk = k // tk
    if ca == 1:
        assert a_start % tk == 0
        a_spec = pl.BlockSpec((tm, tk), lambda i, j, kk: (i, kk + a_start // tk))
    else:
        assert a_start % tm == 0
        a_spec = pl.BlockSpec((tk, tm), lambda i, j, kk: (kk, i + a_start // tm))
    b_spec = pl.BlockSpec((tk, tn), lambda i, j, kk: (kk, j)) if cb == 0 else pl.BlockSpec((tn, tk), lambda i, j, kk: (j, kk))
    o_spec = pl.BlockSpec((tm, tn), lambda i, j, kk: (i, j))
    t_spec = pl.BlockSpec((tn, tm), lambda i, j, kk: (j, i))
    fixed = lambda w: pl.BlockSpec((1, w), lambda i, j, kk: (0, 0))
    ins, in_specs = [a, b], [a_spec, b_spec]
    if addend is not None:
        ins.append(addend)
        in_specs.append(o_spec)
    n_plain = len(ins)
    n_rows, n_gains = (len(post.rows), len(post.gains)) if post else (0, 0)
    if post:
        assert tn == n, name
        ins += post.rows + post.gains
        in_specs += [o_spec] * n_rows + [fixed(n)] * n_gains
    ins += list(after)
    in_specs += [ANY] * len(after)
    n_in = len(ins)
    if post:
        n_straight, n_vals = len(post.outs), len(post.outs) + len(post.t_outs)
        out_specs = [o_spec] * n_straight + [t_spec] * len(post.t_outs) + [fixed(w) for w in post.sums]
        out_shape = [jax.ShapeDtypeStruct((m, n), dt) for dt in post.outs] + [jax.ShapeDtypeStruct((n, m), dt) for dt in post.t_outs]
        out_shape += [jax.ShapeDtypeStruct((1, w), F32) for w in post.sums]
    elif transposed:
        out_specs, out_shape = [t_spec], [jax.ShapeDtypeStruct((n, m), out_dtype)]
    else:
        out_specs, out_shape = [o_spec], [jax.ShapeDtypeStruct((m, n), out_dtype)]
    n_out = len(out_specs)
    dims = (((ca,), (cb,)), ((), ()))

    def emit(refs, r):
        if alpha != 1.0:
            r = r * alpha
        if addend is not None:
            r = r + refs[2][...].astype(F32)
        outs = refs[n_in:n_in + n_out]
        if post is None:
            outs[0][...] = (r.T if transposed else r).astype(out_dtype)
            return
        vals, incs = post.fn(r, [q[...] for q in refs[n_plain:n_plain + n_rows]],
                             [q[...] for q in refs[n_plain + n_rows:n_plain + n_rows + n_gains]])
        for at, (o_ref, val) in enumerate(zip(outs, vals)):
            o_ref[...] = (val if at < n_straight else val.T).astype(o_ref.dtype)
        for s_ref, inc in zip(outs[n_vals:], incs):
            s_ref[...] += inc

    def kern(*refs):
        kk = pl.program_id(2)
        if post and post.sums:
            @pl.when(jnp.logical_and(jnp.logical_and(pl.program_id(0) == 0, pl.program_id(1) == 0), kk == 0))
            def _():
                for s_ref in refs[n_in + n_vals:n_in + n_out]:
                    s_ref[...] = jnp.zeros_like(s_ref)

        dot = lambda: lax.dot_general(refs[0][...].astype(BF16), refs[1][...].astype(BF16), dims,
                                      preferred_element_type=F32)
        if nk == 1:
            emit(refs, dot())
            return
        acc_ref = refs[-1]

        @pl.when(kk == 0)
        def _():
            acc_ref[...] = jnp.zeros_like(acc_ref)

        acc_ref[...] += dot()

        @pl.when(kk == nk - 1)
        def _():
            emit(refs, acc_ref[...])

    res = pl.pallas_call(kern, name=name, grid=(m // tm, n // tn, nk), in_specs=in_specs, out_specs=out_specs,
                         out_shape=out_shape, scratch_shapes=[] if nk == 1 else [pltpu.VMEM((tm, tn), F32)],
                         compiler_params=_params("arbitrary", "arbitrary", "arbitrary"))(*ins)
    return res if post else res[0]


def _rms_fwd(name, x, g):
    t, d = x.shape
    tt = _pick(t, (ROW_TILE, LANE))

    def kern(x_ref, g_ref, h_ref, ht_ref):
        xv = x_ref[...]
        h = xv * _rms_r(xv) * g_ref[...]
        h_ref[...] = h.astype(BF16)
        ht_ref[...] = h.T.astype(BF16)

    return pl.pallas_call(kern, name=name, grid=(t // tt,),
                          in_specs=[pl.BlockSpec((tt, d), lambda i: (i, 0)), pl.BlockSpec((1, d), lambda i: (0, 0))],
                          out_specs=[pl.BlockSpec((tt, d), lambda i: (i, 0)), pl.BlockSpec((d, tt), lambda i: (0, i))],
                          out_shape=[jax.ShapeDtypeStruct((t, d), BF16), jax.ShapeDtypeStruct((d, t), BF16)],
                          compiler_params=_params("arbitrary"))(x, g)


def _ffn_up(name, h, w1, w3):
    t, d = h.shape
    ff = w1.shape[0]
    tm, tn = _pick(t, (1024, 512, 256, 128)), _pick(ff, (1024, 768, 512, 256, 128))

    def kern(h_ref, w1_ref, w3_ref, a_ref, b_ref, z_ref):
        hv = h_ref[...]
        a = lax.dot_general(hv, w1_ref[...], NT, preferred_element_type=F32)
        b = lax.dot_general(hv, w3_ref[...], NT, preferred_element_type=F32)
        a_ref[...] = a.astype(BF16)
        b_ref[...] = b.astype(BF16)
        z_ref[...] = (a * _sigmoid(a) * b).astype(BF16)

    w_spec = pl.BlockSpec((tn, d), lambda i, j: (j, 0))
    o_spec = pl.BlockSpec((tm, tn), lambda i, j: (i, j))
    return pl.pallas_call(kern, name=name, grid=(t // tm, ff // tn),
                          in_specs=[pl.BlockSpec((tm, d), lambda i, j: (i, 0)), w_spec, w_spec], out_specs=[o_spec] * 3,
                          out_shape=[jax.ShapeDtypeStruct((t, ff), BF16)] * 3,
                          compiler_params=_params("arbitrary", "arbitrary"))(h, w1, w3)


def _ffn_dglu(name, dxo, w2, a, b, after=()):
    t, d = dxo.shape
    ff = w2.shape[0]
    tm = _pick(t, (FFN_ROWS, 128))

    def kern(dx_ref, w2_ref, a_ref, b_ref, *rest):
        da_ref, db_ref = rest[-2:]
        dz = lax.dot_general(dx_ref[...].astype(BF16), w2_ref[...], NT, preferred_element_type=F32) * 0.5
        av, bv = a_ref[...].astype(F32), b_ref[...].astype(F32)
        s = _sigmoid(av)
        da_ref[...] = (dz * bv * _dsilu(av, s)).astype(BF16)
        db_ref[...] = (dz * av * s).astype(BF16)

    o_spec = pl.BlockSpec((tm, ff), lambda i: (i, 0))
    return pl.pallas_call(kern, name=name, grid=(t // tm,),
                          in_specs=[pl.BlockSpec((tm, d), lambda i: (i, 0)), pl.BlockSpec((ff, d), lambda i: (0, 0)),
                                    o_spec, o_spec] + [ANY] * len(after),
                          out_specs=[o_spec] * 2, out_shape=[jax.ShapeDtypeStruct((t, ff), BF16)] * 2,
                          compiler_params=_params("arbitrary"))(dxo, w2, a, b, *after)


def _ffn_dh(name, da, db, w1, w3, x, g, dres, after=()):
    t, d = x.shape
    ff = da.shape[1]
    tm = _pick(t, (FFN_ROWS, 128))

    def kern(da_ref, db_ref, w1_ref, w3_ref, x_ref, g_ref, dres_ref, *rest):
        dx_ref, dg_ref = rest[-2:]

        @pl.when(pl.program_id(0) == 0)
        def _():
            dg_ref[...] = jnp.zeros_like(dg_ref)

        dh = (jnp.dot(da_ref[...], w1_ref[...], preferred_element_type=F32)
              + jnp.dot(db_ref[...], w3_ref[...], preferred_element_type=F32))
        xv = x_ref[...]
        r = _rms_r(xv)
        dx_ref[...] = dres_ref[...] + _rms_bwd(xv, r, g_ref[...], dh)
        dg_ref[...] += _colsum(dh * xv * r)

    act = pl.BlockSpec((tm, ff), lambda i: (i, 0))
    wgt = pl.BlockSpec((ff, d), lambda i: (0, 0))
    rows = pl.BlockSpec((tm, d), lambda i: (i, 0))
    gain = pl.BlockSpec((1, d), lambda i: (0, 0))
    return pl.pallas_call(kern, name=name, grid=(t // tm,),
                          in_specs=[act, act, wgt, wgt, rows, gain, rows] + [ANY] * len(after), out_specs=[rows, gain],
                          out_shape=[jax.ShapeDtypeStruct((t, d), F32), jax.ShapeDtypeStruct((1, d), F32)],
                          compiler_params=_params("arbitrary"))(da, db, w1, w3, x, g, dres, *after)


def _ffn_bwd(tag, x, g, w1, w3, w2, saved, dxo, dxo_t):
    ht, a, b, z = saved
    dw2 = _mm(tag + "_dw2", dxo_t, z, 1, 0, BF16, alpha=0.5, transposed=True)
    s2 = _exchange_start(tag + "_w2_send", "scatter", [dw2], [0])
    da, db = _ffn_dglu(tag + "_dglu", dxo, w2, a, b, after=[s2[3]])
    dw1 = _mm(tag + "_dw1", ht, da, 1, 0, BF16, transposed=True)
    s1 = _exchange_start(tag + "_w1_send", "scatter", [dw1], [0])
    dw3 = _mm(tag + "_dw3", ht, db, 1, 0, BF16, after=[s1[3]], transposed=True)
    s3 = _exchange_start(tag + "_w3_send", "scatter", [dw3], [0])
    dx, dg = _ffn_dh(tag + "_dh", da, db, w1, w3, x, g, dxo, after=[s3[3]])
    return dx, dg, {tag + "_w1": (s1, [0]), tag + "_w3": (s3, [0]), tag + "_w2": (s2, [0])}


def _shift_copies(ext_ref, sh_ref):
    n = ext_ref.shape[0] - SUBLANE
    for r in range(1, SUBLANE):
        sh_ref[r, pl.ds(0, n), :] = ext_ref[pl.ds(r, n), :]


def _rows_at(ext_ref, sh_ref, off, rows):
    r = off % SUBLANE
    return ext_ref[pl.ds(off, rows), :] if r == 0 else sh_ref[r, pl.ds(off - r, rows), :]


def _conv_fwd(proj, cw, cb, lng, lnb, og, seq):
    n_rows, c = proj.shape[0], cb.shape[1]
    kw = HALO - 1
    tt = _pick(seq, (CONV_TILE,))
    hb = tt // HALO

    def kern(v_ref, g_ref, vp_ref, gp_ref, w_ref, cb_ref, lg_ref, lb_ref, og_ref, c_ref, an_ref, ext_ref, sh_ref):
        first = (pl.program_id(0) * tt) % seq == 0
        ext_ref[pl.ds(HALO, tt), :] = v_ref[...] * _sigmoid(g_ref[...])
        ext_ref[pl.ds(0, HALO), :] = vp_ref[...] * _sigmoid(gp_ref[...]) * jnp.where(first, 0.0, 1.0)
        _shift_copies(ext_ref, sh_ref)
        for r0 in range(0, tt, CONV_SUB):
            rows = min(CONV_SUB, tt - r0)
            acc = jnp.zeros((rows, c), F32)
            for k in range(kw):
                acc = acc + w_ref[pl.ds(k, 1), :] * _rows_at(ext_ref, sh_ref, r0 + HALO - (kw - 1) + k, rows)
            c_ref[pl.ds(r0, rows), :] = acc + cb_ref[...]
        cv = c_ref[...]
        mu = jnp.mean(cv, axis=-1, keepdims=True)
        xc = cv - mu
        rstd = lax.rsqrt(jnp.mean(xc * xc, axis=-1, keepdims=True) + EPS)
        lv = xc * rstd * lg_ref[...] + lb_ref[...]
        sl = lv * _sigmoid(lv)
        an_ref[...] = (sl * _rms_r(sl) * og_ref[...]).astype(BF16)

    cur = lambda cbk: pl.BlockSpec((tt, c), lambda i: (i, cbk))
    prev = lambda cbk: pl.BlockSpec((HALO, c), lambda i: (jnp.maximum(i * hb - 1, 0), cbk))
    par = lambda p: pl.BlockSpec(p.shape, lambda i: (0, 0))
    return pl.pallas_call(
        kern, name="conv_fwd", grid=(n_rows // tt,),
        in_specs=[cur(0), cur(1), prev(0), prev(1), par(cw), par(cb), par(lng), par(lnb), par(og)],
        out_specs=[pl.BlockSpec((tt, c), lambda i: (i, 0))] * 2,
        out_shape=[jax.ShapeDtypeStruct((n_rows, c), F32), jax.ShapeDtypeStruct((n_rows, c), BF16)],
        scratch_shapes=[pltpu.VMEM((tt + HALO, c), F32), pltpu.VMEM((SUBLANE, tt + HALO, c), F32)],
        compiler_params=_params("arbitrary"),
    )(proj, proj, proj, proj, cw, cb, lng, lnb, og)


def _conv_bwd_rows(dmixed, cpre, lng, lnb, og):
    c = cpre.shape[1]

    def body(ins, outs, accs):
        dan, cv, lg, lb, ogv = ins[0][...], ins[1][...], ins[2][...], ins[3][...], ins[4][...]
        mu = jnp.mean(cv, axis=-1, keepdims=True)
        xc = cv - mu
        rstd = lax.rsqrt(jnp.mean(xc * xc, axis=-1, keepdims=True) + EPS)
        xh = xc * rstd
        lv = xh * lg + lb
        s = _sigmoid(lv)
        sl = lv * s
        r2 = _rms_r(sl)
        accs[0][...] += _colsum(dan * sl * r2)
        dl = _rms_bwd(sl, r2, ogv, dan) * _dsilu(lv, s)
        accs[1][...] += _colsum(dl * xh)
        accs[2][...] += _colsum(dl)
        dxh = dl * lg
        dc = rstd * (dxh - jnp.mean(dxh, axis=-1, keepdims=True) - xh * jnp.mean(dxh * xh, axis=-1, keepdims=True))
        outs[0][...] = dc
        accs[3][...] += _colsum(dc)

    return _rowwise("conv_bwd_rows", body, cpre.shape[0], [(dmixed, c, 0), (cpre, c, 0)], [lng, lnb, og], [(c, F32)],
                    [(1, c)] * 4)


def _conv_bwd_taps(proj, dc, cw, seq):
    n_rows, c = dc.shape
    kw = HALO - 1
    tt = _pick(seq, (CONV_TILE,))
    hb = tt // HALO
    last_blk = n_rows // HALO - 1

    def kern(v_ref, g_ref, vp_ref, gp_ref, dc_ref, dn_ref, w_ref, dv_ref, dg_ref, dw_ref, exta_ref, extd_ref, sha_ref, shd_ref):
        i = pl.program_id(0)
        first = (i * tt) % seq == 0
        last = ((i + 1) * tt) % seq == 0

        @pl.when(i == 0)
        def _():
            dw_ref[...] = jnp.zeros_like(dw_ref)

        sg = _sigmoid(g_ref[...])
        exta_ref[pl.ds(HALO, tt), :] = v_ref[...] * sg
        exta_ref[pl.ds(0, HALO), :] = vp_ref[...] * _sigmoid(gp_ref[...]) * jnp.where(first, 0.0, 1.0)
        dcv = dc_ref[...]
        extd_ref[pl.ds(0, tt), :] = dcv
        extd_ref[pl.ds(tt, HALO), :] = dn_ref[...] * jnp.where(last, 0.0, 1.0)
        _shift_copies(exta_ref, sha_ref)
        _shift_copies(extd_ref, shd_ref)
        for k in range(kw):
            dw_ref[pl.ds(k, 1), :] += _colsum(_rows_at(exta_ref, sha_ref, HALO - (kw - 1) + k, tt) * dcv)
        for r0 in range(0, tt, CONV_SUB):
            rows = min(CONV_SUB, tt - r0)
            acc = jnp.zeros((rows, c), F32)
            for k in range(kw):
                acc = acc + w_ref[pl.ds(k, 1), :] * _rows_at(extd_ref, shd_ref, r0 + (kw - 1) - k, rows)
            dv_ref[pl.ds(r0, rows), :] = acc
        da = dv_ref[...]
        dv_ref[...] = da * sg
        dg_ref[...] = da * v_ref[...] * sg * (1.0 - sg)

    cur = lambda cbk: pl.BlockSpec((tt, c), lambda i: (i, cbk))
    prev = lambda cbk: pl.BlockSpec((HALO, c), lambda i: (jnp.maximum(i * hb - 1, 0), cbk))
    nxt = pl.BlockSpec((HALO, c), lambda i: (jnp.minimum((i + 1) * hb, last_blk), 0))
    return pl.pallas_call(
        kern, name="conv_bwd_taps", grid=(n_rows // tt,),
        in_specs=[cur(0), cur(1), prev(0), prev(1), cur(0), nxt, pl.BlockSpec(cw.shape, lambda i: (0, 0))],
        out_specs=[cur(0), cur(0), pl.BlockSpec((HALO, c), lambda i: (0, 0))],
        out_shape=[jax.ShapeDtypeStruct((n_rows, c), F32), jax.ShapeDtypeStruct((n_rows, c), F32),
                   jax.ShapeDtypeStruct((HALO, c), F32)],
        scratch_shapes=[pltpu.VMEM((tt + HALO, c), F32)] * 2 + [pltpu.VMEM((SUBLANE, tt + HALO, c), F32)] * 2,
        compiler_params=_params("arbitrary"),
    )(proj, proj, proj, proj, dc, dc, cw)


def _s5_params_fwd(lr, li, ldt, btr, bti):
    ns = lr.shape[1]

    def kern(lr_ref, li_ref, ldt_ref, btr_ref, bti_ref, ar_ref, ai_ref, bbr_ref, bbi_ref, pw_ref, pwr_ref):
        lrv, liv = lr_ref[...], li_ref[...]
        dt = jnp.exp(ldt_ref[...])
        zr, zi = lrv * dt, liv * dt
        mag = jnp.exp(zr)
        ar, ai = mag * jnp.cos(zi), mag * jnp.sin(zi)
        den = lrv * lrv + liv * liv
        nr = ar - 1.0
        cr = (nr * lrv + ai * liv) / den
        ci = (ai * lrv - nr * liv) / den
        ar_ref[...] = ar
        ai_ref[...] = ai
        bbr_ref[...] = cr * btr_ref[...] - ci * bti_ref[...]
        bbi_ref[...] = cr * bti_ref[...] + ci * btr_ref[...]
        pr, pi = ar, ai
        for e in range(SUBLANE):
            for ref, at in ((pw_ref, e), (pwr_ref, SUBLANE - 1 - e)):
                ref[pl.ds(at, 1), pl.ds(0, ns)] = pr
                ref[pl.ds(at, 1), pl.ds(ns, ns)] = pi
            pr, pi = pr * ar - pi * ai, pr * ai + pi * ar

    h = btr.shape[0]
    shapes = [jax.ShapeDtypeStruct((1, ns), F32)] * 2 + [jax.ShapeDtypeStruct((h, ns), F32)] * 2
    shapes += [jax.ShapeDtypeStruct((SUBLANE, 2 * ns), F32)] * 2
    return pl.pallas_call(kern, name="s5_params_fwd", out_shape=shapes)(lr, li, ldt, btr, bti)


def _s5_params_bwd(lr, li, ldt, btr, bti, dar, dai, dbbr, dbbi):
    def kern(lr_ref, li_ref, ldt_ref, btr_ref, bti_ref, dar_ref, dai_ref, dbr_ref, dbi_ref,
             dlr_ref, dli_ref, dldt_ref, dbtr_ref, dbti_ref):
        lrv, liv = lr_ref[...], li_ref[...]
        dt = jnp.exp(ldt_ref[...])
        zr, zi = lrv * dt, liv * dt
        mag = jnp.exp(zr)
        ar, ai = mag * jnp.cos(zi), mag * jnp.sin(zi)
        den = lrv * lrv + liv * liv
        nr = ar - 1.0
        cr = (nr * lrv + ai * liv) / den
        ci = (ai * lrv - nr * liv) / den
        dbr, dbi, br, bi = dbr_ref[...], dbi_ref[...], btr_ref[...], bti_ref[...]
        dbtr_ref[...] = cr * dbr + ci * dbi
        dbti_ref[...] = cr * dbi - ci * dbr
        dcr = _colsum(br * dbr + bi * dbi)
        dci = _colsum(br * dbi - bi * dbr)
        ir, ii = lrv / den, -liv / den
        dnr = ir * dcr + ii * dci
        dni = ir * dci - ii * dcr
        wr, wi = cr * ir - ci * ii, cr * ii + ci * ir
        dl1r = -(wr * dcr + wi * dci)
        dl1i = -(wr * dci - wi * dcr)
        dtr, dti = dar_ref[...] + dnr, dai_ref[...] + dni
        dzr = ar * dtr + ai * dti
        dzi = ar * dti - ai * dtr
        dlr_ref[...] = dl1r + dt * dzr
        dli_ref[...] = dl1i + dt * dzi
        dldt_ref[...] = (dzr * lrv + dzi * liv) * dt

    ns, h = lr.shape[1], btr.shape[0]
    shapes = [jax.ShapeDtypeStruct((1, ns), F32)] * 3 + [jax.ShapeDtypeStruct((h, ns), F32)] * 2
    return pl.pallas_call(kern, name="s5_params_bwd", out_shape=shapes)(lr, li, ldt, btr, bti, dar, dai, dbbr, dbbi)


def _scan_tile(s_ref, o_ref, tabs, car_ref, sb, reverse, x_ref=None, acc_ref=None):
    l1, l2, l4, pw = tabs
    rows_t, w = s_ref.shape
    ng = rows_t // SUBLANE
    cw = _pick(sb, (SCAN_COLS,))
    carry_row = 0 if reverse else SUBLANE - 1
    row = lax.broadcasted_iota(jnp.int32, (SUBLANE, cw), 0)

    def group(gi, carry):
        g = (ng - 1 - gi) if reverse else gi
        rows = pl.ds(pl.multiple_of(g * SUBLANE, SUBLANE), SUBLANE)
        for c0 in [b0 + o for b0 in range(0, w, 2 * sb) for o in range(0, sb, cw)]:
            cr, ci = pl.ds(c0, cw), pl.ds(c0 + sb, cw)
            xr, xi = s_ref[rows, cr], s_ref[rows, ci]
            for s, lt in ((1, l1), (2, l2), (4, l4)):
                sh = (SUBLANE - s) if reverse else s
                sr, si = pltpu.roll(xr, sh, 0), pltpu.roll(xi, sh, 0)
                ar, ai = lt[:, cr], lt[:, ci]
                xr, xi = xr + ar * sr - ai * si, xi + ar * si + ai * sr
            kr, ki = car_ref[pl.ds(carry_row, 1), cr], car_ref[pl.ds(carry_row, 1), ci]
            pr, pi = pw[:, cr], pw[:, ci]
            xr, xi = xr + pr * kr - pi * ki, xi + pr * ki + pi * kr
            o_ref[rows, cr] = xr
            o_ref[rows, ci] = xi
            car_ref[:, cr] = xr
            car_ref[:, ci] = xi
            if acc_ref is not None:
                nr = jnp.where(row == SUBLANE - 1, kr, pltpu.roll(xr, SUBLANE - 1, 0))
                ni = jnp.where(row == SUBLANE - 1, ki, pltpu.roll(xi, SUBLANE - 1, 0))
                pxr, pxi = x_ref[rows, cr], x_ref[rows, ci]
                acc_ref[:, cr] += nr * pxr + ni * pxi
                acc_ref[:, ci] += ni * pxr - nr * pxi
        return carry

    lax.fori_loop(0, ng, group, 0)


def _s5_fwd(proj, u_blk, bdc, cdc, tabs, dskip, seq, sb):
    n_rows = proj.shape[0]
    nb, blk, w_blk = bdc.shape
    c, w = nb * blk, nb * w_blk
    tt = _pick(seq, (SCAN_TILE,))

    def kern(u_ref, bd_ref, cd_ref, l1, l2, l4, pw, d_ref, xs_ref, yp_ref, yg_ref, bu_ref, car_ref):
        @pl.when((pl.program_id(0) * tt) % seq == 0)
        def _():
            car_ref[...] = jnp.zeros_like(car_ref)

        for j in range(nb):
            bu_ref[:, pl.ds(j * w_blk, w_blk)] = jnp.dot(u_ref[:, pl.ds(j * blk, blk)].astype(BF16), bd_ref[j],
                                                         preferred_element_type=F32)
        _scan_tile(bu_ref, xs_ref, (l1, l2, l4, pw), car_ref, sb, False)
        for j in range(nb):
            cols = pl.ds(j * blk, blk)
            y0 = jnp.dot(xs_ref[:, pl.ds(j * w_blk, w_blk)].astype(BF16), cd_ref[j], preferred_element_type=F32)
            ypre = y0 + d_ref[:, cols] * u_ref[:, cols]
            yp_ref[:, cols] = ypre
            yg_ref[:, cols] = _gelu(ypre).astype(BF16)

    tab = pl.BlockSpec((SUBLANE, w), lambda i: (0, 0))
    rows = pl.BlockSpec((tt, c), lambda i: (i, 0))
    return pl.pallas_call(
        kern, name="s5_fwd", grid=(n_rows // tt,),
        in_specs=[pl.BlockSpec((tt, c), lambda i: (i, u_blk * blk // c)), pl.BlockSpec(bdc.shape, lambda i: (0, 0, 0)),
                  pl.BlockSpec(cdc.shape, lambda i: (0, 0, 0)), tab, tab, tab, tab, pl.BlockSpec((1, c), lambda i: (0, 0))],
        out_specs=[pl.BlockSpec((tt, w), lambda i: (i, 0)), rows, rows],
        out_shape=[jax.ShapeDtypeStruct((n_rows, w), F32), jax.ShapeDtypeStruct((n_rows, c), F32),
                   jax.ShapeDtypeStruct((n_rows, c), BF16)],
        scratch_shapes=[pltpu.VMEM((tt, w), F32), pltpu.VMEM((SUBLANE, w), F32)],
        compiler_params=_params("arbitrary"))(proj, bdc, cdc, *tabs, dskip)


def _s5_bwd(dypre, du_skip, xs, proj, u_blk, bdc, cdc, tabs, seq, sb):
    n_rows = proj.shape[0]
    nb, blk, w_blk = bdc.shape
    c, w = nb * blk, nb * w_blk
    tt = _pick(seq, (SCAN_TILE,))
    nt = n_rows // tt
    tn = (((0,), (0,)), ((), ()))

    def kern(dy_ref, ds_ref, x_ref, u_ref, bd_ref, cd_ref, l1, l2, l4, pw, du_ref, da_ref, db_ref, dc_ref,
             gx_ref, lam_ref, car_ref, acc_ref):
        i = pl.program_id(0)

        @pl.when(((nt - i) * tt) % seq == 0)
        def _():
            car_ref[...] = jnp.zeros_like(car_ref)

        @pl.when(i == 0)
        def _():
            acc_ref[...] = jnp.zeros_like(acc_ref)
            db_ref[...] = jnp.zeros_like(db_ref)
            dc_ref[...] = jnp.zeros_like(dc_ref)

        for j in range(nb):
            gx_ref[:, pl.ds(j * w_blk, w_blk)] = lax.dot_general(dy_ref[:, pl.ds(j * blk, blk)], cd_ref[j], NT,
                                                                 preferred_element_type=F32)
        _scan_tile(gx_ref, lam_ref, (l1, l2, l4, pw), car_ref, sb, True, x_ref, acc_ref)
        for j in range(nb):
            cols, wide = pl.ds(j * blk, blk), pl.ds(j * w_blk, w_blk)
            lam = lam_ref[:, wide].astype(BF16)
            du_ref[:, cols] = ds_ref[:, cols] + lax.dot_general(lam, bd_ref[j], NT, preferred_element_type=F32)
            db_ref[j] += lax.dot_general(u_ref[:, cols].astype(BF16), lam, tn, preferred_element_type=F32)
            dc_ref[j] += lax.dot_general(x_ref[:, wide].astype(BF16), dy_ref[:, cols], tn, preferred_element_type=F32)

        @pl.when(i == nt - 1)
        def _():
            da_ref[...] = _colsum(acc_ref[...])

    back = lambda i: (nt - 1 - i, 0)
    tab = pl.BlockSpec((SUBLANE, w), lambda i: (0, 0))
    rows = pl.BlockSpec((tt, c), back)
    whole = lambda a: pl.BlockSpec(a.shape, lambda i: (0, 0, 0))
    return pl.pallas_call(
        kern, name="s5_bwd", grid=(nt,),
        in_specs=[rows, rows, pl.BlockSpec((tt, w), back), pl.BlockSpec((tt, c), lambda i: (nt - 1 - i, u_blk * blk // c)),
                  whole(bdc), whole(cdc), tab, tab, tab, tab],
        out_specs=[rows, pl.BlockSpec((1, w), lambda i: (0, 0)), whole(bdc), whole(cdc)],
        out_shape=[jax.ShapeDtypeStruct((n_rows, c), F32), jax.ShapeDtypeStruct((1, w), F32),
                   jax.ShapeDtypeStruct(bdc.shape, F32), jax.ShapeDtypeStruct(cdc.shape, F32)],
        scratch_shapes=[pltpu.VMEM((tt, w), F32), pltpu.VMEM((tt, w), F32), pltpu.VMEM((SUBLANE, w), F32),
                        pltpu.VMEM((SUBLANE, w), F32)],
        compiler_params=_params("arbitrary"))(dypre, du_skip, xs, proj, bdc, cdc, *tabs)


def _s5_post2(yg, q0, bg, og):
    c = yg.shape[1]

    def body(ins, outs, accs):
        ygv = ins[0][...].astype(F32)
        sg = ygv * _sigmoid(ins[1][...] + ins[2][...])
        outs[0][...] = (sg * _rms_r(sg) * ins[3][...]).astype(BF16)

    return _rowwise("s5_post2", body, yg.shape[0], [(yg, c, 0), (q0, c, 0)], [bg, og], [(c, BF16)], [])[0]


def _s5_post2_bwd(dmixed, yg, q0, bg, og):
    c = yg.shape[1]

    def body(ins, outs, accs):
        dsn, ygv = ins[0][...], ins[1][...].astype(F32)
        s = _sigmoid(ins[2][...] + ins[3][...])
        sg = ygv * s
        r = _rms_r(sg)
        accs[0][...] += _colsum(dsn * sg * r)
        dsg = _rms_bwd(sg, r, ins[4][...], dsn)
        dq = dsg * ygv * s * (1.0 - s)
        outs[0][...] = dq.astype(BF16)
        outs[1][...] = dsg * s
        accs[1][...] += _colsum(dq)

    return _rowwise("s5_post2_bwd", body, yg.shape[0], [(dmixed, c, 1), (yg, c, 0), (q0, c, 0)], [bg, og],
                    [(c, BF16), (c, F32)], [(1, c)] * 2)


def _s5_post1_bwd(dyg1, dyg2, ypre, proj, dskip, after=()):
    c = ypre.shape[1]

    def body(ins, outs, accs):
        dyp = (ins[0][...] + ins[1][...]) * _dgelu(ins[2][...])
        outs[0][...] = dyp.astype(BF16)
        outs[1][...] = dyp * ins[4][...]
        accs[0][...] += _colsum(dyp * ins[3][...])

    return _rowwise("s5_post1_bwd", body, ypre.shape[0], [(dyg1, c, 0), (dyg2, c, 0), (ypre, c, 0), (proj, c, 2)], [dskip],
                    [(c, BF16), (c, F32)], [(1, c)], after=after)


def _place():
    return lax.axis_index("x"), lax.axis_index("y"), lax.axis_index("c")


def _window(ref, axis, q, rows, cols):
    if axis == 0:
        return ref.at[pl.ds(pl.multiple_of(q * rows, SUBLANE), rows), :]
    return ref.at[:, pl.ds(pl.multiple_of(q * cols, LANE), cols)]


ALL_RELS = [(fx, fy, fc) for fx in (0, 1) for fy in (0, 1) for fc in (0, 1)][1:]
N_PEERS = {"gather": 3, "scatter": 3, "sibling": 1, "all": len(ALL_RELS)}


def _copies(kind, srcs, lands, shards, axes, send_sems, recv_sems, local_sems):
    x, y, c = _place()
    me, dev = 2 * x + y, 4 * x + 2 * y + c
    n_peers = N_PEERS[kind]
    starts, waits = [], []
    for a, (src, land) in enumerate(zip(srcs, lands)):
        on = lambda k, peer: dict(send_sem=send_sems.at[n_peers * a + k], recv_sem=recv_sems.at[n_peers * a + k],
                                  device_id=peer, device_id_type=MESH)
        if kind == "sibling":
            cp = pltpu.make_async_remote_copy(src_ref=src, dst_ref=land, **on(0, (x, y, 1 - c)))
            starts.append(cp)
            waits.append(cp)
            continue
        if kind == "all":
            own = pltpu.make_async_copy(src, land.at[dev], local_sems.at[a])
            starts.append(own)
            waits.append(own)
            for k, (fx, fy, fc) in enumerate(ALL_RELS):
                px, py, pc = (1 - x) if fx else x, (1 - y) if fy else y, (1 - c) if fc else c
                starts.append(pltpu.make_async_remote_copy(src_ref=src, dst_ref=land.at[dev], **on(k, (px, py, pc))))
                waits.append(pltpu.make_async_remote_copy(src_ref=src, dst_ref=land.at[4 * px + 2 * py + pc],
                                                          **on(k, (px, py, pc))))
            continue
        rows, cols = shards[a]
        if kind == "gather":
            own = pltpu.make_async_copy(src, _window(land, axes[a], me, rows, cols), local_sems.at[a])
        else:
            own = pltpu.make_async_copy(_window(src, axes[a], me, rows, cols), land.at[3], local_sems.at[a])
        starts.append(own)
        waits.append(own)
        for j, (fx, fy) in enumerate(CHIP_RELS):
            px, py = (1 - x) if fx else x, (1 - y) if fy else y
            peer = 2 * px + py
            if kind == "gather":
                starts.append(pltpu.make_async_remote_copy(src_ref=src, dst_ref=_window(land, axes[a], me, rows, cols),
                                                           **on(j, (px, py, c))))
                waits.append(pltpu.make_async_remote_copy(src_ref=src, dst_ref=_window(land, axes[a], peer, rows, cols),
                                                          **on(j, (px, py, c))))
            else:
                cp = pltpu.make_async_remote_copy(src_ref=_window(src, axes[a], peer, rows, cols), dst_ref=land.at[j],
                                                  **on(j, (px, py, c)))
                starts.append(cp)
                waits.append(cp)
    return starts, waits


HBM = pl.BlockSpec(memory_space=pltpu.HBM)
SEM = pl.BlockSpec(memory_space=pltpu.SEMAPHORE)


def _shard_shapes(kind, arrs, axes):
    if kind != "scatter":
        return [a.shape for a in arrs]
    return [(a.shape[0] // N_CHIPS, a.shape[1]) if ax == 0 else (a.shape[0], a.shape[1] // N_CHIPS) for a, ax in zip(arrs, axes)]


def _land_shapes(kind, arrs, axes):
    if kind == "gather":
        return [(N_CHIPS * a.shape[0], a.shape[1]) if ax == 0 else (a.shape[0], N_CHIPS * a.shape[1]) for a, ax in zip(arrs, axes)]
    if kind == "scatter":
        return [(N_CHIPS,) + s for s in _shard_shapes(kind, arrs, axes)]
    return [a.shape if kind == "sibling" else (len(ALL_RELS) + 1,) + a.shape for a in arrs]


def _exchange_start(name, kind, arrs, axes, after=()):
    n, n_after = len(arrs), len(after)
    shards = _shard_shapes(kind, arrs, axes)
    land_shapes = _land_shapes(kind, arrs, axes)
    lands = [lax.empty(s, a.dtype) for s, a in zip(land_shapes, arrs)]

    def kern(*refs):
        outs = refs[2 * n + n_after:]
        starts, _ = _copies(kind, refs[:n], refs[n:2 * n], shards, axes, outs[0], outs[1], outs[2])
        for cp in starts:
            cp.start()
        outs[-1][...] = jnp.zeros_like(outs[-1])

    kept = [pltpu.HBM(a.shape, a.dtype) for a in arrs] + [pltpu.HBM(s, a.dtype) for s, a in zip(land_shapes, arrs)]
    n_sems = N_PEERS[kind] * n
    res = pl.pallas_call(
        kern, name=name, in_specs=[HBM] * (2 * n) + [ANY] * n_after,
        out_specs=[SEM] * 3 + [HBM] * (2 * n) + [pl.BlockSpec(memory_space=pltpu.VMEM)],
        out_shape=[pltpu.SemaphoreType.DMA((n_sems,)), pltpu.SemaphoreType.DMA((n_sems,)), pltpu.SemaphoreType.DMA((n,))]
        + kept + [jax.ShapeDtypeStruct((SUBLANE, LANE), F32)],
        input_output_aliases={i: 3 + i for i in range(2 * n)},
        compiler_params=pltpu.CompilerParams(has_side_effects=pltpu.SideEffectType.DATAFLOW_SIDE_EFFECTING),
    )(*[pltpu.with_memory_space_constraint(a, pltpu.HBM) for a in list(arrs) + lands], *after)
    return res[:3], res[3:3 + n], res[3 + n:3 + 2 * n], res[-1]


def _exchange_wait(name, kind, started, axes, after, sources_too=False):
    sems, srcs, lands, _ = started
    n, n_after = len(srcs), len(after)
    shards = _shard_shapes(kind, srcs, axes)

    def kern(*refs):
        sem_refs = refs[2 * n:2 * n + 3]
        _, waits = _copies(kind, refs[:n], refs[n:2 * n], shards, axes, *sem_refs)
        for cp in waits:
            cp.wait()

    res = pl.pallas_call(
        kern, name=name, in_specs=[HBM] * (2 * n) + [SEM] * 3 + [ANY] * n_after, out_specs=[HBM] * (2 * n),
        out_shape=[pltpu.HBM(a.shape, a.dtype) for a in list(srcs) + list(lands)],
        input_output_aliases={i: i for i in range(2 * n)},
        compiler_params=pltpu.CompilerParams(has_side_effects=pltpu.SideEffectType.DATAFLOW_SIDE_EFFECTING),
    )(*srcs, *lands, *sems, *after)
    return (res[:n], res[n:]) if sources_too else res[n:]


def _sum_devices(parts):
    def kern(p_ref, o_ref):
        acc = p_ref[0]
        for d in range(1, parts.shape[0]):
            acc = acc + p_ref[d]
        o_ref[...] = acc

    return pl.pallas_call(kern, name="sum_devices", out_shape=jax.ShapeDtypeStruct(parts.shape[1:], F32),
                          compiler_params=pltpu.CompilerParams(vmem_limit_bytes=VMEM_LIMIT_BYTES))(parts)


def _sum_slots(name, parts):
    _, rows, cols = parts.shape
    tr = _pick(rows, (ROW_TILE, 128, 64, 32))

    def kern(p_ref, o_ref):
        o_ref[...] = ((p_ref[3].astype(F32) + p_ref[0].astype(F32)) + p_ref[1].astype(F32)) + p_ref[2].astype(F32)

    return pl.pallas_call(kern, name=name, grid=(rows // tr,),
                          in_specs=[pl.BlockSpec((N_CHIPS, tr, cols), lambda i: (0, i, 0))],
                          out_specs=pl.BlockSpec((tr, cols), lambda i: (i, 0)),
                          out_shape=jax.ShapeDtypeStruct((rows, cols), F32), compiler_params=_params("arbitrary"))(parts)


def _adamw_math(g, w, m, v):
    m2 = ADAM_B1 * m + (1.0 - ADAM_B1) * g
    v2 = ADAM_B2 * v + (1.0 - ADAM_B2) * (g * g)
    m_hat = m2 / (1.0 - ADAM_B1 ** ADAM_STEP)
    v_hat = v2 / (1.0 - ADAM_B2 ** ADAM_STEP)
    return -ADAM_LR * (m_hat / (jnp.sqrt(v_hat) + ADAM_EPS) + ADAM_WD * w), m2, v2


def _adamw(name, parts, w, m, v):
    rows, cols = w.shape
    tr = rows if rows * cols <= WHOLE_ELEMS else _pick(rows, (ROW_TILE, 352, 128, 64, 32, 8))
    n = len(parts)

    def kern(*refs):
        g = refs[0][:, pl.ds(0, cols)]
        for p in refs[1:n]:
            g = g + p[:, pl.ds(0, cols)]
        d, m2, v2 = _adamw_math(g, refs[n][...], refs[n + 1][...], refs[n + 2][...])
        refs[n + 3][...] = g
        refs[n + 4][...] = d
        refs[n + 5][...] = m2
        refs[n + 6][...] = v2

    spec = pl.BlockSpec((tr, cols), lambda i: (i, 0))
    return pl.pallas_call(kern, name=name, grid=(rows // tr,),
                          in_specs=[pl.BlockSpec((tr, p.shape[1]), lambda i: (i, 0)) for p in parts] + [spec] * 3,
                          out_specs=[spec] * 4, out_shape=[jax.ShapeDtypeStruct((rows, cols), F32)] * 4,
                          compiler_params=_params("arbitrary"))(*parts, w, m, v)


def _adamw_many(name, gs, ws, ms, vs):
    n = len(gs)

    def kern(*refs):
        for p in range(n):
            d, m2, v2 = _adamw_math(refs[p][...], refs[n + p][...], refs[2 * n + p][...], refs[3 * n + p][...])
            refs[4 * n + p][...] = d
            refs[5 * n + p][...] = m2
            refs[6 * n + p][...] = v2

    res = pl.pallas_call(kern, name=name, out_shape=[jax.ShapeDtypeStruct(w.shape, F32) for w in ws] * 3,
                         compiler_params=pltpu.CompilerParams(vmem_limit_bytes=VMEM_LIMIT_BYTES))(*gs, *ws, *ms, *vs)
    return res[:n], res[n:2 * n], res[2 * n:]


def _pack(arrs):
    parts, rows = [], []
    for a in arrs:
        r = _round_up(-(-a.size // LANE), SUBLANE)
        parts.append(jnp.pad(a.reshape(-1).astype(F32), (0, r * LANE - a.size)).reshape(r, LANE))
        rows.append(r)
    return jnp.concatenate(parts, axis=0), rows


def _unpack(buf, rows, shapes):
    out, r0 = [], 0
    for r, s in zip(rows, shapes):
        size = math.prod(s)
        out.append(buf[r0:r0 + r].reshape(-1)[:size].reshape(s))
        r0 += r
    return out


def kernel(x, norm_ffn1, ffn1_w1, ffn1_w3, ffn1_w2, norm_mix, w_in, conv_w, conv_b, conv_ln_g, conv_ln_b, conv_out_g, ssm_A_re, ssm_A_im, ssm_log_dt, ssm_B_re, ssm_B_im, ssm_C_re, ssm_C_im, ssm_D, ssm_glu_w, ssm_glu_b, ssm_out_g, w_out, norm_ffn2, ffn2_w1, ffn2_w3, ffn2_w2, norm_final, loss_target, m_norm_ffn1, m_ffn1_w1, m_ffn1_w3, m_ffn1_w2, m_norm_mix, m_w_in, m_conv_w, m_conv_b, m_conv_ln_g, m_conv_ln_b, m_conv_out_g, m_ssm_A_re, m_ssm_A_im, m_ssm_log_dt, m_ssm_B_re, m_ssm_B_im, m_ssm_C_re, m_ssm_C_im, m_ssm_D, m_ssm_glu_w, m_ssm_glu_b, m_ssm_out_g, m_w_out, m_norm_ffn2, m_ffn2_w1, m_ffn2_w3, m_ffn2_w2, m_norm_final, v_norm_ffn1, v_ffn1_w1, v_ffn1_w3, v_ffn1_w2, v_norm_mix, v_w_in, v_conv_w, v_conv_b, v_conv_ln_g, v_conv_ln_b, v_conv_out_g, v_ssm_A_re, v_ssm_A_im, v_ssm_log_dt, v_ssm_B_re, v_ssm_B_im, v_ssm_C_re, v_ssm_C_im, v_ssm_D, v_ssm_glu_w, v_ssm_glu_b, v_ssm_out_g, v_w_out, v_norm_ffn2, v_ffn2_w1, v_ffn2_w3, v_ffn2_w2, v_norm_final):
    given = dict(locals())
    wts = {n: given[n] for n in WEIGHTS}
    n_seq, seq, d = x.shape
    n_rows = n_seq * seq
    xf = x.reshape(n_rows, d)
    tgt = loss_target.reshape(n_rows, d)
    row = lambda a: a.reshape(1, -1)

    f = ffn1_w1.shape[-1]
    fp = _round_up(f, LANE)
    held = lambda n, a: a[0].T if n in TRANSPOSED else a[0]
    shards = []
    for n in BIG:
        s = held(n, wts[n]).astype(BF16)
        if n.startswith('ffn'):
            s = jnp.pad(s, ((0, fp - f), (0, 0)))
        shards.append(s)
    n_taps, c_shard = conv_w.shape[1], conv_w.shape[2]
    shards.append(jnp.pad(conv_w[0], ((0, HALO - n_taps), (0, 0))))
    shard_of = dict(zip(BIG + ['conv_w'], shards))
    axis_of = dict(BIG_AXIS, conv_w=1)
    groups = [['ffn1_w1', 'ffn1_w3'], ['ffn1_w2', 'w_in', 'conv_w', 'ssm_glu_w', 'w_out'], ['ffn2_w1', 'ffn2_w3', 'ffn2_w2']]
    fetch, tok = [], []
    for k, names in enumerate(groups):
        fetch.append(_exchange_start("gather%d_send" % k, "gather", [shard_of[n] for n in names],
                                     [axis_of[n] for n in names], tok))
        tok = [fetch[-1][3]]
    full = {}

    def arrive(k, after):
        lands = _exchange_wait("gather%d_recv" % k, "gather", fetch[k], [axis_of[n] for n in groups[k]], after)
        full.update(zip(groups[k], lands))

    h1, h1_t = _rms_fwd("ffn1_rms", xf, norm_ffn1)
    arrive(0, tok + [h1])

    _, n_grp, n_state = ssm_A_re.shape
    grp = ssm_B_re.shape[-1]
    ns = n_grp * n_state
    c_ssm = n_grp * grp
    lr, li = ssm_A_re.reshape(1, ns), ssm_A_im.reshape(1, ns)
    ldt = jnp.repeat(ssm_log_dt.reshape(n_grp), n_state).reshape(1, ns)
    btr = ssm_B_re[0].transpose(2, 0, 1).reshape(grp, ns)
    bti = ssm_B_im[0].transpose(2, 0, 1).reshape(grp, ns)
    ctr = ssm_C_re[0].transpose(1, 0, 2).reshape(grp, ns)
    cti = ssm_C_im[0].transpose(1, 0, 2).reshape(grp, ns)
    _, _, bbr, bbi, pw, pw_falling = _s5_params_fwd(lr, li, ldt, btr, bti)
    nb = c_ssm // LANE
    sb, gpb = ns // nb, n_grp // nb
    diag = (jnp.arange(LANE)[:, None] // grp) == (jnp.arange(sb)[None, :] // n_state)

    def spread(t):
        return jnp.where(diag, jnp.tile(t.reshape(grp, nb, sb).transpose(1, 0, 2), (1, gpb, 1)), 0.0)

    def gather_diag(t):
        return (t * diag).reshape(nb, gpb, grp, sb).sum(1).transpose(1, 0, 2).reshape(grp, ns)

    def interleave(re, im):
        return jnp.stack([re.reshape(-1, nb, sb), im.reshape(-1, nb, sb)], axis=2).reshape(-1, 2 * ns)

    bdc = jnp.concatenate([spread(bbr), spread(bbi)], axis=2).astype(BF16)
    cdc = jnp.concatenate([spread(ctr).transpose(0, 2, 1), -spread(cti).transpose(0, 2, 1)], axis=1).astype(BF16)
    rowi = jnp.arange(SUBLANE)[:, None]
    pwf, pwc = interleave(pw[:, :ns], pw[:, ns:]), interleave(pw[:, :ns], -pw[:, ns:])
    tabs_f = [jnp.where(rowi >= s, pwf[s - 1][None, :], 0.0) for s in (1, 2, 4)] + [pwf]
    tabs_b = [jnp.where(rowi <= SUBLANE - 1 - s, pwc[s - 1][None, :], 0.0) for s in (1, 2, 4)]
    tabs_b.append(interleave(pw_falling[:, :ns], -pw_falling[:, ns:]))
    c_conv = conv_b.shape[1]
    u_blk = 2 * c_conv // LANE

    a1, b1, z1 = _ffn_up("ffn1_up", h1, full['ffn1_w1'], full['ffn1_w3'])
    arrive(1, [z1])
    x1, h2, h2_t = _mm("ffn1_down", z1, full['ffn1_w2'], 1, 0, addend=xf, alpha=0.5, post=_post_rms(norm_mix))
    saved1 = (h1_t, a1, b1, z1)
    cw = full['conv_w']
    proj = _mm("mix_in", h2, full['w_in'], 1, 0, F32)
    assert c_conv == c_ssm and proj.shape[1] == 3 * c_conv
    cpre, an = _conv_fwd(proj, cw, conv_b, conv_ln_g, conv_ln_b, conv_out_g, seq)
    xs, ypre, yg = _s5_fwd(proj, u_blk, bdc, cdc, tabs_f, ssm_D, seq, sb)
    q0 = _mm("s5_gate", yg, full['ssm_glu_w'], 1, 0, F32)
    sn = _s5_post2(yg, q0, ssm_glu_b, ssm_out_g)
    wo = full['w_out']
    mixed = jnp.concatenate([an, sn], axis=1)
    x2, h3, h3_t = _mm("mix_out", mixed, wo, 1, 0, addend=x1, post=_post_rms(norm_ffn2))
    arrive(2, [x2])
    a3, b3, z3 = _ffn_up("ffn2_up", h3, full['ffn2_w1'], full['ffn2_w3'])
    saved2 = (h3_t, a3, b3, z3)
    dx3, dx3_t, loss_row, d_norm_final = _mm("ffn2_down", z3, full['ffn2_w2'], 1, 0, addend=x2, alpha=0.5,
                                             post=_post_loss(row(norm_final), tgt))

    g = {}
    dx2, g['norm_ffn2'], sent = _ffn_bwd("ffn2", x2, norm_ffn2, full['ffn2_w1'], full['ffn2_w3'], full['ffn2_w2'], saved2,
                                         dx3, dx3_t)
    dmixed = _mm("mix_dmixed", dx2, wo, 1, 1, F32)
    dwo = _mm("mix_dwo", mixed, dx2, 0, 0, BF16)
    dq, dyg1, g['ssm_out_g'], g['ssm_glu_b'] = _s5_post2_bwd(dmixed, yg, q0, ssm_glu_b, ssm_out_g)
    dyg2 = _mm("s5_dgate", dq, full['ssm_glu_w'], 1, 1, F32)
    dwg = _mm("s5_dwg", yg, dq, 0, 0, BF16)
    sent['w_out ssm_glu_w'] = (_exchange_start("mix_wo_wg_send", "scatter", [dwo, dwg], [0, 0]), [0, 0])
    dypre, du_skip, g['ssm_D'] = _s5_post1_bwd(dyg1, dyg2, ypre, proj, ssm_D, after=[sent['w_out ssm_glu_w'][0][3]])
    du, dabar, dbdc, dcdc = _s5_bwd(dypre, du_skip, xs, proj, u_blk, bdc, cdc, tabs_b, seq, sb)
    dabar = dabar.reshape(nb, 2, sb)
    dlr, dli, dldt, dbtr, dbti = _s5_params_bwd(lr, li, ldt, btr, bti, dabar[:, 0].reshape(1, ns), dabar[:, 1].reshape(1, ns),
                                                gather_diag(dbdc[:, :, :sb]), gather_diag(dbdc[:, :, sb:]))
    g['ssm_A_re'], g['ssm_A_im'] = dlr, dli
    g['ssm_log_dt'] = dldt.reshape(n_grp, n_state).sum(axis=1)
    g['ssm_B_re'] = dbtr.reshape(grp, n_grp, n_state).transpose(1, 2, 0)
    g['ssm_B_im'] = dbti.reshape(grp, n_grp, n_state).transpose(1, 2, 0)
    g['ssm_C_re'] = gather_diag(dcdc[:, :sb].transpose(0, 2, 1)).reshape(grp, n_grp, n_state).transpose(1, 0, 2)
    g['ssm_C_im'] = -gather_diag(dcdc[:, sb:].transpose(0, 2, 1)).reshape(grp, n_grp, n_state).transpose(1, 0, 2)
    dc, g['conv_out_g'], g['conv_ln_g'], g['conv_ln_b'], g['conv_b'] = _conv_bwd_rows(dmixed, cpre, conv_ln_g, conv_ln_b,
                                                                                    conv_out_g)
    dval, dgate, dcw = _conv_bwd_taps(proj, dc, cw, seq)
    dproj = jnp.concatenate([dval, dgate, du], axis=1)
    sent['w_in'] = (_exchange_start("mix_win_send", "scatter", [_mm("mix_dwin", h2_t, dproj, 1, 0, BF16)], [1]), [1])
    dx1, dx1_t, g['norm_mix'] = _mm("mix_dh", dproj, full['w_in'], 1, 1, after=[sent['w_in'][0][3]],
                                    post=_post_rms_bwd(x1, norm_mix, dx2))
    dx0, g['norm_ffn1'], sent1 = _ffn_bwd("ffn1", xf, norm_ffn1, full['ffn1_w1'], full['ffn1_w3'], full['ffn1_w2'], saved1,
                                          dx1, dx1_t)
    sent.update(sent1)
    g['norm_final'] = d_norm_final
    g['conv_w'] = dcw[:n_taps]

    small_shapes = [(n_taps, c_conv) if n == 'conv_w' else wts[n].shape for n in SMALL]
    buf, buf_rows = _pack([g[n] for n in SMALL] + [loss_row])
    to_all = _exchange_start("small_send", "all", [buf], [0])
    slots = {}
    for names, (started, axes) in sent.items():
        lands = _exchange_wait(names.replace(' ', '_') + "_recv", "scatter", started, axes, after=[dx0, to_all[3]])
        slots.update(zip(names.split(), lands))
    sums = [_sum_slots("sum_" + n, slots[n]) for n in BIG]
    to_sibling = _exchange_start("sums_send", "sibling", sums, [0] * len(sums))
    from_all = _exchange_wait("small_recv", "all", to_all, [0], after=[to_sibling[3]])[0]
    total = _unpack(_sum_devices(from_all), buf_rows, small_shapes + [(1, LANE)])
    loss = total[-1][0, 0]
    grads = dict(zip(SMALL, total[:-1]))
    chip = 2 * lax.axis_index("x") + lax.axis_index("y")
    grads['conv_w'] = lax.dynamic_slice_in_dim(grads['conv_w'], chip * c_shard, c_shard, axis=1)[None]
    flat = lambda a: a.reshape(-1, a.shape[-1])
    small = _adamw_many("adamw_small", *[[flat(src[p + n]) for n in SMALL]
                                         for src, p in ((grads, ''), (given, ''), (given, 'm_'), (given, 'v_'))])
    deltas, new_m, new_v = ({n: o.reshape(wts[n].shape) for n, o in zip(SMALL, outs)} for outs in small)

    sums, theirs = _exchange_wait("sums_recv", "sibling", to_sibling, [0] * len(sums), after=[new_v[SMALL[-1]]],
                                  sources_too=True)
    for n, mine, other in zip(BIG, sums, theirs):
        grads[n], deltas[n], new_m[n], new_v[n] = (
            (o.T if n in TRANSPOSED else o)[None]
            for o in _adamw("adamw_" + n, [mine, other], held(n, given[n]), held(n, given['m_' + n]), held(n, given['v_' + n])))

    return (loss, dx0.reshape(x.shape), *[grads[n] for n in WEIGHTS], *[deltas[n] for n in WEIGHTS],
            *[new_m[n] for n in WEIGHTS], *[new_v[n] for n in WEIGHTS])
```

```python
import math
from typing import Callable, NamedTuple

import jax
import jax.numpy as jnp
from jax import lax
from jax.experimental import pallas as pl
from jax.experimental.pallas import tpu as pltpu

F32 = jnp.float32
BF16 = jnp.bfloat16
EPS = 1e-6
ADAM_LR, ADAM_B1, ADAM_B2, ADAM_EPS, ADAM_WD, ADAM_STEP = 0.001, 0.9, 0.999, 1e-08, 0.01, 10
MESH = pl.DeviceIdType.MESH
ANY = pl.BlockSpec(memory_space=pl.ANY)
LANE = 128
SUBLANE = 8
VMEM_LIMIT_BYTES = 56 << 20
ROW_TILE = 256
ROW_TILE_ELEMS = 256 * 1024
WHOLE_ELEMS = 512 * 1024
WHOLE_WEIGHT_BYTES = 8 << 20
FFN_ROWS = 256
CONV_TILE = 128
CONV_SUB = 32
HALO = 32
SCAN_TILE = 256
SCAN_COLS = 512
N_CHIPS = 4
CHIP_RELS = ((1, 0), (0, 1), (1, 1))
NT = (((1,), (1,)), ((), ()))
GELU_K = math.sqrt(2.0 / math.pi)
GELU_C = 0.044715

WEIGHTS = ['norm_ffn1', 'ffn1_w1', 'ffn1_w3', 'ffn1_w2', 'norm_mix', 'w_in', 'conv_w', 'conv_b', 'conv_ln_g', 'conv_ln_b',
           'conv_out_g', 'ssm_A_re', 'ssm_A_im', 'ssm_log_dt', 'ssm_B_re', 'ssm_B_im', 'ssm_C_re', 'ssm_C_im', 'ssm_D',
           'ssm_glu_w', 'ssm_glu_b', 'ssm_out_g', 'w_out', 'norm_ffn2', 'ffn2_w1', 'ffn2_w3', 'ffn2_w2', 'norm_final']
BIG = ['ffn1_w1', 'ffn1_w3', 'ffn1_w2', 'w_in', 'ssm_glu_w', 'w_out', 'ffn2_w1', 'ffn2_w3', 'ffn2_w2']
BIG_AXIS = {'ffn1_w1': 0, 'ffn1_w3': 0, 'ffn1_w2': 0, 'w_in': 1, 'ssm_glu_w': 0, 'w_out': 0, 'ffn2_w1': 0, 'ffn2_w3': 0,
            'ffn2_w2': 0}
TRANSPOSED = ('ffn1_w1', 'ffn1_w3', 'ffn2_w1', 'ffn2_w3')
SMALL = [n for n in WEIGHTS if n not in BIG]


def _round_up(n, m):
    return -(-n // m) * m


def _pick(n, cands):
    for c in cands:
        if c <= n and n % c == 0:
            return c
    return n


def _params(*sem):
    return pltpu.CompilerParams(dimension_semantics=sem, vmem_limit_bytes=VMEM_LIMIT_BYTES)


def _rms_r(x):
    return lax.rsqrt(jnp.mean(x * x, axis=-1, keepdims=True) + EPS)


def _rms_bwd(x, r, g, dy):
    dyg = dy * g
    return r * dyg - x * (r * r * r) * jnp.mean(x * dyg, axis=-1, keepdims=True)


def _sigmoid(x):
    return jax.nn.sigmoid(x)


def _dsilu(a, s):
    return s * (1.0 + a * (1.0 - s))


def _gelu(x):
    return 0.5 * x * (1.0 + jnp.tanh(GELU_K * (x + GELU_C * x * x * x)))


def _dgelu(x):
    t = jnp.tanh(GELU_K * (x + GELU_C * x * x * x))
    return 0.5 * (1.0 + t) + 0.5 * x * (1.0 - t * t) * GELU_K * (1.0 + 3.0 * GELU_C * x * x)


def _colsum(v):
    return jnp.sum(v, axis=0, keepdims=True)


def _rowwise(name, body, n_rows, row_ins, par_ins, row_outs, acc_outs, after=()):
    widest = max([w for (_, w, _) in row_ins] + [w for (w, _) in row_outs])
    tt = _pick(n_rows, [t for t in (256, 128, 64, 32, 16, 8) if t * widest <= ROW_TILE_ELEMS])
    in_specs = [pl.BlockSpec((tt, w), lambda i, cb=cb: (i, cb)) for (_, w, cb) in row_ins]
    in_specs += [pl.BlockSpec(p.shape, lambda i: (0, 0)) for p in par_ins] + [ANY] * len(after)
    out_specs = [pl.BlockSpec((tt, w), lambda i: (i, 0)) for (w, _) in row_outs]
    out_specs += [pl.BlockSpec((r, w), lambda i: (0, 0)) for (r, w) in acc_outs]
    out_shape = [jax.ShapeDtypeStruct((n_rows, w), dt) for (w, dt) in row_outs]
    out_shape += [jax.ShapeDtypeStruct((r, w), F32) for (r, w) in acc_outs]
    n_in, n_ro = len(row_ins) + len(par_ins), len(row_outs)
    o0 = n_in + len(after)

    def kern(*refs):
        accs = refs[o0 + n_ro:]
        if accs:
            @pl.when(pl.program_id(0) == 0)
            def _():
                for a in accs:
                    a[...] = jnp.zeros_like(a)
        body(refs[:n_in], refs[o0:o0 + n_ro], accs)

    return pl.pallas_call(kern, name=name, grid=(n_rows // tt,), in_specs=in_specs, out_specs=out_specs, out_shape=out_shape,
                          compiler_params=_params("arbitrary"))(*[a for a, _, _ in row_ins], *par_ins, *after)


class Post(NamedTuple):
    rows: list
    gains: list
    outs: list
    t_outs: list
    sums: list
    fn: Callable


def _post_rms(gain):
    def fn(r, rows, gains):
        h = r * _rms_r(r) * gains[0]
        return [r, h, h], []

    return Post([], [gain], [F32, BF16], [BF16], [], fn)


def _post_rms_bwd(x, gain, dres):
    def fn(dh, rows, gains):
        r = _rms_r(rows[0])
        dx = rows[1] + _rms_bwd(rows[0], r, gains[0], dh)
        return [dx, dx], [_colsum(dh * rows[0] * r)]

    return Post([x, dres], [gain], [F32], [BF16], [x.shape[1]], fn)


def _post_loss(gain, tgt):
    d = tgt.shape[1]

    def fn(xv, rows, gains):
        r = _rms_r(xv)
        e = xv * r * gains[0] - rows[0]
        sq = jnp.sum(jnp.sum(e * e, axis=-1, keepdims=True), axis=0, keepdims=True)
        dy = e * (1.0 / d)
        dx = _rms_bwd(xv, r, gains[0], dy)
        return [dx, dx], [jnp.broadcast_to(sq * (0.5 / d), (1, LANE)), _colsum(dy * xv * r)]

    return Post([tgt], [gain], [F32], [BF16], [LANE, d], fn)


def _mm(name, a, b, ca, cb, out_dtype=F32, addend=None, alpha=1.0, a_cols=None, after=(), post=None, transposed=False):
    a_start, a_width = a_cols if a_cols else (0, a.shape[1])
    m, k = (a.shape[0], a_width) if ca == 1 else (a_width, a.shape[0])
    n = b.shape[1 - cb]
    assert b.shape[cb] == k, (name, a.shape, b.shape)
    tn = _pick(n, (1024, 768, 512, 384, 256, 128))
    if post and k * tn * b.dtype.itemsize <= WHOLE_WEIGHT_BYTES:
        tk = k
        tm = _pick(m, (256, 128) if k > 1024 else (512, 256, 128))
    else:
        tm = _pick(m, (512, 256, 128) if post else (1024, 512, 256, 128))
        tk = _pick(k, (2048, 1024, 768, 512, 256, 128) if k >= 4096 and not post else (1024, 768, 512, 256, 128))
    nk = k // tk
    if ca == 1:
        assert a_start % tk == 0
        a_spec = pl.BlockSpec((tm, tk), lambda i, j, kk: (i, kk + a_start // tk))
    else:
        assert a_start % tm == 0
        a_spec = pl.BlockSpec((tk, tm), lambda i, j, kk: (kk, i + a_start // tm))
    b_spec = pl.BlockSpec((tk, tn), lambda i, j, kk: (kk, j)) if cb == 0 else pl.BlockSpec((tn, tk), lambda i, j, kk: (j, kk))
    o_spec = pl.BlockSpec((tm, tn), lambda i, j, kk: (i, j))
    t_spec = pl.BlockSpec((tn, tm), lambda i, j, kk: (j, i))
    fixed = lambda w: pl.BlockSpec((1, w), lambda i, j, kk: (0, 0))
    ins, in_specs = [a, b], [a_spec, b_spec]
    if addend is not None:
        ins.append(addend)
        in_specs.append(o_spec)
    n_plain = len(ins)
    n_rows, n_gains = (len(post.rows), len(post.gains)) if post else (0, 0)
    if post:
        assert tn == n, name
        ins += post.rows + post.gains
        in_specs += [o_spec] * n_rows + [fixed(n)] * n_gains
    ins += list(after)
    in_specs += [ANY] * len(after)
    n_in = len(ins)
    if post:
        n_straight, n_vals = len(post.outs), len(post.outs) + len(post.t_outs)
        out_specs = [o_spec] * n_straight + [t_spec] * len(post.t_outs) + [fixed(w) for w in post.sums]
        out_shape = [jax.ShapeDtypeStruct((m, n), dt) for dt in post.outs] + [jax.ShapeDtypeStruct((n, m), dt) for dt in post.t_outs]
        out_shape += [jax.ShapeDtypeStruct((1, w), F32) for w in post.sums]
    elif transposed:
        out_specs, out_shape = [t_spec], [jax.ShapeDtypeStruct((n, m), out_dtype)]
    else:
        out_specs, out_shape = [o_spec], [jax.ShapeDtypeStruct((m, n), out_dtype)]
    n_out = len(out_specs)
    dims = (((ca,), (cb,)), ((), ()))

    def emit(refs, r):
        if alpha != 1.0:
            r = r * alpha
        if addend is not None:
            r = r + refs[2][...].astype(F32)
        outs = refs[n_in:n_in + n_out]
        if post is None:
            outs[0][...] = (r.T if transposed else r).astype(out_dtype)
            return
        vals, incs = post.fn(r, [q[...] for q in refs[n_plain:n_plain + n_rows]],
                             [q[...] for q in refs[n_plain + n_rows:n_plain + n_rows + n_gains]])
        for at, (o_ref, val) in enumerate(zip(outs, vals)):
            o_ref[...] = (val if at < n_straight else val.T).astype(o_ref.dtype)
        for s_ref, inc in zip(outs[n_vals:], incs):
            s_ref[...] += inc

    def kern(*refs):
        kk = pl.program_id(2)
        if post and post.sums:
            @pl.when(jnp.logical_and(jnp.logical_and(pl.program_id(0) == 0, pl.program_id(1) == 0), kk == 0))
            def _():
                for s_ref in refs[n_in + n_vals:n_in + n_out]:
                    s_ref[...] = jnp.zeros_like(s_ref)

        dot = lambda: lax.dot_general(refs[0][...].astype(BF16), refs[1][...].astype(BF16), dims,
                                      preferred_element_type=F32)
        if nk == 1:
            emit(refs, dot())
            return
        acc_ref = refs[-1]

        @pl.when(kk == 0)
        def _():
            acc_ref[...] = jnp.zeros_like(acc_ref)

        acc_ref[...] += dot()

        @pl.when(kk == nk - 1)
        def _():
            emit(refs, acc_ref[...])

    res = pl.pallas_call(kern, name=name, grid=(m // tm, n // tn, nk), in_specs=in_specs, out_specs=out_specs,
                         out_shape=out_shape, scratch_shapes=[] if nk == 1 else [pltpu.VMEM((tm, tn), F32)],
                         compiler_params=_params("arbitrary", "arbitrary", "arbitrary"))(*ins)
    return res if post else res[0]


def _rms_fwd(name, x, g):
    t, d = x.shape
    tt = _pick(t, (ROW_TILE, LANE))

    def kern(x_ref, g_ref, h_ref, ht_ref):
        xv = x_ref[...]
        h = xv * _rms_r(xv) * g_ref[...]
        h_ref[...] = h.astype(BF16)
        ht_ref[...] = h.T.astype(BF16)

    return pl.pallas_call(kern, name=name, grid=(t // tt,),
                          in_specs=[pl.BlockSpec((tt, d), lambda i: (i, 0)), pl.BlockSpec((1, d), lambda i: (0, 0))],
                          out_specs=[pl.BlockSpec((tt, d), lambda i: (i, 0)), pl.BlockSpec((d, tt), lambda i: (0, i))],
                          out_shape=[jax.ShapeDtypeStruct((t, d), BF16), jax.ShapeDtypeStruct((d, t), BF16)],
                          compiler_params=_params("arbitrary"))(x, g)


def _ffn_up(name, h, w1, w3):
    t, d = h.shape
    ff = w1.shape[0]
    tm, tn = _pick(t, (1024, 512, 256, 128)), _pick(ff, (1024, 768, 512, 256, 128))

    def kern(h_ref, w1_ref, w3_ref, a_ref, b_ref, z_ref):
        hv = h_ref[...]
        a = lax.dot_general(hv, w1_ref[...], NT, preferred_element_type=F32)
        b = lax.dot_general(hv, w3_ref[...], NT, preferred_element_type=F32)
        a_ref[...] = a.astype(BF16)
        b_ref[...] = b.astype(BF16)
        z_ref[...] = (a * _sigmoid(a) * b).astype(BF16)

    w_spec = pl.BlockSpec((tn, d), lambda i, j: (j, 0))
    o_spec = pl.BlockSpec((tm, tn), lambda i, j: (i, j))
    return pl.pallas_call(kern, name=name, grid=(t // tm, ff // tn),
                          in_specs=[pl.BlockSpec((tm, d), lambda i, j: (i, 0)), w_spec, w_spec], out_specs=[o_spec] * 3,
                          out_shape=[jax.ShapeDtypeStruct((t, ff), BF16)] * 3,
                          compiler_params=_params("arbitrary", "arbitrary"))(h, w1, w3)


def _ffn_dglu(name, dxo, w2, a, b, after=()):
    t, d = dxo.shape
    ff = w2.shape[0]
    tm = _pick(t, (FFN_ROWS, 128))

    def kern(dx_ref, w2_ref, a_ref, b_ref, *rest):
        da_ref, db_ref = rest[-2:]
        dz = lax.dot_general(dx_ref[...].astype(BF16), w2_ref[...], NT, preferred_element_type=F32) * 0.5
        av, bv = a_ref[...].astype(F32), b_ref[...].astype(F32)
        s = _sigmoid(av)
        da_ref[...] = (dz * bv * _dsilu(av, s)).astype(BF16)
        db_ref[...] = (dz * av * s).astype(BF16)

    o_spec = pl.BlockSpec((tm, ff), lambda i: (i, 0))
    return pl.pallas_call(kern, name=name, grid=(t // tm,),
                          in_specs=[pl.BlockSpec((tm, d), lambda i: (i, 0)), pl.BlockSpec((ff, d), lambda i: (0, 0)),
                                    o_spec, o_spec] + [ANY] * len(after),
                          out_specs=[o_spec] * 2, out_shape=[jax.ShapeDtypeStruct((t, ff), BF16)] * 2,
                          compiler_params=_params("arbitrary"))(dxo, w2, a, b, *after)


def _ffn_dh(name, da, db, w1, w3, x, g, dres, after=()):
    t, d = x.shape
    ff = da.shape[1]
    tm = _pick(t, (FFN_ROWS, 128))

    def kern(da_ref, db_ref, w1_ref, w3_ref, x_ref, g_ref, dres_ref, *rest):
        dx_ref, dg_ref = rest[-2:]

        @pl.when(pl.program_id(0) == 0)
        def _():
            dg_ref[...] = jnp.zeros_like(dg_ref)

        dh = (jnp.dot(da_ref[...], w1_ref[...], preferred_element_type=F32)
              + jnp.dot(db_ref[...], w3_ref[...], preferred_element_type=F32))
        xv = x_ref[...]
        r = _rms_r(xv)
        dx_ref[...] = dres_ref[...] + _rms_bwd(xv, r, g_ref[...], dh)
        dg_ref[...] += _colsum(dh * xv * r)

    act = pl.BlockSpec((tm, ff), lambda i: (i, 0))
    wgt = pl.BlockSpec((ff, d), lambda i: (0, 0))
    rows = pl.BlockSpec((tm, d), lambda i: (i, 0))
    gain = pl.BlockSpec((1, d), lambda i: (0, 0))
    return pl.pallas_call(kern, name=name, grid=(t // tm,),
                          in_specs=[act, act, wgt, wgt, rows, gain, rows] + [ANY] * len(after), out_specs=[rows, gain],
                          out_shape=[jax.ShapeDtypeStruct((t, d), F32), jax.ShapeDtypeStruct((1, d), F32)],
                          compiler_params=_params("arbitrary"))(da, db, w1, w3, x, g, dres, *after)


def _ffn_bwd(tag, x, g, w1, w3, w2, saved, dxo, dxo_t):
    ht, a, b, z = saved
    dw2 = _mm(tag + "_dw2", dxo_t, z, 1, 0, BF16, alpha=0.5, transposed=True)
    s2 = _exchange_start(tag + "_w2_send", "scatter", [dw2], [0])
    da, db = _ffn_dglu(tag + "_dglu", dxo, w2, a, b, after=[s2[3]])
    dw1 = _mm(tag + "_dw1", ht, da, 1, 0, BF16, transposed=True)
    s1 = _exchange_start(tag + "_w1_send", "scatter", [dw1], [0])
    dw3 = _mm(tag + "_dw3", ht, db, 1, 0, BF16, after=[s1[3]], transposed=True)
    s3 = _exchange_start(tag + "_w3_send", "scatter", [dw3], [0])
    dx, dg = _ffn_dh(tag + "_dh", da, db, w1, w3, x, g, dxo, after=[s3[3]])
    return dx, dg, {tag + "_w1": (s1, [0]), tag + "_w3": (s3, [0]), tag + "_w2": (s2, [0])}


def _shift_copies(ext_ref, sh_ref):
    n = ext_ref.shape[0] - SUBLANE
    for r in range(1, SUBLANE):
        sh_ref[r, pl.ds(0, n), :] = ext_ref[pl.ds(r, n), :]


def _rows_at(ext_ref, sh_ref, off, rows):
    r = off % SUBLANE
    return ext_ref[pl.ds(off, rows), :] if r == 0 else sh_ref[r, pl.ds(off - r, rows), :]


def _conv_fwd(proj, cw, cb, lng, lnb, og, seq):
    n_rows, c = proj.shape[0], cb.shape[1]
    kw = HALO - 1
    tt = _pick(seq, (CONV_TILE,))
    hb = tt // HALO

    def kern(v_ref, g_ref, vp_ref, gp_ref, w_ref, cb_ref, lg_ref, lb_ref, og_ref, c_ref, an_ref, ext_ref, sh_ref):
        first = (pl.program_id(0) * tt) % seq == 0
        ext_ref[pl.ds(HALO, tt), :] = v_ref[...] * _sigmoid(g_ref[...])
        ext_ref[pl.ds(0, HALO), :] = vp_ref[...] * _sigmoid(gp_ref[...]) * jnp.where(first, 0.0, 1.0)
        _shift_copies(ext_ref, sh_ref)
        for r0 in range(0, tt, CONV_SUB):
            rows = min(CONV_SUB, tt - r0)
            acc = jnp.zeros((rows, c), F32)
            for k in range(kw):
                acc = acc + w_ref[pl.ds(k, 1), :] * _rows_at(ext_ref, sh_ref, r0 + HALO - (kw - 1) + k, rows)
            c_ref[pl.ds(r0, rows), :] = acc + cb_ref[...]
        cv = c_ref[...]
        mu = jnp.mean(cv, axis=-1, keepdims=True)
        xc = cv - mu
        rstd = lax.rsqrt(jnp.mean(xc * xc, axis=-1, keepdims=True) + EPS)
        lv = xc * rstd * lg_ref[...] + lb_ref[...]
        sl = lv * _sigmoid(lv)
        an_ref[...] = (sl * _rms_r(sl) * og_ref[...]).astype(BF16)

    cur = lambda cbk: pl.BlockSpec((tt, c), lambda i: (i, cbk))
    prev = lambda cbk: pl.BlockSpec((HALO, c), lambda i: (jnp.maximum(i * hb - 1, 0), cbk))
    par = lambda p: pl.BlockSpec(p.shape, lambda i: (0, 0))
    return pl.pallas_call(
        kern, name="conv_fwd", grid=(n_rows // tt,),
        in_specs=[cur(0), cur(1), prev(0), prev(1), par(cw), par(cb), par(lng), par(lnb), par(og)],
        out_specs=[pl.BlockSpec((tt, c), lambda i: (i, 0))] * 2,
        out_shape=[jax.ShapeDtypeStruct((n_rows, c), F32), jax.ShapeDtypeStruct((n_rows, c), BF16)],
        scratch_shapes=[pltpu.VMEM((tt + HALO, c), F32), pltpu.VMEM((SUBLANE, tt + HALO, c), F32)],
        compiler_params=_params("arbitrary"),
    )(proj, proj, proj, proj, cw, cb, lng, lnb, og)


def _conv_bwd_rows(dmixed, cpre, lng, lnb, og):
    c = cpre.shape[1]

    def body(ins, outs, accs):
        dan, cv, lg, lb, ogv = ins[0][...], ins[1][...], ins[2][...], ins[3][...], ins[4][...]
        mu = jnp.mean(cv, axis=-1, keepdims=True)
        xc = cv - mu
        rstd = lax.rsqrt(jnp.mean(xc * xc, axis=-1, keepdims=True) + EPS)
        xh = xc * rstd
        lv = xh * lg + lb
        s = _sigmoid(lv)
        sl = lv * s
        r2 = _rms_r(sl)
        accs[0][...] += _colsum(dan * sl * r2)
        dl = _rms_bwd(sl, r2, ogv, dan) * _dsilu(lv, s)
        accs[1][...] += _colsum(dl * xh)
        accs[2][...] += _colsum(dl)
        dxh = dl * lg
        dc = rstd * (dxh - jnp.mean(dxh, axis=-1, keepdims=True) - xh * jnp.mean(dxh * xh, axis=-1, keepdims=True))
        outs[0][...] = dc
        accs[3][...] += _colsum(dc)

    return _rowwise("conv_bwd_rows", body, cpre.shape[0], [(dmixed, c, 0), (cpre, c, 0)], [lng, lnb, og], [(c, F32)],
                    [(1, c)] * 4)


def _conv_bwd_taps(proj, dc, cw, seq):
    n_rows, c = dc.shape
    kw = HALO - 1
    tt = _pick(seq, (CONV_TILE,))
    hb = tt // HALO
    last_blk = n_rows // HALO - 1

    def kern(v_ref, g_ref, vp_ref, gp_ref, dc_ref, dn_ref, w_ref, dv_ref, dg_ref, dw_ref, exta_ref, extd_ref, sha_ref, shd_ref):
        i = pl.program_id(0)
        first = (i * tt) % seq == 0
        last = ((i + 1) * tt) % seq == 0

        @pl.when(i == 0)
        def _():
            dw_ref[...] = jnp.zeros_like(dw_ref)

        sg = _sigmoid(g_ref[...])
        exta_ref[pl.ds(HALO, tt), :] = v_ref[...] * sg
        exta_ref[pl.ds(0, HALO), :] = vp_ref[...] * _sigmoid(gp_ref[...]) * jnp.where(first, 0.0, 1.0)
        dcv = dc_ref[...]
        extd_ref[pl.ds(0, tt), :] = dcv
        extd_ref[pl.ds(tt, HALO), :] = dn_ref[...] * jnp.where(last, 0.0, 1.0)
        _shift_copies(exta_ref, sha_ref)
        _shift_copies(extd_ref, shd_ref)
        for k in range(kw):
            dw_ref[pl.ds(k, 1), :] += _colsum(_rows_at(exta_ref, sha_ref, HALO - (kw - 1) + k, tt) * dcv)
        for r0 in range(0, tt, CONV_SUB):
            rows = min(CONV_SUB, tt - r0)
            acc = jnp.zeros((rows, c), F32)
            for k in range(kw):
                acc = acc + w_ref[pl.ds(k, 1), :] * _rows_at(extd_ref, shd_ref, r0 + (kw - 1) - k, rows)
            dv_ref[pl.ds(r0, rows), :] = acc
        da = dv_ref[...]
        dv_ref[...] = da * sg
        dg_ref[...] = da * v_ref[...] * sg * (1.0 - sg)

    cur = lambda cbk: pl.BlockSpec((tt, c), lambda i: (i, cbk))
    prev = lambda cbk: pl.BlockSpec((HALO, c), lambda i: (jnp.maximum(i * hb - 1, 0), cbk))
    nxt = pl.BlockSpec((HALO, c), lambda i: (jnp.minimum((i + 1) * hb, last_blk), 0))
    return pl.pallas_call(
        kern, name="conv_bwd_taps", grid=(n_rows // tt,),
        in_specs=[cur(0), cur(1), prev(0), prev(1), cur(0), nxt, pl.BlockSpec(cw.shape, lambda i: (0, 0))],
        out_specs=[cur(0), cur(0), pl.BlockSpec((HALO, c), lambda i: (0, 0))],
        out_shape=[jax.ShapeDtypeStruct((n_rows, c), F32), jax.ShapeDtypeStruct((n_rows, c), F32),
                   jax.ShapeDtypeStruct((HALO, c), F32)],
        scratch_shapes=[pltpu.VMEM((tt + HALO, c), F32)] * 2 + [pltpu.VMEM((SUBLANE, tt + HALO, c), F32)] * 2,
        compiler_params=_params("arbitrary"),
    )(proj, proj, proj, proj, dc, dc, cw)


def _s5_params_fwd(lr, li, ldt, btr, bti, seg):
    ns = lr.shape[1]

    def kern(lr_ref, li_ref, ldt_ref, btr_ref, bti_ref, ar_ref, ai_ref, bbr_ref, bbi_ref, ps_ref, psf_ref, pc_ref, pcf_ref):
        lrv, liv = lr_ref[...], li_ref[...]
        dt = jnp.exp(ldt_ref[...])
        zr, zi = lrv * dt, liv * dt
        mag = jnp.exp(zr)
        ar, ai = mag * jnp.cos(zi), mag * jnp.sin(zi)
        den = lrv * lrv + liv * liv
        nr = ar - 1.0
        cr = (nr * lrv + ai * liv) / den
        ci = (ai * lrv - nr * liv) / den
        ar_ref[...] = ar
        ai_ref[...] = ai
        bbr_ref[...] = cr * btr_ref[...] - ci * bti_ref[...]
        bbi_ref[...] = cr * bti_ref[...] + ci * btr_ref[...]
        def powers(br, bi, count, up_ref, down_ref):
            pr, pi = br, bi
            for e in range(count):
                for ref, at in ((up_ref, e), (down_ref, count - 1 - e)):
                    ref[pl.ds(at, 1), pl.ds(0, ns)] = pr
                    ref[pl.ds(at, 1), pl.ds(ns, ns)] = pi
                if e < count - 1:
                    pr, pi = pr * br - pi * bi, pr * bi + pi * br
            return pr, pi

        powers(*powers(ar, ai, seg, ps_ref, psf_ref), SUBLANE, pc_ref, pcf_ref)

    h = btr.shape[0]
    shapes = [jax.ShapeDtypeStruct((1, ns), F32)] * 2 + [jax.ShapeDtypeStruct((h, ns), F32)] * 2
    shapes += [jax.ShapeDtypeStruct((seg, 2 * ns), F32)] * 2 + [jax.ShapeDtypeStruct((SUBLANE, 2 * ns), F32)] * 2
    return pl.pallas_call(kern, name="s5_params_fwd", out_shape=shapes)(lr, li, ldt, btr, bti)


def _s5_params_bwd(lr, li, ldt, btr, bti, dar, dai, dbbr, dbbi):
    def kern(lr_ref, li_ref, ldt_ref, btr_ref, bti_ref, dar_ref, dai_ref, dbr_ref, dbi_ref,
             dlr_ref, dli_ref, dldt_ref, dbtr_ref, dbti_ref):
        lrv, liv = lr_ref[...], li_ref[...]
        dt = jnp.exp(ldt_ref[...])
        zr, zi = lrv * dt, liv * dt
        mag = jnp.exp(zr)
        ar, ai = mag * jnp.cos(zi), mag * jnp.sin(zi)
        den = lrv * lrv + liv * liv
        nr = ar - 1.0
        cr = (nr * lrv + ai * liv) / den
        ci = (ai * lrv - nr * liv) / den
        dbr, dbi, br, bi = dbr_ref[...], dbi_ref[...], btr_ref[...], bti_ref[...]
        dbtr_ref[...] = cr * dbr + ci * dbi
        dbti_ref[...] = cr * dbi - ci * dbr
        dcr = _colsum(br * dbr + bi * dbi)
        dci = _colsum(br * dbi - bi * dbr)
        ir, ii = lrv / den, -liv / den
        dnr = ir * dcr + ii * dci
        dni = ir * dci - ii * dcr
        wr, wi = cr * ir - ci * ii, cr * ii + ci * ir
        dl1r = -(wr * dcr + wi * dci)
        dl1i = -(wr * dci - wi * dcr)
        dtr, dti = dar_ref[...] + dnr, dai_ref[...] + dni
        dzr = ar * dtr + ai * dti
        dzi = ar * dti - ai * dtr
        dlr_ref[...] = dl1r + dt * dzr
        dli_ref[...] = dl1i + dt * dzi
        dldt_ref[...] = (dzr * lrv + dzi * liv) * dt

    ns, h = lr.shape[1], btr.shape[0]
    shapes = [jax.ShapeDtypeStruct((1, ns), F32)] * 3 + [jax.ShapeDtypeStruct((h, ns), F32)] * 2
    return pl.pallas_call(kern, name="s5_params_bwd", out_shape=shapes)(lr, li, ldt, btr, bti, dar, dai, dbbr, dbbi)


def _to_segments(nat_ref, seg_ref):
    steps = nat_ref.shape[0] // SUBLANE
    _regroup(nat_ref, seg_ref, lambda r: (r % SUBLANE) * steps + r // SUBLANE)


def _from_segments(seg_ref, nat_ref):
    steps = nat_ref.shape[0] // SUBLANE
    _regroup(seg_ref, nat_ref, lambda r: (r % steps) * SUBLANE + r // steps)


def _regroup(src_ref, dst_ref, src_row):
    rows, width = dst_ref.shape
    sublane = lax.broadcasted_iota(jnp.int32, (SUBLANE, width), 0)
    for r0 in range(0, rows, SUBLANE):
        tile = jnp.broadcast_to(src_ref[pl.ds(src_row(r0), 1), :], (SUBLANE, width))
        for k in range(1, SUBLANE):
            tile = jnp.where(sublane == k, src_ref[pl.ds(src_row(r0 + k), 1), :], tile)
        dst_ref[pl.ds(r0, SUBLANE), :] = tile


def _scan_tile(s_ref, o_ref, fix_ref, tabs, car_ref, sb, reverse, x_ref=None, acc_ref=None):
    l1, l2, l4, pw = tabs
    rows_t, w = s_ref.shape
    steps = rows_t // SUBLANE
    cw = _pick(sb, (SCAN_COLS,))
    last = 0 if reverse else SUBLANE - 1
    first = SUBLANE - 1 - last
    row = lax.broadcasted_iota(jnp.int32, (SUBLANE, cw), 0)
    step_rows = lambda i: pl.ds(pl.multiple_of(((steps - 1 - i) if reverse else i) * SUBLANE, SUBLANE), SUBLANE)
    zero = jnp.zeros((SUBLANE, cw), F32)

    for c0 in [b0 + o for b0 in range(0, w, 2 * sb) for o in range(0, sb, cw)]:
        cr, ci = pl.ds(c0, cw), pl.ds(c0 + sb, cw)
        base = pl.ds(((steps - 1) if reverse else 0) * SUBLANE, SUBLANE)
        ar, ai = fix_ref[base, cr], fix_ref[base, ci]

        def run(i, state):
            xr, xi = state
            rows = step_rows(i)
            xr, xi = ar * xr - ai * xi + s_ref[rows, cr], ar * xi + ai * xr + s_ref[rows, ci]
            o_ref[rows, cr] = xr
            o_ref[rows, ci] = xi
            return xr, xi

        fr, fi = lax.fori_loop(0, steps, run, (zero, zero))
        for s, lt in ((1, l1), (2, l2), (4, l4)):
            sh = (SUBLANE - s) if reverse else s
            sr, si = pltpu.roll(fr, sh, 0), pltpu.roll(fi, sh, 0)
            tr, ti = lt[:, cr], lt[:, ci]
            fr, fi = fr + tr * sr - ti * si, fi + tr * si + ti * sr
        kr, ki = car_ref[pl.ds(last, 1), cr], car_ref[pl.ds(last, 1), ci]
        pr, pi = pw[:, cr], pw[:, ci]
        fr, fi = fr + pr * kr - pi * ki, fi + pr * ki + pi * kr
        car_ref[:, cr] = fr
        car_ref[:, ci] = fi
        to_next = 1 if not reverse else SUBLANE - 1
        gr = jnp.where(row == first, kr, pltpu.roll(fr, to_next, 0))
        gi = jnp.where(row == first, ki, pltpu.roll(fi, to_next, 0))

        def fix(i, state):
            rows = step_rows(i)
            qr, qi = fix_ref[rows, cr], fix_ref[rows, ci]
            yr = o_ref[rows, cr] + qr * gr - qi * gi
            yi = o_ref[rows, ci] + qr * gi + qi * gr
            o_ref[rows, cr] = yr
            o_ref[rows, ci] = yi
            if acc_ref is None:
                return state
            nr, ni, sr, si = state
            pxr, pxi = x_ref[rows, cr], x_ref[rows, ci]
            return yr, yi, sr + nr * pxr + ni * pxi, si + ni * pxr - nr * pxi

        if acc_ref is None:
            lax.fori_loop(0, steps, fix, 0)
        else:
            _, _, sr, si = lax.fori_loop(0, steps, fix, (gr, gi, zero, zero))
            acc_ref[:, cr] += sr
            acc_ref[:, ci] += si


def _s5_fwd(proj, u_blk, bdc, cdc, fix, tabs, dskip, seq, sb):
    n_rows = proj.shape[0]
    nb, blk, w_blk = bdc.shape
    c, w = nb * blk, nb * w_blk
    tt = fix.shape[0]

    def kern(u_ref, bd_ref, cd_ref, fix_ref, l1, l2, l4, pw, d_ref, xs_ref, yp_ref, yg_ref, us_ref, bu_ref, car_ref):
        @pl.when((pl.program_id(0) * tt) % seq == 0)
        def _():
            car_ref[...] = jnp.zeros_like(car_ref)

        _to_segments(u_ref, us_ref)
        for j in range(nb):
            bu_ref[:, pl.ds(j * w_blk, w_blk)] = jnp.dot(us_ref[:, pl.ds(j * blk, blk)].astype(BF16), bd_ref[j],
                                                         preferred_element_type=F32)
        _scan_tile(bu_ref, xs_ref, fix_ref, (l1, l2, l4, pw), car_ref, sb, False)
        for j in range(nb):
            cols = pl.ds(j * blk, blk)
            y0 = jnp.dot(xs_ref[:, pl.ds(j * w_blk, w_blk)].astype(BF16), cd_ref[j], preferred_element_type=F32)
            us_ref[:, cols] = y0 + d_ref[:, cols] * us_ref[:, cols]
        _from_segments(us_ref, yp_ref)
        yg_ref[...] = _gelu(yp_ref[...]).astype(BF16)

    tab = pl.BlockSpec((SUBLANE, w), lambda i: (0, 0))
    rows = pl.BlockSpec((tt, c), lambda i: (i, 0))
    return pl.pallas_call(
        kern, name="s5_fwd", grid=(n_rows // tt,),
        in_specs=[pl.BlockSpec((tt, c), lambda i: (i, u_blk * blk // c)), pl.BlockSpec(bdc.shape, lambda i: (0, 0, 0)),
                  pl.BlockSpec(cdc.shape, lambda i: (0, 0, 0)), pl.BlockSpec((tt, w), lambda i: (0, 0)), tab, tab, tab, tab,
                  pl.BlockSpec((1, c), lambda i: (0, 0))],
        out_specs=[pl.BlockSpec((tt, w), lambda i: (i, 0)), rows, rows],
        out_shape=[jax.ShapeDtypeStruct((n_rows, w), F32), jax.ShapeDtypeStruct((n_rows, c), F32),
                   jax.ShapeDtypeStruct((n_rows, c), BF16)],
        scratch_shapes=[pltpu.VMEM((tt, c), F32), pltpu.VMEM((tt, w), F32), pltpu.VMEM((SUBLANE, w), F32)],
        compiler_params=_params("arbitrary"))(proj, bdc, cdc, fix, *tabs, dskip)


def _s5_bwd(dypre, du_skip, xs, proj, u_blk, bdc, cdc, fix, tabs, seq, sb):
    n_rows = proj.shape[0]
    nb, blk, w_blk = bdc.shape
    c, w = nb * blk, nb * w_blk
    tt = fix.shape[0]
    nt = n_rows // tt
    tn = (((0,), (0,)), ((), ()))

    def kern(dy_ref, ds_ref, x_ref, u_ref, bd_ref, cd_ref, fix_ref, l1, l2, l4, pw, du_ref, da_ref, db_ref, dc_ref,
             dys_ref, us_ref, dus_ref, gx_ref, lam_ref, car_ref, acc_ref):
        i = pl.program_id(0)

        @pl.when(((nt - i) * tt) % seq == 0)
        def _():
            car_ref[...] = jnp.zeros_like(car_ref)

        @pl.when(i == 0)
        def _():
            acc_ref[...] = jnp.zeros_like(acc_ref)
            db_ref[...] = jnp.zeros_like(db_ref)
            dc_ref[...] = jnp.zeros_like(dc_ref)

        _to_segments(dy_ref, dys_ref)
        _to_segments(u_ref, us_ref)
        for j in range(nb):
            gx_ref[:, pl.ds(j * w_blk, w_blk)] = lax.dot_general(dys_ref[:, pl.ds(j * blk, blk)].astype(BF16), cd_ref[j], NT,
                                                                 preferred_element_type=F32)
        _scan_tile(gx_ref, lam_ref, fix_ref, (l1, l2, l4, pw), car_ref, sb, True, x_ref, acc_ref)
        for j in range(nb):
            cols, wide = pl.ds(j * blk, blk), pl.ds(j * w_blk, w_blk)
            lam = lam_ref[:, wide].astype(BF16)
            dus_ref[:, cols] = lax.dot_general(lam, bd_ref[j], NT, preferred_element_type=F32)
            db_ref[j] += lax.dot_general(us_ref[:, cols].astype(BF16), lam, tn, preferred_element_type=F32)
            dc_ref[j] += lax.dot_general(x_ref[:, wide].astype(BF16), dys_ref[:, cols].astype(BF16), tn,
                                         preferred_element_type=F32)
        _from_segments(dus_ref, du_ref)
        du_ref[...] += ds_ref[...]

        @pl.when(i == nt - 1)
        def _():
            da_ref[...] = _colsum(acc_ref[...])

    back = lambda i: (nt - 1 - i, 0)
    tab = pl.BlockSpec((SUBLANE, w), lambda i: (0, 0))
    rows = pl.BlockSpec((tt, c), back)
    whole = lambda a: pl.BlockSpec(a.shape, lambda i: (0, 0, 0))
    return pl.pallas_call(
        kern, name="s5_bwd", grid=(nt,),
        in_specs=[rows, rows, pl.BlockSpec((tt, w), back), pl.BlockSpec((tt, c), lambda i: (nt - 1 - i, u_blk * blk // c)),
                  whole(bdc), whole(cdc), pl.BlockSpec((tt, w), lambda i: (0, 0)), tab, tab, tab, tab],
        out_specs=[rows, pl.BlockSpec((1, w), lambda i: (0, 0)), whole(bdc), whole(cdc)],
        out_shape=[jax.ShapeDtypeStruct((n_rows, c), F32), jax.ShapeDtypeStruct((1, w), F32),
                   jax.ShapeDtypeStruct(bdc.shape, F32), jax.ShapeDtypeStruct(cdc.shape, F32)],
        scratch_shapes=[pltpu.VMEM((tt, c), F32)] * 3 + [pltpu.VMEM((tt, w), F32)] * 2 + [pltpu.VMEM((SUBLANE, w), F32)] * 2,
        compiler_params=_params("arbitrary"))(dypre, du_skip, xs, proj, bdc, cdc, fix, *tabs)


def _s5_post2(yg, q0, bg, og):
    c = yg.shape[1]

    def body(ins, outs, accs):
        ygv = ins[0][...].astype(F32)
        sg = ygv * _sigmoid(ins[1][...] + ins[2][...])
        outs[0][...] = (sg * _rms_r(sg) * ins[3][...]).astype(BF16)

    return _rowwise("s5_post2", body, yg.shape[0], [(yg, c, 0), (q0, c, 0)], [bg, og], [(c, BF16)], [])[0]


def _s5_post2_bwd(dmixed, yg, q0, bg, og):
    c = yg.shape[1]

    def body(ins, outs, accs):
        dsn, ygv = ins[0][...], ins[1][...].astype(F32)
        s = _sigmoid(ins[2][...] + ins[3][...])
        sg = ygv * s
        r = _rms_r(sg)
        accs[0][...] += _colsum(dsn * sg * r)
        dsg = _rms_bwd(sg, r, ins[4][...], dsn)
        dq = dsg * ygv * s * (1.0 - s)
        outs[0][...] = dq.astype(BF16)
        outs[1][...] = dsg * s
        accs[1][...] += _colsum(dq)

    return _rowwise("s5_post2_bwd", body, yg.shape[0], [(dmixed, c, 1), (yg, c, 0), (q0, c, 0)], [bg, og],
                    [(c, BF16), (c, F32)], [(1, c)] * 2)


def _s5_post1_bwd(dyg1, dyg2, ypre, proj, dskip, after=()):
    c = ypre.shape[1]

    def body(ins, outs, accs):
        dyp = (ins[0][...] + ins[1][...]) * _dgelu(ins[2][...])
        outs[0][...] = dyp
        outs[1][...] = dyp * ins[4][...]
        accs[0][...] += _colsum(dyp * ins[3][...])

    return _rowwise("s5_post1_bwd", body, ypre.shape[0], [(dyg1, c, 0), (dyg2, c, 0), (ypre, c, 0), (proj, c, 2)], [dskip],
                    [(c, F32), (c, F32)], [(1, c)], after=after)


def _place():
    return lax.axis_index("x"), lax.axis_index("y"), lax.axis_index("c")


def _window(ref, axis, q, rows, cols):
    if axis == 0:
        return ref.at[pl.ds(pl.multiple_of(q * rows, SUBLANE), rows), :]
    return ref.at[:, pl.ds(pl.multiple_of(q * cols, LANE), cols)]


ALL_RELS = [(fx, fy, fc) for fx in (0, 1) for fy in (0, 1) for fc in (0, 1)][1:]
N_PEERS = {"gather": 3, "scatter": 3, "sibling": 1, "all": len(ALL_RELS)}


def _copies(kind, srcs, lands, shards, axes, send_sems, recv_sems, local_sems):
    x, y, c = _place()
    me, dev = 2 * x + y, 4 * x + 2 * y + c
    n_peers = N_PEERS[kind]
    starts, waits = [], []
    for a, (src, land) in enumerate(zip(srcs, lands)):
        on = lambda k, peer: dict(send_sem=send_sems.at[n_peers * a + k], recv_sem=recv_sems.at[n_peers * a + k],
                                  device_id=peer, device_id_type=MESH)
        if kind == "sibling":
            cp = pltpu.make_async_remote_copy(src_ref=src, dst_ref=land, **on(0, (x, y, 1 - c)))
            starts.append(cp)
            waits.append(cp)
            continue
        if kind == "all":
            own = pltpu.make_async_copy(src, land.at[dev], local_sems.at[a])
            starts.append(own)
            waits.append(own)
            for k, (fx, fy, fc) in enumerate(ALL_RELS):
                px, py, pc = (1 - x) if fx else x, (1 - y) if fy else y, (1 - c) if fc else c
                starts.append(pltpu.make_async_remote_copy(src_ref=src, dst_ref=land.at[dev], **on(k, (px, py, pc))))
                waits.append(pltpu.make_async_remote_copy(src_ref=src, dst_ref=land.at[4 * px + 2 * py + pc],
                                                          **on(k, (px, py, pc))))
            continue
        rows, cols = shards[a]
        if kind == "gather":
            own = pltpu.make_async_copy(src, _window(land, axes[a], me, rows, cols), local_sems.at[a])
        else:
            own = pltpu.make_async_copy(_window(src, axes[a], me, rows, cols), land.at[3], local_sems.at[a])
        starts.append(own)
        waits.append(own)
        for j, (fx, fy) in enumerate(CHIP_RELS):
            px, py = (1 - x) if fx else x, (1 - y) if fy else y
            peer = 2 * px + py
            if kind == "gather":
                starts.append(pltpu.make_async_remote_copy(src_ref=src, dst_ref=_window(land, axes[a], me, rows, cols),
                                                           **on(j, (px, py, c))))
                waits.append(pltpu.make_async_remote_copy(src_ref=src, dst_ref=_window(land, axes[a], peer, rows, cols),
                                                          **on(j, (px, py, c))))
            else:
                cp = pltpu.make_async_remote_copy(src_ref=_window(src, axes[a], peer, rows, cols), dst_ref=land.at[j],
                                                  **on(j, (px, py, c)))
                starts.append(cp)
                waits.append(cp)
    return starts, waits


HBM = pl.BlockSpec(memory_space=pltpu.HBM)
SEM = pl.BlockSpec(memory_space=pltpu.SEMAPHORE)


def _shard_shapes(kind, arrs, axes):
    if kind != "scatter":
        return [a.shape for a in arrs]
    return [(a.shape[0] // N_CHIPS, a.shape[1]) if ax == 0 else (a.shape[0], a.shape[1] // N_CHIPS) for a, ax in zip(arrs, axes)]


def _land_shapes(kind, arrs, axes):
    if kind == "gather":
        return [(N_CHIPS * a.shape[0], a.shape[1]) if ax == 0 else (a.shape[0], N_CHIPS * a.shape[1]) for a, ax in zip(arrs, axes)]
    if kind == "scatter":
        return [(N_CHIPS,) + s for s in _shard_shapes(kind, arrs, axes)]
    return [a.shape if kind == "sibling" else (len(ALL_RELS) + 1,) + a.shape for a in arrs]


def _exchange_start(name, kind, arrs, axes, after=()):
    n, n_after = len(arrs), len(after)
    shards = _shard_shapes(kind, arrs, axes)
    land_shapes = _land_shapes(kind, arrs, axes)
    lands = [lax.empty(s, a.dtype) for s, a in zip(land_shapes, arrs)]

    def kern(*refs):
        outs = refs[2 * n + n_after:]
        starts, _ = _copies(kind, refs[:n], refs[n:2 * n], shards, axes, outs[0], outs[1], outs[2])
        for cp in starts:
            cp.start()
        outs[-1][...] = jnp.zeros_like(outs[-1])

    kept = [pltpu.HBM(a.shape, a.dtype) for a in arrs] + [pltpu.HBM(s, a.dtype) for s, a in zip(land_shapes, arrs)]
    n_sems = N_PEERS[kind] * n
    res = pl.pallas_call(
        kern, name=name, in_specs=[HBM] * (2 * n) + [ANY] * n_after,
        out_specs=[SEM] * 3 + [HBM] * (2 * n) + [pl.BlockSpec(memory_space=pltpu.VMEM)],
        out_shape=[pltpu.SemaphoreType.DMA((n_sems,)), pltpu.SemaphoreType.DMA((n_sems,)), pltpu.SemaphoreType.DMA((n,))]
        + kept + [jax.ShapeDtypeStruct((SUBLANE, LANE), F32)],
        input_output_aliases={i: 3 + i for i in range(2 * n)},
        compiler_params=pltpu.CompilerParams(has_side_effects=pltpu.SideEffectType.DATAFLOW_SIDE_EFFECTING),
    )(*[pltpu.with_memory_space_constraint(a, pltpu.HBM) for a in list(arrs) + lands], *after)
    return res[:3], res[3:3 + n], res[3 + n:3 + 2 * n], res[-1]


def _exchange_wait(name, kind, started, axes, after, sources_too=False):
    sems, srcs, lands, _ = started
    n, n_after = len(srcs), len(after)
    shards = _shard_shapes(kind, srcs, axes)

    def kern(*refs):
        sem_refs = refs[2 * n:2 * n + 3]
        _, waits = _copies(kind, refs[:n], refs[n:2 * n], shards, axes, *sem_refs)
        for cp in waits:
            cp.wait()

    res = pl.pallas_call(
        kern, name=name, in_specs=[HBM] * (2 * n) + [SEM] * 3 + [ANY] * n_after, out_specs=[HBM] * (2 * n),
        out_shape=[pltpu.HBM(a.shape, a.dtype) for a in list(srcs) + list(lands)],
        input_output_aliases={i: i for i in range(2 * n)},
        compiler_params=pltpu.CompilerParams(has_side_effects=pltpu.SideEffectType.DATAFLOW_SIDE_EFFECTING),
    )(*srcs, *lands, *sems, *after)
    return (res[:n], res[n:]) if sources_too else res[n:]


def _sum_devices(parts):
    def kern(p_ref, o_ref):
        acc = p_ref[0]
        for d in range(1, parts.shape[0]):
            acc = acc + p_ref[d]
        o_ref[...] = acc

    return pl.pallas_call(kern, name="sum_devices", out_shape=jax.ShapeDtypeStruct(parts.shape[1:], F32),
                          compiler_params=pltpu.CompilerParams(vmem_limit_bytes=VMEM_LIMIT_BYTES))(parts)


def _sum_slots(name, parts):
    _, rows, cols = parts.shape
    tr = _pick(rows, (ROW_TILE, 128, 64, 32))

    def kern(p_ref, o_ref):
        o_ref[...] = ((p_ref[3].astype(F32) + p_ref[0].astype(F32)) + p_ref[1].astype(F32)) + p_ref[2].astype(F32)

    return pl.pallas_call(kern, name=name, grid=(rows // tr,),
                          in_specs=[pl.BlockSpec((N_CHIPS, tr, cols), lambda i: (0, i, 0))],
                          out_specs=pl.BlockSpec((tr, cols), lambda i: (i, 0)),
                          out_shape=jax.ShapeDtypeStruct((rows, cols), F32), compiler_params=_params("arbitrary"))(parts)


def _adamw_math(g, w, m, v):
    m2 = ADAM_B1 * m + (1.0 - ADAM_B1) * g
    v2 = ADAM_B2 * v + (1.0 - ADAM_B2) * (g * g)
    m_hat = m2 / (1.0 - ADAM_B1 ** ADAM_STEP)
    v_hat = v2 / (1.0 - ADAM_B2 ** ADAM_STEP)
    return -ADAM_LR * (m_hat / (jnp.sqrt(v_hat) + ADAM_EPS) + ADAM_WD * w), m2, v2


def _adamw(name, parts, w, m, v):
    rows, cols = w.shape
    tr = rows if rows * cols <= WHOLE_ELEMS else _pick(rows, (ROW_TILE, 352, 128, 64, 32, 8))
    n = len(parts)

    def kern(*refs):
        g = refs[0][:, pl.ds(0, cols)]
        for p in refs[1:n]:
            g = g + p[:, pl.ds(0, cols)]
        d, m2, v2 = _adamw_math(g, refs[n][...], refs[n + 1][...], refs[n + 2][...])
        refs[n + 3][...] = g
        refs[n + 4][...] = d
        refs[n + 5][...] = m2
        refs[n + 6][...] = v2

    spec = pl.BlockSpec((tr, cols), lambda i: (i, 0))
    return pl.pallas_call(kern, name=name, grid=(rows // tr,),
                          in_specs=[pl.BlockSpec((tr, p.shape[1]), lambda i: (i, 0)) for p in parts] + [spec] * 3,
                          out_specs=[spec] * 4, out_shape=[jax.ShapeDtypeStruct((rows, cols), F32)] * 4,
                          compiler_params=_params("arbitrary"))(*parts, w, m, v)


def _adamw_many(name, gs, ws, ms, vs):
    n = len(gs)

    def kern(*refs):
        for p in range(n):
            d, m2, v2 = _adamw_math(refs[p][...], refs[n + p][...], refs[2 * n + p][...], refs[3 * n + p][...])
            refs[4 * n + p][...] = d
            refs[5 * n + p][...] = m2
            refs[6 * n + p][...] = v2

    res = pl.pallas_call(kern, name=name, out_shape=[jax.ShapeDtypeStruct(w.shape, F32) for w in ws] * 3,
                         compiler_params=pltpu.CompilerParams(vmem_limit_bytes=VMEM_LIMIT_BYTES))(*gs, *ws, *ms, *vs)
    return res[:n], res[n:2 * n], res[2 * n:]


def _pack(arrs):
    parts, rows = [], []
    for a in arrs:
        r = _round_up(-(-a.size // LANE), SUBLANE)
        parts.append(jnp.pad(a.reshape(-1).astype(F32), (0, r * LANE - a.size)).reshape(r, LANE))
        rows.append(r)
    return jnp.concatenate(parts, axis=0), rows


def _unpack(buf, rows, shapes):
    out, r0 = [], 0
    for r, s in zip(rows, shapes):
        size = math.prod(s)
        out.append(buf[r0:r0 + r].reshape(-1)[:size].reshape(s))
        r0 += r
    return out


def kernel(x, norm_ffn1, ffn1_w1, ffn1_w3, ffn1_w2, norm_mix, w_in, conv_w, conv_b, conv_ln_g, conv_ln_b, conv_out_g, ssm_A_re, ssm_A_im, ssm_log_dt, ssm_B_re, ssm_B_im, ssm_C_re, ssm_C_im, ssm_D, ssm_glu_w, ssm_glu_b, ssm_out_g, w_out, norm_ffn2, ffn2_w1, ffn2_w3, ffn2_w2, norm_final, loss_target, m_norm_ffn1, m_ffn1_w1, m_ffn1_w3, m_ffn1_w2, m_norm_mix, m_w_in, m_conv_w, m_conv_b, m_conv_ln_g, m_conv_ln_b, m_conv_out_g, m_ssm_A_re, m_ssm_A_im, m_ssm_log_dt, m_ssm_B_re, m_ssm_B_im, m_ssm_C_re, m_ssm_C_im, m_ssm_D, m_ssm_glu_w, m_ssm_glu_b, m_ssm_out_g, m_w_out, m_norm_ffn2, m_ffn2_w1, m_ffn2_w3, m_ffn2_w2, m_norm_final, v_norm_ffn1, v_ffn1_w1, v_ffn1_w3, v_ffn1_w2, v_norm_mix, v_w_in, v_conv_w, v_conv_b, v_conv_ln_g, v_conv_ln_b, v_conv_out_g, v_ssm_A_re, v_ssm_A_im, v_ssm_log_dt, v_ssm_B_re, v_ssm_B_im, v_ssm_C_re, v_ssm_C_im, v_ssm_D, v_ssm_glu_w, v_ssm_glu_b, v_ssm_out_g, v_w_out, v_norm_ffn2, v_ffn2_w1, v_ffn2_w3, v_ffn2_w2, v_norm_final):
    given = dict(locals())
    wts = {n: given[n] for n in WEIGHTS}
    n_seq, seq, d = x.shape
    n_rows = n_seq * seq
    xf = x.reshape(n_rows, d)
    tgt = loss_target.reshape(n_rows, d)
    row = lambda a: a.reshape(1, -1)

    f = ffn1_w1.shape[-1]
    fp = _round_up(f, LANE)
    held = lambda n, a: a[0].T if n in TRANSPOSED else a[0]
    shards = []
    for n in BIG:
        s = held(n, wts[n]).astype(BF16)
        if n.startswith('ffn'):
            s = jnp.pad(s, ((0, fp - f), (0, 0)))
        shards.append(s)
    n_taps, c_shard = conv_w.shape[1], conv_w.shape[2]
    shards.append(jnp.pad(conv_w[0], ((0, HALO - n_taps), (0, 0))))
    shard_of = dict(zip(BIG + ['conv_w'], shards))
    axis_of = dict(BIG_AXIS, conv_w=1)
    groups = [['ffn1_w1', 'ffn1_w3'], ['ffn1_w2', 'w_in', 'conv_w', 'ssm_glu_w', 'w_out'], ['ffn2_w1', 'ffn2_w3', 'ffn2_w2']]
    fetch, tok = [], []
    for k, names in enumerate(groups):
        fetch.append(_exchange_start("gather%d_send" % k, "gather", [shard_of[n] for n in names],
                                     [axis_of[n] for n in names], tok))
        tok = [fetch[-1][3]]
    full = {}

    def arrive(k, after):
        lands = _exchange_wait("gather%d_recv" % k, "gather", fetch[k], [axis_of[n] for n in groups[k]], after)
        full.update(zip(groups[k], lands))

    h1, h1_t = _rms_fwd("ffn1_rms", xf, norm_ffn1)
    arrive(0, tok + [h1])

    _, n_grp, n_state = ssm_A_re.shape
    grp = ssm_B_re.shape[-1]
    ns = n_grp * n_state
    c_ssm = n_grp * grp
    lr, li = ssm_A_re.reshape(1, ns), ssm_A_im.reshape(1, ns)
    ldt = jnp.repeat(ssm_log_dt.reshape(n_grp), n_state).reshape(1, ns)
    btr = ssm_B_re[0].transpose(2, 0, 1).reshape(grp, ns)
    bti = ssm_B_im[0].transpose(2, 0, 1).reshape(grp, ns)
    ctr = ssm_C_re[0].transpose(1, 0, 2).reshape(grp, ns)
    cti = ssm_C_im[0].transpose(1, 0, 2).reshape(grp, ns)
    scan_tile = _pick(seq, (SCAN_TILE,))
    _, _, bbr, bbi, seg_up, seg_down, pw, pw_falling = _s5_params_fwd(lr, li, ldt, btr, bti, scan_tile // SUBLANE)
    nb = c_ssm // LANE
    sb, gpb = ns // nb, n_grp // nb
    diag = (jnp.arange(LANE)[:, None] // grp) == (jnp.arange(sb)[None, :] // n_state)

    def spread(t):
        return jnp.where(diag, jnp.tile(t.reshape(grp, nb, sb).transpose(1, 0, 2), (1, gpb, 1)), 0.0)

    def gather_diag(t):
        return (t * diag).reshape(nb, gpb, grp, sb).sum(1).transpose(1, 0, 2).reshape(grp, ns)

    def interleave(re, im):
        return jnp.stack([re.reshape(-1, nb, sb), im.reshape(-1, nb, sb)], axis=2).reshape(-1, 2 * ns)

    bdc = jnp.concatenate([spread(bbr), spread(bbi)], axis=2).astype(BF16)
    cdc = jnp.concatenate([spread(ctr).transpose(0, 2, 1), -spread(cti).transpose(0, 2, 1)], axis=1).astype(BF16)
    rowi = jnp.arange(SUBLANE)[:, None]
    pwf, pwc = interleave(pw[:, :ns], pw[:, ns:]), interleave(pw[:, :ns], -pw[:, ns:])
    tabs_f = [jnp.where(rowi >= s, pwf[s - 1][None, :], 0.0) for s in (1, 2, 4)] + [pwf]
    tabs_b = [jnp.where(rowi <= SUBLANE - 1 - s, pwc[s - 1][None, :], 0.0) for s in (1, 2, 4)]
    tabs_b.append(interleave(pw_falling[:, :ns], -pw_falling[:, ns:]))
    fix_f = jnp.repeat(interleave(seg_up[:, :ns], seg_up[:, ns:]), SUBLANE, axis=0)
    fix_b = jnp.repeat(interleave(seg_down[:, :ns], -seg_down[:, ns:]), SUBLANE, axis=0)
    c_conv = conv_b.shape[1]
    u_blk = 2 * c_conv // LANE

    a1, b1, z1 = _ffn_up("ffn1_up", h1, full['ffn1_w1'], full['ffn1_w3'])
    arrive(1, [z1])
    x1, h2, h2_t = _mm("ffn1_down", z1, full['ffn1_w2'], 1, 0, addend=xf, alpha=0.5, post=_post_rms(norm_mix))
    saved1 = (h1_t, a1, b1, z1)
    cw = full['conv_w']
    proj = _mm("mix_in", h2, full['w_in'], 1, 0, F32)
    assert c_conv == c_ssm and proj.shape[1] == 3 * c_conv
    cpre, an = _conv_fwd(proj, cw, conv_b, conv_ln_g, conv_ln_b, conv_out_g, seq)
    xs, ypre, yg = _s5_fwd(proj, u_blk, bdc, cdc, fix_f, tabs_f, ssm_D, seq, sb)
    q0 = _mm("s5_gate", yg, full['ssm_glu_w'], 1, 0, F32)
    sn = _s5_post2(yg, q0, ssm_glu_b, ssm_out_g)
    wo = full['w_out']
    mixed = jnp.concatenate([an, sn], axis=1)
    x2, h3, h3_t = _mm("mix_out", mixed, wo, 1, 0, addend=x1, post=_post_rms(norm_ffn2))
    arrive(2, [x2])
    a3, b3, z3 = _ffn_up("ffn2_up", h3, full['ffn2_w1'], full['ffn2_w3'])
    saved2 = (h3_t, a3, b3, z3)
    dx3, dx3_t, loss_row, d_norm_final = _mm("ffn2_down", z3, full['ffn2_w2'], 1, 0, addend=x2, alpha=0.5,
                                             post=_post_loss(row(norm_final), tgt))

    g = {}
    dx2, g['norm_ffn2'], sent = _ffn_bwd("ffn2", x2, norm_ffn2, full['ffn2_w1'], full['ffn2_w3'], full['ffn2_w2'], saved2,
                                         dx3, dx3_t)
    dmixed = _mm("mix_dmixed", dx2, wo, 1, 1, F32)
    dwo = _mm("mix_dwo", mixed, dx2, 0, 0, BF16)
    dq, dyg1, g['ssm_out_g'], g['ssm_glu_b'] = _s5_post2_bwd(dmixed, yg, q0, ssm_glu_b, ssm_out_g)
    dyg2 = _mm("s5_dgate", dq, full['ssm_glu_w'], 1, 1, F32)
    dwg = _mm("s5_dwg", yg, dq, 0, 0, BF16)
    sent['w_out ssm_glu_w'] = (_exchange_start("mix_wo_wg_send", "scatter", [dwo, dwg], [0, 0]), [0, 0])
    dypre, du_skip, g['ssm_D'] = _s5_post1_bwd(dyg1, dyg2, ypre, proj, ssm_D, after=[sent['w_out ssm_glu_w'][0][3]])
    du, dabar, dbdc, dcdc = _s5_bwd(dypre, du_skip, xs, proj, u_blk, bdc, cdc, fix_b, tabs_b, seq, sb)
    dabar = dabar.reshape(nb, 2, sb)
    dlr, dli, dldt, dbtr, dbti = _s5_params_bwd(lr, li, ldt, btr, bti, dabar[:, 0].reshape(1, ns), dabar[:, 1].reshape(1, ns),
                                                gather_diag(dbdc[:, :, :sb]), gather_diag(dbdc[:, :, sb:]))
    g['ssm_A_re'], g['ssm_A_im'] = dlr, dli
    g['ssm_log_dt'] = dldt.reshape(n_grp, n_state).sum(axis=1)
    g['ssm_B_re'] = dbtr.reshape(grp, n_grp, n_state).transpose(1, 2, 0)
    g['ssm_B_im'] = dbti.reshape(grp, n_grp, n_state).transpose(1, 2, 0)
    g['ssm_C_re'] = gather_diag(dcdc[:, :sb].transpose(0, 2, 1)).reshape(grp, n_grp, n_state).transpose(1, 0, 2)
    g['ssm_C_im'] = -gather_diag(dcdc[:, sb:].transpose(0, 2, 1)).reshape(grp, n_grp, n_state).transpose(1, 0, 2)
    dc, g['conv_out_g'], g['conv_ln_g'], g['conv_ln_b'], g['conv_b'] = _conv_bwd_rows(dmixed, cpre, conv_ln_g, conv_ln_b,
                                                                                    conv_out_g)
    dval, dgate, dcw = _conv_bwd_taps(proj, dc, cw, seq)
    dproj = jnp.concatenate([dval, dgate, du], axis=1)
    sent['w_in'] = (_exchange_start("mix_win_send", "scatter", [_mm("mix_dwin", h2_t, dproj, 1, 0, BF16)], [1]), [1])
    dx1, dx1_t, g['norm_mix'] = _mm("mix_dh", dproj, full['w_in'], 1, 1, after=[sent['w_in'][0][3]],
                                    post=_post_rms_bwd(x1, norm_mix, dx2))
    dx0, g['norm_ffn1'], sent1 = _ffn_bwd("ffn1", xf, norm_ffn1, full['ffn1_w1'], full['ffn1_w3'], full['ffn1_w2'], saved1,
                                          dx1, dx1_t)
    sent.update(sent1)
    g['norm_final'] = d_norm_final
    g['conv_w'] = dcw[:n_taps]

    small_shapes = [(n_taps, c_conv) if n == 'conv_w' else wts[n].shape for n in SMALL]
    buf, buf_rows = _pack([g[n] for n in SMALL] + [loss_row])
    to_all = _exchange_start("small_send", "all", [buf], [0])
    slots = {}
    for names, (started, axes) in sent.items():
        lands = _exchange_wait(names.replace(' ', '_') + "_recv", "scatter", started, axes, after=[dx0, to_all[3]])
        slots.update(zip(names.split(), lands))
    sums = [_sum_slots("sum_" + n, slots[n]) for n in BIG]
    to_sibling = _exchange_start("sums_send", "sibling", sums, [0] * len(sums))
    from_all = _exchange_wait("small_recv", "all", to_all, [0], after=[to_sibling[3]])[0]
    total = _unpack(_sum_devices(from_all), buf_rows, small_shapes + [(1, LANE)])
    loss = total[-1][0, 0]
    grads = dict(zip(SMALL, total[:-1]))
    chip = 2 * lax.axis_index("x") + lax.axis_index("y")
    grads['conv_w'] = lax.dynamic_slice_in_dim(grads['conv_w'], chip * c_shard, c_shard, axis=1)[None]
    flat = lambda a: a.reshape(-1, a.shape[-1])
    small = _adamw_many("adamw_small", *[[flat(src[p + n]) for n in SMALL]
                                         for src, p in ((grads, ''), (given, ''), (given, 'm_'), (given, 'v_'))])
    deltas, new_m, new_v = ({n: o.reshape(wts[n].shape) for n, o in zip(SMALL, outs)} for outs in small)

    sums, theirs = _exchange_wait("sums_recv", "sibling", to_sibling, [0] * len(sums), after=[new_v[SMALL[-1]]],
                                  sources_too=True)
    for n, mine, other in zip(BIG, sums, theirs):
        grads[n], deltas[n], new_m[n], new_v[n] = (
            (o.T if n in TRANSPOSED else o)[None]
            for o in _adamw("adamw_" + n, [mine, other], held(n, given[n]), held(n, given['m_' + n]), held(n, given['v_' + n])))

    return (loss, dx0.reshape(x.shape), *[grads[n] for n in WEIGHTS], *[deltas[n] for n in WEIGHTS],
            *[new_m[n] for n in WEIGHTS], *[new_v[n] for n in WEIGHTS])
```

```python
import math
from typing import Callable, NamedTuple

import jax
import jax.numpy as jnp
from jax import lax
from jax.experimental import pallas as pl
from jax.experimental.pallas import tpu as pltpu

F32 = jnp.float32
BF16 = jnp.bfloat16
EPS = 1e-6
ADAM_LR, ADAM_B1, ADAM_B2, ADAM_EPS, ADAM_WD, ADAM_STEP = 0.001, 0.9, 0.999, 1e-08, 0.01, 10
MESH = pl.DeviceIdType.MESH
ANY = pl.BlockSpec(memory_space=pl.ANY)
LANE = 128
SUBLANE = 8
VMEM_LIMIT_BYTES = 56 << 20
ROW_TILE = 256
ROW_TILE_ELEMS = 256 * 1024
WHOLE_ELEMS = 512 * 1024
WHOLE_WEIGHT_BYTES = 8 << 20
FFN_ROWS = 256
FFN_CHUNK = 512
CONV_TILE = 128
CONV_SUB = 32
HALO = 32
SCAN_TILE = 256
SCAN_COLS = 512
N_CHIPS = 4
CHIP_RELS = ((1, 0), (0, 1), (1, 1))
NT = (((1,), (1,)), ((), ()))
GELU_K = math.sqrt(2.0 / math.pi)
GELU_C = 0.044715

WEIGHTS = ['norm_ffn1', 'ffn1_w1', 'ffn1_w3', 'ffn1_w2', 'norm_mix', 'w_in', 'conv_w', 'conv_b', 'conv_ln_g', 'conv_ln_b',
           'conv_out_g', 'ssm_A_re', 'ssm_A_im', 'ssm_log_dt', 'ssm_B_re', 'ssm_B_im', 'ssm_C_re', 'ssm_C_im', 'ssm_D',
           'ssm_glu_w', 'ssm_glu_b', 'ssm_out_g', 'w_out', 'norm_ffn2', 'ffn2_w1', 'ffn2_w3', 'ffn2_w2', 'norm_final']
BIG = ['ffn1_w1', 'ffn1_w3', 'ffn1_w2', 'w_in', 'ssm_glu_w', 'w_out', 'ffn2_w1', 'ffn2_w3', 'ffn2_w2']
BIG_AXIS = {'ffn1_w1': 0, 'ffn1_w3': 0, 'ffn1_w2': 0, 'w_in': 1, 'ssm_glu_w': 0, 'w_out': 0, 'ffn2_w1': 0, 'ffn2_w3': 0,
            'ffn2_w2': 0}
TRANSPOSED = ('ffn1_w1', 'ffn1_w3', 'ffn2_w1', 'ffn2_w3')
SMALL = [n for n in WEIGHTS if n not in BIG]


def _round_up(n, m):
    return -(-n // m) * m


def _pick(n, cands):
    for c in cands:
        if c <= n and n % c == 0:
            return c
    return n


def _params(*sem):
    return pltpu.CompilerParams(dimension_semantics=sem, vmem_limit_bytes=VMEM_LIMIT_BYTES)


def _rms_r(x):
    return lax.rsqrt(jnp.mean(x * x, axis=-1, keepdims=True) + EPS)


def _rms_bwd(x, r, g, dy):
    dyg = dy * g
    return r * dyg - x * (r * r * r) * jnp.mean(x * dyg, axis=-1, keepdims=True)


def _sigmoid(x):
    return jax.nn.sigmoid(x)


def _dsilu(a, s):
    return s * (1.0 + a * (1.0 - s))


def _gelu(x):
    return 0.5 * x * (1.0 + jnp.tanh(GELU_K * (x + GELU_C * x * x * x)))


def _dgelu(x):
    t = jnp.tanh(GELU_K * (x + GELU_C * x * x * x))
    return 0.5 * (1.0 + t) + 0.5 * x * (1.0 - t * t) * GELU_K * (1.0 + 3.0 * GELU_C * x * x)


def _colsum(v):
    return jnp.sum(v, axis=0, keepdims=True)


def _rowwise(name, body, n_rows, row_ins, par_ins, row_outs, acc_outs, after=()):
    widest = max([w for (_, w, _) in row_ins] + [w for (w, _) in row_outs])
    tt = _pick(n_rows, [t for t in (256, 128, 64, 32, 16, 8) if t * widest <= ROW_TILE_ELEMS])
    in_specs = [pl.BlockSpec((tt, w), lambda i, cb=cb: (i, cb)) for (_, w, cb) in row_ins]
    in_specs += [pl.BlockSpec(p.shape, lambda i: (0, 0)) for p in par_ins] + [ANY] * len(after)
    out_specs = [pl.BlockSpec((tt, w), lambda i: (i, 0)) for (w, _) in row_outs]
    out_specs += [pl.BlockSpec((r, w), lambda i: (0, 0)) for (r, w) in acc_outs]
    out_shape = [jax.ShapeDtypeStruct((n_rows, w), dt) for (w, dt) in row_outs]
    out_shape += [jax.ShapeDtypeStruct((r, w), F32) for (r, w) in acc_outs]
    n_in, n_ro = len(row_ins) + len(par_ins), len(row_outs)
    o0 = n_in + len(after)

    def kern(*refs):
        accs = refs[o0 + n_ro:]
        if accs:
            @pl.when(pl.program_id(0) == 0)
            def _():
                for a in accs:
                    a[...] = jnp.zeros_like(a)
        body(refs[:n_in], refs[o0:o0 + n_ro], accs)

    return pl.pallas_call(kern, name=name, grid=(n_rows // tt,), in_specs=in_specs, out_specs=out_specs, out_shape=out_shape,
                          compiler_params=_params("arbitrary"))(*[a for a, _, _ in row_ins], *par_ins, *after)


class Post(NamedTuple):
    rows: list
    gains: list
    outs: list
    t_outs: list
    sums: list
    fn: Callable


def _post_rms(gain):
    def fn(r, rows, gains):
        h = r * _rms_r(r) * gains[0]
        return [r, h, h], []

    return Post([], [gain], [F32, BF16], [BF16], [], fn)


def _post_rms_bwd(x, gain, dres):
    def fn(dh, rows, gains):
        r = _rms_r(rows[0])
        dx = rows[1] + _rms_bwd(rows[0], r, gains[0], dh)
        return [dx, dx], [_colsum(dh * rows[0] * r)]

    return Post([x, dres], [gain], [F32], [BF16], [x.shape[1]], fn)


def _post_loss(gain, tgt):
    d = tgt.shape[1]

    def fn(xv, rows, gains):
        r = _rms_r(xv)
        e = xv * r * gains[0] - rows[0]
        sq = jnp.sum(jnp.sum(e * e, axis=-1, keepdims=True), axis=0, keepdims=True)
        dy = e * (1.0 / d)
        dx = _rms_bwd(xv, r, gains[0], dy)
        return [dx, dx], [jnp.broadcast_to(sq * (0.5 / d), (1, LANE)), _colsum(dy * xv * r)]

    return Post([tgt], [gain], [F32], [BF16], [LANE, d], fn)


def _mm(name, a, b, ca, cb, out_dtype=F32, addend=None, alpha=1.0, a_cols=None, after=(), post=None, transposed=False):
    a_start, a_width = a_cols if a_cols else (0, a.shape[1])
    m, k = (a.shape[0], a_width) if ca == 1 else (a_width, a.shape[0])
    n = b.shape[1 - cb]
    assert b.shape[cb] == k, (name, a.shape, b.shape)
    tn = _pick(n, (1024, 768, 512, 384, 256, 128))
    if post and k * tn * b.dtype.itemsize <= WHOLE_WEIGHT_BYTES:
        tk = k
        tm = _pick(m, (256, 128) if k > 1024 else (512, 256, 128))
    else:
        tm = _pick(m, (512, 256, 128) if post else (1024, 512, 256, 128))
        tk = _pick(k, (2048, 1024, 768, 512, 256, 128) if k >= 4096 and not post else (1024, 768, 512, 256, 128))
    nk = k // tk
    if ca == 1:
        assert a_start % tk == 0
        a_spec = pl.BlockSpec((tm, tk), lambda i, j, kk: (i, kk + a_start // tk))
    else:
        assert a_start % tm == 0
        a_spec = pl.BlockSpec((tk, tm), lambda i, j, kk: (kk, i + a_start // tm))
    b_spec = pl.BlockSpec((tk, tn), lambda i, j, kk: (kk, j)) if cb == 0 else pl.BlockSpec((tn, tk), lambda i, j, kk: (j, kk))
    o_spec = pl.BlockSpec((tm, tn), lambda i, j, kk: (i, j))
    t_spec = pl.BlockSpec((tn, tm), lambda i, j, kk: (j, i))
    fixed = lambda w: pl.BlockSpec((1, w), lambda i, j, kk: (0, 0))
    ins, in_specs = [a, b], [a_spec, b_spec]
    if addend is not None:
        ins.append(addend)
        in_specs.append(o_spec)
    n_plain = len(ins)
    n_rows, n_gains = (len(post.rows), len(post.gains)) if post else (0, 0)
    if post:
        assert tn == n, name
        ins += post.rows + post.gains
        in_specs += [o_spec] * n_rows + [fixed(n)] * n_gains
    ins += list(after)
    in_specs += [ANY] * len(after)
    n_in = len(ins)
    if post:
        n_straight, n_vals = len(post.outs), len(post.outs) + len(post.t_outs)
        out_specs = [o_spec] * n_straight + [t_spec] * len(post.t_outs) + [fixed(w) for w in post.sums]
        out_shape = [jax.ShapeDtypeStruct((m, n), dt) for dt in post.outs] + [jax.ShapeDtypeStruct((n, m), dt) for dt in post.t_outs]
        out_shape += [jax.ShapeDtypeStruct((1, w), F32) for w in post.sums]
    elif transposed:
        out_specs, out_shape = [t_spec], [jax.ShapeDtypeStruct((n, m), out_dtype)]
    else:
        out_specs, out_shape = [o_spec], [jax.ShapeDtypeStruct((m, n), out_dtype)]
    n_out = len(out_specs)
    dims = (((ca,), (cb,)), ((), ()))

    def emit(refs, r):
        if alpha != 1.0:
            r = r * alpha
        if addend is not None:
            r = r + refs[2][...].astype(F32)
        outs = refs[n_in:n_in + n_out]
        if post is None:
            outs[0][...] = (r.T if transposed else r).astype(out_dtype)
            return
        vals, incs = post.fn(r, [q[...] for q in refs[n_plain:n_plain + n_rows]],
                             [q[...] for q in refs[n_plain + n_rows:n_plain + n_rows + n_gains]])
        for at, (o_ref, val) in enumerate(zip(outs, vals)):
            o_ref[...] = (val if at < n_straight else val.T).astype(o_ref.dtype)
        for s_ref, inc in zip(outs[n_vals:], incs):
            s_ref[...] += inc

    def kern(*refs):
        kk = pl.program_id(2)
        if post and post.sums:
            @pl.when(jnp.logical_and(jnp.logical_and(pl.program_id(0) == 0, pl.program_id(1) == 0), kk == 0))
            def _():
                for s_ref in refs[n_in + n_vals:n_in + n_out]:
                    s_ref[...] = jnp.zeros_like(s_ref)

        dot = lambda: lax.dot_general(refs[0][...].astype(BF16), refs[1][...].astype(BF16), dims,
                                      preferred_element_type=F32)
        if nk == 1:
            emit(refs, dot())
            return
        acc_ref = refs[-1]

        @pl.when(kk == 0)
        def _():
            acc_ref[...] = jnp.zeros_like(acc_ref)

        acc_ref[...] += dot()

        @pl.when(kk == nk - 1)
        def _():
            emit(refs, acc_ref[...])

    res = pl.pallas_call(kern, name=name, grid=(m // tm, n // tn, nk), in_specs=in_specs, out_specs=out_specs,
                         out_shape=out_shape, scratch_shapes=[] if nk == 1 else [pltpu.VMEM((tm, tn), F32)],
                         compiler_params=_params("arbitrary", "arbitrary", "arbitrary"))(*ins)
    return res if post else res[0]


def _rms_fwd(name, x, g):
    t, d = x.shape
    tt = _pick(t, (ROW_TILE, LANE))

    def kern(x_ref, g_ref, h_ref, ht_ref):
        xv = x_ref[...]
        h = xv * _rms_r(xv) * g_ref[...]
        h_ref[...] = h.astype(BF16)
        ht_ref[...] = h.T.astype(BF16)

    return pl.pallas_call(kern, name=name, grid=(t // tt,),
                          in_specs=[pl.BlockSpec((tt, d), lambda i: (i, 0)), pl.BlockSpec((1, d), lambda i: (0, 0))],
                          out_specs=[pl.BlockSpec((tt, d), lambda i: (i, 0)), pl.BlockSpec((d, tt), lambda i: (0, i))],
                          out_shape=[jax.ShapeDtypeStruct((t, d), BF16), jax.ShapeDtypeStruct((d, t), BF16)],
                          compiler_params=_params("arbitrary"))(x, g)


def _ffn_up(name, h, w1, w3):
    t, d = h.shape
    ff = w1.shape[0]
    tm, tn = _pick(t, (1024, 512, 256, 128)), _pick(ff, (1024, 768, 512, 256, 128))

    def kern(h_ref, w1_ref, w3_ref, a_ref, b_ref, z_ref):
        hv = h_ref[...]
        a = lax.dot_general(hv, w1_ref[...], NT, preferred_element_type=F32)
        b = lax.dot_general(hv, w3_ref[...], NT, preferred_element_type=F32)
        a_ref[...] = a.astype(BF16)
        b_ref[...] = b.astype(BF16)
        z_ref[...] = (a * _sigmoid(a) * b).astype(BF16)

    w_spec = pl.BlockSpec((tn, d), lambda i, j: (j, 0))
    o_spec = pl.BlockSpec((tm, tn), lambda i, j: (i, j))
    return pl.pallas_call(kern, name=name, grid=(t // tm, ff // tn),
                          in_specs=[pl.BlockSpec((tm, d), lambda i, j: (i, 0)), w_spec, w_spec], out_specs=[o_spec] * 3,
                          out_shape=[jax.ShapeDtypeStruct((t, ff), BF16)] * 3,
                          compiler_params=_params("arbitrary", "arbitrary"))(h, w1, w3)


def _ffn_bwd_rows(name, dxo, w1, w3, w2, a, b, x, g, after=()):
    t, d = x.shape
    ff = w2.shape[0]
    tm = _pick(t, (FFN_ROWS, 128))
    cw = _pick(ff, (FFN_CHUNK, 128))

    def kern(dx_ref, w1_ref, w3_ref, w2_ref, a_ref, b_ref, x_ref, g_ref, *rest):
        da_ref, db_ref, dxo_ref, dg_ref = rest[-4:]

        @pl.when(pl.program_id(0) == 0)
        def _():
            dg_ref[...] = jnp.zeros_like(dg_ref)

        dxv = dx_ref[...]
        dxb = dxv.astype(BF16)
        dh = jnp.zeros((tm, d), F32)
        for c0 in range(0, ff, cw):
            cols = pl.ds(c0, cw)
            dz = lax.dot_general(dxb, w2_ref[cols, :], NT, preferred_element_type=F32) * 0.5
            av, bv = a_ref[:, cols].astype(F32), b_ref[:, cols].astype(F32)
            s = _sigmoid(av)
            da, db = (dz * bv * _dsilu(av, s)).astype(BF16), (dz * av * s).astype(BF16)
            da_ref[:, cols] = da
            db_ref[:, cols] = db
            dh = dh + jnp.dot(da, w1_ref[cols, :], preferred_element_type=F32)
            dh = dh + jnp.dot(db, w3_ref[cols, :], preferred_element_type=F32)
        xv = x_ref[...]
        r = _rms_r(xv)
        dxo_ref[...] = dxv + _rms_bwd(xv, r, g_ref[...], dh)
        dg_ref[...] += _colsum(dh * xv * r)

    act = pl.BlockSpec((tm, ff), lambda i: (i, 0))
    wgt = pl.BlockSpec((ff, d), lambda i: (0, 0), pipeline_mode=pl.Buffered(1))
    rows = pl.BlockSpec((tm, d), lambda i: (i, 0))
    gain = pl.BlockSpec((1, d), lambda i: (0, 0))
    return pl.pallas_call(kern, name=name, grid=(t // tm,),
                          in_specs=[rows, wgt, wgt, wgt, act, act, rows, gain] + [ANY] * len(after),
                          out_specs=[act, act, rows, gain],
                          out_shape=[jax.ShapeDtypeStruct((t, ff), BF16)] * 2
                          + [jax.ShapeDtypeStruct((t, d), F32), jax.ShapeDtypeStruct((1, d), F32)],
                          compiler_params=_params("arbitrary"))(dxo, w1, w3, w2, a, b, x, g, *after)


def _ffn_bwd(tag, x, g, w1, w3, w2, saved, dxo, dxo_t):
    ht, a, b, z = saved
    dw2 = _mm(tag + "_dw2", dxo_t, z, 1, 0, BF16, alpha=0.5, transposed=True)
    s2 = _exchange_start(tag + "_w2_send", "scatter", [dw2], [0])
    da, db, dx, dg = _ffn_bwd_rows(tag + "_rows", dxo, w1, w3, w2, a, b, x, g, after=[s2[3]])
    dw1 = _mm(tag + "_dw1", ht, da, 1, 0, BF16, transposed=True)
    s1 = _exchange_start(tag + "_w1_send", "scatter", [dw1], [0])
    dw3 = _mm(tag + "_dw3", ht, db, 1, 0, BF16, after=[s1[3]], transposed=True)
    s3 = _exchange_start(tag + "_w3_send", "scatter", [dw3], [0])
    return dx, dg, {tag + "_w2": (s2, [0]), tag + "_w1": (s1, [0]), tag + "_w3": (s3, [0])}


def _shift_copies(ext_ref, sh_ref):
    n = ext_ref.shape[0] - SUBLANE
    for r in range(1, SUBLANE):
        sh_ref[r, pl.ds(0, n), :] = ext_ref[pl.ds(r, n), :]


def _rows_at(ext_ref, sh_ref, off, rows):
    r = off % SUBLANE
    return ext_ref[pl.ds(off, rows), :] if r == 0 else sh_ref[r, pl.ds(off - r, rows), :]


def _conv_fwd(proj, cw, cb, lng, lnb, og, seq):
    n_rows, c = proj.shape[0], cb.shape[1]
    kw = HALO - 1
    tt = _pick(seq, (CONV_TILE,))
    hb = tt // HALO

    def kern(v_ref, g_ref, vp_ref, gp_ref, w_ref, cb_ref, lg_ref, lb_ref, og_ref, c_ref, an_ref, ext_ref, sh_ref):
        first = (pl.program_id(0) * tt) % seq == 0
        ext_ref[pl.ds(HALO, tt), :] = v_ref[...] * _sigmoid(g_ref[...])
        ext_ref[pl.ds(0, HALO), :] = vp_ref[...] * _sigmoid(gp_ref[...]) * jnp.where(first, 0.0, 1.0)
        _shift_copies(ext_ref, sh_ref)
        for r0 in range(0, tt, CONV_SUB):
            rows = min(CONV_SUB, tt - r0)
            acc = jnp.zeros((rows, c), F32)
            for k in range(kw):
                acc = acc + w_ref[pl.ds(k, 1), :] * _rows_at(ext_ref, sh_ref, r0 + HALO - (kw - 1) + k, rows)
            c_ref[pl.ds(r0, rows), :] = acc + cb_ref[...]
        cv = c_ref[...]
        mu = jnp.mean(cv, axis=-1, keepdims=True)
        xc = cv - mu
        rstd = lax.rsqrt(jnp.mean(xc * xc, axis=-1, keepdims=True) + EPS)
        lv = xc * rstd * lg_ref[...] + lb_ref[...]
        sl = lv * _sigmoid(lv)
        an_ref[...] = (sl * _rms_r(sl) * og_ref[...]).astype(BF16)

    cur = lambda cbk: pl.BlockSpec((tt, c), lambda i: (i, cbk))
    prev = lambda cbk: pl.BlockSpec((HALO, c), lambda i: (jnp.maximum(i * hb - 1, 0), cbk))
    par = lambda p: pl.BlockSpec(p.shape, lambda i: (0, 0))
    return pl.pallas_call(
        kern, name="conv_fwd", grid=(n_rows // tt,),
        in_specs=[cur(0), cur(1), prev(0), prev(1), par(cw), par(cb), par(lng), par(lnb), par(og)],
        out_specs=[pl.BlockSpec((tt, c), lambda i: (i, 0))] * 2,
        out_shape=[jax.ShapeDtypeStruct((n_rows, c), F32), jax.ShapeDtypeStruct((n_rows, c), BF16)],
        scratch_shapes=[pltpu.VMEM((tt + HALO, c), F32), pltpu.VMEM((SUBLANE, tt + HALO, c), F32)],
        compiler_params=_params("arbitrary"),
    )(proj, proj, proj, proj, cw, cb, lng, lnb, og)


def _conv_bwd_rows(dmixed, cpre, lng, lnb, og):
    c = cpre.shape[1]

    def body(ins, outs, accs):
        dan, cv, lg, lb, ogv = ins[0][...], ins[1][...], ins[2][...], ins[3][...], ins[4][...]
        mu = jnp.mean(cv, axis=-1, keepdims=True)
        xc = cv - mu
        rstd = lax.rsqrt(jnp.mean(xc * xc, axis=-1, keepdims=True) + EPS)
        xh = xc * rstd
        lv = xh * lg + lb
        s = _sigmoid(lv)
        sl = lv * s
        r2 = _rms_r(sl)
        accs[0][...] += _colsum(dan * sl * r2)
        dl = _rms_bwd(sl, r2, ogv, dan) * _dsilu(lv, s)
        accs[1][...] += _colsum(dl * xh)
        accs[2][...] += _colsum(dl)
        dxh = dl * lg
        dc = rstd * (dxh - jnp.mean(dxh, axis=-1, keepdims=True) - xh * jnp.mean(dxh * xh, axis=-1, keepdims=True))
        outs[0][...] = dc
        accs[3][...] += _colsum(dc)

    return _rowwise("conv_bwd_rows", body, cpre.shape[0], [(dmixed, c, 0), (cpre, c, 0)], [lng, lnb, og], [(c, F32)],
                    [(1, c)] * 4)


def _conv_bwd_taps(proj, dc, cw, seq):
    n_rows, c = dc.shape
    kw = HALO - 1
    tt = _pick(seq, (CONV_TILE,))
    hb = tt // HALO
    last_blk = n_rows // HALO - 1

    def kern(v_ref, g_ref, vp_ref, gp_ref, dc_ref, dn_ref, w_ref, dv_ref, dg_ref, dw_ref, exta_ref, extd_ref, sha_ref, shd_ref):
        i = pl.program_id(0)
        first = (i * tt) % seq == 0
        last = ((i + 1) * tt) % seq == 0

        @pl.when(i == 0)
        def _():
            dw_ref[...] = jnp.zeros_like(dw_ref)

        sg = _sigmoid(g_ref[...])
        exta_ref[pl.ds(HALO, tt), :] = v_ref[...] * sg
        exta_ref[pl.ds(0, HALO), :] = vp_ref[...] * _sigmoid(gp_ref[...]) * jnp.where(first, 0.0, 1.0)
        dcv = dc_ref[...]
        extd_ref[pl.ds(0, tt), :] = dcv
        extd_ref[pl.ds(tt, HALO), :] = dn_ref[...] * jnp.where(last, 0.0, 1.0)
        _shift_copies(exta_ref, sha_ref)
        _shift_copies(extd_ref, shd_ref)
        for k in range(kw):
            dw_ref[pl.ds(k, 1), :] += _colsum(_rows_at(exta_ref, sha_ref, HALO - (kw - 1) + k, tt) * dcv)
        for r0 in range(0, tt, CONV_SUB):
            rows = min(CONV_SUB, tt - r0)
            acc = jnp.zeros((rows, c), F32)
            for k in range(kw):
                acc = acc + w_ref[pl.ds(k, 1), :] * _rows_at(extd_ref, shd_ref, r0 + (kw - 1) - k, rows)
            dv_ref[pl.ds(r0, rows), :] = acc
        da = dv_ref[...]
        dv_ref[...] = da * sg
        dg_ref[...] = da * v_ref[...] * sg * (1.0 - sg)

    cur = lambda cbk: pl.BlockSpec((tt, c), lambda i: (i, cbk))
    prev = lambda cbk: pl.BlockSpec((HALO, c), lambda i: (jnp.maximum(i * hb - 1, 0), cbk))
    nxt = pl.BlockSpec((HALO, c), lambda i: (jnp.minimum((i + 1) * hb, last_blk), 0))
    return pl.pallas_call(
        kern, name="conv_bwd_taps", grid=(n_rows // tt,),
        in_specs=[cur(0), cur(1), prev(0), prev(1), cur(0), nxt, pl.BlockSpec(cw.shape, lambda i: (0, 0))],
        out_specs=[cur(0), cur(0), pl.BlockSpec((HALO, c), lambda i: (0, 0))],
        out_shape=[jax.ShapeDtypeStruct((n_rows, c), F32), jax.ShapeDtypeStruct((n_rows, c), F32),
                   jax.ShapeDtypeStruct((HALO, c), F32)],
        scratch_shapes=[pltpu.VMEM((tt + HALO, c), F32)] * 2 + [pltpu.VMEM((SUBLANE, tt + HALO, c), F32)] * 2,
        compiler_params=_params("arbitrary"),
    )(proj, proj, proj, proj, dc, dc, cw)


def _s5_params_fwd(lr, li, ldt, btr, bti, seg):
    ns = lr.shape[1]

    def kern(lr_ref, li_ref, ldt_ref, btr_ref, bti_ref, ar_ref, ai_ref, bbr_ref, bbi_ref, ps_ref, psf_ref, pc_ref, pcf_ref):
        lrv, liv = lr_ref[...], li_ref[...]
        dt = jnp.exp(ldt_ref[...])
        zr, zi = lrv * dt, liv * dt
        mag = jnp.exp(zr)
        ar, ai = mag * jnp.cos(zi), mag * jnp.sin(zi)
        den = lrv * lrv + liv * liv
        nr = ar - 1.0
        cr = (nr * lrv + ai * liv) / den
        ci = (ai * lrv - nr * liv) / den
        ar_ref[...] = ar
        ai_ref[...] = ai
        bbr_ref[...] = cr * btr_ref[...] - ci * bti_ref[...]
        bbi_ref[...] = cr * bti_ref[...] + ci * btr_ref[...]
        def powers(br, bi, count, up_ref, down_ref):
            pr, pi = br, bi
            for e in range(count):
                for ref, at in ((up_ref, e), (down_ref, count - 1 - e)):
                    ref[pl.ds(at, 1), pl.ds(0, ns)] = pr
                    ref[pl.ds(at, 1), pl.ds(ns, ns)] = pi
                if e < count - 1:
                    pr, pi = pr * br - pi * bi, pr * bi + pi * br
            return pr, pi

        powers(*powers(ar, ai, seg, ps_ref, psf_ref), SUBLANE, pc_ref, pcf_ref)

    h = btr.shape[0]
    shapes = [jax.ShapeDtypeStruct((1, ns), F32)] * 2 + [jax.ShapeDtypeStruct((h, ns), F32)] * 2
    shapes += [jax.ShapeDtypeStruct((seg, 2 * ns), F32)] * 2 + [jax.ShapeDtypeStruct((SUBLANE, 2 * ns), F32)] * 2
    return pl.pallas_call(kern, name="s5_params_fwd", out_shape=shapes)(lr, li, ldt, btr, bti)


def _s5_params_bwd(lr, li, ldt, btr, bti, dar, dai, dbbr, dbbi):
    def kern(lr_ref, li_ref, ldt_ref, btr_ref, bti_ref, dar_ref, dai_ref, dbr_ref, dbi_ref,
             dlr_ref, dli_ref, dldt_ref, dbtr_ref, dbti_ref):
        lrv, liv = lr_ref[...], li_ref[...]
        dt = jnp.exp(ldt_ref[...])
        zr, zi = lrv * dt, liv * dt
        mag = jnp.exp(zr)
        ar, ai = mag * jnp.cos(zi), mag * jnp.sin(zi)
        den = lrv * lrv + liv * liv
        nr = ar - 1.0
        cr = (nr * lrv + ai * liv) / den
        ci = (ai * lrv - nr * liv) / den
        dbr, dbi, br, bi = dbr_ref[...], dbi_ref[...], btr_ref[...], bti_ref[...]
        dbtr_ref[...] = cr * dbr + ci * dbi
        dbti_ref[...] = cr * dbi - ci * dbr
        dcr = _colsum(br * dbr + bi * dbi)
        dci = _colsum(br * dbi - bi * dbr)
        ir, ii = lrv / den, -liv / den
        dnr = ir * dcr + ii * dci
        dni = ir * dci - ii * dcr
        wr, wi = cr * ir - ci * ii, cr * ii + ci * ir
        dl1r = -(wr * dcr + wi * dci)
        dl1i = -(wr * dci - wi * dcr)
        dtr, dti = dar_ref[...] + dnr, dai_ref[...] + dni
        dzr = ar * dtr + ai * dti
        dzi = ar * dti - ai * dtr
        dlr_ref[...] = dl1r + dt * dzr
        dli_ref[...] = dl1i + dt * dzi
        dldt_ref[...] = (dzr * lrv + dzi * liv) * dt

    ns, h = lr.shape[1], btr.shape[0]
    shapes = [jax.ShapeDtypeStruct((1, ns), F32)] * 3 + [jax.ShapeDtypeStruct((h, ns), F32)] * 2
    return pl.pallas_call(kern, name="s5_params_bwd", out_shape=shapes)(lr, li, ldt, btr, bti, dar, dai, dbbr, dbbi)


def _to_segments(nat_ref, seg_ref):
    steps = nat_ref.shape[0] // SUBLANE
    _regroup(nat_ref, seg_ref, lambda r: (r % SUBLANE) * steps + r // SUBLANE)


def _from_segments(seg_ref, nat_ref):
    steps = nat_ref.shape[0] // SUBLANE
    _regroup(seg_ref, nat_ref, lambda r: (r % steps) * SUBLANE + r // steps)


def _regroup(src_ref, dst_ref, src_row):
    rows, width = dst_ref.shape
    sublane = lax.broadcasted_iota(jnp.int32, (SUBLANE, width), 0)
    for r0 in range(0, rows, SUBLANE):
        tile = jnp.broadcast_to(src_ref[pl.ds(src_row(r0), 1), :], (SUBLANE, width))
        for k in range(1, SUBLANE):
            tile = jnp.where(sublane == k, src_ref[pl.ds(src_row(r0 + k), 1), :], tile)
        dst_ref[pl.ds(r0, SUBLANE), :] = tile


def _scan_tile(s_ref, o_ref, fix_ref, tabs, car_ref, sb, reverse, x_ref=None, acc_ref=None):
    l1, l2, l4, pw = tabs
    rows_t, w = s_ref.shape
    steps = rows_t // SUBLANE
    cw = _pick(sb, (SCAN_COLS,))
    last = 0 if reverse else SUBLANE - 1
    first = SUBLANE - 1 - last
    row = lax.broadcasted_iota(jnp.int32, (SUBLANE, cw), 0)
    step_rows = lambda i: pl.ds(pl.multiple_of(((steps - 1 - i) if reverse else i) * SUBLANE, SUBLANE), SUBLANE)
    zero = jnp.zeros((SUBLANE, cw), F32)

    for c0 in [b0 + o for b0 in range(0, w, 2 * sb) for o in range(0, sb, cw)]:
        cr, ci = pl.ds(c0, cw), pl.ds(c0 + sb, cw)
        base = pl.ds(((steps - 1) if reverse else 0) * SUBLANE, SUBLANE)
        ar, ai = fix_ref[base, cr], fix_ref[base, ci]

        def run(i, state):
            xr, xi = state
            rows = step_rows(i)
            xr, xi = ar * xr - ai * xi + s_ref[rows, cr], ar * xi + ai * xr + s_ref[rows, ci]
            o_ref[rows, cr] = xr
            o_ref[rows, ci] = xi
            return xr, xi

        fr, fi = lax.fori_loop(0, steps, run, (zero, zero))
        for s, lt in ((1, l1), (2, l2), (4, l4)):
            sh = (SUBLANE - s) if reverse else s
            sr, si = pltpu.roll(fr, sh, 0), pltpu.roll(fi, sh, 0)
            tr, ti = lt[:, cr], lt[:, ci]
            fr, fi = fr + tr * sr - ti * si, fi + tr * si + ti * sr
        kr, ki = car_ref[pl.ds(last, 1), cr], car_ref[pl.ds(last, 1), ci]
        pr, pi = pw[:, cr], pw[:, ci]
        fr, fi = fr + pr * kr - pi * ki, fi + pr * ki + pi * kr
        car_ref[:, cr] = fr
        car_ref[:, ci] = fi
        to_next = 1 if not reverse else SUBLANE - 1
        gr = jnp.where(row == first, kr, pltpu.roll(fr, to_next, 0))
        gi = jnp.where(row == first, ki, pltpu.roll(fi, to_next, 0))

        def fix(i, state):
            rows = step_rows(i)
            qr, qi = fix_ref[rows, cr], fix_ref[rows, ci]
            yr = o_ref[rows, cr] + qr * gr - qi * gi
            yi = o_ref[rows, ci] + qr * gi + qi * gr
            o_ref[rows, cr] = yr
            o_ref[rows, ci] = yi
            if acc_ref is None:
                return state
            nr, ni, sr, si = state
            pxr, pxi = x_ref[rows, cr], x_ref[rows, ci]
            return yr, yi, sr + nr * pxr + ni * pxi, si + ni * pxr - nr * pxi

        if acc_ref is None:
            lax.fori_loop(0, steps, fix, 0)
        else:
            _, _, sr, si = lax.fori_loop(0, steps, fix, (gr, gi, zero, zero))
            acc_ref[:, cr] += sr
            acc_ref[:, ci] += si


def _s5_fwd(proj, u_blk, bdc, cdc, fix, tabs, dskip, seq, sb):
    n_rows = proj.shape[0]
    nb, blk, w_blk = bdc.shape
    c, w = nb * blk, nb * w_blk
    tt = fix.shape[0]

    def kern(u_ref, bd_ref, cd_ref, fix_ref, l1, l2, l4, pw, d_ref, xs_ref, yp_ref, yg_ref, us_ref, bu_ref, car_ref):
        @pl.when((pl.program_id(0) * tt) % seq == 0)
        def _():
            car_ref[...] = jnp.zeros_like(car_ref)

        _to_segments(u_ref, us_ref)
        for j in range(nb):
            bu_ref[:, pl.ds(j * w_blk, w_blk)] = jnp.dot(us_ref[:, pl.ds(j * blk, blk)].astype(BF16), bd_ref[j],
                                                         preferred_element_type=F32)
        _scan_tile(bu_ref, xs_ref, fix_ref, (l1, l2, l4, pw), car_ref, sb, False)
        for j in range(nb):
            cols = pl.ds(j * blk, blk)
            y0 = jnp.dot(xs_ref[:, pl.ds(j * w_blk, w_blk)].astype(BF16), cd_ref[j], preferred_element_type=F32)
            us_ref[:, cols] = y0 + d_ref[:, cols] * us_ref[:, cols]
        _from_segments(us_ref, yp_ref)
        yg_ref[...] = _gelu(yp_ref[...]).astype(BF16)

    tab = pl.BlockSpec((SUBLANE, w), lambda i: (0, 0))
    rows = pl.BlockSpec((tt, c), lambda i: (i, 0))
    return pl.pallas_call(
        kern, name="s5_fwd", grid=(n_rows // tt,),
        in_specs=[pl.BlockSpec((tt, c), lambda i: (i, u_blk * blk // c)), pl.BlockSpec(bdc.shape, lambda i: (0, 0, 0)),
                  pl.BlockSpec(cdc.shape, lambda i: (0, 0, 0)), pl.BlockSpec((tt, w), lambda i: (0, 0)), tab, tab, tab, tab,
                  pl.BlockSpec((1, c), lambda i: (0, 0))],
        out_specs=[pl.BlockSpec((tt, w), lambda i: (i, 0)), rows, rows],
        out_shape=[jax.ShapeDtypeStruct((n_rows, w), F32), jax.ShapeDtypeStruct((n_rows, c), F32),
                   jax.ShapeDtypeStruct((n_rows, c), BF16)],
        scratch_shapes=[pltpu.VMEM((tt, c), F32), pltpu.VMEM((tt, w), F32), pltpu.VMEM((SUBLANE, w), F32)],
        compiler_params=_params("arbitrary"))(proj, bdc, cdc, fix, *tabs, dskip)


def _s5_bwd(dypre, du_skip, xs, proj, u_blk, bdc, cdc, fix, tabs, seq, sb):
    n_rows = proj.shape[0]
    nb, blk, w_blk = bdc.shape
    c, w = nb * blk, nb * w_blk
    tt = fix.shape[0]
    nt = n_rows // tt
    tn = (((0,), (0,)), ((), ()))

    def kern(dy_ref, ds_ref, x_ref, u_ref, bd_ref, cd_ref, fix_ref, l1, l2, l4, pw, du_ref, da_ref, db_ref, dc_ref,
             dys_ref, us_ref, dus_ref, gx_ref, lam_ref, car_ref, acc_ref):
        i = pl.program_id(0)

        @pl.when(((nt - i) * tt) % seq == 0)
        def _():
            car_ref[...] = jnp.zeros_like(car_ref)

        @pl.when(i == 0)
        def _():
            acc_ref[...] = jnp.zeros_like(acc_ref)
            db_ref[...] = jnp.zeros_like(db_ref)
            dc_ref[...] = jnp.zeros_like(dc_ref)

        _to_segments(dy_ref, dys_ref)
        _to_segments(u_ref, us_ref)
        for j in range(nb):
            gx_ref[:, pl.ds(j * w_blk, w_blk)] = lax.dot_general(dys_ref[:, pl.ds(j * blk, blk)].astype(BF16), cd_ref[j], NT,
                                                                 preferred_element_type=F32)
        _scan_tile(gx_ref, lam_ref, fix_ref, (l1, l2, l4, pw), car_ref, sb, True, x_ref, acc_ref)
        for j in range(nb):
            cols, wide = pl.ds(j * blk, blk), pl.ds(j * w_blk, w_blk)
            lam = lam_ref[:, wide].astype(BF16)
            dus_ref[:, cols] = lax.dot_general(lam, bd_ref[j], NT, preferred_element_type=F32)
            db_ref[j] += lax.dot_general(us_ref[:, cols].astype(BF16), lam, tn, preferred_element_type=F32)
            dc_ref[j] += lax.dot_general(x_ref[:, wide].astype(BF16), dys_ref[:, cols].astype(BF16), tn,
                                         preferred_element_type=F32)
        _from_segments(dus_ref, du_ref)
        du_ref[...] += ds_ref[...]

        @pl.when(i == nt - 1)
        def _():
            da_ref[...] = _colsum(acc_ref[...])

    back = lambda i: (nt - 1 - i, 0)
    tab = pl.BlockSpec((SUBLANE, w), lambda i: (0, 0))
    rows = pl.BlockSpec((tt, c), back)
    whole = lambda a: pl.BlockSpec(a.shape, lambda i: (0, 0, 0))
    return pl.pallas_call(
        kern, name="s5_bwd", grid=(nt,),
        in_specs=[rows, rows, pl.BlockSpec((tt, w), back), pl.BlockSpec((tt, c), lambda i: (nt - 1 - i, u_blk * blk // c)),
                  whole(bdc), whole(cdc), pl.BlockSpec((tt, w), lambda i: (0, 0)), tab, tab, tab, tab],
        out_specs=[rows, pl.BlockSpec((1, w), lambda i: (0, 0)), whole(bdc), whole(cdc)],
        out_shape=[jax.ShapeDtypeStruct((n_rows, c), F32), jax.ShapeDtypeStruct((1, w), F32),
                   jax.ShapeDtypeStruct(bdc.shape, F32), jax.ShapeDtypeStruct(cdc.shape, F32)],
        scratch_shapes=[pltpu.VMEM((tt, c), F32)] * 3 + [pltpu.VMEM((tt, w), F32)] * 2 + [pltpu.VMEM((SUBLANE, w), F32)] * 2,
        compiler_params=_params("arbitrary"))(dypre, du_skip, xs, proj, bdc, cdc, fix, *tabs)


def _s5_post2(yg, q0, bg, og):
    c = yg.shape[1]

    def body(ins, outs, accs):
        ygv = ins[0][...].astype(F32)
        sg = ygv * _sigmoid(ins[1][...] + ins[2][...])
        outs[0][...] = (sg * _rms_r(sg) * ins[3][...]).astype(BF16)

    return _rowwise("s5_post2", body, yg.shape[0], [(yg, c, 0), (q0, c, 0)], [bg, og], [(c, BF16)], [])[0]


def _s5_post2_bwd(dmixed, yg, q0, bg, og):
    c = yg.shape[1]

    def body(ins, outs, accs):
        dsn, ygv = ins[0][...], ins[1][...].astype(F32)
        s = _sigmoid(ins[2][...] + ins[3][...])
        sg = ygv * s
        r = _rms_r(sg)
        accs[0][...] += _colsum(dsn * sg * r)
        dsg = _rms_bwd(sg, r, ins[4][...], dsn)
        dq = dsg * ygv * s * (1.0 - s)
        outs[0][...] = dq.astype(BF16)
        outs[1][...] = dsg * s
        accs[1][...] += _colsum(dq)

    return _rowwise("s5_post2_bwd", body, yg.shape[0], [(dmixed, c, 1), (yg, c, 0), (q0, c, 0)], [bg, og],
                    [(c, BF16), (c, F32)], [(1, c)] * 2)


def _s5_post1_bwd(dyg1, dyg2, ypre, proj, dskip, after=()):
    c = ypre.shape[1]

    def body(ins, outs, accs):
        dyp = (ins[0][...] + ins[1][...]) * _dgelu(ins[2][...])
        outs[0][...] = dyp
        outs[1][...] = dyp * ins[4][...]
        accs[0][...] += _colsum(dyp * ins[3][...])

    return _rowwise("s5_post1_bwd", body, ypre.shape[0], [(dyg1, c, 0), (dyg2, c, 0), (ypre, c, 0), (proj, c, 2)], [dskip],
                    [(c, F32), (c, F32)], [(1, c)], after=after)


def _place():
    return lax.axis_index("x"), lax.axis_index("y"), lax.axis_index("c")


def _window(ref, axis, q, rows, cols):
    if axis == 0:
        return ref.at[pl.ds(pl.multiple_of(q * rows, SUBLANE), rows), :]
    return ref.at[:, pl.ds(pl.multiple_of(q * cols, LANE), cols)]


ALL_RELS = [(fx, fy, fc) for fx in (0, 1) for fy in (0, 1) for fc in (0, 1)][1:]
N_PEERS = {"gather": 3, "scatter": 3, "sibling": 1, "all": len(ALL_RELS)}


def _copies(kind, srcs, lands, shards, axes, send_sems, recv_sems, local_sems):
    x, y, c = _place()
    me, dev = 2 * x + y, 4 * x + 2 * y + c
    n_peers = N_PEERS[kind]
    starts, waits = [], []
    for a, (src, land) in enumerate(zip(srcs, lands)):
        on = lambda k, peer: dict(send_sem=send_sems.at[n_peers * a + k], recv_sem=recv_sems.at[n_peers * a + k],
                                  device_id=peer, device_id_type=MESH)
        if kind == "sibling":
            cp = pltpu.make_async_remote_copy(src_ref=src, dst_ref=land, **on(0, (x, y, 1 - c)))
            starts.append(cp)
            waits.append(cp)
            continue
        if kind == "all":
            own = pltpu.make_async_copy(src, land.at[dev], local_sems.at[a])
            starts.append(own)
            waits.append(own)
            for k, (fx, fy, fc) in enumerate(ALL_RELS):
                px, py, pc = (1 - x) if fx else x, (1 - y) if fy else y, (1 - c) if fc else c
                starts.append(pltpu.make_async_remote_copy(src_ref=src, dst_ref=land.at[dev], **on(k, (px, py, pc))))
                waits.append(pltpu.make_async_remote_copy(src_ref=src, dst_ref=land.at[4 * px + 2 * py + pc],
                                                          **on(k, (px, py, pc))))
            continue
        rows, cols = shards[a]
        if kind == "gather":
            own = pltpu.make_async_copy(src, _window(land, axes[a], me, rows, cols), local_sems.at[a])
        else:
            own = pltpu.make_async_copy(_window(src, axes[a], me, rows, cols), land.at[3], local_sems.at[a])
        starts.append(own)
        waits.append(own)
        for j, (fx, fy) in enumerate(CHIP_RELS):
            px, py = (1 - x) if fx else x, (1 - y) if fy else y
            peer = 2 * px + py
            if kind == "gather":
                starts.append(pltpu.make_async_remote_copy(src_ref=src, dst_ref=_window(land, axes[a], me, rows, cols),
                                                           **on(j, (px, py, c))))
                waits.append(pltpu.make_async_remote_copy(src_ref=src, dst_ref=_window(land, axes[a], peer, rows, cols),
                                                          **on(j, (px, py, c))))
            else:
                cp = pltpu.make_async_remote_copy(src_ref=_window(src, axes[a], peer, rows, cols), dst_ref=land.at[j],
                                                  **on(j, (px, py, c)))
                starts.append(cp)
                waits.append(cp)
    return starts, waits


HBM = pl.BlockSpec(memory_space=pltpu.HBM)
SEM = pl.BlockSpec(memory_space=pltpu.SEMAPHORE)


def _shard_shapes(kind, arrs, axes):
    if kind != "scatter":
        return [a.shape for a in arrs]
    return [(a.shape[0] // N_CHIPS, a.shape[1]) if ax == 0 else (a.shape[0], a.shape[1] // N_CHIPS) for a, ax in zip(arrs, axes)]


def _land_shapes(kind, arrs, axes):
    if kind == "gather":
        return [(N_CHIPS * a.shape[0], a.shape[1]) if ax == 0 else (a.shape[0], N_CHIPS * a.shape[1]) for a, ax in zip(arrs, axes)]
    if kind == "scatter":
        return [(N_CHIPS,) + s for s in _shard_shapes(kind, arrs, axes)]
    return [a.shape if kind == "sibling" else (len(ALL_RELS) + 1,) + a.shape for a in arrs]


def _exchange_start(name, kind, arrs, axes, after=()):
    n, n_after = len(arrs), len(after)
    shards = _shard_shapes(kind, arrs, axes)
    land_shapes = _land_shapes(kind, arrs, axes)
    lands = [lax.empty(s, a.dtype) for s, a in zip(land_shapes, arrs)]

    def kern(*refs):
        outs = refs[2 * n + n_after:]
        starts, _ = _copies(kind, refs[:n], refs[n:2 * n], shards, axes, outs[0], outs[1], outs[2])
        for cp in starts:
            cp.start()
        outs[-1][...] = jnp.zeros_like(outs[-1])

    kept = [pltpu.HBM(a.shape, a.dtype) for a in arrs] + [pltpu.HBM(s, a.dtype) for s, a in zip(land_shapes, arrs)]
    n_sems = N_PEERS[kind] * n
    res = pl.pallas_call(
        kern, name=name, in_specs=[HBM] * (2 * n) + [ANY] * n_after,
        out_specs=[SEM] * 3 + [HBM] * (2 * n) + [pl.BlockSpec(memory_space=pltpu.VMEM)],
        out_shape=[pltpu.SemaphoreType.DMA((n_sems,)), pltpu.SemaphoreType.DMA((n_sems,)), pltpu.SemaphoreType.DMA((n,))]
        + kept + [jax.ShapeDtypeStruct((SUBLANE, LANE), F32)],
        input_output_aliases={i: 3 + i for i in range(2 * n)},
        compiler_params=pltpu.CompilerParams(has_side_effects=pltpu.SideEffectType.DATAFLOW_SIDE_EFFECTING),
    )(*[pltpu.with_memory_space_constraint(a, pltpu.HBM) for a in list(arrs) + lands], *after)
    return res[:3], res[3:3 + n], res[3 + n:3 + 2 * n], res[-1]


def _exchange_wait(name, kind, started, axes, after, sources_too=False):
    sems, srcs, lands, _ = started
    n, n_after = len(srcs), len(after)
    shards = _shard_shapes(kind, srcs, axes)

    def kern(*refs):
        sem_refs = refs[2 * n:2 * n + 3]
        _, waits = _copies(kind, refs[:n], refs[n:2 * n], shards, axes, *sem_refs)
        for cp in waits:
            cp.wait()

    res = pl.pallas_call(
        kern, name=name, in_specs=[HBM] * (2 * n) + [SEM] * 3 + [ANY] * n_after, out_specs=[HBM] * (2 * n),
        out_shape=[pltpu.HBM(a.shape, a.dtype) for a in list(srcs) + list(lands)],
        input_output_aliases={i: i for i in range(2 * n)},
        compiler_params=pltpu.CompilerParams(has_side_effects=pltpu.SideEffectType.DATAFLOW_SIDE_EFFECTING),
    )(*srcs, *lands, *sems, *after)
    return (res[:n], res[n:]) if sources_too else res[n:]


def _sum_devices(parts):
    def kern(p_ref, o_ref):
        acc = p_ref[0]
        for d in range(1, parts.shape[0]):
            acc = acc + p_ref[d]
        o_ref[...] = acc

    return pl.pallas_call(kern, name="sum_devices", out_shape=jax.ShapeDtypeStruct(parts.shape[1:], F32),
                          compiler_params=pltpu.CompilerParams(vmem_limit_bytes=VMEM_LIMIT_BYTES))(parts)


def _sum_slots(name, parts):
    _, rows, cols = parts.shape
    tr = _pick(rows, (ROW_TILE, 128, 64, 32))

    def kern(p_ref, o_ref):
        o_ref[...] = ((p_ref[3].astype(F32) + p_ref[0].astype(F32)) + p_ref[1].astype(F32)) + p_ref[2].astype(F32)

    return pl.pallas_call(kern, name=name, grid=(rows // tr,),
                          in_specs=[pl.BlockSpec((N_CHIPS, tr, cols), lambda i: (0, i, 0))],
                          out_specs=pl.BlockSpec((tr, cols), lambda i: (i, 0)),
                          out_shape=jax.ShapeDtypeStruct((rows, cols), F32), compiler_params=_params("arbitrary"))(parts)


def _adamw_math(g, w, m, v):
    m2 = ADAM_B1 * m + (1.0 - ADAM_B1) * g
    v2 = ADAM_B2 * v + (1.0 - ADAM_B2) * (g * g)
    m_hat = m2 / (1.0 - ADAM_B1 ** ADAM_STEP)
    v_hat = v2 / (1.0 - ADAM_B2 ** ADAM_STEP)
    return -ADAM_LR * (m_hat / (jnp.sqrt(v_hat) + ADAM_EPS) + ADAM_WD * w), m2, v2


def _adamw(name, parts, w, m, v):
    rows, cols = w.shape
    tr = rows if rows * cols <= WHOLE_ELEMS else _pick(rows, (ROW_TILE, 352, 128, 64, 32, 8))
    n = len(parts)

    def kern(*refs):
        g = refs[0][:, pl.ds(0, cols)]
        for p in refs[1:n]:
            g = g + p[:, pl.ds(0, cols)]
        d, m2, v2 = _adamw_math(g, refs[n][...], refs[n + 1][...], refs[n + 2][...])
        refs[n + 3][...] = g
        refs[n + 4][...] = d
        refs[n + 5][...] = m2
        refs[n + 6][...] = v2

    spec = pl.BlockSpec((tr, cols), lambda i: (i, 0))
    return pl.pallas_call(kern, name=name, grid=(rows // tr,),
                          in_specs=[pl.BlockSpec((tr, p.shape[1]), lambda i: (i, 0)) for p in parts] + [spec] * 3,
                          out_specs=[spec] * 4, out_shape=[jax.ShapeDtypeStruct((rows, cols), F32)] * 4,
                          compiler_params=_params("arbitrary"))(*parts, w, m, v)


def _adamw_many(name, gs, ws, ms, vs):
    n = len(gs)

    def kern(*refs):
        for p in range(n):
            d, m2, v2 = _adamw_math(refs[p][...], refs[n + p][...], refs[2 * n + p][...], refs[3 * n + p][...])
            refs[4 * n + p][...] = d
            refs[5 * n + p][...] = m2
            refs[6 * n + p][...] = v2

    res = pl.pallas_call(kern, name=name, out_shape=[jax.ShapeDtypeStruct(w.shape, F32) for w in ws] * 3,
                         compiler_params=pltpu.CompilerParams(vmem_limit_bytes=VMEM_LIMIT_BYTES))(*gs, *ws, *ms, *vs)
    return res[:n], res[n:2 * n], res[2 * n:]


def _pack(arrs):
    parts, rows = [], []
    for a in arrs:
        r = _round_up(-(-a.size // LANE), SUBLANE)
        parts.append(jnp.pad(a.reshape(-1).astype(F32), (0, r * LANE - a.size)).reshape(r, LANE))
        rows.append(r)
    return jnp.concatenate(parts, axis=0), rows


def _unpack(buf, rows, shapes):
    out, r0 = [], 0
    for r, s in zip(rows, shapes):
        size = math.prod(s)
        out.append(buf[r0:r0 + r].reshape(-1)[:size].reshape(s))
        r0 += r
    return out


def kernel(x, norm_ffn1, ffn1_w1, ffn1_w3, ffn1_w2, norm_mix, w_in, conv_w, conv_b, conv_ln_g, conv_ln_b, conv_out_g, ssm_A_re, ssm_A_im, ssm_log_dt, ssm_B_re, ssm_B_im, ssm_C_re, ssm_C_im, ssm_D, ssm_glu_w, ssm_glu_b, ssm_out_g, w_out, norm_ffn2, ffn2_w1, ffn2_w3, ffn2_w2, norm_final, loss_target, m_norm_ffn1, m_ffn1_w1, m_ffn1_w3, m_ffn1_w2, m_norm_mix, m_w_in, m_conv_w, m_conv_b, m_conv_ln_g, m_conv_ln_b, m_conv_out_g, m_ssm_A_re, m_ssm_A_im, m_ssm_log_dt, m_ssm_B_re, m_ssm_B_im, m_ssm_C_re, m_ssm_C_im, m_ssm_D, m_ssm_glu_w, m_ssm_glu_b, m_ssm_out_g, m_w_out, m_norm_ffn2, m_ffn2_w1, m_ffn2_w3, m_ffn2_w2, m_norm_final, v_norm_ffn1, v_ffn1_w1, v_ffn1_w3, v_ffn1_w2, v_norm_mix, v_w_in, v_conv_w, v_conv_b, v_conv_ln_g, v_conv_ln_b, v_conv_out_g, v_ssm_A_re, v_ssm_A_im, v_ssm_log_dt, v_ssm_B_re, v_ssm_B_im, v_ssm_C_re, v_ssm_C_im, v_ssm_D, v_ssm_glu_w, v_ssm_glu_b, v_ssm_out_g, v_w_out, v_norm_ffn2, v_ffn2_w1, v_ffn2_w3, v_ffn2_w2, v_norm_final):
    given = dict(locals())
    wts = {n: given[n] for n in WEIGHTS}
    n_seq, seq, d = x.shape
    n_rows = n_seq * seq
    xf = x.reshape(n_rows, d)
    tgt = loss_target.reshape(n_rows, d)
    row = lambda a: a.reshape(1, -1)

    f = ffn1_w1.shape[-1]
    fp = _round_up(f, LANE)
    held = lambda n, a: a[0].T if n in TRANSPOSED else a[0]
    shards = []
    for n in BIG:
        s = held(n, wts[n]).astype(BF16)
        if n.startswith('ffn'):
            s = jnp.pad(s, ((0, fp - f), (0, 0)))
        shards.append(s)
    n_taps, c_shard = conv_w.shape[1], conv_w.shape[2]
    shards.append(jnp.pad(conv_w[0], ((0, HALO - n_taps), (0, 0))))
    shard_of = dict(zip(BIG + ['conv_w'], shards))
    axis_of = dict(BIG_AXIS, conv_w=1)
    groups = [['ffn1_w1', 'ffn1_w3'], ['ffn1_w2', 'w_in', 'conv_w', 'ssm_glu_w', 'w_out'], ['ffn2_w1', 'ffn2_w3', 'ffn2_w2']]
    fetch, tok = [], []
    for k, names in enumerate(groups):
        fetch.append(_exchange_start("gather%d_send" % k, "gather", [shard_of[n] for n in names],
                                     [axis_of[n] for n in names], tok))
        tok = [fetch[-1][3]]
    full = {}

    def arrive(k, after):
        lands = _exchange_wait("gather%d_recv" % k, "gather", fetch[k], [axis_of[n] for n in groups[k]], after)
        full.update(zip(groups[k], lands))

    h1, h1_t = _rms_fwd("ffn1_rms", xf, norm_ffn1)
    arrive(0, tok + [h1])

    _, n_grp, n_state = ssm_A_re.shape
    grp = ssm_B_re.shape[-1]
    ns = n_grp * n_state
    c_ssm = n_grp * grp
    lr, li = ssm_A_re.reshape(1, ns), ssm_A_im.reshape(1, ns)
    ldt = jnp.repeat(ssm_log_dt.reshape(n_grp), n_state).reshape(1, ns)
    btr = ssm_B_re[0].transpose(2, 0, 1).reshape(grp, ns)
    bti = ssm_B_im[0].transpose(2, 0, 1).reshape(grp, ns)
    ctr = ssm_C_re[0].transpose(1, 0, 2).reshape(grp, ns)
    cti = ssm_C_im[0].transpose(1, 0, 2).reshape(grp, ns)
    scan_tile = _pick(seq, (SCAN_TILE,))
    _, _, bbr, bbi, seg_up, seg_down, pw, pw_falling = _s5_params_fwd(lr, li, ldt, btr, bti, scan_tile // SUBLANE)
    nb = c_ssm // LANE
    sb, gpb = ns // nb, n_grp // nb
    diag = (jnp.arange(LANE)[:, None] // grp) == (jnp.arange(sb)[None, :] // n_state)

    def spread(t):
        return jnp.where(diag, jnp.tile(t.reshape(grp, nb, sb).transpose(1, 0, 2), (1, gpb, 1)), 0.0)

    def gather_diag(t):
        return (t * diag).reshape(nb, gpb, grp, sb).sum(1).transpose(1, 0, 2).reshape(grp, ns)

    def interleave(re, im):
        return jnp.stack([re.reshape(-1, nb, sb), im.reshape(-1, nb, sb)], axis=2).reshape(-1, 2 * ns)

    bdc = jnp.concatenate([spread(bbr), spread(bbi)], axis=2).astype(BF16)
    cdc = jnp.concatenate([spread(ctr).transpose(0, 2, 1), -spread(cti).transpose(0, 2, 1)], axis=1).astype(BF16)
    rowi = jnp.arange(SUBLANE)[:, None]
    pwf, pwc = interleave(pw[:, :ns], pw[:, ns:]), interleave(pw[:, :ns], -pw[:, ns:])
    tabs_f = [jnp.where(rowi >= s, pwf[s - 1][None, :], 0.0) for s in (1, 2, 4)] + [pwf]
    tabs_b = [jnp.where(rowi <= SUBLANE - 1 - s, pwc[s - 1][None, :], 0.0) for s in (1, 2, 4)]
    tabs_b.append(interleave(pw_falling[:, :ns], -pw_falling[:, ns:]))
    fix_f = jnp.repeat(interleave(seg_up[:, :ns], seg_up[:, ns:]), SUBLANE, axis=0)
    fix_b = jnp.repeat(interleave(seg_down[:, :ns], -seg_down[:, ns:]), SUBLANE, axis=0)
    c_conv = conv_b.shape[1]
    u_blk = 2 * c_conv // LANE

    a1, b1, z1 = _ffn_up("ffn1_up", h1, full['ffn1_w1'], full['ffn1_w3'])
    arrive(1, [z1])
    x1, h2, h2_t = _mm("ffn1_down", z1, full['ffn1_w2'], 1, 0, addend=xf, alpha=0.5, post=_post_rms(norm_mix))
    saved1 = (h1_t, a1, b1, z1)
    cw = full['conv_w']
    proj = _mm("mix_in", h2, full['w_in'], 1, 0, F32)
    assert c_conv == c_ssm and proj.shape[1] == 3 * c_conv
    cpre, an = _conv_fwd(proj, cw, conv_b, conv_ln_g, conv_ln_b, conv_out_g, seq)
    xs, ypre, yg = _s5_fwd(proj, u_blk, bdc, cdc, fix_f, tabs_f, ssm_D, seq, sb)
    q0 = _mm("s5_gate", yg, full['ssm_glu_w'], 1, 0, F32)
    sn = _s5_post2(yg, q0, ssm_glu_b, ssm_out_g)
    wo = full['w_out']
    mixed = jnp.concatenate([an, sn], axis=1)
    x2, h3, h3_t = _mm("mix_out", mixed, wo, 1, 0, addend=x1, post=_post_rms(norm_ffn2))
    arrive(2, [x2])
    a3, b3, z3 = _ffn_up("ffn2_up", h3, full['ffn2_w1'], full['ffn2_w3'])
    saved2 = (h3_t, a3, b3, z3)
    dx3, dx3_t, loss_row, d_norm_final = _mm("ffn2_down", z3, full['ffn2_w2'], 1, 0, addend=x2, alpha=0.5,
                                             post=_post_loss(row(norm_final), tgt))

    g = {}
    dx2, g['norm_ffn2'], sent = _ffn_bwd("ffn2", x2, norm_ffn2, full['ffn2_w1'], full['ffn2_w3'], full['ffn2_w2'], saved2,
                                         dx3, dx3_t)
    dmixed = _mm("mix_dmixed", dx2, wo, 1, 1, F32)
    dwo = _mm("mix_dwo", mixed, dx2, 0, 0, BF16)
    dq, dyg1, g['ssm_out_g'], g['ssm_glu_b'] = _s5_post2_bwd(dmixed, yg, q0, ssm_glu_b, ssm_out_g)
    dyg2 = _mm("s5_dgate", dq, full['ssm_glu_w'], 1, 1, F32)
    dwg = _mm("s5_dwg", yg, dq, 0, 0, BF16)
    sent['w_out ssm_glu_w'] = (_exchange_start("mix_wo_wg_send", "scatter", [dwo, dwg], [0, 0]), [0, 0])
    dypre, du_skip, g['ssm_D'] = _s5_post1_bwd(dyg1, dyg2, ypre, proj, ssm_D, after=[sent['w_out ssm_glu_w'][0][3]])
    du, dabar, dbdc, dcdc = _s5_bwd(dypre, du_skip, xs, proj, u_blk, bdc, cdc, fix_b, tabs_b, seq, sb)
    dabar = dabar.reshape(nb, 2, sb)
    dlr, dli, dldt, dbtr, dbti = _s5_params_bwd(lr, li, ldt, btr, bti, dabar[:, 0].reshape(1, ns), dabar[:, 1].reshape(1, ns),
                                                gather_diag(dbdc[:, :, :sb]), gather_diag(dbdc[:, :, sb:]))
    g['ssm_A_re'], g['ssm_A_im'] = dlr, dli
    g['ssm_log_dt'] = dldt.reshape(n_grp, n_state).sum(axis=1)
    g['ssm_B_re'] = dbtr.reshape(grp, n_grp, n_state).transpose(1, 2, 0)
    g['ssm_B_im'] = dbti.reshape(grp, n_grp, n_state).transpose(1, 2, 0)
    g['ssm_C_re'] = gather_diag(dcdc[:, :sb].transpose(0, 2, 1)).reshape(grp, n_grp, n_state).transpose(1, 0, 2)
    g['ssm_C_im'] = -gather_diag(dcdc[:, sb:].transpose(0, 2, 1)).reshape(grp, n_grp, n_state).transpose(1, 0, 2)
    dc, g['conv_out_g'], g['conv_ln_g'], g['conv_ln_b'], g['conv_b'] = _conv_bwd_rows(dmixed, cpre, conv_ln_g, conv_ln_b,
                                                                                    conv_out_g)
    dval, dgate, dcw = _conv_bwd_taps(proj, dc, cw, seq)
    dproj = jnp.concatenate([dval, dgate, du], axis=1)
    sent['w_in'] = (_exchange_start("mix_win_send", "scatter", [_mm("mix_dwin", h2_t, dproj, 1, 0, BF16)], [1]), [1])
    dx1, dx1_t, g['norm_mix'] = _mm("mix_dh", dproj, full['w_in'], 1, 1, after=[sent['w_in'][0][3]],
                                    post=_post_rms_bwd(x1, norm_mix, dx2))
    dx0, g['norm_ffn1'], sent1 = _ffn_bwd("ffn1", xf, norm_ffn1, full['ffn1_w1'], full['ffn1_w3'], full['ffn1_w2'], saved1,
                                          dx1, dx1_t)
    sent.update(sent1)
    g['norm_final'] = d_norm_final
    g['conv_w'] = dcw[:n_taps]

    small_shapes = [(n_taps, c_conv) if n == 'conv_w' else wts[n].shape for n in SMALL]
    buf, buf_rows = _pack([g[n] for n in SMALL] + [loss_row])
    to_all = _exchange_start("small_send", "all", [buf], [0])
    slots = {}
    for names, (started, axes) in sent.items():
        lands = _exchange_wait(names.replace(' ', '_') + "_recv", "scatter", started, axes, after=[dx0, to_all[3]])
        slots.update(zip(names.split(), lands))
    sums = [_sum_slots("sum_" + n, slots[n]) for n in BIG]
    to_sibling = _exchange_start("sums_send", "sibling", sums, [0] * len(sums))
    from_all = _exchange_wait("small_recv", "all", to_all, [0], after=[to_sibling[3]])[0]
    total = _unpack(_sum_devices(from_all), buf_rows, small_shapes + [(1, LANE)])
    loss = total[-1][0, 0]
    grads = dict(zip(SMALL, total[:-1]))
    chip = 2 * lax.axis_index("x") + lax.axis_index("y")
    grads['conv_w'] = lax.dynamic_slice_in_dim(grads['conv_w'], chip * c_shard, c_shard, axis=1)[None]
    flat = lambda a: a.reshape(-1, a.shape[-1])
    small = _adamw_many("adamw_small", *[[flat(src[p + n]) for n in SMALL]
                                         for src, p in ((grads, ''), (given, ''), (given, 'm_'), (given, 'v_'))])
    deltas, new_m, new_v = ({n: o.reshape(wts[n].shape) for n, o in zip(SMALL, outs)} for outs in small)

    sums, theirs = _exchange_wait("sums_recv", "sibling", to_sibling, [0] * len(sums), after=[new_v[SMALL[-1]]],
                                  sources_too=True)
    for n, mine, other in zip(BIG, sums, theirs):
        grads[n], deltas[n], new_m[n], new_v[n] = (
            (o.T if n in TRANSPOSED else o)[None]
            for o in _adamw("adamw_" + n, [mine, other], held(n, given[n]), held(n, given['m_' + n]), held(n, given['v_' + n])))

    return (loss, dx0.reshape(x.shape), *[grads[n] for n in WEIGHTS], *[deltas[n] for n in WEIGHTS],
            *[new_m[n] for n in WEIGHTS], *[new_v[n] for n in WEIGHTS])
```

```python
import math
from typing import Callable, NamedTuple

import jax
import jax.numpy as jnp
from jax import lax
from jax.experimental import pallas as pl
from jax.experimental.pallas import tpu as pltpu

F32 = jnp.float32
BF16 = jnp.bfloat16
EPS = 1e-6
ADAM_LR, ADAM_B1, ADAM_B2, ADAM_EPS, ADAM_WD, ADAM_STEP = 0.001, 0.9, 0.999, 1e-08, 0.01, 10
MESH = pl.DeviceIdType.MESH
ANY = pl.BlockSpec(memory_space=pl.ANY)
LANE = 128
SUBLANE = 8
VMEM_LIMIT_BYTES = 56 << 20
ROW_TILE = 256
ROW_TILE_ELEMS = 256 * 1024
WHOLE_ELEMS = 512 * 1024
WHOLE_WEIGHT_BYTES = 8 << 20
FFN_ROWS = 256
CONV_TILE = 128
CONV_SUB = 32
HALO = 32
SCAN_TILE = 256
SCAN_COLS = 512
N_CHIPS = 4
CHIP_RELS = ((1, 0), (0, 1), (1, 1))
NT = (((1,), (1,)), ((), ()))
GELU_K = math.sqrt(2.0 / math.pi)
GELU_C = 0.044715

WEIGHTS = ['norm_ffn1', 'ffn1_w1', 'ffn1_w3', 'ffn1_w2', 'norm_mix', 'w_in', 'conv_w', 'conv_b', 'conv_ln_g', 'conv_ln_b',
           'conv_out_g', 'ssm_A_re', 'ssm_A_im', 'ssm_log_dt', 'ssm_B_re', 'ssm_B_im', 'ssm_C_re', 'ssm_C_im', 'ssm_D',
           'ssm_glu_w', 'ssm_glu_b', 'ssm_out_g', 'w_out', 'norm_ffn2', 'ffn2_w1', 'ffn2_w3', 'ffn2_w2', 'norm_final']
BIG = ['ffn1_w1', 'ffn1_w3', 'ffn1_w2', 'w_in', 'ssm_glu_w', 'w_out', 'ffn2_w1', 'ffn2_w3', 'ffn2_w2']
BIG_AXIS = {'ffn1_w1': 0, 'ffn1_w3': 0, 'ffn1_w2': 0, 'w_in': 1, 'ssm_glu_w': 0, 'w_out': 0, 'ffn2_w1': 0, 'ffn2_w3': 0,
            'ffn2_w2': 0}
TRANSPOSED = ('ffn1_w1', 'ffn1_w3', 'ffn2_w1', 'ffn2_w3')
SMALL = [n for n in WEIGHTS if n not in BIG]


def _round_up(n, m):
    return -(-n // m) * m


def _pick(n, cands):
    for c in cands:
        if c <= n and n % c == 0:
            return c
    return n


def _params(*sem):
    return pltpu.CompilerParams(dimension_semantics=sem, vmem_limit_bytes=VMEM_LIMIT_BYTES)


def _rms_r(x):
    return lax.rsqrt(jnp.mean(x * x, axis=-1, keepdims=True) + EPS)


def _rms_bwd(x, r, g, dy):
    dyg = dy * g
    return r * dyg - x * (r * r * r) * jnp.mean(x * dyg, axis=-1, keepdims=True)


def _sigmoid(x):
    return jax.nn.sigmoid(x)


def _dsilu(a, s):
    return s * (1.0 + a * (1.0 - s))


def _gelu(x):
    return 0.5 * x * (1.0 + jnp.tanh(GELU_K * (x + GELU_C * x * x * x)))


def _dgelu(x):
    t = jnp.tanh(GELU_K * (x + GELU_C * x * x * x))
    return 0.5 * (1.0 + t) + 0.5 * x * (1.0 - t * t) * GELU_K * (1.0 + 3.0 * GELU_C * x * x)


def _colsum(v):
    return jnp.sum(v, axis=0, keepdims=True)


def _rowwise(name, body, n_rows, row_ins, par_ins, row_outs, acc_outs, after=()):
    widest = max([w for (_, w, _) in row_ins] + [w for (w, _) in row_outs])
    tt = _pick(n_rows, [t for t in (256, 128, 64, 32, 16, 8) if t * widest <= ROW_TILE_ELEMS])
    in_specs = [pl.BlockSpec((tt, w), lambda i, cb=cb: (i, cb)) for (_, w, cb) in row_ins]
    in_specs += [pl.BlockSpec(p.shape, lambda i: (0, 0)) for p in par_ins] + [ANY] * len(after)
    out_specs = [pl.BlockSpec((tt, w), lambda i: (i, 0)) for (w, _) in row_outs]
    out_specs += [pl.BlockSpec((r, w), lambda i: (0, 0)) for (r, w) in acc_outs]
    out_shape = [jax.ShapeDtypeStruct((n_rows, w), dt) for (w, dt) in row_outs]
    out_shape += [jax.ShapeDtypeStruct((r, w), F32) for (r, w) in acc_outs]
    n_in, n_ro = len(row_ins) + len(par_ins), len(row_outs)
    o0 = n_in + len(after)

    def kern(*refs):
        accs = refs[o0 + n_ro:]
        if accs:
            @pl.when(pl.program_id(0) == 0)
            def _():
                for a in accs:
                    a[...] = jnp.zeros_like(a)
        body(refs[:n_in], refs[o0:o0 + n_ro], accs)

    return pl.pallas_call(kern, name=name, grid=(n_rows // tt,), in_specs=in_specs, out_specs=out_specs, out_shape=out_shape,
                          compiler_params=_params("arbitrary"))(*[a for a, _, _ in row_ins], *par_ins, *after)


class Post(NamedTuple):
    rows: list
    gains: list
    outs: list
    t_outs: list
    sums: list
    fn: Callable


def _post_rms(gain):
    def fn(r, rows, gains):
        h = r * _rms_r(r) * gains[0]
        return [r, h, h], []

    return Post([], [gain], [F32, BF16], [BF16], [], fn)


def _post_rms_bwd(x, gain, dres):
    def fn(dh, rows, gains):
        r = _rms_r(rows[0])
        dx = rows[1] + _rms_bwd(rows[0], r, gains[0], dh)
        return [dx, dx], [_colsum(dh * rows[0] * r)]

    return Post([x, dres], [gain], [F32], [BF16], [x.shape[1]], fn)


def _post_loss(gain, tgt):
    d = tgt.shape[1]

    def fn(xv, rows, gains):
        r = _rms_r(xv)
        e = xv * r * gains[0] - rows[0]
        sq = jnp.sum(jnp.sum(e * e, axis=-1, keepdims=True), axis=0, keepdims=True)
        dy = e * (1.0 / d)
        dx = _rms_bwd(xv, r, gains[0], dy)
        return [dx, dx], [jnp.broadcast_to(sq * (0.5 / d), (1, LANE)), _colsum(dy * xv * r)]

    return Post([tgt], [gain], [F32], [BF16], [LANE, d], fn)


def _mm(name, a, b, ca, cb, out_dtype=F32, addend=None, alpha=1.0, a_cols=None, after=(), post=None, transposed=False):
    a_start, a_width = a_cols if a_cols else (0, a.shape[1])
    m, k = (a.shape[0], a_width) if ca == 1 else (a_width, a.shape[0])
    n = b.shape[1 - cb]
    assert b.shape[cb] == k, (name, a.shape, b.shape)
    tn = _pick(n, (1024, 768, 512, 384, 256, 128))
    if post and k * tn * b.dtype.itemsize <= WHOLE_WEIGHT_BYTES:
        tk = k
        tm = _pick(m, (256, 128) if k > 1024 else (512, 256, 128))
    else:
        tm = _pick(m, (512, 256, 128) if post else (1024, 512, 256, 128))
        tk = _pick(k, (2048, 1024, 768, 512, 256, 128) if k >= 4096 and not post else (1024, 768, 512, 256, 128))
    nk = k // tk
    if ca == 1:
        assert a_start % tk == 0
        a_spec = pl.BlockSpec((tm, tk), lambda i, j, kk: (i, kk + a_start // tk))
    else:
        assert a_start % tm == 0
        a_spec = pl.BlockSpec((tk, tm), lambda i, j, kk: (kk, i + a_start // tm))
    b_spec = pl.BlockSpec((tk, tn), lambda i, j, kk: (kk, j)) if cb == 0 else pl.BlockSpec((tn, tk), lambda i, j, kk: (j, kk))
    o_spec = pl.BlockSpec((tm, tn), lambda i, j, kk: (i, j))
    t_spec = pl.BlockSpec((tn, tm), lambda i, j, kk: (j, i))
    fixed = lambda w: pl.BlockSpec((1, w), lambda i, j, kk: (0, 0))
    ins, in_specs = [a, b], [a_spec, b_spec]
    if addend is not None:
        ins.append(addend)
        in_specs.append(o_spec)
    n_plain = len(ins)
    n_rows, n_gains = (len(post.rows), len(post.gains)) if post else (0, 0)
    if post:
        assert tn == n, name
        ins += post.rows + post.gains
        in_specs += [o_spec] * n_rows + [fixed(n)] * n_gains
    ins += list(after)
    in_specs += [ANY] * len(after)
    n_in = len(ins)
    if post:
        n_straight, n_vals = len(post.outs), len(post.outs) + len(post.t_outs)
        out_specs = [o_spec] * n_straight + [t_spec] * len(post.t_outs) + [fixed(w) for w in post.sums]
        out_shape = [jax.ShapeDtypeStruct((m, n), dt) for dt in post.outs] + [jax.ShapeDtypeStruct((n, m), dt) for dt in post.t_outs]
        out_shape += [jax.ShapeDtypeStruct((1, w), F32) for w in post.sums]
    elif transposed:
        out_specs, out_shape = [t_spec], [jax.ShapeDtypeStruct((n, m), out_dtype)]
    else:
        out_specs, out_shape = [o_spec], [jax.ShapeDtypeStruct((m, n), out_dtype)]
    n_out = len(out_specs)
    dims = (((ca,), (cb,)), ((), ()))

    def emit(refs, r):
        if alpha != 1.0:
            r = r * alpha
        if addend is not None:
            r = r + refs[2][...].astype(F32)
        outs = refs[n_in:n_in + n_out]
        if post is None:
            outs[0][...] = (r.T if transposed else r).astype(out_dtype)
            return
        vals, incs = post.fn(r, [q[...] for q in refs[n_plain:n_plain + n_rows]],
                             [q[...] for q in refs[n_plain + n_rows:n_plain + n_rows + n_gains]])
        for at, (o_ref, val) in enumerate(zip(outs, vals)):
            o_ref[...] = (val if at < n_straight else val.T).astype(o_ref.dtype)
        for s_ref, inc in zip(outs[n_vals:], incs):
            s_ref[...] += inc

    def kern(*refs):
        kk = pl.program_id(2)
        if post and post.sums:
            @pl.when(jnp.logical_and(jnp.logical_and(pl.program_id(0) == 0, pl.program_id(1) == 0), kk == 0))
            def _():
                for s_ref in refs[n_in + n_vals:n_in + n_out]:
                    s_ref[...] = jnp.zeros_like(s_ref)

        dot = lambda: lax.dot_general(refs[0][...].astype(BF16), refs[1][...].astype(BF16), dims,
                                      preferred_element_type=F32)
        if nk == 1:
            emit(refs, dot())
            return
        acc_ref = refs[-1]

        @pl.when(kk == 0)
        def _():
            acc_ref[...] = jnp.zeros_like(acc_ref)

        acc_ref[...] += dot()

        @pl.when(kk == nk - 1)
        def _():
            emit(refs, acc_ref[...])

    res = pl.pallas_call(kern, name=name, grid=(m // tm, n // tn, nk), in_specs=in_specs, out_specs=out_specs,
                         out_shape=out_shape, scratch_shapes=[] if nk == 1 else [pltpu.VMEM((tm, tn), F32)],
                         compiler_params=_params("arbitrary", "arbitrary", "arbitrary"))(*ins)
    return res if post else res[0]


def _rms_fwd(name, x, g):
    t, d = x.shape
    tt = _pick(t, (ROW_TILE, LANE))

    def kern(x_ref, g_ref, h_ref, ht_ref):
        xv = x_ref[...]
        h = xv * _rms_r(xv) * g_ref[...]
        h_ref[...] = h.astype(BF16)
        ht_ref[...] = h.T.astype(BF16)

    return pl.pallas_call(kern, name=name, grid=(t // tt,),
                          in_specs=[pl.BlockSpec((tt, d), lambda i: (i, 0)), pl.BlockSpec((1, d), lambda i: (0, 0))],
                          out_specs=[pl.BlockSpec((tt, d), lambda i: (i, 0)), pl.BlockSpec((d, tt), lambda i: (0, i))],
                          out_shape=[jax.ShapeDtypeStruct((t, d), BF16), jax.ShapeDtypeStruct((d, t), BF16)],
                          compiler_params=_params("arbitrary"))(x, g)


def _ffn_up(name, h, w1, w3):
    t, d = h.shape
    ff = w1.shape[0]
    tm, tn = _pick(t, (1024, 512, 256, 128)), _pick(ff, (1024, 768, 512, 256, 128))

    def kern(h_ref, w1_ref, w3_ref, a_ref, b_ref, z_ref):
        hv = h_ref[...]
        a = lax.dot_general(hv, w1_ref[...], NT, preferred_element_type=F32)
        b = lax.dot_general(hv, w3_ref[...], NT, preferred_element_type=F32)
        a_ref[...] = a.astype(BF16)
        b_ref[...] = b.astype(BF16)
        z_ref[...] = (a * _sigmoid(a) * b).astype(BF16)

    w_spec = pl.BlockSpec((tn, d), lambda i, j: (j, 0))
    o_spec = pl.BlockSpec((tm, tn), lambda i, j: (i, j))
    return pl.pallas_call(kern, name=name, grid=(t // tm, ff // tn),
                          in_specs=[pl.BlockSpec((tm, d), lambda i, j: (i, 0)), w_spec, w_spec], out_specs=[o_spec] * 3,
                          out_shape=[jax.ShapeDtypeStruct((t, ff), BF16)] * 3,
                          compiler_params=_params("arbitrary", "arbitrary"))(h, w1, w3)


def _ffn_dglu(name, dxo, w2, a, b, after=()):
    t, d = dxo.shape
    ff = w2.shape[0]
    tm = _pick(t, (FFN_ROWS, 128))

    def kern(dx_ref, w2_ref, a_ref, b_ref, *rest):
        da_ref, db_ref = rest[-2:]
        dz = lax.dot_general(dx_ref[...].astype(BF16), w2_ref[...], NT, preferred_element_type=F32) * 0.5
        av, bv = a_ref[...].astype(F32), b_ref[...].astype(F32)
        s = _sigmoid(av)
        da_ref[...] = (dz * bv * _dsilu(av, s)).astype(BF16)
        db_ref[...] = (dz * av * s).astype(BF16)

    o_spec = pl.BlockSpec((tm, ff), lambda i: (i, 0))
    return pl.pallas_call(kern, name=name, grid=(t // tm,),
                          in_specs=[pl.BlockSpec((tm, d), lambda i: (i, 0)), pl.BlockSpec((ff, d), lambda i: (0, 0)),
                                    o_spec, o_spec] + [ANY] * len(after),
                          out_specs=[o_spec] * 2, out_shape=[jax.ShapeDtypeStruct((t, ff), BF16)] * 2,
                          compiler_params=_params("arbitrary"))(dxo, w2, a, b, *after)


def _ffn_dh(name, da, db, w1, w3, x, g, dres, after=()):
    t, d = x.shape
    ff = da.shape[1]
    tm = _pick(t, (FFN_ROWS, 128))

    def kern(da_ref, db_ref, w1_ref, w3_ref, x_ref, g_ref, dres_ref, *rest):
        dx_ref, dg_ref = rest[-2:]

        @pl.when(pl.program_id(0) == 0)
        def _():
            dg_ref[...] = jnp.zeros_like(dg_ref)

        dh = (jnp.dot(da_ref[...], w1_ref[...], preferred_element_type=F32)
              + jnp.dot(db_ref[...], w3_ref[...], preferred_element_type=F32))
        xv = x_ref[...]
        r = _rms_r(xv)
        dx_ref[...] = dres_ref[...] + _rms_bwd(xv, r, g_ref[...], dh)
        dg_ref[...] += _colsum(dh * xv * r)

    act = pl.BlockSpec((tm, ff), lambda i: (i, 0))
    wgt = pl.BlockSpec((ff, d), lambda i: (0, 0))
    rows = pl.BlockSpec((tm, d), lambda i: (i, 0))
    gain = pl.BlockSpec((1, d), lambda i: (0, 0))
    return pl.pallas_call(kern, name=name, grid=(t // tm,),
                          in_specs=[act, act, wgt, wgt, rows, gain, rows] + [ANY] * len(after), out_specs=[rows, gain],
                          out_shape=[jax.ShapeDtypeStruct((t, d), F32), jax.ShapeDtypeStruct((1, d), F32)],
                          compiler_params=_params("arbitrary"))(da, db, w1, w3, x, g, dres, *after)


def _ffn_bwd(tag, x, g, w1, w3, w2, saved, dxo, dxo_t, early):
    ht, a, b, z = saved
    dw2 = _mm(tag + "_dw2", dxo_t, z, 1, 0, BF16, alpha=0.5, transposed=True)
    da, db = _ffn_dglu(tag + "_dglu", dxo, w2, a, b)
    dw1 = _mm(tag + "_dw1", ht, da, 1, 0, BF16, transposed=True)
    sent, pin = {}, []
    if early:
        sent[tag + "_w2 " + tag + "_w1"] = (_exchange_start(tag + "_w2_w1_send", "scatter", [dw2, dw1], [0, 0]), [0, 0])
        pin = [sent[tag + "_w2 " + tag + "_w1"][0][3]]
    dw3 = _mm(tag + "_dw3", ht, db, 1, 0, BF16, after=pin, transposed=True)
    last = [dw3] if early else [dw2, dw1, dw3]
    names = [tag + "_w3"] if early else [tag + "_w2", tag + "_w1", tag + "_w3"]
    sent[" ".join(names)] = (_exchange_start(tag + "_w3_send", "scatter", last, [0] * len(last)), [0] * len(last))
    dx, dg = _ffn_dh(tag + "_dh", da, db, w1, w3, x, g, dxo, after=[sent[" ".join(names)][0][3]])
    return dx, dg, sent


def _shift_copies(ext_ref, sh_ref):
    n = ext_ref.shape[0] - SUBLANE
    for r in range(1, SUBLANE):
        sh_ref[r, pl.ds(0, n), :] = ext_ref[pl.ds(r, n), :]


def _rows_at(ext_ref, sh_ref, off, rows):
    r = off % SUBLANE
    return ext_ref[pl.ds(off, rows), :] if r == 0 else sh_ref[r, pl.ds(off - r, rows), :]


def _conv_fwd(proj, cw, cb, lng, lnb, og, seq):
    n_rows, c = proj.shape[0], cb.shape[1]
    kw = HALO - 1
    tt = _pick(seq, (CONV_TILE,))
    hb = tt // HALO

    def kern(v_ref, g_ref, vp_ref, gp_ref, w_ref, cb_ref, lg_ref, lb_ref, og_ref, c_ref, an_ref, ext_ref, sh_ref):
        first = (pl.program_id(0) * tt) % seq == 0
        ext_ref[pl.ds(HALO, tt), :] = v_ref[...] * _sigmoid(g_ref[...])
        ext_ref[pl.ds(0, HALO), :] = vp_ref[...] * _sigmoid(gp_ref[...]) * jnp.where(first, 0.0, 1.0)
        _shift_copies(ext_ref, sh_ref)
        for r0 in range(0, tt, CONV_SUB):
            rows = min(CONV_SUB, tt - r0)
            acc = jnp.zeros((rows, c), F32)
            for k in range(kw):
                acc = acc + w_ref[pl.ds(k, 1), :] * _rows_at(ext_ref, sh_ref, r0 + HALO - (kw - 1) + k, rows)
            c_ref[pl.ds(r0, rows), :] = acc + cb_ref[...]
        cv = c_ref[...]
        mu = jnp.mean(cv, axis=-1, keepdims=True)
        xc = cv - mu
        rstd = lax.rsqrt(jnp.mean(xc * xc, axis=-1, keepdims=True) + EPS)
        lv = xc * rstd * lg_ref[...] + lb_ref[...]
        sl = lv * _sigmoid(lv)
        an_ref[...] = (sl * _rms_r(sl) * og_ref[...]).astype(BF16)

    cur = lambda cbk: pl.BlockSpec((tt, c), lambda i: (i, cbk))
    prev = lambda cbk: pl.BlockSpec((HALO, c), lambda i: (jnp.maximum(i * hb - 1, 0), cbk))
    par = lambda p: pl.BlockSpec(p.shape, lambda i: (0, 0))
    return pl.pallas_call(
        kern, name="conv_fwd", grid=(n_rows // tt,),
        in_specs=[cur(0), cur(1), prev(0), prev(1), par(cw), par(cb), par(lng), par(lnb), par(og)],
        out_specs=[pl.BlockSpec((tt, c), lambda i: (i, 0))] * 2,
        out_shape=[jax.ShapeDtypeStruct((n_rows, c), F32), jax.ShapeDtypeStruct((n_rows, c), BF16)],
        scratch_shapes=[pltpu.VMEM((tt + HALO, c), F32), pltpu.VMEM((SUBLANE, tt + HALO, c), F32)],
        compiler_params=_params("arbitrary"),
    )(proj, proj, proj, proj, cw, cb, lng, lnb, og)


def _conv_bwd_rows(dmixed, cpre, lng, lnb, og):
    c = cpre.shape[1]

    def body(ins, outs, accs):
        dan, cv, lg, lb, ogv = ins[0][...], ins[1][...], ins[2][...], ins[3][...], ins[4][...]
        mu = jnp.mean(cv, axis=-1, keepdims=True)
        xc = cv - mu
        rstd = lax.rsqrt(jnp.mean(xc * xc, axis=-1, keepdims=True) + EPS)
        xh = xc * rstd
        lv = xh * lg + lb
        s = _sigmoid(lv)
        sl = lv * s
        r2 = _rms_r(sl)
        accs[0][...] += _colsum(dan * sl * r2)
        dl = _rms_bwd(sl, r2, ogv, dan) * _dsilu(lv, s)
        accs[1][...] += _colsum(dl * xh)
        accs[2][...] += _colsum(dl)
        dxh = dl * lg
        dc = rstd * (dxh - jnp.mean(dxh, axis=-1, keepdims=True) - xh * jnp.mean(dxh * xh, axis=-1, keepdims=True))
        outs[0][...] = dc
        accs[3][...] += _colsum(dc)

    return _rowwise("conv_bwd_rows", body, cpre.shape[0], [(dmixed, c, 0), (cpre, c, 0)], [lng, lnb, og], [(c, F32)],
                    [(1, c)] * 4)


def _conv_bwd_taps(proj, dc, cw, seq):
    n_rows, c = dc.shape
    kw = HALO - 1
    tt = _pick(seq, (CONV_TILE,))
    hb = tt // HALO
    last_blk = n_rows // HALO - 1

    def kern(v_ref, g_ref, vp_ref, gp_ref, dc_ref, dn_ref, w_ref, dv_ref, dg_ref, dw_ref, exta_ref, extd_ref, sha_ref, shd_ref):
        i = pl.program_id(0)
        first = (i * tt) % seq == 0
        last = ((i + 1) * tt) % seq == 0

        @pl.when(i == 0)
        def _():
            dw_ref[...] = jnp.zeros_like(dw_ref)

        sg = _sigmoid(g_ref[...])
        exta_ref[pl.ds(HALO, tt), :] = v_ref[...] * sg
        exta_ref[pl.ds(0, HALO), :] = vp_ref[...] * _sigmoid(gp_ref[...]) * jnp.where(first, 0.0, 1.0)
        dcv = dc_ref[...]
        extd_ref[pl.ds(0, tt), :] = dcv
        extd_ref[pl.ds(tt, HALO), :] = dn_ref[...] * jnp.where(last, 0.0, 1.0)
        _shift_copies(exta_ref, sha_ref)
        _shift_copies(extd_ref, shd_ref)
        for k in range(kw):
            dw_ref[pl.ds(k, 1), :] += _colsum(_rows_at(exta_ref, sha_ref, HALO - (kw - 1) + k, tt) * dcv)
        for r0 in range(0, tt, CONV_SUB):
            rows = min(CONV_SUB, tt - r0)
            acc = jnp.zeros((rows, c), F32)
            for k in range(kw):
                acc = acc + w_ref[pl.ds(k, 1), :] * _rows_at(extd_ref, shd_ref, r0 + (kw - 1) - k, rows)
            dv_ref[pl.ds(r0, rows), :] = acc
        da = dv_ref[...]
        dv_ref[...] = da * sg
        dg_ref[...] = da * v_ref[...] * sg * (1.0 - sg)

    cur = lambda cbk: pl.BlockSpec((tt, c), lambda i: (i, cbk))
    prev = lambda cbk: pl.BlockSpec((HALO, c), lambda i: (jnp.maximum(i * hb - 1, 0), cbk))
    nxt = pl.BlockSpec((HALO, c), lambda i: (jnp.minimum((i + 1) * hb, last_blk), 0))
    return pl.pallas_call(
        kern, name="conv_bwd_taps", grid=(n_rows // tt,),
        in_specs=[cur(0), cur(1), prev(0), prev(1), cur(0), nxt, pl.BlockSpec(cw.shape, lambda i: (0, 0))],
        out_specs=[cur(0), cur(0), pl.BlockSpec((HALO, c), lambda i: (0, 0))],
        out_shape=[jax.ShapeDtypeStruct((n_rows, c), F32), jax.ShapeDtypeStruct((n_rows, c), F32),
                   jax.ShapeDtypeStruct((HALO, c), F32)],
        scratch_shapes=[pltpu.VMEM((tt + HALO, c), F32)] * 2 + [pltpu.VMEM((SUBLANE, tt + HALO, c), F32)] * 2,
        compiler_params=_params("arbitrary"),
    )(proj, proj, proj, proj, dc, dc, cw)


def _s5_params_fwd(lr, li, ldt, btr, bti, seg):
    ns = lr.shape[1]

    def kern(lr_ref, li_ref, ldt_ref, btr_ref, bti_ref, ar_ref, ai_ref, bbr_ref, bbi_ref, ps_ref, psf_ref, pc_ref, pcf_ref):
        lrv, liv = lr_ref[...], li_ref[...]
        dt = jnp.exp(ldt_ref[...])
        zr, zi = lrv * dt, liv * dt
        mag = jnp.exp(zr)
        ar, ai = mag * jnp.cos(zi), mag * jnp.sin(zi)
        den = lrv * lrv + liv * liv
        nr = ar - 1.0
        cr = (nr * lrv + ai * liv) / den
        ci = (ai * lrv - nr * liv) / den
        ar_ref[...] = ar
        ai_ref[...] = ai
        bbr_ref[...] = cr * btr_ref[...] - ci * bti_ref[...]
        bbi_ref[...] = cr * bti_ref[...] + ci * btr_ref[...]
        def powers(br, bi, count, up_ref, down_ref):
            pr, pi = br, bi
            for e in range(count):
                for ref, at in ((up_ref, e), (down_ref, count - 1 - e)):
                    ref[pl.ds(at, 1), pl.ds(0, ns)] = pr
                    ref[pl.ds(at, 1), pl.ds(ns, ns)] = pi
                if e < count - 1:
                    pr, pi = pr * br - pi * bi, pr * bi + pi * br
            return pr, pi

        powers(*powers(ar, ai, seg, ps_ref, psf_ref), SUBLANE, pc_ref, pcf_ref)

    h = btr.shape[0]
    shapes = [jax.ShapeDtypeStruct((1, ns), F32)] * 2 + [jax.ShapeDtypeStruct((h, ns), F32)] * 2
    shapes += [jax.ShapeDtypeStruct((seg, 2 * ns), F32)] * 2 + [jax.ShapeDtypeStruct((SUBLANE, 2 * ns), F32)] * 2
    return pl.pallas_call(kern, name="s5_params_fwd", out_shape=shapes)(lr, li, ldt, btr, bti)


def _s5_params_bwd(lr, li, ldt, btr, bti, dar, dai, dbbr, dbbi):
    def kern(lr_ref, li_ref, ldt_ref, btr_ref, bti_ref, dar_ref, dai_ref, dbr_ref, dbi_ref,
             dlr_ref, dli_ref, dldt_ref, dbtr_ref, dbti_ref):
        lrv, liv = lr_ref[...], li_ref[...]
        dt = jnp.exp(ldt_ref[...])
        zr, zi = lrv * dt, liv * dt
        mag = jnp.exp(zr)
        ar, ai = mag * jnp.cos(zi), mag * jnp.sin(zi)
        den = lrv * lrv + liv * liv
        nr = ar - 1.0
        cr = (nr * lrv + ai * liv) / den
        ci = (ai * lrv - nr * liv) / den
        dbr, dbi, br, bi = dbr_ref[...], dbi_ref[...], btr_ref[...], bti_ref[...]
        dbtr_ref[...] = cr * dbr + ci * dbi
        dbti_ref[...] = cr * dbi - ci * dbr
        dcr = _colsum(br * dbr + bi * dbi)
        dci = _colsum(br * dbi - bi * dbr)
        ir, ii = lrv / den, -liv / den
        dnr = ir * dcr + ii * dci
        dni = ir * dci - ii * dcr
        wr, wi = cr * ir - ci * ii, cr * ii + ci * ir
        dl1r = -(wr * dcr + wi * dci)
        dl1i = -(wr * dci - wi * dcr)
        dtr, dti = dar_ref[...] + dnr, dai_ref[...] + dni
        dzr = ar * dtr + ai * dti
        dzi = ar * dti - ai * dtr
        dlr_ref[...] = dl1r + dt * dzr
        dli_ref[...] = dl1i + dt * dzi
        dldt_ref[...] = (dzr * lrv + dzi * liv) * dt

    ns, h = lr.shape[1], btr.shape[0]
    shapes = [jax.ShapeDtypeStruct((1, ns), F32)] * 3 + [jax.ShapeDtypeStruct((h, ns), F32)] * 2
    return pl.pallas_call(kern, name="s5_params_bwd", out_shape=shapes)(lr, li, ldt, btr, bti, dar, dai, dbbr, dbbi)


def _to_segments(nat_ref, seg_ref):
    steps = nat_ref.shape[0] // SUBLANE
    _regroup(nat_ref, seg_ref, lambda r: (r % SUBLANE) * steps + r // SUBLANE)


def _from_segments(seg_ref, nat_ref):
    steps = nat_ref.shape[0] // SUBLANE
    _regroup(seg_ref, nat_ref, lambda r: (r % steps) * SUBLANE + r // steps)


def _regroup(src_ref, dst_ref, src_row):
    rows, width = dst_ref.shape
    sublane = lax.broadcasted_iota(jnp.int32, (SUBLANE, width), 0)
    for r0 in range(0, rows, SUBLANE):
        tile = jnp.broadcast_to(src_ref[pl.ds(src_row(r0), 1), :], (SUBLANE, width))
        for k in range(1, SUBLANE):
            tile = jnp.where(sublane == k, src_ref[pl.ds(src_row(r0 + k), 1), :], tile)
        dst_ref[pl.ds(r0, SUBLANE), :] = tile


def _scan_tile(s_ref, o_ref, fix_ref, tabs, car_ref, sb, reverse, x_ref=None, acc_ref=None):
    l1, l2, l4, pw = tabs
    rows_t, w = s_ref.shape
    steps = rows_t // SUBLANE
    cw = _pick(sb, (SCAN_COLS,))
    last = 0 if reverse else SUBLANE - 1
    first = SUBLANE - 1 - last
    row = lax.broadcasted_iota(jnp.int32, (SUBLANE, cw), 0)
    step_rows = lambda i: pl.ds(pl.multiple_of(((steps - 1 - i) if reverse else i) * SUBLANE, SUBLANE), SUBLANE)
    zero = jnp.zeros((SUBLANE, cw), F32)

    for c0 in [b0 + o for b0 in range(0, w, 2 * sb) for o in range(0, sb, cw)]:
        cr, ci = pl.ds(c0, cw), pl.ds(c0 + sb, cw)
        base = pl.ds(((steps - 1) if reverse else 0) * SUBLANE, SUBLANE)
        ar, ai = fix_ref[base, cr], fix_ref[base, ci]

        def run(i, state):
            xr, xi = state
            rows = step_rows(i)
            xr, xi = ar * xr - ai * xi + s_ref[rows, cr], ar * xi + ai * xr + s_ref[rows, ci]
            o_ref[rows, cr] = xr
            o_ref[rows, ci] = xi
            return xr, xi

        fr, fi = lax.fori_loop(0, steps, run, (zero, zero))
        for s, lt in ((1, l1), (2, l2), (4, l4)):
            sh = (SUBLANE - s) if reverse else s
            sr, si = pltpu.roll(fr, sh, 0), pltpu.roll(fi, sh, 0)
            tr, ti = lt[:, cr], lt[:, ci]
            fr, fi = fr + tr * sr - ti * si, fi + tr * si + ti * sr
        kr, ki = car_ref[pl.ds(last, 1), cr], car_ref[pl.ds(last, 1), ci]
        pr, pi = pw[:, cr], pw[:, ci]
        fr, fi = fr + pr * kr - pi * ki, fi + pr * ki + pi * kr
        car_ref[:, cr] = fr
        car_ref[:, ci] = fi
        to_next = 1 if not reverse else SUBLANE - 1
        gr = jnp.where(row == first, kr, pltpu.roll(fr, to_next, 0))
        gi = jnp.where(row == first, ki, pltpu.roll(fi, to_next, 0))

        def fix(i, state):
            rows = step_rows(i)
            qr, qi = fix_ref[rows, cr], fix_ref[rows, ci]
            yr = o_ref[rows, cr] + qr * gr - qi * gi
            yi = o_ref[rows, ci] + qr * gi + qi * gr
            o_ref[rows, cr] = yr
            o_ref[rows, ci] = yi
            if acc_ref is None:
                return state
            nr, ni, sr, si = state
            pxr, pxi = x_ref[rows, cr], x_ref[rows, ci]
            return yr, yi, sr + nr * pxr + ni * pxi, si + ni * pxr - nr * pxi

        if acc_ref is None:
            lax.fori_loop(0, steps, fix, 0)
        else:
            _, _, sr, si = lax.fori_loop(0, steps, fix, (gr, gi, zero, zero))
            acc_ref[:, cr] += sr
            acc_ref[:, ci] += si


def _s5_fwd(proj, u_blk, bdc, cdc, fix, tabs, dskip, seq, sb):
    n_rows = proj.shape[0]
    nb, blk, w_blk = bdc.shape
    c, w = nb * blk, nb * w_blk
    tt = fix.shape[0]

    def kern(u_ref, bd_ref, cd_ref, fix_ref, l1, l2, l4, pw, d_ref, xs_ref, yp_ref, yg_ref, us_ref, bu_ref, car_ref):
        @pl.when((pl.program_id(0) * tt) % seq == 0)
        def _():
            car_ref[...] = jnp.zeros_like(car_ref)

        _to_segments(u_ref, us_ref)
        for j in range(nb):
            bu_ref[:, pl.ds(j * w_blk, w_blk)] = jnp.dot(us_ref[:, pl.ds(j * blk, blk)].astype(BF16), bd_ref[j],
                                                         preferred_element_type=F32)
        _scan_tile(bu_ref, xs_ref, fix_ref, (l1, l2, l4, pw), car_ref, sb, False)
        for j in range(nb):
            cols = pl.ds(j * blk, blk)
            y0 = jnp.dot(xs_ref[:, pl.ds(j * w_blk, w_blk)].astype(BF16), cd_ref[j], preferred_element_type=F32)
            us_ref[:, cols] = y0 + d_ref[:, cols] * us_ref[:, cols]
        _from_segments(us_ref, yp_ref)
        yg_ref[...] = _gelu(yp_ref[...]).astype(BF16)

    tab = pl.BlockSpec((SUBLANE, w), lambda i: (0, 0))
    rows = pl.BlockSpec((tt, c), lambda i: (i, 0))
    return pl.pallas_call(
        kern, name="s5_fwd", grid=(n_rows // tt,),
        in_specs=[pl.BlockSpec((tt, c), lambda i: (i, u_blk * blk // c)), pl.BlockSpec(bdc.shape, lambda i: (0, 0, 0)),
                  pl.BlockSpec(cdc.shape, lambda i: (0, 0, 0)), pl.BlockSpec((tt, w), lambda i: (0, 0)), tab, tab, tab, tab,
                  pl.BlockSpec((1, c), lambda i: (0, 0))],
        out_specs=[pl.BlockSpec((tt, w), lambda i: (i, 0)), rows, rows],
        out_shape=[jax.ShapeDtypeStruct((n_rows, w), F32), jax.ShapeDtypeStruct((n_rows, c), F32),
                   jax.ShapeDtypeStruct((n_rows, c), BF16)],
        scratch_shapes=[pltpu.VMEM((tt, c), F32), pltpu.VMEM((tt, w), F32), pltpu.VMEM((SUBLANE, w), F32)],
        compiler_params=_params("arbitrary"))(proj, bdc, cdc, fix, *tabs, dskip)


def _s5_bwd(dypre, du_skip, xs, proj, u_blk, bdc, cdc, fix, tabs, seq, sb):
    n_rows = proj.shape[0]
    nb, blk, w_blk = bdc.shape
    c, w = nb * blk, nb * w_blk
    tt = fix.shape[0]
    nt = n_rows // tt
    tn = (((0,), (0,)), ((), ()))

    def kern(dy_ref, ds_ref, x_ref, u_ref, bd_ref, cd_ref, fix_ref, l1, l2, l4, pw, du_ref, da_ref, db_ref, dc_ref,
             dys_ref, us_ref, dus_ref, gx_ref, lam_ref, car_ref, acc_ref):
        i = pl.program_id(0)

        @pl.when(((nt - i) * tt) % seq == 0)
        def _():
            car_ref[...] = jnp.zeros_like(car_ref)

        @pl.when(i == 0)
        def _():
            acc_ref[...] = jnp.zeros_like(acc_ref)
            db_ref[...] = jnp.zeros_like(db_ref)
            dc_ref[...] = jnp.zeros_like(dc_ref)

        _to_segments(dy_ref, dys_ref)
        _to_segments(u_ref, us_ref)
        for j in range(nb):
            gx_ref[:, pl.ds(j * w_blk, w_blk)] = lax.dot_general(dys_ref[:, pl.ds(j * blk, blk)].astype(BF16), cd_ref[j], NT,
                                                                 preferred_element_type=F32)
        _scan_tile(gx_ref, lam_ref, fix_ref, (l1, l2, l4, pw), car_ref, sb, True, x_ref, acc_ref)
        for j in range(nb):
            cols, wide = pl.ds(j * blk, blk), pl.ds(j * w_blk, w_blk)
            lam = lam_ref[:, wide].astype(BF16)
            dus_ref[:, cols] = lax.dot_general(lam, bd_ref[j], NT, preferred_element_type=F32)
            db_ref[j] += lax.dot_general(us_ref[:, cols].astype(BF16), lam, tn, preferred_element_type=F32)
            dc_ref[j] += lax.dot_general(x_ref[:, wide].astype(BF16), dys_ref[:, cols].astype(BF16), tn,
                                         preferred_element_type=F32)
        _from_segments(dus_ref, du_ref)
        du_ref[...] += ds_ref[...]

        @pl.when(i == nt - 1)
        def _():
            da_ref[...] = _colsum(acc_ref[...])

    back = lambda i: (nt - 1 - i, 0)
    tab = pl.BlockSpec((SUBLANE, w), lambda i: (0, 0))
    rows = pl.BlockSpec((tt, c), back)
    whole = lambda a: pl.BlockSpec(a.shape, lambda i: (0, 0, 0))
    return pl.pallas_call(
        kern, name="s5_bwd", grid=(nt,),
        in_specs=[rows, rows, pl.BlockSpec((tt, w), back), pl.BlockSpec((tt, c), lambda i: (nt - 1 - i, u_blk * blk // c)),
                  whole(bdc), whole(cdc), pl.BlockSpec((tt, w), lambda i: (0, 0)), tab, tab, tab, tab],
        out_specs=[rows, pl.BlockSpec((1, w), lambda i: (0, 0)), whole(bdc), whole(cdc)],
        out_shape=[jax.ShapeDtypeStruct((n_rows, c), F32), jax.ShapeDtypeStruct((1, w), F32),
                   jax.ShapeDtypeStruct(bdc.shape, F32), jax.ShapeDtypeStruct(cdc.shape, F32)],
        scratch_shapes=[pltpu.VMEM((tt, c), F32)] * 3 + [pltpu.VMEM((tt, w), F32)] * 2 + [pltpu.VMEM((SUBLANE, w), F32)] * 2,
        compiler_params=_params("arbitrary"))(dypre, du_skip, xs, proj, bdc, cdc, fix, *tabs)


def _s5_post2(yg, q0, bg, og):
    c = yg.shape[1]

    def body(ins, outs, accs):
        ygv = ins[0][...].astype(F32)
        sg = ygv * _sigmoid(ins[1][...] + ins[2][...])
        outs[0][...] = (sg * _rms_r(sg) * ins[3][...]).astype(BF16)

    return _rowwise("s5_post2", body, yg.shape[0], [(yg, c, 0), (q0, c, 0)], [bg, og], [(c, BF16)], [])[0]


def _s5_post2_bwd(dmixed, yg, q0, bg, og):
    c = yg.shape[1]

    def body(ins, outs, accs):
        dsn, ygv = ins[0][...], ins[1][...].astype(F32)
        s = _sigmoid(ins[2][...] + ins[3][...])
        sg = ygv * s
        r = _rms_r(sg)
        accs[0][...] += _colsum(dsn * sg * r)
        dsg = _rms_bwd(sg, r, ins[4][...], dsn)
        dq = dsg * ygv * s * (1.0 - s)
        outs[0][...] = dq.astype(BF16)
        outs[1][...] = dsg * s
        accs[1][...] += _colsum(dq)

    return _rowwise("s5_post2_bwd", body, yg.shape[0], [(dmixed, c, 1), (yg, c, 0), (q0, c, 0)], [bg, og],
                    [(c, BF16), (c, F32)], [(1, c)] * 2)


def _s5_post1_bwd(dyg1, dyg2, ypre, proj, dskip, after=()):
    c = ypre.shape[1]

    def body(ins, outs, accs):
        dyp = (ins[0][...] + ins[1][...]) * _dgelu(ins[2][...])
        outs[0][...] = dyp
        outs[1][...] = dyp * ins[4][...]
        accs[0][...] += _colsum(dyp * ins[3][...])

    return _rowwise("s5_post1_bwd", body, ypre.shape[0], [(dyg1, c, 0), (dyg2, c, 0), (ypre, c, 0), (proj, c, 2)], [dskip],
                    [(c, F32), (c, F32)], [(1, c)], after=after)


def _place():
    return lax.axis_index("x"), lax.axis_index("y"), lax.axis_index("c")


def _window(ref, axis, q, rows, cols):
    if axis == 0:
        return ref.at[pl.ds(pl.multiple_of(q * rows, SUBLANE), rows), :]
    return ref.at[:, pl.ds(pl.multiple_of(q * cols, LANE), cols)]


ALL_RELS = [(fx, fy, fc) for fx in (0, 1) for fy in (0, 1) for fc in (0, 1)][1:]
N_PEERS = {"gather": 3, "scatter": 3, "sibling": 1, "all": len(ALL_RELS)}


def _copies(kind, srcs, lands, shards, axes, send_sems, recv_sems, local_sems):
    x, y, c = _place()
    me, dev = 2 * x + y, 4 * x + 2 * y + c
    n_peers = N_PEERS[kind]
    starts, waits = [], []
    for a, (src, land) in enumerate(zip(srcs, lands)):
        on = lambda k, peer: dict(send_sem=send_sems.at[n_peers * a + k], recv_sem=recv_sems.at[n_peers * a + k],
                                  device_id=peer, device_id_type=MESH)
        if kind == "sibling":
            cp = pltpu.make_async_remote_copy(src_ref=src, dst_ref=land, **on(0, (x, y, 1 - c)))
            starts.append(cp)
            waits.append(cp)
            continue
        if kind == "all":
            own = pltpu.make_async_copy(src, land.at[dev], local_sems.at[a])
            starts.append(own)
            waits.append(own)
            for k, (fx, fy, fc) in enumerate(ALL_RELS):
                px, py, pc = (1 - x) if fx else x, (1 - y) if fy else y, (1 - c) if fc else c
                starts.append(pltpu.make_async_remote_copy(src_ref=src, dst_ref=land.at[dev], **on(k, (px, py, pc))))
                waits.append(pltpu.make_async_remote_copy(src_ref=src, dst_ref=land.at[4 * px + 2 * py + pc],
                                                          **on(k, (px, py, pc))))
            continue
        rows, cols = shards[a]
        if kind == "gather":
            own = pltpu.make_async_copy(src, _window(land, axes[a], me, rows, cols), local_sems.at[a])
        else:
            own = pltpu.make_async_copy(_window(src, axes[a], me, rows, cols), land.at[3], local_sems.at[a])
        starts.append(own)
        waits.append(own)
        for j, (fx, fy) in enumerate(CHIP_RELS):
            px, py = (1 - x) if fx else x, (1 - y) if fy else y
            peer = 2 * px + py
            if kind == "gather":
                starts.append(pltpu.make_async_remote_copy(src_ref=src, dst_ref=_window(land, axes[a], me, rows, cols),
                                                           **on(j, (px, py, c))))
                waits.append(pltpu.make_async_remote_copy(src_ref=src, dst_ref=_window(land, axes[a], peer, rows, cols),
                                                          **on(j, (px, py, c))))
            else:
                cp = pltpu.make_async_remote_copy(src_ref=_window(src, axes[a], peer, rows, cols), dst_ref=land.at[j],
                                                  **on(j, (px, py, c)))
                starts.append(cp)
                waits.append(cp)
    return starts, waits


HBM = pl.BlockSpec(memory_space=pltpu.HBM)
SEM = pl.BlockSpec(memory_space=pltpu.SEMAPHORE)


def _shard_shapes(kind, arrs, axes):
    if kind != "scatter":
        return [a.shape for a in arrs]
    return [(a.shape[0] // N_CHIPS, a.shape[1]) if ax == 0 else (a.shape[0], a.shape[1] // N_CHIPS) for a, ax in zip(arrs, axes)]


def _land_shapes(kind, arrs, axes):
    if kind == "gather":
        return [(N_CHIPS * a.shape[0], a.shape[1]) if ax == 0 else (a.shape[0], N_CHIPS * a.shape[1]) for a, ax in zip(arrs, axes)]
    if kind == "scatter":
        return [(N_CHIPS,) + s for s in _shard_shapes(kind, arrs, axes)]
    return [a.shape if kind == "sibling" else (len(ALL_RELS) + 1,) + a.shape for a in arrs]


def _exchange_start(name, kind, arrs, axes, after=()):
    n, n_after = len(arrs), len(after)
    shards = _shard_shapes(kind, arrs, axes)
    land_shapes = _land_shapes(kind, arrs, axes)
    lands = [lax.empty(s, a.dtype) for s, a in zip(land_shapes, arrs)]

    def kern(*refs):
        outs = refs[2 * n + n_after:]
        starts, _ = _copies(kind, refs[:n], refs[n:2 * n], shards, axes, outs[0], outs[1], outs[2])
        for cp in starts:
            cp.start()
        outs[-1][...] = jnp.zeros_like(outs[-1])

    kept = [pltpu.HBM(a.shape, a.dtype) for a in arrs] + [pltpu.HBM(s, a.dtype) for s, a in zip(land_shapes, arrs)]
    n_sems = N_PEERS[kind] * n
    res = pl.pallas_call(
        kern, name=name, in_specs=[HBM] * (2 * n) + [ANY] * n_after,
        out_specs=[SEM] * 3 + [HBM] * (2 * n) + [pl.BlockSpec(memory_space=pltpu.VMEM)],
        out_shape=[pltpu.SemaphoreType.DMA((n_sems,)), pltpu.SemaphoreType.DMA((n_sems,)), pltpu.SemaphoreType.DMA((n,))]
        + kept + [jax.ShapeDtypeStruct((SUBLANE, LANE), F32)],
        input_output_aliases={i: 3 + i for i in range(2 * n)},
        compiler_params=pltpu.CompilerParams(has_side_effects=pltpu.SideEffectType.DATAFLOW_SIDE_EFFECTING),
    )(*[pltpu.with_memory_space_constraint(a, pltpu.HBM) for a in list(arrs) + lands], *after)
    return res[:3], res[3:3 + n], res[3 + n:3 + 2 * n], res[-1]


def _exchange_wait(name, kind, started, axes, after, sources_too=False):
    sems, srcs, lands, _ = started
    n, n_after = len(srcs), len(after)
    shards = _shard_shapes(kind, srcs, axes)

    def kern(*refs):
        sem_refs = refs[2 * n:2 * n + 3]
        _, waits = _copies(kind, refs[:n], refs[n:2 * n], shards, axes, *sem_refs)
        for cp in waits:
            cp.wait()

    res = pl.pallas_call(
        kern, name=name, in_specs=[HBM] * (2 * n) + [SEM] * 3 + [ANY] * n_after, out_specs=[HBM] * (2 * n),
        out_shape=[pltpu.HBM(a.shape, a.dtype) for a in list(srcs) + list(lands)],
        input_output_aliases={i: i for i in range(2 * n)},
        compiler_params=pltpu.CompilerParams(has_side_effects=pltpu.SideEffectType.DATAFLOW_SIDE_EFFECTING),
    )(*srcs, *lands, *sems, *after)
    return (res[:n], res[n:]) if sources_too else res[n:]


def _sum_devices(parts):
    def kern(p_ref, o_ref):
        acc = p_ref[0]
        for d in range(1, parts.shape[0]):
            acc = acc + p_ref[d]
        o_ref[...] = acc

    return pl.pallas_call(kern, name="sum_devices", out_shape=jax.ShapeDtypeStruct(parts.shape[1:], F32),
                          compiler_params=pltpu.CompilerParams(vmem_limit_bytes=VMEM_LIMIT_BYTES))(parts)


def _sum_slots(name, parts):
    _, rows, cols = parts.shape
    tr = _pick(rows, (ROW_TILE, 128, 64, 32))

    def kern(p_ref, o_ref):
        o_ref[...] = ((p_ref[3].astype(F32) + p_ref[0].astype(F32)) + p_ref[1].astype(F32)) + p_ref[2].astype(F32)

    return pl.pallas_call(kern, name=name, grid=(rows // tr,),
                          in_specs=[pl.BlockSpec((N_CHIPS, tr, cols), lambda i: (0, i, 0))],
                          out_specs=pl.BlockSpec((tr, cols), lambda i: (i, 0)),
                          out_shape=jax.ShapeDtypeStruct((rows, cols), F32), compiler_params=_params("arbitrary"))(parts)


def _adamw_math(g, w, m, v):
    m2 = ADAM_B1 * m + (1.0 - ADAM_B1) * g
    v2 = ADAM_B2 * v + (1.0 - ADAM_B2) * (g * g)
    m_hat = m2 / (1.0 - ADAM_B1 ** ADAM_STEP)
    v_hat = v2 / (1.0 - ADAM_B2 ** ADAM_STEP)
    return -ADAM_LR * (m_hat / (jnp.sqrt(v_hat) + ADAM_EPS) + ADAM_WD * w), m2, v2


def _adamw(name, parts, w, m, v):
    rows, cols = w.shape
    tr = rows if rows * cols <= WHOLE_ELEMS else _pick(rows, (ROW_TILE, 352, 128, 64, 32, 8))
    n = len(parts)

    def kern(*refs):
        g = refs[0][:, pl.ds(0, cols)]
        for p in refs[1:n]:
            g = g + p[:, pl.ds(0, cols)]
        d, m2, v2 = _adamw_math(g, refs[n][...], refs[n + 1][...], refs[n + 2][...])
        refs[n + 3][...] = g
        refs[n + 4][...] = d
        refs[n + 5][...] = m2
        refs[n + 6][...] = v2

    spec = pl.BlockSpec((tr, cols), lambda i: (i, 0))
    return pl.pallas_call(kern, name=name, grid=(rows // tr,),
                          in_specs=[pl.BlockSpec((tr, p.shape[1]), lambda i: (i, 0)) for p in parts] + [spec] * 3,
                          out_specs=[spec] * 4, out_shape=[jax.ShapeDtypeStruct((rows, cols), F32)] * 4,
                          compiler_params=_params("arbitrary"))(*parts, w, m, v)


def _adamw_many(name, gs, ws, ms, vs):
    n = len(gs)

    def kern(*refs):
        for p in range(n):
            d, m2, v2 = _adamw_math(refs[p][...], refs[n + p][...], refs[2 * n + p][...], refs[3 * n + p][...])
            refs[4 * n + p][...] = d
            refs[5 * n + p][...] = m2
            refs[6 * n + p][...] = v2

    res = pl.pallas_call(kern, name=name, out_shape=[jax.ShapeDtypeStruct(w.shape, F32) for w in ws] * 3,
                         compiler_params=pltpu.CompilerParams(vmem_limit_bytes=VMEM_LIMIT_BYTES))(*gs, *ws, *ms, *vs)
    return res[:n], res[n:2 * n], res[2 * n:]


def _pack(arrs):
    parts, rows = [], []
    for a in arrs:
        r = _round_up(-(-a.size // LANE), SUBLANE)
        parts.append(jnp.pad(a.reshape(-1).astype(F32), (0, r * LANE - a.size)).reshape(r, LANE))
        rows.append(r)
    return jnp.concatenate(parts, axis=0), rows


def _unpack(buf, rows, shapes):
    out, r0 = [], 0
    for r, s in zip(rows, shapes):
        size = math.prod(s)
        out.append(buf[r0:r0 + r].reshape(-1)[:size].reshape(s))
        r0 += r
    return out


def kernel(x, norm_ffn1, ffn1_w1, ffn1_w3, ffn1_w2, norm_mix, w_in, conv_w, conv_b, conv_ln_g, conv_ln_b, conv_out_g, ssm_A_re, ssm_A_im, ssm_log_dt, ssm_B_re, ssm_B_im, ssm_C_re, ssm_C_im, ssm_D, ssm_glu_w, ssm_glu_b, ssm_out_g, w_out, norm_ffn2, ffn2_w1, ffn2_w3, ffn2_w2, norm_final, loss_target, m_norm_ffn1, m_ffn1_w1, m_ffn1_w3, m_ffn1_w2, m_norm_mix, m_w_in, m_conv_w, m_conv_b, m_conv_ln_g, m_conv_ln_b, m_conv_out_g, m_ssm_A_re, m_ssm_A_im, m_ssm_log_dt, m_ssm_B_re, m_ssm_B_im, m_ssm_C_re, m_ssm_C_im, m_ssm_D, m_ssm_glu_w, m_ssm_glu_b, m_ssm_out_g, m_w_out, m_norm_ffn2, m_ffn2_w1, m_ffn2_w3, m_ffn2_w2, m_norm_final, v_norm_ffn1, v_ffn1_w1, v_ffn1_w3, v_ffn1_w2, v_norm_mix, v_w_in, v_conv_w, v_conv_b, v_conv_ln_g, v_conv_ln_b, v_conv_out_g, v_ssm_A_re, v_ssm_A_im, v_ssm_log_dt, v_ssm_B_re, v_ssm_B_im, v_ssm_C_re, v_ssm_C_im, v_ssm_D, v_ssm_glu_w, v_ssm_glu_b, v_ssm_out_g, v_w_out, v_norm_ffn2, v_ffn2_w1, v_ffn2_w3, v_ffn2_w2, v_norm_final):
    given = dict(locals())
    wts = {n: given[n] for n in WEIGHTS}
    n_seq, seq, d = x.shape
    n_rows = n_seq * seq
    xf = x.reshape(n_rows, d)
    tgt = loss_target.reshape(n_rows, d)
    row = lambda a: a.reshape(1, -1)

    f = ffn1_w1.shape[-1]
    fp = _round_up(f, LANE)
    held = lambda n, a: a[0].T if n in TRANSPOSED else a[0]
    shards = []
    for n in BIG:
        s = held(n, wts[n]).astype(BF16)
        if n.startswith('ffn'):
            s = jnp.pad(s, ((0, fp - f), (0, 0)))
        shards.append(s)
    n_taps, c_shard = conv_w.shape[1], conv_w.shape[2]
    shards.append(jnp.pad(conv_w[0], ((0, HALO - n_taps), (0, 0))))
    shard_of = dict(zip(BIG + ['conv_w'], shards))
    axis_of = dict(BIG_AXIS, conv_w=1)
    groups = [['ffn1_w1', 'ffn1_w3'], ['ffn1_w2', 'w_in', 'conv_w', 'ssm_glu_w', 'w_out'], ['ffn2_w1', 'ffn2_w3', 'ffn2_w2']]
    fetch, tok = [], []
    for k, names in enumerate(groups):
        fetch.append(_exchange_start("gather%d_send" % k, "gather", [shard_of[n] for n in names],
                                     [axis_of[n] for n in names], tok))
        tok = [fetch[-1][3]]
    full = {}

    def arrive(k, after):
        lands = _exchange_wait("gather%d_recv" % k, "gather", fetch[k], [axis_of[n] for n in groups[k]], after)
        full.update(zip(groups[k], lands))

    h1, h1_t = _rms_fwd("ffn1_rms", xf, norm_ffn1)
    arrive(0, tok + [h1])

    _, n_grp, n_state = ssm_A_re.shape
    grp = ssm_B_re.shape[-1]
    ns = n_grp * n_state
    c_ssm = n_grp * grp
    lr, li = ssm_A_re.reshape(1, ns), ssm_A_im.reshape(1, ns)
    ldt = jnp.repeat(ssm_log_dt.reshape(n_grp), n_state).reshape(1, ns)
    btr = ssm_B_re[0].transpose(2, 0, 1).reshape(grp, ns)
    bti = ssm_B_im[0].transpose(2, 0, 1).reshape(grp, ns)
    ctr = ssm_C_re[0].transpose(1, 0, 2).reshape(grp, ns)
    cti = ssm_C_im[0].transpose(1, 0, 2).reshape(grp, ns)
    scan_tile = _pick(seq, (SCAN_TILE,))
    _, _, bbr, bbi, seg_up, seg_down, pw, pw_falling = _s5_params_fwd(lr, li, ldt, btr, bti, scan_tile // SUBLANE)
    nb = c_ssm // LANE
    sb, gpb = ns // nb, n_grp // nb
    diag = (jnp.arange(LANE)[:, None] // grp) == (jnp.arange(sb)[None, :] // n_state)

    def spread(t):
        return jnp.where(diag, jnp.tile(t.reshape(grp, nb, sb).transpose(1, 0, 2), (1, gpb, 1)), 0.0)

    def gather_diag(t):
        return (t * diag).reshape(nb, gpb, grp, sb).sum(1).transpose(1, 0, 2).reshape(grp, ns)

    def interleave(re, im):
        return jnp.stack([re.reshape(-1, nb, sb), im.reshape(-1, nb, sb)], axis=2).reshape(-1, 2 * ns)

    bdc = jnp.concatenate([spread(bbr), spread(bbi)], axis=2).astype(BF16)
    cdc = jnp.concatenate([spread(ctr).transpose(0, 2, 1), -spread(cti).transpose(0, 2, 1)], axis=1).astype(BF16)
    rowi = jnp.arange(SUBLANE)[:, None]
    pwf, pwc = interleave(pw[:, :ns], pw[:, ns:]), interleave(pw[:, :ns], -pw[:, ns:])
    tabs_f = [jnp.where(rowi >= s, pwf[s - 1][None, :], 0.0) for s in (1, 2, 4)] + [pwf]
    tabs_b = [jnp.where(rowi <= SUBLANE - 1 - s, pwc[s - 1][None, :], 0.0) for s in (1, 2, 4)]
    tabs_b.append(interleave(pw_falling[:, :ns], -pw_falling[:, ns:]))
    fix_f = jnp.repeat(interleave(seg_up[:, :ns], seg_up[:, ns:]), SUBLANE, axis=0)
    fix_b = jnp.repeat(interleave(seg_down[:, :ns], -seg_down[:, ns:]), SUBLANE, axis=0)
    c_conv = conv_b.shape[1]
    u_blk = 2 * c_conv // LANE

    a1, b1, z1 = _ffn_up("ffn1_up", h1, full['ffn1_w1'], full['ffn1_w3'])
    arrive(1, [z1])
    x1, h2, h2_t = _mm("ffn1_down", z1, full['ffn1_w2'], 1, 0, addend=xf, alpha=0.5, post=_post_rms(norm_mix))
    saved1 = (h1_t, a1, b1, z1)
    cw = full['conv_w']
    proj = _mm("mix_in", h2, full['w_in'], 1, 0, F32)
    assert c_conv == c_ssm and proj.shape[1] == 3 * c_conv
    cpre, an = _conv_fwd(proj, cw, conv_b, conv_ln_g, conv_ln_b, conv_out_g, seq)
    xs, ypre, yg = _s5_fwd(proj, u_blk, bdc, cdc, fix_f, tabs_f, ssm_D, seq, sb)
    q0 = _mm("s5_gate", yg, full['ssm_glu_w'], 1, 0, F32)
    sn = _s5_post2(yg, q0, ssm_glu_b, ssm_out_g)
    wo = full['w_out']
    mixed = jnp.concatenate([an, sn], axis=1)
    x2, h3, h3_t = _mm("mix_out", mixed, wo, 1, 0, addend=x1, post=_post_rms(norm_ffn2))
    arrive(2, [x2])
    a3, b3, z3 = _ffn_up("ffn2_up", h3, full['ffn2_w1'], full['ffn2_w3'])
    saved2 = (h3_t, a3, b3, z3)
    dx3, dx3_t, loss_row, d_norm_final = _mm("ffn2_down", z3, full['ffn2_w2'], 1, 0, addend=x2, alpha=0.5,
                                             post=_post_loss(row(norm_final), tgt))

    g = {}
    dx2, g['norm_ffn2'], sent = _ffn_bwd("ffn2", x2, norm_ffn2, full['ffn2_w1'], full['ffn2_w3'], full['ffn2_w2'], saved2,
                                         dx3, dx3_t, early=False)
    dmixed = _mm("mix_dmixed", dx2, wo, 1, 1, F32)
    dwo = _mm("mix_dwo", mixed, dx2, 0, 0, BF16)
    dq, dyg1, g['ssm_out_g'], g['ssm_glu_b'] = _s5_post2_bwd(dmixed, yg, q0, ssm_glu_b, ssm_out_g)
    dyg2 = _mm("s5_dgate", dq, full['ssm_glu_w'], 1, 1, F32)
    dwg = _mm("s5_dwg", yg, dq, 0, 0, BF16)
    dypre, du_skip, g['ssm_D'] = _s5_post1_bwd(dyg1, dyg2, ypre, proj, ssm_D)
    du, dabar, dbdc, dcdc = _s5_bwd(dypre, du_skip, xs, proj, u_blk, bdc, cdc, fix_b, tabs_b, seq, sb)
    dabar = dabar.reshape(nb, 2, sb)
    dlr, dli, dldt, dbtr, dbti = _s5_params_bwd(lr, li, ldt, btr, bti, dabar[:, 0].reshape(1, ns), dabar[:, 1].reshape(1, ns),
                                                gather_diag(dbdc[:, :, :sb]), gather_diag(dbdc[:, :, sb:]))
    g['ssm_A_re'], g['ssm_A_im'] = dlr, dli
    g['ssm_log_dt'] = dldt.reshape(n_grp, n_state).sum(axis=1)
    g['ssm_B_re'] = dbtr.reshape(grp, n_grp, n_state).transpose(1, 2, 0)
    g['ssm_B_im'] = dbti.reshape(grp, n_grp, n_state).transpose(1, 2, 0)
    g['ssm_C_re'] = gather_diag(dcdc[:, :sb].transpose(0, 2, 1)).reshape(grp, n_grp, n_state).transpose(1, 0, 2)
    g['ssm_C_im'] = -gather_diag(dcdc[:, sb:].transpose(0, 2, 1)).reshape(grp, n_grp, n_state).transpose(1, 0, 2)
    dc, g['conv_out_g'], g['conv_ln_g'], g['conv_ln_b'], g['conv_b'] = _conv_bwd_rows(dmixed, cpre, conv_ln_g, conv_ln_b,
                                                                                    conv_out_g)
    dval, dgate, dcw = _conv_bwd_taps(proj, dc, cw, seq)
    dproj = jnp.concatenate([dval, dgate, du], axis=1)
    dwin = _mm("mix_dwin", h2_t, dproj, 1, 0, BF16)
    sent['w_out ssm_glu_w w_in'] = (_exchange_start("mix_send", "scatter", [dwo, dwg, dwin], [0, 0, 1]), [0, 0, 1])
    dx1, dx1_t, g['norm_mix'] = _mm("mix_dh", dproj, full['w_in'], 1, 1, after=[sent['w_out ssm_glu_w w_in'][0][3]],
                                    post=_post_rms_bwd(x1, norm_mix, dx2))
    dx0, g['norm_ffn1'], sent1 = _ffn_bwd("ffn1", xf, norm_ffn1, full['ffn1_w1'], full['ffn1_w3'], full['ffn1_w2'], saved1,
                                          dx1, dx1_t, early=True)
    sent.update(sent1)
    g['norm_final'] = d_norm_final
    g['conv_w'] = dcw[:n_taps]

    small_shapes = [(n_taps, c_conv) if n == 'conv_w' else wts[n].shape for n in SMALL]
    buf, buf_rows = _pack([g[n] for n in SMALL] + [loss_row])
    to_all = _exchange_start("small_send", "all", [buf], [0])
    slots = {}
    for names, (started, axes) in sent.items():
        lands = _exchange_wait(names.replace(' ', '_') + "_recv", "scatter", started, axes, after=[dx0, to_all[3]])
        slots.update(zip(names.split(), lands))
    sums = [_sum_slots("sum_" + n, slots[n]) for n in BIG]
    to_sibling = _exchange_start("sums_send", "sibling", sums, [0] * len(sums))
    from_all = _exchange_wait("small_recv", "all", to_all, [0], after=[to_sibling[3]])[0]
    total = _unpack(_sum_devices(from_all), buf_rows, small_shapes + [(1, LANE)])
    loss = total[-1][0, 0]
    grads = dict(zip(SMALL, total[:-1]))
    chip = 2 * lax.axis_index("x") + lax.axis_index("y")
    grads['conv_w'] = lax.dynamic_slice_in_dim(grads['conv_w'], chip * c_shard, c_shard, axis=1)[None]
    flat = lambda a: a.reshape(-1, a.shape[-1])
    small = _adamw_many("adamw_small", *[[flat(src[p + n]) for n in SMALL]
                                         for src, p in ((grads, ''), (given, ''), (given, 'm_'), (given, 'v_'))])
    deltas, new_m, new_v = ({n: o.reshape(wts[n].shape) for n, o in zip(SMALL, outs)} for outs in small)

    sums, theirs = _exchange_wait("sums_recv", "sibling", to_sibling, [0] * len(sums), after=[new_v[SMALL[-1]]],
                                  sources_too=True)
    for n, mine, other in zip(BIG, sums, theirs):
        grads[n], deltas[n], new_m[n], new_v[n] = (
            (o.T if n in TRANSPOSED else o)[None]
            for o in _adamw("adamw_" + n, [mine, other], held(n, given[n]), held(n, given['m_' + n]), held(n, given['v_' + n])))

    return (loss, dx0.reshape(x.shape), *[grads[n] for n in WEIGHTS], *[deltas[n] for n in WEIGHTS],
            *[new_m[n] for n in WEIGHTS], *[new_v[n] for n in WEIGHTS])
```

```python
import math
from typing import Callable, NamedTuple

import jax
import jax.numpy as jnp
from jax import lax
from jax.experimental import pallas as pl
from jax.experimental.pallas import tpu as pltpu

F32 = jnp.float32
BF16 = jnp.bfloat16
EPS = 1e-6
ADAM_LR, ADAM_B1, ADAM_B2, ADAM_EPS, ADAM_WD, ADAM_STEP = 0.001, 0.9, 0.999, 1e-08, 0.01, 10
MESH = pl.DeviceIdType.MESH
ANY = pl.BlockSpec(memory_space=pl.ANY)
LANE = 128
SUBLANE = 8
VMEM_LIMIT_BYTES = 56 << 20
ROW_TILE = 256
ROW_TILE_ELEMS = 256 * 1024
WHOLE_ELEMS = 512 * 1024
WHOLE_WEIGHT_BYTES = 8 << 20
FFN_ROWS = 256
CONV_TILE = 128
CONV_SUB = 32
HALO = 32
SCAN_TILE = 256
SCAN_COLS = 512
N_CHIPS = 4
CHIP_RELS = ((1, 0), (0, 1), (1, 1))
NT = (((1,), (1,)), ((), ()))
GELU_K = math.sqrt(2.0 / math.pi)
GELU_C = 0.044715

WEIGHTS = ['norm_ffn1', 'ffn1_w1', 'ffn1_w3', 'ffn1_w2', 'norm_mix', 'w_in', 'conv_w', 'conv_b', 'conv_ln_g', 'conv_ln_b',
           'conv_out_g', 'ssm_A_re', 'ssm_A_im', 'ssm_log_dt', 'ssm_B_re', 'ssm_B_im', 'ssm_C_re', 'ssm_C_im', 'ssm_D',
           'ssm_glu_w', 'ssm_glu_b', 'ssm_out_g', 'w_out', 'norm_ffn2', 'ffn2_w1', 'ffn2_w3', 'ffn2_w2', 'norm_final']
BIG = ['ffn1_w1', 'ffn1_w3', 'ffn1_w2', 'w_in', 'ssm_glu_w', 'w_out', 'ffn2_w1', 'ffn2_w3', 'ffn2_w2']
BIG_AXIS = {'ffn1_w1': 0, 'ffn1_w3': 0, 'ffn1_w2': 0, 'w_in': 1, 'ssm_glu_w': 0, 'w_out': 0, 'ffn2_w1': 0, 'ffn2_w3': 0,
            'ffn2_w2': 0}
TRANSPOSED = ('ffn1_w1', 'ffn1_w3', 'ffn2_w1', 'ffn2_w3')
SMALL = [n for n in WEIGHTS if n not in BIG]


def _round_up(n, m):
    return -(-n // m) * m


def _pick(n, cands):
    for c in cands:
        if c <= n and n % c == 0:
            return c
    return n


def _params(*sem):
    return pltpu.CompilerParams(dimension_semantics=sem, vmem_limit_bytes=VMEM_LIMIT_BYTES)


def _rms_r(x):
    return lax.rsqrt(jnp.mean(x * x, axis=-1, keepdims=True) + EPS)


def _rms_bwd(x, r, g, dy):
    dyg = dy * g
    return r * dyg - x * (r * r * r) * jnp.mean(x * dyg, axis=-1, keepdims=True)


def _sigmoid(x):
    return jax.nn.sigmoid(x)


def _dsilu(a, s):
    return s * (1.0 + a * (1.0 - s))


def _gelu(x):
    return 0.5 * x * (1.0 + jnp.tanh(GELU_K * (x + GELU_C * x * x * x)))


def _dgelu(x):
    t = jnp.tanh(GELU_K * (x + GELU_C * x * x * x))
    return 0.5 * (1.0 + t) + 0.5 * x * (1.0 - t * t) * GELU_K * (1.0 + 3.0 * GELU_C * x * x)


def _colsum(v):
    return jnp.sum(v, axis=0, keepdims=True)


def _rowwise(name, body, n_rows, row_ins, par_ins, row_outs, acc_outs, after=()):
    widest = max([w for (_, w, _) in row_ins] + [w for (w, _) in row_outs])
    tt = _pick(n_rows, [t for t in (256, 128, 64, 32, 16, 8) if t * widest <= ROW_TILE_ELEMS])
    in_specs = [pl.BlockSpec((tt, w), lambda i, cb=cb: (i, cb)) for (_, w, cb) in row_ins]
    in_specs += [pl.BlockSpec(p.shape, lambda i: (0, 0)) for p in par_ins] + [ANY] * len(after)
    out_specs = [pl.BlockSpec((tt, w), lambda i: (i, 0)) for (w, _) in row_outs]
    out_specs += [pl.BlockSpec((r, w), lambda i: (0, 0)) for (r, w) in acc_outs]
    out_shape = [jax.ShapeDtypeStruct((n_rows, w), dt) for (w, dt) in row_outs]
    out_shape += [jax.ShapeDtypeStruct((r, w), F32) for (r, w) in acc_outs]
    n_in, n_ro = len(row_ins) + len(par_ins), len(row_outs)
    o0 = n_in + len(after)

    def kern(*refs):
        accs = refs[o0 + n_ro:]
        if accs:
            @pl.when(pl.program_id(0) == 0)
            def _():
                for a in accs:
                    a[...] = jnp.zeros_like(a)
        body(refs[:n_in], refs[o0:o0 + n_ro], accs)

    return pl.pallas_call(kern, name=name, grid=(n_rows // tt,), in_specs=in_specs, out_specs=out_specs, out_shape=out_shape,
                          compiler_params=_params("arbitrary"))(*[a for a, _, _ in row_ins], *par_ins, *after)


class Post(NamedTuple):
    rows: list
    gains: list
    outs: list
    t_outs: list
    sums: list
    fn: Callable


def _post_rms(gain):
    def fn(r, rows, gains):
        h = r * _rms_r(r) * gains[0]
        return [r, h, h], []

    return Post([], [gain], [F32, BF16], [BF16], [], fn)


def _post_rms_bwd(x, gain, dres):
    def fn(dh, rows, gains):
        r = _rms_r(rows[0])
        dx = rows[1] + _rms_bwd(rows[0], r, gains[0], dh)
        return [dx, dx], [_colsum(dh * rows[0] * r)]

    return Post([x, dres], [gain], [F32], [BF16], [x.shape[1]], fn)


def _post_loss(gain, tgt):
    d = tgt.shape[1]

    def fn(xv, rows, gains):
        r = _rms_r(xv)
        e = xv * r * gains[0] - rows[0]
        sq = jnp.sum(jnp.sum(e * e, axis=-1, keepdims=True), axis=0, keepdims=True)
        dy = e * (1.0 / d)
        dx = _rms_bwd(xv, r, gains[0], dy)
        return [dx, dx], [jnp.broadcast_to(sq * (0.5 / d), (1, LANE)), _colsum(dy * xv * r)]

    return Post([tgt], [gain], [F32], [BF16], [LANE, d], fn)


def _mm(name, a, b, ca, cb, out_dtype=F32, addend=None, alpha=1.0, a_cols=None, after=(), post=None, transposed=False):
    a_start, a_width = a_cols if a_cols else (0, a.shape[1])
    m, k = (a.shape[0], a_width) if ca == 1 else (a_width, a.shape[0])
    n = b.shape[1 - cb]
    assert b.shape[cb] == k, (name, a.shape, b.shape)
    tn = _pick(n, (1024, 768, 512, 384, 256, 128))
    if post and k * tn * b.dtype.itemsize <= WHOLE_WEIGHT_BYTES:
        tk = k
        tm = _pick(m, (256, 128) if k > 1024 else (512, 256, 128))
    else:
        tm = _pick(m, (512, 256, 128) if post else (1024, 512, 256, 128))
        tk = _pick(k, (2048, 1024, 768, 512, 256, 128) if k >= 4096 and not post else (1024, 768, 512, 256, 128))
    nk = k // tk
    if ca == 1:
        assert a_start % tk == 0
        a_spec = pl.BlockSpec((tm, tk), lambda i, j, kk: (i, kk + a_start // tk))
    else:
        assert a_start % tm == 0
        a_spec = pl.BlockSpec((tk, tm), lambda i, j, kk: (kk, i + a_start // tm))
    b_spec = pl.BlockSpec((tk, tn), lambda i, j, kk: (kk, j)) if cb == 0 else pl.BlockSpec((tn, tk), lambda i, j, kk: (j, kk))
    o_spec = pl.BlockSpec((tm, tn), lambda i, j, kk: (i, j))
    t_spec = pl.BlockSpec((tn, tm), lambda i, j, kk: (j, i))
    fixed = lambda w: pl.BlockSpec((1, w), lambda i, j, kk: (0, 0))
    ins, in_specs = [a, b], [a_spec, b_spec]
    if addend is not None:
        ins.append(addend)
        in_specs.append(o_spec)
    n_plain = len(ins)
    n_rows, n_gains = (len(post.rows), len(post.gains)) if post else (0, 0)
    if post:
        assert tn == n, name
        ins += post.rows + post.gains
        in_specs += [o_spec] * n_rows + [fixed(n)] * n_gains
    ins += list(after)
    in_specs += [ANY] * len(after)
    n_in = len(ins)
    if post:
        n_straight, n_vals = len(post.outs), len(post.outs) + len(post.t_outs)
        out_specs = [o_spec] * n_straight + [t_spec] * len(post.t_outs) + [fixed(w) for w in post.sums]
        out_shape = [jax.ShapeDtypeStruct((m, n), dt) for dt in post.outs] + [jax.ShapeDtypeStruct((n, m), dt) for dt in post.t_outs]
        out_shape += [jax.ShapeDtypeStruct((1, w), F32) for w in post.sums]
    elif transposed:
        out_specs, out_shape = [t_spec], [jax.ShapeDtypeStruct((n, m), out_dtype)]
    else:
        out_specs, out_shape = [o_spec], [jax.ShapeDtypeStruct((m, n), out_dtype)]
    n_out = len(out_specs)
    dims = (((ca,), (cb,)), ((), ()))

    def emit(refs, r):
        if alpha != 1.0:
            r = r * alpha
        if addend is not None:
            r = r + refs[2][...].astype(F32)
        outs = refs[n_in:n_in + n_out]
        if post is None:
            outs[0][...] = (r.T if transposed else r).astype(out_dtype)
            return
        vals, incs = post.fn(r, [q[...] for q in refs[n_plain:n_plain + n_rows]],
                             [q[...] for q in refs[n_plain + n_rows:n_plain + n_rows + n_gains]])
        for at, (o_ref, val) in enumerate(zip(outs, vals)):
            o_ref[...] = (val if at < n_straight else val.T).astype(o_ref.dtype)
        for s_ref, inc in zip(outs[n_vals:], incs):
            s_ref[...] += inc

    def kern(*refs):
        kk = pl.program_id(2)
        if post and post.sums:
            @pl.when(jnp.logical_and(jnp.logical_and(pl.program_id(0) == 0, pl.program_id(1) == 0), kk == 0))
            def _():
                for s_ref in refs[n_in + n_vals:n_in + n_out]:
                    s_ref[...] = jnp.zeros_like(s_ref)

        dot = lambda: lax.dot_general(refs[0][...].astype(BF16), refs[1][...].astype(BF16), dims,
                                      preferred_element_type=F32)
        if nk == 1:
            emit(refs, dot())
            return
        acc_ref = refs[-1]

        @pl.when(kk == 0)
        def _():
            acc_ref[...] = jnp.zeros_like(acc_ref)

        acc_ref[...] += dot()

        @pl.when(kk == nk - 1)
        def _():
            emit(refs, acc_ref[...])

    res = pl.pallas_call(kern, name=name, grid=(m // tm, n // tn, nk), in_specs=in_specs, out_specs=out_specs,
                         out_shape=out_shape, scratch_shapes=[] if nk == 1 else [pltpu.VMEM((tm, tn), F32)],
                         compiler_params=_params("arbitrary", "arbitrary", "arbitrary"))(*ins)
    return res if post else res[0]


def _rms_fwd(name, x, g):
    t, d = x.shape
    tt = _pick(t, (ROW_TILE, LANE))

    def kern(x_ref, g_ref, h_ref, ht_ref):
        xv = x_ref[...]
        h = xv * _rms_r(xv) * g_ref[...]
        h_ref[...] = h.astype(BF16)
        ht_ref[...] = h.T.astype(BF16)

    return pl.pallas_call(kern, name=name, grid=(t // tt,),
                          in_specs=[pl.BlockSpec((tt, d), lambda i: (i, 0)), pl.BlockSpec((1, d), lambda i: (0, 0))],
                          out_specs=[pl.BlockSpec((tt, d), lambda i: (i, 0)), pl.BlockSpec((d, tt), lambda i: (0, i))],
                          out_shape=[jax.ShapeDtypeStruct((t, d), BF16), jax.ShapeDtypeStruct((d, t), BF16)],
                          compiler_params=_params("arbitrary"))(x, g)


def _ffn_up(name, h, w1, w3):
    t, d = h.shape
    ff = w1.shape[0]
    tm, tn = _pick(t, (1024, 512, 256, 128)), _pick(ff, (1024, 768, 512, 256, 128))

    def kern(h_ref, w1_ref, w3_ref, a_ref, b_ref, z_ref):
        hv = h_ref[...]
        a = lax.dot_general(hv, w1_ref[...], NT, preferred_element_type=F32)
        b = lax.dot_general(hv, w3_ref[...], NT, preferred_element_type=F32)
        a_ref[...] = a.astype(BF16)
        b_ref[...] = b.astype(BF16)
        z_ref[...] = (a * _sigmoid(a) * b).astype(BF16)

    w_spec = pl.BlockSpec((tn, d), lambda i, j: (j, 0))
    o_spec = pl.BlockSpec((tm, tn), lambda i, j: (i, j))
    return pl.pallas_call(kern, name=name, grid=(t // tm, ff // tn),
                          in_specs=[pl.BlockSpec((tm, d), lambda i, j: (i, 0)), w_spec, w_spec], out_specs=[o_spec] * 3,
                          out_shape=[jax.ShapeDtypeStruct((t, ff), BF16)] * 3,
                          compiler_params=_params("arbitrary", "arbitrary"))(h, w1, w3)


def _ffn_up_gate(name, h, w3, a):
    t, d = h.shape
    ff = w3.shape[0]
    tm, tn = _pick(t, (1024, 512, 256, 128)), _pick(ff, (1024, 768, 512, 256, 128))

    def kern(h_ref, w3_ref, a_ref, b_ref, z_ref):
        b = lax.dot_general(h_ref[...], w3_ref[...], NT, preferred_element_type=F32)
        av = a_ref[...].astype(F32)
        b_ref[...] = b.astype(BF16)
        z_ref[...] = (av * _sigmoid(av) * b).astype(BF16)

    o_spec = pl.BlockSpec((tm, tn), lambda i, j: (i, j))
    return pl.pallas_call(kern, name=name, grid=(t // tm, ff // tn),
                          in_specs=[pl.BlockSpec((tm, d), lambda i, j: (i, 0)), pl.BlockSpec((tn, d), lambda i, j: (j, 0)),
                                    o_spec], out_specs=[o_spec] * 2, out_shape=[jax.ShapeDtypeStruct((t, ff), BF16)] * 2,
                          compiler_params=_params("arbitrary", "arbitrary"))(h, w3, a)


def _ffn_dglu(name, dxo, w2, a, b, after=()):
    t, d = dxo.shape
    ff = w2.shape[0]
    tm = _pick(t, (FFN_ROWS, 128))

    def kern(dx_ref, w2_ref, a_ref, b_ref, *rest):
        da_ref, db_ref = rest[-2:]
        dz = lax.dot_general(dx_ref[...].astype(BF16), w2_ref[...], NT, preferred_element_type=F32) * 0.5
        av, bv = a_ref[...].astype(F32), b_ref[...].astype(F32)
        s = _sigmoid(av)
        da_ref[...] = (dz * bv * _dsilu(av, s)).astype(BF16)
        db_ref[...] = (dz * av * s).astype(BF16)

    o_spec = pl.BlockSpec((tm, ff), lambda i: (i, 0))
    return pl.pallas_call(kern, name=name, grid=(t // tm,),
                          in_specs=[pl.BlockSpec((tm, d), lambda i: (i, 0)), pl.BlockSpec((ff, d), lambda i: (0, 0)),
                                    o_spec, o_spec] + [ANY] * len(after),
                          out_specs=[o_spec] * 2, out_shape=[jax.ShapeDtypeStruct((t, ff), BF16)] * 2,
                          compiler_params=_params("arbitrary"))(dxo, w2, a, b, *after)


def _ffn_dh(name, da, db, w1, w3, x, g, dres, after=()):
    t, d = x.shape
    ff = da.shape[1]
    tm = _pick(t, (FFN_ROWS, 128))

    def kern(da_ref, db_ref, w1_ref, w3_ref, x_ref, g_ref, dres_ref, *rest):
        dx_ref, dg_ref = rest[-2:]

        @pl.when(pl.program_id(0) == 0)
        def _():
            dg_ref[...] = jnp.zeros_like(dg_ref)

        dh = (jnp.dot(da_ref[...], w1_ref[...], preferred_element_type=F32)
              + jnp.dot(db_ref[...], w3_ref[...], preferred_element_type=F32))
        xv = x_ref[...]
        r = _rms_r(xv)
        dx_ref[...] = dres_ref[...] + _rms_bwd(xv, r, g_ref[...], dh)
        dg_ref[...] += _colsum(dh * xv * r)

    act = pl.BlockSpec((tm, ff), lambda i: (i, 0))
    wgt = pl.BlockSpec((ff, d), lambda i: (0, 0))
    rows = pl.BlockSpec((tm, d), lambda i: (i, 0))
    gain = pl.BlockSpec((1, d), lambda i: (0, 0))
    return pl.pallas_call(kern, name=name, grid=(t // tm,),
                          in_specs=[act, act, wgt, wgt, rows, gain, rows] + [ANY] * len(after), out_specs=[rows, gain],
                          out_shape=[jax.ShapeDtypeStruct((t, d), F32), jax.ShapeDtypeStruct((1, d), F32)],
                          compiler_params=_params("arbitrary"))(da, db, w1, w3, x, g, dres, *after)


def _ffn_bwd(tag, x, g, w1, w3, w2, saved, dxo, dxo_t, early):
    ht, a, b, z = saved
    dw2 = _mm(tag + "_dw2", dxo_t, z, 1, 0, BF16, alpha=0.5, transposed=True)
    da, db = _ffn_dglu(tag + "_dglu", dxo, w2, a, b)
    dw1 = _mm(tag + "_dw1", ht, da, 1, 0, BF16, transposed=True)
    sent, pin = {}, []
    if early:
        sent[tag + "_w2 " + tag + "_w1"] = (_exchange_start(tag + "_w2_w1_send", "scatter", [dw2, dw1], [0, 0]), [0, 0])
        pin = [sent[tag + "_w2 " + tag + "_w1"][0][3]]
    dw3 = _mm(tag + "_dw3", ht, db, 1, 0, BF16, after=pin, transposed=True)
    last = [dw3] if early else [dw2, dw1, dw3]
    names = [tag + "_w3"] if early else [tag + "_w2", tag + "_w1", tag + "_w3"]
    sent[" ".join(names)] = (_exchange_start(tag + "_w3_send", "scatter", last, [0] * len(last)), [0] * len(last))
    dx, dg = _ffn_dh(tag + "_dh", da, db, w1, w3, x, g, dxo, after=[sent[" ".join(names)][0][3]])
    return dx, dg, sent


def _shift_copies(ext_ref, sh_ref):
    n = ext_ref.shape[0] - SUBLANE
    for r in range(1, SUBLANE):
        sh_ref[r, pl.ds(0, n), :] = ext_ref[pl.ds(r, n), :]


def _rows_at(ext_ref, sh_ref, off, rows):
    r = off % SUBLANE
    return ext_ref[pl.ds(off, rows), :] if r == 0 else sh_ref[r, pl.ds(off - r, rows), :]


def _conv_fwd(proj, cw, cb, lng, lnb, og, seq):
    n_rows, c = proj.shape[0], cb.shape[1]
    kw = HALO - 1
    tt = _pick(seq, (CONV_TILE,))
    hb = tt // HALO

    def kern(v_ref, g_ref, vp_ref, gp_ref, w_ref, cb_ref, lg_ref, lb_ref, og_ref, c_ref, an_ref, ext_ref, sh_ref):
        first = (pl.program_id(0) * tt) % seq == 0
        ext_ref[pl.ds(HALO, tt), :] = v_ref[...] * _sigmoid(g_ref[...])
        ext_ref[pl.ds(0, HALO), :] = vp_ref[...] * _sigmoid(gp_ref[...]) * jnp.where(first, 0.0, 1.0)
        _shift_copies(ext_ref, sh_ref)
        for r0 in range(0, tt, CONV_SUB):
            rows = min(CONV_SUB, tt - r0)
            acc = jnp.zeros((rows, c), F32)
            for k in range(kw):
                acc = acc + w_ref[pl.ds(k, 1), :] * _rows_at(ext_ref, sh_ref, r0 + HALO - (kw - 1) + k, rows)
            c_ref[pl.ds(r0, rows), :] = acc + cb_ref[...]
        cv = c_ref[...]
        mu = jnp.mean(cv, axis=-1, keepdims=True)
        xc = cv - mu
        rstd = lax.rsqrt(jnp.mean(xc * xc, axis=-1, keepdims=True) + EPS)
        lv = xc * rstd * lg_ref[...] + lb_ref[...]
        sl = lv * _sigmoid(lv)
        an_ref[...] = (sl * _rms_r(sl) * og_ref[...]).astype(BF16)

    cur = lambda cbk: pl.BlockSpec((tt, c), lambda i: (i, cbk))
    prev = lambda cbk: pl.BlockSpec((HALO, c), lambda i: (jnp.maximum(i * hb - 1, 0), cbk))
    par = lambda p: pl.BlockSpec(p.shape, lambda i: (0, 0))
    return pl.pallas_call(
        kern, name="conv_fwd", grid=(n_rows // tt,),
        in_specs=[cur(0), cur(1), prev(0), prev(1), par(cw), par(cb), par(lng), par(lnb), par(og)],
        out_specs=[pl.BlockSpec((tt, c), lambda i: (i, 0))] * 2,
        out_shape=[jax.ShapeDtypeStruct((n_rows, c), F32), jax.ShapeDtypeStruct((n_rows, c), BF16)],
        scratch_shapes=[pltpu.VMEM((tt + HALO, c), F32), pltpu.VMEM((SUBLANE, tt + HALO, c), F32)],
        compiler_params=_params("arbitrary"),
    )(proj, proj, proj, proj, cw, cb, lng, lnb, og)


def _conv_bwd_rows(dmixed, cpre, lng, lnb, og):
    c = cpre.shape[1]

    def body(ins, outs, accs):
        dan, cv, lg, lb, ogv = ins[0][...], ins[1][...], ins[2][...], ins[3][...], ins[4][...]
        mu = jnp.mean(cv, axis=-1, keepdims=True)
        xc = cv - mu
        rstd = lax.rsqrt(jnp.mean(xc * xc, axis=-1, keepdims=True) + EPS)
        xh = xc * rstd
        lv = xh * lg + lb
        s = _sigmoid(lv)
        sl = lv * s
        r2 = _rms_r(sl)
        accs[0][...] += _colsum(dan * sl * r2)
        dl = _rms_bwd(sl, r2, ogv, dan) * _dsilu(lv, s)
        accs[1][...] += _colsum(dl * xh)
        accs[2][...] += _colsum(dl)
        dxh = dl * lg
        dc = rstd * (dxh - jnp.mean(dxh, axis=-1, keepdims=True) - xh * jnp.mean(dxh * xh, axis=-1, keepdims=True))
        outs[0][...] = dc
        accs[3][...] += _colsum(dc)

    return _rowwise("conv_bwd_rows", body, cpre.shape[0], [(dmixed, c, 0), (cpre, c, 0)], [lng, lnb, og], [(c, F32)],
                    [(1, c)] * 4)


def _conv_bwd_taps(proj, dc, cw, seq):
    n_rows, c = dc.shape
    kw = HALO - 1
    tt = _pick(seq, (CONV_TILE,))
    hb = tt // HALO
    last_blk = n_rows // HALO - 1

    def kern(v_ref, g_ref, vp_ref, gp_ref, dc_ref, dn_ref, w_ref, dv_ref, dg_ref, dw_ref, exta_ref, extd_ref, sha_ref, shd_ref):
        i = pl.program_id(0)
        first = (i * tt) % seq == 0
        last = ((i + 1) * tt) % seq == 0

        @pl.when(i == 0)
        def _():
            dw_ref[...] = jnp.zeros_like(dw_ref)

        sg = _sigmoid(g_ref[...])
        exta_ref[pl.ds(HALO, tt), :] = v_ref[...] * sg
        exta_ref[pl.ds(0, HALO), :] = vp_ref[...] * _sigmoid(gp_ref[...]) * jnp.where(first, 0.0, 1.0)
        dcv = dc_ref[...]
        extd_ref[pl.ds(0, tt), :] = dcv
        extd_ref[pl.ds(tt, HALO), :] = dn_ref[...] * jnp.where(last, 0.0, 1.0)
        _shift_copies(exta_ref, sha_ref)
        _shift_copies(extd_ref, shd_ref)
        for k in range(kw):
            dw_ref[pl.ds(k, 1), :] += _colsum(_rows_at(exta_ref, sha_ref, HALO - (kw - 1) + k, tt) * dcv)
        for r0 in range(0, tt, CONV_SUB):
            rows = min(CONV_SUB, tt - r0)
            acc = jnp.zeros((rows, c), F32)
            for k in range(kw):
                acc = acc + w_ref[pl.ds(k, 1), :] * _rows_at(extd_ref, shd_ref, r0 + (kw - 1) - k, rows)
            dv_ref[pl.ds(r0, rows), :] = acc
        da = dv_ref[...]
        dv_ref[...] = da * sg
        dg_ref[...] = da * v_ref[...] * sg * (1.0 - sg)

    cur = lambda cbk: pl.BlockSpec((tt, c), lambda i: (i, cbk))
    prev = lambda cbk: pl.BlockSpec((HALO, c), lambda i: (jnp.maximum(i * hb - 1, 0), cbk))
    nxt = pl.BlockSpec((HALO, c), lambda i: (jnp.minimum((i + 1) * hb, last_blk), 0))
    return pl.pallas_call(
        kern, name="conv_bwd_taps", grid=(n_rows // tt,),
        in_specs=[cur(0), cur(1), prev(0), prev(1), cur(0), nxt, pl.BlockSpec(cw.shape, lambda i: (0, 0))],
        out_specs=[cur(0), cur(0), pl.BlockSpec((HALO, c), lambda i: (0, 0))],
        out_shape=[jax.ShapeDtypeStruct((n_rows, c), F32), jax.ShapeDtypeStruct((n_rows, c), F32),
                   jax.ShapeDtypeStruct((HALO, c), F32)],
        scratch_shapes=[pltpu.VMEM((tt + HALO, c), F32)] * 2 + [pltpu.VMEM((SUBLANE, tt + HALO, c), F32)] * 2,
        compiler_params=_params("arbitrary"),
    )(proj, proj, proj, proj, dc, dc, cw)


def _s5_params_fwd(lr, li, ldt, btr, bti, seg):
    ns = lr.shape[1]

    def kern(lr_ref, li_ref, ldt_ref, btr_ref, bti_ref, ar_ref, ai_ref, bbr_ref, bbi_ref, ps_ref, psf_ref, pc_ref, pcf_ref):
        lrv, liv = lr_ref[...], li_ref[...]
        dt = jnp.exp(ldt_ref[...])
        zr, zi = lrv * dt, liv * dt
        mag = jnp.exp(zr)
        ar, ai = mag * jnp.cos(zi), mag * jnp.sin(zi)
        den = lrv * lrv + liv * liv
        nr = ar - 1.0
        cr = (nr * lrv + ai * liv) / den
        ci = (ai * lrv - nr * liv) / den
        ar_ref[...] = ar
        ai_ref[...] = ai
        bbr_ref[...] = cr * btr_ref[...] - ci * bti_ref[...]
        bbi_ref[...] = cr * bti_ref[...] + ci * btr_ref[...]
        def powers(br, bi, count, up_ref, down_ref):
            pr, pi = br, bi
            for e in range(count):
                for ref, at in ((up_ref, e), (down_ref, count - 1 - e)):
                    ref[pl.ds(at, 1), pl.ds(0, ns)] = pr
                    ref[pl.ds(at, 1), pl.ds(ns, ns)] = pi
                if e < count - 1:
                    pr, pi = pr * br - pi * bi, pr * bi + pi * br
            return pr, pi

        powers(*powers(ar, ai, seg, ps_ref, psf_ref), SUBLANE, pc_ref, pcf_ref)

    h = btr.shape[0]
    shapes = [jax.ShapeDtypeStruct((1, ns), F32)] * 2 + [jax.ShapeDtypeStruct((h, ns), F32)] * 2
    shapes += [jax.ShapeDtypeStruct((seg, 2 * ns), F32)] * 2 + [jax.ShapeDtypeStruct((SUBLANE, 2 * ns), F32)] * 2
    return pl.pallas_call(kern, name="s5_params_fwd", out_shape=shapes)(lr, li, ldt, btr, bti)


def _s5_params_bwd(lr, li, ldt, btr, bti, dar, dai, dbbr, dbbi):
    def kern(lr_ref, li_ref, ldt_ref, btr_ref, bti_ref, dar_ref, dai_ref, dbr_ref, dbi_ref,
             dlr_ref, dli_ref, dldt_ref, dbtr_ref, dbti_ref):
        lrv, liv = lr_ref[...], li_ref[...]
        dt = jnp.exp(ldt_ref[...])
        zr, zi = lrv * dt, liv * dt
        mag = jnp.exp(zr)
        ar, ai = mag * jnp.cos(zi), mag * jnp.sin(zi)
        den = lrv * lrv + liv * liv
        nr = ar - 1.0
        cr = (nr * lrv + ai * liv) / den
        ci = (ai * lrv - nr * liv) / den
        dbr, dbi, br, bi = dbr_ref[...], dbi_ref[...], btr_ref[...], bti_ref[...]
        dbtr_ref[...] = cr * dbr + ci * dbi
        dbti_ref[...] = cr * dbi - ci * dbr
        dcr = _colsum(br * dbr + bi * dbi)
        dci = _colsum(br * dbi - bi * dbr)
        ir, ii = lrv / den, -liv / den
        dnr = ir * dcr + ii * dci
        dni = ir * dci - ii * dcr
        wr, wi = cr * ir - ci * ii, cr * ii + ci * ir
        dl1r = -(wr * dcr + wi * dci)
        dl1i = -(wr * dci - wi * dcr)
        dtr, dti = dar_ref[...] + dnr, dai_ref[...] + dni
        dzr = ar * dtr + ai * dti
        dzi = ar * dti - ai * dtr
        dlr_ref[...] = dl1r + dt * dzr
        dli_ref[...] = dl1i + dt * dzi
        dldt_ref[...] = (dzr * lrv + dzi * liv) * dt

    ns, h = lr.shape[1], btr.shape[0]
    shapes = [jax.ShapeDtypeStruct((1, ns), F32)] * 3 + [jax.ShapeDtypeStruct((h, ns), F32)] * 2
    return pl.pallas_call(kern, name="s5_params_bwd", out_shape=shapes)(lr, li, ldt, btr, bti, dar, dai, dbbr, dbbi)


def _to_segments(nat_ref, seg_ref):
    steps = nat_ref.shape[0] // SUBLANE
    _regroup(nat_ref, seg_ref, lambda r: (r % SUBLANE) * steps + r // SUBLANE)


def _from_segments(seg_ref, nat_ref):
    steps = nat_ref.shape[0] // SUBLANE
    _regroup(seg_ref, nat_ref, lambda r: (r % steps) * SUBLANE + r // steps)


def _regroup(src_ref, dst_ref, src_row):
    rows, width = dst_ref.shape
    sublane = lax.broadcasted_iota(jnp.int32, (SUBLANE, width), 0)
    for r0 in range(0, rows, SUBLANE):
        tile = jnp.broadcast_to(src_ref[pl.ds(src_row(r0), 1), :], (SUBLANE, width))
        for k in range(1, SUBLANE):
            tile = jnp.where(sublane == k, src_ref[pl.ds(src_row(r0 + k), 1), :], tile)
        dst_ref[pl.ds(r0, SUBLANE), :] = tile


def _scan_tile(s_ref, o_ref, fix_ref, tabs, car_ref, sb, reverse, x_ref=None, acc_ref=None):
    l1, l2, l4, pw = tabs
    rows_t, w = s_ref.shape
    steps = rows_t // SUBLANE
    cw = _pick(sb, (SCAN_COLS,))
    last = 0 if reverse else SUBLANE - 1
    first = SUBLANE - 1 - last
    row = lax.broadcasted_iota(jnp.int32, (SUBLANE, cw), 0)
    step_rows = lambda i: pl.ds(pl.multiple_of(((steps - 1 - i) if reverse else i) * SUBLANE, SUBLANE), SUBLANE)
    zero = jnp.zeros((SUBLANE, cw), F32)

    for c0 in [b0 + o for b0 in range(0, w, 2 * sb) for o in range(0, sb, cw)]:
        cr, ci = pl.ds(c0, cw), pl.ds(c0 + sb, cw)
        base = pl.ds(((steps - 1) if reverse else 0) * SUBLANE, SUBLANE)
        ar, ai = fix_ref[base, cr], fix_ref[base, ci]

        def run(i, state):
            xr, xi = state
            rows = step_rows(i)
            xr, xi = ar * xr - ai * xi + s_ref[rows, cr], ar * xi + ai * xr + s_ref[rows, ci]
            o_ref[rows, cr] = xr
            o_ref[rows, ci] = xi
            return xr, xi

        fr, fi = lax.fori_loop(0, steps, run, (zero, zero))
        for s, lt in ((1, l1), (2, l2), (4, l4)):
            sh = (SUBLANE - s) if reverse else s
            sr, si = pltpu.roll(fr, sh, 0), pltpu.roll(fi, sh, 0)
            tr, ti = lt[:, cr], lt[:, ci]
            fr, fi = fr + tr * sr - ti * si, fi + tr * si + ti * sr
        kr, ki = car_ref[pl.ds(last, 1), cr], car_ref[pl.ds(last, 1), ci]
        pr, pi = pw[:, cr], pw[:, ci]
        fr, fi = fr + pr * kr - pi * ki, fi + pr * ki + pi * kr
        car_ref[:, cr] = fr
        car_ref[:, ci] = fi
        to_next = 1 if not reverse else SUBLANE - 1
        gr = jnp.where(row == first, kr, pltpu.roll(fr, to_next, 0))
        gi = jnp.where(row == first, ki, pltpu.roll(fi, to_next, 0))

        def fix(i, state):
            rows = step_rows(i)
            qr, qi = fix_ref[rows, cr], fix_ref[rows, ci]
            yr = o_ref[rows, cr] + qr * gr - qi * gi
            yi = o_ref[rows, ci] + qr * gi + qi * gr
            o_ref[rows, cr] = yr
            o_ref[rows, ci] = yi
            if acc_ref is None:
                return state
            nr, ni, sr, si = state
            pxr, pxi = x_ref[rows, cr], x_ref[rows, ci]
            return yr, yi, sr + nr * pxr + ni * pxi, si + ni * pxr - nr * pxi

        if acc_ref is None:
            lax.fori_loop(0, steps, fix, 0)
        else:
            _, _, sr, si = lax.fori_loop(0, steps, fix, (gr, gi, zero, zero))
            acc_ref[:, cr] += sr
            acc_ref[:, ci] += si


def _s5_fwd(proj, u_blk, bdc, cdc, fix, tabs, dskip, seq, sb):
    n_rows = proj.shape[0]
    nb, blk, w_blk = bdc.shape
    c, w = nb * blk, nb * w_blk
    tt = fix.shape[0]

    def kern(u_ref, bd_ref, cd_ref, fix_ref, l1, l2, l4, pw, d_ref, xs_ref, yp_ref, yg_ref, us_ref, bu_ref, car_ref):
        @pl.when((pl.program_id(0) * tt) % seq == 0)
        def _():
            car_ref[...] = jnp.zeros_like(car_ref)

        _to_segments(u_ref, us_ref)
        for j in range(nb):
            bu_ref[:, pl.ds(j * w_blk, w_blk)] = jnp.dot(us_ref[:, pl.ds(j * blk, blk)].astype(BF16), bd_ref[j],
                                                         preferred_element_type=F32)
        _scan_tile(bu_ref, xs_ref, fix_ref, (l1, l2, l4, pw), car_ref, sb, False)
        for j in range(nb):
            cols = pl.ds(j * blk, blk)
            y0 = jnp.dot(xs_ref[:, pl.ds(j * w_blk, w_blk)].astype(BF16), cd_ref[j], preferred_element_type=F32)
            us_ref[:, cols] = y0 + d_ref[:, cols] * us_ref[:, cols]
        _from_segments(us_ref, yp_ref)
        yg_ref[...] = _gelu(yp_ref[...]).astype(BF16)

    tab = pl.BlockSpec((SUBLANE, w), lambda i: (0, 0))
    rows = pl.BlockSpec((tt, c), lambda i: (i, 0))
    return pl.pallas_call(
        kern, name="s5_fwd", grid=(n_rows // tt,),
        in_specs=[pl.BlockSpec((tt, c), lambda i: (i, u_blk * blk // c)), pl.BlockSpec(bdc.shape, lambda i: (0, 0, 0)),
                  pl.BlockSpec(cdc.shape, lambda i: (0, 0, 0)), pl.BlockSpec((tt, w), lambda i: (0, 0)), tab, tab, tab, tab,
                  pl.BlockSpec((1, c), lambda i: (0, 0))],
        out_specs=[pl.BlockSpec((tt, w), lambda i: (i, 0)), rows, rows],
        out_shape=[jax.ShapeDtypeStruct((n_rows, w), F32), jax.ShapeDtypeStruct((n_rows, c), F32),
                   jax.ShapeDtypeStruct((n_rows, c), BF16)],
        scratch_shapes=[pltpu.VMEM((tt, c), F32), pltpu.VMEM((tt, w), F32), pltpu.VMEM((SUBLANE, w), F32)],
        compiler_params=_params("arbitrary"))(proj, bdc, cdc, fix, *tabs, dskip)


def _s5_bwd(dypre, du_skip, xs, proj, u_blk, bdc, cdc, fix, tabs, seq, sb):
    n_rows = proj.shape[0]
    nb, blk, w_blk = bdc.shape
    c, w = nb * blk, nb * w_blk
    tt = fix.shape[0]
    nt = n_rows // tt
    tn = (((0,), (0,)), ((), ()))

    def kern(dy_ref, ds_ref, x_ref, u_ref, bd_ref, cd_ref, fix_ref, l1, l2, l4, pw, du_ref, da_ref, db_ref, dc_ref,
             dys_ref, us_ref, dus_ref, gx_ref, lam_ref, car_ref, acc_ref):
        i = pl.program_id(0)

        @pl.when(((nt - i) * tt) % seq == 0)
        def _():
            car_ref[...] = jnp.zeros_like(car_ref)

        @pl.when(i == 0)
        def _():
            acc_ref[...] = jnp.zeros_like(acc_ref)
            db_ref[...] = jnp.zeros_like(db_ref)
            dc_ref[...] = jnp.zeros_like(dc_ref)

        _to_segments(dy_ref, dys_ref)
        _to_segments(u_ref, us_ref)
        for j in range(nb):
            gx_ref[:, pl.ds(j * w_blk, w_blk)] = lax.dot_general(dys_ref[:, pl.ds(j * blk, blk)].astype(BF16), cd_ref[j], NT,
                                                                 preferred_element_type=F32)
        _scan_tile(gx_ref, lam_ref, fix_ref, (l1, l2, l4, pw), car_ref, sb, True, x_ref, acc_ref)
        for j in range(nb):
            cols, wide = pl.ds(j * blk, blk), pl.ds(j * w_blk, w_blk)
            lam = lam_ref[:, wide].astype(BF16)
            dus_ref[:, cols] = lax.dot_general(lam, bd_ref[j], NT, preferred_element_type=F32)
            db_ref[j] += lax.dot_general(us_ref[:, cols].astype(BF16), lam, tn, preferred_element_type=F32)
            dc_ref[j] += lax.dot_general(x_ref[:, wide].astype(BF16), dys_ref[:, cols].astype(BF16), tn,
                                         preferred_element_type=F32)
        _from_segments(dus_ref, du_ref)
        du_ref[...] += ds_ref[...]

        @pl.when(i == nt - 1)
        def _():
            da_ref[...] = _colsum(acc_ref[...])

    back = lambda i: (nt - 1 - i, 0)
    tab = pl.BlockSpec((SUBLANE, w), lambda i: (0, 0))
    rows = pl.BlockSpec((tt, c), back)
    whole = lambda a: pl.BlockSpec(a.shape, lambda i: (0, 0, 0))
    return pl.pallas_call(
        kern, name="s5_bwd", grid=(nt,),
        in_specs=[rows, rows, pl.BlockSpec((tt, w), back), pl.BlockSpec((tt, c), lambda i: (nt - 1 - i, u_blk * blk // c)),
                  whole(bdc), whole(cdc), pl.BlockSpec((tt, w), lambda i: (0, 0)), tab, tab, tab, tab],
        out_specs=[rows, pl.BlockSpec((1, w), lambda i: (0, 0)), whole(bdc), whole(cdc)],
        out_shape=[jax.ShapeDtypeStruct((n_rows, c), F32), jax.ShapeDtypeStruct((1, w), F32),
                   jax.ShapeDtypeStruct(bdc.shape, F32), jax.ShapeDtypeStruct(cdc.shape, F32)],
        scratch_shapes=[pltpu.VMEM((tt, c), F32)] * 3 + [pltpu.VMEM((tt, w), F32)] * 2 + [pltpu.VMEM((SUBLANE, w), F32)] * 2,
        compiler_params=_params("arbitrary"))(dypre, du_skip, xs, proj, bdc, cdc, fix, *tabs)


def _s5_post2(yg, q0, bg, og):
    c = yg.shape[1]

    def body(ins, outs, accs):
        ygv = ins[0][...].astype(F32)
        sg = ygv * _sigmoid(ins[1][...] + ins[2][...])
        outs[0][...] = (sg * _rms_r(sg) * ins[3][...]).astype(BF16)

    return _rowwise("s5_post2", body, yg.shape[0], [(yg, c, 0), (q0, c, 0)], [bg, og], [(c, BF16)], [])[0]


def _s5_post2_bwd(dmixed, yg, q0, bg, og):
    c = yg.shape[1]

    def body(ins, outs, accs):
        dsn, ygv = ins[0][...], ins[1][...].astype(F32)
        s = _sigmoid(ins[2][...] + ins[3][...])
        sg = ygv * s
        r = _rms_r(sg)
        accs[0][...] += _colsum(dsn * sg * r)
        dsg = _rms_bwd(sg, r, ins[4][...], dsn)
        dq = dsg * ygv * s * (1.0 - s)
        outs[0][...] = dq.astype(BF16)
        outs[1][...] = dsg * s
        accs[1][...] += _colsum(dq)

    return _rowwise("s5_post2_bwd", body, yg.shape[0], [(dmixed, c, 1), (yg, c, 0), (q0, c, 0)], [bg, og],
                    [(c, BF16), (c, F32)], [(1, c)] * 2)


def _s5_post1_bwd(dyg1, dyg2, ypre, proj, dskip, after=()):
    c = ypre.shape[1]

    def body(ins, outs, accs):
        dyp = (ins[0][...] + ins[1][...]) * _dgelu(ins[2][...])
        outs[0][...] = dyp
        outs[1][...] = dyp * ins[4][...]
        accs[0][...] += _colsum(dyp * ins[3][...])

    return _rowwise("s5_post1_bwd", body, ypre.shape[0], [(dyg1, c, 0), (dyg2, c, 0), (ypre, c, 0), (proj, c, 2)], [dskip],
                    [(c, F32), (c, F32)], [(1, c)], after=after)


def _place():
    return lax.axis_index("x"), lax.axis_index("y"), lax.axis_index("c")


def _window(ref, axis, q, rows, cols):
    if axis == 0:
        return ref.at[pl.ds(pl.multiple_of(q * rows, SUBLANE), rows), :]
    return ref.at[:, pl.ds(pl.multiple_of(q * cols, LANE), cols)]


ALL_RELS = [(fx, fy, fc) for fx in (0, 1) for fy in (0, 1) for fc in (0, 1)][1:]
N_PEERS = {"gather": 3, "scatter": 3, "sibling": 1, "all": len(ALL_RELS)}


def _copies(kind, srcs, lands, shards, axes, send_sems, recv_sems, local_sems):
    x, y, c = _place()
    me, dev = 2 * x + y, 4 * x + 2 * y + c
    n_peers = N_PEERS[kind]
    starts, waits = [], []
    for a, (src, land) in enumerate(zip(srcs, lands)):
        on = lambda k, peer: dict(send_sem=send_sems.at[n_peers * a + k], recv_sem=recv_sems.at[n_peers * a + k],
                                  device_id=peer, device_id_type=MESH)
        if kind == "sibling":
            cp = pltpu.make_async_remote_copy(src_ref=src, dst_ref=land, **on(0, (x, y, 1 - c)))
            starts.append(cp)
            waits.append(cp)
            continue
        if kind == "all":
            own = pltpu.make_async_copy(src, land.at[dev], local_sems.at[a])
            starts.append(own)
            waits.append(own)
            for k, (fx, fy, fc) in enumerate(ALL_RELS):
                px, py, pc = (1 - x) if fx else x, (1 - y) if fy else y, (1 - c) if fc else c
                starts.append(pltpu.make_async_remote_copy(src_ref=src, dst_ref=land.at[dev], **on(k, (px, py, pc))))
                waits.append(pltpu.make_async_remote_copy(src_ref=src, dst_ref=land.at[4 * px + 2 * py + pc],
                                                          **on(k, (px, py, pc))))
            continue
        rows, cols = shards[a]
        if kind == "gather":
            own = pltpu.make_async_copy(src, _window(land, axes[a], me, rows, cols), local_sems.at[a])
        else:
            own = pltpu.make_async_copy(_window(src, axes[a], me, rows, cols), land.at[3], local_sems.at[a])
        starts.append(own)
        waits.append(own)
        for j, (fx, fy) in enumerate(CHIP_RELS):
            px, py = (1 - x) if fx else x, (1 - y) if fy else y
            peer = 2 * px + py
            if kind == "gather":
                starts.append(pltpu.make_async_remote_copy(src_ref=src, dst_ref=_window(land, axes[a], me, rows, cols),
                                                           **on(j, (px, py, c))))
                waits.append(pltpu.make_async_remote_copy(src_ref=src, dst_ref=_window(land, axes[a], peer, rows, cols),
                                                          **on(j, (px, py, c))))
            else:
                cp = pltpu.make_async_remote_copy(src_ref=_window(src, axes[a], peer, rows, cols), dst_ref=land.at[j],
                                                  **on(j, (px, py, c)))
                starts.append(cp)
                waits.append(cp)
    return starts, waits


HBM = pl.BlockSpec(memory_space=pltpu.HBM)
SEM = pl.BlockSpec(memory_space=pltpu.SEMAPHORE)


def _shard_shapes(kind, arrs, axes):
    if kind != "scatter":
        return [a.shape for a in arrs]
    return [(a.shape[0] // N_CHIPS, a.shape[1]) if ax == 0 else (a.shape[0], a.shape[1] // N_CHIPS) for a, ax in zip(arrs, axes)]


def _land_shapes(kind, arrs, axes):
    if kind == "gather":
        return [(N_CHIPS * a.shape[0], a.shape[1]) if ax == 0 else (a.shape[0], N_CHIPS * a.shape[1]) for a, ax in zip(arrs, axes)]
    if kind == "scatter":
        return [(N_CHIPS,) + s for s in _shard_shapes(kind, arrs, axes)]
    return [a.shape if kind == "sibling" else (len(ALL_RELS) + 1,) + a.shape for a in arrs]


def _exchange_start(name, kind, arrs, axes, after=()):
    n, n_after = len(arrs), len(after)
    shards = _shard_shapes(kind, arrs, axes)
    land_shapes = _land_shapes(kind, arrs, axes)
    lands = [lax.empty(s, a.dtype) for s, a in zip(land_shapes, arrs)]

    def kern(*refs):
        outs = refs[2 * n + n_after:]
        starts, _ = _copies(kind, refs[:n], refs[n:2 * n], shards, axes, outs[0], outs[1], outs[2])
        for cp in starts:
            cp.start()
        outs[-1][...] = jnp.zeros_like(outs[-1])

    kept = [pltpu.HBM(a.shape, a.dtype) for a in arrs] + [pltpu.HBM(s, a.dtype) for s, a in zip(land_shapes, arrs)]
    n_sems = N_PEERS[kind] * n
    res = pl.pallas_call(
        kern, name=name, in_specs=[HBM] * (2 * n) + [ANY] * n_after,
        out_specs=[SEM] * 3 + [HBM] * (2 * n) + [pl.BlockSpec(memory_space=pltpu.VMEM)],
        out_shape=[pltpu.SemaphoreType.DMA((n_sems,)), pltpu.SemaphoreType.DMA((n_sems,)), pltpu.SemaphoreType.DMA((n,))]
        + kept + [jax.ShapeDtypeStruct((SUBLANE, LANE), F32)],
        input_output_aliases={i: 3 + i for i in range(2 * n)},
        compiler_params=pltpu.CompilerParams(has_side_effects=pltpu.SideEffectType.DATAFLOW_SIDE_EFFECTING),
    )(*[pltpu.with_memory_space_constraint(a, pltpu.HBM) for a in list(arrs) + lands], *after)
    return res[:3], res[3:3 + n], res[3 + n:3 + 2 * n], res[-1]


def _exchange_wait(name, kind, started, axes, after, sources_too=False):
    sems, srcs, lands, _ = started
    n, n_after = len(srcs), len(after)
    shards = _shard_shapes(kind, srcs, axes)

    def kern(*refs):
        sem_refs = refs[2 * n:2 * n + 3]
        _, waits = _copies(kind, refs[:n], refs[n:2 * n], shards, axes, *sem_refs)
        for cp in waits:
            cp.wait()

    res = pl.pallas_call(
        kern, name=name, in_specs=[HBM] * (2 * n) + [SEM] * 3 + [ANY] * n_after, out_specs=[HBM] * (2 * n),
        out_shape=[pltpu.HBM(a.shape, a.dtype) for a in list(srcs) + list(lands)],
        input_output_aliases={i: i for i in range(2 * n)},
        compiler_params=pltpu.CompilerParams(has_side_effects=pltpu.SideEffectType.DATAFLOW_SIDE_EFFECTING),
    )(*srcs, *lands, *sems, *after)
    return (res[:n], res[n:]) if sources_too else res[n:]


def _sum_devices(parts):
    def kern(p_ref, o_ref):
        acc = p_ref[0]
        for d in range(1, parts.shape[0]):
            acc = acc + p_ref[d]
        o_ref[...] = acc

    return pl.pallas_call(kern, name="sum_devices", out_shape=jax.ShapeDtypeStruct(parts.shape[1:], F32),
                          compiler_params=pltpu.CompilerParams(vmem_limit_bytes=VMEM_LIMIT_BYTES))(parts)


def _sum_slots(name, parts):
    _, rows, cols = parts.shape
    tr = _pick(rows, (ROW_TILE, 128, 64, 32))

    def kern(p_ref, o_ref):
        o_ref[...] = ((p_ref[3].astype(F32) + p_ref[0].astype(F32)) + p_ref[1].astype(F32)) + p_ref[2].astype(F32)

    return pl.pallas_call(kern, name=name, grid=(rows // tr,),
                          in_specs=[pl.BlockSpec((N_CHIPS, tr, cols), lambda i: (0, i, 0))],
                          out_specs=pl.BlockSpec((tr, cols), lambda i: (i, 0)),
                          out_shape=jax.ShapeDtypeStruct((rows, cols), F32), compiler_params=_params("arbitrary"))(parts)


def _adamw_math(g, w, m, v):
    m2 = ADAM_B1 * m + (1.0 - ADAM_B1) * g
    v2 = ADAM_B2 * v + (1.0 - ADAM_B2) * (g * g)
    m_hat = m2 / (1.0 - ADAM_B1 ** ADAM_STEP)
    v_hat = v2 / (1.0 - ADAM_B2 ** ADAM_STEP)
    return -ADAM_LR * (m_hat / (jnp.sqrt(v_hat) + ADAM_EPS) + ADAM_WD * w), m2, v2


def _adamw(name, parts, w, m, v):
    rows, cols = w.shape
    tr = rows if rows * cols <= WHOLE_ELEMS else _pick(rows, (ROW_TILE, 352, 128, 64, 32, 8))
    n = len(parts)

    def kern(*refs):
        g = refs[0][:, pl.ds(0, cols)]
        for p in refs[1:n]:
            g = g + p[:, pl.ds(0, cols)]
        d, m2, v2 = _adamw_math(g, refs[n][...], refs[n + 1][...], refs[n + 2][...])
        refs[n + 3][...] = g
        refs[n + 4][...] = d
        refs[n + 5][...] = m2
        refs[n + 6][...] = v2

    spec = pl.BlockSpec((tr, cols), lambda i: (i, 0))
    return pl.pallas_call(kern, name=name, grid=(rows // tr,),
                          in_specs=[pl.BlockSpec((tr, p.shape[1]), lambda i: (i, 0)) for p in parts] + [spec] * 3,
                          out_specs=[spec] * 4, out_shape=[jax.ShapeDtypeStruct((rows, cols), F32)] * 4,
                          compiler_params=_params("arbitrary"))(*parts, w, m, v)


def _adamw_many(name, gs, ws, ms, vs):
    n = len(gs)

    def kern(*refs):
        for p in range(n):
            d, m2, v2 = _adamw_math(refs[p][...], refs[n + p][...], refs[2 * n + p][...], refs[3 * n + p][...])
            refs[4 * n + p][...] = d
            refs[5 * n + p][...] = m2
            refs[6 * n + p][...] = v2

    res = pl.pallas_call(kern, name=name, out_shape=[jax.ShapeDtypeStruct(w.shape, F32) for w in ws] * 3,
                         compiler_params=pltpu.CompilerParams(vmem_limit_bytes=VMEM_LIMIT_BYTES))(*gs, *ws, *ms, *vs)
    return res[:n], res[n:2 * n], res[2 * n:]


def _pack(arrs):
    parts, rows = [], []
    for a in arrs:
        r = _round_up(-(-a.size // LANE), SUBLANE)
        parts.append(jnp.pad(a.reshape(-1).astype(F32), (0, r * LANE - a.size)).reshape(r, LANE))
        rows.append(r)
    return jnp.concatenate(parts, axis=0), rows


def _unpack(buf, rows, shapes):
    out, r0 = [], 0
    for r, s in zip(rows, shapes):
        size = math.prod(s)
        out.append(buf[r0:r0 + r].reshape(-1)[:size].reshape(s))
        r0 += r
    return out


def kernel(x, norm_ffn1, ffn1_w1, ffn1_w3, ffn1_w2, norm_mix, w_in, conv_w, conv_b, conv_ln_g, conv_ln_b, conv_out_g, ssm_A_re, ssm_A_im, ssm_log_dt, ssm_B_re, ssm_B_im, ssm_C_re, ssm_C_im, ssm_D, ssm_glu_w, ssm_glu_b, ssm_out_g, w_out, norm_ffn2, ffn2_w1, ffn2_w3, ffn2_w2, norm_final, loss_target, m_norm_ffn1, m_ffn1_w1, m_ffn1_w3, m_ffn1_w2, m_norm_mix, m_w_in, m_conv_w, m_conv_b, m_conv_ln_g, m_conv_ln_b, m_conv_out_g, m_ssm_A_re, m_ssm_A_im, m_ssm_log_dt, m_ssm_B_re, m_ssm_B_im, m_ssm_C_re, m_ssm_C_im, m_ssm_D, m_ssm_glu_w, m_ssm_glu_b, m_ssm_out_g, m_w_out, m_norm_ffn2, m_ffn2_w1, m_ffn2_w3, m_ffn2_w2, m_norm_final, v_norm_ffn1, v_ffn1_w1, v_ffn1_w3, v_ffn1_w2, v_norm_mix, v_w_in, v_conv_w, v_conv_b, v_conv_ln_g, v_conv_ln_b, v_conv_out_g, v_ssm_A_re, v_ssm_A_im, v_ssm_log_dt, v_ssm_B_re, v_ssm_B_im, v_ssm_C_re, v_ssm_C_im, v_ssm_D, v_ssm_glu_w, v_ssm_glu_b, v_ssm_out_g, v_w_out, v_norm_ffn2, v_ffn2_w1, v_ffn2_w3, v_ffn2_w2, v_norm_final):
    given = dict(locals())
    wts = {n: given[n] for n in WEIGHTS}
    n_seq, seq, d = x.shape
    n_rows = n_seq * seq
    xf = x.reshape(n_rows, d)
    tgt = loss_target.reshape(n_rows, d)
    row = lambda a: a.reshape(1, -1)

    f = ffn1_w1.shape[-1]
    fp = _round_up(f, LANE)
    held = lambda n, a: a[0].T if n in TRANSPOSED else a[0]
    shards = []
    for n in BIG:
        s = held(n, wts[n]).astype(BF16)
        if n.startswith('ffn'):
            s = jnp.pad(s, ((0, fp - f), (0, 0)))
        shards.append(s)
    n_taps, c_shard = conv_w.shape[1], conv_w.shape[2]
    shards.append(jnp.pad(conv_w[0], ((0, HALO - n_taps), (0, 0))))
    shard_of = dict(zip(BIG + ['conv_w'], shards))
    axis_of = dict(BIG_AXIS, conv_w=1)
    groups = [['ffn1_w1'], ['ffn1_w3'], ['ffn1_w2', 'w_in', 'conv_w', 'ssm_glu_w', 'w_out'], ['ffn2_w1', 'ffn2_w3', 'ffn2_w2']]
    fetch, tok = [], []
    for k, names in enumerate(groups):
        fetch.append(_exchange_start("gather%d_send" % k, "gather", [shard_of[n] for n in names],
                                     [axis_of[n] for n in names], tok))
        tok = [fetch[-1][3]]
    full = {}

    def arrive(k, after):
        lands = _exchange_wait("gather%d_recv" % k, "gather", fetch[k], [axis_of[n] for n in groups[k]], after)
        full.update(zip(groups[k], lands))

    h1, h1_t = _rms_fwd("ffn1_rms", xf, norm_ffn1)
    arrive(0, tok + [h1])

    _, n_grp, n_state = ssm_A_re.shape
    grp = ssm_B_re.shape[-1]
    ns = n_grp * n_state
    c_ssm = n_grp * grp
    lr, li = ssm_A_re.reshape(1, ns), ssm_A_im.reshape(1, ns)
    ldt = jnp.repeat(ssm_log_dt.reshape(n_grp), n_state).reshape(1, ns)
    btr = ssm_B_re[0].transpose(2, 0, 1).reshape(grp, ns)
    bti = ssm_B_im[0].transpose(2, 0, 1).reshape(grp, ns)
    ctr = ssm_C_re[0].transpose(1, 0, 2).reshape(grp, ns)
    cti = ssm_C_im[0].transpose(1, 0, 2).reshape(grp, ns)
    scan_tile = _pick(seq, (SCAN_TILE,))
    _, _, bbr, bbi, seg_up, seg_down, pw, pw_falling = _s5_params_fwd(lr, li, ldt, btr, bti, scan_tile // SUBLANE)
    nb = c_ssm // LANE
    sb, gpb = ns // nb, n_grp // nb
    diag = (jnp.arange(LANE)[:, None] // grp) == (jnp.arange(sb)[None, :] // n_state)

    def spread(t):
        return jnp.where(diag, jnp.tile(t.reshape(grp, nb, sb).transpose(1, 0, 2), (1, gpb, 1)), 0.0)

    def gather_diag(t):
        return (t * diag).reshape(nb, gpb, grp, sb).sum(1).transpose(1, 0, 2).reshape(grp, ns)

    def interleave(re, im):
        return jnp.stack([re.reshape(-1, nb, sb), im.reshape(-1, nb, sb)], axis=2).reshape(-1, 2 * ns)

    bdc = jnp.concatenate([spread(bbr), spread(bbi)], axis=2).astype(BF16)
    cdc = jnp.concatenate([spread(ctr).transpose(0, 2, 1), -spread(cti).transpose(0, 2, 1)], axis=1).astype(BF16)
    rowi = jnp.arange(SUBLANE)[:, None]
    pwf, pwc = interleave(pw[:, :ns], pw[:, ns:]), interleave(pw[:, :ns], -pw[:, ns:])
    tabs_f = [jnp.where(rowi >= s, pwf[s - 1][None, :], 0.0) for s in (1, 2, 4)] + [pwf]
    tabs_b = [jnp.where(rowi <= SUBLANE - 1 - s, pwc[s - 1][None, :], 0.0) for s in (1, 2, 4)]
    tabs_b.append(interleave(pw_falling[:, :ns], -pw_falling[:, ns:]))
    fix_f = jnp.repeat(interleave(seg_up[:, :ns], seg_up[:, ns:]), SUBLANE, axis=0)
    fix_b = jnp.repeat(interleave(seg_down[:, :ns], -seg_down[:, ns:]), SUBLANE, axis=0)
    c_conv = conv_b.shape[1]
    u_blk = 2 * c_conv // LANE

    a1 = _mm("ffn1_up", h1, full['ffn1_w1'], 1, 1, BF16)
    arrive(1, [a1])
    b1, z1 = _ffn_up_gate("ffn1_gate", h1, full['ffn1_w3'], a1)
    arrive(2, [z1])
    x1, h2, h2_t = _mm("ffn1_down", z1, full['ffn1_w2'], 1, 0, addend=xf, alpha=0.5, post=_post_rms(norm_mix))
    saved1 = (h1_t, a1, b1, z1)
    cw = full['conv_w']
    proj = _mm("mix_in", h2, full['w_in'], 1, 0, F32)
    assert c_conv == c_ssm and proj.shape[1] == 3 * c_conv
    cpre, an = _conv_fwd(proj, cw, conv_b, conv_ln_g, conv_ln_b, conv_out_g, seq)
    xs, ypre, yg = _s5_fwd(proj, u_blk, bdc, cdc, fix_f, tabs_f, ssm_D, seq, sb)
    q0 = _mm("s5_gate", yg, full['ssm_glu_w'], 1, 0, F32)
    sn = _s5_post2(yg, q0, ssm_glu_b, ssm_out_g)
    wo = full['w_out']
    mixed = jnp.concatenate([an, sn], axis=1)
    x2, h3, h3_t = _mm("mix_out", mixed, wo, 1, 0, addend=x1, post=_post_rms(norm_ffn2))
    arrive(3, [x2])
    a3, b3, z3 = _ffn_up("ffn2_up", h3, full['ffn2_w1'], full['ffn2_w3'])
    saved2 = (h3_t, a3, b3, z3)
    dx3, dx3_t, loss_row, d_norm_final = _mm("ffn2_down", z3, full['ffn2_w2'], 1, 0, addend=x2, alpha=0.5,
                                             post=_post_loss(row(norm_final), tgt))

    g = {}
    dx2, g['norm_ffn2'], sent = _ffn_bwd("ffn2", x2, norm_ffn2, full['ffn2_w1'], full['ffn2_w3'], full['ffn2_w2'], saved2,
                                         dx3, dx3_t, early=False)
    dmixed = _mm("mix_dmixed", dx2, wo, 1, 1, F32)
    dwo = _mm("mix_dwo", mixed, dx2, 0, 0, BF16)
    dq, dyg1, g['ssm_out_g'], g['ssm_glu_b'] = _s5_post2_bwd(dmixed, yg, q0, ssm_glu_b, ssm_out_g)
    dyg2 = _mm("s5_dgate", dq, full['ssm_glu_w'], 1, 1, F32)
    dwg = _mm("s5_dwg", yg, dq, 0, 0, BF16)
    dypre, du_skip, g['ssm_D'] = _s5_post1_bwd(dyg1, dyg2, ypre, proj, ssm_D)
    du, dabar, dbdc, dcdc = _s5_bwd(dypre, du_skip, xs, proj, u_blk, bdc, cdc, fix_b, tabs_b, seq, sb)
    dabar = dabar.reshape(nb, 2, sb)
    dlr, dli, dldt, dbtr, dbti = _s5_params_bwd(lr, li, ldt, btr, bti, dabar[:, 0].reshape(1, ns), dabar[:, 1].reshape(1, ns),
                                                gather_diag(dbdc[:, :, :sb]), gather_diag(dbdc[:, :, sb:]))
    g['ssm_A_re'], g['ssm_A_im'] = dlr, dli
    g['ssm_log_dt'] = dldt.reshape(n_grp, n_state).sum(axis=1)
    g['ssm_B_re'] = dbtr.reshape(grp, n_grp, n_state).transpose(1, 2, 0)
    g['ssm_B_im'] = dbti.reshape(grp, n_grp, n_state).transpose(1, 2, 0)
    g['ssm_C_re'] = gather_diag(dcdc[:, :sb].transpose(0, 2, 1)).reshape(grp, n_grp, n_state).transpose(1, 0, 2)
    g['ssm_C_im'] = -gather_diag(dcdc[:, sb:].transpose(0, 2, 1)).reshape(grp, n_grp, n_state).transpose(1, 0, 2)
    dc, g['conv_out_g'], g['conv_ln_g'], g['conv_ln_b'], g['conv_b'] = _conv_bwd_rows(dmixed, cpre, conv_ln_g, conv_ln_b,
                                                                                    conv_out_g)
    dval, dgate, dcw = _conv_bwd_taps(proj, dc, cw, seq)
    dproj = jnp.concatenate([dval, dgate, du], axis=1)
    dwin = _mm("mix_dwin", h2_t, dproj, 1, 0, BF16)
    sent['w_out ssm_glu_w w_in'] = (_exchange_start("mix_send", "scatter", [dwo, dwg, dwin], [0, 0, 1]), [0, 0, 1])
    dx1, dx1_t, g['norm_mix'] = _mm("mix_dh", dproj, full['w_in'], 1, 1, after=[sent['w_out ssm_glu_w w_in'][0][3]],
                                    post=_post_rms_bwd(x1, norm_mix, dx2))
    dx0, g['norm_ffn1'], sent1 = _ffn_bwd("ffn1", xf, norm_ffn1, full['ffn1_w1'], full['ffn1_w3'], full['ffn1_w2'], saved1,
                                          dx1, dx1_t, early=True)
    sent.update(sent1)
    g['norm_final'] = d_norm_final
    g['conv_w'] = dcw[:n_taps]

    small_shapes = [(n_taps, c_conv) if n == 'conv_w' else wts[n].shape for n in SMALL]
    buf, buf_rows = _pack([g[n] for n in SMALL] + [loss_row])
    to_all = _exchange_start("small_send", "all", [buf], [0])
    slots = {}
    for names, (started, axes) in sent.items():
        lands = _exchange_wait(names.replace(' ', '_') + "_recv", "scatter", started, axes, after=[dx0, to_all[3]])
        slots.update(zip(names.split(), lands))
    sums = [_sum_slots("sum_" + n, slots[n]) for n in BIG]
    to_sibling = _exchange_start("sums_send", "sibling", sums, [0] * len(sums))
    from_all = _exchange_wait("small_recv", "all", to_all, [0], after=[to_sibling[3]])[0]
    total = _unpack(_sum_devices(from_all), buf_rows, small_shapes + [(1, LANE)])
    loss = total[-1][0, 0]
    grads = dict(zip(SMALL, total[:-1]))
    chip = 2 * lax.axis_index("x") + lax.axis_index("y")
    grads['conv_w'] = lax.dynamic_slice_in_dim(grads['conv_w'], chip * c_shard, c_shard, axis=1)[None]
    flat = lambda a: a.reshape(-1, a.shape[-1])
    small = _adamw_many("adamw_small", *[[flat(src[p + n]) for n in SMALL]
                                         for src, p in ((grads, ''), (given, ''), (given, 'm_'), (given, 'v_'))])
    deltas, new_m, new_v = ({n: o.reshape(wts[n].shape) for n, o in zip(SMALL, outs)} for outs in small)

    sums, theirs = _exchange_wait("sums_recv", "sibling", to_sibling, [0] * len(sums), after=[new_v[SMALL[-1]]],
                                  sources_too=True)
    for n, mine, other in zip(BIG, sums, theirs):
        grads[n], deltas[n], new_m[n], new_v[n] = (
            (o.T if n in TRANSPOSED else o)[None]
            for o in _adamw("adamw_" + n, [mine, other], held(n, given[n]), held(n, given['m_' + n]), held(n, given['v_' + n])))

    return (loss, dx0.reshape(x.shape), *[grads[n] for n in WEIGHTS], *[deltas[n] for n in WEIGHTS],
            *[new_m[n] for n in WEIGHTS], *[new_v[n] for n in WEIGHTS])
```

```python
import math
from typing import Callable, NamedTuple

import jax
import jax.numpy as jnp
from jax import lax
from jax.experimental import pallas as pl
from jax.experimental.pallas import tpu as pltpu

F32 = jnp.float32
BF16 = jnp.bfloat16
EPS = 1e-6
ADAM_LR, ADAM_B1, ADAM_B2, ADAM_EPS, ADAM_WD, ADAM_STEP = 0.001, 0.9, 0.999, 1e-08, 0.01, 10
MESH = pl.DeviceIdType.MESH
ANY = pl.BlockSpec(memory_space=pl.ANY)
LANE = 128
SUBLANE = 8
VMEM_LIMIT_BYTES = 56 << 20
ROW_TILE = 256
ROW_TILE_ELEMS = 256 * 1024
WHOLE_ELEMS = 512 * 1024
WHOLE_WEIGHT_BYTES = 8 << 20
FFN_ROWS = 256
CONV_TILE = 128
CONV_SUB = 32
HALO = 32
SCAN_TILE = 256
SCAN_COLS = 512
N_CHIPS = 4
CHIP_RELS = ((1, 0), (0, 1), (1, 1))
NT = (((1,), (1,)), ((), ()))
GELU_K = math.sqrt(2.0 / math.pi)
GELU_C = 0.044715

WEIGHTS = ['norm_ffn1', 'ffn1_w1', 'ffn1_w3', 'ffn1_w2', 'norm_mix', 'w_in', 'conv_w', 'conv_b', 'conv_ln_g', 'conv_ln_b',
           'conv_out_g', 'ssm_A_re', 'ssm_A_im', 'ssm_log_dt', 'ssm_B_re', 'ssm_B_im', 'ssm_C_re', 'ssm_C_im', 'ssm_D',
           'ssm_glu_w', 'ssm_glu_b', 'ssm_out_g', 'w_out', 'norm_ffn2', 'ffn2_w1', 'ffn2_w3', 'ffn2_w2', 'norm_final']
BIG = ['ffn1_w1', 'ffn1_w3', 'ffn1_w2', 'w_in', 'ssm_glu_w', 'w_out', 'ffn2_w1', 'ffn2_w3', 'ffn2_w2']
BIG_AXIS = {'ffn1_w1': 0, 'ffn1_w3': 0, 'ffn1_w2': 0, 'w_in': 1, 'ssm_glu_w': 0, 'w_out': 0, 'ffn2_w1': 0, 'ffn2_w3': 0,
            'ffn2_w2': 0}
TRANSPOSED = ('ffn1_w1', 'ffn1_w3', 'ffn2_w1', 'ffn2_w3')
SMALL = [n for n in WEIGHTS if n not in BIG]


def _round_up(n, m):
    return -(-n // m) * m


def _pick(n, cands):
    for c in cands:
        if c <= n and n % c == 0:
            return c
    return n


def _params(*sem):
    return pltpu.CompilerParams(dimension_semantics=sem, vmem_limit_bytes=VMEM_LIMIT_BYTES)


def _rms_r(x):
    return lax.rsqrt(jnp.mean(x * x, axis=-1, keepdims=True) + EPS)


def _rms_bwd(x, r, g, dy):
    dyg = dy * g
    return r * dyg - x * (r * r * r) * jnp.mean(x * dyg, axis=-1, keepdims=True)


def _sigmoid(x):
    return jax.nn.sigmoid(x)


def _dsilu(a, s):
    return s * (1.0 + a * (1.0 - s))


def _gelu(x):
    return 0.5 * x * (1.0 + jnp.tanh(GELU_K * (x + GELU_C * x * x * x)))


def _dgelu(x):
    t = jnp.tanh(GELU_K * (x + GELU_C * x * x * x))
    return 0.5 * (1.0 + t) + 0.5 * x * (1.0 - t * t) * GELU_K * (1.0 + 3.0 * GELU_C * x * x)


def _colsum(v):
    return jnp.sum(v, axis=0, keepdims=True)


def _rowwise(name, body, n_rows, row_ins, par_ins, row_outs, acc_outs, after=()):
    widest = max([w for (_, w, _) in row_ins] + [w for (w, _) in row_outs])
    tt = _pick(n_rows, [t for t in (256, 128, 64, 32, 16, 8) if t * widest <= ROW_TILE_ELEMS])
    in_specs = [pl.BlockSpec((tt, w), lambda i, cb=cb: (i, cb)) for (_, w, cb) in row_ins]
    in_specs += [pl.BlockSpec(p.shape, lambda i: (0, 0)) for p in par_ins] + [ANY] * len(after)
    out_specs = [pl.BlockSpec((tt, w), lambda i: (i, 0)) for (w, _) in row_outs]
    out_specs += [pl.BlockSpec((r, w), lambda i: (0, 0)) for (r, w) in acc_outs]
    out_shape = [jax.ShapeDtypeStruct((n_rows, w), dt) for (w, dt) in row_outs]
    out_shape += [jax.ShapeDtypeStruct((r, w), F32) for (r, w) in acc_outs]
    n_in, n_ro = len(row_ins) + len(par_ins), len(row_outs)
    o0 = n_in + len(after)

    def kern(*refs):
        accs = refs[o0 + n_ro:]
        if accs:
            @pl.when(pl.program_id(0) == 0)
            def _():
                for a in accs:
                    a[...] = jnp.zeros_like(a)
        body(refs[:n_in], refs[o0:o0 + n_ro], accs)

    return pl.pallas_call(kern, name=name, grid=(n_rows // tt,), in_specs=in_specs, out_specs=out_specs, out_shape=out_shape,
                          compiler_params=_params("arbitrary"))(*[a for a, _, _ in row_ins], *par_ins, *after)


class Post(NamedTuple):
    rows: list
    gains: list
    outs: list
    t_outs: list
    sums: list
    fn: Callable


def _post_rms(gain):
    def fn(r, rows, gains):
        h = r * _rms_r(r) * gains[0]
        return [r, h, h], []

    return Post([], [gain], [F32, BF16], [BF16], [], fn)


def _post_rms_bwd(x, gain, dres):
    def fn(dh, rows, gains):
        r = _rms_r(rows[0])
        dx = rows[1] + _rms_bwd(rows[0], r, gains[0], dh)
        return [dx, dx], [_colsum(dh * rows[0] * r)]

    return Post([x, dres], [gain], [F32], [BF16], [x.shape[1]], fn)


def _post_loss(gain, tgt):
    d = tgt.shape[1]

    def fn(xv, rows, gains):
        r = _rms_r(xv)
        e = xv * r * gains[0] - rows[0]
        sq = jnp.sum(jnp.sum(e * e, axis=-1, keepdims=True), axis=0, keepdims=True)
        dy = e * (1.0 / d)
        dx = _rms_bwd(xv, r, gains[0], dy)
        return [dx, dx], [jnp.broadcast_to(sq * (0.5 / d), (1, LANE)), _colsum(dy * xv * r)]

    return Post([tgt], [gain], [F32], [BF16], [LANE, d], fn)


def _mm(name, a, b, ca, cb, out_dtype=F32, addend=None, alpha=1.0, a_cols=None, after=(), post=None, transposed=False):
    a_start, a_width = a_cols if a_cols else (0, a.shape[1])
    m, k = (a.shape[0], a_width) if ca == 1 else (a_width, a.shape[0])
    n = b.shape[1 - cb]
    assert b.shape[cb] == k, (name, a.shape, b.shape)
    tn = _pick(n, (1024, 768, 512, 384, 256, 128))
    whole_b = bool(post) and k * tn * b.dtype.itemsize <= WHOLE_WEIGHT_BYTES
    if whole_b:
        tk = k
        tm = _pick(m, (512, 256, 128))
    else:
        tm = _pick(m, (512, 256, 128) if post else (1024, 512, 256, 128))
        tk = _pick(k, (2048, 1024, 768, 512, 256, 128) if k >= 4096 and not post else (1024, 768, 512, 256, 128))
    nk = k // tk
    if ca == 1:
        assert a_start % tk == 0
        a_spec = pl.BlockSpec((tm, tk), lambda i, j, kk: (i, kk + a_start // tk))
    else:
        assert a_start % tm == 0
        a_spec = pl.BlockSpec((tk, tm), lambda i, j, kk: (kk, i + a_start // tm))
    b_mode = dict(pipeline_mode=pl.Buffered(1)) if whole_b else {}
    b_spec = (pl.BlockSpec((tk, tn), lambda i, j, kk: (kk, j), **b_mode) if cb == 0 else
              pl.BlockSpec((tn, tk), lambda i, j, kk: (j, kk), **b_mode))
    o_spec = pl.BlockSpec((tm, tn), lambda i, j, kk: (i, j))
    t_spec = pl.BlockSpec((tn, tm), lambda i, j, kk: (j, i))
    fixed = lambda w: pl.BlockSpec((1, w), lambda i, j, kk: (0, 0))
    ins, in_specs = [a, b], [a_spec, b_spec]
    if addend is not None:
        ins.append(addend)
        in_specs.append(o_spec)
    n_plain = len(ins)
    n_rows, n_gains = (len(post.rows), len(post.gains)) if post else (0, 0)
    if post:
        assert tn == n, name
        ins += post.rows + post.gains
        in_specs += [o_spec] * n_rows + [fixed(n)] * n_gains
    ins += list(after)
    in_specs += [ANY] * len(after)
    n_in = len(ins)
    if post:
        n_straight, n_vals = len(post.outs), len(post.outs) + len(post.t_outs)
        out_specs = [o_spec] * n_straight + [t_spec] * len(post.t_outs) + [fixed(w) for w in post.sums]
        out_shape = [jax.ShapeDtypeStruct((m, n), dt) for dt in post.outs] + [jax.ShapeDtypeStruct((n, m), dt) for dt in post.t_outs]
        out_shape += [jax.ShapeDtypeStruct((1, w), F32) for w in post.sums]
    elif transposed:
        out_specs, out_shape = [t_spec], [jax.ShapeDtypeStruct((n, m), out_dtype)]
    else:
        out_specs, out_shape = [o_spec], [jax.ShapeDtypeStruct((m, n), out_dtype)]
    n_out = len(out_specs)
    dims = (((ca,), (cb,)), ((), ()))

    def emit(refs, r):
        if alpha != 1.0:
            r = r * alpha
        if addend is not None:
            r = r + refs[2][...].astype(F32)
        outs = refs[n_in:n_in + n_out]
        if post is None:
            outs[0][...] = (r.T if transposed else r).astype(out_dtype)
            return
        vals, incs = post.fn(r, [q[...] for q in refs[n_plain:n_plain + n_rows]],
                             [q[...] for q in refs[n_plain + n_rows:n_plain + n_rows + n_gains]])
        for at, (o_ref, val) in enumerate(zip(outs, vals)):
            o_ref[...] = (val if at < n_straight else val.T).astype(o_ref.dtype)
        for s_ref, inc in zip(outs[n_vals:], incs):
            s_ref[...] += inc

    def kern(*refs):
        kk = pl.program_id(2)
        if post and post.sums:
            @pl.when(jnp.logical_and(jnp.logical_and(pl.program_id(0) == 0, pl.program_id(1) == 0), kk == 0))
            def _():
                for s_ref in refs[n_in + n_vals:n_in + n_out]:
                    s_ref[...] = jnp.zeros_like(s_ref)

        dot = lambda: lax.dot_general(refs[0][...].astype(BF16), refs[1][...].astype(BF16), dims,
                                      preferred_element_type=F32)
        if nk == 1:
            emit(refs, dot())
            return
        acc_ref = refs[-1]

        @pl.when(kk == 0)
        def _():
            acc_ref[...] = jnp.zeros_like(acc_ref)

        acc_ref[...] += dot()

        @pl.when(kk == nk - 1)
        def _():
            emit(refs, acc_ref[...])

    res = pl.pallas_call(kern, name=name, grid=(m // tm, n // tn, nk), in_specs=in_specs, out_specs=out_specs,
                         out_shape=out_shape, scratch_shapes=[] if nk == 1 else [pltpu.VMEM((tm, tn), F32)],
                         compiler_params=_params("arbitrary", "arbitrary", "arbitrary"))(*ins)
    return res if post else res[0]


def _rms_fwd(name, x, g):
    t, d = x.shape
    tt = _pick(t, (ROW_TILE, LANE))

    def kern(x_ref, g_ref, h_ref, ht_ref):
        xv = x_ref[...]
        h = xv * _rms_r(xv) * g_ref[...]
        h_ref[...] = h.astype(BF16)
        ht_ref[...] = h.T.astype(BF16)

    return pl.pallas_call(kern, name=name, grid=(t // tt,),
                          in_specs=[pl.BlockSpec((tt, d), lambda i: (i, 0)), pl.BlockSpec((1, d), lambda i: (0, 0))],
                          out_specs=[pl.BlockSpec((tt, d), lambda i: (i, 0)), pl.BlockSpec((d, tt), lambda i: (0, i))],
                          out_shape=[jax.ShapeDtypeStruct((t, d), BF16), jax.ShapeDtypeStruct((d, t), BF16)],
                          compiler_params=_params("arbitrary"))(x, g)


def _ffn_up(name, h, w1, w3):
    t, d = h.shape
    ff = w1.shape[0]
    tm, tn = _pick(t, (1024, 512, 256, 128)), _pick(ff, (1024, 768, 512, 256, 128))

    def kern(h_ref, w1_ref, w3_ref, a_ref, b_ref, z_ref):
        hv = h_ref[...]
        a = lax.dot_general(hv, w1_ref[...], NT, preferred_element_type=F32)
        b = lax.dot_general(hv, w3_ref[...], NT, preferred_element_type=F32)
        a_ref[...] = a.astype(BF16)
        b_ref[...] = b.astype(BF16)
        z_ref[...] = (a * _sigmoid(a) * b).astype(BF16)

    w_spec = pl.BlockSpec((tn, d), lambda i, j: (j, 0))
    o_spec = pl.BlockSpec((tm, tn), lambda i, j: (i, j))
    return pl.pallas_call(kern, name=name, grid=(t // tm, ff // tn),
                          in_specs=[pl.BlockSpec((tm, d), lambda i, j: (i, 0)), w_spec, w_spec], out_specs=[o_spec] * 3,
                          out_shape=[jax.ShapeDtypeStruct((t, ff), BF16)] * 3,
                          compiler_params=_params("arbitrary", "arbitrary"))(h, w1, w3)


def _ffn_dglu(name, dxo, w2, a, b, after=()):
    t, d = dxo.shape
    ff = w2.shape[0]
    tm = _pick(t, (FFN_ROWS, 128))

    def kern(dx_ref, w2_ref, a_ref, b_ref, *rest):
        da_ref, db_ref = rest[-2:]
        dz = lax.dot_general(dx_ref[...].astype(BF16), w2_ref[...], NT, preferred_element_type=F32) * 0.5
        av, bv = a_ref[...].astype(F32), b_ref[...].astype(F32)
        s = _sigmoid(av)
        da_ref[...] = (dz * bv * _dsilu(av, s)).astype(BF16)
        db_ref[...] = (dz * av * s).astype(BF16)

    o_spec = pl.BlockSpec((tm, ff), lambda i: (i, 0))
    return pl.pallas_call(kern, name=name, grid=(t // tm,),
                          in_specs=[pl.BlockSpec((tm, d), lambda i: (i, 0)), pl.BlockSpec((ff, d), lambda i: (0, 0)),
                                    o_spec, o_spec] + [ANY] * len(after),
                          out_specs=[o_spec] * 2, out_shape=[jax.ShapeDtypeStruct((t, ff), BF16)] * 2,
                          compiler_params=_params("arbitrary"))(dxo, w2, a, b, *after)


def _ffn_dh(name, da, db, w1, w3, x, g, dres, after=()):
    t, d = x.shape
    ff = da.shape[1]
    tm = _pick(t, (2 * FFN_ROWS, 128))

    def kern(da_ref, db_ref, w1_ref, w3_ref, x_ref, g_ref, dres_ref, *rest):
        dx_ref, dg_ref = rest[-2:]

        @pl.when(pl.program_id(0) == 0)
        def _():
            dg_ref[...] = jnp.zeros_like(dg_ref)

        dh = (jnp.dot(da_ref[...], w1_ref[...], preferred_element_type=F32)
              + jnp.dot(db_ref[...], w3_ref[...], preferred_element_type=F32))
        xv = x_ref[...]
        r = _rms_r(xv)
        dx_ref[...] = dres_ref[...] + _rms_bwd(xv, r, g_ref[...], dh)
        dg_ref[...] += _colsum(dh * xv * r)

    act = pl.BlockSpec((tm, ff), lambda i: (i, 0))
    wgt = pl.BlockSpec((ff, d), lambda i: (0, 0), pipeline_mode=pl.Buffered(1))
    rows = pl.BlockSpec((tm, d), lambda i: (i, 0))
    gain = pl.BlockSpec((1, d), lambda i: (0, 0))
    return pl.pallas_call(kern, name=name, grid=(t // tm,),
                          in_specs=[act, act, wgt, wgt, rows, gain, rows] + [ANY] * len(after), out_specs=[rows, gain],
                          out_shape=[jax.ShapeDtypeStruct((t, d), F32), jax.ShapeDtypeStruct((1, d), F32)],
                          compiler_params=_params("arbitrary"))(da, db, w1, w3, x, g, dres, *after)


def _ffn_bwd(tag, x, g, w1, w3, w2, saved, dxo, dxo_t, early):
    ht, a, b, z = saved
    dw2 = _mm(tag + "_dw2", dxo_t, z, 1, 0, BF16, alpha=0.5, transposed=True)
    da, db = _ffn_dglu(tag + "_dglu", dxo, w2, a, b)
    dw1 = _mm(tag + "_dw1", ht, da, 1, 0, BF16, transposed=True)
    sent, pin = {}, []
    if early:
        sent[tag + "_w2 " + tag + "_w1"] = (_exchange_start(tag + "_w2_w1_send", "scatter", [dw2, dw1], [0, 0]), [0, 0])
        pin = [sent[tag + "_w2 " + tag + "_w1"][0][3]]
    dw3 = _mm(tag + "_dw3", ht, db, 1, 0, BF16, after=pin, transposed=True)
    last = [dw3] if early else [dw2, dw1, dw3]
    names = [tag + "_w3"] if early else [tag + "_w2", tag + "_w1", tag + "_w3"]
    sent[" ".join(names)] = (_exchange_start(tag + "_w3_send", "scatter", last, [0] * len(last)), [0] * len(last))
    dx, dg = _ffn_dh(tag + "_dh", da, db, w1, w3, x, g, dxo, after=[sent[" ".join(names)][0][3]])
    return dx, dg, sent


def _shift_copies(ext_ref, sh_ref):
    n = ext_ref.shape[0] - SUBLANE
    for r in range(1, SUBLANE):
        sh_ref[r, pl.ds(0, n), :] = ext_ref[pl.ds(r, n), :]


def _rows_at(ext_ref, sh_ref, off, rows):
    r = off % SUBLANE
    return ext_ref[pl.ds(off, rows), :] if r == 0 else sh_ref[r, pl.ds(off - r, rows), :]


def _conv_fwd(proj, cw, cb, lng, lnb, og, seq):
    n_rows, c = proj.shape[0], cb.shape[1]
    kw = HALO - 1
    tt = _pick(seq, (CONV_TILE,))
    hb = tt // HALO

    def kern(v_ref, g_ref, vp_ref, gp_ref, w_ref, cb_ref, lg_ref, lb_ref, og_ref, c_ref, an_ref, ext_ref, sh_ref):
        first = (pl.program_id(0) * tt) % seq == 0
        ext_ref[pl.ds(HALO, tt), :] = v_ref[...] * _sigmoid(g_ref[...])
        ext_ref[pl.ds(0, HALO), :] = vp_ref[...] * _sigmoid(gp_ref[...]) * jnp.where(first, 0.0, 1.0)
        _shift_copies(ext_ref, sh_ref)
        for r0 in range(0, tt, CONV_SUB):
            rows = min(CONV_SUB, tt - r0)
            acc = jnp.zeros((rows, c), F32)
            for k in range(kw):
                acc = acc + w_ref[pl.ds(k, 1), :] * _rows_at(ext_ref, sh_ref, r0 + HALO - (kw - 1) + k, rows)
            c_ref[pl.ds(r0, rows), :] = acc + cb_ref[...]
        cv = c_ref[...]
        mu = jnp.mean(cv, axis=-1, keepdims=True)
        xc = cv - mu
        rstd = lax.rsqrt(jnp.mean(xc * xc, axis=-1, keepdims=True) + EPS)
        lv = xc * rstd * lg_ref[...] + lb_ref[...]
        sl = lv * _sigmoid(lv)
        an_ref[...] = (sl * _rms_r(sl) * og_ref[...]).astype(BF16)

    cur = lambda cbk: pl.BlockSpec((tt, c), lambda i: (i, cbk))
    prev = lambda cbk: pl.BlockSpec((HALO, c), lambda i: (jnp.maximum(i * hb - 1, 0), cbk))
    par = lambda p: pl.BlockSpec(p.shape, lambda i: (0, 0))
    return pl.pallas_call(
        kern, name="conv_fwd", grid=(n_rows // tt,),
        in_specs=[cur(0), cur(1), prev(0), prev(1), par(cw), par(cb), par(lng), par(lnb), par(og)],
        out_specs=[pl.BlockSpec((tt, c), lambda i: (i, 0))] * 2,
        out_shape=[jax.ShapeDtypeStruct((n_rows, c), F32), jax.ShapeDtypeStruct((n_rows, c), BF16)],
        scratch_shapes=[pltpu.VMEM((tt + HALO, c), F32), pltpu.VMEM((SUBLANE, tt + HALO, c), F32)],
        compiler_params=_params("arbitrary"),
    )(proj, proj, proj, proj, cw, cb, lng, lnb, og)


def _conv_bwd_rows(dmixed, cpre, lng, lnb, og):
    c = cpre.shape[1]

    def body(ins, outs, accs):
        dan, cv, lg, lb, ogv = ins[0][...], ins[1][...], ins[2][...], ins[3][...], ins[4][...]
        mu = jnp.mean(cv, axis=-1, keepdims=True)
        xc = cv - mu
        rstd = lax.rsqrt(jnp.mean(xc * xc, axis=-1, keepdims=True) + EPS)
        xh = xc * rstd
        lv = xh * lg + lb
        s = _sigmoid(lv)
        sl = lv * s
        r2 = _rms_r(sl)
        accs[0][...] += _colsum(dan * sl * r2)
        dl = _rms_bwd(sl, r2, ogv, dan) * _dsilu(lv, s)
        accs[1][...] += _colsum(dl * xh)
        accs[2][...] += _colsum(dl)
        dxh = dl * lg
        dc = rstd * (dxh - jnp.mean(dxh, axis=-1, keepdims=True) - xh * jnp.mean(dxh * xh, axis=-1, keepdims=True))
        outs[0][...] = dc
        accs[3][...] += _colsum(dc)

    return _rowwise("conv_bwd_rows", body, cpre.shape[0], [(dmixed, c, 0), (cpre, c, 0)], [lng, lnb, og], [(c, F32)],
                    [(1, c)] * 4)


def _conv_bwd_taps(proj, dc, cw, seq):
    n_rows, c = dc.shape
    kw = HALO - 1
    tt = _pick(seq, (CONV_TILE,))
    hb = tt // HALO
    last_blk = n_rows // HALO - 1

    def kern(v_ref, g_ref, vp_ref, gp_ref, dc_ref, dn_ref, w_ref, dv_ref, dg_ref, dw_ref, exta_ref, extd_ref, sha_ref, shd_ref):
        i = pl.program_id(0)
        first = (i * tt) % seq == 0
        last = ((i + 1) * tt) % seq == 0

        @pl.when(i == 0)
        def _():
            dw_ref[...] = jnp.zeros_like(dw_ref)

        sg = _sigmoid(g_ref[...])
        exta_ref[pl.ds(HALO, tt), :] = v_ref[...] * sg
        exta_ref[pl.ds(0, HALO), :] = vp_ref[...] * _sigmoid(gp_ref[...]) * jnp.where(first, 0.0, 1.0)
        dcv = dc_ref[...]
        extd_ref[pl.ds(0, tt), :] = dcv
        extd_ref[pl.ds(tt, HALO), :] = dn_ref[...] * jnp.where(last, 0.0, 1.0)
        _shift_copies(exta_ref, sha_ref)
        _shift_copies(extd_ref, shd_ref)
        for k in range(kw):
            dw_ref[pl.ds(k, 1), :] += _colsum(_rows_at(exta_ref, sha_ref, HALO - (kw - 1) + k, tt) * dcv)
        for r0 in range(0, tt, CONV_SUB):
            rows = min(CONV_SUB, tt - r0)
            acc = jnp.zeros((rows, c), F32)
            for k in range(kw):
                acc = acc + w_ref[pl.ds(k, 1), :] * _rows_at(extd_ref, shd_ref, r0 + (kw - 1) - k, rows)
            dv_ref[pl.ds(r0, rows), :] = acc
        da = dv_ref[...]
        dv_ref[...] = da * sg
        dg_ref[...] = da * v_ref[...] * sg * (1.0 - sg)

    cur = lambda cbk: pl.BlockSpec((tt, c), lambda i: (i, cbk))
    prev = lambda cbk: pl.BlockSpec((HALO, c), lambda i: (jnp.maximum(i * hb - 1, 0), cbk))
    nxt = pl.BlockSpec((HALO, c), lambda i: (jnp.minimum((i + 1) * hb, last_blk), 0))
    return pl.pallas_call(
        kern, name="conv_bwd_taps", grid=(n_rows // tt,),
        in_specs=[cur(0), cur(1), prev(0), prev(1), cur(0), nxt, pl.BlockSpec(cw.shape, lambda i: (0, 0))],
        out_specs=[cur(0), cur(0), pl.BlockSpec((HALO, c), lambda i: (0, 0))],
        out_shape=[jax.ShapeDtypeStruct((n_rows, c), F32), jax.ShapeDtypeStruct((n_rows, c), F32),
                   jax.ShapeDtypeStruct((HALO, c), F32)],
        scratch_shapes=[pltpu.VMEM((tt + HALO, c), F32)] * 2 + [pltpu.VMEM((SUBLANE, tt + HALO, c), F32)] * 2,
        compiler_params=_params("arbitrary"),
    )(proj, proj, proj, proj, dc, dc, cw)


def _s5_params_fwd(lr, li, ldt, btr, bti, seg):
    ns = lr.shape[1]

    def kern(lr_ref, li_ref, ldt_ref, btr_ref, bti_ref, ar_ref, ai_ref, bbr_ref, bbi_ref, ps_ref, psf_ref, pc_ref, pcf_ref):
        lrv, liv = lr_ref[...], li_ref[...]
        dt = jnp.exp(ldt_ref[...])
        zr, zi = lrv * dt, liv * dt
        mag = jnp.exp(zr)
        ar, ai = mag * jnp.cos(zi), mag * jnp.sin(zi)
        den = lrv * lrv + liv * liv
        nr = ar - 1.0
        cr = (nr * lrv + ai * liv) / den
        ci = (ai * lrv - nr * liv) / den
        ar_ref[...] = ar
        ai_ref[...] = ai
        bbr_ref[...] = cr * btr_ref[...] - ci * bti_ref[...]
        bbi_ref[...] = cr * bti_ref[...] + ci * btr_ref[...]
        def powers(br, bi, count, up_ref, down_ref):
            pr, pi = br, bi
            for e in range(count):
                for ref, at in ((up_ref, e), (down_ref, count - 1 - e)):
                    ref[pl.ds(at, 1), pl.ds(0, ns)] = pr
                    ref[pl.ds(at, 1), pl.ds(ns, ns)] = pi
                if e < count - 1:
                    pr, pi = pr * br - pi * bi, pr * bi + pi * br
            return pr, pi

        powers(*powers(ar, ai, seg, ps_ref, psf_ref), SUBLANE, pc_ref, pcf_ref)

    h = btr.shape[0]
    shapes = [jax.ShapeDtypeStruct((1, ns), F32)] * 2 + [jax.ShapeDtypeStruct((h, ns), F32)] * 2
    shapes += [jax.ShapeDtypeStruct((seg, 2 * ns), F32)] * 2 + [jax.ShapeDtypeStruct((SUBLANE, 2 * ns), F32)] * 2
    return pl.pallas_call(kern, name="s5_params_fwd", out_shape=shapes)(lr, li, ldt, btr, bti)


def _s5_params_bwd(lr, li, ldt, btr, bti, dar, dai, dbbr, dbbi):
    def kern(lr_ref, li_ref, ldt_ref, btr_ref, bti_ref, dar_ref, dai_ref, dbr_ref, dbi_ref,
             dlr_ref, dli_ref, dldt_ref, dbtr_ref, dbti_ref):
        lrv, liv = lr_ref[...], li_ref[...]
        dt = jnp.exp(ldt_ref[...])
        zr, zi = lrv * dt, liv * dt
        mag = jnp.exp(zr)
        ar, ai = mag * jnp.cos(zi), mag * jnp.sin(zi)
        den = lrv * lrv + liv * liv
        nr = ar - 1.0
        cr = (nr * lrv + ai * liv) / den
        ci = (ai * lrv - nr * liv) / den
        dbr, dbi, br, bi = dbr_ref[...], dbi_ref[...], btr_ref[...], bti_ref[...]
        dbtr_ref[...] = cr * dbr + ci * dbi
        dbti_ref[...] = cr * dbi - ci * dbr
        dcr = _colsum(br * dbr + bi * dbi)
        dci = _colsum(br * dbi - bi * dbr)
        ir, ii = lrv / den, -liv / den
        dnr = ir * dcr + ii * dci
        dni = ir * dci - ii * dcr
        wr, wi = cr * ir - ci * ii, cr * ii + ci * ir
        dl1r = -(wr * dcr + wi * dci)
        dl1i = -(wr * dci - wi * dcr)
        dtr, dti = dar_ref[...] + dnr, dai_ref[...] + dni
        dzr = ar * dtr + ai * dti
        dzi = ar * dti - ai * dtr
        dlr_ref[...] = dl1r + dt * dzr
        dli_ref[...] = dl1i + dt * dzi
        dldt_ref[...] = (dzr * lrv + dzi * liv) * dt

    ns, h = lr.shape[1], btr.shape[0]
    shapes = [jax.ShapeDtypeStruct((1, ns), F32)] * 3 + [jax.ShapeDtypeStruct((h, ns), F32)] * 2
    return pl.pallas_call(kern, name="s5_params_bwd", out_shape=shapes)(lr, li, ldt, btr, bti, dar, dai, dbbr, dbbi)


def _to_segments(nat_ref, seg_ref):
    steps = nat_ref.shape[0] // SUBLANE
    _regroup(nat_ref, seg_ref, lambda r: (r % SUBLANE) * steps + r // SUBLANE)


def _from_segments(seg_ref, nat_ref):
    steps = nat_ref.shape[0] // SUBLANE
    _regroup(seg_ref, nat_ref, lambda r: (r % steps) * SUBLANE + r // steps)


def _regroup(src_ref, dst_ref, src_row):
    rows, width = dst_ref.shape
    sublane = lax.broadcasted_iota(jnp.int32, (SUBLANE, width), 0)
    for r0 in range(0, rows, SUBLANE):
        tile = jnp.broadcast_to(src_ref[pl.ds(src_row(r0), 1), :], (SUBLANE, width))
        for k in range(1, SUBLANE):
            tile = jnp.where(sublane == k, src_ref[pl.ds(src_row(r0 + k), 1), :], tile)
        dst_ref[pl.ds(r0, SUBLANE), :] = tile


def _scan_tile(s_ref, o_ref, fix_ref, tabs, car_ref, sb, reverse, x_ref=None, acc_ref=None):
    l1, l2, l4, pw = tabs
    rows_t, w = s_ref.shape
    steps = rows_t // SUBLANE
    cw = _pick(sb, (SCAN_COLS,))
    last = 0 if reverse else SUBLANE - 1
    first = SUBLANE - 1 - last
    row = lax.broadcasted_iota(jnp.int32, (SUBLANE, cw), 0)
    step_rows = lambda i: pl.ds(pl.multiple_of(((steps - 1 - i) if reverse else i) * SUBLANE, SUBLANE), SUBLANE)
    zero = jnp.zeros((SUBLANE, cw), F32)

    for c0 in [b0 + o for b0 in range(0, w, 2 * sb) for o in range(0, sb, cw)]:
        cr, ci = pl.ds(c0, cw), pl.ds(c0 + sb, cw)
        base = pl.ds(((steps - 1) if reverse else 0) * SUBLANE, SUBLANE)
        ar, ai = fix_ref[base, cr], fix_ref[base, ci]

        def run(i, state):
            xr, xi = state
            rows = step_rows(i)
            xr, xi = ar * xr - ai * xi + s_ref[rows, cr], ar * xi + ai * xr + s_ref[rows, ci]
            o_ref[rows, cr] = xr
            o_ref[rows, ci] = xi
            return xr, xi

        fr, fi = lax.fori_loop(0, steps, run, (zero, zero))
        for s, lt in ((1, l1), (2, l2), (4, l4)):
            sh = (SUBLANE - s) if reverse else s
            sr, si = pltpu.roll(fr, sh, 0), pltpu.roll(fi, sh, 0)
            tr, ti = lt[:, cr], lt[:, ci]
            fr, fi = fr + tr * sr - ti * si, fi + tr * si + ti * sr
        kr, ki = car_ref[pl.ds(last, 1), cr], car_ref[pl.ds(last, 1), ci]
        pr, pi = pw[:, cr], pw[:, ci]
        fr, fi = fr + pr * kr - pi * ki, fi + pr * ki + pi * kr
        car_ref[:, cr] = fr
        car_ref[:, ci] = fi
        to_next = 1 if not reverse else SUBLANE - 1
        gr = jnp.where(row == first, kr, pltpu.roll(fr, to_next, 0))
        gi = jnp.where(row == first, ki, pltpu.roll(fi, to_next, 0))

        def fix(i, state):
            rows = step_rows(i)
            qr, qi = fix_ref[rows, cr], fix_ref[rows, ci]
            yr = o_ref[rows, cr] + qr * gr - qi * gi
            yi = o_ref[rows, ci] + qr * gi + qi * gr
            o_ref[rows, cr] = yr
            o_ref[rows, ci] = yi
            if acc_ref is None:
                return state
            nr, ni, sr, si = state
            pxr, pxi = x_ref[rows, cr], x_ref[rows, ci]
            return yr, yi, sr + nr * pxr + ni * pxi, si + ni * pxr - nr * pxi

        if acc_ref is None:
            lax.fori_loop(0, steps, fix, 0)
        else:
            _, _, sr, si = lax.fori_loop(0, steps, fix, (gr, gi, zero, zero))
            acc_ref[:, cr] += sr
            acc_ref[:, ci] += si


def _s5_fwd(proj, u_blk, bdc, cdc, fix, tabs, dskip, seq, sb):
    n_rows = proj.shape[0]
    nb, blk, w_blk = bdc.shape
    c, w = nb * blk, nb * w_blk
    tt = fix.shape[0]

    def kern(u_ref, bd_ref, cd_ref, fix_ref, l1, l2, l4, pw, d_ref, xs_ref, yp_ref, yg_ref, us_ref, bu_ref, car_ref):
        @pl.when((pl.program_id(0) * tt) % seq == 0)
        def _():
            car_ref[...] = jnp.zeros_like(car_ref)

        _to_segments(u_ref, us_ref)
        for j in range(nb):
            bu_ref[:, pl.ds(j * w_blk, w_blk)] = jnp.dot(us_ref[:, pl.ds(j * blk, blk)].astype(BF16), bd_ref[j],
                                                         preferred_element_type=F32)
        _scan_tile(bu_ref, xs_ref, fix_ref, (l1, l2, l4, pw), car_ref, sb, False)
        for j in range(nb):
            cols = pl.ds(j * blk, blk)
            y0 = jnp.dot(xs_ref[:, pl.ds(j * w_blk, w_blk)].astype(BF16), cd_ref[j], preferred_element_type=F32)
            us_ref[:, cols] = y0 + d_ref[:, cols] * us_ref[:, cols]
        _from_segments(us_ref, yp_ref)
        yg_ref[...] = _gelu(yp_ref[...]).astype(BF16)

    tab = pl.BlockSpec((SUBLANE, w), lambda i: (0, 0))
    rows = pl.BlockSpec((tt, c), lambda i: (i, 0))
    return pl.pallas_call(
        kern, name="s5_fwd", grid=(n_rows // tt,),
        in_specs=[pl.BlockSpec((tt, c), lambda i: (i, u_blk * blk // c)), pl.BlockSpec(bdc.shape, lambda i: (0, 0, 0)),
                  pl.BlockSpec(cdc.shape, lambda i: (0, 0, 0)), pl.BlockSpec((tt, w), lambda i: (0, 0)), tab, tab, tab, tab,
                  pl.BlockSpec((1, c), lambda i: (0, 0))],
        out_specs=[pl.BlockSpec((tt, w), lambda i: (i, 0)), rows, rows],
        out_shape=[jax.ShapeDtypeStruct((n_rows, w), F32), jax.ShapeDtypeStruct((n_rows, c), F32),
                   jax.ShapeDtypeStruct((n_rows, c), BF16)],
        scratch_shapes=[pltpu.VMEM((tt, c), F32), pltpu.VMEM((tt, w), F32), pltpu.VMEM((SUBLANE, w), F32)],
        compiler_params=_params("arbitrary"))(proj, bdc, cdc, fix, *tabs, dskip)


def _s5_bwd(dypre, du_skip, xs, proj, u_blk, bdc, cdc, fix, tabs, seq, sb):
    n_rows = proj.shape[0]
    nb, blk, w_blk = bdc.shape
    c, w = nb * blk, nb * w_blk
    tt = fix.shape[0]
    nt = n_rows // tt
    tn = (((0,), (0,)), ((), ()))

    def kern(dy_ref, ds_ref, x_ref, u_ref, bd_ref, cd_ref, fix_ref, l1, l2, l4, pw, du_ref, da_ref, db_ref, dc_ref,
             dys_ref, us_ref, dus_ref, gx_ref, lam_ref, car_ref, acc_ref):
        i = pl.program_id(0)

        @pl.when(((nt - i) * tt) % seq == 0)
        def _():
            car_ref[...] = jnp.zeros_like(car_ref)

        @pl.when(i == 0)
        def _():
            acc_ref[...] = jnp.zeros_like(acc_ref)
            db_ref[...] = jnp.zeros_like(db_ref)
            dc_ref[...] = jnp.zeros_like(dc_ref)

        _to_segments(dy_ref, dys_ref)
        _to_segments(u_ref, us_ref)
        for j in range(nb):
            gx_ref[:, pl.ds(j * w_blk, w_blk)] = lax.dot_general(dys_ref[:, pl.ds(j * blk, blk)].astype(BF16), cd_ref[j], NT,
                                                                 preferred_element_type=F32)
        _scan_tile(gx_ref, lam_ref, fix_ref, (l1, l2, l4, pw), car_ref, sb, True, x_ref, acc_ref)
        for j in range(nb):
            cols, wide = pl.ds(j * blk, blk), pl.ds(j * w_blk, w_blk)
            lam = lam_ref[:, wide].astype(BF16)
            dus_ref[:, cols] = lax.dot_general(lam, bd_ref[j], NT, preferred_element_type=F32)
            db_ref[j] += lax.dot_general(us_ref[:, cols].astype(BF16), lam, tn, preferred_element_type=F32)
            dc_ref[j] += lax.dot_general(x_ref[:, wide].astype(BF16), dys_ref[:, cols].astype(BF16), tn,
                                         preferred_element_type=F32)
        _from_segments(dus_ref, du_ref)
        du_ref[...] += ds_ref[...]

        @pl.when(i == nt - 1)
        def _():
            da_ref[...] = _colsum(acc_ref[...])

    back = lambda i: (nt - 1 - i, 0)
    tab = pl.BlockSpec((SUBLANE, w), lambda i: (0, 0))
    rows = pl.BlockSpec((tt, c), back)
    whole = lambda a: pl.BlockSpec(a.shape, lambda i: (0, 0, 0))
    return pl.pallas_call(
        kern, name="s5_bwd", grid=(nt,),
        in_specs=[rows, rows, pl.BlockSpec((tt, w), back), pl.BlockSpec((tt, c), lambda i: (nt - 1 - i, u_blk * blk // c)),
                  whole(bdc), whole(cdc), pl.BlockSpec((tt, w), lambda i: (0, 0)), tab, tab, tab, tab],
        out_specs=[rows, pl.BlockSpec((1, w), lambda i: (0, 0)), whole(bdc), whole(cdc)],
        out_shape=[jax.ShapeDtypeStruct((n_rows, c), F32), jax.ShapeDtypeStruct((1, w), F32),
                   jax.ShapeDtypeStruct(bdc.shape, F32), jax.ShapeDtypeStruct(cdc.shape, F32)],
        scratch_shapes=[pltpu.VMEM((tt, c), F32)] * 3 + [pltpu.VMEM((tt, w), F32)] * 2 + [pltpu.VMEM((SUBLANE, w), F32)] * 2,
        compiler_params=_params("arbitrary"))(dypre, du_skip, xs, proj, bdc, cdc, fix, *tabs)


def _s5_post2(yg, q0, bg, og):
    c = yg.shape[1]

    def body(ins, outs, accs):
        ygv = ins[0][...].astype(F32)
        sg = ygv * _sigmoid(ins[1][...] + ins[2][...])
        outs[0][...] = (sg * _rms_r(sg) * ins[3][...]).astype(BF16)

    return _rowwise("s5_post2", body, yg.shape[0], [(yg, c, 0), (q0, c, 0)], [bg, og], [(c, BF16)], [])[0]


def _s5_post2_bwd(dmixed, yg, q0, bg, og):
    c = yg.shape[1]

    def body(ins, outs, accs):
        dsn, ygv = ins[0][...], ins[1][...].astype(F32)
        s = _sigmoid(ins[2][...] + ins[3][...])
        sg = ygv * s
        r = _rms_r(sg)
        accs[0][...] += _colsum(dsn * sg * r)
        dsg = _rms_bwd(sg, r, ins[4][...], dsn)
        dq = dsg * ygv * s * (1.0 - s)
        outs[0][...] = dq.astype(BF16)
        outs[1][...] = dsg * s
        accs[1][...] += _colsum(dq)

    return _rowwise("s5_post2_bwd", body, yg.shape[0], [(dmixed, c, 1), (yg, c, 0), (q0, c, 0)], [bg, og],
                    [(c, BF16), (c, F32)], [(1, c)] * 2)


def _s5_post1_bwd(dyg1, dyg2, ypre, proj, dskip, after=()):
    c = ypre.shape[1]

    def body(ins, outs, accs):
        dyp = (ins[0][...] + ins[1][...]) * _dgelu(ins[2][...])
        outs[0][...] = dyp
        outs[1][...] = dyp * ins[4][...]
        accs[0][...] += _colsum(dyp * ins[3][...])

    return _rowwise("s5_post1_bwd", body, ypre.shape[0], [(dyg1, c, 0), (dyg2, c, 0), (ypre, c, 0), (proj, c, 2)], [dskip],
                    [(c, F32), (c, F32)], [(1, c)], after=after)


def _place():
    return lax.axis_index("x"), lax.axis_index("y"), lax.axis_index("c")


def _window(ref, axis, q, rows, cols):
    if axis == 0:
        return ref.at[pl.ds(pl.multiple_of(q * rows, SUBLANE), rows), :]
    return ref.at[:, pl.ds(pl.multiple_of(q * cols, LANE), cols)]


ALL_RELS = [(fx, fy, fc) for fx in (0, 1) for fy in (0, 1) for fc in (0, 1)][1:]
N_PEERS = {"gather": 3, "scatter": 3, "sibling": 1, "all": len(ALL_RELS)}


def _copies(kind, srcs, lands, shards, axes, send_sems, recv_sems, local_sems):
    x, y, c = _place()
    me, dev = 2 * x + y, 4 * x + 2 * y + c
    n_peers = N_PEERS[kind]
    starts, waits = [], []
    for a, (src, land) in enumerate(zip(srcs, lands)):
        on = lambda k, peer: dict(send_sem=send_sems.at[n_peers * a + k], recv_sem=recv_sems.at[n_peers * a + k],
                                  device_id=peer, device_id_type=MESH)
        if kind == "sibling":
            cp = pltpu.make_async_remote_copy(src_ref=src, dst_ref=land, **on(0, (x, y, 1 - c)))
            starts.append(cp)
            waits.append(cp)
            continue
        if kind == "all":
            own = pltpu.make_async_copy(src, land.at[dev], local_sems.at[a])
            starts.append(own)
            waits.append(own)
            for k, (fx, fy, fc) in enumerate(ALL_RELS):
                px, py, pc = (1 - x) if fx else x, (1 - y) if fy else y, (1 - c) if fc else c
                starts.append(pltpu.make_async_remote_copy(src_ref=src, dst_ref=land.at[dev], **on(k, (px, py, pc))))
                waits.append(pltpu.make_async_remote_copy(src_ref=src, dst_ref=land.at[4 * px + 2 * py + pc],
                                                          **on(k, (px, py, pc))))
            continue
        rows, cols = shards[a]
        if kind == "gather":
            own = pltpu.make_async_copy(src, _window(land, axes[a], me, rows, cols), local_sems.at[a])
        else:
            own = pltpu.make_async_copy(_window(src, axes[a], me, rows, cols), land.at[3], local_sems.at[a])
        starts.append(own)
        waits.append(own)
        for j, (fx, fy) in enumerate(CHIP_RELS):
            px, py = (1 - x) if fx else x, (1 - y) if fy else y
            peer = 2 * px + py
            if kind == "gather":
                starts.append(pltpu.make_async_remote_copy(src_ref=src, dst_ref=_window(land, axes[a], me, rows, cols),
                                                           **on(j, (px, py, c))))
                waits.append(pltpu.make_async_remote_copy(src_ref=src, dst_ref=_window(land, axes[a], peer, rows, cols),
                                                          **on(j, (px, py, c))))
            else:
                cp = pltpu.make_async_remote_copy(src_ref=_window(src, axes[a], peer, rows, cols), dst_ref=land.at[j],
                                                  **on(j, (px, py, c)))
                starts.append(cp)
                waits.append(cp)
    return starts, waits


HBM = pl.BlockSpec(memory_space=pltpu.HBM)
SEM = pl.BlockSpec(memory_space=pltpu.SEMAPHORE)


def _shard_shapes(kind, arrs, axes):
    if kind != "scatter":
        return [a.shape for a in arrs]
    return [(a.shape[0] // N_CHIPS, a.shape[1]) if ax == 0 else (a.shape[0], a.shape[1] // N_CHIPS) for a, ax in zip(arrs, axes)]


def _land_shapes(kind, arrs, axes):
    if kind == "gather":
        return [(N_CHIPS * a.shape[0], a.shape[1]) if ax == 0 else (a.shape[0], N_CHIPS * a.shape[1]) for a, ax in zip(arrs, axes)]
    if kind == "scatter":
        return [(N_CHIPS,) + s for s in _shard_shapes(kind, arrs, axes)]
    return [a.shape if kind == "sibling" else (len(ALL_RELS) + 1,) + a.shape for a in arrs]


def _exchange_start(name, kind, arrs, axes, after=()):
    n, n_after = len(arrs), len(after)
    shards = _shard_shapes(kind, arrs, axes)
    land_shapes = _land_shapes(kind, arrs, axes)
    lands = [lax.empty(s, a.dtype) for s, a in zip(land_shapes, arrs)]

    def kern(*refs):
        outs = refs[2 * n + n_after:]
        starts, _ = _copies(kind, refs[:n], refs[n:2 * n], shards, axes, outs[0], outs[1], outs[2])
        for cp in starts:
            cp.start()
        outs[-1][...] = jnp.zeros_like(outs[-1])

    kept = [pltpu.HBM(a.shape, a.dtype) for a in arrs] + [pltpu.HBM(s, a.dtype) for s, a in zip(land_shapes, arrs)]
    n_sems = N_PEERS[kind] * n
    res = pl.pallas_call(
        kern, name=name, in_specs=[HBM] * (2 * n) + [ANY] * n_after,
        out_specs=[SEM] * 3 + [HBM] * (2 * n) + [pl.BlockSpec(memory_space=pltpu.VMEM)],
        out_shape=[pltpu.SemaphoreType.DMA((n_sems,)), pltpu.SemaphoreType.DMA((n_sems,)), pltpu.SemaphoreType.DMA((n,))]
        + kept + [jax.ShapeDtypeStruct((SUBLANE, LANE), F32)],
        input_output_aliases={i: 3 + i for i in range(2 * n)},
        compiler_params=pltpu.CompilerParams(has_side_effects=pltpu.SideEffectType.DATAFLOW_SIDE_EFFECTING),
    )(*[pltpu.with_memory_space_constraint(a, pltpu.HBM) for a in list(arrs) + lands], *after)
    return res[:3], res[3:3 + n], res[3 + n:3 + 2 * n], res[-1]


def _exchange_wait(name, kind, started, axes, after, sources_too=False):
    sems, srcs, lands, _ = started
    n, n_after = len(srcs), len(after)
    shards = _shard_shapes(kind, srcs, axes)

    def kern(*refs):
        sem_refs = refs[2 * n:2 * n + 3]
        _, waits = _copies(kind, refs[:n], refs[n:2 * n], shards, axes, *sem_refs)
        for cp in waits:
            cp.wait()

    res = pl.pallas_call(
        kern, name=name, in_specs=[HBM] * (2 * n) + [SEM] * 3 + [ANY] * n_after, out_specs=[HBM] * (2 * n),
        out_shape=[pltpu.HBM(a.shape, a.dtype) for a in list(srcs) + list(lands)],
        input_output_aliases={i: i for i in range(2 * n)},
        compiler_params=pltpu.CompilerParams(has_side_effects=pltpu.SideEffectType.DATAFLOW_SIDE_EFFECTING),
    )(*srcs, *lands, *sems, *after)
    return (res[:n], res[n:]) if sources_too else res[n:]


def _sum_devices(parts):
    def kern(p_ref, o_ref):
        acc = p_ref[0]
        for d in range(1, parts.shape[0]):
            acc = acc + p_ref[d]
        o_ref[...] = acc

    return pl.pallas_call(kern, name="sum_devices", out_shape=jax.ShapeDtypeStruct(parts.shape[1:], F32),
                          compiler_params=pltpu.CompilerParams(vmem_limit_bytes=VMEM_LIMIT_BYTES))(parts)


def _sum_slots(name, parts):
    _, rows, cols = parts.shape
    tr = _pick(rows, (ROW_TILE, 128, 64, 32))

    def kern(p_ref, o_ref):
        o_ref[...] = ((p_ref[3].astype(F32) + p_ref[0].astype(F32)) + p_ref[1].astype(F32)) + p_ref[2].astype(F32)

    return pl.pallas_call(kern, name=name, grid=(rows // tr,),
                          in_specs=[pl.BlockSpec((N_CHIPS, tr, cols), lambda i: (0, i, 0))],
                          out_specs=pl.BlockSpec((tr, cols), lambda i: (i, 0)),
                          out_shape=jax.ShapeDtypeStruct((rows, cols), F32), compiler_params=_params("arbitrary"))(parts)


def _adamw_math(g, w, m, v):
    m2 = ADAM_B1 * m + (1.0 - ADAM_B1) * g
    v2 = ADAM_B2 * v + (1.0 - ADAM_B2) * (g * g)
    m_hat = m2 / (1.0 - ADAM_B1 ** ADAM_STEP)
    v_hat = v2 / (1.0 - ADAM_B2 ** ADAM_STEP)
    return -ADAM_LR * (m_hat / (jnp.sqrt(v_hat) + ADAM_EPS) + ADAM_WD * w), m2, v2


def _adamw(name, parts, w, m, v):
    rows, cols = w.shape
    tr = rows if rows * cols <= WHOLE_ELEMS else _pick(rows, (ROW_TILE, 352, 128, 64, 32, 8))
    n = len(parts)

    def kern(*refs):
        g = refs[0][:, pl.ds(0, cols)]
        for p in refs[1:n]:
            g = g + p[:, pl.ds(0, cols)]
        d, m2, v2 = _adamw_math(g, refs[n][...], refs[n + 1][...], refs[n + 2][...])
        refs[n + 3][...] = g
        refs[n + 4][...] = d
        refs[n + 5][...] = m2
        refs[n + 6][...] = v2

    spec = pl.BlockSpec((tr, cols), lambda i: (i, 0))
    return pl.pallas_call(kern, name=name, grid=(rows // tr,),
                          in_specs=[pl.BlockSpec((tr, p.shape[1]), lambda i: (i, 0)) for p in parts] + [spec] * 3,
                          out_specs=[spec] * 4, out_shape=[jax.ShapeDtypeStruct((rows, cols), F32)] * 4,
                          compiler_params=_params("arbitrary"))(*parts, w, m, v)


def _adamw_many(name, gs, ws, ms, vs):
    n = len(gs)

    def kern(*refs):
        for p in range(n):
            d, m2, v2 = _adamw_math(refs[p][...], refs[n + p][...], refs[2 * n + p][...], refs[3 * n + p][...])
            refs[4 * n + p][...] = d
            refs[5 * n + p][...] = m2
            refs[6 * n + p][...] = v2

    res = pl.pallas_call(kern, name=name, out_shape=[jax.ShapeDtypeStruct(w.shape, F32) for w in ws] * 3,
                         compiler_params=pltpu.CompilerParams(vmem_limit_bytes=VMEM_LIMIT_BYTES))(*gs, *ws, *ms, *vs)
    return res[:n], res[n:2 * n], res[2 * n:]


def _pack(arrs):
    parts, rows = [], []
    for a in arrs:
        r = _round_up(-(-a.size // LANE), SUBLANE)
        parts.append(jnp.pad(a.reshape(-1).astype(F32), (0, r * LANE - a.size)).reshape(r, LANE))
        rows.append(r)
    return jnp.concatenate(parts, axis=0), rows


def _unpack(buf, rows, shapes):
    out, r0 = [], 0
    for r, s in zip(rows, shapes):
        size = math.prod(s)
        out.append(buf[r0:r0 + r].reshape(-1)[:size].reshape(s))
        r0 += r
    return out


def kernel(x, norm_ffn1, ffn1_w1, ffn1_w3, ffn1_w2, norm_mix, w_in, conv_w, conv_b, conv_ln_g, conv_ln_b, conv_out_g, ssm_A_re, ssm_A_im, ssm_log_dt, ssm_B_re, ssm_B_im, ssm_C_re, ssm_C_im, ssm_D, ssm_glu_w, ssm_glu_b, ssm_out_g, w_out, norm_ffn2, ffn2_w1, ffn2_w3, ffn2_w2, norm_final, loss_target, m_norm_ffn1, m_ffn1_w1, m_ffn1_w3, m_ffn1_w2, m_norm_mix, m_w_in, m_conv_w, m_conv_b, m_conv_ln_g, m_conv_ln_b, m_conv_out_g, m_ssm_A_re, m_ssm_A_im, m_ssm_log_dt, m_ssm_B_re, m_ssm_B_im, m_ssm_C_re, m_ssm_C_im, m_ssm_D, m_ssm_glu_w, m_ssm_glu_b, m_ssm_out_g, m_w_out, m_norm_ffn2, m_ffn2_w1, m_ffn2_w3, m_ffn2_w2, m_norm_final, v_norm_ffn1, v_ffn1_w1, v_ffn1_w3, v_ffn1_w2, v_norm_mix, v_w_in, v_conv_w, v_conv_b, v_conv_ln_g, v_conv_ln_b, v_conv_out_g, v_ssm_A_re, v_ssm_A_im, v_ssm_log_dt, v_ssm_B_re, v_ssm_B_im, v_ssm_C_re, v_ssm_C_im, v_ssm_D, v_ssm_glu_w, v_ssm_glu_b, v_ssm_out_g, v_w_out, v_norm_ffn2, v_ffn2_w1, v_ffn2_w3, v_ffn2_w2, v_norm_final):
    given = dict(locals())
    wts = {n: given[n] for n in WEIGHTS}
    n_seq, seq, d = x.shape
    n_rows = n_seq * seq
    xf = x.reshape(n_rows, d)
    tgt = loss_target.reshape(n_rows, d)
    row = lambda a: a.reshape(1, -1)

    f = ffn1_w1.shape[-1]
    fp = _round_up(f, LANE)
    held = lambda n, a: a[0].T if n in TRANSPOSED else a[0]
    shards = []
    for n in BIG:
        s = held(n, wts[n]).astype(BF16)
        if n.startswith('ffn'):
            s = jnp.pad(s, ((0, fp - f), (0, 0)))
        shards.append(s)
    n_taps, c_shard = conv_w.shape[1], conv_w.shape[2]
    shards.append(jnp.pad(conv_w[0], ((0, HALO - n_taps), (0, 0))))
    shard_of = dict(zip(BIG + ['conv_w'], shards))
    axis_of = dict(BIG_AXIS, conv_w=1)
    groups = [['ffn1_w1', 'ffn1_w3'], ['ffn1_w2', 'w_in', 'conv_w', 'ssm_glu_w', 'w_out'], ['ffn2_w1', 'ffn2_w3', 'ffn2_w2']]
    fetch, tok = [], []
    for k, names in enumerate(groups):
        fetch.append(_exchange_start("gather%d_send" % k, "gather", [shard_of[n] for n in names],
                                     [axis_of[n] for n in names], tok))
        tok = [fetch[-1][3]]
    full = {}

    def arrive(k, after):
        lands = _exchange_wait("gather%d_recv" % k, "gather", fetch[k], [axis_of[n] for n in groups[k]], after)
        full.update(zip(groups[k], lands))

    h1, h1_t = _rms_fwd("ffn1_rms", xf, norm_ffn1)
    arrive(0, tok + [h1])

    _, n_grp, n_state = ssm_A_re.shape
    grp = ssm_B_re.shape[-1]
    ns = n_grp * n_state
    c_ssm = n_grp * grp
    lr, li = ssm_A_re.reshape(1, ns), ssm_A_im.reshape(1, ns)
    ldt = jnp.repeat(ssm_log_dt.reshape(n_grp), n_state).reshape(1, ns)
    btr = ssm_B_re[0].transpose(2, 0, 1).reshape(grp, ns)
    bti = ssm_B_im[0].transpose(2, 0, 1).reshape(grp, ns)
    ctr = ssm_C_re[0].transpose(1, 0, 2).reshape(grp, ns)
    cti = ssm_C_im[0].transpose(1, 0, 2).reshape(grp, ns)
    scan_tile = _pick(seq, (SCAN_TILE,))
    _, _, bbr, bbi, seg_up, seg_down, pw, pw_falling = _s5_params_fwd(lr, li, ldt, btr, bti, scan_tile // SUBLANE)
    nb = c_ssm // LANE
    sb, gpb = ns // nb, n_grp // nb
    diag = (jnp.arange(LANE)[:, None] // grp) == (jnp.arange(sb)[None, :] // n_state)

    def spread(t):
        return jnp.where(diag, jnp.tile(t.reshape(grp, nb, sb).transpose(1, 0, 2), (1, gpb, 1)), 0.0)

    def gather_diag(t):
        return (t * diag).reshape(nb, gpb, grp, sb).sum(1).transpose(1, 0, 2).reshape(grp, ns)

    def interleave(re, im):
        return jnp.stack([re.reshape(-1, nb, sb), im.reshape(-1, nb, sb)], axis=2).reshape(-1, 2 * ns)

    bdc = jnp.concatenate([spread(bbr), spread(bbi)], axis=2).astype(BF16)
    cdc = jnp.concatenate([spread(ctr).transpose(0, 2, 1), -spread(cti).transpose(0, 2, 1)], axis=1).astype(BF16)
    rowi = jnp.arange(SUBLANE)[:, None]
    pwf, pwc = interleave(pw[:, :ns], pw[:, ns:]), interleave(pw[:, :ns], -pw[:, ns:])
    tabs_f = [jnp.where(rowi >= s, pwf[s - 1][None, :], 0.0) for s in (1, 2, 4)] + [pwf]
    tabs_b = [jnp.where(rowi <= SUBLANE - 1 - s, pwc[s - 1][None, :], 0.0) for s in (1, 2, 4)]
    tabs_b.append(interleave(pw_falling[:, :ns], -pw_falling[:, ns:]))
    fix_f = jnp.repeat(interleave(seg_up[:, :ns], seg_up[:, ns:]), SUBLANE, axis=0)
    fix_b = jnp.repeat(interleave(seg_down[:, :ns], -seg_down[:, ns:]), SUBLANE, axis=0)
    c_conv = conv_b.shape[1]
    u_blk = 2 * c_conv // LANE

    a1, b1, z1 = _ffn_up("ffn1_up", h1, full['ffn1_w1'], full['ffn1_w3'])
    arrive(1, [z1])
    x1, h2, h2_t = _mm("ffn1_down", z1, full['ffn1_w2'], 1, 0, addend=xf, alpha=0.5, post=_post_rms(norm_mix))
    saved1 = (h1_t, a1, b1, z1)
    cw = full['conv_w']
    proj = _mm("mix_in", h2, full['w_in'], 1, 0, F32)
    assert c_conv == c_ssm and proj.shape[1] == 3 * c_conv
    cpre, an = _conv_fwd(proj, cw, conv_b, conv_ln_g, conv_ln_b, conv_out_g, seq)
    xs, ypre, yg = _s5_fwd(proj, u_blk, bdc, cdc, fix_f, tabs_f, ssm_D, seq, sb)
    q0 = _mm("s5_gate", yg, full['ssm_glu_w'], 1, 0, F32)
    sn = _s5_post2(yg, q0, ssm_glu_b, ssm_out_g)
    wo = full['w_out']
    mixed = jnp.concatenate([an, sn], axis=1)
    x2, h3, h3_t = _mm("mix_out", mixed, wo, 1, 0, addend=x1, post=_post_rms(norm_ffn2))
    arrive(2, [x2])
    a3, b3, z3 = _ffn_up("ffn2_up", h3, full['ffn2_w1'], full['ffn2_w3'])
    saved2 = (h3_t, a3, b3, z3)
    dx3, dx3_t, loss_row, d_norm_final = _mm("ffn2_down", z3, full['ffn2_w2'], 1, 0, addend=x2, alpha=0.5,
                                             post=_post_loss(row(norm_final), tgt))

    g = {}
    dx2, g['norm_ffn2'], sent = _ffn_bwd("ffn2", x2, norm_ffn2, full['ffn2_w1'], full['ffn2_w3'], full['ffn2_w2'], saved2,
                                         dx3, dx3_t, early=False)
    dmixed = _mm("mix_dmixed", dx2, wo, 1, 1, F32)
    dwo = _mm("mix_dwo", mixed, dx2, 0, 0, BF16)
    dq, dyg1, g['ssm_out_g'], g['ssm_glu_b'] = _s5_post2_bwd(dmixed, yg, q0, ssm_glu_b, ssm_out_g)
    dyg2 = _mm("s5_dgate", dq, full['ssm_glu_w'], 1, 1, F32)
    dwg = _mm("s5_dwg", yg, dq, 0, 0, BF16)
    dypre, du_skip, g['ssm_D'] = _s5_post1_bwd(dyg1, dyg2, ypre, proj, ssm_D)
    du, dabar, dbdc, dcdc = _s5_bwd(dypre, du_skip, xs, proj, u_blk, bdc, cdc, fix_b, tabs_b, seq, sb)
    dabar = dabar.reshape(nb, 2, sb)
    dlr, dli, dldt, dbtr, dbti = _s5_params_bwd(lr, li, ldt, btr, bti, dabar[:, 0].reshape(1, ns), dabar[:, 1].reshape(1, ns),
                                                gather_diag(dbdc[:, :, :sb]), gather_diag(dbdc[:, :, sb:]))
    g['ssm_A_re'], g['ssm_A_im'] = dlr, dli
    g['ssm_log_dt'] = dldt.reshape(n_grp, n_state).sum(axis=1)
    g['ssm_B_re'] = dbtr.reshape(grp, n_grp, n_state).transpose(1, 2, 0)
    g['ssm_B_im'] = dbti.reshape(grp, n_grp, n_state).transpose(1, 2, 0)
    g['ssm_C_re'] = gather_diag(dcdc[:, :sb].transpose(0, 2, 1)).reshape(grp, n_grp, n_state).transpose(1, 0, 2)
    g['ssm_C_im'] = -gather_diag(dcdc[:, sb:].transpose(0, 2, 1)).reshape(grp, n_grp, n_state).transpose(1, 0, 2)
    dc, g['conv_out_g'], g['conv_ln_g'], g['conv_ln_b'], g['conv_b'] = _conv_bwd_rows(dmixed, cpre, conv_ln_g, conv_ln_b,
                                                                                    conv_out_g)
    dval, dgate, dcw = _conv_bwd_taps(proj, dc, cw, seq)
    dproj = jnp.concatenate([dval, dgate, du], axis=1)
    dwin = _mm("mix_dwin", h2_t, dproj, 1, 0, BF16)
    sent['w_out ssm_glu_w w_in'] = (_exchange_start("mix_send", "scatter", [dwo, dwg, dwin], [0, 0, 1]), [0, 0, 1])
    dx1, dx1_t, g['norm_mix'] = _mm("mix_dh", dproj, full['w_in'], 1, 1, after=[sent['w_out ssm_glu_w w_in'][0][3]],
                                    post=_post_rms_bwd(x1, norm_mix, dx2))
    dx0, g['norm_ffn1'], sent1 = _ffn_bwd("ffn1", xf, norm_ffn1, full['ffn1_w1'], full['ffn1_w3'], full['ffn1_w2'], saved1,
                                          dx1, dx1_t, early=True)
    sent.update(sent1)
    g['norm_final'] = d_norm_final
    g['conv_w'] = dcw[:n_taps]

    small_shapes = [(n_taps, c_conv) if n == 'conv_w' else wts[n].shape for n in SMALL]
    buf, buf_rows = _pack([g[n] for n in SMALL] + [loss_row])
    to_all = _exchange_start("small_send", "all", [buf], [0])
    slots = {}
    for names, (started, axes) in sent.items():
        lands = _exchange_wait(names.replace(' ', '_') + "_recv", "scatter", started, axes, after=[dx0, to_all[3]])
        slots.update(zip(names.split(), lands))
    sums = [_sum_slots("sum_" + n, slots[n]) for n in BIG]
    to_sibling = _exchange_start("sums_send", "sibling", sums, [0] * len(sums))
    from_all = _exchange_wait("small_recv", "all", to_all, [0], after=[to_sibling[3]])[0]
    total = _unpack(_sum_devices(from_all), buf_rows, small_shapes + [(1, LANE)])
    loss = total[-1][0, 0]
    grads = dict(zip(SMALL, total[:-1]))
    chip = 2 * lax.axis_index("x") + lax.axis_index("y")
    grads['conv_w'] = lax.dynamic_slice_in_dim(grads['conv_w'], chip * c_shard, c_shard, axis=1)[None]
    flat = lambda a: a.reshape(-1, a.shape[-1])
    small = _adamw_many("adamw_small", *[[flat(src[p + n]) for n in SMALL]
                                         for src, p in ((grads, ''), (given, ''), (given, 'm_'), (given, 'v_'))])
    deltas, new_m, new_v = ({n: o.reshape(wts[n].shape) for n, o in zip(SMALL, outs)} for outs in small)

    sums, theirs = _exchange_wait("sums_recv", "sibling", to_sibling, [0] * len(sums), after=[new_v[SMALL[-1]]],
                                  sources_too=True)
    for n, mine, other in zip(BIG, sums, theirs):
        grads[n], deltas[n], new_m[n], new_v[n] = (
            (o.T if n in TRANSPOSED else o)[None]
            for o in _adamw("adamw_" + n, [mine, other], held(n, given[n]), held(n, given['m_' + n]), held(n, given['v_' + n])))

    return (loss, dx0.reshape(x.shape), *[grads[n] for n in WEIGHTS], *[deltas[n] for n in WEIGHTS],
            *[new_m[n] for n in WEIGHTS], *[new_v[n] for n in WEIGHTS])
```

```python
import math
from typing import Callable, NamedTuple

import jax
import jax.numpy as jnp
from jax import lax
from jax.experimental import pallas as pl
from jax.experimental.pallas import tpu as pltpu

F32 = jnp.float32
BF16 = jnp.bfloat16
EPS = 1e-6
ADAM_LR, ADAM_B1, ADAM_B2, ADAM_EPS, ADAM_WD, ADAM_STEP = 0.001, 0.9, 0.999, 1e-08, 0.01, 10
MESH = pl.DeviceIdType.MESH
ANY = pl.BlockSpec(memory_space=pl.ANY)
LANE = 128
SUBLANE = 8
VMEM_LIMIT_BYTES = 56 << 20
ROW_TILE = 256
ROW_TILE_ELEMS = 256 * 1024
WHOLE_ELEMS = 512 * 1024
WHOLE_WEIGHT_BYTES = 8 << 20
FFN_ROWS = 256
CONV_TILE = 256
CONV_SUB = 32
HALO = 32
SCAN_TILE = 256
SCAN_COLS = 512
N_CHIPS = 4
CHIP_RELS = ((1, 0), (0, 1), (1, 1))
NT = (((1,), (1,)), ((), ()))
GELU_K = math.sqrt(2.0 / math.pi)
GELU_C = 0.044715

WEIGHTS = ['norm_ffn1', 'ffn1_w1', 'ffn1_w3', 'ffn1_w2', 'norm_mix', 'w_in', 'conv_w', 'conv_b', 'conv_ln_g', 'conv_ln_b',
           'conv_out_g', 'ssm_A_re', 'ssm_A_im', 'ssm_log_dt', 'ssm_B_re', 'ssm_B_im', 'ssm_C_re', 'ssm_C_im', 'ssm_D',
           'ssm_glu_w', 'ssm_glu_b', 'ssm_out_g', 'w_out', 'norm_ffn2', 'ffn2_w1', 'ffn2_w3', 'ffn2_w2', 'norm_final']
BIG = ['ffn1_w1', 'ffn1_w3', 'ffn1_w2', 'w_in', 'ssm_glu_w', 'w_out', 'ffn2_w1', 'ffn2_w3', 'ffn2_w2']
BIG_AXIS = {'ffn1_w1': 0, 'ffn1_w3': 0, 'ffn1_w2': 0, 'w_in': 1, 'ssm_glu_w': 0, 'w_out': 0, 'ffn2_w1': 0, 'ffn2_w3': 0,
            'ffn2_w2': 0}
TRANSPOSED = ('ffn1_w1', 'ffn1_w3', 'ffn2_w1', 'ffn2_w3')
SMALL = [n for n in WEIGHTS if n not in BIG]


def _round_up(n, m):
    return -(-n // m) * m


def _pick(n, cands):
    for c in cands:
        if c <= n and n % c == 0:
            return c
    return n


def _params(*sem):
    return pltpu.CompilerParams(dimension_semantics=sem, vmem_limit_bytes=VMEM_LIMIT_BYTES)


def _rms_r(x):
    return lax.rsqrt(jnp.mean(x * x, axis=-1, keepdims=True) + EPS)


def _rms_bwd(x, r, g, dy):
    dyg = dy * g
    return r * dyg - x * (r * r * r) * jnp.mean(x * dyg, axis=-1, keepdims=True)


def _sigmoid(x):
    return jax.nn.sigmoid(x)


def _dsilu(a, s):
    return s * (1.0 + a * (1.0 - s))


def _gelu(x):
    return 0.5 * x * (1.0 + jnp.tanh(GELU_K * (x + GELU_C * x * x * x)))


def _dgelu(x):
    t = jnp.tanh(GELU_K * (x + GELU_C * x * x * x))
    return 0.5 * (1.0 + t) + 0.5 * x * (1.0 - t * t) * GELU_K * (1.0 + 3.0 * GELU_C * x * x)


def _colsum(v):
    return jnp.sum(v, axis=0, keepdims=True)


def _rowwise(name, body, n_rows, row_ins, par_ins, row_outs, acc_outs, after=()):
    widest = max([w for (_, w, _) in row_ins] + [w for (w, _) in row_outs])
    tt = _pick(n_rows, [t for t in (256, 128, 64, 32, 16, 8) if t * widest <= ROW_TILE_ELEMS])
    in_specs = [pl.BlockSpec((tt, w), lambda i, cb=cb: (i, cb)) for (_, w, cb) in row_ins]
    in_specs += [pl.BlockSpec(p.shape, lambda i: (0, 0)) for p in par_ins] + [ANY] * len(after)
    out_specs = [pl.BlockSpec((tt, w), lambda i: (i, 0)) for (w, _) in row_outs]
    out_specs += [pl.BlockSpec((r, w), lambda i: (0, 0)) for (r, w) in acc_outs]
    out_shape = [jax.ShapeDtypeStruct((n_rows, w), dt) for (w, dt) in row_outs]
    out_shape += [jax.ShapeDtypeStruct((r, w), F32) for (r, w) in acc_outs]
    n_in, n_ro = len(row_ins) + len(par_ins), len(row_outs)
    o0 = n_in + len(after)

    def kern(*refs):
        accs = refs[o0 + n_ro:]
        if accs:
            @pl.when(pl.program_id(0) == 0)
            def _():
                for a in accs:
                    a[...] = jnp.zeros_like(a)
        body(refs[:n_in], refs[o0:o0 + n_ro], accs)

    return pl.pallas_call(kern, name=name, grid=(n_rows // tt,), in_specs=in_specs, out_specs=out_specs, out_shape=out_shape,
                          compiler_params=_params("arbitrary"))(*[a for a, _, _ in row_ins], *par_ins, *after)


class Post(NamedTuple):
    rows: list
    gains: list
    outs: list
    t_outs: list
    sums: list
    fn: Callable


def _post_rms(gain):
    def fn(r, rows, gains):
        h = r * _rms_r(r) * gains[0]
        return [r, h, h], []

    return Post([], [gain], [F32, BF16], [BF16], [], fn)


def _post_rms_bwd(x, gain, dres):
    def fn(dh, rows, gains):
        r = _rms_r(rows[0])
        dx = rows[1] + _rms_bwd(rows[0], r, gains[0], dh)
        return [dx, dx], [_colsum(dh * rows[0] * r)]

    return Post([x, dres], [gain], [F32], [BF16], [x.shape[1]], fn)


def _post_loss(gain, tgt):
    d = tgt.shape[1]

    def fn(xv, rows, gains):
        r = _rms_r(xv)
        e = xv * r * gains[0] - rows[0]
        sq = jnp.sum(jnp.sum(e * e, axis=-1, keepdims=True), axis=0, keepdims=True)
        dy = e * (1.0 / d)
        dx = _rms_bwd(xv, r, gains[0], dy)
        return [dx, dx], [jnp.broadcast_to(sq * (0.5 / d), (1, LANE)), _colsum(dy * xv * r)]

    return Post([tgt], [gain], [F32], [BF16], [LANE, d], fn)


def _mm(name, a, b, ca, cb, out_dtype=F32, addend=None, alpha=1.0, a_cols=None, after=(), post=None, transposed=False):
    a_start, a_width = a_cols if a_cols else (0, a.shape[1])
    m, k = (a.shape[0], a_width) if ca == 1 else (a_width, a.shape[0])
    n = b.shape[1 - cb]
    assert b.shape[cb] == k, (name, a.shape, b.shape)
    tn = _pick(n, (1024, 768, 512, 384, 256, 128))
    whole_b = bool(post) and k * tn * b.dtype.itemsize <= WHOLE_WEIGHT_BYTES
    if whole_b:
        tk = k
        tm = _pick(m, (512, 256, 128))
    else:
        tm = _pick(m, (512, 256, 128) if post else (1024, 512, 256, 128))
        tk = _pick(k, (2048, 1024, 768, 512, 256, 128) if k >= 4096 and not post else (1024, 768, 512, 256, 128))
    nk = k // tk
    if ca == 1:
        assert a_start % tk == 0
        a_spec = pl.BlockSpec((tm, tk), lambda i, j, kk: (i, kk + a_start // tk))
    else:
        assert a_start % tm == 0
        a_spec = pl.BlockSpec((tk, tm), lambda i, j, kk: (kk, i + a_start // tm))
    b_mode = dict(pipeline_mode=pl.Buffered(1)) if whole_b else {}
    b_spec = (pl.BlockSpec((tk, tn), lambda i, j, kk: (kk, j), **b_mode) if cb == 0 else
              pl.BlockSpec((tn, tk), lambda i, j, kk: (j, kk), **b_mode))
    o_spec = pl.BlockSpec((tm, tn), lambda i, j, kk: (i, j))
    t_spec = pl.BlockSpec((tn, tm), lambda i, j, kk: (j, i))
    fixed = lambda w: pl.BlockSpec((1, w), lambda i, j, kk: (0, 0))
    ins, in_specs = [a, b], [a_spec, b_spec]
    if addend is not None:
        ins.append(addend)
        in_specs.append(o_spec)
    n_plain = len(ins)
    n_rows, n_gains = (len(post.rows), len(post.gains)) if post else (0, 0)
    if post:
        assert tn == n, name
        ins += post.rows + post.gains
        in_specs += [o_spec] * n_rows + [fixed(n)] * n_gains
    ins += list(after)
    in_specs += [ANY] * len(after)
    n_in = len(ins)
    if post:
        n_straight, n_vals = len(post.outs), len(post.outs) + len(post.t_outs)
        out_specs = [o_spec] * n_straight + [t_spec] * len(post.t_outs) + [fixed(w) for w in post.sums]
        out_shape = [jax.ShapeDtypeStruct((m, n), dt) for dt in post.outs] + [jax.ShapeDtypeStruct((n, m), dt) for dt in post.t_outs]
        out_shape += [jax.ShapeDtypeStruct((1, w), F32) for w in post.sums]
    elif transposed:
        out_specs, out_shape = [t_spec], [jax.ShapeDtypeStruct((n, m), out_dtype)]
    else:
        out_specs, out_shape = [o_spec], [jax.ShapeDtypeStruct((m, n), out_dtype)]
    n_out = len(out_specs)
    dims = (((ca,), (cb,)), ((), ()))

    def emit(refs, r):
        if alpha != 1.0:
            r = r * alpha
        if addend is not None:
            r = r + refs[2][...].astype(F32)
        outs = refs[n_in:n_in + n_out]
        if post is None:
            outs[0][...] = (r.T if transposed else r).astype(out_dtype)
            return
        vals, incs = post.fn(r, [q[...] for q in refs[n_plain:n_plain + n_rows]],
                             [q[...] for q in refs[n_plain + n_rows:n_plain + n_rows + n_gains]])
        for at, (o_ref, val) in enumerate(zip(outs, vals)):
            o_ref[...] = (val if at < n_straight else val.T).astype(o_ref.dtype)
        for s_ref, inc in zip(outs[n_vals:], incs):
            s_ref[...] += inc

    def kern(*refs):
        kk = pl.program_id(2)
        if post and post.sums:
            @pl.when(jnp.logical_and(jnp.logical_and(pl.program_id(0) == 0, pl.program_id(1) == 0), kk == 0))
            def _():
                for s_ref in refs[n_in + n_vals:n_in + n_out]:
                    s_ref[...] = jnp.zeros_like(s_ref)

        dot = lambda: lax.dot_general(refs[0][...].astype(BF16), refs[1][...].astype(BF16), dims,
                                      preferred_element_type=F32)
        if nk == 1:
            emit(refs, dot())
            return
        acc_ref = refs[-1]

        @pl.when(kk == 0)
        def _():
            acc_ref[...] = jnp.zeros_like(acc_ref)

        acc_ref[...] += dot()

        @pl.when(kk == nk - 1)
        def _():
            emit(refs, acc_ref[...])

    res = pl.pallas_call(kern, name=name, grid=(m // tm, n // tn, nk), in_specs=in_specs, out_specs=out_specs,
                         out_shape=out_shape, scratch_shapes=[] if nk == 1 else [pltpu.VMEM((tm, tn), F32)],
                         compiler_params=_params("arbitrary", "arbitrary", "arbitrary"))(*ins)
    return res if post else res[0]


def _rms_fwd(name, x, g):
    t, d = x.shape
    tt = _pick(t, (ROW_TILE, LANE))

    def kern(x_ref, g_ref, h_ref, ht_ref):
        xv = x_ref[...]
        h = xv * _rms_r(xv) * g_ref[...]
        h_ref[...] = h.astype(BF16)
        ht_ref[...] = h.T.astype(BF16)

    return pl.pallas_call(kern, name=name, grid=(t // tt,),
                          in_specs=[pl.BlockSpec((tt, d), lambda i: (i, 0)), pl.BlockSpec((1, d), lambda i: (0, 0))],
                          out_specs=[pl.BlockSpec((tt, d), lambda i: (i, 0)), pl.BlockSpec((d, tt), lambda i: (0, i))],
                          out_shape=[jax.ShapeDtypeStruct((t, d), BF16), jax.ShapeDtypeStruct((d, t), BF16)],
                          compiler_params=_params("arbitrary"))(x, g)


def _ffn_up(name, h, w1, w3):
    t, d = h.shape
    ff = w1.shape[0]
    tm, tn = _pick(t, (1024, 512, 256, 128)), _pick(ff, (1024, 768, 512, 256, 128))

    def kern(h_ref, w1_ref, w3_ref, a_ref, b_ref, z_ref):
        hv = h_ref[...]
        a = lax.dot_general(hv, w1_ref[...], NT, preferred_element_type=F32)
        b = lax.dot_general(hv, w3_ref[...], NT, preferred_element_type=F32)
        a_ref[...] = a.astype(BF16)
        b_ref[...] = b.astype(BF16)
        z_ref[...] = (a * _sigmoid(a) * b).astype(BF16)

    w_spec = pl.BlockSpec((tn, d), lambda i, j: (j, 0))
    o_spec = pl.BlockSpec((tm, tn), lambda i, j: (i, j))
    return pl.pallas_call(kern, name=name, grid=(t // tm, ff // tn),
                          in_specs=[pl.BlockSpec((tm, d), lambda i, j: (i, 0)), w_spec, w_spec], out_specs=[o_spec] * 3,
                          out_shape=[jax.ShapeDtypeStruct((t, ff), BF16)] * 3,
                          compiler_params=_params("arbitrary", "arbitrary"))(h, w1, w3)


def _ffn_dglu(name, dxo, w2, a, b, after=()):
    t, d = dxo.shape
    ff = w2.shape[0]
    tm = _pick(t, (FFN_ROWS, 128))

    def kern(dx_ref, w2_ref, a_ref, b_ref, *rest):
        da_ref, db_ref = rest[-2:]
        dz = lax.dot_general(dx_ref[...].astype(BF16), w2_ref[...], NT, preferred_element_type=F32) * 0.5
        av, bv = a_ref[...].astype(F32), b_ref[...].astype(F32)
        s = _sigmoid(av)
        da_ref[...] = (dz * bv * _dsilu(av, s)).astype(BF16)
        db_ref[...] = (dz * av * s).astype(BF16)

    o_spec = pl.BlockSpec((tm, ff), lambda i: (i, 0))
    return pl.pallas_call(kern, name=name, grid=(t // tm,),
                          in_specs=[pl.BlockSpec((tm, d), lambda i: (i, 0)), pl.BlockSpec((ff, d), lambda i: (0, 0)),
                                    o_spec, o_spec] + [ANY] * len(after),
                          out_specs=[o_spec] * 2, out_shape=[jax.ShapeDtypeStruct((t, ff), BF16)] * 2,
                          compiler_params=_params("arbitrary"))(dxo, w2, a, b, *after)


def _ffn_dh(name, da, db, w1, w3, x, g, dres, after=()):
    t, d = x.shape
    ff = da.shape[1]
    tm = _pick(t, (2 * FFN_ROWS, 128))

    def kern(da_ref, db_ref, w1_ref, w3_ref, x_ref, g_ref, dres_ref, *rest):
        dx_ref, dg_ref = rest[-2:]

        @pl.when(pl.program_id(0) == 0)
        def _():
            dg_ref[...] = jnp.zeros_like(dg_ref)

        dh = (jnp.dot(da_ref[...], w1_ref[...], preferred_element_type=F32)
              + jnp.dot(db_ref[...], w3_ref[...], preferred_element_type=F32))
        xv = x_ref[...]
        r = _rms_r(xv)
        dx_ref[...] = dres_ref[...] + _rms_bwd(xv, r, g_ref[...], dh)
        dg_ref[...] += _colsum(dh * xv * r)

    act = pl.BlockSpec((tm, ff), lambda i: (i, 0))
    wgt = pl.BlockSpec((ff, d), lambda i: (0, 0), pipeline_mode=pl.Buffered(1))
    rows = pl.BlockSpec((tm, d), lambda i: (i, 0))
    gain = pl.BlockSpec((1, d), lambda i: (0, 0))
    return pl.pallas_call(kern, name=name, grid=(t // tm,),
                          in_specs=[act, act, wgt, wgt, rows, gain, rows] + [ANY] * len(after), out_specs=[rows, gain],
                          out_shape=[jax.ShapeDtypeStruct((t, d), F32), jax.ShapeDtypeStruct((1, d), F32)],
                          compiler_params=_params("arbitrary"))(da, db, w1, w3, x, g, dres, *after)


def _ffn_bwd(tag, x, g, w1, w3, w2, saved, dxo, dxo_t, early):
    ht, a, b, z = saved
    dw2 = _mm(tag + "_dw2", dxo_t, z, 1, 0, BF16, alpha=0.5, transposed=True)
    da, db = _ffn_dglu(tag + "_dglu", dxo, w2, a, b)
    dw1 = _mm(tag + "_dw1", ht, da, 1, 0, BF16, transposed=True)
    sent, pin = {}, []
    if early:
        sent[tag + "_w2 " + tag + "_w1"] = (_exchange_start(tag + "_w2_w1_send", "scatter", [dw2, dw1], [0, 0]), [0, 0])
        pin = [sent[tag + "_w2 " + tag + "_w1"][0][3]]
    dw3 = _mm(tag + "_dw3", ht, db, 1, 0, BF16, after=pin, transposed=True)
    last = [dw3] if early else [dw2, dw1, dw3]
    names = [tag + "_w3"] if early else [tag + "_w2", tag + "_w1", tag + "_w3"]
    sent[" ".join(names)] = (_exchange_start(tag + "_w3_send", "scatter", last, [0] * len(last)), [0] * len(last))
    dx, dg = _ffn_dh(tag + "_dh", da, db, w1, w3, x, g, dxo, after=[sent[" ".join(names)][0][3]])
    return dx, dg, sent


def _shift_copies(ext_ref, sh_ref):
    n = ext_ref.shape[0] - SUBLANE
    for r in range(1, SUBLANE):
        sh_ref[r, pl.ds(0, n), :] = ext_ref[pl.ds(r, n), :]


def _rows_at(ext_ref, sh_ref, off, rows):
    r = off % SUBLANE
    return ext_ref[pl.ds(off, rows), :] if r == 0 else sh_ref[r, pl.ds(off - r, rows), :]


def _conv_fwd(proj, cw, cb, lng, lnb, og, seq):
    n_rows, c = proj.shape[0], cb.shape[1]
    kw = HALO - 1
    tt = _pick(seq, (CONV_TILE,))
    hb = tt // HALO

    def kern(v_ref, g_ref, vp_ref, gp_ref, w_ref, cb_ref, lg_ref, lb_ref, og_ref, c_ref, an_ref, ext_ref, sh_ref):
        first = (pl.program_id(0) * tt) % seq == 0
        ext_ref[pl.ds(HALO, tt), :] = v_ref[...] * _sigmoid(g_ref[...])
        ext_ref[pl.ds(0, HALO), :] = vp_ref[...] * _sigmoid(gp_ref[...]) * jnp.where(first, 0.0, 1.0)
        _shift_copies(ext_ref, sh_ref)
        for r0 in range(0, tt, CONV_SUB):
            rows = min(CONV_SUB, tt - r0)
            acc = jnp.zeros((rows, c), F32)
            for k in range(kw):
                acc = acc + w_ref[pl.ds(k, 1), :] * _rows_at(ext_ref, sh_ref, r0 + HALO - (kw - 1) + k, rows)
            c_ref[pl.ds(r0, rows), :] = acc + cb_ref[...]
        cv = c_ref[...]
        mu = jnp.mean(cv, axis=-1, keepdims=True)
        xc = cv - mu
        rstd = lax.rsqrt(jnp.mean(xc * xc, axis=-1, keepdims=True) + EPS)
        lv = xc * rstd * lg_ref[...] + lb_ref[...]
        sl = lv * _sigmoid(lv)
        an_ref[...] = (sl * _rms_r(sl) * og_ref[...]).astype(BF16)

    cur = lambda cbk: pl.BlockSpec((tt, c), lambda i: (i, cbk))
    prev = lambda cbk: pl.BlockSpec((HALO, c), lambda i: (jnp.maximum(i * hb - 1, 0), cbk))
    par = lambda p: pl.BlockSpec(p.shape, lambda i: (0, 0))
    return pl.pallas_call(
        kern, name="conv_fwd", grid=(n_rows // tt,),
        in_specs=[cur(0), cur(1), prev(0), prev(1), par(cw), par(cb), par(lng), par(lnb), par(og)],
        out_specs=[pl.BlockSpec((tt, c), lambda i: (i, 0))] * 2,
        out_shape=[jax.ShapeDtypeStruct((n_rows, c), F32), jax.ShapeDtypeStruct((n_rows, c), BF16)],
        scratch_shapes=[pltpu.VMEM((tt + HALO, c), F32), pltpu.VMEM((SUBLANE, tt + HALO, c), F32)],
        compiler_params=_params("arbitrary"),
    )(proj, proj, proj, proj, cw, cb, lng, lnb, og)


def _conv_bwd_rows(dmixed, cpre, lng, lnb, og):
    c = cpre.shape[1]

    def body(ins, outs, accs):
        dan, cv, lg, lb, ogv = ins[0][...], ins[1][...], ins[2][...], ins[3][...], ins[4][...]
        mu = jnp.mean(cv, axis=-1, keepdims=True)
        xc = cv - mu
        rstd = lax.rsqrt(jnp.mean(xc * xc, axis=-1, keepdims=True) + EPS)
        xh = xc * rstd
        lv = xh * lg + lb
        s = _sigmoid(lv)
        sl = lv * s
        r2 = _rms_r(sl)
        accs[0][...] += _colsum(dan * sl * r2)
        dl = _rms_bwd(sl, r2, ogv, dan) * _dsilu(lv, s)
        accs[1][...] += _colsum(dl * xh)
        accs[2][...] += _colsum(dl)
        dxh = dl * lg
        dc = rstd * (dxh - jnp.mean(dxh, axis=-1, keepdims=True) - xh * jnp.mean(dxh * xh, axis=-1, keepdims=True))
        outs[0][...] = dc
        accs[3][...] += _colsum(dc)

    return _rowwise("conv_bwd_rows", body, cpre.shape[0], [(dmixed, c, 0), (cpre, c, 0)], [lng, lnb, og], [(c, F32)],
                    [(1, c)] * 4)


def _conv_bwd_taps(proj, dc, cw, seq):
    n_rows, c = dc.shape
    kw = HALO - 1
    tt = _pick(seq, (CONV_TILE,))
    hb = tt // HALO
    last_blk = n_rows // HALO - 1

    def kern(v_ref, g_ref, vp_ref, gp_ref, dc_ref, dn_ref, w_ref, dv_ref, dg_ref, dw_ref, exta_ref, extd_ref, sha_ref, shd_ref):
        i = pl.program_id(0)
        first = (i * tt) % seq == 0
        last = ((i + 1) * tt) % seq == 0

        @pl.when(i == 0)
        def _():
            dw_ref[...] = jnp.zeros_like(dw_ref)

        sg = _sigmoid(g_ref[...])
        exta_ref[pl.ds(HALO, tt), :] = v_ref[...] * sg
        exta_ref[pl.ds(0, HALO), :] = vp_ref[...] * _sigmoid(gp_ref[...]) * jnp.where(first, 0.0, 1.0)
        extd_ref[pl.ds(0, tt), :] = dc_ref[...]
        extd_ref[pl.ds(tt, HALO), :] = dn_ref[...] * jnp.where(last, 0.0, 1.0)
        _shift_copies(exta_ref, sha_ref)
        _shift_copies(extd_ref, shd_ref)
        for k0 in range(0, kw, SUBLANE):
            taps = range(k0, min(k0 + SUBLANE, kw))
            sums = [jnp.zeros((SUBLANE, c), F32) for _ in taps]
            for r0 in range(0, tt, SUBLANE):
                dcb = dc_ref[pl.ds(r0, SUBLANE), :]
                for n, k in enumerate(taps):
                    sums[n] = sums[n] + _rows_at(exta_ref, sha_ref, r0 + HALO - (kw - 1) + k, SUBLANE) * dcb
            for n, k in enumerate(taps):
                dw_ref[pl.ds(k, 1), :] += _colsum(sums[n])
        for r0 in range(0, tt, CONV_SUB):
            rows = min(CONV_SUB, tt - r0)
            acc = jnp.zeros((rows, c), F32)
            for k in range(kw):
                acc = acc + w_ref[pl.ds(k, 1), :] * _rows_at(extd_ref, shd_ref, r0 + (kw - 1) - k, rows)
            dv_ref[pl.ds(r0, rows), :] = acc
        da = dv_ref[...]
        dv_ref[...] = da * sg
        dg_ref[...] = da * v_ref[...] * sg * (1.0 - sg)

    cur = lambda cbk: pl.BlockSpec((tt, c), lambda i: (i, cbk))
    prev = lambda cbk: pl.BlockSpec((HALO, c), lambda i: (jnp.maximum(i * hb - 1, 0), cbk))
    nxt = pl.BlockSpec((HALO, c), lambda i: (jnp.minimum((i + 1) * hb, last_blk), 0))
    return pl.pallas_call(
        kern, name="conv_bwd_taps", grid=(n_rows // tt,),
        in_specs=[cur(0), cur(1), prev(0), prev(1), cur(0), nxt, pl.BlockSpec(cw.shape, lambda i: (0, 0))],
        out_specs=[cur(0), cur(0), pl.BlockSpec((HALO, c), lambda i: (0, 0))],
        out_shape=[jax.ShapeDtypeStruct((n_rows, c), F32), jax.ShapeDtypeStruct((n_rows, c), F32),
                   jax.ShapeDtypeStruct((HALO, c), F32)],
        scratch_shapes=[pltpu.VMEM((tt + HALO, c), F32)] * 2 + [pltpu.VMEM((SUBLANE, tt + HALO, c), F32)] * 2,
        compiler_params=_params("arbitrary"),
    )(proj, proj, proj, proj, dc, dc, cw)


def _s5_params_fwd(lr, li, ldt, btr, bti, seg):
    ns = lr.shape[1]

    def kern(lr_ref, li_ref, ldt_ref, btr_ref, bti_ref, ar_ref, ai_ref, bbr_ref, bbi_ref, ps_ref, psf_ref, pc_ref, pcf_ref):
        lrv, liv = lr_ref[...], li_ref[...]
        dt = jnp.exp(ldt_ref[...])
        zr, zi = lrv * dt, liv * dt
        mag = jnp.exp(zr)
        ar, ai = mag * jnp.cos(zi), mag * jnp.sin(zi)
        den = lrv * lrv + liv * liv
        nr = ar - 1.0
        cr = (nr * lrv + ai * liv) / den
        ci = (ai * lrv - nr * liv) / den
        ar_ref[...] = ar
        ai_ref[...] = ai
        bbr_ref[...] = cr * btr_ref[...] - ci * bti_ref[...]
        bbi_ref[...] = cr * bti_ref[...] + ci * btr_ref[...]
        def powers(br, bi, count, up_ref, down_ref):
            pr, pi = br, bi
            for e in range(count):
                for ref, at in ((up_ref, e), (down_ref, count - 1 - e)):
                    ref[pl.ds(at, 1), pl.ds(0, ns)] = pr
                    ref[pl.ds(at, 1), pl.ds(ns, ns)] = pi
                if e < count - 1:
                    pr, pi = pr * br - pi * bi, pr * bi + pi * br
            return pr, pi

        powers(*powers(ar, ai, seg, ps_ref, psf_ref), SUBLANE, pc_ref, pcf_ref)

    h = btr.shape[0]
    shapes = [jax.ShapeDtypeStruct((1, ns), F32)] * 2 + [jax.ShapeDtypeStruct((h, ns), F32)] * 2
    shapes += [jax.ShapeDtypeStruct((seg, 2 * ns), F32)] * 2 + [jax.ShapeDtypeStruct((SUBLANE, 2 * ns), F32)] * 2
    return pl.pallas_call(kern, name="s5_params_fwd", out_shape=shapes)(lr, li, ldt, btr, bti)


def _s5_params_bwd(lr, li, ldt, btr, bti, dar, dai, dbbr, dbbi):
    def kern(lr_ref, li_ref, ldt_ref, btr_ref, bti_ref, dar_ref, dai_ref, dbr_ref, dbi_ref,
             dlr_ref, dli_ref, dldt_ref, dbtr_ref, dbti_ref):
        lrv, liv = lr_ref[...], li_ref[...]
        dt = jnp.exp(ldt_ref[...])
        zr, zi = lrv * dt, liv * dt
        mag = jnp.exp(zr)
        ar, ai = mag * jnp.cos(zi), mag * jnp.sin(zi)
        den = lrv * lrv + liv * liv
        nr = ar - 1.0
        cr = (nr * lrv + ai * liv) / den
        ci = (ai * lrv - nr * liv) / den
        dbr, dbi, br, bi = dbr_ref[...], dbi_ref[...], btr_ref[...], bti_ref[...]
        dbtr_ref[...] = cr * dbr + ci * dbi
        dbti_ref[...] = cr * dbi - ci * dbr
        dcr = _colsum(br * dbr + bi * dbi)
        dci = _colsum(br * dbi - bi * dbr)
        ir, ii = lrv / den, -liv / den
        dnr = ir * dcr + ii * dci
        dni = ir * dci - ii * dcr
        wr, wi = cr * ir - ci * ii, cr * ii + ci * ir
        dl1r = -(wr * dcr + wi * dci)
        dl1i = -(wr * dci - wi * dcr)
        dtr, dti = dar_ref[...] + dnr, dai_ref[...] + dni
        dzr = ar * dtr + ai * dti
        dzi = ar * dti - ai * dtr
        dlr_ref[...] = dl1r + dt * dzr
        dli_ref[...] = dl1i + dt * dzi
        dldt_ref[...] = (dzr * lrv + dzi * liv) * dt

    ns, h = lr.shape[1], btr.shape[0]
    shapes = [jax.ShapeDtypeStruct((1, ns), F32)] * 3 + [jax.ShapeDtypeStruct((h, ns), F32)] * 2
    return pl.pallas_call(kern, name="s5_params_bwd", out_shape=shapes)(lr, li, ldt, btr, bti, dar, dai, dbbr, dbbi)


def _to_segments(nat_ref, seg_ref):
    steps = nat_ref.shape[0] // SUBLANE
    _regroup(nat_ref, seg_ref, lambda r: (r % SUBLANE) * steps + r // SUBLANE)


def _from_segments(seg_ref, nat_ref):
    steps = nat_ref.shape[0] // SUBLANE
    _regroup(seg_ref, nat_ref, lambda r: (r % steps) * SUBLANE + r // steps)


def _regroup(src_ref, dst_ref, src_row):
    rows, width = dst_ref.shape
    sublane = lax.broadcasted_iota(jnp.int32, (SUBLANE, width), 0)
    for r0 in range(0, rows, SUBLANE):
        tile = jnp.broadcast_to(src_ref[pl.ds(src_row(r0), 1), :], (SUBLANE, width))
        for k in range(1, SUBLANE):
            tile = jnp.where(sublane == k, src_ref[pl.ds(src_row(r0 + k), 1), :], tile)
        dst_ref[pl.ds(r0, SUBLANE), :] = tile


def _scan_tile(s_ref, o_ref, fix_ref, tabs, car_ref, sb, reverse, x_ref=None, acc_ref=None):
    l1, l2, l4, pw = tabs
    rows_t, w = s_ref.shape
    steps = rows_t // SUBLANE
    cw = _pick(sb, (SCAN_COLS,))
    last = 0 if reverse else SUBLANE - 1
    first = SUBLANE - 1 - last
    row = lax.broadcasted_iota(jnp.int32, (SUBLANE, cw), 0)
    step_rows = lambda i: pl.ds(pl.multiple_of(((steps - 1 - i) if reverse else i) * SUBLANE, SUBLANE), SUBLANE)
    zero = jnp.zeros((SUBLANE, cw), F32)

    for c0 in [b0 + o for b0 in range(0, w, 2 * sb) for o in range(0, sb, cw)]:
        cr, ci = pl.ds(c0, cw), pl.ds(c0 + sb, cw)
        base = pl.ds(((steps - 1) if reverse else 0) * SUBLANE, SUBLANE)
        ar, ai = fix_ref[base, cr], fix_ref[base, ci]

        def run(i, state):
            xr, xi = state
            rows = step_rows(i)
            xr, xi = ar * xr - ai * xi + s_ref[rows, cr], ar * xi + ai * xr + s_ref[rows, ci]
            o_ref[rows, cr] = xr
            o_ref[rows, ci] = xi
            return xr, xi

        fr, fi = lax.fori_loop(0, steps, run, (zero, zero))
        for s, lt in ((1, l1), (2, l2), (4, l4)):
            sh = (SUBLANE - s) if reverse else s
            sr, si = pltpu.roll(fr, sh, 0), pltpu.roll(fi, sh, 0)
            tr, ti = lt[:, cr], lt[:, ci]
            fr, fi = fr + tr * sr - ti * si, fi + tr * si + ti * sr
        kr, ki = car_ref[pl.ds(last, 1), cr], car_ref[pl.ds(last, 1), ci]
        pr, pi = pw[:, cr], pw[:, ci]
        fr, fi = fr + pr * kr - pi * ki, fi + pr * ki + pi * kr
        car_ref[:, cr] = fr
        car_ref[:, ci] = fi
        to_next = 1 if not reverse else SUBLANE - 1
        gr = jnp.where(row == first, kr, pltpu.roll(fr, to_next, 0))
        gi = jnp.where(row == first, ki, pltpu.roll(fi, to_next, 0))

        def fix(i, state):
            rows = step_rows(i)
            qr, qi = fix_ref[rows, cr], fix_ref[rows, ci]
            yr = o_ref[rows, cr] + qr * gr - qi * gi
            yi = o_ref[rows, ci] + qr * gi + qi * gr
            o_ref[rows, cr] = yr
            o_ref[rows, ci] = yi
            if acc_ref is None:
                return state
            nr, ni, sr, si = state
            pxr, pxi = x_ref[rows, cr], x_ref[rows, ci]
            return yr, yi, sr + nr * pxr + ni * pxi, si + ni * pxr - nr * pxi

        if acc_ref is None:
            lax.fori_loop(0, steps, fix, 0)
        else:
            _, _, sr, si = lax.fori_loop(0, steps, fix, (gr, gi, zero, zero))
            acc_ref[:, cr] += sr
            acc_ref[:, ci] += si


def _s5_fwd(proj, u_blk, bdc, cdc, fix, tabs, dskip, seq, sb):
    n_rows = proj.shape[0]
    nb, blk, w_blk = bdc.shape
    c, w = nb * blk, nb * w_blk
    tt = fix.shape[0]

    def kern(u_ref, bd_ref, cd_ref, fix_ref, l1, l2, l4, pw, d_ref, xs_ref, yp_ref, yg_ref, us_ref, bu_ref, car_ref):
        @pl.when((pl.program_id(0) * tt) % seq == 0)
        def _():
            car_ref[...] = jnp.zeros_like(car_ref)

        _to_segments(u_ref, us_ref)
        for j in range(nb):
            bu_ref[:, pl.ds(j * w_blk, w_blk)] = jnp.dot(us_ref[:, pl.ds(j * blk, blk)].astype(BF16), bd_ref[j],
                                                         preferred_element_type=F32)
        _scan_tile(bu_ref, xs_ref, fix_ref, (l1, l2, l4, pw), car_ref, sb, False)
        for j in range(nb):
            cols = pl.ds(j * blk, blk)
            y0 = jnp.dot(xs_ref[:, pl.ds(j * w_blk, w_blk)].astype(BF16), cd_ref[j], preferred_element_type=F32)
            us_ref[:, cols] = y0 + d_ref[:, cols] * us_ref[:, cols]
        _from_segments(us_ref, yp_ref)
        yg_ref[...] = _gelu(yp_ref[...]).astype(BF16)

    tab = pl.BlockSpec((SUBLANE, w), lambda i: (0, 0))
    rows = pl.BlockSpec((tt, c), lambda i: (i, 0))
    return pl.pallas_call(
        kern, name="s5_fwd", grid=(n_rows // tt,),
        in_specs=[pl.BlockSpec((tt, c), lambda i: (i, u_blk * blk // c)), pl.BlockSpec(bdc.shape, lambda i: (0, 0, 0)),
                  pl.BlockSpec(cdc.shape, lambda i: (0, 0, 0)), pl.BlockSpec((tt, w), lambda i: (0, 0)), tab, tab, tab, tab,
                  pl.BlockSpec((1, c), lambda i: (0, 0))],
        out_specs=[pl.BlockSpec((tt, w), lambda i: (i, 0)), rows, rows],
        out_shape=[jax.ShapeDtypeStruct((n_rows, w), F32), jax.ShapeDtypeStruct((n_rows, c), F32),
                   jax.ShapeDtypeStruct((n_rows, c), BF16)],
        scratch_shapes=[pltpu.VMEM((tt, c), F32), pltpu.VMEM((tt, w), F32), pltpu.VMEM((SUBLANE, w), F32)],
        compiler_params=_params("arbitrary"))(proj, bdc, cdc, fix, *tabs, dskip)


def _s5_bwd(dypre, du_skip, xs, proj, u_blk, bdc, cdc, fix, tabs, seq, sb):
    n_rows = proj.shape[0]
    nb, blk, w_blk = bdc.shape
    c, w = nb * blk, nb * w_blk
    tt = fix.shape[0]
    nt = n_rows // tt
    tn = (((0,), (0,)), ((), ()))

    def kern(dy_ref, ds_ref, x_ref, u_ref, bd_ref, cd_ref, fix_ref, l1, l2, l4, pw, du_ref, da_ref, db_ref, dc_ref,
             dys_ref, us_ref, dus_ref, gx_ref, lam_ref, car_ref, acc_ref):
        i = pl.program_id(0)

        @pl.when(((nt - i) * tt) % seq == 0)
        def _():
            car_ref[...] = jnp.zeros_like(car_ref)

        @pl.when(i == 0)
        def _():
            acc_ref[...] = jnp.zeros_like(acc_ref)
            db_ref[...] = jnp.zeros_like(db_ref)
            dc_ref[...] = jnp.zeros_like(dc_ref)

        _to_segments(dy_ref, dys_ref)
        _to_segments(u_ref, us_ref)
        for j in range(nb):
            gx_ref[:, pl.ds(j * w_blk, w_blk)] = lax.dot_general(dys_ref[:, pl.ds(j * blk, blk)].astype(BF16), cd_ref[j], NT,
                                                                 preferred_element_type=F32)
        _scan_tile(gx_ref, lam_ref, fix_ref, (l1, l2, l4, pw), car_ref, sb, True, x_ref, acc_ref)
        for j in range(nb):
            cols, wide = pl.ds(j * blk, blk), pl.ds(j * w_blk, w_blk)
            lam = lam_ref[:, wide].astype(BF16)
            dus_ref[:, cols] = lax.dot_general(lam, bd_ref[j], NT, preferred_element_type=F32)
            db_ref[j] += lax.dot_general(us_ref[:, cols].astype(BF16), lam, tn, preferred_element_type=F32)
            dc_ref[j] += lax.dot_general(x_ref[:, wide].astype(BF16), dys_ref[:, cols].astype(BF16), tn,
                                         preferred_element_type=F32)
        _from_segments(dus_ref, du_ref)
        du_ref[...] += ds_ref[...]

        @pl.when(i == nt - 1)
        def _():
            da_ref[...] = _colsum(acc_ref[...])

    back = lambda i: (nt - 1 - i, 0)
    tab = pl.BlockSpec((SUBLANE, w), lambda i: (0, 0))
    rows = pl.BlockSpec((tt, c), back)
    whole = lambda a: pl.BlockSpec(a.shape, lambda i: (0, 0, 0))
    return pl.pallas_call(
        kern, name="s5_bwd", grid=(nt,),
        in_specs=[rows, rows, pl.BlockSpec((tt, w), back), pl.BlockSpec((tt, c), lambda i: (nt - 1 - i, u_blk * blk // c)),
                  whole(bdc), whole(cdc), pl.BlockSpec((tt, w), lambda i: (0, 0)), tab, tab, tab, tab],
        out_specs=[rows, pl.BlockSpec((1, w), lambda i: (0, 0)), whole(bdc), whole(cdc)],
        out_shape=[jax.ShapeDtypeStruct((n_rows, c), F32), jax.ShapeDtypeStruct((1, w), F32),
                   jax.ShapeDtypeStruct(bdc.shape, F32), jax.ShapeDtypeStruct(cdc.shape, F32)],
        scratch_shapes=[pltpu.VMEM((tt, c), F32)] * 3 + [pltpu.VMEM((tt, w), F32)] * 2 + [pltpu.VMEM((SUBLANE, w), F32)] * 2,
        compiler_params=_params("arbitrary"))(dypre, du_skip, xs, proj, bdc, cdc, fix, *tabs)


def _s5_post2(yg, q0, bg, og):
    c = yg.shape[1]

    def body(ins, outs, accs):
        ygv = ins[0][...].astype(F32)
        sg = ygv * _sigmoid(ins[1][...] + ins[2][...])
        outs[0][...] = (sg * _rms_r(sg) * ins[3][...]).astype(BF16)

    return _rowwise("s5_post2", body, yg.shape[0], [(yg, c, 0), (q0, c, 0)], [bg, og], [(c, BF16)], [])[0]


def _s5_post2_bwd(dmixed, yg, q0, bg, og):
    c = yg.shape[1]

    def body(ins, outs, accs):
        dsn, ygv = ins[0][...], ins[1][...].astype(F32)
        s = _sigmoid(ins[2][...] + ins[3][...])
        sg = ygv * s
        r = _rms_r(sg)
        accs[0][...] += _colsum(dsn * sg * r)
        dsg = _rms_bwd(sg, r, ins[4][...], dsn)
        dq = dsg * ygv * s * (1.0 - s)
        outs[0][...] = dq.astype(BF16)
        outs[1][...] = dsg * s
        accs[1][...] += _colsum(dq)

    return _rowwise("s5_post2_bwd", body, yg.shape[0], [(dmixed, c, 1), (yg, c, 0), (q0, c, 0)], [bg, og],
                    [(c, BF16), (c, F32)], [(1, c)] * 2)


def _s5_post1_bwd(dyg1, dyg2, ypre, proj, dskip, after=()):
    c = ypre.shape[1]

    def body(ins, outs, accs):
        dyp = (ins[0][...] + ins[1][...]) * _dgelu(ins[2][...])
        outs[0][...] = dyp
        outs[1][...] = dyp * ins[4][...]
        accs[0][...] += _colsum(dyp * ins[3][...])

    return _rowwise("s5_post1_bwd", body, ypre.shape[0], [(dyg1, c, 0), (dyg2, c, 0), (ypre, c, 0), (proj, c, 2)], [dskip],
                    [(c, F32), (c, F32)], [(1, c)], after=after)


def _place():
    return lax.axis_index("x"), lax.axis_index("y"), lax.axis_index("c")


def _window(ref, axis, q, rows, cols):
    if axis == 0:
        return ref.at[pl.ds(pl.multiple_of(q * rows, SUBLANE), rows), :]
    return ref.at[:, pl.ds(pl.multiple_of(q * cols, LANE), cols)]


ALL_RELS = [(fx, fy, fc) for fx in (0, 1) for fy in (0, 1) for fc in (0, 1)][1:]
N_PEERS = {"gather": 3, "scatter": 3, "sibling": 1, "all": len(ALL_RELS)}


def _copies(kind, srcs, lands, shards, axes, send_sems, recv_sems, local_sems):
    x, y, c = _place()
    me, dev = 2 * x + y, 4 * x + 2 * y + c
    n_peers = N_PEERS[kind]
    starts, waits = [], []
    for a, (src, land) in enumerate(zip(srcs, lands)):
        on = lambda k, peer: dict(send_sem=send_sems.at[n_peers * a + k], recv_sem=recv_sems.at[n_peers * a + k],
                                  device_id=peer, device_id_type=MESH)
        if kind == "sibling":
            cp = pltpu.make_async_remote_copy(src_ref=src, dst_ref=land, **on(0, (x, y, 1 - c)))
            starts.append(cp)
            waits.append(cp)
            continue
        if kind == "all":
            own = pltpu.make_async_copy(src, land.at[dev], local_sems.at[a])
            starts.append(own)
            waits.append(own)
            for k, (fx, fy, fc) in enumerate(ALL_RELS):
                px, py, pc = (1 - x) if fx else x, (1 - y) if fy else y, (1 - c) if fc else c
                starts.append(pltpu.make_async_remote_copy(src_ref=src, dst_ref=land.at[dev], **on(k, (px, py, pc))))
                waits.append(pltpu.make_async_remote_copy(src_ref=src, dst_ref=land.at[4 * px + 2 * py + pc],
                                                          **on(k, (px, py, pc))))
            continue
        rows, cols = shards[a]
        if kind == "gather":
            own = pltpu.make_async_copy(src, _window(land, axes[a], me, rows, cols), local_sems.at[a])
        else:
            own = pltpu.make_async_copy(_window(src, axes[a], me, rows, cols), land.at[3], local_sems.at[a])
        starts.append(own)
        waits.append(own)
        for j, (fx, fy) in enumerate(CHIP_RELS):
            px, py = (1 - x) if fx else x, (1 - y) if fy else y
            peer = 2 * px + py
            if kind == "gather":
                starts.append(pltpu.make_async_remote_copy(src_ref=src, dst_ref=_window(land, axes[a], me, rows, cols),
                                                           **on(j, (px, py, c))))
                waits.append(pltpu.make_async_remote_copy(src_ref=src, dst_ref=_window(land, axes[a], peer, rows, cols),
                                                          **on(j, (px, py, c))))
            else:
                cp = pltpu.make_async_remote_copy(src_ref=_window(src, axes[a], peer, rows, cols), dst_ref=land.at[j],
                                                  **on(j, (px, py, c)))
                starts.append(cp)
                waits.append(cp)
    return starts, waits


HBM = pl.BlockSpec(memory_space=pltpu.HBM)
SEM = pl.BlockSpec(memory_space=pltpu.SEMAPHORE)


def _shard_shapes(kind, arrs, axes):
    if kind != "scatter":
        return [a.shape for a in arrs]
    return [(a.shape[0] // N_CHIPS, a.shape[1]) if ax == 0 else (a.shape[0], a.shape[1] // N_CHIPS) for a, ax in zip(arrs, axes)]


def _land_shapes(kind, arrs, axes):
    if kind == "gather":
        return [(N_CHIPS * a.shape[0], a.shape[1]) if ax == 0 else (a.shape[0], N_CHIPS * a.shape[1]) for a, ax in zip(arrs, axes)]
    if kind == "scatter":
        return [(N_CHIPS,) + s for s in _shard_shapes(kind, arrs, axes)]
    return [a.shape if kind == "sibling" else (len(ALL_RELS) + 1,) + a.shape for a in arrs]


def _exchange_start(name, kind, arrs, axes, after=()):
    n, n_after = len(arrs), len(after)
    shards = _shard_shapes(kind, arrs, axes)
    land_shapes = _land_shapes(kind, arrs, axes)
    lands = [lax.empty(s, a.dtype) for s, a in zip(land_shapes, arrs)]

    def kern(*refs):
        outs = refs[2 * n + n_after:]
        starts, _ = _copies(kind, refs[:n], refs[n:2 * n], shards, axes, outs[0], outs[1], outs[2])
        for cp in starts:
            cp.start()
        outs[-1][...] = jnp.zeros_like(outs[-1])

    kept = [pltpu.HBM(a.shape, a.dtype) for a in arrs] + [pltpu.HBM(s, a.dtype) for s, a in zip(land_shapes, arrs)]
    n_sems = N_PEERS[kind] * n
    res = pl.pallas_call(
        kern, name=name, in_specs=[HBM] * (2 * n) + [ANY] * n_after,
        out_specs=[SEM] * 3 + [HBM] * (2 * n) + [pl.BlockSpec(memory_space=pltpu.VMEM)],
        out_shape=[pltpu.SemaphoreType.DMA((n_sems,)), pltpu.SemaphoreType.DMA((n_sems,)), pltpu.SemaphoreType.DMA((n,))]
        + kept + [jax.ShapeDtypeStruct((SUBLANE, LANE), F32)],
        input_output_aliases={i: 3 + i for i in range(2 * n)},
        compiler_params=pltpu.CompilerParams(has_side_effects=pltpu.SideEffectType.DATAFLOW_SIDE_EFFECTING),
    )(*[pltpu.with_memory_space_constraint(a, pltpu.HBM) for a in list(arrs) + lands], *after)
    return res[:3], res[3:3 + n], res[3 + n:3 + 2 * n], res[-1]


def _exchange_wait(name, kind, started, axes, after, sources_too=False):
    sems, srcs, lands, _ = started
    n, n_after = len(srcs), len(after)
    shards = _shard_shapes(kind, srcs, axes)

    def kern(*refs):
        sem_refs = refs[2 * n:2 * n + 3]
        _, waits = _copies(kind, refs[:n], refs[n:2 * n], shards, axes, *sem_refs)
        for cp in waits:
            cp.wait()

    res = pl.pallas_call(
        kern, name=name, in_specs=[HBM] * (2 * n) + [SEM] * 3 + [ANY] * n_after, out_specs=[HBM] * (2 * n),
        out_shape=[pltpu.HBM(a.shape, a.dtype) for a in list(srcs) + list(lands)],
        input_output_aliases={i: i for i in range(2 * n)},
        compiler_params=pltpu.CompilerParams(has_side_effects=pltpu.SideEffectType.DATAFLOW_SIDE_EFFECTING),
    )(*srcs, *lands, *sems, *after)
    return (res[:n], res[n:]) if sources_too else res[n:]


def _sum_devices(parts):
    def kern(p_ref, o_ref):
        acc = p_ref[0]
        for d in range(1, parts.shape[0]):
            acc = acc + p_ref[d]
        o_ref[...] = acc

    return pl.pallas_call(kern, name="sum_devices", out_shape=jax.ShapeDtypeStruct(parts.shape[1:], F32),
                          compiler_params=pltpu.CompilerParams(vmem_limit_bytes=VMEM_LIMIT_BYTES))(parts)


def _sum_slots(name, parts):
    _, rows, cols = parts.shape
    tr = _pick(rows, (ROW_TILE, 128, 64, 32))

    def kern(p_ref, o_ref):
        o_ref[...] = ((p_ref[3].astype(F32) + p_ref[0].astype(F32)) + p_ref[1].astype(F32)) + p_ref[2].astype(F32)

    return pl.pallas_call(kern, name=name, grid=(rows // tr,),
                          in_specs=[pl.BlockSpec((N_CHIPS, tr, cols), lambda i: (0, i, 0))],
                          out_specs=pl.BlockSpec((tr, cols), lambda i: (i, 0)),
                          out_shape=jax.ShapeDtypeStruct((rows, cols), F32), compiler_params=_params("arbitrary"))(parts)


def _adamw_math(g, w, m, v):
    m2 = ADAM_B1 * m + (1.0 - ADAM_B1) * g
    v2 = ADAM_B2 * v + (1.0 - ADAM_B2) * (g * g)
    m_hat = m2 / (1.0 - ADAM_B1 ** ADAM_STEP)
    v_hat = v2 / (1.0 - ADAM_B2 ** ADAM_STEP)
    return -ADAM_LR * (m_hat / (jnp.sqrt(v_hat) + ADAM_EPS) + ADAM_WD * w), m2, v2


def _adamw(name, parts, w, m, v):
    rows, cols = w.shape
    tr = rows if rows * cols <= WHOLE_ELEMS else _pick(rows, (ROW_TILE, 352, 128, 64, 32, 8))
    n = len(parts)

    def kern(*refs):
        g = refs[0][:, pl.ds(0, cols)]
        for p in refs[1:n]:
            g = g + p[:, pl.ds(0, cols)]
        d, m2, v2 = _adamw_math(g, refs[n][...], refs[n + 1][...], refs[n + 2][...])
        refs[n + 3][...] = g
        refs[n + 4][...] = d
        refs[n + 5][...] = m2
        refs[n + 6][...] = v2

    spec = pl.BlockSpec((tr, cols), lambda i: (i, 0))
    return pl.pallas_call(kern, name=name, grid=(rows // tr,),
                          in_specs=[pl.BlockSpec((tr, p.shape[1]), lambda i: (i, 0)) for p in parts] + [spec] * 3,
                          out_specs=[spec] * 4, out_shape=[jax.ShapeDtypeStruct((rows, cols), F32)] * 4,
                          compiler_params=_params("arbitrary"))(*parts, w, m, v)


def _adamw_many(name, gs, ws, ms, vs):
    n = len(gs)

    def kern(*refs):
        for p in range(n):
            d, m2, v2 = _adamw_math(refs[p][...], refs[n + p][...], refs[2 * n + p][...], refs[3 * n + p][...])
            refs[4 * n + p][...] = d
            refs[5 * n + p][...] = m2
            refs[6 * n + p][...] = v2

    res = pl.pallas_call(kern, name=name, out_shape=[jax.ShapeDtypeStruct(w.shape, F32) for w in ws] * 3,
                         compiler_params=pltpu.CompilerParams(vmem_limit_bytes=VMEM_LIMIT_BYTES))(*gs, *ws, *ms, *vs)
    return res[:n], res[n:2 * n], res[2 * n:]


def _pack(arrs):
    parts, rows = [], []
    for a in arrs:
        r = _round_up(-(-a.size // LANE), SUBLANE)
        parts.append(jnp.pad(a.reshape(-1).astype(F32), (0, r * LANE - a.size)).reshape(r, LANE))
        rows.append(r)
    return jnp.concatenate(parts, axis=0), rows


def _unpack(buf, rows, shapes):
    out, r0 = [], 0
    for r, s in zip(rows, shapes):
        size = math.prod(s)
        out.append(buf[r0:r0 + r].reshape(-1)[:size].reshape(s))
        r0 += r
    return out


def kernel(x, norm_ffn1, ffn1_w1, ffn1_w3, ffn1_w2, norm_mix, w_in, conv_w, conv_b, conv_ln_g, conv_ln_b, conv_out_g, ssm_A_re, ssm_A_im, ssm_log_dt, ssm_B_re, ssm_B_im, ssm_C_re, ssm_C_im, ssm_D, ssm_glu_w, ssm_glu_b, ssm_out_g, w_out, norm_ffn2, ffn2_w1, ffn2_w3, ffn2_w2, norm_final, loss_target, m_norm_ffn1, m_ffn1_w1, m_ffn1_w3, m_ffn1_w2, m_norm_mix, m_w_in, m_conv_w, m_conv_b, m_conv_ln_g, m_conv_ln_b, m_conv_out_g, m_ssm_A_re, m_ssm_A_im, m_ssm_log_dt, m_ssm_B_re, m_ssm_B_im, m_ssm_C_re, m_ssm_C_im, m_ssm_D, m_ssm_glu_w, m_ssm_glu_b, m_ssm_out_g, m_w_out, m_norm_ffn2, m_ffn2_w1, m_ffn2_w3, m_ffn2_w2, m_norm_final, v_norm_ffn1, v_ffn1_w1, v_ffn1_w3, v_ffn1_w2, v_norm_mix, v_w_in, v_conv_w, v_conv_b, v_conv_ln_g, v_conv_ln_b, v_conv_out_g, v_ssm_A_re, v_ssm_A_im, v_ssm_log_dt, v_ssm_B_re, v_ssm_B_im, v_ssm_C_re, v_ssm_C_im, v_ssm_D, v_ssm_glu_w, v_ssm_glu_b, v_ssm_out_g, v_w_out, v_norm_ffn2, v_ffn2_w1, v_ffn2_w3, v_ffn2_w2, v_norm_final):
    given = dict(locals())
    wts = {n: given[n] for n in WEIGHTS}
    n_seq, seq, d = x.shape
    n_rows = n_seq * seq
    xf = x.reshape(n_rows, d)
    tgt = loss_target.reshape(n_rows, d)
    row = lambda a: a.reshape(1, -1)

    f = ffn1_w1.shape[-1]
    fp = _round_up(f, LANE)
    held = lambda n, a: a[0].T if n in TRANSPOSED else a[0]
    shards = []
    for n in BIG:
        s = held(n, wts[n]).astype(BF16)
        if n.startswith('ffn'):
            s = jnp.pad(s, ((0, fp - f), (0, 0)))
        shards.append(s)
    n_taps, c_shard = conv_w.shape[1], conv_w.shape[2]
    shards.append(jnp.pad(conv_w[0], ((0, HALO - n_taps), (0, 0))))
    shard_of = dict(zip(BIG + ['conv_w'], shards))
    axis_of = dict(BIG_AXIS, conv_w=1)
    groups = [['ffn1_w1', 'ffn1_w3'], ['ffn1_w2', 'w_in', 'conv_w', 'ssm_glu_w', 'w_out'], ['ffn2_w1', 'ffn2_w3', 'ffn2_w2']]
    fetch, tok = [], []
    for k, names in enumerate(groups):
        fetch.append(_exchange_start("gather%d_send" % k, "gather", [shard_of[n] for n in names],
                                     [axis_of[n] for n in names], tok))
        tok = [fetch[-1][3]]
    full = {}

    def arrive(k, after):
        lands = _exchange_wait("gather%d_recv" % k, "gather", fetch[k], [axis_of[n] for n in groups[k]], after)
        full.update(zip(groups[k], lands))

    h1, h1_t = _rms_fwd("ffn1_rms", xf, norm_ffn1)
    arrive(0, tok + [h1])

    _, n_grp, n_state = ssm_A_re.shape
    grp = ssm_B_re.shape[-1]
    ns = n_grp * n_state
    c_ssm = n_grp * grp
    lr, li = ssm_A_re.reshape(1, ns), ssm_A_im.reshape(1, ns)
    ldt = jnp.repeat(ssm_log_dt.reshape(n_grp), n_state).reshape(1, ns)
    btr = ssm_B_re[0].transpose(2, 0, 1).reshape(grp, ns)
    bti = ssm_B_im[0].transpose(2, 0, 1).reshape(grp, ns)
    ctr = ssm_C_re[0].transpose(1, 0, 2).reshape(grp, ns)
    cti = ssm_C_im[0].transpose(1, 0, 2).reshape(grp, ns)
    scan_tile = _pick(seq, (SCAN_TILE,))
    _, _, bbr, bbi, seg_up, seg_down, pw, pw_falling = _s5_params_fwd(lr, li, ldt, btr, bti, scan_tile // SUBLANE)
    nb = c_ssm // LANE
    sb, gpb = ns // nb, n_grp // nb
    diag = (jnp.arange(LANE)[:, None] // grp) == (jnp.arange(sb)[None, :] // n_state)

    def spread(t):
        return jnp.where(diag, jnp.tile(t.reshape(grp, nb, sb).transpose(1, 0, 2), (1, gpb, 1)), 0.0)

    def gather_diag(t):
        return (t * diag).reshape(nb, gpb, grp, sb).sum(1).transpose(1, 0, 2).reshape(grp, ns)

    def interleave(re, im):
        return jnp.stack([re.reshape(-1, nb, sb), im.reshape(-1, nb, sb)], axis=2).reshape(-1, 2 * ns)

    bdc = jnp.concatenate([spread(bbr), spread(bbi)], axis=2).astype(BF16)
    cdc = jnp.concatenate([spread(ctr).transpose(0, 2, 1), -spread(cti).transpose(0, 2, 1)], axis=1).astype(BF16)
    rowi = jnp.arange(SUBLANE)[:, None]
    pwf, pwc = interleave(pw[:, :ns], pw[:, ns:]), interleave(pw[:, :ns], -pw[:, ns:])
    tabs_f = [jnp.where(rowi >= s, pwf[s - 1][None, :], 0.0) for s in (1, 2, 4)] + [pwf]
    tabs_b = [jnp.where(rowi <= SUBLANE - 1 - s, pwc[s - 1][None, :], 0.0) for s in (1, 2, 4)]
    tabs_b.append(interleave(pw_falling[:, :ns], -pw_falling[:, ns:]))
    fix_f = jnp.repeat(interleave(seg_up[:, :ns], seg_up[:, ns:]), SUBLANE, axis=0)
    fix_b = jnp.repeat(interleave(seg_down[:, :ns], -seg_down[:, ns:]), SUBLANE, axis=0)
    c_conv = conv_b.shape[1]
    u_blk = 2 * c_conv // LANE

    a1, b1, z1 = _ffn_up("ffn1_up", h1, full['ffn1_w1'], full['ffn1_w3'])
    arrive(1, [z1])
    x1, h2, h2_t = _mm("ffn1_down", z1, full['ffn1_w2'], 1, 0, addend=xf, alpha=0.5, post=_post_rms(norm_mix))
    saved1 = (h1_t, a1, b1, z1)
    cw = full['conv_w']
    proj = _mm("mix_in", h2, full['w_in'], 1, 0, F32)
    assert c_conv == c_ssm and proj.shape[1] == 3 * c_conv
    cpre, an = _conv_fwd(proj, cw, conv_b, conv_ln_g, conv_ln_b, conv_out_g, seq)
    xs, ypre, yg = _s5_fwd(proj, u_blk, bdc, cdc, fix_f, tabs_f, ssm_D, seq, sb)
    q0 = _mm("s5_gate", yg, full['ssm_glu_w'], 1, 0, F32)
    sn = _s5_post2(yg, q0, ssm_glu_b, ssm_out_g)
    wo = full['w_out']
    mixed = jnp.concatenate([an, sn], axis=1)
    x2, h3, h3_t = _mm("mix_out", mixed, wo, 1, 0, addend=x1, post=_post_rms(norm_ffn2))
    arrive(2, [x2])
    a3, b3, z3 = _ffn_up("ffn2_up", h3, full['ffn2_w1'], full['ffn2_w3'])
    saved2 = (h3_t, a3, b3, z3)
    dx3, dx3_t, loss_row, d_norm_final = _mm("ffn2_down", z3, full['ffn2_w2'], 1, 0, addend=x2, alpha=0.5,
                                             post=_post_loss(row(norm_final), tgt))

    g = {}
    dx2, g['norm_ffn2'], sent = _ffn_bwd("ffn2", x2, norm_ffn2, full['ffn2_w1'], full['ffn2_w3'], full['ffn2_w2'], saved2,
                                         dx3, dx3_t, early=False)
    dmixed = _mm("mix_dmixed", dx2, wo, 1, 1, F32)
    dwo = _mm("mix_dwo", mixed, dx2, 0, 0, BF16)
    dq, dyg1, g['ssm_out_g'], g['ssm_glu_b'] = _s5_post2_bwd(dmixed, yg, q0, ssm_glu_b, ssm_out_g)
    dyg2 = _mm("s5_dgate", dq, full['ssm_glu_w'], 1, 1, F32)
    dwg = _mm("s5_dwg", yg, dq, 0, 0, BF16)
    dypre, du_skip, g['ssm_D'] = _s5_post1_bwd(dyg1, dyg2, ypre, proj, ssm_D)
    du, dabar, dbdc, dcdc = _s5_bwd(dypre, du_skip, xs, proj, u_blk, bdc, cdc, fix_b, tabs_b, seq, sb)
    dabar = dabar.reshape(nb, 2, sb)
    dlr, dli, dldt, dbtr, dbti = _s5_params_bwd(lr, li, ldt, btr, bti, dabar[:, 0].reshape(1, ns), dabar[:, 1].reshape(1, ns),
                                                gather_diag(dbdc[:, :, :sb]), gather_diag(dbdc[:, :, sb:]))
    g['ssm_A_re'], g['ssm_A_im'] = dlr, dli
    g['ssm_log_dt'] = dldt.reshape(n_grp, n_state).sum(axis=1)
    g['ssm_B_re'] = dbtr.reshape(grp, n_grp, n_state).transpose(1, 2, 0)
    g['ssm_B_im'] = dbti.reshape(grp, n_grp, n_state).transpose(1, 2, 0)
    g['ssm_C_re'] = gather_diag(dcdc[:, :sb].transpose(0, 2, 1)).reshape(grp, n_grp, n_state).transpose(1, 0, 2)
    g['ssm_C_im'] = -gather_diag(dcdc[:, sb:].transpose(0, 2, 1)).reshape(grp, n_grp, n_state).transpose(1, 0, 2)
    dc, g['conv_out_g'], g['conv_ln_g'], g['conv_ln_b'], g['conv_b'] = _conv_bwd_rows(dmixed, cpre, conv_ln_g, conv_ln_b,
                                                                                    conv_out_g)
    dval, dgate, dcw = _conv_bwd_taps(proj, dc, cw, seq)
    dproj = jnp.concatenate([dval, dgate, du], axis=1)
    dwin = _mm("mix_dwin", h2_t, dproj, 1, 0, BF16)
    sent['w_out ssm_glu_w w_in'] = (_exchange_start("mix_send", "scatter", [dwo, dwg, dwin], [0, 0, 1]), [0, 0, 1])
    dx1, dx1_t, g['norm_mix'] = _mm("mix_dh", dproj, full['w_in'], 1, 1, after=[sent['w_out ssm_glu_w w_in'][0][3]],
                                    post=_post_rms_bwd(x1, norm_mix, dx2))
    dx0, g['norm_ffn1'], sent1 = _ffn_bwd("ffn1", xf, norm_ffn1, full['ffn1_w1'], full['ffn1_w3'], full['ffn1_w2'], saved1,
                                          dx1, dx1_t, early=True)
    sent.update(sent1)
    g['norm_final'] = d_norm_final
    g['conv_w'] = dcw[:n_taps]

    small_shapes = [(n_taps, c_conv) if n == 'conv_w' else wts[n].shape for n in SMALL]
    buf, buf_rows = _pack([g[n] for n in SMALL] + [loss_row])
    to_all = _exchange_start("small_send", "all", [buf], [0])
    slots = {}
    for names, (started, axes) in sent.items():
        lands = _exchange_wait(names.replace(' ', '_') + "_recv", "scatter", started, axes, after=[dx0, to_all[3]])
        slots.update(zip(names.split(), lands))
    sums = [_sum_slots("sum_" + n, slots[n]) for n in BIG]
    to_sibling = _exchange_start("sums_send", "sibling", sums, [0] * len(sums))
    from_all = _exchange_wait("small_recv", "all", to_all, [0], after=[to_sibling[3]])[0]
    total = _unpack(_sum_devices(from_all), buf_rows, small_shapes + [(1, LANE)])
    loss = total[-1][0, 0]
    grads = dict(zip(SMALL, total[:-1]))
    chip = 2 * lax.axis_index("x") + lax.axis_index("y")
    grads['conv_w'] = lax.dynamic_slice_in_dim(grads['conv_w'], chip * c_shard, c_shard, axis=1)[None]
    flat = lambda a: a.reshape(-1, a.shape[-1])
    small = _adamw_many("adamw_small", *[[flat(src[p + n]) for n in SMALL]
                                         for src, p in ((grads, ''), (given, ''), (given, 'm_'), (given, 'v_'))])
    deltas, new_m, new_v = ({n: o.reshape(wts[n].shape) for n, o in zip(SMALL, outs)} for outs in small)

    sums, theirs = _exchange_wait("sums_recv", "sibling", to_sibling, [0] * len(sums), after=[new_v[SMALL[-1]]],
                                  sources_too=True)
    for n, mine, other in zip(BIG, sums, theirs):
        grads[n], deltas[n], new_m[n], new_v[n] = (
            (o.T if n in TRANSPOSED else o)[None]
            for o in _adamw("adamw_" + n, [mine, other], held(n, given[n]), held(n, given['m_' + n]), held(n, given['v_' + n])))

    return (loss, dx0.reshape(x.shape), *[grads[n] for n in WEIGHTS], *[deltas[n] for n in WEIGHTS],
            *[new_m[n] for n in WEIGHTS], *[new_v[n] for n in WEIGHTS])
```

```python
import math
from typing import Callable, NamedTuple

import jax
import jax.numpy as jnp
from jax import lax
from jax.experimental import pallas as pl
from jax.experimental.pallas import tpu as pltpu

F32 = jnp.float32
BF16 = jnp.bfloat16
EPS = 1e-6
ADAM_LR, ADAM_B1, ADAM_B2, ADAM_EPS, ADAM_WD, ADAM_STEP = 0.001, 0.9, 0.999, 1e-08, 0.01, 10
MESH = pl.DeviceIdType.MESH
ANY = pl.BlockSpec(memory_space=pl.ANY)
LANE = 128
SUBLANE = 8
VMEM_LIMIT_BYTES = 56 << 20
ROW_TILE = 256
ROW_TILE_ELEMS = 256 * 1024
WHOLE_ELEMS = 512 * 1024
WHOLE_WEIGHT_BYTES = 8 << 20
FFN_ROWS = 256
CONV_TILE = 256
CONV_SUB = 32
HALO = 32
SCAN_TILE = 256
SCAN_COLS = 512
N_CHIPS = 4
CHIP_RELS = ((1, 0), (0, 1), (1, 1))
NT = (((1,), (1,)), ((), ()))
GELU_K = math.sqrt(2.0 / math.pi)
GELU_C = 0.044715

WEIGHTS = ['norm_ffn1', 'ffn1_w1', 'ffn1_w3', 'ffn1_w2', 'norm_mix', 'w_in', 'conv_w', 'conv_b', 'conv_ln_g', 'conv_ln_b',
           'conv_out_g', 'ssm_A_re', 'ssm_A_im', 'ssm_log_dt', 'ssm_B_re', 'ssm_B_im', 'ssm_C_re', 'ssm_C_im', 'ssm_D',
           'ssm_glu_w', 'ssm_glu_b', 'ssm_out_g', 'w_out', 'norm_ffn2', 'ffn2_w1', 'ffn2_w3', 'ffn2_w2', 'norm_final']
BIG = ['ffn1_w1', 'ffn1_w3', 'ffn1_w2', 'w_in', 'ssm_glu_w', 'w_out', 'ffn2_w1', 'ffn2_w3', 'ffn2_w2']
BIG_AXIS = {'ffn1_w1': 0, 'ffn1_w3': 0, 'ffn1_w2': 0, 'w_in': 1, 'ssm_glu_w': 0, 'w_out': 0, 'ffn2_w1': 0, 'ffn2_w3': 0,
            'ffn2_w2': 0}
TRANSPOSED = ('ffn1_w1', 'ffn1_w3', 'ffn2_w1', 'ffn2_w3')
SMALL = [n for n in WEIGHTS if n not in BIG]


def _round_up(n, m):
    return -(-n // m) * m


def _pick(n, cands):
    for c in cands:
        if c <= n and n % c == 0:
            return c
    return n


def _params(*sem):
    return pltpu.CompilerParams(dimension_semantics=sem, vmem_limit_bytes=VMEM_LIMIT_BYTES)


def _rms_r(x):
    return lax.rsqrt(jnp.mean(x * x, axis=-1, keepdims=True) + EPS)


def _rms_bwd(x, r, g, dy):
    dyg = dy * g
    return r * dyg - x * (r * r * r) * jnp.mean(x * dyg, axis=-1, keepdims=True)


def _sigmoid(x):
    return jax.nn.sigmoid(x)


def _dsilu(a, s):
    return s * (1.0 + a * (1.0 - s))


def _gelu(x):
    return 0.5 * x * (1.0 + jnp.tanh(GELU_K * (x + GELU_C * x * x * x)))


def _dgelu(x):
    t = jnp.tanh(GELU_K * (x + GELU_C * x * x * x))
    return 0.5 * (1.0 + t) + 0.5 * x * (1.0 - t * t) * GELU_K * (1.0 + 3.0 * GELU_C * x * x)


def _colsum(v):
    return jnp.sum(v, axis=0, keepdims=True)


def _rowwise(name, body, n_rows, row_ins, par_ins, row_outs, acc_outs, after=()):
    widest = max([w for (_, w, _) in row_ins] + [w for (w, _) in row_outs])
    tt = _pick(n_rows, [t for t in (512, 256, 128, 64, 32, 16, 8) if t * widest <= ROW_TILE_ELEMS])
    in_specs = [pl.BlockSpec((tt, w), lambda i, cb=cb: (i, cb)) for (_, w, cb) in row_ins]
    in_specs += [pl.BlockSpec(p.shape, lambda i: (0, 0)) for p in par_ins] + [ANY] * len(after)
    out_specs = [pl.BlockSpec((tt, w), lambda i: (i, 0)) for (w, _) in row_outs]
    out_specs += [pl.BlockSpec((r, w), lambda i: (0, 0)) for (r, w) in acc_outs]
    out_shape = [jax.ShapeDtypeStruct((n_rows, w), dt) for (w, dt) in row_outs]
    out_shape += [jax.ShapeDtypeStruct((r, w), F32) for (r, w) in acc_outs]
    n_in, n_ro = len(row_ins) + len(par_ins), len(row_outs)
    o0 = n_in + len(after)

    def kern(*refs):
        accs = refs[o0 + n_ro:]
        if accs:
            @pl.when(pl.program_id(0) == 0)
            def _():
                for a in accs:
                    a[...] = jnp.zeros_like(a)
        body(refs[:n_in], refs[o0:o0 + n_ro], accs)

    return pl.pallas_call(kern, name=name, grid=(n_rows // tt,), in_specs=in_specs, out_specs=out_specs, out_shape=out_shape,
                          compiler_params=_params("arbitrary"))(*[a for a, _, _ in row_ins], *par_ins, *after)


class Post(NamedTuple):
    rows: list
    gains: list
    outs: list
    t_outs: list
    sums: list
    fn: Callable


def _post_rms(gain):
    def fn(r, rows, gains):
        h = r * _rms_r(r) * gains[0]
        return [r, h, h], []

    return Post([], [gain], [F32, BF16], [BF16], [], fn)


def _post_rms_bwd(x, gain, dres):
    def fn(dh, rows, gains):
        r = _rms_r(rows[0])
        dx = rows[1] + _rms_bwd(rows[0], r, gains[0], dh)
        return [dx, dx], [_colsum(dh * rows[0] * r)]

    return Post([x, dres], [gain], [F32], [BF16], [x.shape[1]], fn)


def _post_loss(gain, tgt):
    d = tgt.shape[1]

    def fn(xv, rows, gains):
        r = _rms_r(xv)
        e = xv * r * gains[0] - rows[0]
        sq = jnp.sum(jnp.sum(e * e, axis=-1, keepdims=True), axis=0, keepdims=True)
        dy = e * (1.0 / d)
        dx = _rms_bwd(xv, r, gains[0], dy)
        return [dx, dx], [jnp.broadcast_to(sq * (0.5 / d), (1, LANE)), _colsum(dy * xv * r)]

    return Post([tgt], [gain], [F32], [BF16], [LANE, d], fn)


def _mm(name, a, b, ca, cb, out_dtype=F32, addend=None, alpha=1.0, a_cols=None, after=(), post=None, transposed=False):
    a_start, a_width = a_cols if a_cols else (0, a.shape[1])
    m, k = (a.shape[0], a_width) if ca == 1 else (a_width, a.shape[0])
    n = b.shape[1 - cb]
    assert b.shape[cb] == k, (name, a.shape, b.shape)
    tn = _pick(n, (1024, 768, 512, 384, 256, 128))
    whole_b = bool(post) and k * tn * b.dtype.itemsize <= WHOLE_WEIGHT_BYTES
    if whole_b:
        tk = k
        tm = _pick(m, (512, 256, 128))
    else:
        tm = _pick(m, (512, 256, 128) if post else (1024, 512, 256, 128))
        deep = (4096,) if a.dtype == BF16 and b.dtype == BF16 else ()
        tk = _pick(k, deep + (2048, 1024, 768, 512, 256, 128) if k >= 4096 and not post else (1024, 768, 512, 256, 128))
    nk = k // tk
    if ca == 1:
        assert a_start % tk == 0
        a_spec = pl.BlockSpec((tm, tk), lambda i, j, kk: (i, kk + a_start // tk))
    else:
        assert a_start % tm == 0
        a_spec = pl.BlockSpec((tk, tm), lambda i, j, kk: (kk, i + a_start // tm))
    b_mode = dict(pipeline_mode=pl.Buffered(1)) if whole_b else {}
    b_spec = (pl.BlockSpec((tk, tn), lambda i, j, kk: (kk, j), **b_mode) if cb == 0 else
              pl.BlockSpec((tn, tk), lambda i, j, kk: (j, kk), **b_mode))
    o_spec = pl.BlockSpec((tm, tn), lambda i, j, kk: (i, j))
    t_spec = pl.BlockSpec((tn, tm), lambda i, j, kk: (j, i))
    fixed = lambda w: pl.BlockSpec((1, w), lambda i, j, kk: (0, 0))
    ins, in_specs = [a, b], [a_spec, b_spec]
    if addend is not None:
        ins.append(addend)
        in_specs.append(o_spec)
    n_plain = len(ins)
    n_rows, n_gains = (len(post.rows), len(post.gains)) if post else (0, 0)
    if post:
        assert tn == n, name
        ins += post.rows + post.gains
        in_specs += [o_spec] * n_rows + [fixed(n)] * n_gains
    ins += list(after)
    in_specs += [ANY] * len(after)
    n_in = len(ins)
    if post:
        n_straight, n_vals = len(post.outs), len(post.outs) + len(post.t_outs)
        out_specs = [o_spec] * n_straight + [t_spec] * len(post.t_outs) + [fixed(w) for w in post.sums]
        out_shape = [jax.ShapeDtypeStruct((m, n), dt) for dt in post.outs] + [jax.ShapeDtypeStruct((n, m), dt) for dt in post.t_outs]
        out_shape += [jax.ShapeDtypeStruct((1, w), F32) for w in post.sums]
    elif transposed:
        out_specs, out_shape = [t_spec], [jax.ShapeDtypeStruct((n, m), out_dtype)]
    else:
        out_specs, out_shape = [o_spec], [jax.ShapeDtypeStruct((m, n), out_dtype)]
    n_out = len(out_specs)
    dims = (((ca,), (cb,)), ((), ()))

    def emit(refs, r):
        if alpha != 1.0:
            r = r * alpha
        if addend is not None:
            r = r + refs[2][...].astype(F32)
        outs = refs[n_in:n_in + n_out]
        if post is None:
            outs[0][...] = (r.T if transposed else r).astype(out_dtype)
            return
        vals, incs = post.fn(r, [q[...] for q in refs[n_plain:n_plain + n_rows]],
                             [q[...] for q in refs[n_plain + n_rows:n_plain + n_rows + n_gains]])
        for at, (o_ref, val) in enumerate(zip(outs, vals)):
            o_ref[...] = (val if at < n_straight else val.T).astype(o_ref.dtype)
        for s_ref, inc in zip(outs[n_vals:], incs):
            s_ref[...] += inc

    def kern(*refs):
        kk = pl.program_id(2)
        if post and post.sums:
            @pl.when(jnp.logical_and(jnp.logical_and(pl.program_id(0) == 0, pl.program_id(1) == 0), kk == 0))
            def _():
                for s_ref in refs[n_in + n_vals:n_in + n_out]:
                    s_ref[...] = jnp.zeros_like(s_ref)

        dot = lambda: lax.dot_general(refs[0][...].astype(BF16), refs[1][...].astype(BF16), dims,
                                      preferred_element_type=F32)
        if nk == 1:
            emit(refs, dot())
            return
        acc_ref = refs[-1]

        @pl.when(kk == 0)
        def _():
            acc_ref[...] = jnp.zeros_like(acc_ref)

        acc_ref[...] += dot()

        @pl.when(kk == nk - 1)
        def _():
            emit(refs, acc_ref[...])

    res = pl.pallas_call(kern, name=name, grid=(m // tm, n // tn, nk), in_specs=in_specs, out_specs=out_specs,
                         out_shape=out_shape, scratch_shapes=[] if nk == 1 else [pltpu.VMEM((tm, tn), F32)],
                         compiler_params=_params("arbitrary", "arbitrary", "arbitrary"))(*ins)
    return res if post else res[0]


def _rms_fwd(name, x, g):
    t, d = x.shape
    tt = _pick(t, (ROW_TILE, LANE))

    def kern(x_ref, g_ref, h_ref, ht_ref):
        xv = x_ref[...]
        h = xv * _rms_r(xv) * g_ref[...]
        h_ref[...] = h.astype(BF16)
        ht_ref[...] = h.T.astype(BF16)

    return pl.pallas_call(kern, name=name, grid=(t // tt,),
                          in_specs=[pl.BlockSpec((tt, d), lambda i: (i, 0)), pl.BlockSpec((1, d), lambda i: (0, 0))],
                          out_specs=[pl.BlockSpec((tt, d), lambda i: (i, 0)), pl.BlockSpec((d, tt), lambda i: (0, i))],
                          out_shape=[jax.ShapeDtypeStruct((t, d), BF16), jax.ShapeDtypeStruct((d, t), BF16)],
                          compiler_params=_params("arbitrary"))(x, g)


def _ffn_up(name, h, w1, w3):
    t, d = h.shape
    ff = w1.shape[0]
    tm, tn = _pick(t, (1024, 512, 256, 128)), _pick(ff, (1024, 768, 512, 256, 128))

    def kern(h_ref, w1_ref, w3_ref, a_ref, b_ref, z_ref):
        hv = h_ref[...]
        a = lax.dot_general(hv, w1_ref[...], NT, preferred_element_type=F32)
        b = lax.dot_general(hv, w3_ref[...], NT, preferred_element_type=F32)
        a_ref[...] = a.astype(BF16)
        b_ref[...] = b.astype(BF16)
        z_ref[...] = (a * _sigmoid(a) * b).astype(BF16)

    w_spec = pl.BlockSpec((tn, d), lambda i, j: (j, 0))
    o_spec = pl.BlockSpec((tm, tn), lambda i, j: (i, j))
    return pl.pallas_call(kern, name=name, grid=(t // tm, ff // tn),
                          in_specs=[pl.BlockSpec((tm, d), lambda i, j: (i, 0)), w_spec, w_spec], out_specs=[o_spec] * 3,
                          out_shape=[jax.ShapeDtypeStruct((t, ff), BF16)] * 3,
                          compiler_params=_params("arbitrary", "arbitrary"))(h, w1, w3)


def _ffn_dglu(name, dxo, w2, a, b, after=()):
    t, d = dxo.shape
    ff = w2.shape[0]
    tm = _pick(t, (FFN_ROWS, 128))

    def kern(dx_ref, w2_ref, a_ref, b_ref, *rest):
        da_ref, db_ref = rest[-2:]
        dz = lax.dot_general(dx_ref[...].astype(BF16), w2_ref[...], NT, preferred_element_type=F32) * 0.5
        av, bv = a_ref[...].astype(F32), b_ref[...].astype(F32)
        s = _sigmoid(av)
        da_ref[...] = (dz * bv * _dsilu(av, s)).astype(BF16)
        db_ref[...] = (dz * av * s).astype(BF16)

    o_spec = pl.BlockSpec((tm, ff), lambda i: (i, 0))
    return pl.pallas_call(kern, name=name, grid=(t // tm,),
                          in_specs=[pl.BlockSpec((tm, d), lambda i: (i, 0)), pl.BlockSpec((ff, d), lambda i: (0, 0)),
                                    o_spec, o_spec] + [ANY] * len(after),
                          out_specs=[o_spec] * 2, out_shape=[jax.ShapeDtypeStruct((t, ff), BF16)] * 2,
                          compiler_params=_params("arbitrary"))(dxo, w2, a, b, *after)


def _ffn_dh(name, da, db, w1, w3, x, g, dres, after=()):
    t, d = x.shape
    ff = da.shape[1]
    tm = _pick(t, (2 * FFN_ROWS, 128))

    def kern(da_ref, db_ref, w1_ref, w3_ref, x_ref, g_ref, dres_ref, *rest):
        dx_ref, dg_ref = rest[-2:]

        @pl.when(pl.program_id(0) == 0)
        def _():
            dg_ref[...] = jnp.zeros_like(dg_ref)

        dh = (jnp.dot(da_ref[...], w1_ref[...], preferred_element_type=F32)
              + jnp.dot(db_ref[...], w3_ref[...], preferred_element_type=F32))
        xv = x_ref[...]
        r = _rms_r(xv)
        dx_ref[...] = dres_ref[...] + _rms_bwd(xv, r, g_ref[...], dh)
        dg_ref[...] += _colsum(dh * xv * r)

    act = pl.BlockSpec((tm, ff), lambda i: (i, 0))
    wgt = pl.BlockSpec((ff, d), lambda i: (0, 0), pipeline_mode=pl.Buffered(1))
    rows = pl.BlockSpec((tm, d), lambda i: (i, 0))
    gain = pl.BlockSpec((1, d), lambda i: (0, 0))
    return pl.pallas_call(kern, name=name, grid=(t // tm,),
                          in_specs=[act, act, wgt, wgt, rows, gain, rows] + [ANY] * len(after), out_specs=[rows, gain],
                          out_shape=[jax.ShapeDtypeStruct((t, d), F32), jax.ShapeDtypeStruct((1, d), F32)],
                          compiler_params=_params("arbitrary"))(da, db, w1, w3, x, g, dres, *after)


def _ffn_bwd(tag, x, g, w1, w3, w2, saved, dxo, dxo_t, early):
    ht, a, b, z = saved
    dw2 = _mm(tag + "_dw2", dxo_t, z, 1, 0, BF16, alpha=0.5, transposed=True)
    da, db = _ffn_dglu(tag + "_dglu", dxo, w2, a, b)
    dw1 = _mm(tag + "_dw1", ht, da, 1, 0, BF16, transposed=True)
    sent, pin = {}, []
    if early:
        sent[tag + "_w2 " + tag + "_w1"] = (_exchange_start(tag + "_w2_w1_send", "scatter", [dw2, dw1], [0, 0]), [0, 0])
        pin = [sent[tag + "_w2 " + tag + "_w1"][0][3]]
    dw3 = _mm(tag + "_dw3", ht, db, 1, 0, BF16, after=pin, transposed=True)
    last = [dw3] if early else [dw2, dw1, dw3]
    names = [tag + "_w3"] if early else [tag + "_w2", tag + "_w1", tag + "_w3"]
    sent[" ".join(names)] = (_exchange_start(tag + "_w3_send", "scatter", last, [0] * len(last)), [0] * len(last))
    dx, dg = _ffn_dh(tag + "_dh", da, db, w1, w3, x, g, dxo, after=[sent[" ".join(names)][0][3]])
    return dx, dg, sent


def _shift_copies(ext_ref, sh_ref):
    n = ext_ref.shape[0] - SUBLANE
    for r in range(1, SUBLANE):
        sh_ref[r, pl.ds(0, n), :] = ext_ref[pl.ds(r, n), :]


def _rows_at(ext_ref, sh_ref, off, rows):
    r = off % SUBLANE
    return ext_ref[pl.ds(off, rows), :] if r == 0 else sh_ref[r, pl.ds(off - r, rows), :]


def _conv_fwd(proj, cw, cb, lng, lnb, og, seq):
    n_rows, c = proj.shape[0], cb.shape[1]
    kw = HALO - 1
    tt = _pick(seq, (CONV_TILE,))
    hb = tt // HALO

    def kern(v_ref, g_ref, vp_ref, gp_ref, w_ref, cb_ref, lg_ref, lb_ref, og_ref, c_ref, an_ref, ext_ref, sh_ref):
        first = (pl.program_id(0) * tt) % seq == 0
        ext_ref[pl.ds(HALO, tt), :] = v_ref[...] * _sigmoid(g_ref[...])
        ext_ref[pl.ds(0, HALO), :] = vp_ref[...] * _sigmoid(gp_ref[...]) * jnp.where(first, 0.0, 1.0)
        _shift_copies(ext_ref, sh_ref)
        for r0 in range(0, tt, CONV_SUB):
            rows = min(CONV_SUB, tt - r0)
            acc = jnp.zeros((rows, c), F32)
            for k in range(kw):
                acc = acc + w_ref[pl.ds(k, 1), :] * _rows_at(ext_ref, sh_ref, r0 + HALO - (kw - 1) + k, rows)
            c_ref[pl.ds(r0, rows), :] = acc + cb_ref[...]
        cv = c_ref[...]
        mu = jnp.mean(cv, axis=-1, keepdims=True)
        xc = cv - mu
        rstd = lax.rsqrt(jnp.mean(xc * xc, axis=-1, keepdims=True) + EPS)
        lv = xc * rstd * lg_ref[...] + lb_ref[...]
        sl = lv * _sigmoid(lv)
        an_ref[...] = (sl * _rms_r(sl) * og_ref[...]).astype(BF16)

    cur = lambda cbk: pl.BlockSpec((tt, c), lambda i: (i, cbk))
    prev = lambda cbk: pl.BlockSpec((HALO, c), lambda i: (jnp.maximum(i * hb - 1, 0), cbk))
    par = lambda p: pl.BlockSpec(p.shape, lambda i: (0, 0))
    return pl.pallas_call(
        kern, name="conv_fwd", grid=(n_rows // tt,),
        in_specs=[cur(0), cur(1), prev(0), prev(1), par(cw), par(cb), par(lng), par(lnb), par(og)],
        out_specs=[pl.BlockSpec((tt, c), lambda i: (i, 0))] * 2,
        out_shape=[jax.ShapeDtypeStruct((n_rows, c), F32), jax.ShapeDtypeStruct((n_rows, c), BF16)],
        scratch_shapes=[pltpu.VMEM((tt + HALO, c), F32), pltpu.VMEM((SUBLANE, tt + HALO, c), F32)],
        compiler_params=_params("arbitrary"),
    )(proj, proj, proj, proj, cw, cb, lng, lnb, og)


def _conv_bwd_rows(dmixed, cpre, lng, lnb, og):
    c = cpre.shape[1]

    def body(ins, outs, accs):
        dan, cv, lg, lb, ogv = ins[0][...], ins[1][...], ins[2][...], ins[3][...], ins[4][...]
        mu = jnp.mean(cv, axis=-1, keepdims=True)
        xc = cv - mu
        rstd = lax.rsqrt(jnp.mean(xc * xc, axis=-1, keepdims=True) + EPS)
        xh = xc * rstd
        lv = xh * lg + lb
        s = _sigmoid(lv)
        sl = lv * s
        r2 = _rms_r(sl)
        accs[0][...] += _colsum(dan * sl * r2)
        dl = _rms_bwd(sl, r2, ogv, dan) * _dsilu(lv, s)
        accs[1][...] += _colsum(dl * xh)
        accs[2][...] += _colsum(dl)
        dxh = dl * lg
        dc = rstd * (dxh - jnp.mean(dxh, axis=-1, keepdims=True) - xh * jnp.mean(dxh * xh, axis=-1, keepdims=True))
        outs[0][...] = dc
        accs[3][...] += _colsum(dc)

    return _rowwise("conv_bwd_rows", body, cpre.shape[0], [(dmixed, c, 0), (cpre, c, 0)], [lng, lnb, og], [(c, F32)],
                    [(1, c)] * 4)


def _conv_bwd_taps(proj, dc, cw, seq):
    n_rows, c = dc.shape
    kw = HALO - 1
    tt = _pick(seq, (CONV_TILE,))
    hb = tt // HALO
    last_blk = n_rows // HALO - 1

    def kern(v_ref, g_ref, vp_ref, gp_ref, dc_ref, dn_ref, w_ref, dv_ref, dg_ref, dw_ref, exta_ref, extd_ref, sha_ref, shd_ref):
        i = pl.program_id(0)
        first = (i * tt) % seq == 0
        last = ((i + 1) * tt) % seq == 0

        @pl.when(i == 0)
        def _():
            dw_ref[...] = jnp.zeros_like(dw_ref)

        sg = _sigmoid(g_ref[...])
        exta_ref[pl.ds(HALO, tt), :] = v_ref[...] * sg
        exta_ref[pl.ds(0, HALO), :] = vp_ref[...] * _sigmoid(gp_ref[...]) * jnp.where(first, 0.0, 1.0)
        extd_ref[pl.ds(0, tt), :] = dc_ref[...]
        extd_ref[pl.ds(tt, HALO), :] = dn_ref[...] * jnp.where(last, 0.0, 1.0)
        _shift_copies(exta_ref, sha_ref)
        _shift_copies(extd_ref, shd_ref)
        for k0 in range(0, kw, SUBLANE):
            taps = range(k0, min(k0 + SUBLANE, kw))
            sums = [jnp.zeros((SUBLANE, c), F32) for _ in taps]
            for r0 in range(0, tt, SUBLANE):
                dcb = dc_ref[pl.ds(r0, SUBLANE), :]
                for n, k in enumerate(taps):
                    sums[n] = sums[n] + _rows_at(exta_ref, sha_ref, r0 + HALO - (kw - 1) + k, SUBLANE) * dcb
            for n, k in enumerate(taps):
                dw_ref[pl.ds(k, 1), :] += _colsum(sums[n])
        for r0 in range(0, tt, CONV_SUB):
            rows = min(CONV_SUB, tt - r0)
            acc = jnp.zeros((rows, c), F32)
            for k in range(kw):
                acc = acc + w_ref[pl.ds(k, 1), :] * _rows_at(extd_ref, shd_ref, r0 + (kw - 1) - k, rows)
            dv_ref[pl.ds(r0, rows), :] = acc
        da = dv_ref[...]
        dv_ref[...] = da * sg
        dg_ref[...] = da * v_ref[...] * sg * (1.0 - sg)

    cur = lambda cbk: pl.BlockSpec((tt, c), lambda i: (i, cbk))
    prev = lambda cbk: pl.BlockSpec((HALO, c), lambda i: (jnp.maximum(i * hb - 1, 0), cbk))
    nxt = pl.BlockSpec((HALO, c), lambda i: (jnp.minimum((i + 1) * hb, last_blk), 0))
    return pl.pallas_call(
        kern, name="conv_bwd_taps", grid=(n_rows // tt,),
        in_specs=[cur(0), cur(1), prev(0), prev(1), cur(0), nxt, pl.BlockSpec(cw.shape, lambda i: (0, 0))],
        out_specs=[cur(0), cur(0), pl.BlockSpec((HALO, c), lambda i: (0, 0))],
        out_shape=[jax.ShapeDtypeStruct((n_rows, c), F32), jax.ShapeDtypeStruct((n_rows, c), F32),
                   jax.ShapeDtypeStruct((HALO, c), F32)],
        scratch_shapes=[pltpu.VMEM((tt + HALO, c), F32)] * 2 + [pltpu.VMEM((SUBLANE, tt + HALO, c), F32)] * 2,
        compiler_params=_params("arbitrary"),
    )(proj, proj, proj, proj, dc, dc, cw)


def _s5_params_fwd(lr, li, ldt, btr, bti, seg):
    ns = lr.shape[1]

    def kern(lr_ref, li_ref, ldt_ref, btr_ref, bti_ref, ar_ref, ai_ref, bbr_ref, bbi_ref, ps_ref, psf_ref, pc_ref, pcf_ref):
        lrv, liv = lr_ref[...], li_ref[...]
        dt = jnp.exp(ldt_ref[...])
        zr, zi = lrv * dt, liv * dt
        mag = jnp.exp(zr)
        ar, ai = mag * jnp.cos(zi), mag * jnp.sin(zi)
        den = lrv * lrv + liv * liv
        nr = ar - 1.0
        cr = (nr * lrv + ai * liv) / den
        ci = (ai * lrv - nr * liv) / den
        ar_ref[...] = ar
        ai_ref[...] = ai
        bbr_ref[...] = cr * btr_ref[...] - ci * bti_ref[...]
        bbi_ref[...] = cr * bti_ref[...] + ci * btr_ref[...]
        def powers(br, bi, count, up_ref, down_ref):
            pr, pi = br, bi
            for e in range(count):
                for ref, at in ((up_ref, e), (down_ref, count - 1 - e)):
                    ref[pl.ds(at, 1), pl.ds(0, ns)] = pr
                    ref[pl.ds(at, 1), pl.ds(ns, ns)] = pi
                if e < count - 1:
                    pr, pi = pr * br - pi * bi, pr * bi + pi * br
            return pr, pi

        powers(*powers(ar, ai, seg, ps_ref, psf_ref), SUBLANE, pc_ref, pcf_ref)

    h = btr.shape[0]
    shapes = [jax.ShapeDtypeStruct((1, ns), F32)] * 2 + [jax.ShapeDtypeStruct((h, ns), F32)] * 2
    shapes += [jax.ShapeDtypeStruct((seg, 2 * ns), F32)] * 2 + [jax.ShapeDtypeStruct((SUBLANE, 2 * ns), F32)] * 2
    return pl.pallas_call(kern, name="s5_params_fwd", out_shape=shapes)(lr, li, ldt, btr, bti)


def _s5_params_bwd(lr, li, ldt, btr, bti, dar, dai, dbbr, dbbi):
    def kern(lr_ref, li_ref, ldt_ref, btr_ref, bti_ref, dar_ref, dai_ref, dbr_ref, dbi_ref,
             dlr_ref, dli_ref, dldt_ref, dbtr_ref, dbti_ref):
        lrv, liv = lr_ref[...], li_ref[...]
        dt = jnp.exp(ldt_ref[...])
        zr, zi = lrv * dt, liv * dt
        mag = jnp.exp(zr)
        ar, ai = mag * jnp.cos(zi), mag * jnp.sin(zi)
        den = lrv * lrv + liv * liv
        nr = ar - 1.0
        cr = (nr * lrv + ai * liv) / den
        ci = (ai * lrv - nr * liv) / den
        dbr, dbi, br, bi = dbr_ref[...], dbi_ref[...], btr_ref[...], bti_ref[...]
        dbtr_ref[...] = cr * dbr + ci * dbi
        dbti_ref[...] = cr * dbi - ci * dbr
        dcr = _colsum(br * dbr + bi * dbi)
        dci = _colsum(br * dbi - bi * dbr)
        ir, ii = lrv / den, -liv / den
        dnr = ir * dcr + ii * dci
        dni = ir * dci - ii * dcr
        wr, wi = cr * ir - ci * ii, cr * ii + ci * ir
        dl1r = -(wr * dcr + wi * dci)
        dl1i = -(wr * dci - wi * dcr)
        dtr, dti = dar_ref[...] + dnr, dai_ref[...] + dni
        dzr = ar * dtr + ai * dti
        dzi = ar * dti - ai * dtr
        dlr_ref[...] = dl1r + dt * dzr
        dli_ref[...] = dl1i + dt * dzi
        dldt_ref[...] = (dzr * lrv + dzi * liv) * dt

    ns, h = lr.shape[1], btr.shape[0]
    shapes = [jax.ShapeDtypeStruct((1, ns), F32)] * 3 + [jax.ShapeDtypeStruct((h, ns), F32)] * 2
    return pl.pallas_call(kern, name="s5_params_bwd", out_shape=shapes)(lr, li, ldt, btr, bti, dar, dai, dbbr, dbbi)


def _to_segments(nat_ref, seg_ref):
    steps = nat_ref.shape[0] // SUBLANE
    _regroup(nat_ref, seg_ref, lambda r: (r % SUBLANE) * steps + r // SUBLANE)


def _from_segments(seg_ref, nat_ref):
    steps = nat_ref.shape[0] // SUBLANE
    _regroup(seg_ref, nat_ref, lambda r: (r % steps) * SUBLANE + r // steps)


def _regroup(src_ref, dst_ref, src_row):
    rows, width = dst_ref.shape
    sublane = lax.broadcasted_iota(jnp.int32, (SUBLANE, width), 0)
    for r0 in range(0, rows, SUBLANE):
        tile = jnp.broadcast_to(src_ref[pl.ds(src_row(r0), 1), :], (SUBLANE, width))
        for k in range(1, SUBLANE):
            tile = jnp.where(sublane == k, src_ref[pl.ds(src_row(r0 + k), 1), :], tile)
        dst_ref[pl.ds(r0, SUBLANE), :] = tile


def _scan_tile(s_ref, o_ref, fix_ref, tabs, car_ref, sb, reverse, x_ref=None, acc_ref=None):
    l1, l2, l4, pw = tabs
    rows_t, w = s_ref.shape
    steps = rows_t // SUBLANE
    cw = _pick(sb, (SCAN_COLS,))
    last = 0 if reverse else SUBLANE - 1
    first = SUBLANE - 1 - last
    row = lax.broadcasted_iota(jnp.int32, (SUBLANE, cw), 0)
    step_rows = lambda i: pl.ds(pl.multiple_of(((steps - 1 - i) if reverse else i) * SUBLANE, SUBLANE), SUBLANE)
    zero = jnp.zeros((SUBLANE, cw), F32)

    for c0 in [b0 + o for b0 in range(0, w, 2 * sb) for o in range(0, sb, cw)]:
        cr, ci = pl.ds(c0, cw), pl.ds(c0 + sb, cw)
        base = pl.ds(((steps - 1) if reverse else 0) * SUBLANE, SUBLANE)
        ar, ai = fix_ref[base, cr], fix_ref[base, ci]

        def run(i, state):
            xr, xi = state
            rows = step_rows(i)
            xr, xi = ar * xr - ai * xi + s_ref[rows, cr], ar * xi + ai * xr + s_ref[rows, ci]
            o_ref[rows, cr] = xr
            o_ref[rows, ci] = xi
            return xr, xi

        fr, fi = lax.fori_loop(0, steps, run, (zero, zero))
        for s, lt in ((1, l1), (2, l2), (4, l4)):
            sh = (SUBLANE - s) if reverse else s
            sr, si = pltpu.roll(fr, sh, 0), pltpu.roll(fi, sh, 0)
            tr, ti = lt[:, cr], lt[:, ci]
            fr, fi = fr + tr * sr - ti * si, fi + tr * si + ti * sr
        kr, ki = car_ref[pl.ds(last, 1), cr], car_ref[pl.ds(last, 1), ci]
        pr, pi = pw[:, cr], pw[:, ci]
        fr, fi = fr + pr * kr - pi * ki, fi + pr * ki + pi * kr
        car_ref[:, cr] = fr
        car_ref[:, ci] = fi
        to_next = 1 if not reverse else SUBLANE - 1
        gr = jnp.where(row == first, kr, pltpu.roll(fr, to_next, 0))
        gi = jnp.where(row == first, ki, pltpu.roll(fi, to_next, 0))

        def fix(i, state):
            rows = step_rows(i)
            qr, qi = fix_ref[rows, cr], fix_ref[rows, ci]
            yr = o_ref[rows, cr] + qr * gr - qi * gi
            yi = o_ref[rows, ci] + qr * gi + qi * gr
            o_ref[rows, cr] = yr
            o_ref[rows, ci] = yi
            if acc_ref is None:
                return state
            nr, ni, sr, si = state
            pxr, pxi = x_ref[rows, cr], x_ref[rows, ci]
            return yr, yi, sr + nr * pxr + ni * pxi, si + ni * pxr - nr * pxi

        if acc_ref is None:
            lax.fori_loop(0, steps, fix, 0)
        else:
            _, _, sr, si = lax.fori_loop(0, steps, fix, (gr, gi, zero, zero))
            acc_ref[:, cr] += sr
            acc_ref[:, ci] += si


def _s5_fwd(proj, u_blk, bdc, cdc, fix, tabs, dskip, seq, sb):
    n_rows = proj.shape[0]
    nb, blk, w_blk = bdc.shape
    c, w = nb * blk, nb * w_blk
    tt = fix.shape[0]

    def kern(u_ref, bd_ref, cd_ref, fix_ref, l1, l2, l4, pw, d_ref, xs_ref, yp_ref, yg_ref, us_ref, bu_ref, car_ref):
        @pl.when((pl.program_id(0) * tt) % seq == 0)
        def _():
            car_ref[...] = jnp.zeros_like(car_ref)

        _to_segments(u_ref, us_ref)
        for j in range(nb):
            bu_ref[:, pl.ds(j * w_blk, w_blk)] = jnp.dot(us_ref[:, pl.ds(j * blk, blk)].astype(BF16), bd_ref[j],
                                                         preferred_element_type=F32)
        _scan_tile(bu_ref, xs_ref, fix_ref, (l1, l2, l4, pw), car_ref, sb, False)
        for j in range(nb):
            cols = pl.ds(j * blk, blk)
            y0 = jnp.dot(xs_ref[:, pl.ds(j * w_blk, w_blk)].astype(BF16), cd_ref[j], preferred_element_type=F32)
            us_ref[:, cols] = y0 + d_ref[:, cols] * us_ref[:, cols]
        _from_segments(us_ref, yp_ref)
        yg_ref[...] = _gelu(yp_ref[...]).astype(BF16)

    tab = pl.BlockSpec((SUBLANE, w), lambda i: (0, 0))
    rows = pl.BlockSpec((tt, c), lambda i: (i, 0))
    return pl.pallas_call(
        kern, name="s5_fwd", grid=(n_rows // tt,),
        in_specs=[pl.BlockSpec((tt, c), lambda i: (i, u_blk * blk // c)), pl.BlockSpec(bdc.shape, lambda i: (0, 0, 0)),
                  pl.BlockSpec(cdc.shape, lambda i: (0, 0, 0)), pl.BlockSpec((tt, w), lambda i: (0, 0)), tab, tab, tab, tab,
                  pl.BlockSpec((1, c), lambda i: (0, 0))],
        out_specs=[pl.BlockSpec((tt, w), lambda i: (i, 0)), rows, rows],
        out_shape=[jax.ShapeDtypeStruct((n_rows, w), F32), jax.ShapeDtypeStruct((n_rows, c), F32),
                   jax.ShapeDtypeStruct((n_rows, c), BF16)],
        scratch_shapes=[pltpu.VMEM((tt, c), F32), pltpu.VMEM((tt, w), F32), pltpu.VMEM((SUBLANE, w), F32)],
        compiler_params=_params("arbitrary"))(proj, bdc, cdc, fix, *tabs, dskip)


def _s5_bwd(dypre, du_skip, xs, proj, u_blk, bdc, cdc, fix, tabs, seq, sb):
    n_rows = proj.shape[0]
    nb, blk, w_blk = bdc.shape
    c, w = nb * blk, nb * w_blk
    tt = fix.shape[0]
    nt = n_rows // tt
    tn = (((0,), (0,)), ((), ()))

    def kern(dy_ref, ds_ref, x_ref, u_ref, bd_ref, cd_ref, fix_ref, l1, l2, l4, pw, du_ref, da_ref, db_ref, dc_ref,
             dys_ref, us_ref, dus_ref, gx_ref, lam_ref, car_ref, acc_ref):
        i = pl.program_id(0)

        @pl.when(((nt - i) * tt) % seq == 0)
        def _():
            car_ref[...] = jnp.zeros_like(car_ref)

        @pl.when(i == 0)
        def _():
            acc_ref[...] = jnp.zeros_like(acc_ref)
            db_ref[...] = jnp.zeros_like(db_ref)
            dc_ref[...] = jnp.zeros_like(dc_ref)

        _to_segments(dy_ref, dys_ref)
        _to_segments(u_ref, us_ref)
        for j in range(nb):
            gx_ref[:, pl.ds(j * w_blk, w_blk)] = lax.dot_general(dys_ref[:, pl.ds(j * blk, blk)].astype(BF16), cd_ref[j], NT,
                                                                 preferred_element_type=F32)
        _scan_tile(gx_ref, lam_ref, fix_ref, (l1, l2, l4, pw), car_ref, sb, True, x_ref, acc_ref)
        for j in range(nb):
            cols, wide = pl.ds(j * blk, blk), pl.ds(j * w_blk, w_blk)
            lam = lam_ref[:, wide].astype(BF16)
            dus_ref[:, cols] = lax.dot_general(lam, bd_ref[j], NT, preferred_element_type=F32)
            db_ref[j] += lax.dot_general(us_ref[:, cols].astype(BF16), lam, tn, preferred_element_type=F32)
            dc_ref[j] += lax.dot_general(x_ref[:, wide].astype(BF16), dys_ref[:, cols].astype(BF16), tn,
                                         preferred_element_type=F32)
        _from_segments(dus_ref, du_ref)
        du_ref[...] += ds_ref[...]

        @pl.when(i == nt - 1)
        def _():
            da_ref[...] = _colsum(acc_ref[...])

    back = lambda i: (nt - 1 - i, 0)
    tab = pl.BlockSpec((SUBLANE, w), lambda i: (0, 0))
    rows = pl.BlockSpec((tt, c), back)
    whole = lambda a: pl.BlockSpec(a.shape, lambda i: (0, 0, 0))
    return pl.pallas_call(
        kern, name="s5_bwd", grid=(nt,),
        in_specs=[rows, rows, pl.BlockSpec((tt, w), back), pl.BlockSpec((tt, c), lambda i: (nt - 1 - i, u_blk * blk // c)),
                  whole(bdc), whole(cdc), pl.BlockSpec((tt, w), lambda i: (0, 0)), tab, tab, tab, tab],
        out_specs=[rows, pl.BlockSpec((1, w), lambda i: (0, 0)), whole(bdc), whole(cdc)],
        out_shape=[jax.ShapeDtypeStruct((n_rows, c), F32), jax.ShapeDtypeStruct((1, w), F32),
                   jax.ShapeDtypeStruct(bdc.shape, F32), jax.ShapeDtypeStruct(cdc.shape, F32)],
        scratch_shapes=[pltpu.VMEM((tt, c), F32)] * 3 + [pltpu.VMEM((tt, w), F32)] * 2 + [pltpu.VMEM((SUBLANE, w), F32)] * 2,
        compiler_params=_params("arbitrary"))(dypre, du_skip, xs, proj, bdc, cdc, fix, *tabs)


def _s5_post2(yg, q0, bg, og):
    c = yg.shape[1]

    def body(ins, outs, accs):
        ygv = ins[0][...].astype(F32)
        sg = ygv * _sigmoid(ins[1][...] + ins[2][...])
        outs[0][...] = (sg * _rms_r(sg) * ins[3][...]).astype(BF16)

    return _rowwise("s5_post2", body, yg.shape[0], [(yg, c, 0), (q0, c, 0)], [bg, og], [(c, BF16)], [])[0]


def _s5_post2_bwd(dmixed, yg, q0, bg, og):
    c = yg.shape[1]

    def body(ins, outs, accs):
        dsn, ygv = ins[0][...], ins[1][...].astype(F32)
        s = _sigmoid(ins[2][...] + ins[3][...])
        sg = ygv * s
        r = _rms_r(sg)
        accs[0][...] += _colsum(dsn * sg * r)
        dsg = _rms_bwd(sg, r, ins[4][...], dsn)
        dq = dsg * ygv * s * (1.0 - s)
        outs[0][...] = dq.astype(BF16)
        outs[1][...] = dsg * s
        accs[1][...] += _colsum(dq)

    return _rowwise("s5_post2_bwd", body, yg.shape[0], [(dmixed, c, 1), (yg, c, 0), (q0, c, 0)], [bg, og],
                    [(c, BF16), (c, F32)], [(1, c)] * 2)


def _s5_post1_bwd(dyg1, dyg2, ypre, proj, dskip, after=()):
    c = ypre.shape[1]

    def body(ins, outs, accs):
        dyp = (ins[0][...] + ins[1][...]) * _dgelu(ins[2][...])
        outs[0][...] = dyp
        outs[1][...] = dyp * ins[4][...]
        accs[0][...] += _colsum(dyp * ins[3][...])

    return _rowwise("s5_post1_bwd", body, ypre.shape[0], [(dyg1, c, 0), (dyg2, c, 0), (ypre, c, 0), (proj, c, 2)], [dskip],
                    [(c, F32), (c, F32)], [(1, c)], after=after)


def _place():
    return lax.axis_index("x"), lax.axis_index("y"), lax.axis_index("c")


def _window(ref, axis, q, rows, cols):
    if axis == 0:
        return ref.at[pl.ds(pl.multiple_of(q * rows, SUBLANE), rows), :]
    return ref.at[:, pl.ds(pl.multiple_of(q * cols, LANE), cols)]


ALL_RELS = [(fx, fy, fc) for fx in (0, 1) for fy in (0, 1) for fc in (0, 1)][1:]
N_PEERS = {"gather": 3, "scatter": 3, "sibling": 1, "all": len(ALL_RELS)}


def _copies(kind, srcs, lands, shards, axes, send_sems, recv_sems, local_sems):
    x, y, c = _place()
    me, dev = 2 * x + y, 4 * x + 2 * y + c
    n_peers = N_PEERS[kind]
    starts, waits = [], []
    for a, (src, land) in enumerate(zip(srcs, lands)):
        on = lambda k, peer: dict(send_sem=send_sems.at[n_peers * a + k], recv_sem=recv_sems.at[n_peers * a + k],
                                  device_id=peer, device_id_type=MESH)
        if kind == "sibling":
            cp = pltpu.make_async_remote_copy(src_ref=src, dst_ref=land, **on(0, (x, y, 1 - c)))
            starts.append(cp)
            waits.append(cp)
            continue
        if kind == "all":
            own = pltpu.make_async_copy(src, land.at[dev], local_sems.at[a])
            starts.append(own)
            waits.append(own)
            for k, (fx, fy, fc) in enumerate(ALL_RELS):
                px, py, pc = (1 - x) if fx else x, (1 - y) if fy else y, (1 - c) if fc else c
                starts.append(pltpu.make_async_remote_copy(src_ref=src, dst_ref=land.at[dev], **on(k, (px, py, pc))))
                waits.append(pltpu.make_async_remote_copy(src_ref=src, dst_ref=land.at[4 * px + 2 * py + pc],
                                                          **on(k, (px, py, pc))))
            continue
        rows, cols = shards[a]
        if kind == "gather":
            own = pltpu.make_async_copy(src, _window(land, axes[a], me, rows, cols), local_sems.at[a])
        else:
            own = pltpu.make_async_copy(_window(src, axes[a], me, rows, cols), land.at[3], local_sems.at[a])
        starts.append(own)
        waits.append(own)
        for j, (fx, fy) in enumerate(CHIP_RELS):
            px, py = (1 - x) if fx else x, (1 - y) if fy else y
            peer = 2 * px + py
            if kind == "gather":
                starts.append(pltpu.make_async_remote_copy(src_ref=src, dst_ref=_window(land, axes[a], me, rows, cols),
                                                           **on(j, (px, py, c))))
                waits.append(pltpu.make_async_remote_copy(src_ref=src, dst_ref=_window(land, axes[a], peer, rows, cols),
                                                          **on(j, (px, py, c))))
            else:
                cp = pltpu.make_async_remote_copy(src_ref=_window(src, axes[a], peer, rows, cols), dst_ref=land.at[j],
                                                  **on(j, (px, py, c)))
                starts.append(cp)
                waits.append(cp)
    return starts, waits


HBM = pl.BlockSpec(memory_space=pltpu.HBM)
SEM = pl.BlockSpec(memory_space=pltpu.SEMAPHORE)


def _shard_shapes(kind, arrs, axes):
    if kind != "scatter":
        return [a.shape for a in arrs]
    return [(a.shape[0] // N_CHIPS, a.shape[1]) if ax == 0 else (a.shape[0], a.shape[1] // N_CHIPS) for a, ax in zip(arrs, axes)]


def _land_shapes(kind, arrs, axes):
    if kind == "gather":
        return [(N_CHIPS * a.shape[0], a.shape[1]) if ax == 0 else (a.shape[0], N_CHIPS * a.shape[1]) for a, ax in zip(arrs, axes)]
    if kind == "scatter":
        return [(N_CHIPS,) + s for s in _shard_shapes(kind, arrs, axes)]
    return [a.shape if kind == "sibling" else (len(ALL_RELS) + 1,) + a.shape for a in arrs]


def _exchange_start(name, kind, arrs, axes, after=()):
    n, n_after = len(arrs), len(after)
    shards = _shard_shapes(kind, arrs, axes)
    land_shapes = _land_shapes(kind, arrs, axes)
    lands = [lax.empty(s, a.dtype) for s, a in zip(land_shapes, arrs)]

    def kern(*refs):
        outs = refs[2 * n + n_after:]
        starts, _ = _copies(kind, refs[:n], refs[n:2 * n], shards, axes, outs[0], outs[1], outs[2])
        for cp in starts:
            cp.start()
        outs[-1][...] = jnp.zeros_like(outs[-1])

    kept = [pltpu.HBM(a.shape, a.dtype) for a in arrs] + [pltpu.HBM(s, a.dtype) for s, a in zip(land_shapes, arrs)]
    n_sems = N_PEERS[kind] * n
    res = pl.pallas_call(
        kern, name=name, in_specs=[HBM] * (2 * n) + [ANY] * n_after,
        out_specs=[SEM] * 3 + [HBM] * (2 * n) + [pl.BlockSpec(memory_space=pltpu.VMEM)],
        out_shape=[pltpu.SemaphoreType.DMA((n_sems,)), pltpu.SemaphoreType.DMA((n_sems,)), pltpu.SemaphoreType.DMA((n,))]
        + kept + [jax.ShapeDtypeStruct((SUBLANE, LANE), F32)],
        input_output_aliases={i: 3 + i for i in range(2 * n)},
        compiler_params=pltpu.CompilerParams(has_side_effects=pltpu.SideEffectType.DATAFLOW_SIDE_EFFECTING),
    )(*[pltpu.with_memory_space_constraint(a, pltpu.HBM) for a in list(arrs) + lands], *after)
    return res[:3], res[3:3 + n], res[3 + n:3 + 2 * n], res[-1]


def _exchange_wait(name, kind, started, axes, after, sources_too=False):
    sems, srcs, lands, _ = started
    n, n_after = len(srcs), len(after)
    shards = _shard_shapes(kind, srcs, axes)

    def kern(*refs):
        sem_refs = refs[2 * n:2 * n + 3]
        _, waits = _copies(kind, refs[:n], refs[n:2 * n], shards, axes, *sem_refs)
        for cp in waits:
            cp.wait()

    res = pl.pallas_call(
        kern, name=name, in_specs=[HBM] * (2 * n) + [SEM] * 3 + [ANY] * n_after, out_specs=[HBM] * (2 * n),
        out_shape=[pltpu.HBM(a.shape, a.dtype) for a in list(srcs) + list(lands)],
        input_output_aliases={i: i for i in range(2 * n)},
        compiler_params=pltpu.CompilerParams(has_side_effects=pltpu.SideEffectType.DATAFLOW_SIDE_EFFECTING),
    )(*srcs, *lands, *sems, *after)
    return (res[:n], res[n:]) if sources_too else res[n:]


def _sum_devices(parts):
    def kern(p_ref, o_ref):
        acc = p_ref[0]
        for d in range(1, parts.shape[0]):
            acc = acc + p_ref[d]
        o_ref[...] = acc

    return pl.pallas_call(kern, name="sum_devices", out_shape=jax.ShapeDtypeStruct(parts.shape[1:], F32),
                          compiler_params=pltpu.CompilerParams(vmem_limit_bytes=VMEM_LIMIT_BYTES))(parts)


def _sum_slots(name, parts):
    _, rows, cols = parts.shape
    tr = _pick(rows, (ROW_TILE, 128, 64, 32))

    def kern(p_ref, o_ref):
        o_ref[...] = ((p_ref[3].astype(F32) + p_ref[0].astype(F32)) + p_ref[1].astype(F32)) + p_ref[2].astype(F32)

    return pl.pallas_call(kern, name=name, grid=(rows // tr,),
                          in_specs=[pl.BlockSpec((N_CHIPS, tr, cols), lambda i: (0, i, 0))],
                          out_specs=pl.BlockSpec((tr, cols), lambda i: (i, 0)),
                          out_shape=jax.ShapeDtypeStruct((rows, cols), F32), compiler_params=_params("arbitrary"))(parts)


def _adamw_math(g, w, m, v):
    m2 = ADAM_B1 * m + (1.0 - ADAM_B1) * g
    v2 = ADAM_B2 * v + (1.0 - ADAM_B2) * (g * g)
    m_hat = m2 / (1.0 - ADAM_B1 ** ADAM_STEP)
    v_hat = v2 / (1.0 - ADAM_B2 ** ADAM_STEP)
    return -ADAM_LR * (m_hat / (jnp.sqrt(v_hat) + ADAM_EPS) + ADAM_WD * w), m2, v2


def _adamw(name, parts, w, m, v):
    rows, cols = w.shape
    tr = rows if rows * cols <= WHOLE_ELEMS else _pick(rows, (ROW_TILE, 352, 128, 64, 32, 8))
    n = len(parts)

    def kern(*refs):
        g = refs[0][:, pl.ds(0, cols)]
        for p in refs[1:n]:
            g = g + p[:, pl.ds(0, cols)]
        d, m2, v2 = _adamw_math(g, refs[n][...], refs[n + 1][...], refs[n + 2][...])
        refs[n + 3][...] = g
        refs[n + 4][...] = d
        refs[n + 5][...] = m2
        refs[n + 6][...] = v2

    spec = pl.BlockSpec((tr, cols), lambda i: (i, 0))
    return pl.pallas_call(kern, name=name, grid=(rows // tr,),
                          in_specs=[pl.BlockSpec((tr, p.shape[1]), lambda i: (i, 0)) for p in parts] + [spec] * 3,
                          out_specs=[spec] * 4, out_shape=[jax.ShapeDtypeStruct((rows, cols), F32)] * 4,
                          compiler_params=_params("arbitrary"))(*parts, w, m, v)


def _adamw_many(name, gs, ws, ms, vs):
    n = len(gs)

    def kern(*refs):
        for p in range(n):
            d, m2, v2 = _adamw_math(refs[p][...], refs[n + p][...], refs[2 * n + p][...], refs[3 * n + p][...])
            refs[4 * n + p][...] = d
            refs[5 * n + p][...] = m2
            refs[6 * n + p][...] = v2

    res = pl.pallas_call(kern, name=name, out_shape=[jax.ShapeDtypeStruct(w.shape, F32) for w in ws] * 3,
                         compiler_params=pltpu.CompilerParams(vmem_limit_bytes=VMEM_LIMIT_BYTES))(*gs, *ws, *ms, *vs)
    return res[:n], res[n:2 * n], res[2 * n:]


def _pack(arrs):
    parts, rows = [], []
    for a in arrs:
        r = _round_up(-(-a.size // LANE), SUBLANE)
        parts.append(jnp.pad(a.reshape(-1).astype(F32), (0, r * LANE - a.size)).reshape(r, LANE))
        rows.append(r)
    return jnp.concatenate(parts, axis=0), rows


def _unpack(buf, rows, shapes):
    out, r0 = [], 0
    for r, s in zip(rows, shapes):
        size = math.prod(s)
        out.append(buf[r0:r0 + r].reshape(-1)[:size].reshape(s))
        r0 += r
    return out


def kernel(x, norm_ffn1, ffn1_w1, ffn1_w3, ffn1_w2, norm_mix, w_in, conv_w, conv_b, conv_ln_g, conv_ln_b, conv_out_g, ssm_A_re, ssm_A_im, ssm_log_dt, ssm_B_re, ssm_B_im, ssm_C_re, ssm_C_im, ssm_D, ssm_glu_w, ssm_glu_b, ssm_out_g, w_out, norm_ffn2, ffn2_w1, ffn2_w3, ffn2_w2, norm_final, loss_target, m_norm_ffn1, m_ffn1_w1, m_ffn1_w3, m_ffn1_w2, m_norm_mix, m_w_in, m_conv_w, m_conv_b, m_conv_ln_g, m_conv_ln_b, m_conv_out_g, m_ssm_A_re, m_ssm_A_im, m_ssm_log_dt, m_ssm_B_re, m_ssm_B_im, m_ssm_C_re, m_ssm_C_im, m_ssm_D, m_ssm_glu_w, m_ssm_glu_b, m_ssm_out_g, m_w_out, m_norm_ffn2, m_ffn2_w1, m_ffn2_w3, m_ffn2_w2, m_norm_final, v_norm_ffn1, v_ffn1_w1, v_ffn1_w3, v_ffn1_w2, v_norm_mix, v_w_in, v_conv_w, v_conv_b, v_conv_ln_g, v_conv_ln_b, v_conv_out_g, v_ssm_A_re, v_ssm_A_im, v_ssm_log_dt, v_ssm_B_re, v_ssm_B_im, v_ssm_C_re, v_ssm_C_im, v_ssm_D, v_ssm_glu_w, v_ssm_glu_b, v_ssm_out_g, v_w_out, v_norm_ffn2, v_ffn2_w1, v_ffn2_w3, v_ffn2_w2, v_norm_final):
    given = dict(locals())
    wts = {n: given[n] for n in WEIGHTS}
    n_seq, seq, d = x.shape
    n_rows = n_seq * seq
    xf = x.reshape(n_rows, d)
    tgt = loss_target.reshape(n_rows, d)
    row = lambda a: a.reshape(1, -1)

    f = ffn1_w1.shape[-1]
    fp = _round_up(f, LANE)
    held = lambda n, a: a[0].T if n in TRANSPOSED else a[0]
    shards = []
    for n in BIG:
        s = held(n, wts[n]).astype(BF16)
        if n.startswith('ffn'):
            s = jnp.pad(s, ((0, fp - f), (0, 0)))
        shards.append(s)
    n_taps, c_shard = conv_w.shape[1], conv_w.shape[2]
    shards.append(jnp.pad(conv_w[0], ((0, HALO - n_taps), (0, 0))))
    shard_of = dict(zip(BIG + ['conv_w'], shards))
    axis_of = dict(BIG_AXIS, conv_w=1)
    groups = [['ffn1_w1', 'ffn1_w3'], ['ffn1_w2', 'w_in', 'conv_w', 'ssm_glu_w', 'w_out'], ['ffn2_w1', 'ffn2_w3', 'ffn2_w2']]
    fetch, tok = [], []
    for k, names in enumerate(groups):
        fetch.append(_exchange_start("gather%d_send" % k, "gather", [shard_of[n] for n in names],
                                     [axis_of[n] for n in names], tok))
        tok = [fetch[-1][3]]
    full = {}

    def arrive(k, after):
        lands = _exchange_wait("gather%d_recv" % k, "gather", fetch[k], [axis_of[n] for n in groups[k]], after)
        full.update(zip(groups[k], lands))

    h1, h1_t = _rms_fwd("ffn1_rms", xf, norm_ffn1)
    arrive(0, tok + [h1])

    _, n_grp, n_state = ssm_A_re.shape
    grp = ssm_B_re.shape[-1]
    ns = n_grp * n_state
    c_ssm = n_grp * grp
    lr, li = ssm_A_re.reshape(1, ns), ssm_A_im.reshape(1, ns)
    ldt = jnp.repeat(ssm_log_dt.reshape(n_grp), n_state).reshape(1, ns)
    btr = ssm_B_re[0].transpose(2, 0, 1).reshape(grp, ns)
    bti = ssm_B_im[0].transpose(2, 0, 1).reshape(grp, ns)
    ctr = ssm_C_re[0].transpose(1, 0, 2).reshape(grp, ns)
    cti = ssm_C_im[0].transpose(1, 0, 2).reshape(grp, ns)
    scan_tile = _pick(seq, (SCAN_TILE,))
    _, _, bbr, bbi, seg_up, seg_down, pw, pw_falling = _s5_params_fwd(lr, li, ldt, btr, bti, scan_tile // SUBLANE)
    nb = c_ssm // LANE
    sb, gpb = ns // nb, n_grp // nb
    diag = (jnp.arange(LANE)[:, None] // grp) == (jnp.arange(sb)[None, :] // n_state)

    def spread(t):
        return jnp.where(diag, jnp.tile(t.reshape(grp, nb, sb).transpose(1, 0, 2), (1, gpb, 1)), 0.0)

    def gather_diag(t):
        return (t * diag).reshape(nb, gpb, grp, sb).sum(1).transpose(1, 0, 2).reshape(grp, ns)

    def interleave(re, im):
        return jnp.stack([re.reshape(-1, nb, sb), im.reshape(-1, nb, sb)], axis=2).reshape(-1, 2 * ns)

    bdc = jnp.concatenate([spread(bbr), spread(bbi)], axis=2).astype(BF16)
    cdc = jnp.concatenate([spread(ctr).transpose(0, 2, 1), -spread(cti).transpose(0, 2, 1)], axis=1).astype(BF16)
    rowi = jnp.arange(SUBLANE)[:, None]
    pwf, pwc = interleave(pw[:, :ns], pw[:, ns:]), interleave(pw[:, :ns], -pw[:, ns:])
    tabs_f = [jnp.where(rowi >= s, pwf[s - 1][None, :], 0.0) for s in (1, 2, 4)] + [pwf]
    tabs_b = [jnp.where(rowi <= SUBLANE - 1 - s, pwc[s - 1][None, :], 0.0) for s in (1, 2, 4)]
    tabs_b.append(interleave(pw_falling[:, :ns], -pw_falling[:, ns:]))
    fix_f = jnp.repeat(interleave(seg_up[:, :ns], seg_up[:, ns:]), SUBLANE, axis=0)
    fix_b = jnp.repeat(interleave(seg_down[:, :ns], -seg_down[:, ns:]), SUBLANE, axis=0)
    c_conv = conv_b.shape[1]
    u_blk = 2 * c_conv // LANE

    a1, b1, z1 = _ffn_up("ffn1_up", h1, full['ffn1_w1'], full['ffn1_w3'])
    arrive(1, [z1])
    x1, h2, h2_t = _mm("ffn1_down", z1, full['ffn1_w2'], 1, 0, addend=xf, alpha=0.5, post=_post_rms(norm_mix))
    saved1 = (h1_t, a1, b1, z1)
    cw = full['conv_w']
    proj = _mm("mix_in", h2, full['w_in'], 1, 0, F32)
    assert c_conv == c_ssm and proj.shape[1] == 3 * c_conv
    cpre, an = _conv_fwd(proj, cw, conv_b, conv_ln_g, conv_ln_b, conv_out_g, seq)
    xs, ypre, yg = _s5_fwd(proj, u_blk, bdc, cdc, fix_f, tabs_f, ssm_D, seq, sb)
    q0 = _mm("s5_gate", yg, full['ssm_glu_w'], 1, 0, F32)
    sn = _s5_post2(yg, q0, ssm_glu_b, ssm_out_g)
    wo = full['w_out']
    mixed = jnp.concatenate([an, sn], axis=1)
    x2, h3, h3_t = _mm("mix_out", mixed, wo, 1, 0, addend=x1, post=_post_rms(norm_ffn2))
    arrive(2, [x2])
    a3, b3, z3 = _ffn_up("ffn2_up", h3, full['ffn2_w1'], full['ffn2_w3'])
    saved2 = (h3_t, a3, b3, z3)
    dx3, dx3_t, loss_row, d_norm_final = _mm("ffn2_down", z3, full['ffn2_w2'], 1, 0, addend=x2, alpha=0.5,
                                             post=_post_loss(row(norm_final), tgt))

    g = {}
    dx2, g['norm_ffn2'], sent = _ffn_bwd("ffn2", x2, norm_ffn2, full['ffn2_w1'], full['ffn2_w3'], full['ffn2_w2'], saved2,
                                         dx3, dx3_t, early=False)
    dmixed = _mm("mix_dmixed", dx2, wo, 1, 1, F32)
    dwo = _mm("mix_dwo", mixed, dx2, 0, 0, BF16)
    dq, dyg1, g['ssm_out_g'], g['ssm_glu_b'] = _s5_post2_bwd(dmixed, yg, q0, ssm_glu_b, ssm_out_g)
    dyg2 = _mm("s5_dgate", dq, full['ssm_glu_w'], 1, 1, F32)
    dwg = _mm("s5_dwg", yg, dq, 0, 0, BF16)
    dypre, du_skip, g['ssm_D'] = _s5_post1_bwd(dyg1, dyg2, ypre, proj, ssm_D)
    du, dabar, dbdc, dcdc = _s5_bwd(dypre, du_skip, xs, proj, u_blk, bdc, cdc, fix_b, tabs_b, seq, sb)
    dabar = dabar.reshape(nb, 2, sb)
    dlr, dli, dldt, dbtr, dbti = _s5_params_bwd(lr, li, ldt, btr, bti, dabar[:, 0].reshape(1, ns), dabar[:, 1].reshape(1, ns),
                                                gather_diag(dbdc[:, :, :sb]), gather_diag(dbdc[:, :, sb:]))
    g['ssm_A_re'], g['ssm_A_im'] = dlr, dli
    g['ssm_log_dt'] = dldt.reshape(n_grp, n_state).sum(axis=1)
    g['ssm_B_re'] = dbtr.reshape(grp, n_grp, n_state).transpose(1, 2, 0)
    g['ssm_B_im'] = dbti.reshape(grp, n_grp, n_state).transpose(1, 2, 0)
    g['ssm_C_re'] = gather_diag(dcdc[:, :sb].transpose(0, 2, 1)).reshape(grp, n_grp, n_state).transpose(1, 0, 2)
    g['ssm_C_im'] = -gather_diag(dcdc[:, sb:].transpose(0, 2, 1)).reshape(grp, n_grp, n_state).transpose(1, 0, 2)
    dc, g['conv_out_g'], g['conv_ln_g'], g['conv_ln_b'], g['conv_b'] = _conv_bwd_rows(dmixed, cpre, conv_ln_g, conv_ln_b,
                                                                                    conv_out_g)
    dval, dgate, dcw = _conv_bwd_taps(proj, dc, cw, seq)
    dproj = jnp.concatenate([dval, dgate, du], axis=1)
    dwin = _mm("mix_dwin", h2_t, dproj, 1, 0, BF16)
    sent['w_out ssm_glu_w w_in'] = (_exchange_start("mix_send", "scatter", [dwo, dwg, dwin], [0, 0, 1]), [0, 0, 1])
    dx1, dx1_t, g['norm_mix'] = _mm("mix_dh", dproj, full['w_in'], 1, 1, after=[sent['w_out ssm_glu_w w_in'][0][3]],
                                    post=_post_rms_bwd(x1, norm_mix, dx2))
    dx0, g['norm_ffn1'], sent1 = _ffn_bwd("ffn1", xf, norm_ffn1, full['ffn1_w1'], full['ffn1_w3'], full['ffn1_w2'], saved1,
                                          dx1, dx1_t, early=True)
    sent.update(sent1)
    g['norm_final'] = d_norm_final
    g['conv_w'] = dcw[:n_taps]

    small_shapes = [(n_taps, c_conv) if n == 'conv_w' else wts[n].shape for n in SMALL]
    buf, buf_rows = _pack([g[n] for n in SMALL] + [loss_row])
    to_all = _exchange_start("small_send", "all", [buf], [0])
    slots = {}
    for names, (started, axes) in sent.items():
        lands = _exchange_wait(names.replace(' ', '_') + "_recv", "scatter", started, axes, after=[dx0, to_all[3]])
        slots.update(zip(names.split(), lands))
    sums = [_sum_slots("sum_" + n, slots[n]) for n in BIG]
    to_sibling = _exchange_start("sums_send", "sibling", sums, [0] * len(sums))
    from_all = _exchange_wait("small_recv", "all", to_all, [0], after=[to_sibling[3]])[0]
    total = _unpack(_sum_devices(from_all), buf_rows, small_shapes + [(1, LANE)])
    loss = total[-1][0, 0]
    grads = dict(zip(SMALL, total[:-1]))
    chip = 2 * lax.axis_index("x") + lax.axis_index("y")
    grads['conv_w'] = lax.dynamic_slice_in_dim(grads['conv_w'], chip * c_shard, c_shard, axis=1)[None]
    flat = lambda a: a.reshape(-1, a.shape[-1])
    small = _adamw_many("adamw_small", *[[flat(src[p + n]) for n in SMALL]
                                         for src, p in ((grads, ''), (given, ''), (given, 'm_'), (given, 'v_'))])
    deltas, new_m, new_v = ({n: o.reshape(wts[n].shape) for n, o in zip(SMALL, outs)} for outs in small)

    sums, theirs = _exchange_wait("sums_recv", "sibling", to_sibling, [0] * len(sums), after=[new_v[SMALL[-1]]],
                                  sources_too=True)
    for n, mine, other in zip(BIG, sums, theirs):
        grads[n], deltas[n], new_m[n], new_v[n] = (
            (o.T if n in TRANSPOSED else o)[None]
            for o in _adamw("adamw_" + n, [mine, other], held(n, given[n]), held(n, given['m_' + n]), held(n, given['v_' + n])))

    return (loss, dx0.reshape(x.shape), *[grads[n] for n in WEIGHTS], *[deltas[n] for n in WEIGHTS],
            *[new_m[n] for n in WEIGHTS], *[new_v[n] for n in WEIGHTS])
```

```python
import math
from typing import Callable, NamedTuple

import jax
import jax.numpy as jnp
from jax import lax
from jax.experimental import pallas as pl
from jax.experimental.pallas import tpu as pltpu

F32 = jnp.float32
BF16 = jnp.bfloat16
EPS = 1e-6
ADAM_LR, ADAM_B1, ADAM_B2, ADAM_EPS, ADAM_WD, ADAM_STEP = 0.001, 0.9, 0.999, 1e-08, 0.01, 10
MESH = pl.DeviceIdType.MESH
ANY = pl.BlockSpec(memory_space=pl.ANY)
LANE = 128
SUBLANE = 8
VMEM_LIMIT_BYTES = 56 << 20
ROW_TILE = 256
ROW_TILE_ELEMS = 512 * 1024
WHOLE_ELEMS = 512 * 1024
WHOLE_WEIGHT_BYTES = 8 << 20
FFN_ROWS = 256
CONV_TILE = 256
CONV_SUB = 32
HALO = 32
SCAN_TILE = 256
SCAN_COLS = 512
N_CHIPS = 4
CHIP_RELS = ((1, 0), (0, 1), (1, 1))
NT = (((1,), (1,)), ((), ()))
GELU_K = math.sqrt(2.0 / math.pi)
GELU_C = 0.044715

WEIGHTS = ['norm_ffn1', 'ffn1_w1', 'ffn1_w3', 'ffn1_w2', 'norm_mix', 'w_in', 'conv_w', 'conv_b', 'conv_ln_g', 'conv_ln_b',
           'conv_out_g', 'ssm_A_re', 'ssm_A_im', 'ssm_log_dt', 'ssm_B_re', 'ssm_B_im', 'ssm_C_re', 'ssm_C_im', 'ssm_D',
           'ssm_glu_w', 'ssm_glu_b', 'ssm_out_g', 'w_out', 'norm_ffn2', 'ffn2_w1', 'ffn2_w3', 'ffn2_w2', 'norm_final']
BIG = ['ffn1_w1', 'ffn1_w3', 'ffn1_w2', 'w_in', 'ssm_glu_w', 'w_out', 'ffn2_w1', 'ffn2_w3', 'ffn2_w2']
BIG_AXIS = {'ffn1_w1': 0, 'ffn1_w3': 0, 'ffn1_w2': 0, 'w_in': 1, 'ssm_glu_w': 0, 'w_out': 0, 'ffn2_w1': 0, 'ffn2_w3': 0,
            'ffn2_w2': 0}
TRANSPOSED = ('ffn1_w1', 'ffn1_w3', 'ffn2_w1', 'ffn2_w3')
SMALL = [n for n in WEIGHTS if n not in BIG]


def _round_up(n, m):
    return -(-n // m) * m


def _pick(n, cands):
    for c in cands:
        if c <= n and n % c == 0:
            return c
    return n


def _params(*sem):
    return pltpu.CompilerParams(dimension_semantics=sem, vmem_limit_bytes=VMEM_LIMIT_BYTES)


def _rms_r(x):
    return lax.rsqrt(jnp.mean(x * x, axis=-1, keepdims=True) + EPS)


def _rms_bwd(x, r, g, dy):
    dyg = dy * g
    return r * dyg - x * (r * r * r) * jnp.mean(x * dyg, axis=-1, keepdims=True)


def _sigmoid(x):
    return jax.nn.sigmoid(x)


def _dsilu(a, s):
    return s * (1.0 + a * (1.0 - s))


def _gelu(x):
    return 0.5 * x * (1.0 + jnp.tanh(GELU_K * (x + GELU_C * x * x * x)))


def _dgelu(x):
    t = jnp.tanh(GELU_K * (x + GELU_C * x * x * x))
    return 0.5 * (1.0 + t) + 0.5 * x * (1.0 - t * t) * GELU_K * (1.0 + 3.0 * GELU_C * x * x)


def _colsum(v):
    return jnp.sum(v, axis=0, keepdims=True)


def _rowwise(name, body, n_rows, row_ins, par_ins, row_outs, acc_outs, after=()):
    widest = max([w for (_, w, _) in row_ins] + [w for (w, _) in row_outs])
    tt = _pick(n_rows, [t for t in (1024, 512, 256, 128, 64, 32, 16, 8) if t * widest <= ROW_TILE_ELEMS])
    in_specs = [pl.BlockSpec((tt, w), lambda i, cb=cb: (i, cb)) for (_, w, cb) in row_ins]
    in_specs += [pl.BlockSpec(p.shape, lambda i: (0, 0)) for p in par_ins] + [ANY] * len(after)
    out_specs = [pl.BlockSpec((tt, w), lambda i: (i, 0)) for (w, _) in row_outs]
    out_specs += [pl.BlockSpec((r, w), lambda i: (0, 0)) for (r, w) in acc_outs]
    out_shape = [jax.ShapeDtypeStruct((n_rows, w), dt) for (w, dt) in row_outs]
    out_shape += [jax.ShapeDtypeStruct((r, w), F32) for (r, w) in acc_outs]
    n_in, n_ro = len(row_ins) + len(par_ins), len(row_outs)
    o0 = n_in + len(after)

    def kern(*refs):
        accs = refs[o0 + n_ro:]
        if accs:
            @pl.when(pl.program_id(0) == 0)
            def _():
                for a in accs:
                    a[...] = jnp.zeros_like(a)
        body(refs[:n_in], refs[o0:o0 + n_ro], accs)

    return pl.pallas_call(kern, name=name, grid=(n_rows // tt,), in_specs=in_specs, out_specs=out_specs, out_shape=out_shape,
                          compiler_params=_params("arbitrary"))(*[a for a, _, _ in row_ins], *par_ins, *after)


class Post(NamedTuple):
    rows: list
    gains: list
    outs: list
    t_outs: list
    sums: list
    fn: Callable


def _post_rms(gain):
    def fn(r, rows, gains):
        h = r * _rms_r(r) * gains[0]
        return [r, h, h], []

    return Post([], [gain], [F32, BF16], [BF16], [], fn)


def _post_rms_bwd(x, gain, dres):
    def fn(dh, rows, gains):
        r = _rms_r(rows[0])
        dx = rows[1] + _rms_bwd(rows[0], r, gains[0], dh)
        return [dx, dx], [_colsum(dh * rows[0] * r)]

    return Post([x, dres], [gain], [F32], [BF16], [x.shape[1]], fn)


def _post_loss(gain, tgt):
    d = tgt.shape[1]

    def fn(xv, rows, gains):
        r = _rms_r(xv)
        e = xv * r * gains[0] - rows[0]
        sq = jnp.sum(jnp.sum(e * e, axis=-1, keepdims=True), axis=0, keepdims=True)
        dy = e * (1.0 / d)
        dx = _rms_bwd(xv, r, gains[0], dy)
        return [dx, dx], [jnp.broadcast_to(sq * (0.5 / d), (1, LANE)), _colsum(dy * xv * r)]

    return Post([tgt], [gain], [F32], [BF16], [LANE, d], fn)


def _mm(name, a, b, ca, cb, out_dtype=F32, addend=None, alpha=1.0, a_cols=None, after=(), post=None, transposed=False):
    a_start, a_width = a_cols if a_cols else (0, a.shape[1])
    m, k = (a.shape[0], a_width) if ca == 1 else (a_width, a.shape[0])
    n = b.shape[1 - cb]
    assert b.shape[cb] == k, (name, a.shape, b.shape)
    tn = _pick(n, (1024, 768, 512, 384, 256, 128))
    whole_b = bool(post) and k * tn * b.dtype.itemsize <= WHOLE_WEIGHT_BYTES
    if whole_b:
        tk = k
        tm = _pick(m, (512, 256, 128))
    else:
        tm = _pick(m, (512, 256, 128) if post else (1024, 512, 256, 128))
        deep = (4096,) if a.dtype == BF16 and b.dtype == BF16 else ()
        tk = _pick(k, deep + (2048, 1024, 768, 512, 256, 128) if k >= 4096 and not post else (1024, 768, 512, 256, 128))
    nk = k // tk
    if ca == 1:
        assert a_start % tk == 0
        a_spec = pl.BlockSpec((tm, tk), lambda i, j, kk: (i, kk + a_start // tk))
    else:
        assert a_start % tm == 0
        a_spec = pl.BlockSpec((tk, tm), lambda i, j, kk: (kk, i + a_start // tm))
    b_mode = dict(pipeline_mode=pl.Buffered(1)) if whole_b else {}
    b_spec = (pl.BlockSpec((tk, tn), lambda i, j, kk: (kk, j), **b_mode) if cb == 0 else
              pl.BlockSpec((tn, tk), lambda i, j, kk: (j, kk), **b_mode))
    o_spec = pl.BlockSpec((tm, tn), lambda i, j, kk: (i, j))
    t_spec = pl.BlockSpec((tn, tm), lambda i, j, kk: (j, i))
    fixed = lambda w: pl.BlockSpec((1, w), lambda i, j, kk: (0, 0))
    ins, in_specs = [a, b], [a_spec, b_spec]
    if addend is not None:
        ins.append(addend)
        in_specs.append(o_spec)
    n_plain = len(ins)
    n_rows, n_gains = (len(post.rows), len(post.gains)) if post else (0, 0)
    if post:
        assert tn == n, name
        ins += post.rows + post.gains
        in_specs += [o_spec] * n_rows + [fixed(n)] * n_gains
    ins += list(after)
    in_specs += [ANY] * len(after)
    n_in = len(ins)
    if post:
        n_straight, n_vals = len(post.outs), len(post.outs) + len(post.t_outs)
        out_specs = [o_spec] * n_straight + [t_spec] * len(post.t_outs) + [fixed(w) for w in post.sums]
        out_shape = [jax.ShapeDtypeStruct((m, n), dt) for dt in post.outs] + [jax.ShapeDtypeStruct((n, m), dt) for dt in post.t_outs]
        out_shape += [jax.ShapeDtypeStruct((1, w), F32) for w in post.sums]
    elif transposed:
        out_specs, out_shape = [t_spec], [jax.ShapeDtypeStruct((n, m), out_dtype)]
    else:
        out_specs, out_shape = [o_spec], [jax.ShapeDtypeStruct((m, n), out_dtype)]
    n_out = len(out_specs)
    dims = (((ca,), (cb,)), ((), ()))

    def emit(refs, r):
        if alpha != 1.0:
            r = r * alpha
        if addend is not None:
            r = r + refs[2][...].astype(F32)
        outs = refs[n_in:n_in + n_out]
        if post is None:
            outs[0][...] = (r.T if transposed else r).astype(out_dtype)
            return
        vals, incs = post.fn(r, [q[...] for q in refs[n_plain:n_plain + n_rows]],
                             [q[...] for q in refs[n_plain + n_rows:n_plain + n_rows + n_gains]])
        for at, (o_ref, val) in enumerate(zip(outs, vals)):
            o_ref[...] = (val if at < n_straight else val.T).astype(o_ref.dtype)
        for s_ref, inc in zip(outs[n_vals:], incs):
            s_ref[...] += inc

    def kern(*refs):
        kk = pl.program_id(2)
        if post and post.sums:
            @pl.when(jnp.logical_and(jnp.logical_and(pl.program_id(0) == 0, pl.program_id(1) == 0), kk == 0))
            def _():
                for s_ref in refs[n_in + n_vals:n_in + n_out]:
                    s_ref[...] = jnp.zeros_like(s_ref)

        dot = lambda: lax.dot_general(refs[0][...].astype(BF16), refs[1][...].astype(BF16), dims,
                                      preferred_element_type=F32)
        if nk == 1:
            emit(refs, dot())
            return
        acc_ref = refs[-1]

        @pl.when(kk == 0)
        def _():
            acc_ref[...] = jnp.zeros_like(acc_ref)

        acc_ref[...] += dot()

        @pl.when(kk == nk - 1)
        def _():
            emit(refs, acc_ref[...])

    res = pl.pallas_call(kern, name=name, grid=(m // tm, n // tn, nk), in_specs=in_specs, out_specs=out_specs,
                         out_shape=out_shape, scratch_shapes=[] if nk == 1 else [pltpu.VMEM((tm, tn), F32)],
                         compiler_params=_params("arbitrary", "arbitrary", "arbitrary"))(*ins)
    return res if post else res[0]


def _rms_fwd(name, x, g):
    t, d = x.shape
    tt = _pick(t, (ROW_TILE, LANE))

    def kern(x_ref, g_ref, h_ref, ht_ref):
        xv = x_ref[...]
        h = xv * _rms_r(xv) * g_ref[...]
        h_ref[...] = h.astype(BF16)
        ht_ref[...] = h.T.astype(BF16)

    return pl.pallas_call(kern, name=name, grid=(t // tt,),
                          in_specs=[pl.BlockSpec((tt, d), lambda i: (i, 0)), pl.BlockSpec((1, d), lambda i: (0, 0))],
                          out_specs=[pl.BlockSpec((tt, d), lambda i: (i, 0)), pl.BlockSpec((d, tt), lambda i: (0, i))],
                          out_shape=[jax.ShapeDtypeStruct((t, d), BF16), jax.ShapeDtypeStruct((d, t), BF16)],
                          compiler_params=_params("arbitrary"))(x, g)


def _ffn_up(name, h, w1, w3):
    t, d = h.shape
    ff = w1.shape[0]
    tm, tn = _pick(t, (1024, 512, 256, 128)), _pick(ff, (1024, 768, 512, 256, 128))

    def kern(h_ref, w1_ref, w3_ref, a_ref, b_ref, z_ref):
        hv = h_ref[...]
        a = lax.dot_general(hv, w1_ref[...], NT, preferred_element_type=F32)
        b = lax.dot_general(hv, w3_ref[...], NT, preferred_element_type=F32)
        a_ref[...] = a.astype(BF16)
        b_ref[...] = b.astype(BF16)
        z_ref[...] = (a * _sigmoid(a) * b).astype(BF16)

    w_spec = pl.BlockSpec((tn, d), lambda i, j: (j, 0))
    o_spec = pl.BlockSpec((tm, tn), lambda i, j: (i, j))
    return pl.pallas_call(kern, name=name, grid=(t // tm, ff // tn),
                          in_specs=[pl.BlockSpec((tm, d), lambda i, j: (i, 0)), w_spec, w_spec], out_specs=[o_spec] * 3,
                          out_shape=[jax.ShapeDtypeStruct((t, ff), BF16)] * 3,
                          compiler_params=_params("arbitrary", "arbitrary"))(h, w1, w3)


def _ffn_dglu(name, dxo, w2, a, b, after=()):
    t, d = dxo.shape
    ff = w2.shape[0]
    tm = _pick(t, (FFN_ROWS, 128))

    def kern(dx_ref, w2_ref, a_ref, b_ref, *rest):
        da_ref, db_ref = rest[-2:]
        dz = lax.dot_general(dx_ref[...].astype(BF16), w2_ref[...], NT, preferred_element_type=F32) * 0.5
        av, bv = a_ref[...].astype(F32), b_ref[...].astype(F32)
        s = _sigmoid(av)
        da_ref[...] = (dz * bv * _dsilu(av, s)).astype(BF16)
        db_ref[...] = (dz * av * s).astype(BF16)

    o_spec = pl.BlockSpec((tm, ff), lambda i: (i, 0))
    return pl.pallas_call(kern, name=name, grid=(t // tm,),
                          in_specs=[pl.BlockSpec((tm, d), lambda i: (i, 0)), pl.BlockSpec((ff, d), lambda i: (0, 0)),
                                    o_spec, o_spec] + [ANY] * len(after),
                          out_specs=[o_spec] * 2, out_shape=[jax.ShapeDtypeStruct((t, ff), BF16)] * 2,
                          compiler_params=_params("arbitrary"))(dxo, w2, a, b, *after)


def _ffn_dh(name, da, db, w1, w3, x, g, dres, after=()):
    t, d = x.shape
    ff = da.shape[1]
    tm = _pick(t, (2 * FFN_ROWS, 128))

    def kern(da_ref, db_ref, w1_ref, w3_ref, x_ref, g_ref, dres_ref, *rest):
        dx_ref, dg_ref = rest[-2:]

        @pl.when(pl.program_id(0) == 0)
        def _():
            dg_ref[...] = jnp.zeros_like(dg_ref)

        dh = (jnp.dot(da_ref[...], w1_ref[...], preferred_element_type=F32)
              + jnp.dot(db_ref[...], w3_ref[...], preferred_element_type=F32))
        xv = x_ref[...]
        r = _rms_r(xv)
        dx_ref[...] = dres_ref[...] + _rms_bwd(xv, r, g_ref[...], dh)
        dg_ref[...] += _colsum(dh * xv * r)

    act = pl.BlockSpec((tm, ff), lambda i: (i, 0))
    wgt = pl.BlockSpec((ff, d), lambda i: (0, 0), pipeline_mode=pl.Buffered(1))
    rows = pl.BlockSpec((tm, d), lambda i: (i, 0))
    gain = pl.BlockSpec((1, d), lambda i: (0, 0))
    return pl.pallas_call(kern, name=name, grid=(t // tm,),
                          in_specs=[act, act, wgt, wgt, rows, gain, rows] + [ANY] * len(after), out_specs=[rows, gain],
                          out_shape=[jax.ShapeDtypeStruct((t, d), F32), jax.ShapeDtypeStruct((1, d), F32)],
                          compiler_params=_params("arbitrary"))(da, db, w1, w3, x, g, dres, *after)


def _ffn_bwd(tag, x, g, w1, w3, w2, saved, dxo, dxo_t, early):
    ht, a, b, z = saved
    dw2 = _mm(tag + "_dw2", dxo_t, z, 1, 0, BF16, alpha=0.5, transposed=True)
    da, db = _ffn_dglu(tag + "_dglu", dxo, w2, a, b)
    dw1 = _mm(tag + "_dw1", ht, da, 1, 0, BF16, transposed=True)
    sent, pin = {}, []
    if early:
        sent[tag + "_w2 " + tag + "_w1"] = (_exchange_start(tag + "_w2_w1_send", "scatter", [dw2, dw1], [0, 0]), [0, 0])
        pin = [sent[tag + "_w2 " + tag + "_w1"][0][3]]
    dw3 = _mm(tag + "_dw3", ht, db, 1, 0, BF16, after=pin, transposed=True)
    last = [dw3] if early else [dw2, dw1, dw3]
    names = [tag + "_w3"] if early else [tag + "_w2", tag + "_w1", tag + "_w3"]
    sent[" ".join(names)] = (_exchange_start(tag + "_w3_send", "scatter", last, [0] * len(last)), [0] * len(last))
    dx, dg = _ffn_dh(tag + "_dh", da, db, w1, w3, x, g, dxo, after=[sent[" ".join(names)][0][3]])
    return dx, dg, sent


def _shift_copies(ext_ref, sh_ref):
    n = ext_ref.shape[0] - SUBLANE
    for r in range(1, SUBLANE):
        sh_ref[r, pl.ds(0, n), :] = ext_ref[pl.ds(r, n), :]


def _rows_at(ext_ref, sh_ref, off, rows):
    r = off % SUBLANE
    return ext_ref[pl.ds(off, rows), :] if r == 0 else sh_ref[r, pl.ds(off - r, rows), :]


def _conv_fwd(proj, cw, cb, lng, lnb, og, seq):
    n_rows, c = proj.shape[0], cb.shape[1]
    kw = HALO - 1
    tt = _pick(seq, (CONV_TILE,))
    hb = tt // HALO

    def kern(v_ref, g_ref, vp_ref, gp_ref, w_ref, cb_ref, lg_ref, lb_ref, og_ref, c_ref, an_ref, ext_ref, sh_ref):
        first = (pl.program_id(0) * tt) % seq == 0
        ext_ref[pl.ds(HALO, tt), :] = v_ref[...] * _sigmoid(g_ref[...])
        ext_ref[pl.ds(0, HALO), :] = vp_ref[...] * _sigmoid(gp_ref[...]) * jnp.where(first, 0.0, 1.0)
        _shift_copies(ext_ref, sh_ref)
        for r0 in range(0, tt, CONV_SUB):
            rows = min(CONV_SUB, tt - r0)
            acc = jnp.zeros((rows, c), F32)
            for k in range(kw):
                acc = acc + w_ref[pl.ds(k, 1), :] * _rows_at(ext_ref, sh_ref, r0 + HALO - (kw - 1) + k, rows)
            c_ref[pl.ds(r0, rows), :] = acc + cb_ref[...]
        cv = c_ref[...]
        mu = jnp.mean(cv, axis=-1, keepdims=True)
        xc = cv - mu
        rstd = lax.rsqrt(jnp.mean(xc * xc, axis=-1, keepdims=True) + EPS)
        lv = xc * rstd * lg_ref[...] + lb_ref[...]
        sl = lv * _sigmoid(lv)
        an_ref[...] = (sl * _rms_r(sl) * og_ref[...]).astype(BF16)

    cur = lambda cbk: pl.BlockSpec((tt, c), lambda i: (i, cbk))
    prev = lambda cbk: pl.BlockSpec((HALO, c), lambda i: (jnp.maximum(i * hb - 1, 0), cbk))
    par = lambda p: pl.BlockSpec(p.shape, lambda i: (0, 0))
    return pl.pallas_call(
        kern, name="conv_fwd", grid=(n_rows // tt,),
        in_specs=[cur(0), cur(1), prev(0), prev(1), par(cw), par(cb), par(lng), par(lnb), par(og)],
        out_specs=[pl.BlockSpec((tt, c), lambda i: (i, 0))] * 2,
        out_shape=[jax.ShapeDtypeStruct((n_rows, c), F32), jax.ShapeDtypeStruct((n_rows, c), BF16)],
        scratch_shapes=[pltpu.VMEM((tt + HALO, c), F32), pltpu.VMEM((SUBLANE, tt + HALO, c), F32)],
        compiler_params=_params("arbitrary"),
    )(proj, proj, proj, proj, cw, cb, lng, lnb, og)


def _conv_bwd_rows(dmixed, cpre, lng, lnb, og):
    c = cpre.shape[1]

    def body(ins, outs, accs):
        dan, cv, lg, lb, ogv = ins[0][...], ins[1][...], ins[2][...], ins[3][...], ins[4][...]
        mu = jnp.mean(cv, axis=-1, keepdims=True)
        xc = cv - mu
        rstd = lax.rsqrt(jnp.mean(xc * xc, axis=-1, keepdims=True) + EPS)
        xh = xc * rstd
        lv = xh * lg + lb
        s = _sigmoid(lv)
        sl = lv * s
        r2 = _rms_r(sl)
        accs[0][...] += _colsum(dan * sl * r2)
        dl = _rms_bwd(sl, r2, ogv, dan) * _dsilu(lv, s)
        accs[1][...] += _colsum(dl * xh)
        accs[2][...] += _colsum(dl)
        dxh = dl * lg
        dc = rstd * (dxh - jnp.mean(dxh, axis=-1, keepdims=True) - xh * jnp.mean(dxh * xh, axis=-1, keepdims=True))
        outs[0][...] = dc
        accs[3][...] += _colsum(dc)

    return _rowwise("conv_bwd_rows", body, cpre.shape[0], [(dmixed, c, 0), (cpre, c, 0)], [lng, lnb, og], [(c, F32)],
                    [(1, c)] * 4)


def _conv_bwd_taps(proj, dc, cw, seq):
    n_rows, c = dc.shape
    kw = HALO - 1
    tt = _pick(seq, (CONV_TILE,))
    hb = tt // HALO
    last_blk = n_rows // HALO - 1

    def kern(v_ref, g_ref, vp_ref, gp_ref, dc_ref, dn_ref, w_ref, dv_ref, dg_ref, dw_ref, exta_ref, extd_ref, sha_ref, shd_ref):
        i = pl.program_id(0)
        first = (i * tt) % seq == 0
        last = ((i + 1) * tt) % seq == 0

        @pl.when(i == 0)
        def _():
            dw_ref[...] = jnp.zeros_like(dw_ref)

        sg = _sigmoid(g_ref[...])
        exta_ref[pl.ds(HALO, tt), :] = v_ref[...] * sg
        exta_ref[pl.ds(0, HALO), :] = vp_ref[...] * _sigmoid(gp_ref[...]) * jnp.where(first, 0.0, 1.0)
        extd_ref[pl.ds(0, tt), :] = dc_ref[...]
        extd_ref[pl.ds(tt, HALO), :] = dn_ref[...] * jnp.where(last, 0.0, 1.0)
        _shift_copies(exta_ref, sha_ref)
        _shift_copies(extd_ref, shd_ref)
        for k0 in range(0, kw, SUBLANE):
            taps = range(k0, min(k0 + SUBLANE, kw))
            sums = [jnp.zeros((SUBLANE, c), F32) for _ in taps]
            for r0 in range(0, tt, SUBLANE):
                dcb = dc_ref[pl.ds(r0, SUBLANE), :]
                for n, k in enumerate(taps):
                    sums[n] = sums[n] + _rows_at(exta_ref, sha_ref, r0 + HALO - (kw - 1) + k, SUBLANE) * dcb
            for n, k in enumerate(taps):
                dw_ref[pl.ds(k, 1), :] += _colsum(sums[n])
        for r0 in range(0, tt, CONV_SUB):
            rows = min(CONV_SUB, tt - r0)
            acc = jnp.zeros((rows, c), F32)
            for k in range(kw):
                acc = acc + w_ref[pl.ds(k, 1), :] * _rows_at(extd_ref, shd_ref, r0 + (kw - 1) - k, rows)
            dv_ref[pl.ds(r0, rows), :] = acc
        da = dv_ref[...]
        dv_ref[...] = da * sg
        dg_ref[...] = da * v_ref[...] * sg * (1.0 - sg)

    cur = lambda cbk: pl.BlockSpec((tt, c), lambda i: (i, cbk))
    prev = lambda cbk: pl.BlockSpec((HALO, c), lambda i: (jnp.maximum(i * hb - 1, 0), cbk))
    nxt = pl.BlockSpec((HALO, c), lambda i: (jnp.minimum((i + 1) * hb, last_blk), 0))
    return pl.pallas_call(
        kern, name="conv_bwd_taps", grid=(n_rows // tt,),
        in_specs=[cur(0), cur(1), prev(0), prev(1), cur(0), nxt, pl.BlockSpec(cw.shape, lambda i: (0, 0))],
        out_specs=[cur(0), cur(0), pl.BlockSpec((HALO, c), lambda i: (0, 0))],
        out_shape=[jax.ShapeDtypeStruct((n_rows, c), F32), jax.ShapeDtypeStruct((n_rows, c), F32),
                   jax.ShapeDtypeStruct((HALO, c), F32)],
        scratch_shapes=[pltpu.VMEM((tt + HALO, c), F32)] * 2 + [pltpu.VMEM((SUBLANE, tt + HALO, c), F32)] * 2,
        compiler_params=_params("arbitrary"),
    )(proj, proj, proj, proj, dc, dc, cw)


def _s5_params_fwd(lr, li, ldt, btr, bti, seg):
    ns = lr.shape[1]

    def kern(lr_ref, li_ref, ldt_ref, btr_ref, bti_ref, ar_ref, ai_ref, bbr_ref, bbi_ref, ps_ref, psf_ref, pc_ref, pcf_ref):
        lrv, liv = lr_ref[...], li_ref[...]
        dt = jnp.exp(ldt_ref[...])
        zr, zi = lrv * dt, liv * dt
        mag = jnp.exp(zr)
        ar, ai = mag * jnp.cos(zi), mag * jnp.sin(zi)
        den = lrv * lrv + liv * liv
        nr = ar - 1.0
        cr = (nr * lrv + ai * liv) / den
        ci = (ai * lrv - nr * liv) / den
        ar_ref[...] = ar
        ai_ref[...] = ai
        bbr_ref[...] = cr * btr_ref[...] - ci * bti_ref[...]
        bbi_ref[...] = cr * bti_ref[...] + ci * btr_ref[...]
        def powers(br, bi, count, up_ref, down_ref):
            pr, pi = br, bi
            for e in range(count):
                for ref, at in ((up_ref, e), (down_ref, count - 1 - e)):
                    ref[pl.ds(at, 1), pl.ds(0, ns)] = pr
                    ref[pl.ds(at, 1), pl.ds(ns, ns)] = pi
                if e < count - 1:
                    pr, pi = pr * br - pi * bi, pr * bi + pi * br
            return pr, pi

        powers(*powers(ar, ai, seg, ps_ref, psf_ref), SUBLANE, pc_ref, pcf_ref)

    h = btr.shape[0]
    shapes = [jax.ShapeDtypeStruct((1, ns), F32)] * 2 + [jax.ShapeDtypeStruct((h, ns), F32)] * 2
    shapes += [jax.ShapeDtypeStruct((seg, 2 * ns), F32)] * 2 + [jax.ShapeDtypeStruct((SUBLANE, 2 * ns), F32)] * 2
    return pl.pallas_call(kern, name="s5_params_fwd", out_shape=shapes)(lr, li, ldt, btr, bti)


def _s5_params_bwd(lr, li, ldt, btr, bti, dar, dai, dbbr, dbbi):
    def kern(lr_ref, li_ref, ldt_ref, btr_ref, bti_ref, dar_ref, dai_ref, dbr_ref, dbi_ref,
             dlr_ref, dli_ref, dldt_ref, dbtr_ref, dbti_ref):
        lrv, liv = lr_ref[...], li_ref[...]
        dt = jnp.exp(ldt_ref[...])
        zr, zi = lrv * dt, liv * dt
        mag = jnp.exp(zr)
        ar, ai = mag * jnp.cos(zi), mag * jnp.sin(zi)
        den = lrv * lrv + liv * liv
        nr = ar - 1.0
        cr = (nr * lrv + ai * liv) / den
        ci = (ai * lrv - nr * liv) / den
        dbr, dbi, br, bi = dbr_ref[...], dbi_ref[...], btr_ref[...], bti_ref[...]
        dbtr_ref[...] = cr * dbr + ci * dbi
        dbti_ref[...] = cr * dbi - ci * dbr
        dcr = _colsum(br * dbr + bi * dbi)
        dci = _colsum(br * dbi - bi * dbr)
        ir, ii = lrv / den, -liv / den
        dnr = ir * dcr + ii * dci
        dni = ir * dci - ii * dcr
        wr, wi = cr * ir - ci * ii, cr * ii + ci * ir
        dl1r = -(wr * dcr + wi * dci)
        dl1i = -(wr * dci - wi * dcr)
        dtr, dti = dar_ref[...] + dnr, dai_ref[...] + dni
        dzr = ar * dtr + ai * dti
        dzi = ar * dti - ai * dtr
        dlr_ref[...] = dl1r + dt * dzr
        dli_ref[...] = dl1i + dt * dzi
        dldt_ref[...] = (dzr * lrv + dzi * liv) * dt

    ns, h = lr.shape[1], btr.shape[0]
    shapes = [jax.ShapeDtypeStruct((1, ns), F32)] * 3 + [jax.ShapeDtypeStruct((h, ns), F32)] * 2
    return pl.pallas_call(kern, name="s5_params_bwd", out_shape=shapes)(lr, li, ldt, btr, bti, dar, dai, dbbr, dbbi)


def _to_segments(nat_ref, seg_ref):
    steps = nat_ref.shape[0] // SUBLANE
    _regroup(nat_ref, seg_ref, lambda r: (r % SUBLANE) * steps + r // SUBLANE)


def _from_segments(seg_ref, nat_ref):
    steps = nat_ref.shape[0] // SUBLANE
    _regroup(seg_ref, nat_ref, lambda r: (r % steps) * SUBLANE + r // steps)


def _regroup(src_ref, dst_ref, src_row):
    rows, width = dst_ref.shape
    sublane = lax.broadcasted_iota(jnp.int32, (SUBLANE, width), 0)
    for r0 in range(0, rows, SUBLANE):
        tile = jnp.broadcast_to(src_ref[pl.ds(src_row(r0), 1), :], (SUBLANE, width))
        for k in range(1, SUBLANE):
            tile = jnp.where(sublane == k, src_ref[pl.ds(src_row(r0 + k), 1), :], tile)
        dst_ref[pl.ds(r0, SUBLANE), :] = tile


def _scan_tile(s_ref, o_ref, fix_ref, tabs, car_ref, sb, reverse, x_ref=None, acc_ref=None):
    l1, l2, l4, pw = tabs
    rows_t, w = s_ref.shape
    steps = rows_t // SUBLANE
    cw = _pick(sb, (SCAN_COLS,))
    last = 0 if reverse else SUBLANE - 1
    first = SUBLANE - 1 - last
    row = lax.broadcasted_iota(jnp.int32, (SUBLANE, cw), 0)
    step_rows = lambda i: pl.ds(pl.multiple_of(((steps - 1 - i) if reverse else i) * SUBLANE, SUBLANE), SUBLANE)
    zero = jnp.zeros((SUBLANE, cw), F32)

    for c0 in [b0 + o for b0 in range(0, w, 2 * sb) for o in range(0, sb, cw)]:
        cr, ci = pl.ds(c0, cw), pl.ds(c0 + sb, cw)
        base = pl.ds(((steps - 1) if reverse else 0) * SUBLANE, SUBLANE)
        ar, ai = fix_ref[base, cr], fix_ref[base, ci]

        def run(i, state):
            xr, xi = state
            rows = step_rows(i)
            xr, xi = ar * xr - ai * xi + s_ref[rows, cr], ar * xi + ai * xr + s_ref[rows, ci]
            o_ref[rows, cr] = xr
            o_ref[rows, ci] = xi
            return xr, xi

        fr, fi = lax.fori_loop(0, steps, run, (zero, zero))
        for s, lt in ((1, l1), (2, l2), (4, l4)):
            sh = (SUBLANE - s) if reverse else s
            sr, si = pltpu.roll(fr, sh, 0), pltpu.roll(fi, sh, 0)
            tr, ti = lt[:, cr], lt[:, ci]
            fr, fi = fr + tr * sr - ti * si, fi + tr * si + ti * sr
        kr, ki = car_ref[pl.ds(last, 1), cr], car_ref[pl.ds(last, 1), ci]
        pr, pi = pw[:, cr], pw[:, ci]
        fr, fi = fr + pr * kr - pi * ki, fi + pr * ki + pi * kr
        car_ref[:, cr] = fr
        car_ref[:, ci] = fi
        to_next = 1 if not reverse else SUBLANE - 1
        gr = jnp.where(row == first, kr, pltpu.roll(fr, to_next, 0))
        gi = jnp.where(row == first, ki, pltpu.roll(fi, to_next, 0))

        def fix(i, state):
            rows = step_rows(i)
            qr, qi = fix_ref[rows, cr], fix_ref[rows, ci]
            yr = o_ref[rows, cr] + qr * gr - qi * gi
            yi = o_ref[rows, ci] + qr * gi + qi * gr
            o_ref[rows, cr] = yr
            o_ref[rows, ci] = yi
            if acc_ref is None:
                return state
            nr, ni, sr, si = state
            pxr, pxi = x_ref[rows, cr], x_ref[rows, ci]
            return yr, yi, sr + nr * pxr + ni * pxi, si + ni * pxr - nr * pxi

        if acc_ref is None:
            lax.fori_loop(0, steps, fix, 0)
        else:
            _, _, sr, si = lax.fori_loop(0, steps, fix, (gr, gi, zero, zero))
            acc_ref[:, cr] += sr
            acc_ref[:, ci] += si


def _s5_fwd(proj, u_blk, bdc, cdc, fix, tabs, dskip, seq, sb):
    n_rows = proj.shape[0]
    nb, blk, w_blk = bdc.shape
    c, w = nb * blk, nb * w_blk
    tt = fix.shape[0]

    def kern(u_ref, bd_ref, cd_ref, fix_ref, l1, l2, l4, pw, d_ref, xs_ref, yp_ref, yg_ref, us_ref, bu_ref, car_ref):
        @pl.when((pl.program_id(0) * tt) % seq == 0)
        def _():
            car_ref[...] = jnp.zeros_like(car_ref)

        _to_segments(u_ref, us_ref)
        for j in range(nb):
            bu_ref[:, pl.ds(j * w_blk, w_blk)] = jnp.dot(us_ref[:, pl.ds(j * blk, blk)].astype(BF16), bd_ref[j],
                                                         preferred_element_type=F32)
        _scan_tile(bu_ref, xs_ref, fix_ref, (l1, l2, l4, pw), car_ref, sb, False)
        for j in range(nb):
            cols = pl.ds(j * blk, blk)
            y0 = jnp.dot(xs_ref[:, pl.ds(j * w_blk, w_blk)].astype(BF16), cd_ref[j], preferred_element_type=F32)
            us_ref[:, cols] = y0 + d_ref[:, cols] * us_ref[:, cols]
        _from_segments(us_ref, yp_ref)
        yg_ref[...] = _gelu(yp_ref[...]).astype(BF16)

    tab = pl.BlockSpec((SUBLANE, w), lambda i: (0, 0))
    rows = pl.BlockSpec((tt, c), lambda i: (i, 0))
    return pl.pallas_call(
        kern, name="s5_fwd", grid=(n_rows // tt,),
        in_specs=[pl.BlockSpec((tt, c), lambda i: (i, u_blk * blk // c)), pl.BlockSpec(bdc.shape, lambda i: (0, 0, 0)),
                  pl.BlockSpec(cdc.shape, lambda i: (0, 0, 0)), pl.BlockSpec((tt, w), lambda i: (0, 0)), tab, tab, tab, tab,
                  pl.BlockSpec((1, c), lambda i: (0, 0))],
        out_specs=[pl.BlockSpec((tt, w), lambda i: (i, 0)), rows, rows],
        out_shape=[jax.ShapeDtypeStruct((n_rows, w), F32), jax.ShapeDtypeStruct((n_rows, c), F32),
                   jax.ShapeDtypeStruct((n_rows, c), BF16)],
        scratch_shapes=[pltpu.VMEM((tt, c), F32), pltpu.VMEM((tt, w), F32), pltpu.VMEM((SUBLANE, w), F32)],
        compiler_params=_params("arbitrary"))(proj, bdc, cdc, fix, *tabs, dskip)


def _s5_bwd(dypre, du_skip, xs, proj, u_blk, bdc, cdc, fix, tabs, seq, sb):
    n_rows = proj.shape[0]
    nb, blk, w_blk = bdc.shape
    c, w = nb * blk, nb * w_blk
    tt = fix.shape[0]
    nt = n_rows // tt
    tn = (((0,), (0,)), ((), ()))

    def kern(dy_ref, ds_ref, x_ref, u_ref, bd_ref, cd_ref, fix_ref, l1, l2, l4, pw, du_ref, da_ref, db_ref, dc_ref,
             dys_ref, us_ref, dus_ref, gx_ref, lam_ref, car_ref, acc_ref):
        i = pl.program_id(0)

        @pl.when(((nt - i) * tt) % seq == 0)
        def _():
            car_ref[...] = jnp.zeros_like(car_ref)

        @pl.when(i == 0)
        def _():
            acc_ref[...] = jnp.zeros_like(acc_ref)
            db_ref[...] = jnp.zeros_like(db_ref)
            dc_ref[...] = jnp.zeros_like(dc_ref)

        _to_segments(dy_ref, dys_ref)
        _to_segments(u_ref, us_ref)
        for j in range(nb):
            gx_ref[:, pl.ds(j * w_blk, w_blk)] = lax.dot_general(dys_ref[:, pl.ds(j * blk, blk)].astype(BF16), cd_ref[j], NT,
                                                                 preferred_element_type=F32)
        _scan_tile(gx_ref, lam_ref, fix_ref, (l1, l2, l4, pw), car_ref, sb, True, x_ref, acc_ref)
        for j in range(nb):
            cols, wide = pl.ds(j * blk, blk), pl.ds(j * w_blk, w_blk)
            lam = lam_ref[:, wide].astype(BF16)
            dus_ref[:, cols] = lax.dot_general(lam, bd_ref[j], NT, preferred_element_type=F32)
            db_ref[j] += lax.dot_general(us_ref[:, cols].astype(BF16), lam, tn, preferred_element_type=F32)
            dc_ref[j] += lax.dot_general(x_ref[:, wide].astype(BF16), dys_ref[:, cols].astype(BF16), tn,
                                         preferred_element_type=F32)
        _from_segments(dus_ref, du_ref)
        du_ref[...] += ds_ref[...]

        @pl.when(i == nt - 1)
        def _():
            da_ref[...] = _colsum(acc_ref[...])

    back = lambda i: (nt - 1 - i, 0)
    tab = pl.BlockSpec((SUBLANE, w), lambda i: (0, 0))
    rows = pl.BlockSpec((tt, c), back)
    whole = lambda a: pl.BlockSpec(a.shape, lambda i: (0, 0, 0))
    return pl.pallas_call(
        kern, name="s5_bwd", grid=(nt,),
        in_specs=[rows, rows, pl.BlockSpec((tt, w), back), pl.BlockSpec((tt, c), lambda i: (nt - 1 - i, u_blk * blk // c)),
                  whole(bdc), whole(cdc), pl.BlockSpec((tt, w), lambda i: (0, 0)), tab, tab, tab, tab],
        out_specs=[rows, pl.BlockSpec((1, w), lambda i: (0, 0)), whole(bdc), whole(cdc)],
        out_shape=[jax.ShapeDtypeStruct((n_rows, c), F32), jax.ShapeDtypeStruct((1, w), F32),
                   jax.ShapeDtypeStruct(bdc.shape, F32), jax.ShapeDtypeStruct(cdc.shape, F32)],
        scratch_shapes=[pltpu.VMEM((tt, c), F32)] * 3 + [pltpu.VMEM((tt, w), F32)] * 2 + [pltpu.VMEM((SUBLANE, w), F32)] * 2,
        compiler_params=_params("arbitrary"))(dypre, du_skip, xs, proj, bdc, cdc, fix, *tabs)


def _s5_post2(yg, q0, bg, og):
    c = yg.shape[1]

    def body(ins, outs, accs):
        ygv = ins[0][...].astype(F32)
        sg = ygv * _sigmoid(ins[1][...] + ins[2][...])
        outs[0][...] = (sg * _rms_r(sg) * ins[3][...]).astype(BF16)

    return _rowwise("s5_post2", body, yg.shape[0], [(yg, c, 0), (q0, c, 0)], [bg, og], [(c, BF16)], [])[0]


def _s5_post2_bwd(dmixed, yg, q0, bg, og):
    c = yg.shape[1]

    def body(ins, outs, accs):
        dsn, ygv = ins[0][...], ins[1][...].astype(F32)
        s = _sigmoid(ins[2][...] + ins[3][...])
        sg = ygv * s
        r = _rms_r(sg)
        accs[0][...] += _colsum(dsn * sg * r)
        dsg = _rms_bwd(sg, r, ins[4][...], dsn)
        dq = dsg * ygv * s * (1.0 - s)
        outs[0][...] = dq.astype(BF16)
        outs[1][...] = dsg * s
        accs[1][...] += _colsum(dq)

    return _rowwise("s5_post2_bwd", body, yg.shape[0], [(dmixed, c, 1), (yg, c, 0), (q0, c, 0)], [bg, og],
                    [(c, BF16), (c, F32)], [(1, c)] * 2)


def _s5_post1_bwd(dyg1, dyg2, ypre, proj, dskip, after=()):
    c = ypre.shape[1]

    def body(ins, outs, accs):
        dyp = (ins[0][...] + ins[1][...]) * _dgelu(ins[2][...])
        outs[0][...] = dyp
        outs[1][...] = dyp * ins[4][...]
        accs[0][...] += _colsum(dyp * ins[3][...])

    return _rowwise("s5_post1_bwd", body, ypre.shape[0], [(dyg1, c, 0), (dyg2, c, 0), (ypre, c, 0), (proj, c, 2)], [dskip],
                    [(c, F32), (c, F32)], [(1, c)], after=after)


def _place():
    return lax.axis_index("x"), lax.axis_index("y"), lax.axis_index("c")


def _window(ref, axis, q, rows, cols):
    if axis == 0:
        return ref.at[pl.ds(pl.multiple_of(q * rows, SUBLANE), rows), :]
    return ref.at[:, pl.ds(pl.multiple_of(q * cols, LANE), cols)]


ALL_RELS = [(fx, fy, fc) for fx in (0, 1) for fy in (0, 1) for fc in (0, 1)][1:]
N_PEERS = {"gather": 3, "scatter": 3, "sibling": 1, "all": len(ALL_RELS)}


def _copies(kind, srcs, lands, shards, axes, send_sems, recv_sems, local_sems):
    x, y, c = _place()
    me, dev = 2 * x + y, 4 * x + 2 * y + c
    n_peers = N_PEERS[kind]
    starts, waits = [], []
    for a, (src, land) in enumerate(zip(srcs, lands)):
        on = lambda k, peer: dict(send_sem=send_sems.at[n_peers * a + k], recv_sem=recv_sems.at[n_peers * a + k],
                                  device_id=peer, device_id_type=MESH)
        if kind == "sibling":
            cp = pltpu.make_async_remote_copy(src_ref=src, dst_ref=land, **on(0, (x, y, 1 - c)))
            starts.append(cp)
            waits.append(cp)
            continue
        if kind == "all":
            own = pltpu.make_async_copy(src, land.at[dev], local_sems.at[a])
            starts.append(own)
            waits.append(own)
            for k, (fx, fy, fc) in enumerate(ALL_RELS):
                px, py, pc = (1 - x) if fx else x, (1 - y) if fy else y, (1 - c) if fc else c
                starts.append(pltpu.make_async_remote_copy(src_ref=src, dst_ref=land.at[dev], **on(k, (px, py, pc))))
                waits.append(pltpu.make_async_remote_copy(src_ref=src, dst_ref=land.at[4 * px + 2 * py + pc],
                                                          **on(k, (px, py, pc))))
            continue
        rows, cols = shards[a]
        if kind == "gather":
            own = pltpu.make_async_copy(src, _window(land, axes[a], me, rows, cols), local_sems.at[a])
        else:
            own = pltpu.make_async_copy(_window(src, axes[a], me, rows, cols), land.at[3], local_sems.at[a])
        starts.append(own)
        waits.append(own)
        for j, (fx, fy) in enumerate(CHIP_RELS):
            px, py = (1 - x) if fx else x, (1 - y) if fy else y
            peer = 2 * px + py
            if kind == "gather":
                starts.append(pltpu.make_async_remote_copy(src_ref=src, dst_ref=_window(land, axes[a], me, rows, cols),
                                                           **on(j, (px, py, c))))
                waits.append(pltpu.make_async_remote_copy(src_ref=src, dst_ref=_window(land, axes[a], peer, rows, cols),
                                                          **on(j, (px, py, c))))
            else:
                cp = pltpu.make_async_remote_copy(src_ref=_window(src, axes[a], peer, rows, cols), dst_ref=land.at[j],
                                                  **on(j, (px, py, c)))
                starts.append(cp)
                waits.append(cp)
    return starts, waits


HBM = pl.BlockSpec(memory_space=pltpu.HBM)
SEM = pl.BlockSpec(memory_space=pltpu.SEMAPHORE)


def _shard_shapes(kind, arrs, axes):
    if kind != "scatter":
        return [a.shape for a in arrs]
    return [(a.shape[0] // N_CHIPS, a.shape[1]) if ax == 0 else (a.shape[0], a.shape[1] // N_CHIPS) for a, ax in zip(arrs, axes)]


def _land_shapes(kind, arrs, axes):
    if kind == "gather":
        return [(N_CHIPS * a.shape[0], a.shape[1]) if ax == 0 else (a.shape[0], N_CHIPS * a.shape[1]) for a, ax in zip(arrs, axes)]
    if kind == "scatter":
        return [(N_CHIPS,) + s for s in _shard_shapes(kind, arrs, axes)]
    return [a.shape if kind == "sibling" else (len(ALL_RELS) + 1,) + a.shape for a in arrs]


def _exchange_start(name, kind, arrs, axes, after=()):
    n, n_after = len(arrs), len(after)
    shards = _shard_shapes(kind, arrs, axes)
    land_shapes = _land_shapes(kind, arrs, axes)
    lands = [lax.empty(s, a.dtype) for s, a in zip(land_shapes, arrs)]

    def kern(*refs):
        outs = refs[2 * n + n_after:]
        starts, _ = _copies(kind, refs[:n], refs[n:2 * n], shards, axes, outs[0], outs[1], outs[2])
        for cp in starts:
            cp.start()
        outs[-1][...] = jnp.zeros_like(outs[-1])

    kept = [pltpu.HBM(a.shape, a.dtype) for a in arrs] + [pltpu.HBM(s, a.dtype) for s, a in zip(land_shapes, arrs)]
    n_sems = N_PEERS[kind] * n
    res = pl.pallas_call(
        kern, name=name, in_specs=[HBM] * (2 * n) + [ANY] * n_after,
        out_specs=[SEM] * 3 + [HBM] * (2 * n) + [pl.BlockSpec(memory_space=pltpu.VMEM)],
        out_shape=[pltpu.SemaphoreType.DMA((n_sems,)), pltpu.SemaphoreType.DMA((n_sems,)), pltpu.SemaphoreType.DMA((n,))]
        + kept + [jax.ShapeDtypeStruct((SUBLANE, LANE), F32)],
        input_output_aliases={i: 3 + i for i in range(2 * n)},
        compiler_params=pltpu.CompilerParams(has_side_effects=pltpu.SideEffectType.DATAFLOW_SIDE_EFFECTING),
    )(*[pltpu.with_memory_space_constraint(a, pltpu.HBM) for a in list(arrs) + lands], *after)
    return res[:3], res[3:3 + n], res[3 + n:3 + 2 * n], res[-1]


def _exchange_wait(name, kind, started, axes, after, sources_too=False):
    sems, srcs, lands, _ = started
    n, n_after = len(srcs), len(after)
    shards = _shard_shapes(kind, srcs, axes)

    def kern(*refs):
        sem_refs = refs[2 * n:2 * n + 3]
        _, waits = _copies(kind, refs[:n], refs[n:2 * n], shards, axes, *sem_refs)
        for cp in waits:
            cp.wait()

    res = pl.pallas_call(
        kern, name=name, in_specs=[HBM] * (2 * n) + [SEM] * 3 + [ANY] * n_after, out_specs=[HBM] * (2 * n),
        out_shape=[pltpu.HBM(a.shape, a.dtype) for a in list(srcs) + list(lands)],
        input_output_aliases={i: i for i in range(2 * n)},
        compiler_params=pltpu.CompilerParams(has_side_effects=pltpu.SideEffectType.DATAFLOW_SIDE_EFFECTING),
    )(*srcs, *lands, *sems, *after)
    return (res[:n], res[n:]) if sources_too else res[n:]


def _sum_devices(parts):
    def kern(p_ref, o_ref):
        acc = p_ref[0]
        for d in range(1, parts.shape[0]):
            acc = acc + p_ref[d]
        o_ref[...] = acc

    return pl.pallas_call(kern, name="sum_devices", out_shape=jax.ShapeDtypeStruct(parts.shape[1:], F32),
                          compiler_params=pltpu.CompilerParams(vmem_limit_bytes=VMEM_LIMIT_BYTES))(parts)


def _sum_slots(name, parts):
    _, rows, cols = parts.shape
    tr = _pick(rows, (ROW_TILE, 128, 64, 32))

    def kern(p_ref, o_ref):
        o_ref[...] = ((p_ref[3].astype(F32) + p_ref[0].astype(F32)) + p_ref[1].astype(F32)) + p_ref[2].astype(F32)

    return pl.pallas_call(kern, name=name, grid=(rows // tr,),
                          in_specs=[pl.BlockSpec((N_CHIPS, tr, cols), lambda i: (0, i, 0))],
                          out_specs=pl.BlockSpec((tr, cols), lambda i: (i, 0)),
                          out_shape=jax.ShapeDtypeStruct((rows, cols), F32), compiler_params=_params("arbitrary"))(parts)


def _adamw_math(g, w, m, v):
    m2 = ADAM_B1 * m + (1.0 - ADAM_B1) * g
    v2 = ADAM_B2 * v + (1.0 - ADAM_B2) * (g * g)
    m_hat = m2 / (1.0 - ADAM_B1 ** ADAM_STEP)
    v_hat = v2 / (1.0 - ADAM_B2 ** ADAM_STEP)
    return -ADAM_LR * (m_hat / (jnp.sqrt(v_hat) + ADAM_EPS) + ADAM_WD * w), m2, v2


def _adamw(name, parts, w, m, v):
    rows, cols = w.shape
    tr = rows if rows * cols <= WHOLE_ELEMS else _pick(rows, (ROW_TILE, 352, 128, 64, 32, 8))
    n = len(parts)

    def kern(*refs):
        g = refs[0][:, pl.ds(0, cols)]
        for p in refs[1:n]:
            g = g + p[:, pl.ds(0, cols)]
        d, m2, v2 = _adamw_math(g, refs[n][...], refs[n + 1][...], refs[n + 2][...])
        refs[n + 3][...] = g
        refs[n + 4][...] = d
        refs[n + 5][...] = m2
        refs[n + 6][...] = v2

    spec = pl.BlockSpec((tr, cols), lambda i: (i, 0))
    return pl.pallas_call(kern, name=name, grid=(rows // tr,),
                          in_specs=[pl.BlockSpec((tr, p.shape[1]), lambda i: (i, 0)) for p in parts] + [spec] * 3,
                          out_specs=[spec] * 4, out_shape=[jax.ShapeDtypeStruct((rows, cols), F32)] * 4,
                          compiler_params=_params("arbitrary"))(*parts, w, m, v)


def _adamw_many(name, gs, ws, ms, vs):
    n = len(gs)

    def kern(*refs):
        for p in range(n):
            d, m2, v2 = _adamw_math(refs[p][...], refs[n + p][...], refs[2 * n + p][...], refs[3 * n + p][...])
            refs[4 * n + p][...] = d
            refs[5 * n + p][...] = m2
            refs[6 * n + p][...] = v2

    res = pl.pallas_call(kern, name=name, out_shape=[jax.ShapeDtypeStruct(w.shape, F32) for w in ws] * 3,
                         compiler_params=pltpu.CompilerParams(vmem_limit_bytes=VMEM_LIMIT_BYTES))(*gs, *ws, *ms, *vs)
    return res[:n], res[n:2 * n], res[2 * n:]


def _pack(arrs):
    parts, rows = [], []
    for a in arrs:
        r = _round_up(-(-a.size // LANE), SUBLANE)
        parts.append(jnp.pad(a.reshape(-1).astype(F32), (0, r * LANE - a.size)).reshape(r, LANE))
        rows.append(r)
    return jnp.concatenate(parts, axis=0), rows


def _unpack(buf, rows, shapes):
    out, r0 = [], 0
    for r, s in zip(rows, shapes):
        size = math.prod(s)
        out.append(buf[r0:r0 + r].reshape(-1)[:size].reshape(s))
        r0 += r
    return out


def kernel(x, norm_ffn1, ffn1_w1, ffn1_w3, ffn1_w2, norm_mix, w_in, conv_w, conv_b, conv_ln_g, conv_ln_b, conv_out_g, ssm_A_re, ssm_A_im, ssm_log_dt, ssm_B_re, ssm_B_im, ssm_C_re, ssm_C_im, ssm_D, ssm_glu_w, ssm_glu_b, ssm_out_g, w_out, norm_ffn2, ffn2_w1, ffn2_w3, ffn2_w2, norm_final, loss_target, m_norm_ffn1, m_ffn1_w1, m_ffn1_w3, m_ffn1_w2, m_norm_mix, m_w_in, m_conv_w, m_conv_b, m_conv_ln_g, m_conv_ln_b, m_conv_out_g, m_ssm_A_re, m_ssm_A_im, m_ssm_log_dt, m_ssm_B_re, m_ssm_B_im, m_ssm_C_re, m_ssm_C_im, m_ssm_D, m_ssm_glu_w, m_ssm_glu_b, m_ssm_out_g, m_w_out, m_norm_ffn2, m_ffn2_w1, m_ffn2_w3, m_ffn2_w2, m_norm_final, v_norm_ffn1, v_ffn1_w1, v_ffn1_w3, v_ffn1_w2, v_norm_mix, v_w_in, v_conv_w, v_conv_b, v_conv_ln_g, v_conv_ln_b, v_conv_out_g, v_ssm_A_re, v_ssm_A_im, v_ssm_log_dt, v_ssm_B_re, v_ssm_B_im, v_ssm_C_re, v_ssm_C_im, v_ssm_D, v_ssm_glu_w, v_ssm_glu_b, v_ssm_out_g, v_w_out, v_norm_ffn2, v_ffn2_w1, v_ffn2_w3, v_ffn2_w2, v_norm_final):
    given = dict(locals())
    wts = {n: given[n] for n in WEIGHTS}
    n_seq, seq, d = x.shape
    n_rows = n_seq * seq
    xf = x.reshape(n_rows, d)
    tgt = loss_target.reshape(n_rows, d)
    row = lambda a: a.reshape(1, -1)

    f = ffn1_w1.shape[-1]
    fp = _round_up(f, LANE)
    held = lambda n, a: a[0].T if n in TRANSPOSED else a[0]
    shards = []
    for n in BIG:
        s = held(n, wts[n]).astype(BF16)
        if n.startswith('ffn'):
            s = jnp.pad(s, ((0, fp - f), (0, 0)))
        shards.append(s)
    n_taps, c_shard = conv_w.shape[1], conv_w.shape[2]
    shards.append(jnp.pad(conv_w[0], ((0, HALO - n_taps), (0, 0))))
    shard_of = dict(zip(BIG + ['conv_w'], shards))
    axis_of = dict(BIG_AXIS, conv_w=1)
    groups = [['ffn1_w1', 'ffn1_w3'], ['ffn1_w2', 'w_in', 'conv_w', 'ssm_glu_w', 'w_out'], ['ffn2_w1', 'ffn2_w3', 'ffn2_w2']]
    fetch, tok = [], []
    for k, names in enumerate(groups):
        fetch.append(_exchange_start("gather%d_send" % k, "gather", [shard_of[n] for n in names],
                                     [axis_of[n] for n in names], tok))
        tok = [fetch[-1][3]]
    full = {}

    def arrive(k, after):
        lands = _exchange_wait("gather%d_recv" % k, "gather", fetch[k], [axis_of[n] for n in groups[k]], after)
        full.update(zip(groups[k], lands))

    h1, h1_t = _rms_fwd("ffn1_rms", xf, norm_ffn1)
    arrive(0, tok + [h1])

    _, n_grp, n_state = ssm_A_re.shape
    grp = ssm_B_re.shape[-1]
    ns = n_grp * n_state
    c_ssm = n_grp * grp
    lr, li = ssm_A_re.reshape(1, ns), ssm_A_im.reshape(1, ns)
    ldt = jnp.repeat(ssm_log_dt.reshape(n_grp), n_state).reshape(1, ns)
    btr = ssm_B_re[0].transpose(2, 0, 1).reshape(grp, ns)
    bti = ssm_B_im[0].transpose(2, 0, 1).reshape(grp, ns)
    ctr = ssm_C_re[0].transpose(1, 0, 2).reshape(grp, ns)
    cti = ssm_C_im[0].transpose(1, 0, 2).reshape(grp, ns)
    scan_tile = _pick(seq, (SCAN_TILE,))
    _, _, bbr, bbi, seg_up, seg_down, pw, pw_falling = _s5_params_fwd(lr, li, ldt, btr, bti, scan_tile // SUBLANE)
    nb = c_ssm // LANE
    sb, gpb = ns // nb, n_grp // nb
    diag = (jnp.arange(LANE)[:, None] // grp) == (jnp.arange(sb)[None, :] // n_state)

    def spread(t):
        return jnp.where(diag, jnp.tile(t.reshape(grp, nb, sb).transpose(1, 0, 2), (1, gpb, 1)), 0.0)

    def gather_diag(t):
        return (t * diag).reshape(nb, gpb, grp, sb).sum(1).transpose(1, 0, 2).reshape(grp, ns)

    def interleave(re, im):
        return jnp.stack([re.reshape(-1, nb, sb), im.reshape(-1, nb, sb)], axis=2).reshape(-1, 2 * ns)

    bdc = jnp.concatenate([spread(bbr), spread(bbi)], axis=2).astype(BF16)
    cdc = jnp.concatenate([spread(ctr).transpose(0, 2, 1), -spread(cti).transpose(0, 2, 1)], axis=1).astype(BF16)
    rowi = jnp.arange(SUBLANE)[:, None]
    pwf, pwc = interleave(pw[:, :ns], pw[:, ns:]), interleave(pw[:, :ns], -pw[:, ns:])
    tabs_f = [jnp.where(rowi >= s, pwf[s - 1][None, :], 0.0) for s in (1, 2, 4)] + [pwf]
    tabs_b = [jnp.where(rowi <= SUBLANE - 1 - s, pwc[s - 1][None, :], 0.0) for s in (1, 2, 4)]
    tabs_b.append(interleave(pw_falling[:, :ns], -pw_falling[:, ns:]))
    fix_f = jnp.repeat(interleave(seg_up[:, :ns], seg_up[:, ns:]), SUBLANE, axis=0)
    fix_b = jnp.repeat(interleave(seg_down[:, :ns], -seg_down[:, ns:]), SUBLANE, axis=0)
    c_conv = conv_b.shape[1]
    u_blk = 2 * c_conv // LANE

    a1, b1, z1 = _ffn_up("ffn1_up", h1, full['ffn1_w1'], full['ffn1_w3'])
    arrive(1, [z1])
    x1, h2, h2_t = _mm("ffn1_down", z1, full['ffn1_w2'], 1, 0, addend=xf, alpha=0.5, post=_post_rms(norm_mix))
    saved1 = (h1_t, a1, b1, z1)
    cw = full['conv_w']
    proj = _mm("mix_in", h2, full['w_in'], 1, 0, F32)
    assert c_conv == c_ssm and proj.shape[1] == 3 * c_conv
    cpre, an = _conv_fwd(proj, cw, conv_b, conv_ln_g, conv_ln_b, conv_out_g, seq)
    xs, ypre, yg = _s5_fwd(proj, u_blk, bdc, cdc, fix_f, tabs_f, ssm_D, seq, sb)
    q0 = _mm("s5_gate", yg, full['ssm_glu_w'], 1, 0, F32)
    sn = _s5_post2(yg, q0, ssm_glu_b, ssm_out_g)
    wo = full['w_out']
    mixed = jnp.concatenate([an, sn], axis=1)
    x2, h3, h3_t = _mm("mix_out", mixed, wo, 1, 0, addend=x1, post=_post_rms(norm_ffn2))
    arrive(2, [x2])
    a3, b3, z3 = _ffn_up("ffn2_up", h3, full['ffn2_w1'], full['ffn2_w3'])
    saved2 = (h3_t, a3, b3, z3)
    dx3, dx3_t, loss_row, d_norm_final = _mm("ffn2_down", z3, full['ffn2_w2'], 1, 0, addend=x2, alpha=0.5,
                                             post=_post_loss(row(norm_final), tgt))

    g = {}
    dx2, g['norm_ffn2'], sent = _ffn_bwd("ffn2", x2, norm_ffn2, full['ffn2_w1'], full['ffn2_w3'], full['ffn2_w2'], saved2,
                                         dx3, dx3_t, early=False)
    dmixed = _mm("mix_dmixed", dx2, wo, 1, 1, F32)
    dwo = _mm("mix_dwo", mixed, dx2, 0, 0, BF16)
    dq, dyg1, g['ssm_out_g'], g['ssm_glu_b'] = _s5_post2_bwd(dmixed, yg, q0, ssm_glu_b, ssm_out_g)
    dyg2 = _mm("s5_dgate", dq, full['ssm_glu_w'], 1, 1, F32)
    dwg = _mm("s5_dwg", yg, dq, 0, 0, BF16)
    dypre, du_skip, g['ssm_D'] = _s5_post1_bwd(dyg1, dyg2, ypre, proj, ssm_D)
    du, dabar, dbdc, dcdc = _s5_bwd(dypre, du_skip, xs, proj, u_blk, bdc, cdc, fix_b, tabs_b, seq, sb)
    dabar = dabar.reshape(nb, 2, sb)
    dlr, dli, dldt, dbtr, dbti = _s5_params_bwd(lr, li, ldt, btr, bti, dabar[:, 0].reshape(1, ns), dabar[:, 1].reshape(1, ns),
                                                gather_diag(dbdc[:, :, :sb]), gather_diag(dbdc[:, :, sb:]))
    g['ssm_A_re'], g['ssm_A_im'] = dlr, dli
    g['ssm_log_dt'] = dldt.reshape(n_grp, n_state).sum(axis=1)
    g['ssm_B_re'] = dbtr.reshape(grp, n_grp, n_state).transpose(1, 2, 0)
    g['ssm_B_im'] = dbti.reshape(grp, n_grp, n_state).transpose(1, 2, 0)
    g['ssm_C_re'] = gather_diag(dcdc[:, :sb].transpose(0, 2, 1)).reshape(grp, n_grp, n_state).transpose(1, 0, 2)
    g['ssm_C_im'] = -gather_diag(dcdc[:, sb:].transpose(0, 2, 1)).reshape(grp, n_grp, n_state).transpose(1, 0, 2)
    dc, g['conv_out_g'], g['conv_ln_g'], g['conv_ln_b'], g['conv_b'] = _conv_bwd_rows(dmixed, cpre, conv_ln_g, conv_ln_b,
                                                                                    conv_out_g)
    dval, dgate, dcw = _conv_bwd_taps(proj, dc, cw, seq)
    dproj = jnp.concatenate([dval, dgate, du], axis=1)
    dwin = _mm("mix_dwin", h2_t, dproj, 1, 0, BF16)
    sent['w_out ssm_glu_w w_in'] = (_exchange_start("mix_send", "scatter", [dwo, dwg, dwin], [0, 0, 1]), [0, 0, 1])
    dx1, dx1_t, g['norm_mix'] = _mm("mix_dh", dproj, full['w_in'], 1, 1, after=[sent['w_out ssm_glu_w w_in'][0][3]],
                                    post=_post_rms_bwd(x1, norm_mix, dx2))
    dx0, g['norm_ffn1'], sent1 = _ffn_bwd("ffn1", xf, norm_ffn1, full['ffn1_w1'], full['ffn1_w3'], full['ffn1_w2'], saved1,
                                          dx1, dx1_t, early=True)
    sent.update(sent1)
    g['norm_final'] = d_norm_final
    g['conv_w'] = dcw[:n_taps]

    small_shapes = [(n_taps, c_conv) if n == 'conv_w' else wts[n].shape for n in SMALL]
    buf, buf_rows = _pack([g[n] for n in SMALL] + [loss_row])
    to_all = _exchange_start("small_send", "all", [buf], [0])
    slots = {}
    for names, (started, axes) in sent.items():
        lands = _exchange_wait(names.replace(' ', '_') + "_recv", "scatter", started, axes, after=[dx0, to_all[3]])
        slots.update(zip(names.split(), lands))
    sums = [_sum_slots("sum_" + n, slots[n]) for n in BIG]
    to_sibling = _exchange_start("sums_send", "sibling", sums, [0] * len(sums))
    from_all = _exchange_wait("small_recv", "all", to_all, [0], after=[to_sibling[3]])[0]
    total = _unpack(_sum_devices(from_all), buf_rows, small_shapes + [(1, LANE)])
    loss = total[-1][0, 0]
    grads = dict(zip(SMALL, total[:-1]))
    chip = 2 * lax.axis_index("x") + lax.axis_index("y")
    grads['conv_w'] = lax.dynamic_slice_in_dim(grads['conv_w'], chip * c_shard, c_shard, axis=1)[None]
    flat = lambda a: a.reshape(-1, a.shape[-1])
    small = _adamw_many("adamw_small", *[[flat(src[p + n]) for n in SMALL]
                                         for src, p in ((grads, ''), (given, ''), (given, 'm_'), (given, 'v_'))])
    deltas, new_m, new_v = ({n: o.reshape(wts[n].shape) for n, o in zip(SMALL, outs)} for outs in small)

    sums, theirs = _exchange_wait("sums_recv", "sibling", to_sibling, [0] * len(sums), after=[new_v[SMALL[-1]]],
                                  sources_too=True)
    for n, mine, other in zip(BIG, sums, theirs):
        grads[n], deltas[n], new_m[n], new_v[n] = (
            (o.T if n in TRANSPOSED else o)[None]
            for o in _adamw("adamw_" + n, [mine, other], held(n, given[n]), held(n, given['m_' + n]), held(n, given['v_' + n])))

    return (loss, dx0.reshape(x.shape), *[grads[n] for n in WEIGHTS], *[deltas[n] for n in WEIGHTS],
            *[new_m[n] for n in WEIGHTS], *[new_v[n] for n in WEIGHTS])
```

```python
import math
from typing import Callable, NamedTuple

import jax
import jax.numpy as jnp
from jax import lax
from jax.experimental import pallas as pl
from jax.experimental.pallas import tpu as pltpu

F32 = jnp.float32
BF16 = jnp.bfloat16
EPS = 1e-6
ADAM_LR, ADAM_B1, ADAM_B2, ADAM_EPS, ADAM_WD, ADAM_STEP = 0.001, 0.9, 0.999, 1e-08, 0.01, 10
MESH = pl.DeviceIdType.MESH
ANY = pl.BlockSpec(memory_space=pl.ANY)
LANE = 128
SUBLANE = 8
VMEM_LIMIT_BYTES = 56 << 20
ROW_TILE = 256
ROW_TILE_ELEMS = 512 * 1024
WHOLE_ELEMS = 512 * 1024
WHOLE_WEIGHT_BYTES = 8 << 20
FFN_ROWS = 256
CONV_TILE = 256
CONV_SUB = 32
HALO = 32
SCAN_TILE = 256
SCAN_COLS = 512
N_CHIPS = 4
CHIP_RELS = ((1, 0), (0, 1), (1, 1))
NT = (((1,), (1,)), ((), ()))
GELU_K = math.sqrt(2.0 / math.pi)
GELU_C = 0.044715

WEIGHTS = ['norm_ffn1', 'ffn1_w1', 'ffn1_w3', 'ffn1_w2', 'norm_mix', 'w_in', 'conv_w', 'conv_b', 'conv_ln_g', 'conv_ln_b',
           'conv_out_g', 'ssm_A_re', 'ssm_A_im', 'ssm_log_dt', 'ssm_B_re', 'ssm_B_im', 'ssm_C_re', 'ssm_C_im', 'ssm_D',
           'ssm_glu_w', 'ssm_glu_b', 'ssm_out_g', 'w_out', 'norm_ffn2', 'ffn2_w1', 'ffn2_w3', 'ffn2_w2', 'norm_final']
BIG = ['ffn1_w1', 'ffn1_w3', 'ffn1_w2', 'w_in', 'ssm_glu_w', 'w_out', 'ffn2_w1', 'ffn2_w3', 'ffn2_w2']
BIG_AXIS = {'ffn1_w1': 0, 'ffn1_w3': 0, 'ffn1_w2': 0, 'w_in': 1, 'ssm_glu_w': 0, 'w_out': 0, 'ffn2_w1': 0, 'ffn2_w3': 0,
            'ffn2_w2': 0}
TRANSPOSED = ('ffn1_w1', 'ffn1_w3', 'ffn2_w1', 'ffn2_w3')
SMALL = [n for n in WEIGHTS if n not in BIG]


def _round_up(n, m):
    return -(-n // m) * m


def _pick(n, cands):
    for c in cands:
        if c <= n and n % c == 0:
            return c
    return n


def _params(*sem):
    return pltpu.CompilerParams(dimension_semantics=sem, vmem_limit_bytes=VMEM_LIMIT_BYTES)


def _rms_r(x):
    return lax.rsqrt(jnp.mean(x * x, axis=-1, keepdims=True) + EPS)


def _rms_bwd(x, r, g, dy):
    dyg = dy * g
    return r * dyg - x * (r * r * r) * jnp.mean(x * dyg, axis=-1, keepdims=True)


def _sigmoid(x):
    return jax.nn.sigmoid(x)


def _dsilu(a, s):
    return s * (1.0 + a * (1.0 - s))


def _gelu(x):
    return 0.5 * x * (1.0 + jnp.tanh(GELU_K * (x + GELU_C * x * x * x)))


def _dgelu(x):
    t = jnp.tanh(GELU_K * (x + GELU_C * x * x * x))
    return 0.5 * (1.0 + t) + 0.5 * x * (1.0 - t * t) * GELU_K * (1.0 + 3.0 * GELU_C * x * x)


def _colsum(v):
    return jnp.sum(v, axis=0, keepdims=True)


def _rowwise(name, body, n_rows, row_ins, par_ins, row_outs, acc_outs, after=()):
    widest = max([w for (_, w, _) in row_ins] + [w for (w, _) in row_outs])
    tt = _pick(n_rows, [t for t in (1024, 512, 256, 128, 64, 32, 16, 8) if t * widest <= ROW_TILE_ELEMS])
    in_specs = [pl.BlockSpec((tt, w), lambda i, cb=cb: (i, cb)) for (_, w, cb) in row_ins]
    in_specs += [pl.BlockSpec(p.shape, lambda i: (0, 0)) for p in par_ins] + [ANY] * len(after)
    out_specs = [pl.BlockSpec((tt, w), lambda i: (i, 0)) for (w, _) in row_outs]
    out_specs += [pl.BlockSpec((r, w), lambda i: (0, 0)) for (r, w) in acc_outs]
    out_shape = [jax.ShapeDtypeStruct((n_rows, w), dt) for (w, dt) in row_outs]
    out_shape += [jax.ShapeDtypeStruct((r, w), F32) for (r, w) in acc_outs]
    n_in, n_ro = len(row_ins) + len(par_ins), len(row_outs)
    o0 = n_in + len(after)

    def kern(*refs):
        accs = refs[o0 + n_ro:]
        if accs:
            @pl.when(pl.program_id(0) == 0)
            def _():
                for a in accs:
                    a[...] = jnp.zeros_like(a)
        body(refs[:n_in], refs[o0:o0 + n_ro], accs)

    return pl.pallas_call(kern, name=name, grid=(n_rows // tt,), in_specs=in_specs, out_specs=out_specs, out_shape=out_shape,
                          compiler_params=_params("arbitrary"))(*[a for a, _, _ in row_ins], *par_ins, *after)


class Post(NamedTuple):
    rows: list
    gains: list
    outs: list
    t_outs: list
    sums: list
    fn: Callable


def _post_rms(gain):
    def fn(r, rows, gains):
        h = r * _rms_r(r) * gains[0]
        return [r, h, h], []

    return Post([], [gain], [F32, BF16], [BF16], [], fn)


def _post_rms_bwd(x, gain, dres):
    def fn(dh, rows, gains):
        r = _rms_r(rows[0])
        dx = rows[1] + _rms_bwd(rows[0], r, gains[0], dh)
        return [dx, dx], [_colsum(dh * rows[0] * r)]

    return Post([x, dres], [gain], [F32], [BF16], [x.shape[1]], fn)


def _post_loss(gain, tgt):
    d = tgt.shape[1]

    def fn(xv, rows, gains):
        r = _rms_r(xv)
        e = xv * r * gains[0] - rows[0]
        sq = jnp.sum(jnp.sum(e * e, axis=-1, keepdims=True), axis=0, keepdims=True)
        dy = e * (1.0 / d)
        dx = _rms_bwd(xv, r, gains[0], dy)
        return [dx, dx], [jnp.broadcast_to(sq * (0.5 / d), (1, LANE)), _colsum(dy * xv * r)]

    return Post([tgt], [gain], [F32], [BF16], [LANE, d], fn)


def _mm(name, a, b, ca, cb, out_dtype=F32, addend=None, alpha=1.0, a_cols=None, after=(), post=None, transposed=False):
    a_start, a_width = a_cols if a_cols else (0, a.shape[1])
    m, k = (a.shape[0], a_width) if ca == 1 else (a_width, a.shape[0])
    n = b.shape[1 - cb]
    assert b.shape[cb] == k, (name, a.shape, b.shape)
    tn = _pick(n, (1024, 768, 512, 384, 256, 128))
    whole_b = bool(post) and k * tn * b.dtype.itemsize <= WHOLE_WEIGHT_BYTES
    if whole_b:
        tk = k
        tm = _pick(m, (512, 256, 128))
    else:
        tm = _pick(m, (512, 256, 128) if post else (1024, 512, 256, 128))
        deep = (4096,) if a.dtype == BF16 and b.dtype == BF16 else ()
        tk = _pick(k, deep + (2048, 1024, 768, 512, 256, 128) if k >= 4096 and not post else (1024, 768, 512, 256, 128))
    nk = k // tk
    if ca == 1:
        assert a_start % tk == 0
        a_spec = pl.BlockSpec((tm, tk), lambda i, j, kk: (i, kk + a_start // tk))
    else:
        assert a_start % tm == 0
        a_spec = pl.BlockSpec((tk, tm), lambda i, j, kk: (kk, i + a_start // tm))
    b_mode = dict(pipeline_mode=pl.Buffered(1)) if whole_b else {}
    b_spec = (pl.BlockSpec((tk, tn), lambda i, j, kk: (kk, j), **b_mode) if cb == 0 else
              pl.BlockSpec((tn, tk), lambda i, j, kk: (j, kk), **b_mode))
    o_spec = pl.BlockSpec((tm, tn), lambda i, j, kk: (i, j))
    t_spec = pl.BlockSpec((tn, tm), lambda i, j, kk: (j, i))
    fixed = lambda w: pl.BlockSpec((1, w), lambda i, j, kk: (0, 0))
    ins, in_specs = [a, b], [a_spec, b_spec]
    if addend is not None:
        ins.append(addend)
        in_specs.append(o_spec)
    n_plain = len(ins)
    n_rows, n_gains = (len(post.rows), len(post.gains)) if post else (0, 0)
    if post:
        assert tn == n, name
        ins += post.rows + post.gains
        in_specs += [o_spec] * n_rows + [fixed(n)] * n_gains
    ins += list(after)
    in_specs += [ANY] * len(after)
    n_in = len(ins)
    if post:
        n_straight, n_vals = len(post.outs), len(post.outs) + len(post.t_outs)
        out_specs = [o_spec] * n_straight + [t_spec] * len(post.t_outs) + [fixed(w) for w in post.sums]
        out_shape = [jax.ShapeDtypeStruct((m, n), dt) for dt in post.outs] + [jax.ShapeDtypeStruct((n, m), dt) for dt in post.t_outs]
        out_shape += [jax.ShapeDtypeStruct((1, w), F32) for w in post.sums]
    elif transposed:
        out_specs, out_shape = [t_spec], [jax.ShapeDtypeStruct((n, m), out_dtype)]
    else:
        out_specs, out_shape = [o_spec], [jax.ShapeDtypeStruct((m, n), out_dtype)]
    n_out = len(out_specs)
    dims = (((ca,), (cb,)), ((), ()))

    def emit(refs, r):
        if alpha != 1.0:
            r = r * alpha
        if addend is not None:
            r = r + refs[2][...].astype(F32)
        outs = refs[n_in:n_in + n_out]
        if post is None:
            outs[0][...] = (r.T if transposed else r).astype(out_dtype)
            return
        vals, incs = post.fn(r, [q[...] for q in refs[n_plain:n_plain + n_rows]],
                             [q[...] for q in refs[n_plain + n_rows:n_plain + n_rows + n_gains]])
        for at, (o_ref, val) in enumerate(zip(outs, vals)):
            o_ref[...] = (val if at < n_straight else val.T).astype(o_ref.dtype)
        for s_ref, inc in zip(outs[n_vals:], incs):
            s_ref[...] += inc

    def kern(*refs):
        kk = pl.program_id(2)
        if post and post.sums:
            @pl.when(jnp.logical_and(jnp.logical_and(pl.program_id(0) == 0, pl.program_id(1) == 0), kk == 0))
            def _():
                for s_ref in refs[n_in + n_vals:n_in + n_out]:
                    s_ref[...] = jnp.zeros_like(s_ref)

        dot = lambda: lax.dot_general(refs[0][...].astype(BF16), refs[1][...].astype(BF16), dims,
                                      preferred_element_type=F32)
        if nk == 1:
            emit(refs, dot())
            return
        acc_ref = refs[-1]

        @pl.when(kk == 0)
        def _():
            acc_ref[...] = jnp.zeros_like(acc_ref)

        acc_ref[...] += dot()

        @pl.when(kk == nk - 1)
        def _():
            emit(refs, acc_ref[...])

    res = pl.pallas_call(kern, name=name, grid=(m // tm, n // tn, nk), in_specs=in_specs, out_specs=out_specs,
                         out_shape=out_shape, scratch_shapes=[] if nk == 1 else [pltpu.VMEM((tm, tn), F32)],
                         compiler_params=_params("arbitrary", "arbitrary", "arbitrary"))(*ins)
    return res if post else res[0]


def _rms_fwd(name, x, g):
    t, d = x.shape
    tt = _pick(t, (ROW_TILE, LANE))

    def kern(x_ref, g_ref, h_ref, ht_ref):
        xv = x_ref[...]
        h = xv * _rms_r(xv) * g_ref[...]
        h_ref[...] = h.astype(BF16)
        ht_ref[...] = h.T.astype(BF16)

    return pl.pallas_call(kern, name=name, grid=(t // tt,),
                          in_specs=[pl.BlockSpec((tt, d), lambda i: (i, 0)), pl.BlockSpec((1, d), lambda i: (0, 0))],
                          out_specs=[pl.BlockSpec((tt, d), lambda i: (i, 0)), pl.BlockSpec((d, tt), lambda i: (0, i))],
                          out_shape=[jax.ShapeDtypeStruct((t, d), BF16), jax.ShapeDtypeStruct((d, t), BF16)],
                          compiler_params=_params("arbitrary"))(x, g)


def _ffn_up(name, h, w1, w3):
    t, d = h.shape
    ff = w1.shape[0]
    tm, tn = _pick(t, (1024, 512, 256, 128)), _pick(ff, (1024, 768, 512, 256, 128))

    def kern(h_ref, w1_ref, w3_ref, a_ref, b_ref, z_ref):
        hv = h_ref[...]
        a = lax.dot_general(hv, w1_ref[...], NT, preferred_element_type=F32)
        b = lax.dot_general(hv, w3_ref[...], NT, preferred_element_type=F32)
        a_ref[...] = a.astype(BF16)
        b_ref[...] = b.astype(BF16)
        z_ref[...] = (a * _sigmoid(a) * b).astype(BF16)

    w_spec = pl.BlockSpec((tn, d), lambda i, j: (j, 0))
    o_spec = pl.BlockSpec((tm, tn), lambda i, j: (i, j))
    return pl.pallas_call(kern, name=name, grid=(t // tm, ff // tn),
                          in_specs=[pl.BlockSpec((tm, d), lambda i, j: (i, 0)), w_spec, w_spec], out_specs=[o_spec] * 3,
                          out_shape=[jax.ShapeDtypeStruct((t, ff), BF16)] * 3,
                          compiler_params=_params("arbitrary", "arbitrary"))(h, w1, w3)


def _ffn_dglu(name, dxo, w2, a, b, after=()):
    t, d = dxo.shape
    ff = w2.shape[0]
    tm = _pick(t, (2 * FFN_ROWS, 128))

    def kern(dx_ref, w2_ref, a_ref, b_ref, *rest):
        da_ref, db_ref = rest[-2:]
        dz = lax.dot_general(dx_ref[...].astype(BF16), w2_ref[...], NT, preferred_element_type=F32) * 0.5
        av, bv = a_ref[...].astype(F32), b_ref[...].astype(F32)
        s = _sigmoid(av)
        da_ref[...] = (dz * bv * _dsilu(av, s)).astype(BF16)
        db_ref[...] = (dz * av * s).astype(BF16)

    o_spec = pl.BlockSpec((tm, ff), lambda i: (i, 0))
    return pl.pallas_call(kern, name=name, grid=(t // tm,),
                          in_specs=[pl.BlockSpec((tm, d), lambda i: (i, 0)),
                                    pl.BlockSpec((ff, d), lambda i: (0, 0), pipeline_mode=pl.Buffered(1)),
                                    o_spec, o_spec] + [ANY] * len(after),
                          out_specs=[o_spec] * 2, out_shape=[jax.ShapeDtypeStruct((t, ff), BF16)] * 2,
                          compiler_params=_params("arbitrary"))(dxo, w2, a, b, *after)


def _ffn_dh(name, da, db, w1, w3, x, g, dres, after=()):
    t, d = x.shape
    ff = da.shape[1]
    tm = _pick(t, (2 * FFN_ROWS, 128))

    def kern(da_ref, db_ref, w1_ref, w3_ref, x_ref, g_ref, dres_ref, *rest):
        dx_ref, dg_ref = rest[-2:]

        @pl.when(pl.program_id(0) == 0)
        def _():
            dg_ref[...] = jnp.zeros_like(dg_ref)

        dh = (jnp.dot(da_ref[...], w1_ref[...], preferred_element_type=F32)
              + jnp.dot(db_ref[...], w3_ref[...], preferred_element_type=F32))
        xv = x_ref[...]
        r = _rms_r(xv)
        dx_ref[...] = dres_ref[...] + _rms_bwd(xv, r, g_ref[...], dh)
        dg_ref[...] += _colsum(dh * xv * r)

    act = pl.BlockSpec((tm, ff), lambda i: (i, 0))
    wgt = pl.BlockSpec((ff, d), lambda i: (0, 0), pipeline_mode=pl.Buffered(1))
    rows = pl.BlockSpec((tm, d), lambda i: (i, 0))
    gain = pl.BlockSpec((1, d), lambda i: (0, 0))
    return pl.pallas_call(kern, name=name, grid=(t // tm,),
                          in_specs=[act, act, wgt, wgt, rows, gain, rows] + [ANY] * len(after), out_specs=[rows, gain],
                          out_shape=[jax.ShapeDtypeStruct((t, d), F32), jax.ShapeDtypeStruct((1, d), F32)],
                          compiler_params=_params("arbitrary"))(da, db, w1, w3, x, g, dres, *after)


def _ffn_bwd(tag, x, g, w1, w3, w2, saved, dxo, dxo_t, early):
    ht, a, b, z = saved
    dw2 = _mm(tag + "_dw2", dxo_t, z, 1, 0, BF16, alpha=0.5, transposed=True)
    da, db = _ffn_dglu(tag + "_dglu", dxo, w2, a, b)
    dw1 = _mm(tag + "_dw1", ht, da, 1, 0, BF16, transposed=True)
    sent, pin = {}, []
    if early:
        sent[tag + "_w2 " + tag + "_w1"] = (_exchange_start(tag + "_w2_w1_send", "scatter", [dw2, dw1], [0, 0]), [0, 0])
        pin = [sent[tag + "_w2 " + tag + "_w1"][0][3]]
    dw3 = _mm(tag + "_dw3", ht, db, 1, 0, BF16, after=pin, transposed=True)
    last = [dw3] if early else [dw2, dw1, dw3]
    names = [tag + "_w3"] if early else [tag + "_w2", tag + "_w1", tag + "_w3"]
    sent[" ".join(names)] = (_exchange_start(tag + "_w3_send", "scatter", last, [0] * len(last)), [0] * len(last))
    dx, dg = _ffn_dh(tag + "_dh", da, db, w1, w3, x, g, dxo, after=[sent[" ".join(names)][0][3]])
    return dx, dg, sent


def _shift_copies(ext_ref, sh_ref):
    n = ext_ref.shape[0] - SUBLANE
    for r in range(1, SUBLANE):
        sh_ref[r, pl.ds(0, n), :] = ext_ref[pl.ds(r, n), :]


def _rows_at(ext_ref, sh_ref, off, rows):
    r = off % SUBLANE
    return ext_ref[pl.ds(off, rows), :] if r == 0 else sh_ref[r, pl.ds(off - r, rows), :]


def _conv_fwd(proj, cw, cb, lng, lnb, og, seq):
    n_rows, c = proj.shape[0], cb.shape[1]
    kw = HALO - 1
    tt = _pick(seq, (CONV_TILE,))
    hb = tt // HALO

    def kern(v_ref, g_ref, vp_ref, gp_ref, w_ref, cb_ref, lg_ref, lb_ref, og_ref, c_ref, an_ref, ext_ref, sh_ref):
        first = (pl.program_id(0) * tt) % seq == 0
        ext_ref[pl.ds(HALO, tt), :] = v_ref[...] * _sigmoid(g_ref[...])
        ext_ref[pl.ds(0, HALO), :] = vp_ref[...] * _sigmoid(gp_ref[...]) * jnp.where(first, 0.0, 1.0)
        _shift_copies(ext_ref, sh_ref)
        for r0 in range(0, tt, CONV_SUB):
            rows = min(CONV_SUB, tt - r0)
            acc = jnp.zeros((rows, c), F32)
            for k in range(kw):
                acc = acc + w_ref[pl.ds(k, 1), :] * _rows_at(ext_ref, sh_ref, r0 + HALO - (kw - 1) + k, rows)
            c_ref[pl.ds(r0, rows), :] = acc + cb_ref[...]
        cv = c_ref[...]
        mu = jnp.mean(cv, axis=-1, keepdims=True)
        xc = cv - mu
        rstd = lax.rsqrt(jnp.mean(xc * xc, axis=-1, keepdims=True) + EPS)
        lv = xc * rstd * lg_ref[...] + lb_ref[...]
        sl = lv * _sigmoid(lv)
        an_ref[...] = (sl * _rms_r(sl) * og_ref[...]).astype(BF16)

    cur = lambda cbk: pl.BlockSpec((tt, c), lambda i: (i, cbk))
    prev = lambda cbk: pl.BlockSpec((HALO, c), lambda i: (jnp.maximum(i * hb - 1, 0), cbk))
    par = lambda p: pl.BlockSpec(p.shape, lambda i: (0, 0))
    return pl.pallas_call(
        kern, name="conv_fwd", grid=(n_rows // tt,),
        in_specs=[cur(0), cur(1), prev(0), prev(1), par(cw), par(cb), par(lng), par(lnb), par(og)],
        out_specs=[pl.BlockSpec((tt, c), lambda i: (i, 0))] * 2,
        out_shape=[jax.ShapeDtypeStruct((n_rows, c), F32), jax.ShapeDtypeStruct((n_rows, c), BF16)],
        scratch_shapes=[pltpu.VMEM((tt + HALO, c), F32), pltpu.VMEM((SUBLANE, tt + HALO, c), F32)],
        compiler_params=_params("arbitrary"),
    )(proj, proj, proj, proj, cw, cb, lng, lnb, og)


def _conv_bwd_rows(dmixed, cpre, lng, lnb, og):
    c = cpre.shape[1]

    def body(ins, outs, accs):
        dan, cv, lg, lb, ogv = ins[0][...], ins[1][...], ins[2][...], ins[3][...], ins[4][...]
        mu = jnp.mean(cv, axis=-1, keepdims=True)
        xc = cv - mu
        rstd = lax.rsqrt(jnp.mean(xc * xc, axis=-1, keepdims=True) + EPS)
        xh = xc * rstd
        lv = xh * lg + lb
        s = _sigmoid(lv)
        sl = lv * s
        r2 = _rms_r(sl)
        accs[0][...] += _colsum(dan * sl * r2)
        dl = _rms_bwd(sl, r2, ogv, dan) * _dsilu(lv, s)
        accs[1][...] += _colsum(dl * xh)
        accs[2][...] += _colsum(dl)
        dxh = dl * lg
        dc = rstd * (dxh - jnp.mean(dxh, axis=-1, keepdims=True) - xh * jnp.mean(dxh * xh, axis=-1, keepdims=True))
        outs[0][...] = dc
        accs[3][...] += _colsum(dc)

    return _rowwise("conv_bwd_rows", body, cpre.shape[0], [(dmixed, c, 0), (cpre, c, 0)], [lng, lnb, og], [(c, F32)],
                    [(1, c)] * 4)


def _conv_bwd_taps(proj, dc, cw, seq):
    n_rows, c = dc.shape
    kw = HALO - 1
    tt = _pick(seq, (CONV_TILE,))
    hb = tt // HALO
    last_blk = n_rows // HALO - 1

    def kern(v_ref, g_ref, vp_ref, gp_ref, dc_ref, dn_ref, w_ref, dv_ref, dg_ref, dw_ref, exta_ref, extd_ref, sha_ref, shd_ref):
        i = pl.program_id(0)
        first = (i * tt) % seq == 0
        last = ((i + 1) * tt) % seq == 0

        @pl.when(i == 0)
        def _():
            dw_ref[...] = jnp.zeros_like(dw_ref)

        sg = _sigmoid(g_ref[...])
        exta_ref[pl.ds(HALO, tt), :] = v_ref[...] * sg
        exta_ref[pl.ds(0, HALO), :] = vp_ref[...] * _sigmoid(gp_ref[...]) * jnp.where(first, 0.0, 1.0)
        extd_ref[pl.ds(0, tt), :] = dc_ref[...]
        extd_ref[pl.ds(tt, HALO), :] = dn_ref[...] * jnp.where(last, 0.0, 1.0)
        _shift_copies(exta_ref, sha_ref)
        _shift_copies(extd_ref, shd_ref)
        for k0 in range(0, kw, SUBLANE):
            taps = range(k0, min(k0 + SUBLANE, kw))
            sums = [jnp.zeros((SUBLANE, c), F32) for _ in taps]
            for r0 in range(0, tt, SUBLANE):
                dcb = dc_ref[pl.ds(r0, SUBLANE), :]
                for n, k in enumerate(taps):
                    sums[n] = sums[n] + _rows_at(exta_ref, sha_ref, r0 + HALO - (kw - 1) + k, SUBLANE) * dcb
            for n, k in enumerate(taps):
                dw_ref[pl.ds(k, 1), :] += _colsum(sums[n])
        for r0 in range(0, tt, CONV_SUB):
            rows = min(CONV_SUB, tt - r0)
            acc = jnp.zeros((rows, c), F32)
            for k in range(kw):
                acc = acc + w_ref[pl.ds(k, 1), :] * _rows_at(extd_ref, shd_ref, r0 + (kw - 1) - k, rows)
            dv_ref[pl.ds(r0, rows), :] = acc
        da = dv_ref[...]
        dv_ref[...] = da * sg
        dg_ref[...] = da * v_ref[...] * sg * (1.0 - sg)

    cur = lambda cbk: pl.BlockSpec((tt, c), lambda i: (i, cbk))
    prev = lambda cbk: pl.BlockSpec((HALO, c), lambda i: (jnp.maximum(i * hb - 1, 0), cbk))
    nxt = pl.BlockSpec((HALO, c), lambda i: (jnp.minimum((i + 1) * hb, last_blk), 0))
    return pl.pallas_call(
        kern, name="conv_bwd_taps", grid=(n_rows // tt,),
        in_specs=[cur(0), cur(1), prev(0), prev(1), cur(0), nxt, pl.BlockSpec(cw.shape, lambda i: (0, 0))],
        out_specs=[cur(0), cur(0), pl.BlockSpec((HALO, c), lambda i: (0, 0))],
        out_shape=[jax.ShapeDtypeStruct((n_rows, c), F32), jax.ShapeDtypeStruct((n_rows, c), F32),
                   jax.ShapeDtypeStruct((HALO, c), F32)],
        scratch_shapes=[pltpu.VMEM((tt + HALO, c), F32)] * 2 + [pltpu.VMEM((SUBLANE, tt + HALO, c), F32)] * 2,
        compiler_params=_params("arbitrary"),
    )(proj, proj, proj, proj, dc, dc, cw)


def _s5_params_fwd(lr, li, ldt, btr, bti, seg):
    ns = lr.shape[1]

    def kern(lr_ref, li_ref, ldt_ref, btr_ref, bti_ref, ar_ref, ai_ref, bbr_ref, bbi_ref, ps_ref, psf_ref, pc_ref, pcf_ref):
        lrv, liv = lr_ref[...], li_ref[...]
        dt = jnp.exp(ldt_ref[...])
        zr, zi = lrv * dt, liv * dt
        mag = jnp.exp(zr)
        ar, ai = mag * jnp.cos(zi), mag * jnp.sin(zi)
        den = lrv * lrv + liv * liv
        nr = ar - 1.0
        cr = (nr * lrv + ai * liv) / den
        ci = (ai * lrv - nr * liv) / den
        ar_ref[...] = ar
        ai_ref[...] = ai
        bbr_ref[...] = cr * btr_ref[...] - ci * bti_ref[...]
        bbi_ref[...] = cr * bti_ref[...] + ci * btr_ref[...]
        def powers(br, bi, count, up_ref, down_ref):
            pr, pi = br, bi
            for e in range(count):
                for ref, at in ((up_ref, e), (down_ref, count - 1 - e)):
                    ref[pl.ds(at, 1), pl.ds(0, ns)] = pr
                    ref[pl.ds(at, 1), pl.ds(ns, ns)] = pi
                if e < count - 1:
                    pr, pi = pr * br - pi * bi, pr * bi + pi * br
            return pr, pi

        powers(*powers(ar, ai, seg, ps_ref, psf_ref), SUBLANE, pc_ref, pcf_ref)

    h = btr.shape[0]
    shapes = [jax.ShapeDtypeStruct((1, ns), F32)] * 2 + [jax.ShapeDtypeStruct((h, ns), F32)] * 2
    shapes += [jax.ShapeDtypeStruct((seg, 2 * ns), F32)] * 2 + [jax.ShapeDtypeStruct((SUBLANE, 2 * ns), F32)] * 2
    return pl.pallas_call(kern, name="s5_params_fwd", out_shape=shapes)(lr, li, ldt, btr, bti)


def _s5_params_bwd(lr, li, ldt, btr, bti, dar, dai, dbbr, dbbi):
    def kern(lr_ref, li_ref, ldt_ref, btr_ref, bti_ref, dar_ref, dai_ref, dbr_ref, dbi_ref,
             dlr_ref, dli_ref, dldt_ref, dbtr_ref, dbti_ref):
        lrv, liv = lr_ref[...], li_ref[...]
        dt = jnp.exp(ldt_ref[...])
        zr, zi = lrv * dt, liv * dt
        mag = jnp.exp(zr)
        ar, ai = mag * jnp.cos(zi), mag * jnp.sin(zi)
        den = lrv * lrv + liv * liv
        nr = ar - 1.0
        cr = (nr * lrv + ai * liv) / den
        ci = (ai * lrv - nr * liv) / den
        dbr, dbi, br, bi = dbr_ref[...], dbi_ref[...], btr_ref[...], bti_ref[...]
        dbtr_ref[...] = cr * dbr + ci * dbi
        dbti_ref[...] = cr * dbi - ci * dbr
        dcr = _colsum(br * dbr + bi * dbi)
        dci = _colsum(br * dbi - bi * dbr)
        ir, ii = lrv / den, -liv / den
        dnr = ir * dcr + ii * dci
        dni = ir * dci - ii * dcr
        wr, wi = cr * ir - ci * ii, cr * ii + ci * ir
        dl1r = -(wr * dcr + wi * dci)
        dl1i = -(wr * dci - wi * dcr)
        dtr, dti = dar_ref[...] + dnr, dai_ref[...] + dni
        dzr = ar * dtr + ai * dti
        dzi = ar * dti - ai * dtr
        dlr_ref[...] = dl1r + dt * dzr
        dli_ref[...] = dl1i + dt * dzi
        dldt_ref[...] = (dzr * lrv + dzi * liv) * dt

    ns, h = lr.shape[1], btr.shape[0]
    shapes = [jax.ShapeDtypeStruct((1, ns), F32)] * 3 + [jax.ShapeDtypeStruct((h, ns), F32)] * 2
    return pl.pallas_call(kern, name="s5_params_bwd", out_shape=shapes)(lr, li, ldt, btr, bti, dar, dai, dbbr, dbbi)


def _to_segments(nat_ref, seg_ref):
    steps = nat_ref.shape[0] // SUBLANE
    _regroup(nat_ref, seg_ref, lambda r: (r % SUBLANE) * steps + r // SUBLANE)


def _from_segments(seg_ref, nat_ref):
    steps = nat_ref.shape[0] // SUBLANE
    _regroup(seg_ref, nat_ref, lambda r: (r % steps) * SUBLANE + r // steps)


def _regroup(src_ref, dst_ref, src_row):
    rows, width = dst_ref.shape
    sublane = lax.broadcasted_iota(jnp.int32, (SUBLANE, width), 0)
    for r0 in range(0, rows, SUBLANE):
        tile = jnp.broadcast_to(src_ref[pl.ds(src_row(r0), 1), :], (SUBLANE, width))
        for k in range(1, SUBLANE):
            tile = jnp.where(sublane == k, src_ref[pl.ds(src_row(r0 + k), 1), :], tile)
        dst_ref[pl.ds(r0, SUBLANE), :] = tile


def _scan_tile(s_ref, o_ref, fix_ref, tabs, car_ref, sb, reverse, x_ref=None, acc_ref=None):
    l1, l2, l4, pw = tabs
    rows_t, w = s_ref.shape
    steps = rows_t // SUBLANE
    cw = _pick(sb, (SCAN_COLS,))
    last = 0 if reverse else SUBLANE - 1
    first = SUBLANE - 1 - last
    row = lax.broadcasted_iota(jnp.int32, (SUBLANE, cw), 0)
    step_rows = lambda i: pl.ds(pl.multiple_of(((steps - 1 - i) if reverse else i) * SUBLANE, SUBLANE), SUBLANE)
    zero = jnp.zeros((SUBLANE, cw), F32)

    for c0 in [b0 + o for b0 in range(0, w, 2 * sb) for o in range(0, sb, cw)]:
        cr, ci = pl.ds(c0, cw), pl.ds(c0 + sb, cw)
        base = pl.ds(((steps - 1) if reverse else 0) * SUBLANE, SUBLANE)
        ar, ai = fix_ref[base, cr], fix_ref[base, ci]

        def run(i, state):
            xr, xi = state
            rows = step_rows(i)
            xr, xi = ar * xr - ai * xi + s_ref[rows, cr], ar * xi + ai * xr + s_ref[rows, ci]
            o_ref[rows, cr] = xr
            o_ref[rows, ci] = xi
            return xr, xi

        fr, fi = lax.fori_loop(0, steps, run, (zero, zero))
        for s, lt in ((1, l1), (2, l2), (4, l4)):
            sh = (SUBLANE - s) if reverse else s
            sr, si = pltpu.roll(fr, sh, 0), pltpu.roll(fi, sh, 0)
            tr, ti = lt[:, cr], lt[:, ci]
            fr, fi = fr + tr * sr - ti * si, fi + tr * si + ti * sr
        kr, ki = car_ref[pl.ds(last, 1), cr], car_ref[pl.ds(last, 1), ci]
        pr, pi = pw[:, cr], pw[:, ci]
        fr, fi = fr + pr * kr - pi * ki, fi + pr * ki + pi * kr
        car_ref[:, cr] = fr
        car_ref[:, ci] = fi
        to_next = 1 if not reverse else SUBLANE - 1
        gr = jnp.where(row == first, kr, pltpu.roll(fr, to_next, 0))
        gi = jnp.where(row == first, ki, pltpu.roll(fi, to_next, 0))

        def fix(i, state):
            rows = step_rows(i)
            qr, qi = fix_ref[rows, cr], fix_ref[rows, ci]
            yr = o_ref[rows, cr] + qr * gr - qi * gi
            yi = o_ref[rows, ci] + qr * gi + qi * gr
            o_ref[rows, cr] = yr
            o_ref[rows, ci] = yi
            if acc_ref is None:
                return state
            nr, ni, sr, si = state
            pxr, pxi = x_ref[rows, cr], x_ref[rows, ci]
            return yr, yi, sr + nr * pxr + ni * pxi, si + ni * pxr - nr * pxi

        if acc_ref is None:
            lax.fori_loop(0, steps, fix, 0)
        else:
            _, _, sr, si = lax.fori_loop(0, steps, fix, (gr, gi, zero, zero))
            acc_ref[:, cr] += sr
            acc_ref[:, ci] += si


def _s5_fwd(proj, u_blk, bdc, cdc, fix, tabs, dskip, seq, sb):
    n_rows = proj.shape[0]
    nb, blk, w_blk = bdc.shape
    c, w = nb * blk, nb * w_blk
    tt = fix.shape[0]

    def kern(u_ref, bd_ref, cd_ref, fix_ref, l1, l2, l4, pw, d_ref, xs_ref, yp_ref, yg_ref, us_ref, bu_ref, car_ref):
        @pl.when((pl.program_id(0) * tt) % seq == 0)
        def _():
            car_ref[...] = jnp.zeros_like(car_ref)

        _to_segments(u_ref, us_ref)
        for j in range(nb):
            bu_ref[:, pl.ds(j * w_blk, w_blk)] = jnp.dot(us_ref[:, pl.ds(j * blk, blk)].astype(BF16), bd_ref[j],
                                                         preferred_element_type=F32)
        _scan_tile(bu_ref, xs_ref, fix_ref, (l1, l2, l4, pw), car_ref, sb, False)
        for j in range(nb):
            cols = pl.ds(j * blk, blk)
            y0 = jnp.dot(xs_ref[:, pl.ds(j * w_blk, w_blk)].astype(BF16), cd_ref[j], preferred_element_type=F32)
            us_ref[:, cols] = y0 + d_ref[:, cols] * us_ref[:, cols]
        _from_segments(us_ref, yp_ref)
        yg_ref[...] = _gelu(yp_ref[...]).astype(BF16)

    tab = pl.BlockSpec((SUBLANE, w), lambda i: (0, 0))
    rows = pl.BlockSpec((tt, c), lambda i: (i, 0))
    return pl.pallas_call(
        kern, name="s5_fwd", grid=(n_rows // tt,),
        in_specs=[pl.BlockSpec((tt, c), lambda i: (i, u_blk * blk // c)), pl.BlockSpec(bdc.shape, lambda i: (0, 0, 0)),
                  pl.BlockSpec(cdc.shape, lambda i: (0, 0, 0)), pl.BlockSpec((tt, w), lambda i: (0, 0)), tab, tab, tab, tab,
                  pl.BlockSpec((1, c), lambda i: (0, 0))],
        out_specs=[pl.BlockSpec((tt, w), lambda i: (i, 0)), rows, rows],
        out_shape=[jax.ShapeDtypeStruct((n_rows, w), F32), jax.ShapeDtypeStruct((n_rows, c), F32),
                   jax.ShapeDtypeStruct((n_rows, c), BF16)],
        scratch_shapes=[pltpu.VMEM((tt, c), F32), pltpu.VMEM((tt, w), F32), pltpu.VMEM((SUBLANE, w), F32)],
        compiler_params=_params("arbitrary"))(proj, bdc, cdc, fix, *tabs, dskip)


def _s5_bwd(dypre, du_skip, xs, proj, u_blk, bdc, cdc, fix, tabs, seq, sb):
    n_rows = proj.shape[0]
    nb, blk, w_blk = bdc.shape
    c, w = nb * blk, nb * w_blk
    tt = fix.shape[0]
    nt = n_rows // tt
    tn = (((0,), (0,)), ((), ()))

    def kern(dy_ref, ds_ref, x_ref, u_ref, bd_ref, cd_ref, fix_ref, l1, l2, l4, pw, du_ref, da_ref, db_ref, dc_ref,
             dys_ref, us_ref, dus_ref, gx_ref, lam_ref, car_ref, acc_ref):
        i = pl.program_id(0)

        @pl.when(((nt - i) * tt) % seq == 0)
        def _():
            car_ref[...] = jnp.zeros_like(car_ref)

        @pl.when(i == 0)
        def _():
            acc_ref[...] = jnp.zeros_like(acc_ref)
            db_ref[...] = jnp.zeros_like(db_ref)
            dc_ref[...] = jnp.zeros_like(dc_ref)

        _to_segments(dy_ref, dys_ref)
        _to_segments(u_ref, us_ref)
        for j in range(nb):
            gx_ref[:, pl.ds(j * w_blk, w_blk)] = lax.dot_general(dys_ref[:, pl.ds(j * blk, blk)].astype(BF16), cd_ref[j], NT,
                                                                 preferred_element_type=F32)
        _scan_tile(gx_ref, lam_ref, fix_ref, (l1, l2, l4, pw), car_ref, sb, True, x_ref, acc_ref)
        for j in range(nb):
            cols, wide = pl.ds(j * blk, blk), pl.ds(j * w_blk, w_blk)
            lam = lam_ref[:, wide].astype(BF16)
            dus_ref[:, cols] = lax.dot_general(lam, bd_ref[j], NT, preferred_element_type=F32)
            db_ref[j] += lax.dot_general(us_ref[:, cols].astype(BF16), lam, tn, preferred_element_type=F32)
            dc_ref[j] += lax.dot_general(x_ref[:, wide].astype(BF16), dys_ref[:, cols].astype(BF16), tn,
                                         preferred_element_type=F32)
        _from_segments(dus_ref, du_ref)
        du_ref[...] += ds_ref[...]

        @pl.when(i == nt - 1)
        def _():
            da_ref[...] = _colsum(acc_ref[...])

    back = lambda i: (nt - 1 - i, 0)
    tab = pl.BlockSpec((SUBLANE, w), lambda i: (0, 0))
    rows = pl.BlockSpec((tt, c), back)
    whole = lambda a: pl.BlockSpec(a.shape, lambda i: (0, 0, 0))
    return pl.pallas_call(
        kern, name="s5_bwd", grid=(nt,),
        in_specs=[rows, rows, pl.BlockSpec((tt, w), back), pl.BlockSpec((tt, c), lambda i: (nt - 1 - i, u_blk * blk // c)),
                  whole(bdc), whole(cdc), pl.BlockSpec((tt, w), lambda i: (0, 0)), tab, tab, tab, tab],
        out_specs=[rows, pl.BlockSpec((1, w), lambda i: (0, 0)), whole(bdc), whole(cdc)],
        out_shape=[jax.ShapeDtypeStruct((n_rows, c), F32), jax.ShapeDtypeStruct((1, w), F32),
                   jax.ShapeDtypeStruct(bdc.shape, F32), jax.ShapeDtypeStruct(cdc.shape, F32)],
        scratch_shapes=[pltpu.VMEM((tt, c), F32)] * 3 + [pltpu.VMEM((tt, w), F32)] * 2 + [pltpu.VMEM((SUBLANE, w), F32)] * 2,
        compiler_params=_params("arbitrary"))(dypre, du_skip, xs, proj, bdc, cdc, fix, *tabs)


def _s5_post2(yg, q0, bg, og):
    c = yg.shape[1]

    def body(ins, outs, accs):
        ygv = ins[0][...].astype(F32)
        sg = ygv * _sigmoid(ins[1][...] + ins[2][...])
        outs[0][...] = (sg * _rms_r(sg) * ins[3][...]).astype(BF16)

    return _rowwise("s5_post2", body, yg.shape[0], [(yg, c, 0), (q0, c, 0)], [bg, og], [(c, BF16)], [])[0]


def _s5_post2_bwd(dmixed, yg, q0, bg, og):
    c = yg.shape[1]

    def body(ins, outs, accs):
        dsn, ygv = ins[0][...], ins[1][...].astype(F32)
        s = _sigmoid(ins[2][...] + ins[3][...])
        sg = ygv * s
        r = _rms_r(sg)
        accs[0][...] += _colsum(dsn * sg * r)
        dsg = _rms_bwd(sg, r, ins[4][...], dsn)
        dq = dsg * ygv * s * (1.0 - s)
        outs[0][...] = dq.astype(BF16)
        outs[1][...] = dsg * s
        accs[1][...] += _colsum(dq)

    return _rowwise("s5_post2_bwd", body, yg.shape[0], [(dmixed, c, 1), (yg, c, 0), (q0, c, 0)], [bg, og],
                    [(c, BF16), (c, F32)], [(1, c)] * 2)


def _s5_post1_bwd(dyg1, dyg2, ypre, proj, dskip, after=()):
    c = ypre.shape[1]

    def body(ins, outs, accs):
        dyp = (ins[0][...] + ins[1][...]) * _dgelu(ins[2][...])
        outs[0][...] = dyp
        outs[1][...] = dyp * ins[4][...]
        accs[0][...] += _colsum(dyp * ins[3][...])

    return _rowwise("s5_post1_bwd", body, ypre.shape[0], [(dyg1, c, 0), (dyg2, c, 0), (ypre, c, 0), (proj, c, 2)], [dskip],
                    [(c, F32), (c, F32)], [(1, c)], after=after)


def _place():
    return lax.axis_index("x"), lax.axis_index("y"), lax.axis_index("c")


def _window(ref, axis, q, rows, cols):
    if axis == 0:
        return ref.at[pl.ds(pl.multiple_of(q * rows, SUBLANE), rows), :]
    return ref.at[:, pl.ds(pl.multiple_of(q * cols, LANE), cols)]


ALL_RELS = [(fx, fy, fc) for fx in (0, 1) for fy in (0, 1) for fc in (0, 1)][1:]
N_PEERS = {"gather": 3, "scatter": 3, "sibling": 1, "all": len(ALL_RELS)}


def _copies(kind, srcs, lands, shards, axes, send_sems, recv_sems, local_sems):
    x, y, c = _place()
    me, dev = 2 * x + y, 4 * x + 2 * y + c
    n_peers = N_PEERS[kind]
    starts, waits = [], []
    for a, (src, land) in enumerate(zip(srcs, lands)):
        on = lambda k, peer: dict(send_sem=send_sems.at[n_peers * a + k], recv_sem=recv_sems.at[n_peers * a + k],
                                  device_id=peer, device_id_type=MESH)
        if kind == "sibling":
            cp = pltpu.make_async_remote_copy(src_ref=src, dst_ref=land, **on(0, (x, y, 1 - c)))
            starts.append(cp)
            waits.append(cp)
            continue
        if kind == "all":
            own = pltpu.make_async_copy(src, land.at[dev], local_sems.at[a])
            starts.append(own)
            waits.append(own)
            for k, (fx, fy, fc) in enumerate(ALL_RELS):
                px, py, pc = (1 - x) if fx else x, (1 - y) if fy else y, (1 - c) if fc else c
                starts.append(pltpu.make_async_remote_copy(src_ref=src, dst_ref=land.at[dev], **on(k, (px, py, pc))))
                waits.append(pltpu.make_async_remote_copy(src_ref=src, dst_ref=land.at[4 * px + 2 * py + pc],
                                                          **on(k, (px, py, pc))))
            continue
        rows, cols = shards[a]
        if kind == "gather":
            own = pltpu.make_async_copy(src, _window(land, axes[a], me, rows, cols), local_sems.at[a])
        else:
            own = pltpu.make_async_copy(_window(src, axes[a], me, rows, cols), land.at[3], local_sems.at[a])
        starts.append(own)
        waits.append(own)
        for j, (fx, fy) in enumerate(CHIP_RELS):
            px, py = (1 - x) if fx else x, (1 - y) if fy else y
            peer = 2 * px + py
            if kind == "gather":
                starts.append(pltpu.make_async_remote_copy(src_ref=src, dst_ref=_window(land, axes[a], me, rows, cols),
                                                           **on(j, (px, py, c))))
                waits.append(pltpu.make_async_remote_copy(src_ref=src, dst_ref=_window(land, axes[a], peer, rows, cols),
                                                          **on(j, (px, py, c))))
            else:
                cp = pltpu.make_async_remote_copy(src_ref=_window(src, axes[a], peer, rows, cols), dst_ref=land.at[j],
                                                  **on(j, (px, py, c)))
                starts.append(cp)
                waits.append(cp)
    return starts, waits


HBM = pl.BlockSpec(memory_space=pltpu.HBM)
SEM = pl.BlockSpec(memory_space=pltpu.SEMAPHORE)


def _shard_shapes(kind, arrs, axes):
    if kind != "scatter":
        return [a.shape for a in arrs]
    return [(a.shape[0] // N_CHIPS, a.shape[1]) if ax == 0 else (a.shape[0], a.shape[1] // N_CHIPS) for a, ax in zip(arrs, axes)]


def _land_shapes(kind, arrs, axes):
    if kind == "gather":
        return [(N_CHIPS * a.shape[0], a.shape[1]) if ax == 0 else (a.shape[0], N_CHIPS * a.shape[1]) for a, ax in zip(arrs, axes)]
    if kind == "scatter":
        return [(N_CHIPS,) + s for s in _shard_shapes(kind, arrs, axes)]
    return [a.shape if kind == "sibling" else (len(ALL_RELS) + 1,) + a.shape for a in arrs]


def _exchange_start(name, kind, arrs, axes, after=()):
    n, n_after = len(arrs), len(after)
    shards = _shard_shapes(kind, arrs, axes)
    land_shapes = _land_shapes(kind, arrs, axes)
    lands = [lax.empty(s, a.dtype) for s, a in zip(land_shapes, arrs)]

    def kern(*refs):
        outs = refs[2 * n + n_after:]
        starts, _ = _copies(kind, refs[:n], refs[n:2 * n], shards, axes, outs[0], outs[1], outs[2])
        for cp in starts:
            cp.start()
        outs[-1][...] = jnp.zeros_like(outs[-1])

    kept = [pltpu.HBM(a.shape, a.dtype) for a in arrs] + [pltpu.HBM(s, a.dtype) for s, a in zip(land_shapes, arrs)]
    n_sems = N_PEERS[kind] * n
    res = pl.pallas_call(
        kern, name=name, in_specs=[HBM] * (2 * n) + [ANY] * n_after,
        out_specs=[SEM] * 3 + [HBM] * (2 * n) + [pl.BlockSpec(memory_space=pltpu.VMEM)],
        out_shape=[pltpu.SemaphoreType.DMA((n_sems,)), pltpu.SemaphoreType.DMA((n_sems,)), pltpu.SemaphoreType.DMA((n,))]
        + kept + [jax.ShapeDtypeStruct((SUBLANE, LANE), F32)],
        input_output_aliases={i: 3 + i for i in range(2 * n)},
        compiler_params=pltpu.CompilerParams(has_side_effects=pltpu.SideEffectType.DATAFLOW_SIDE_EFFECTING),
    )(*[pltpu.with_memory_space_constraint(a, pltpu.HBM) for a in list(arrs) + lands], *after)
    return res[:3], res[3:3 + n], res[3 + n:3 + 2 * n], res[-1]


def _exchange_wait(name, kind, started, axes, after, sources_too=False):
    sems, srcs, lands, _ = started
    n, n_after = len(srcs), len(after)
    shards = _shard_shapes(kind, srcs, axes)

    def kern(*refs):
        sem_refs = refs[2 * n:2 * n + 3]
        _, waits = _copies(kind, refs[:n], refs[n:2 * n], shards, axes, *sem_refs)
        for cp in waits:
            cp.wait()

    res = pl.pallas_call(
        kern, name=name, in_specs=[HBM] * (2 * n) + [SEM] * 3 + [ANY] * n_after, out_specs=[HBM] * (2 * n),
        out_shape=[pltpu.HBM(a.shape, a.dtype) for a in list(srcs) + list(lands)],
        input_output_aliases={i: i for i in range(2 * n)},
        compiler_params=pltpu.CompilerParams(has_side_effects=pltpu.SideEffectType.DATAFLOW_SIDE_EFFECTING),
    )(*srcs, *lands, *sems, *after)
    return (res[:n], res[n:]) if sources_too else res[n:]


def _sum_devices(parts):
    def kern(p_ref, o_ref):
        acc = p_ref[0]
        for d in range(1, parts.shape[0]):
            acc = acc + p_ref[d]
        o_ref[...] = acc

    return pl.pallas_call(kern, name="sum_devices", out_shape=jax.ShapeDtypeStruct(parts.shape[1:], F32),
                          compiler_params=pltpu.CompilerParams(vmem_limit_bytes=VMEM_LIMIT_BYTES))(parts)


def _sum_slots(name, parts):
    _, rows, cols = parts.shape
    tr = _pick(rows, (ROW_TILE, 128, 64, 32))

    def kern(p_ref, o_ref):
        o_ref[...] = ((p_ref[3].astype(F32) + p_ref[0].astype(F32)) + p_ref[1].astype(F32)) + p_ref[2].astype(F32)

    return pl.pallas_call(kern, name=name, grid=(rows // tr,),
                          in_specs=[pl.BlockSpec((N_CHIPS, tr, cols), lambda i: (0, i, 0))],
                          out_specs=pl.BlockSpec((tr, cols), lambda i: (i, 0)),
                          out_shape=jax.ShapeDtypeStruct((rows, cols), F32), compiler_params=_params("arbitrary"))(parts)


def _adamw_math(g, w, m, v):
    m2 = ADAM_B1 * m + (1.0 - ADAM_B1) * g
    v2 = ADAM_B2 * v + (1.0 - ADAM_B2) * (g * g)
    m_hat = m2 / (1.0 - ADAM_B1 ** ADAM_STEP)
    v_hat = v2 / (1.0 - ADAM_B2 ** ADAM_STEP)
    return -ADAM_LR * (m_hat / (jnp.sqrt(v_hat) + ADAM_EPS) + ADAM_WD * w), m2, v2


def _adamw(name, parts, w, m, v):
    rows, cols = w.shape
    tr = rows if rows * cols <= WHOLE_ELEMS else _pick(rows, (ROW_TILE, 352, 128, 64, 32, 8))
    n = len(parts)

    def kern(*refs):
        g = refs[0][:, pl.ds(0, cols)]
        for p in refs[1:n]:
            g = g + p[:, pl.ds(0, cols)]
        d, m2, v2 = _adamw_math(g, refs[n][...], refs[n + 1][...], refs[n + 2][...])
        refs[n + 3][...] = g
        refs[n + 4][...] = d
        refs[n + 5][...] = m2
        refs[n + 6][...] = v2

    spec = pl.BlockSpec((tr, cols), lambda i: (i, 0))
    return pl.pallas_call(kern, name=name, grid=(rows // tr,),
                          in_specs=[pl.BlockSpec((tr, p.shape[1]), lambda i: (i, 0)) for p in parts] + [spec] * 3,
                          out_specs=[spec] * 4, out_shape=[jax.ShapeDtypeStruct((rows, cols), F32)] * 4,
                          compiler_params=_params("arbitrary"))(*parts, w, m, v)


def _adamw_many(name, gs, ws, ms, vs):
    n = len(gs)

    def kern(*refs):
        for p in range(n):
            d, m2, v2 = _adamw_math(refs[p][...], refs[n + p][...], refs[2 * n + p][...], refs[3 * n + p][...])
            refs[4 * n + p][...] = d
            refs[5 * n + p][...] = m2
            refs[6 * n + p][...] = v2

    res = pl.pallas_call(kern, name=name, out_shape=[jax.ShapeDtypeStruct(w.shape, F32) for w in ws] * 3,
                         compiler_params=pltpu.CompilerParams(vmem_limit_bytes=VMEM_LIMIT_BYTES))(*gs, *ws, *ms, *vs)
    return res[:n], res[n:2 * n], res[2 * n:]


def _pack(arrs):
    parts, rows = [], []
    for a in arrs:
        r = _round_up(-(-a.size // LANE), SUBLANE)
        parts.append(jnp.pad(a.reshape(-1).astype(F32), (0, r * LANE - a.size)).reshape(r, LANE))
        rows.append(r)
    return jnp.concatenate(parts, axis=0), rows


def _unpack(buf, rows, shapes):
    out, r0 = [], 0
    for r, s in zip(rows, shapes):
        size = math.prod(s)
        out.append(buf[r0:r0 + r].reshape(-1)[:size].reshape(s))
        r0 += r
    return out


def kernel(x, norm_ffn1, ffn1_w1, ffn1_w3, ffn1_w2, norm_mix, w_in, conv_w, conv_b, conv_ln_g, conv_ln_b, conv_out_g, ssm_A_re, ssm_A_im, ssm_log_dt, ssm_B_re, ssm_B_im, ssm_C_re, ssm_C_im, ssm_D, ssm_glu_w, ssm_glu_b, ssm_out_g, w_out, norm_ffn2, ffn2_w1, ffn2_w3, ffn2_w2, norm_final, loss_target, m_norm_ffn1, m_ffn1_w1, m_ffn1_w3, m_ffn1_w2, m_norm_mix, m_w_in, m_conv_w, m_conv_b, m_conv_ln_g, m_conv_ln_b, m_conv_out_g, m_ssm_A_re, m_ssm_A_im, m_ssm_log_dt, m_ssm_B_re, m_ssm_B_im, m_ssm_C_re, m_ssm_C_im, m_ssm_D, m_ssm_glu_w, m_ssm_glu_b, m_ssm_out_g, m_w_out, m_norm_ffn2, m_ffn2_w1, m_ffn2_w3, m_ffn2_w2, m_norm_final, v_norm_ffn1, v_ffn1_w1, v_ffn1_w3, v_ffn1_w2, v_norm_mix, v_w_in, v_conv_w, v_conv_b, v_conv_ln_g, v_conv_ln_b, v_conv_out_g, v_ssm_A_re, v_ssm_A_im, v_ssm_log_dt, v_ssm_B_re, v_ssm_B_im, v_ssm_C_re, v_ssm_C_im, v_ssm_D, v_ssm_glu_w, v_ssm_glu_b, v_ssm_out_g, v_w_out, v_norm_ffn2, v_ffn2_w1, v_ffn2_w3, v_ffn2_w2, v_norm_final):
    given = dict(locals())
    wts = {n: given[n] for n in WEIGHTS}
    n_seq, seq, d = x.shape
    n_rows = n_seq * seq
    xf = x.reshape(n_rows, d)
    tgt = loss_target.reshape(n_rows, d)
    row = lambda a: a.reshape(1, -1)

    f = ffn1_w1.shape[-1]
    fp = _round_up(f, LANE)
    held = lambda n, a: a[0].T if n in TRANSPOSED else a[0]
    shards = []
    for n in BIG:
        s = held(n, wts[n]).astype(BF16)
        if n.startswith('ffn'):
            s = jnp.pad(s, ((0, fp - f), (0, 0)))
        shards.append(s)
    n_taps, c_shard = conv_w.shape[1], conv_w.shape[2]
    shards.append(jnp.pad(conv_w[0], ((0, HALO - n_taps), (0, 0))))
    shard_of = dict(zip(BIG + ['conv_w'], shards))
    axis_of = dict(BIG_AXIS, conv_w=1)
    groups = [['ffn1_w1', 'ffn1_w3'], ['ffn1_w2', 'w_in', 'conv_w', 'ssm_glu_w', 'w_out'], ['ffn2_w1', 'ffn2_w3', 'ffn2_w2']]
    fetch, tok = [], []
    for k, names in enumerate(groups):
        fetch.append(_exchange_start("gather%d_send" % k, "gather", [shard_of[n] for n in names],
                                     [axis_of[n] for n in names], tok))
        tok = [fetch[-1][3]]
    full = {}

    def arrive(k, after):
        lands = _exchange_wait("gather%d_recv" % k, "gather", fetch[k], [axis_of[n] for n in groups[k]], after)
        full.update(zip(groups[k], lands))

    h1, h1_t = _rms_fwd("ffn1_rms", xf, norm_ffn1)
    arrive(0, tok + [h1])

    _, n_grp, n_state = ssm_A_re.shape
    grp = ssm_B_re.shape[-1]
    ns = n_grp * n_state
    c_ssm = n_grp * grp
    lr, li = ssm_A_re.reshape(1, ns), ssm_A_im.reshape(1, ns)
    ldt = jnp.repeat(ssm_log_dt.reshape(n_grp), n_state).reshape(1, ns)
    btr = ssm_B_re[0].transpose(2, 0, 1).reshape(grp, ns)
    bti = ssm_B_im[0].transpose(2, 0, 1).reshape(grp, ns)
    ctr = ssm_C_re[0].transpose(1, 0, 2).reshape(grp, ns)
    cti = ssm_C_im[0].transpose(1, 0, 2).reshape(grp, ns)
    scan_tile = _pick(seq, (SCAN_TILE,))
    _, _, bbr, bbi, seg_up, seg_down, pw, pw_falling = _s5_params_fwd(lr, li, ldt, btr, bti, scan_tile // SUBLANE)
    nb = c_ssm // LANE
    sb, gpb = ns // nb, n_grp // nb
    diag = (jnp.arange(LANE)[:, None] // grp) == (jnp.arange(sb)[None, :] // n_state)

    def spread(t):
        return jnp.where(diag, jnp.tile(t.reshape(grp, nb, sb).transpose(1, 0, 2), (1, gpb, 1)), 0.0)

    def gather_diag(t):
        return (t * diag).reshape(nb, gpb, grp, sb).sum(1).transpose(1, 0, 2).reshape(grp, ns)

    def interleave(re, im):
        return jnp.stack([re.reshape(-1, nb, sb), im.reshape(-1, nb, sb)], axis=2).reshape(-1, 2 * ns)

    bdc = jnp.concatenate([spread(bbr), spread(bbi)], axis=2).astype(BF16)
    cdc = jnp.concatenate([spread(ctr).transpose(0, 2, 1), -spread(cti).transpose(0, 2, 1)], axis=1).astype(BF16)
    rowi = jnp.arange(SUBLANE)[:, None]
    pwf, pwc = interleave(pw[:, :ns], pw[:, ns:]), interleave(pw[:, :ns], -pw[:, ns:])
    tabs_f = [jnp.where(rowi >= s, pwf[s - 1][None, :], 0.0) for s in (1, 2, 4)] + [pwf]
    tabs_b = [jnp.where(rowi <= SUBLANE - 1 - s, pwc[s - 1][None, :], 0.0) for s in (1, 2, 4)]
    tabs_b.append(interleave(pw_falling[:, :ns], -pw_falling[:, ns:]))
    fix_f = jnp.repeat(interleave(seg_up[:, :ns], seg_up[:, ns:]), SUBLANE, axis=0)
    fix_b = jnp.repeat(interleave(seg_down[:, :ns], -seg_down[:, ns:]), SUBLANE, axis=0)
    c_conv = conv_b.shape[1]
    u_blk = 2 * c_conv // LANE

    a1, b1, z1 = _ffn_up("ffn1_up", h1, full['ffn1_w1'], full['ffn1_w3'])
    arrive(1, [z1])
    x1, h2, h2_t = _mm("ffn1_down", z1, full['ffn1_w2'], 1, 0, addend=xf, alpha=0.5, post=_post_rms(norm_mix))
    saved1 = (h1_t, a1, b1, z1)
    cw = full['conv_w']
    proj = _mm("mix_in", h2, full['w_in'], 1, 0, F32)
    assert c_conv == c_ssm and proj.shape[1] == 3 * c_conv
    cpre, an = _conv_fwd(proj, cw, conv_b, conv_ln_g, conv_ln_b, conv_out_g, seq)
    xs, ypre, yg = _s5_fwd(proj, u_blk, bdc, cdc, fix_f, tabs_f, ssm_D, seq, sb)
    q0 = _mm("s5_gate", yg, full['ssm_glu_w'], 1, 0, F32)
    sn = _s5_post2(yg, q0, ssm_glu_b, ssm_out_g)
    wo = full['w_out']
    mixed = jnp.concatenate([an, sn], axis=1)
    x2, h3, h3_t = _mm("mix_out", mixed, wo, 1, 0, addend=x1, post=_post_rms(norm_ffn2))
    arrive(2, [x2])
    a3, b3, z3 = _ffn_up("ffn2_up", h3, full['ffn2_w1'], full['ffn2_w3'])
    saved2 = (h3_t, a3, b3, z3)
    dx3, dx3_t, loss_row, d_norm_final = _mm("ffn2_down", z3, full['ffn2_w2'], 1, 0, addend=x2, alpha=0.5,
                                             post=_post_loss(row(norm_final), tgt))

    g = {}
    dx2, g['norm_ffn2'], sent = _ffn_bwd("ffn2", x2, norm_ffn2, full['ffn2_w1'], full['ffn2_w3'], full['ffn2_w2'], saved2,
                                         dx3, dx3_t, early=False)
    dmixed = _mm("mix_dmixed", dx2, wo, 1, 1, F32)
    dwo = _mm("mix_dwo", mixed, dx2, 0, 0, BF16)
    dq, dyg1, g['ssm_out_g'], g['ssm_glu_b'] = _s5_post2_bwd(dmixed, yg, q0, ssm_glu_b, ssm_out_g)
    dyg2 = _mm("s5_dgate", dq, full['ssm_glu_w'], 1, 1, F32)
    dwg = _mm("s5_dwg", yg, dq, 0, 0, BF16)
    dypre, du_skip, g['ssm_D'] = _s5_post1_bwd(dyg1, dyg2, ypre, proj, ssm_D)
    du, dabar, dbdc, dcdc = _s5_bwd(dypre, du_skip, xs, proj, u_blk, bdc, cdc, fix_b, tabs_b, seq, sb)
    dabar = dabar.reshape(nb, 2, sb)
    dlr, dli, dldt, dbtr, dbti = _s5_params_bwd(lr, li, ldt, btr, bti, dabar[:, 0].reshape(1, ns), dabar[:, 1].reshape(1, ns),
                                                gather_diag(dbdc[:, :, :sb]), gather_diag(dbdc[:, :, sb:]))
    g['ssm_A_re'], g['ssm_A_im'] = dlr, dli
    g['ssm_log_dt'] = dldt.reshape(n_grp, n_state).sum(axis=1)
    g['ssm_B_re'] = dbtr.reshape(grp, n_grp, n_state).transpose(1, 2, 0)
    g['ssm_B_im'] = dbti.reshape(grp, n_grp, n_state).transpose(1, 2, 0)
    g['ssm_C_re'] = gather_diag(dcdc[:, :sb].transpose(0, 2, 1)).reshape(grp, n_grp, n_state).transpose(1, 0, 2)
    g['ssm_C_im'] = -gather_diag(dcdc[:, sb:].transpose(0, 2, 1)).reshape(grp, n_grp, n_state).transpose(1, 0, 2)
    dc, g['conv_out_g'], g['conv_ln_g'], g['conv_ln_b'], g['conv_b'] = _conv_bwd_rows(dmixed, cpre, conv_ln_g, conv_ln_b,
                                                                                    conv_out_g)
    dval, dgate, dcw = _conv_bwd_taps(proj, dc, cw, seq)
    dproj = jnp.concatenate([dval, dgate, du], axis=1)
    dwin = _mm("mix_dwin", h2_t, dproj, 1, 0, BF16)
    sent['w_out ssm_glu_w w_in'] = (_exchange_start("mix_send", "scatter", [dwo, dwg, dwin], [0, 0, 1]), [0, 0, 1])
    dx1, dx1_t, g['norm_mix'] = _mm("mix_dh", dproj, full['w_in'], 1, 1, after=[sent['w_out ssm_glu_w w_in'][0][3]],
                                    post=_post_rms_bwd(x1, norm_mix, dx2))
    dx0, g['norm_ffn1'], sent1 = _ffn_bwd("ffn1", xf, norm_ffn1, full['ffn1_w1'], full['ffn1_w3'], full['ffn1_w2'], saved1,
                                          dx1, dx1_t, early=True)
    sent.update(sent1)
    g['norm_final'] = d_norm_final
    g['conv_w'] = dcw[:n_taps]

    small_shapes = [(n_taps, c_conv) if n == 'conv_w' else wts[n].shape for n in SMALL]
    buf, buf_rows = _pack([g[n] for n in SMALL] + [loss_row])
    to_all = _exchange_start("small_send", "all", [buf], [0])
    slots = {}
    for names, (started, axes) in sent.items():
        lands = _exchange_wait(names.replace(' ', '_') + "_recv", "scatter", started, axes, after=[dx0, to_all[3]])
        slots.update(zip(names.split(), lands))
    sums = [_sum_slots("sum_" + n, slots[n]) for n in BIG]
    to_sibling = _exchange_start("sums_send", "sibling", sums, [0] * len(sums))
    from_all = _exchange_wait("small_recv", "all", to_all, [0], after=[to_sibling[3]])[0]
    total = _unpack(_sum_devices(from_all), buf_rows, small_shapes + [(1, LANE)])
    loss = total[-1][0, 0]
    grads = dict(zip(SMALL, total[:-1]))
    chip = 2 * lax.axis_index("x") + lax.axis_index("y")
    grads['conv_w'] = lax.dynamic_slice_in_dim(grads['conv_w'], chip * c_shard, c_shard, axis=1)[None]
    flat = lambda a: a.reshape(-1, a.shape[-1])
    small = _adamw_many("adamw_small", *[[flat(src[p + n]) for n in SMALL]
                                         for src, p in ((grads, ''), (given, ''), (given, 'm_'), (given, 'v_'))])
    deltas, new_m, new_v = ({n: o.reshape(wts[n].shape) for n, o in zip(SMALL, outs)} for outs in small)

    sums, theirs = _exchange_wait("sums_recv", "sibling", to_sibling, [0] * len(sums), after=[new_v[SMALL[-1]]],
                                  sources_too=True)
    for n, mine, other in zip(BIG, sums, theirs):
        grads[n], deltas[n], new_m[n], new_v[n] = (
            (o.T if n in TRANSPOSED else o)[None]
            for o in _adamw("adamw_" + n, [mine, other], held(n, given[n]), held(n, given['m_' + n]), held(n, given['v_' + n])))

    return (loss, dx0.reshape(x.shape), *[grads[n] for n in WEIGHTS], *[deltas[n] for n in WEIGHTS],
            *[new_m[n] for n in WEIGHTS], *[new_v[n] for n in WEIGHTS])
```
